```python
import math
import jax, jax.numpy as jnp
from jax import lax
import numpy as np

D_MODEL = 1024
BATCH = 1
SEQ = 16384
DEPTH = 1

CHUNK = 64
PLE_DIM = 256

D_CONV = D_MODEL
CONV_WIDTH = 31
D_POOL = D_MODEL
N_POOL_GROUPS = 4
POOL_GROUP = D_POOL // N_POOL_GROUPS
POOL_WINDOWS = (2, 4, 8, 16)
N_BRANCHES = 2
D_IN = 2 * D_CONV + D_POOL + N_BRANCHES * D_MODEL

N_EXPERTS = 64
TOP_K = 8
N_GROUPS = 8
TOPK_GROUPS = 4
D_EXPERT = 256
D_SHARED = 256
ROUTED_SCALE = 2.5
MOE_BLOCK = 128

NORM_EPS = 1e-6

kernel_name = 'hybrid_conv_pool_moe_block'


def rms_norm(x, g):
    x32 = x.astype(jnp.float32)
    y = x32 * lax.rsqrt(jnp.mean(x32 * x32, axis=-1, keepdims=True) + NORM_EPS)
    return (y * g.astype(jnp.float32)).astype(x.dtype)


def layer_norm(x, g, b):
    x32 = x.astype(jnp.float32)
    mu = jnp.mean(x32, axis=-1, keepdims=True)
    xc = x32 - mu
    var = jnp.mean(xc * xc, axis=-1, keepdims=True)
    y = xc * lax.rsqrt(var + NORM_EPS) * g.astype(jnp.float32) + b.astype(jnp.float32)
    return y.astype(x.dtype)


def causal_depthwise_conv(c, w, b):
    k = w.shape[0]
    y = lax.conv_general_dilated(
        c, w[:, None, :], window_strides=(1,), padding=[(k - 1, 0)],
        dimension_numbers=('NWC', 'WIO', 'NWC'), feature_group_count=c.shape[-1])
    return y + b


def multiscale_causal_pool(u):
    bsz, s, _ = u.shape
    cs = jnp.cumsum(u.astype(jnp.float32), axis=1)
    cs = jnp.pad(cs, ((0, 0), (1, 0), (0, 0)))
    t1 = jnp.arange(1, s + 1, dtype=jnp.float32)
    outs = []
    for gi, w in enumerate(POOL_WINDOWS):
        c = cs[:, :, gi * POOL_GROUP:(gi + 1) * POOL_GROUP]
        upper = c[:, 1:]
        lower = jnp.pad(c, ((0, 0), (w, 0), (0, 0)))[:, 1:s + 1]
        count = jnp.minimum(t1, float(w))[None, :, None]
        outs.append((upper - lower) / count)
    pooled = jnp.concatenate(outs, axis=-1)
    return (pooled - u.astype(jnp.float32)).astype(u.dtype)


def moe_ffn(h, w_router, b_router, w_e_gate, w_e_up, w_e_down, w_s_gate, w_s_up, w_s_down):
    bsz, s, d = h.shape
    t = bsz * s
    xt = h.reshape(t, d)
    scores = jax.nn.sigmoid((xt @ w_router).astype(jnp.float32))
    sel = scores + b_router.astype(jnp.float32)
    per_group = N_EXPERTS // N_GROUPS
    group_score = lax.top_k(sel.reshape(t, N_GROUPS, per_group), 2)[0].sum(-1)
    _, top_groups = lax.top_k(group_score, TOPK_GROUPS)
    group_mask = jax.nn.one_hot(top_groups, N_GROUPS).sum(axis=1) > 0
    masked = jnp.where(jnp.repeat(group_mask, per_group, axis=1), sel, -jnp.inf)
    _, top_idx = lax.top_k(masked, TOP_K)
    top_s = jnp.take_along_axis(scores, top_idx, axis=1)
    gates = top_s / jnp.sum(top_s, axis=-1, keepdims=True) * ROUTED_SCALE

    flat_e = top_idx.reshape(-1).astype(jnp.int32)
    flat_tok = jnp.repeat(jnp.arange(t, dtype=jnp.int32), TOP_K)
    flat_g = gates.reshape(-1)
    order = jnp.argsort(flat_e)
    se, stok, sg = flat_e[order], flat_tok[order], flat_g[order]
    counts = jnp.bincount(flat_e, length=N_EXPERTS).astype(jnp.int32)
    padded = ((counts + MOE_BLOCK - 1) // MOE_BLOCK) * MOE_BLOCK
    starts = jnp.cumsum(counts) - counts
    pends = jnp.cumsum(padded)
    pstarts = pends - padded
    rank = jnp.arange(t * TOP_K, dtype=jnp.int32) - starts[se]
    dest = pstarts[se] + rank
    total = t * TOP_K + N_EXPERTS * MOE_BLOCK
    n_blocks = total // MOE_BLOCK
    tok_buf = jnp.zeros((total,), jnp.int32).at[dest].set(stok)
    gate_buf = jnp.zeros((total,), jnp.float32).at[dest].set(sg)
    block_expert = jnp.minimum(
        jnp.searchsorted(pends, jnp.arange(n_blocks, dtype=jnp.int32) * MOE_BLOCK, side='right'),
        N_EXPERTS - 1).astype(jnp.int32)

    def expert_block(acc, blk):
        tok, g, e = blk
        xb = xt[tok]
        hb = jax.nn.silu(xb @ w_e_gate[e]) * (xb @ w_e_up[e])
        yb = (hb @ w_e_down[e]) * g[:, None].astype(xb.dtype)
        return acc.at[tok].add(yb), None

    routed, _ = lax.scan(
        expert_block, jnp.zeros((t, d), xt.dtype),
        (tok_buf.reshape(n_blocks, MOE_BLOCK), gate_buf.reshape(n_blocks, MOE_BLOCK), block_expert))
    shared = (jax.nn.silu(xt @ w_s_gate) * (xt @ w_s_up)) @ w_s_down
    return (routed + shared).reshape(bsz, s, d)


def setup_inputs(seed: int = 0) -> dict:
    key = jax.random.key(seed)
    ks = iter(jax.random.split(key, 40))
    L, D = DEPTH, D_MODEL

    def nrm(shape, scale):
        return jax.random.normal(next(ks), shape, jnp.float32) * scale

    def gain(shape):
        return 1.0 + nrm(shape, 0.05)

    return {
        'x': nrm((BATCH, SEQ, D), 1.0),
        'p': nrm((DEPTH, BATCH, SEQ, PLE_DIM), 1.0),
        'g_mix': gain((L, D)),
        'w_in': nrm((L, D, D_IN), D ** -0.5),
        'b_in': nrm((L, D_IN), 0.02),
        'w_dw': nrm((L, CONV_WIDTH, D_CONV), CONV_WIDTH ** -0.5),
        'b_dw': nrm((L, D_CONV), 0.02),
        'g_cln': gain((L, D_CONV)),
        'b_cln': nrm((L, D_CONV), 0.02),
        'w_conv_out': nrm((L, D_CONV, D), D_CONV ** -0.5),
        'b_conv_out': nrm((L, D), 0.02),
        'w_pool': nrm((L, N_POOL_GROUPS, POOL_GROUP, POOL_GROUP), POOL_GROUP ** -0.5),
        's_pool': 1.0 + nrm((L, D_POOL), 0.1),
        'w_out': nrm((L, D, D), D ** -0.5),
        'g_ffn': gain((L, D)),
        'w_router': nrm((L, D, N_EXPERTS), D ** -0.5),
        'b_router': nrm((L, N_EXPERTS), 0.01),
        'w_e_gate': nrm((L, N_EXPERTS, D, D_EXPERT), D ** -0.5),
        'w_e_up': nrm((L, N_EXPERTS, D, D_EXPERT), D ** -0.5),
        'w_e_down': nrm((L, N_EXPERTS, D_EXPERT, D), D_EXPERT ** -0.5),
        'w_s_gate': nrm((L, D, D_SHARED), D ** -0.5),
        'w_s_up': nrm((L, D, D_SHARED), D ** -0.5),
        'w_s_down': nrm((L, D_SHARED, D), D_SHARED ** -0.5),
        'g_ple': gain((L, D)),
        'w_ple_gate': nrm((L, D, D), D ** -0.5),
        'w_ple': nrm((L, PLE_DIM, D), PLE_DIM ** -0.5),
        'g_final': gain((D,)),
    }


def reference(x, p, g_mix, w_in, b_in, w_dw, b_dw, g_cln, b_cln, w_conv_out, b_conv_out,
              w_pool, s_pool, w_out, g_ffn, w_router, b_router, w_e_gate, w_e_up, w_e_down,
              w_s_gate, w_s_up, w_s_down, g_ple, w_ple_gate, w_ple, g_final):
    bsz, s, _ = x.shape
    c0 = D_CONV
    c1 = 2 * D_CONV
    c2 = 2 * D_CONV + D_POOL
    for i in range(DEPTH):
        h = rms_norm(x, g_mix[i])
        z = h @ w_in[i] + b_in[i]
        conv_val, conv_gate = z[..., :c0], z[..., c0:c1]
        pool_in = z[..., c1:c2]
        gate_logits = z[..., c2:].reshape(bsz, s, N_BRANCHES, D_MODEL)

        a = conv_val * jax.nn.sigmoid(conv_gate)
        a = causal_depthwise_conv(a, w_dw[i], b_dw[i])
        a = jax.nn.silu(layer_norm(a, g_cln[i], b_cln[i]))
        a = a @ w_conv_out[i] + b_conv_out[i]

        q = multiscale_causal_pool(pool_in).reshape(bsz, s, N_POOL_GROUPS, POOL_GROUP)
        q = jnp.einsum('bsgc,gcd->bsgd', q, w_pool[i]).reshape(bsz, s, D_POOL) * s_pool[i]

        gates = jax.nn.sigmoid(gate_logits.astype(jnp.float32)).astype(x.dtype)
        merged = gates[:, :, 0] * a + gates[:, :, 1] * q
        x = x + merged @ w_out[i]

        x = x + moe_ffn(rms_norm(x, g_ffn[i]), w_router[i], b_router[i], w_e_gate[i],
                        w_e_up[i], w_e_down[i], w_s_gate[i], w_s_up[i], w_s_down[i])

        ple_gate = jax.nn.sigmoid((rms_norm(x, g_ple[i]) @ w_ple_gate[i]).astype(jnp.float32))
        x = x + ple_gate.astype(x.dtype) * (p[i] @ w_ple[i])
    return rms_norm(x, g_final)
```

```python
import functools

import jax
import jax.numpy as jnp
from jax import lax
from jax.experimental import pallas as pl
from jax.experimental.pallas import tpu as pltpu

D_MODEL = 1024
D_CONV = 1024
D_POOL = 1024
CONV_WIDTH = 31
POOL_WINDOWS = (2, 4, 8, 16)
POOL_GROUP = 256
PLE_DIM = 256
N_EXPERTS = 64
N_GROUPS = 8
GROUP_SIZE = N_EXPERTS // N_GROUPS
TOPK_GROUPS = 4
TOP_K = 8
D_EXPERT = 256
ROUTED_SCALE = 2.5
NORM_EPS = 1e-6

F32 = jnp.float32
BF16 = jnp.bfloat16

MIX_TM = 256
CONV_HALO = 32
POOL_HALO = 16
ROW_CHUNK = 64
LANE = 128

ROUTER_TM = 1024
MOE_TM = 1024
MOE_EG = 4
PLE_TM = 512

VMEM_LIMIT = 56 * 1024 * 1024


def _rms(x, g):
    ms = jnp.mean(x * x, axis=-1, keepdims=True)
    return x * lax.rsqrt(ms + NORM_EPS) * g


def _dot(a, b):
    return jnp.dot(a, b, preferred_element_type=F32)


def _mixer_kernel(x_ref, gmix_ref, win_ref, bin_ref, wdw_ref, bdw_ref, gcln_ref, bcln_ref,
                  wco_ref, bco_ref, wpool_ref, spool_ref, wout_ref, gffn_ref,
                  x1_ref, h2_ref, a_buf, u_buf, c_buf, q_buf, *, tiles_per_seq):
    i = pl.program_id(0) % tiles_per_seq
    tm = MIX_TM

    @pl.when(i == 0)
    def _():
        a_buf[0:CONV_HALO, :] = jnp.zeros((CONV_HALO, D_CONV), F32)
        u_buf[0:POOL_HALO, :] = jnp.zeros((POOL_HALO, D_POOL), F32)

    x = x_ref[...]
    h = _rms(x, gmix_ref[...]).astype(BF16)

    def proj(lo, hi):
        return _dot(h, win_ref[:, lo:hi]) + bin_ref[:, lo:hi]

    a_buf[CONV_HALO:CONV_HALO + tm, :] = proj(0, D_CONV) * jax.nn.sigmoid(proj(D_CONV, 2 * D_CONV))
    u_buf[POOL_HALO:POOL_HALO + tm, :] = proj(2 * D_CONV, 2 * D_CONV + D_POOL)

    off0 = CONV_HALO - (CONV_WIDTH - 1)
    for rc in range(tm // ROW_CHUNK):
        r0 = rc * ROW_CHUNK
        for lc in range(D_CONV // LANE):
            ls = slice(lc * LANE, (lc + 1) * LANE)
            acc = None
            for r in range(8):
                qs = [q for q in range(5) if off0 <= 8 * q + r <= off0 + CONV_WIDTH - 1]
                rows = 8 * max(qs) + ROW_CHUNK
                s = a_buf[r0 + r:r0 + r + rows, ls]
                for q in qs:
                    k = 8 * q + r - off0
                    term = s[8 * q:8 * q + ROW_CHUNK] * wdw_ref[k:k + 1, ls]
                    acc = term if acc is None else acc + term
            c_buf[r0:r0 + ROW_CHUNK, ls] = acc + bdw_ref[:, ls]
    a_buf[0:CONV_HALO, :] = a_buf[tm:tm + CONV_HALO, :]

    c = c_buf[...]
    mu = jnp.mean(c, axis=-1, keepdims=True)
    xc = c - mu
    var = jnp.mean(xc * xc, axis=-1, keepdims=True)
    y = xc * lax.rsqrt(var + NORM_EPS) * gcln_ref[...] + bcln_ref[...]
    y = y * jax.nn.sigmoid(y)
    branch_a = _dot(y.astype(BF16), wco_ref[...]) + bco_ref[...]

    for rc in range(tm // ROW_CHUNK):
        r0 = rc * ROW_CHUNK
        t1 = (i * tm + r0 + 1) + lax.broadcasted_iota(jnp.int32, (ROW_CHUNK, POOL_GROUP), 0)
        for gi, w in enumerate(POOL_WINDOWS):
            ls = slice(gi * POOL_GROUP, (gi + 1) * POOL_GROUP)
            tok = u_buf[POOL_HALO + r0:POOL_HALO + r0 + ROW_CHUNK, ls]
            s = tok
            for j in range(1, w):
                s = s + u_buf[POOL_HALO + r0 - j:POOL_HALO + r0 - j + ROW_CHUNK, ls]
            cnt = jnp.minimum(t1, w).astype(F32)
            q_buf[r0:r0 + ROW_CHUNK, ls] = s / cnt - tok
    u_buf[0:POOL_HALO, :] = u_buf[tm:tm + POOL_HALO, :]

    qs_out = []
    for gi in range(len(POOL_WINDOWS)):
        ls = slice(gi * POOL_GROUP, (gi + 1) * POOL_GROUP)
        qs_out.append(_dot(q_buf[:, ls].astype(BF16), wpool_ref[gi]) * spool_ref[:, ls])
    branch_b = jnp.concatenate(qs_out, axis=-1)

    c2 = 2 * D_CONV + D_POOL
    gate_a = jax.nn.sigmoid(proj(c2, c2 + D_MODEL))
    gate_b = jax.nn.sigmoid(proj(c2 + D_MODEL, c2 + 2 * D_MODEL))
    merged = gate_a * branch_a + gate_b * branch_b
    x1 = x + _dot(merged.astype(BF16), wout_ref[...])
    x1_ref[...] = x1
    h2_ref[...] = _rms(x1, gffn_ref[...]).astype(BF16)


def _const_spec(shape):
    n = len(shape)
    return pl.BlockSpec(shape, lambda i, _n=n: (0,) * _n)


def _mixer(x, seq_len, g_mix, w_in, b_in, w_dw, b_dw, g_cln, b_cln, w_co, b_co, w_pool, s_pool, w_out,
           g_ffn):
    t = x.shape[0]
    tm = MIX_TM
    assert seq_len % tm == 0 and tm >= CONV_HALO
    d_in = w_in.shape[1]
    row = pl.BlockSpec((tm, D_MODEL), lambda i: (i, 0))
    return pl.pallas_call(
        functools.partial(_mixer_kernel, tiles_per_seq=seq_len // tm),
        grid=(t // tm,),
        in_specs=[
            row,
            _const_spec((1, D_MODEL)),
            _const_spec((D_MODEL, d_in)),
            _const_spec((1, d_in)),
            _const_spec((CONV_WIDTH, D_CONV)),
            _const_spec((1, D_CONV)),
            _const_spec((1, D_CONV)),
            _const_spec((1, D_CONV)),
            _const_spec((D_CONV, D_MODEL)),
            _const_spec((1, D_MODEL)),
            _const_spec((len(POOL_WINDOWS), POOL_GROUP, POOL_GROUP)),
            _const_spec((1, D_POOL)),
            _const_spec((D_MODEL, D_MODEL)),
            _const_spec((1, D_MODEL)),
        ],
        out_specs=[row, row],
        out_shape=[jax.ShapeDtypeStruct((t, D_MODEL), F32),
                   jax.ShapeDtypeStruct((t, D_MODEL), BF16)],
        scratch_shapes=[
            pltpu.VMEM((CONV_HALO + tm, D_CONV), F32),
            pltpu.VMEM((POOL_HALO + tm, D_POOL), F32),
            pltpu.VMEM((tm, D_CONV), F32),
            pltpu.VMEM((tm, D_POOL), F32),
        ],
        compiler_params=pltpu.CompilerParams(
            dimension_semantics=("arbitrary",), vmem_limit_bytes=VMEM_LIMIT),
        name="mixer",
    )(x, g_mix, w_in, b_in, w_dw, b_dw, g_cln, b_cln, w_co, b_co, w_pool, s_pool, w_out, g_ffn)


def _beats(v, other, other_is_later):
    v = jnp.broadcast_to(v, other.shape)
    return jnp.where(other_is_later, jnp.where(v >= other, 1, 0), jnp.where(v > other, 1, 0))


def _router_kernel(h2_ref, wrt_ref, br_ref, gt_ref):
    tm = ROUTER_TM
    logits = lax.dot_general(wrt_ref[...], h2_ref[...], (((1,), (1,)), ((), ())),
                             preferred_element_type=F32)
    scores = jax.nn.sigmoid(logits)
    sel = scores + br_ref[...]
    shape3 = (N_GROUPS, GROUP_SIZE, tm)
    sel3 = sel.reshape(shape3)
    scores3 = scores.reshape(shape3)
    neg_inf = jnp.float32(-jnp.inf)

    member = lax.broadcasted_iota(jnp.int32, shape3, 1)
    m1 = jnp.max(sel3, axis=1, keepdims=True)
    first = jnp.min(jnp.where(sel3 == m1, member, GROUP_SIZE), axis=1, keepdims=True)
    m2 = jnp.max(jnp.where(member == first, neg_inf, sel3), axis=1, keepdims=True)
    gscore = jnp.broadcast_to(m1 + m2, shape3)

    gidx = lax.broadcasted_iota(jnp.int32, shape3, 0)
    grank = jnp.zeros(shape3, jnp.int32)
    for j in range(N_GROUPS):
        sj = gscore[j:j + 1]
        grank = grank + _beats(sj, gscore, gidx > j)
    masked = jnp.where(grank < TOPK_GROUPS, sel3, neg_inf)

    eidx = lax.broadcasted_iota(jnp.int32, shape3, 0) * GROUP_SIZE + member
    erank = jnp.zeros(shape3, jnp.int32)
    for gj in range(N_GROUPS):
        for mj in range(GROUP_SIZE):
            v = masked[gj:gj + 1, mj:mj + 1, :]
            erank = erank + _beats(v, masked, eidx > gj * GROUP_SIZE + mj)
    top_s = jnp.where(erank < TOP_K, scores3, 0.0)
    denom = jnp.sum(jnp.sum(top_s, axis=1, keepdims=True), axis=0, keepdims=True)
    gates = top_s / denom * ROUTED_SCALE
    gt_ref[...] = gates.reshape(N_EXPERTS, tm)


def _router(h2, w_rt, b_r):
    t = h2.shape[0]
    tm = ROUTER_TM
    return pl.pallas_call(
        _router_kernel,
        grid=(t // tm,),
        in_specs=[
            pl.BlockSpec((tm, D_MODEL), lambda i: (i, 0)),
            _const_spec((N_EXPERTS, D_MODEL)),
            _const_spec((N_EXPERTS, 1)),
        ],
        out_specs=pl.BlockSpec((N_EXPERTS, tm), lambda i: (0, i)),
        out_shape=jax.ShapeDtypeStruct((N_EXPERTS, t), F32),
        compiler_params=pltpu.CompilerParams(
            dimension_semantics=("arbitrary",), vmem_limit_bytes=VMEM_LIMIT),
        name="router",
    )(h2, w_rt, b_r)


def _moe_kernel(x1_ref, h2_ref, g_ref, weg_ref, weu_ref, wed_ref, wsg_ref, wsu_ref, wsd_ref, x2_ref):
    j = pl.program_id(1)
    h2 = h2_ref[...]

    @pl.when(j == 0)
    def _():
        hs = _dot(h2, wsg_ref[...])
        hs = hs * jax.nn.sigmoid(hs) * _dot(h2, wsu_ref[...])
        x2_ref[...] = x1_ref[...] + _dot(hs.astype(BF16), wsd_ref[...])

    g = g_ref[0]
    parts = []
    for k in range(MOE_EG):
        hg = _dot(h2, weg_ref[k])
        hb = hg * jax.nn.sigmoid(hg) * _dot(h2, weu_ref[k]) * g[:, k:k + 1]
        parts.append(hb.astype(BF16))
    hcat = jnp.concatenate(parts, axis=-1)
    x2_ref[...] += _dot(hcat, wed_ref[...])


def _moe(x1, h2, g3, weg, weu, wed2, wsg, wsu, wsd):
    t = x1.shape[0]
    tm = MOE_TM
    n_steps = N_EXPERTS // MOE_EG
    const2 = lambda shape: pl.BlockSpec(shape, lambda i, j: (0, 0))
    return pl.pallas_call(
        _moe_kernel,
        grid=(t // tm, n_steps),
        in_specs=[
            pl.BlockSpec((tm, D_MODEL), lambda i, j: (i, 0)),
            pl.BlockSpec((tm, D_MODEL), lambda i, j: (i, 0)),
            pl.BlockSpec((1, tm, MOE_EG), lambda i, j: (j, i, 0)),
            pl.BlockSpec((MOE_EG, D_MODEL, D_EXPERT), lambda i, j: (j, 0, 0)),
            pl.BlockSpec((MOE_EG, D_MODEL, D_EXPERT), lambda i, j: (j, 0, 0)),
            pl.BlockSpec((MOE_EG * D_EXPERT, D_MODEL), lambda i, j: (j, 0)),
            const2((D_MODEL, D_EXPERT)),
            const2((D_MODEL, D_EXPERT)),
            const2((D_EXPERT, D_MODEL)),
        ],
        out_specs=pl.BlockSpec((tm, D_MODEL), lambda i, j: (i, 0)),
        out_shape=jax.ShapeDtypeStruct((t, D_MODEL), F32),
        compiler_params=pltpu.CompilerParams(
            dimension_semantics=("arbitrary", "arbitrary"), vmem_limit_bytes=VMEM_LIMIT),
        name="moe",
    )(x1, h2, g3, weg, weu, wed2, wsg, wsu, wsd)


def _ple_kernel(x2_ref, p_ref, gple_ref, wpg_ref, wp_ref, gfin_ref, o_ref, *, final_norm):
    x2 = x2_ref[...]
    hp = _rms(x2, gple_ref[...]).astype(BF16)
    gate = jax.nn.sigmoid(_dot(hp, wpg_ref[...]))
    emb = _dot(p_ref[...].astype(BF16), wp_ref[...])
    x3 = x2 + gate * emb
    o_ref[...] = _rms(x3, gfin_ref[...]) if final_norm else x3


def _ple(x2, p, g_ple, w_pg, w_p, g_fin, final_norm):
    t = x2.shape[0]
    tm = PLE_TM
    return pl.pallas_call(
        functools.partial(_ple_kernel, final_norm=final_norm),
        grid=(t // tm,),
        in_specs=[
            pl.BlockSpec((tm, D_MODEL), lambda i: (i, 0)),
            pl.BlockSpec((tm, PLE_DIM), lambda i: (i, 0)),
            _const_spec((1, D_MODEL)),
            _const_spec((D_MODEL, D_MODEL)),
            _const_spec((PLE_DIM, D_MODEL)),
            _const_spec((1, D_MODEL)),
        ],
        out_specs=pl.BlockSpec((tm, D_MODEL), lambda i: (i, 0)),
        out_shape=jax.ShapeDtypeStruct((t, D_MODEL), F32),
        compiler_params=pltpu.CompilerParams(
            dimension_semantics=("arbitrary",), vmem_limit_bytes=VMEM_LIMIT),
        name="ple",
    )(x2, p, g_ple, w_pg, w_p, g_fin)


def kernel(x, p, g_mix, w_in, b_in, w_dw, b_dw, g_cln, b_cln, w_conv_out, b_conv_out, w_pool, s_pool,
           w_out, g_ffn, w_router, b_router, w_e_gate, w_e_up, w_e_down, w_s_gate, w_s_up, w_s_down,
           g_ple, w_ple_gate, w_ple, g_final):
    bsz, s, d = x.shape
    t = bsz * s
    depth = w_in.shape[0]
    xt = x.reshape(t, d)
    row = lambda v: v.reshape(1, -1)
    for i in range(depth):
        x1, h2 = _mixer(
            xt, s, row(g_mix[i]), w_in[i].astype(BF16), row(b_in[i]), w_dw[i], row(b_dw[i]),
            row(g_cln[i]), row(b_cln[i]), w_conv_out[i].astype(BF16), row(b_conv_out[i]),
            w_pool[i].astype(BF16), row(s_pool[i]), w_out[i].astype(BF16), row(g_ffn[i]))
        gates_t = _router(h2, w_router[i].T.astype(BF16), b_router[i].reshape(N_EXPERTS, 1))
        g3 = gates_t.reshape(N_EXPERTS // MOE_EG, MOE_EG, t).transpose(0, 2, 1)
        x2 = _moe(
            x1, h2, g3, w_e_gate[i].astype(BF16), w_e_up[i].astype(BF16),
            w_e_down[i].astype(BF16).reshape(N_EXPERTS * D_EXPERT, D_MODEL),
            w_s_gate[i].astype(BF16), w_s_up[i].astype(BF16), w_s_down[i].astype(BF16))
        xt = _ple(x2, p[i].reshape(t, PLE_DIM), row(g_ple[i]), w_ple_gate[i].astype(BF16),
                  w_ple[i].astype(BF16), row(g_final), final_norm=(i == depth - 1))
    return xt.reshape(bsz, s, d)
```

```python
import functools

import jax
import jax.numpy as jnp
from jax import lax
from jax.experimental import pallas as pl
from jax.experimental.pallas import tpu as pltpu

D_MODEL = 1024
D_CONV = 1024
D_POOL = 1024
CONV_WIDTH = 31
POOL_WINDOWS = (2, 4, 8, 16)
POOL_GROUP = 256
PLE_DIM = 256
N_EXPERTS = 64
N_GROUPS = 8
GROUP_SIZE = N_EXPERTS // N_GROUPS
TOPK_GROUPS = 4
TOP_K = 8
D_EXPERT = 256
ROUTED_SCALE = 2.5
NORM_EPS = 1e-6

F32 = jnp.float32
BF16 = jnp.bfloat16

MIX_TM = 256
CONV_HALO = 32
POOL_HALO = 16
ROW_CHUNK = 64
LANE = 128

ROUTER_TM = 1024
WIN = 256
SEL_ROWS = 2560
SEL_RG = 64
SEL_MM = 512
EXP_BM = 256
CMB_RG = 32
CMB_LG = 512

VMEM_LIMIT = 56 * 1024 * 1024


def _rms(x, g):
    ms = jnp.mean(x * x, axis=-1, keepdims=True)
    return x * lax.rsqrt(ms + NORM_EPS) * g


def _dot(a, b):
    return jnp.dot(a, b, preferred_element_type=F32)


def _mixer_kernel(x_ref, gmix_ref, win_ref, bin_ref, wdw_ref, bdw_ref, gcln_ref, bcln_ref,
                  wco_ref, bco_ref, wpool_ref, spool_ref, wout_ref, gffn_ref,
                  x1_ref, h2_ref, a_buf, u_buf, c_buf, q_buf, *, tiles_per_seq):
    i = pl.program_id(0) % tiles_per_seq
    tm = MIX_TM

    @pl.when(i == 0)
    def _():
        a_buf[0:CONV_HALO, :] = jnp.zeros((CONV_HALO, D_CONV), F32)
        u_buf[0:POOL_HALO, :] = jnp.zeros((POOL_HALO, D_POOL), F32)

    x = x_ref[...]
    h = _rms(x, gmix_ref[...]).astype(BF16)

    def proj(lo, hi):
        return _dot(h, win_ref[:, lo:hi]) + bin_ref[:, lo:hi]

    a_buf[CONV_HALO:CONV_HALO + tm, :] = proj(0, D_CONV) * jax.nn.sigmoid(proj(D_CONV, 2 * D_CONV))
    u_buf[POOL_HALO:POOL_HALO + tm, :] = proj(2 * D_CONV, 2 * D_CONV + D_POOL)

    off0 = CONV_HALO - (CONV_WIDTH - 1)
    for rc in range(tm // ROW_CHUNK):
        r0 = rc * ROW_CHUNK
        for lc in range(D_CONV // LANE):
            ls = slice(lc * LANE, (lc + 1) * LANE)
            acc = None
            for r in range(8):
                qs = [q for q in range(5) if off0 <= 8 * q + r <= off0 + CONV_WIDTH - 1]
                rows = 8 * max(qs) + ROW_CHUNK
                s = a_buf[r0 + r:r0 + r + rows, ls]
                for q in qs:
                    k = 8 * q + r - off0
                    term = s[8 * q:8 * q + ROW_CHUNK] * wdw_ref[k:k + 1, ls]
                    acc = term if acc is None else acc + term
            c_buf[r0:r0 + ROW_CHUNK, ls] = acc + bdw_ref[:, ls]
    a_buf[0:CONV_HALO, :] = a_buf[tm:tm + CONV_HALO, :]

    c = c_buf[...]
    mu = jnp.mean(c, axis=-1, keepdims=True)
    xc = c - mu
    var = jnp.mean(xc * xc, axis=-1, keepdims=True)
    y = xc * lax.rsqrt(var + NORM_EPS) * gcln_ref[...] + bcln_ref[...]
    y = y * jax.nn.sigmoid(y)
    branch_a = _dot(y.astype(BF16), wco_ref[...]) + bco_ref[...]

    for rc in range(tm // ROW_CHUNK):
        r0 = rc * ROW_CHUNK
        t1 = (i * tm + r0 + 1) + lax.broadcasted_iota(jnp.int32, (ROW_CHUNK, POOL_GROUP), 0)
        for gi, w in enumerate(POOL_WINDOWS):
            ls = slice(gi * POOL_GROUP, (gi + 1) * POOL_GROUP)
            tok = u_buf[POOL_HALO + r0:POOL_HALO + r0 + ROW_CHUNK, ls]
            s = tok
            for j in range(1, w):
                s = s + u_buf[POOL_HALO + r0 - j:POOL_HALO + r0 - j + ROW_CHUNK, ls]
            cnt = jnp.minimum(t1, w).astype(F32)
            q_buf[r0:r0 + ROW_CHUNK, ls] = s / cnt - tok
    u_buf[0:POOL_HALO, :] = u_buf[tm:tm + POOL_HALO, :]

    qs_out = []
    for gi in range(len(POOL_WINDOWS)):
        ls = slice(gi * POOL_GROUP, (gi + 1) * POOL_GROUP)
        qs_out.append(_dot(q_buf[:, ls].astype(BF16), wpool_ref[gi]) * spool_ref[:, ls])
    branch_b = jnp.concatenate(qs_out, axis=-1)

    c2 = 2 * D_CONV + D_POOL
    gate_a = jax.nn.sigmoid(proj(c2, c2 + D_MODEL))
    gate_b = jax.nn.sigmoid(proj(c2 + D_MODEL, c2 + 2 * D_MODEL))
    merged = gate_a * branch_a + gate_b * branch_b
    x1 = x + _dot(merged.astype(BF16), wout_ref[...])
    x1_ref[...] = x1
    h2_ref[...] = _rms(x1, gffn_ref[...]).astype(BF16)


def _const_spec(shape):
    n = len(shape)
    return pl.BlockSpec(shape, lambda i, _n=n: (0,) * _n)


def _mixer(x, seq_len, g_mix, w_in, b_in, w_dw, b_dw, g_cln, b_cln, w_co, b_co, w_pool, s_pool, w_out,
           g_ffn):
    t = x.shape[0]
    tm = MIX_TM
    assert seq_len % tm == 0 and tm >= CONV_HALO
    d_in = w_in.shape[1]
    row = pl.BlockSpec((tm, D_MODEL), lambda i: (i, 0))
    return pl.pallas_call(
        functools.partial(_mixer_kernel, tiles_per_seq=seq_len // tm),
        grid=(t // tm,),
        in_specs=[
            row,
            _const_spec((1, D_MODEL)),
            _const_spec((D_MODEL, d_in)),
            _const_spec((1, d_in)),
            _const_spec((CONV_WIDTH, D_CONV)),
            _const_spec((1, D_CONV)),
            _const_spec((1, D_CONV)),
            _const_spec((1, D_CONV)),
            _const_spec((D_CONV, D_MODEL)),
            _const_spec((1, D_MODEL)),
            _const_spec((len(POOL_WINDOWS), POOL_GROUP, POOL_GROUP)),
            _const_spec((1, D_POOL)),
            _const_spec((D_MODEL, D_MODEL)),
            _const_spec((1, D_MODEL)),
        ],
        out_specs=[row, row],
        out_shape=[jax.ShapeDtypeStruct((t, D_MODEL), F32),
                   jax.ShapeDtypeStruct((t, D_MODEL), BF16)],
        scratch_shapes=[
            pltpu.VMEM((CONV_HALO + tm, D_CONV), F32),
            pltpu.VMEM((POOL_HALO + tm, D_POOL), F32),
            pltpu.VMEM((tm, D_CONV), F32),
            pltpu.VMEM((tm, D_POOL), F32),
        ],
        compiler_params=pltpu.CompilerParams(
            dimension_semantics=("arbitrary",), vmem_limit_bytes=VMEM_LIMIT),
        name="mixer",
    )(x, g_mix, w_in, b_in, w_dw, b_dw, g_cln, b_cln, w_co, b_co, w_pool, s_pool, w_out, g_ffn)


def _beats(v, other, other_is_later):
    v = jnp.broadcast_to(v, other.shape)
    return jnp.where(other_is_later, jnp.where(v >= other, 1, 0), jnp.where(v > other, 1, 0))


def _router_kernel(h2_ref, wrt_ref, br_ref, utri_ref, ltri_ref, gate_ref, pos_ref, cnt_ref):
    tm = ROUTER_TM
    logits = lax.dot_general(wrt_ref[...], h2_ref[...], (((1,), (1,)), ((), ())),
                             preferred_element_type=F32)
    scores = jax.nn.sigmoid(logits)
    sel = scores + br_ref[...]
    shape3 = (N_GROUPS, GROUP_SIZE, tm)
    sel3 = sel.reshape(shape3)
    scores3 = scores.reshape(shape3)
    neg_inf = jnp.float32(-jnp.inf)

    member = lax.broadcasted_iota(jnp.int32, shape3, 1)
    m1 = jnp.max(sel3, axis=1, keepdims=True)
    first = jnp.min(jnp.where(sel3 == m1, member, GROUP_SIZE), axis=1, keepdims=True)
    m2 = jnp.max(jnp.where(member == first, neg_inf, sel3), axis=1, keepdims=True)
    gscore = jnp.broadcast_to(m1 + m2, shape3)

    gidx = lax.broadcasted_iota(jnp.int32, shape3, 0)
    grank = jnp.zeros(shape3, jnp.int32)
    for j in range(N_GROUPS):
        sj = gscore[j:j + 1]
        grank = grank + _beats(sj, gscore, gidx > j)
    masked = jnp.where(grank < TOPK_GROUPS, sel3, neg_inf)

    eidx = gidx * GROUP_SIZE + member
    erank = jnp.zeros(shape3, jnp.int32)
    for gj in range(N_GROUPS):
        for mj in range(GROUP_SIZE):
            v = masked[gj:gj + 1, mj:mj + 1, :]
            erank = erank + _beats(v, masked, eidx > gj * GROUP_SIZE + mj)
    chosen = erank < TOP_K
    top_s = jnp.where(chosen, scores3, 0.0)
    denom = jnp.sum(jnp.sum(top_s, axis=1, keepdims=True), axis=0, keepdims=True)
    gates3 = top_s / denom * ROUTED_SCALE
    chosen2 = jnp.where(chosen, 1.0, 0.0).reshape(N_EXPERTS, tm)

    for w in range(tm // WIN):
        ls = slice(w * WIN, (w + 1) * WIN)
        mw = chosen2[:, ls]
        rank = _dot(mw.astype(BF16), utri_ref[...])
        n = jnp.sum(mw, axis=1, keepdims=True)
        run = jnp.floor((n + 7.0) * 0.125) * 8.0
        start = _dot(ltri_ref[...], jnp.broadcast_to(run, (N_EXPERTS, WIN)).astype(BF16))
        row3 = (rank + start).reshape(N_GROUPS, GROUP_SIZE, WIN)
        er = erank[:, :, ls]
        g3 = gates3[:, :, ls]
        for k in range(TOP_K):
            hit = er == k
            pk = jnp.sum(jnp.sum(jnp.where(hit, row3, 0.0), axis=1, keepdims=True), axis=0, keepdims=True)
            gk = jnp.sum(jnp.sum(jnp.where(hit, g3, 0.0), axis=1, keepdims=True), axis=0, keepdims=True)
            pos_ref[k:k + 1, ls] = pk.reshape(1, WIN).astype(jnp.int32)
            gate_ref[k:k + 1, ls] = gk.reshape(1, WIN)
        cnt_ref[w] = n


def _router(h2, w_rt, b_r):
    t = h2.shape[0]
    tm = ROUTER_TM
    utri = jnp.triu(jnp.ones((WIN, WIN), BF16), k=1)
    ltri = jnp.tril(jnp.ones((N_EXPERTS, N_EXPERTS), BF16), k=-1)
    return pl.pallas_call(
        _router_kernel,
        grid=(t // tm,),
        in_specs=[
            pl.BlockSpec((tm, D_MODEL), lambda i: (i, 0)),
            _const_spec((N_EXPERTS, D_MODEL)),
            _const_spec((N_EXPERTS, 1)),
            _const_spec((WIN, WIN)),
            _const_spec((N_EXPERTS, N_EXPERTS)),
        ],
        out_specs=[
            pl.BlockSpec((TOP_K, tm), lambda i: (0, i)),
            pl.BlockSpec((TOP_K, tm), lambda i: (0, i)),
            pl.BlockSpec((tm // WIN, N_EXPERTS, 1), lambda i: (i, 0, 0)),
        ],
        out_shape=[
            jax.ShapeDtypeStruct((TOP_K, t), F32),
            jax.ShapeDtypeStruct((TOP_K, t), jnp.int32),
            jax.ShapeDtypeStruct((t // WIN, N_EXPERTS, 1), F32),
        ],
        compiler_params=pltpu.CompilerParams(
            dimension_semantics=("arbitrary",), vmem_limit_bytes=VMEM_LIMIT),
        name="router",
    )(h2, w_rt, b_r, utri, ltri)


def _sorted_rows_bound(t):
    rows = t * TOP_K + (t // WIN) * N_EXPERTS * 7 + N_EXPERTS * (EXP_BM - 1)
    return -(-rows // EXP_BM) * EXP_BM


def _dispatch_plan(cnt, t):
    nw = t // WIN
    n = cnt.reshape(nw, N_EXPERTS).astype(jnp.int32)
    run = (n + 7) // 8 * 8
    local_end = jnp.cumsum(run, axis=1)
    local_off = jnp.concatenate([jnp.zeros((nw, 1), jnp.int32), local_end], axis=1)
    total = jnp.sum(run, axis=0)
    region = (total + EXP_BM - 1) // EXP_BM * EXP_BM
    region_end = jnp.cumsum(region)
    base = region_end - region
    global_off = base[None, :] + jnp.cumsum(run, axis=0) - run
    n_blocks = _sorted_rows_bound(t) // EXP_BM
    n_used = region_end[-1] // EXP_BM
    blk = jnp.arange(n_blocks, dtype=jnp.int32)
    blk_expert = jnp.minimum(jnp.searchsorted(region_end, blk * EXP_BM, side='right'),
                             N_EXPERTS - 1).astype(jnp.int32)
    last_expert = blk_expert[jnp.maximum(n_used - 1, 0)]
    blk_expert = jnp.where(blk < n_used, blk_expert, last_expert)
    return dict(
        local_off=local_off.reshape(-1), global_off=global_off.reshape(-1),
        fill_off=base + total, fill_cnt=region - total,
        blk_expert=blk_expert, n_used=n_used.reshape(1).astype(jnp.int32))


def _run_copy(local_ref, global_ref, win, e, vmem_buf, slot, hbm_buf, sem, to_hbm):
    lo = pl.multiple_of(local_ref[win * (N_EXPERTS + 1) + e], 8)
    cnt = pl.multiple_of(local_ref[win * (N_EXPERTS + 1) + e + 1] - lo, 8)
    go = pl.multiple_of(global_ref[win * N_EXPERTS + e], 8)
    v = vmem_buf.at[pl.ds(pl.multiple_of(slot * SEL_ROWS + lo, 8), cnt)]
    h = hbm_buf.at[pl.ds(go, cnt)]
    cp = pltpu.make_async_copy(v, h, sem.at[slot]) if to_hbm else pltpu.make_async_copy(h, v, sem.at[slot])
    return cnt, cp


def _for_each_run(local_ref, global_ref, win, vmem_buf, slot, hbm_buf, sem, to_hbm, wait):
    def body(e, carry):
        cnt, cp = _run_copy(local_ref, global_ref, win, e, vmem_buf, slot, hbm_buf, sem, to_hbm)

        @pl.when(cnt > 0)
        def _():
            if wait:
                cp.wait()
            else:
                cp.start()
        return carry
    lax.fori_loop(0, N_EXPERTS, body, 0)


def _dispatch_kernel(local_ref, global_ref, fill_off_ref, fill_cnt_ref, h2_ref, pos_ref, xs_hbm, sbuf,
                     s_ref, sem, zsem, *, n_win):
    w = pl.program_id(0)
    slot = w % 2
    pos = pos_ref[...]

    def build(rg, carry):
        r0 = pl.multiple_of(rg * SEL_RG, SEL_RG)
        rid = r0 + lax.broadcasted_iota(jnp.int32, (SEL_RG, WIN), 0)
        acc = jnp.zeros((SEL_RG, WIN), F32)
        for k in range(TOP_K):
            acc = jnp.where(rid == pos[k:k + 1, :], 1.0, acc)
        s_ref[pl.ds(r0, SEL_RG), :] = acc.astype(BF16)
        return carry
    lax.fori_loop(0, SEL_ROWS // SEL_RG, build, 0)

    h2 = h2_ref[...]
    for g in range(SEL_ROWS // SEL_MM):
        rows = slice(g * SEL_MM, (g + 1) * SEL_MM)
        dst = pl.multiple_of(slot * SEL_ROWS + g * SEL_MM, SEL_MM)
        sbuf[pl.ds(dst, SEL_MM), :] = _dot(s_ref[rows, :], h2).astype(BF16)

    _for_each_run(local_ref, global_ref, w, sbuf, slot, xs_hbm, sem, True, False)

    @pl.when(w > 0)
    def _():
        _for_each_run(local_ref, global_ref, w - 1, sbuf, 1 - slot, xs_hbm, sem, True, True)

    @pl.when(w == n_win - 1)
    def _():
        sbuf[2 * SEL_ROWS:, :] = jnp.zeros((EXP_BM, D_MODEL), BF16)

        def fill(e, wait):
            cnt = pl.multiple_of(fill_cnt_ref[e], 8)
            off = pl.multiple_of(fill_off_ref[e], 8)
            cp = pltpu.make_async_copy(sbuf.at[pl.ds(2 * SEL_ROWS, cnt)], xs_hbm.at[pl.ds(off, cnt)], zsem)

            @pl.when(cnt > 0)
            def _():
                if wait:
                    cp.wait()
                else:
                    cp.start()

        def start_body(e, carry):
            fill(e, False)
            return carry

        def wait_body(e, carry):
            fill(e, True)
            return carry
        lax.fori_loop(0, N_EXPERTS, start_body, 0)
        _for_each_run(local_ref, global_ref, w, sbuf, slot, xs_hbm, sem, True, True)
        lax.fori_loop(0, N_EXPERTS, wait_body, 0)


def _staging_shape(extra_rows):
    return jax.ShapeDtypeStruct((2 * SEL_ROWS + extra_rows, D_MODEL), BF16)


def _staging_spec(extra_rows):
    return pl.BlockSpec((2 * SEL_ROWS + extra_rows, D_MODEL), lambda w, *_: (0, 0))


def _dispatch(plan, h2, pos):
    t = h2.shape[0]
    n_win = t // WIN
    return pl.pallas_call(
        functools.partial(_dispatch_kernel, n_win=n_win),
        grid_spec=pltpu.PrefetchScalarGridSpec(
            num_scalar_prefetch=4,
            grid=(n_win,),
            in_specs=[
                pl.BlockSpec((WIN, D_MODEL), lambda w, *_: (w, 0)),
                pl.BlockSpec((TOP_K, WIN), lambda w, *_: (0, w)),
            ],
            out_specs=[pl.BlockSpec(memory_space=pl.ANY), _staging_spec(EXP_BM)],
            scratch_shapes=[
                pltpu.VMEM((SEL_ROWS, WIN), BF16),
                pltpu.SemaphoreType.DMA((2,)),
                pltpu.SemaphoreType.DMA,
            ]),
        out_shape=[jax.ShapeDtypeStruct((_sorted_rows_bound(t), D_MODEL), BF16), _staging_shape(EXP_BM)],
        compiler_params=pltpu.CompilerParams(
            dimension_semantics=("arbitrary",), vmem_limit_bytes=VMEM_LIMIT),
        name="dispatch",
    )(plan['local_off'], plan['global_off'], plan['fill_off'], plan['fill_cnt'], h2, pos)[0]


def _expert_kernel(blk_expert_ref, n_used_ref, xs_ref, wg_ref, wu_ref, wd_ref, y_ref, wg_bf, wu_bf, wd_bf):
    b = pl.program_id(0)

    @pl.when(b < n_used_ref[0])
    def _():
        new_expert = jnp.logical_or(b == 0, blk_expert_ref[b] != blk_expert_ref[jnp.maximum(b - 1, 0)])

        @pl.when(new_expert)
        def _():
            wg_bf[...] = wg_ref[0].astype(BF16)
            wu_bf[...] = wu_ref[0].astype(BF16)
            wd_bf[...] = wd_ref[0].astype(BF16)

        x = xs_ref[...]
        hg = _dot(x, wg_bf[...])
        hb = hg * jax.nn.sigmoid(hg) * _dot(x, wu_bf[...])
        y_ref[...] = _dot(hb.astype(BF16), wd_bf[...]).astype(BF16)


def _experts(plan, xs, w_gate, w_up, w_down):
    n_blocks = xs.shape[0] // EXP_BM
    row_map = lambda b, be, nu: (jnp.minimum(b, nu[0] - 1), 0)
    w_map = lambda b, be, nu: (be[b], 0, 0)
    return pl.pallas_call(
        _expert_kernel,
        grid_spec=pltpu.PrefetchScalarGridSpec(
            num_scalar_prefetch=2,
            grid=(n_blocks,),
            in_specs=[
                pl.BlockSpec((EXP_BM, D_MODEL), row_map),
                pl.BlockSpec((1, D_MODEL, D_EXPERT), w_map),
                pl.BlockSpec((1, D_MODEL, D_EXPERT), w_map),
                pl.BlockSpec((1, D_EXPERT, D_MODEL), w_map),
            ],
            out_specs=pl.BlockSpec((EXP_BM, D_MODEL), row_map),
            scratch_shapes=[
                pltpu.VMEM((D_MODEL, D_EXPERT), BF16),
                pltpu.VMEM((D_MODEL, D_EXPERT), BF16),
                pltpu.VMEM((D_EXPERT, D_MODEL), BF16),
            ]),
        out_shape=jax.ShapeDtypeStruct(xs.shape, BF16),
        compiler_params=pltpu.CompilerParams(
            dimension_semantics=("arbitrary",), vmem_limit_bytes=VMEM_LIMIT),
        name="experts",
    )(plan['blk_expert'], plan['n_used'], xs, w_gate, w_up, w_down)


def _combine_kernel(local_ref, global_ref, x1_ref, h2_ref, p_ref, pos_ref, gate_ref, wsg_ref, wsu_ref,
                    wsd_ref, gple_ref, wpg_ref, wp_ref, gfin_ref, ys_hbm, o_ref, ybuf, st_ref, sem,
                    *, n_win, final_norm):
    w = pl.program_id(0)
    slot = w % 2

    @pl.when(w == 0)
    def _():
        ybuf[...] = jnp.zeros(ybuf.shape, BF16)
        _for_each_run(local_ref, global_ref, w, ybuf, slot, ys_hbm, sem, False, False)

    @pl.when(w + 1 < n_win)
    def _():
        _for_each_run(local_ref, global_ref, w + 1, ybuf, 1 - slot, ys_hbm, sem, False, False)

    def build(rg, carry):
        r0 = pl.multiple_of(rg * CMB_RG, CMB_RG)
        pos_g = pos_ref[pl.ds(r0, CMB_RG), :]
        gate_g = gate_ref[pl.ds(r0, CMB_RG), :]
        for lg in range(SEL_ROWS // CMB_LG):
            cid = lg * CMB_LG + lax.broadcasted_iota(jnp.int32, (CMB_RG, CMB_LG), 1)
            acc = jnp.zeros((CMB_RG, CMB_LG), F32)
            for k in range(TOP_K):
                acc = jnp.where(cid == pos_g[:, k:k + 1], gate_g[:, k:k + 1], acc)
            st_ref[pl.ds(r0, CMB_RG), lg * CMB_LG:(lg + 1) * CMB_LG] = acc.astype(BF16)
        return carry
    lax.fori_loop(0, WIN // CMB_RG, build, 0)

    h2 = h2_ref[...]
    hs = _dot(h2, wsg_ref[...])
    hs = hs * jax.nn.sigmoid(hs) * _dot(h2, wsu_ref[...])
    shared = _dot(hs.astype(BF16), wsd_ref[...])

    _for_each_run(local_ref, global_ref, w, ybuf, slot, ys_hbm, sem, False, True)
    routed = _dot(st_ref[...], ybuf[pl.ds(pl.multiple_of(slot * SEL_ROWS, SEL_ROWS), SEL_ROWS), :])
    x2 = x1_ref[...] + routed + shared

    hp = _rms(x2, gple_ref[...]).astype(BF16)
    gate = jax.nn.sigmoid(_dot(hp, wpg_ref[...]))
    x3 = x2 + gate * _dot(p_ref[...].astype(BF16), wp_ref[...])
    o_ref[...] = _rms(x3, gfin_ref[...]) if final_norm else x3


def _combine(plan, ys, x1, h2, p, pos_tm, gate_tm, wsg, wsu, wsd, g_ple, w_pg, w_p, g_fin, final_norm):
    t = x1.shape[0]
    n_win = t // WIN
    row = lambda width: pl.BlockSpec((WIN, width), lambda w, *_: (w, 0))
    const = lambda shape: pl.BlockSpec(shape, lambda w, *_: (0,) * len(shape))
    return pl.pallas_call(
        functools.partial(_combine_kernel, n_win=n_win, final_norm=final_norm),
        grid_spec=pltpu.PrefetchScalarGridSpec(
            num_scalar_prefetch=2,
            grid=(n_win,),
            in_specs=[
                row(D_MODEL), row(D_MODEL), row(PLE_DIM), row(TOP_K), row(TOP_K),
                const((D_MODEL, D_EXPERT)), const((D_MODEL, D_EXPERT)), const((D_EXPERT, D_MODEL)),
                const((1, D_MODEL)), const((D_MODEL, D_MODEL)), const((PLE_DIM, D_MODEL)),
                const((1, D_MODEL)),
                pl.BlockSpec(memory_space=pl.ANY),
            ],
            out_specs=[row(D_MODEL), _staging_spec(0)],
            scratch_shapes=[
                pltpu.VMEM((WIN, SEL_ROWS), BF16),
                pltpu.SemaphoreType.DMA((2,)),
            ]),
        out_shape=[jax.ShapeDtypeStruct((t, D_MODEL), F32), _staging_shape(0)],
        compiler_params=pltpu.CompilerParams(
            dimension_semantics=("arbitrary",), vmem_limit_bytes=VMEM_LIMIT),
        name="combine",
    )(plan['local_off'], plan['global_off'], x1, h2, p, pos_tm, gate_tm, wsg, wsu, wsd, g_ple, w_pg, w_p,
      g_fin, ys)[0]


def kernel(x, p, g_mix, w_in, b_in, w_dw, b_dw, g_cln, b_cln, w_conv_out, b_conv_out, w_pool, s_pool,
           w_out, g_ffn, w_router, b_router, w_e_gate, w_e_up, w_e_down, w_s_gate, w_s_up, w_s_down,
           g_ple, w_ple_gate, w_ple, g_final):
    bsz, s, d = x.shape
    t = bsz * s
    depth = w_in.shape[0]
    xt = x.reshape(t, d)
    row = lambda v: v.reshape(1, -1)
    for i in range(depth):
        x1, h2 = _mixer(
            xt, s, row(g_mix[i]), w_in[i].astype(BF16), row(b_in[i]), w_dw[i], row(b_dw[i]),
            row(g_cln[i]), row(b_cln[i]), w_conv_out[i].astype(BF16), row(b_conv_out[i]),
            w_pool[i].astype(BF16), row(s_pool[i]), w_out[i].astype(BF16), row(g_ffn[i]))
        gate, pos, cnt = _router(h2, w_router[i].T.astype(BF16), b_router[i].reshape(N_EXPERTS, 1))
        plan = _dispatch_plan(cnt, t)
        xs = _dispatch(plan, h2, pos)
        ys = _experts(plan, xs, w_e_gate[i], w_e_up[i], w_e_down[i])
        xt = _combine(
            plan, ys, x1, h2, p[i].reshape(t, PLE_DIM), pos.T, gate.T,
            w_s_gate[i].astype(BF16), w_s_up[i].astype(BF16), w_s_down[i].astype(BF16),
            row(g_ple[i]), w_ple_gate[i].astype(BF16), w_ple[i].astype(BF16), row(g_final),
            final_norm=(i == depth - 1))
    return xt.reshape(bsz, s, d)
```

```python
import functools

import jax
import jax.numpy as jnp
from jax import lax
from jax.experimental import pallas as pl
from jax.experimental.pallas import tpu as pltpu

D_MODEL = 1024
D_CONV = 1024
D_POOL = 1024
CONV_WIDTH = 31
POOL_WINDOWS = (2, 4, 8, 16)
POOL_GROUP = 256
PLE_DIM = 256
N_EXPERTS = 64
N_GROUPS = 8
GROUP_SIZE = N_EXPERTS // N_GROUPS
TOPK_GROUPS = 4
TOP_K = 8
D_EXPERT = 256
ROUTED_SCALE = 2.5
NORM_EPS = 1e-6

F32 = jnp.float32
BF16 = jnp.bfloat16

MIX_TM = 256
CONV_HALO = 32
POOL_HALO = 16
ROW_CHUNK = 64
LANE = 128

ROUTER_TM = 1024
WIN = 256
SEL_ROWS = 2560
SEL_RG = 64
SEL_MM = 512
EXP_BM = 512
CMB_RG = 32
CMB_LG = 512

VMEM_LIMIT = 56 * 1024 * 1024


def _rms(x, g):
    ms = jnp.mean(x * x, axis=-1, keepdims=True)
    return x * lax.rsqrt(ms + NORM_EPS) * g


def _dot(a, b):
    return jnp.dot(a, b, preferred_element_type=F32)


def _mixer_kernel(x_ref, gmix_ref, win_ref, bin_ref, wdw_ref, bdw_ref, gcln_ref, bcln_ref,
                  wco_ref, bco_ref, wpool_ref, spool_ref, wout_ref, gffn_ref,
                  x1_ref, h2_ref, a_buf, u_buf, c_buf, q_buf, *, tiles_per_seq):
    i = pl.program_id(0) % tiles_per_seq
    tm = MIX_TM

    @pl.when(i == 0)
    def _():
        a_buf[0:CONV_HALO, :] = jnp.zeros((CONV_HALO, D_CONV), F32)
        u_buf[0:POOL_HALO, :] = jnp.zeros((POOL_HALO, D_POOL), F32)

    x = x_ref[...]
    h = _rms(x, gmix_ref[...]).astype(BF16)

    def proj(lo, hi):
        return _dot(h, win_ref[:, lo:hi]) + bin_ref[:, lo:hi]

    a_buf[CONV_HALO:CONV_HALO + tm, :] = proj(0, D_CONV) * jax.nn.sigmoid(proj(D_CONV, 2 * D_CONV))
    u_buf[POOL_HALO:POOL_HALO + tm, :] = proj(2 * D_CONV, 2 * D_CONV + D_POOL)

    off0 = CONV_HALO - (CONV_WIDTH - 1)
    for rc in range(tm // ROW_CHUNK):
        r0 = rc * ROW_CHUNK
        for lc in range(D_CONV // LANE):
            ls = slice(lc * LANE, (lc + 1) * LANE)
            acc = None
            for r in range(8):
                qs = [q for q in range(5) if off0 <= 8 * q + r <= off0 + CONV_WIDTH - 1]
                rows = 8 * max(qs) + ROW_CHUNK
                s = a_buf[r0 + r:r0 + r + rows, ls]
                for q in qs:
                    k = 8 * q + r - off0
                    term = s[8 * q:8 * q + ROW_CHUNK] * wdw_ref[k:k + 1, ls]
                    acc = term if acc is None else acc + term
            c_buf[r0:r0 + ROW_CHUNK, ls] = acc + bdw_ref[:, ls]
    a_buf[0:CONV_HALO, :] = a_buf[tm:tm + CONV_HALO, :]

    c = c_buf[...]
    mu = jnp.mean(c, axis=-1, keepdims=True)
    xc = c - mu
    var = jnp.mean(xc * xc, axis=-1, keepdims=True)
    y = xc * lax.rsqrt(var + NORM_EPS) * gcln_ref[...] + bcln_ref[...]
    y = y * jax.nn.sigmoid(y)
    branch_a = _dot(y.astype(BF16), wco_ref[...]) + bco_ref[...]

    for rc in range(tm // ROW_CHUNK):
        r0 = rc * ROW_CHUNK
        t1 = (i * tm + r0 + 1) + lax.broadcasted_iota(jnp.int32, (ROW_CHUNK, POOL_GROUP), 0)
        for gi, w in enumerate(POOL_WINDOWS):
            ls = slice(gi * POOL_GROUP, (gi + 1) * POOL_GROUP)
            tok = u_buf[POOL_HALO + r0:POOL_HALO + r0 + ROW_CHUNK, ls]
            s = tok
            for j in range(1, w):
                s = s + u_buf[POOL_HALO + r0 - j:POOL_HALO + r0 - j + ROW_CHUNK, ls]
            cnt = jnp.minimum(t1, w).astype(F32)
            q_buf[r0:r0 + ROW_CHUNK, ls] = s / cnt - tok
    u_buf[0:POOL_HALO, :] = u_buf[tm:tm + POOL_HALO, :]

    qs_out = []
    for gi in range(len(POOL_WINDOWS)):
        ls = slice(gi * POOL_GROUP, (gi + 1) * POOL_GROUP)
        qs_out.append(_dot(q_buf[:, ls].astype(BF16), wpool_ref[gi]) * spool_ref[:, ls])
    branch_b = jnp.concatenate(qs_out, axis=-1)

    c2 = 2 * D_CONV + D_POOL
    gate_a = jax.nn.sigmoid(proj(c2, c2 + D_MODEL))
    gate_b = jax.nn.sigmoid(proj(c2 + D_MODEL, c2 + 2 * D_MODEL))
    merged = gate_a * branch_a + gate_b * branch_b
    x1 = x + _dot(merged.astype(BF16), wout_ref[...])
    x1_ref[...] = x1
    h2_ref[...] = _rms(x1, gffn_ref[...]).astype(BF16)


def _const_spec(shape):
    n = len(shape)
    return pl.BlockSpec(shape, lambda i, _n=n: (0,) * _n)


def _mixer(x, seq_len, g_mix, w_in, b_in, w_dw, b_dw, g_cln, b_cln, w_co, b_co, w_pool, s_pool, w_out,
           g_ffn):
    t = x.shape[0]
    tm = MIX_TM
    assert seq_len % tm == 0 and tm >= CONV_HALO
    d_in = w_in.shape[1]
    row = pl.BlockSpec((tm, D_MODEL), lambda i: (i, 0))
    return pl.pallas_call(
        functools.partial(_mixer_kernel, tiles_per_seq=seq_len // tm),
        grid=(t // tm,),
        in_specs=[
            row,
            _const_spec((1, D_MODEL)),
            _const_spec((D_MODEL, d_in)),
            _const_spec((1, d_in)),
            _const_spec((CONV_WIDTH, D_CONV)),
            _const_spec((1, D_CONV)),
            _const_spec((1, D_CONV)),
            _const_spec((1, D_CONV)),
            _const_spec((D_CONV, D_MODEL)),
            _const_spec((1, D_MODEL)),
            _const_spec((len(POOL_WINDOWS), POOL_GROUP, POOL_GROUP)),
            _const_spec((1, D_POOL)),
            _const_spec((D_MODEL, D_MODEL)),
            _const_spec((1, D_MODEL)),
        ],
        out_specs=[row, row],
        out_shape=[jax.ShapeDtypeStruct((t, D_MODEL), F32),
                   jax.ShapeDtypeStruct((t, D_MODEL), BF16)],
        scratch_shapes=[
            pltpu.VMEM((CONV_HALO + tm, D_CONV), F32),
            pltpu.VMEM((POOL_HALO + tm, D_POOL), F32),
            pltpu.VMEM((tm, D_CONV), F32),
            pltpu.VMEM((tm, D_POOL), F32),
        ],
        compiler_params=pltpu.CompilerParams(
            dimension_semantics=("arbitrary",), vmem_limit_bytes=VMEM_LIMIT),
        name="mixer",
    )(x, g_mix, w_in, b_in, w_dw, b_dw, g_cln, b_cln, w_co, b_co, w_pool, s_pool, w_out, g_ffn)


def _beats(v, other, other_is_later):
    v = jnp.broadcast_to(v, other.shape)
    return jnp.where(other_is_later, jnp.where(v >= other, 1, 0), jnp.where(v > other, 1, 0))


def _router_kernel(h2_ref, wrt_ref, br_ref, utri_ref, ltri_ref, gate_ref, pos_ref, cnt_ref):
    tm = ROUTER_TM
    logits = lax.dot_general(wrt_ref[...], h2_ref[...], (((1,), (1,)), ((), ())),
                             preferred_element_type=F32)
    scores = jax.nn.sigmoid(logits)
    sel = scores + br_ref[...]
    shape3 = (N_GROUPS, GROUP_SIZE, tm)
    sel3 = sel.reshape(shape3)
    scores3 = scores.reshape(shape3)
    neg_inf = jnp.float32(-jnp.inf)

    member = lax.broadcasted_iota(jnp.int32, shape3, 1)
    m1 = jnp.max(sel3, axis=1, keepdims=True)
    first = jnp.min(jnp.where(sel3 == m1, member, GROUP_SIZE), axis=1, keepdims=True)
    m2 = jnp.max(jnp.where(member == first, neg_inf, sel3), axis=1, keepdims=True)
    gscore = jnp.broadcast_to(m1 + m2, shape3)

    gidx = lax.broadcasted_iota(jnp.int32, shape3, 0)
    grank = jnp.zeros(shape3, jnp.int32)
    for j in range(N_GROUPS):
        sj = gscore[j:j + 1]
        grank = grank + _beats(sj, gscore, gidx > j)
    masked = jnp.where(grank < TOPK_GROUPS, sel3, neg_inf)

    eidx = gidx * GROUP_SIZE + member
    erank = jnp.zeros(shape3, jnp.int32)
    for gj in range(N_GROUPS):
        for mj in range(GROUP_SIZE):
            v = masked[gj:gj + 1, mj:mj + 1, :]
            erank = erank + _beats(v, masked, eidx > gj * GROUP_SIZE + mj)
    chosen = erank < TOP_K
    top_s = jnp.where(chosen, scores3, 0.0)
    denom = jnp.sum(jnp.sum(top_s, axis=1, keepdims=True), axis=0, keepdims=True)
    gates3 = top_s / denom * ROUTED_SCALE
    chosen2 = jnp.where(chosen, 1.0, 0.0).reshape(N_EXPERTS, tm)

    for w in range(tm // WIN):
        ls = slice(w * WIN, (w + 1) * WIN)
        mw = chosen2[:, ls]
        rank = _dot(mw.astype(BF16), utri_ref[...])
        n = jnp.sum(mw, axis=1, keepdims=True)
        run = jnp.floor((n + 7.0) * 0.125) * 8.0
        start = _dot(ltri_ref[...], jnp.broadcast_to(run, (N_EXPERTS, WIN)).astype(BF16))
        row3 = (rank + start).reshape(N_GROUPS, GROUP_SIZE, WIN)
        er = erank[:, :, ls]
        g3 = gates3[:, :, ls]
        for k in range(TOP_K):
            hit = er == k
            pk = jnp.sum(jnp.sum(jnp.where(hit, row3, 0.0), axis=1, keepdims=True), axis=0, keepdims=True)
            gk = jnp.sum(jnp.sum(jnp.where(hit, g3, 0.0), axis=1, keepdims=True), axis=0, keepdims=True)
            pos_ref[k:k + 1, ls] = pk.reshape(1, WIN).astype(jnp.int32)
            gate_ref[k:k + 1, ls] = gk.reshape(1, WIN)
        cnt_ref[w] = n


def _router(h2, w_rt, b_r):
    t = h2.shape[0]
    tm = ROUTER_TM
    utri = jnp.triu(jnp.ones((WIN, WIN), BF16), k=1)
    ltri = jnp.tril(jnp.ones((N_EXPERTS, N_EXPERTS), BF16), k=-1)
    return pl.pallas_call(
        _router_kernel,
        grid=(t // tm,),
        in_specs=[
            pl.BlockSpec((tm, D_MODEL), lambda i: (i, 0)),
            _const_spec((N_EXPERTS, D_MODEL)),
            _const_spec((N_EXPERTS, 1)),
            _const_spec((WIN, WIN)),
            _const_spec((N_EXPERTS, N_EXPERTS)),
        ],
        out_specs=[
            pl.BlockSpec((TOP_K, tm), lambda i: (0, i)),
            pl.BlockSpec((TOP_K, tm), lambda i: (0, i)),
            pl.BlockSpec((tm // WIN, N_EXPERTS, 1), lambda i: (i, 0, 0)),
        ],
        out_shape=[
            jax.ShapeDtypeStruct((TOP_K, t), F32),
            jax.ShapeDtypeStruct((TOP_K, t), jnp.int32),
            jax.ShapeDtypeStruct((t // WIN, N_EXPERTS, 1), F32),
        ],
        compiler_params=pltpu.CompilerParams(
            dimension_semantics=("arbitrary",), vmem_limit_bytes=VMEM_LIMIT),
        name="router",
    )(h2, w_rt, b_r, utri, ltri)


def _sorted_rows_bound(t):
    rows = t * TOP_K + (t // WIN) * N_EXPERTS * 7 + N_EXPERTS * (EXP_BM - 1)
    return -(-rows // EXP_BM) * EXP_BM


def _dispatch_plan(cnt, t):
    nw = t // WIN
    n = cnt.reshape(nw, N_EXPERTS).astype(jnp.int32)
    run = (n + 7) // 8 * 8
    local_end = jnp.cumsum(run, axis=1)
    local_off = jnp.concatenate([jnp.zeros((nw, 1), jnp.int32), local_end], axis=1)
    total = jnp.sum(run, axis=0)
    region = (total + EXP_BM - 1) // EXP_BM * EXP_BM
    region_end = jnp.cumsum(region)
    base = region_end - region
    global_off = base[None, :] + jnp.cumsum(run, axis=0) - run
    n_blocks = _sorted_rows_bound(t) // EXP_BM
    n_used = region_end[-1] // EXP_BM
    blk = jnp.arange(n_blocks, dtype=jnp.int32)
    blk_expert = jnp.sum((region_end[None, :] <= blk[:, None] * EXP_BM).astype(jnp.int32), axis=1)
    blk_expert = jnp.minimum(blk_expert, N_EXPERTS - 1)
    last_expert = blk_expert[jnp.maximum(n_used - 1, 0)]
    blk_expert = jnp.where(blk < n_used, blk_expert, last_expert)
    return dict(
        local_off=local_off.reshape(-1), global_off=global_off.reshape(-1),
        fill_off=base + total, fill_cnt=region - total,
        blk_expert=blk_expert, n_used=n_used.reshape(1).astype(jnp.int32))


def _run_copy(local_ref, global_ref, win, e, vmem_buf, slot, hbm_buf, sem, to_hbm):
    lo = pl.multiple_of(local_ref[win * (N_EXPERTS + 1) + e], 8)
    cnt = pl.multiple_of(local_ref[win * (N_EXPERTS + 1) + e + 1] - lo, 8)
    go = pl.multiple_of(global_ref[win * N_EXPERTS + e], 8)
    v = vmem_buf.at[pl.ds(pl.multiple_of(slot * SEL_ROWS + lo, 8), cnt)]
    h = hbm_buf.at[pl.ds(go, cnt)]
    cp = pltpu.make_async_copy(v, h, sem.at[slot]) if to_hbm else pltpu.make_async_copy(h, v, sem.at[slot])
    return cnt, cp


def _start_runs(local_ref, global_ref, win, vmem_buf, slot, hbm_buf, sem, to_hbm):
    def body(e, carry):
        cnt, cp = _run_copy(local_ref, global_ref, win, e, vmem_buf, slot, hbm_buf, sem, to_hbm)

        @pl.when(cnt > 0)
        def _():
            cp.start()
        return carry
    lax.fori_loop(0, N_EXPERTS, body, 0)


def _wait_runs(local_ref, win, vmem_buf, slot, hbm_buf, sem, to_hbm):
    total = pl.multiple_of(local_ref[win * (N_EXPERTS + 1) + N_EXPERTS], 8)
    v = vmem_buf.at[pl.ds(pl.multiple_of(slot * SEL_ROWS, 8), total)]
    h = hbm_buf.at[pl.ds(0, total)]
    cp = pltpu.make_async_copy(v, h, sem.at[slot]) if to_hbm else pltpu.make_async_copy(h, v, sem.at[slot])

    @pl.when(total > 0)
    def _():
        cp.wait()


def _dispatch_kernel(local_ref, global_ref, fill_off_ref, fill_cnt_ref, h2_ref, pos_ref, xs_hbm, sbuf,
                     s_ref, sem, zsem, *, n_win):
    w = pl.program_id(0)
    slot = w % 2
    pos = pos_ref[...]

    h2 = h2_ref[...]
    for g in range(SEL_ROWS // SEL_MM):
        for sg in range(SEL_MM // SEL_RG):
            r0 = g * SEL_MM + sg * SEL_RG
            rid = r0 + lax.broadcasted_iota(jnp.int32, (SEL_RG, WIN), 0)
            acc = jnp.zeros((SEL_RG, WIN), F32)
            for k in range(TOP_K):
                acc = jnp.where(rid == pos[k:k + 1, :], 1.0, acc)
            s_ref[r0:r0 + SEL_RG, :] = acc.astype(BF16)
        rows = slice(g * SEL_MM, (g + 1) * SEL_MM)
        dst = pl.multiple_of(slot * SEL_ROWS + g * SEL_MM, SEL_MM)
        sbuf[pl.ds(dst, SEL_MM), :] = _dot(s_ref[rows, :], h2).astype(BF16)

    _start_runs(local_ref, global_ref, w, sbuf, slot, xs_hbm, sem, True)

    @pl.when(w > 0)
    def _():
        _wait_runs(local_ref, w - 1, sbuf, 1 - slot, xs_hbm, sem, True)

    @pl.when(w == n_win - 1)
    def _():
        sbuf[2 * SEL_ROWS:, :] = jnp.zeros((EXP_BM, D_MODEL), BF16)

        def fill(e, wait):
            cnt = pl.multiple_of(fill_cnt_ref[e], 8)
            off = pl.multiple_of(fill_off_ref[e], 8)
            cp = pltpu.make_async_copy(sbuf.at[pl.ds(2 * SEL_ROWS, cnt)], xs_hbm.at[pl.ds(off, cnt)], zsem)

            @pl.when(cnt > 0)
            def _():
                if wait:
                    cp.wait()
                else:
                    cp.start()

        def start_body(e, carry):
            fill(e, False)
            return carry

        def wait_body(e, carry):
            fill(e, True)
            return carry
        lax.fori_loop(0, N_EXPERTS, start_body, 0)
        _wait_runs(local_ref, w, sbuf, slot, xs_hbm, sem, True)
        lax.fori_loop(0, N_EXPERTS, wait_body, 0)


def _staging_shape(extra_rows):
    return jax.ShapeDtypeStruct((2 * SEL_ROWS + extra_rows, D_MODEL), BF16)


def _staging_spec(extra_rows):
    return pl.BlockSpec((2 * SEL_ROWS + extra_rows, D_MODEL), lambda w, *_: (0, 0))


def _dispatch(plan, h2, pos):
    t = h2.shape[0]
    n_win = t // WIN
    return pl.pallas_call(
        functools.partial(_dispatch_kernel, n_win=n_win),
        grid_spec=pltpu.PrefetchScalarGridSpec(
            num_scalar_prefetch=4,
            grid=(n_win,),
            in_specs=[
                pl.BlockSpec((WIN, D_MODEL), lambda w, *_: (w, 0)),
                pl.BlockSpec((TOP_K, WIN), lambda w, *_: (0, w)),
            ],
            out_specs=[pl.BlockSpec(memory_space=pl.ANY), _staging_spec(EXP_BM)],
            scratch_shapes=[
                pltpu.VMEM((SEL_ROWS, WIN), BF16),
                pltpu.SemaphoreType.DMA((2,)),
                pltpu.SemaphoreType.DMA,
            ]),
        out_shape=[jax.ShapeDtypeStruct((_sorted_rows_bound(t), D_MODEL), BF16), _staging_shape(EXP_BM)],
        compiler_params=pltpu.CompilerParams(
            dimension_semantics=("arbitrary",), vmem_limit_bytes=VMEM_LIMIT),
        name="dispatch",
    )(plan['local_off'], plan['global_off'], plan['fill_off'], plan['fill_cnt'], h2, pos)[0]


def _expert_kernel(blk_expert_ref, n_used_ref, xs_ref, wg_ref, wu_ref, wd_ref, y_ref, wg_bf, wu_bf, wd_bf):
    b = pl.program_id(0)

    @pl.when(b < n_used_ref[0])
    def _():
        new_expert = jnp.logical_or(b == 0, blk_expert_ref[b] != blk_expert_ref[jnp.maximum(b - 1, 0)])

        @pl.when(new_expert)
        def _():
            wg_bf[...] = wg_ref[0].astype(BF16)
            wu_bf[...] = wu_ref[0].astype(BF16)
            wd_bf[...] = wd_ref[0].astype(BF16)

        x = xs_ref[...]
        hg = _dot(x, wg_bf[...])
        hb = hg * jax.nn.sigmoid(hg) * _dot(x, wu_bf[...])
        y_ref[...] = _dot(hb.astype(BF16), wd_bf[...]).astype(BF16)


def _experts(plan, xs, w_gate, w_up, w_down):
    n_blocks = xs.shape[0] // EXP_BM
    row_map = lambda b, be, nu: (jnp.minimum(b, nu[0] - 1), 0)
    w_map = lambda b, be, nu: (be[b], 0, 0)
    return pl.pallas_call(
        _expert_kernel,
        grid_spec=pltpu.PrefetchScalarGridSpec(
            num_scalar_prefetch=2,
            grid=(n_blocks,),
            in_specs=[
                pl.BlockSpec((EXP_BM, D_MODEL), row_map),
                pl.BlockSpec((1, D_MODEL, D_EXPERT), w_map),
                pl.BlockSpec((1, D_MODEL, D_EXPERT), w_map),
                pl.BlockSpec((1, D_EXPERT, D_MODEL), w_map),
            ],
            out_specs=pl.BlockSpec((EXP_BM, D_MODEL), row_map),
            scratch_shapes=[
                pltpu.VMEM((D_MODEL, D_EXPERT), BF16),
                pltpu.VMEM((D_MODEL, D_EXPERT), BF16),
                pltpu.VMEM((D_EXPERT, D_MODEL), BF16),
            ]),
        out_shape=jax.ShapeDtypeStruct(xs.shape, BF16),
        compiler_params=pltpu.CompilerParams(
            dimension_semantics=("arbitrary",), vmem_limit_bytes=VMEM_LIMIT),
        name="experts",
    )(plan['blk_expert'], plan['n_used'], xs, w_gate, w_up, w_down)


def _combine_kernel(local_ref, global_ref, x1_ref, h2_ref, p_ref, pos_ref, gate_ref, wsg_ref, wsu_ref,
                    wsd_ref, gple_ref, wpg_ref, wp_ref, gfin_ref, ys_hbm, o_ref, ybuf, st_ref, sem,
                    *, n_win, final_norm):
    w = pl.program_id(0)
    slot = w % 2

    @pl.when(w == 0)
    def _():
        ybuf[...] = jnp.zeros(ybuf.shape, BF16)
        _start_runs(local_ref, global_ref, w, ybuf, slot, ys_hbm, sem, False)

    @pl.when(w + 1 < n_win)
    def _():
        _start_runs(local_ref, global_ref, w + 1, ybuf, 1 - slot, ys_hbm, sem, False)

    def build_group(lg):
        cols = slice(lg * CMB_LG, (lg + 1) * CMB_LG)
        for rg in range(WIN // CMB_RG):
            rows = slice(rg * CMB_RG, (rg + 1) * CMB_RG)
            cid = lg * CMB_LG + lax.broadcasted_iota(jnp.int32, (CMB_RG, CMB_LG), 1)
            acc = jnp.zeros((CMB_RG, CMB_LG), F32)
            for k in range(TOP_K):
                acc = jnp.where(cid == pos_ref[rows, k:k + 1], gate_ref[rows, k:k + 1], acc)
            st_ref[rows, cols] = acc.astype(BF16)

    build_group(0)
    h2 = h2_ref[...]
    hs = _dot(h2, wsg_ref[...])
    hs = hs * jax.nn.sigmoid(hs) * _dot(h2, wsu_ref[...])
    shared = _dot(hs.astype(BF16), wsd_ref[...])

    _wait_runs(local_ref, w, ybuf, slot, ys_hbm, sem, False)
    routed = None
    n_groups = SEL_ROWS // CMB_LG
    for lg in range(n_groups):
        if lg + 1 < n_groups:
            build_group(lg + 1)
        src = pl.multiple_of(slot * SEL_ROWS + lg * CMB_LG, CMB_LG)
        part = _dot(st_ref[:, lg * CMB_LG:(lg + 1) * CMB_LG], ybuf[pl.ds(src, CMB_LG), :])
        routed = part if routed is None else routed + part
    x2 = x1_ref[...] + routed + shared

    hp = _rms(x2, gple_ref[...]).astype(BF16)
    gate = jax.nn.sigmoid(_dot(hp, wpg_ref[...]))
    x3 = x2 + gate * _dot(p_ref[...].astype(BF16), wp_ref[...])
    o_ref[...] = _rms(x3, gfin_ref[...]) if final_norm else x3


def _combine(plan, ys, x1, h2, p, pos_tm, gate_tm, wsg, wsu, wsd, g_ple, w_pg, w_p, g_fin, final_norm):
    t = x1.shape[0]
    n_win = t // WIN
    row = lambda width: pl.BlockSpec((WIN, width), lambda w, *_: (w, 0))
    const = lambda shape: pl.BlockSpec(shape, lambda w, *_: (0,) * len(shape))
    return pl.pallas_call(
        functools.partial(_combine_kernel, n_win=n_win, final_norm=final_norm),
        grid_spec=pltpu.PrefetchScalarGridSpec(
            num_scalar_prefetch=2,
            grid=(n_win,),
            in_specs=[
                row(D_MODEL), row(D_MODEL), row(PLE_DIM), row(TOP_K), row(TOP_K),
                const((D_MODEL, D_EXPERT)), const((D_MODEL, D_EXPERT)), const((D_EXPERT, D_MODEL)),
                const((1, D_MODEL)), const((D_MODEL, D_MODEL)), const((PLE_DIM, D_MODEL)),
                const((1, D_MODEL)),
                pl.BlockSpec(memory_space=pl.ANY),
            ],
            out_specs=[row(D_MODEL), _staging_spec(0)],
            scratch_shapes=[
                pltpu.VMEM((WIN, SEL_ROWS), BF16),
                pltpu.SemaphoreType.DMA((2,)),
            ]),
        out_shape=[jax.ShapeDtypeStruct((t, D_MODEL), F32), _staging_shape(0)],
        compiler_params=pltpu.CompilerParams(
            dimension_semantics=("arbitrary",), vmem_limit_bytes=VMEM_LIMIT),
        name="combine",
    )(plan['local_off'], plan['global_off'], x1, h2, p, pos_tm, gate_tm, wsg, wsu, wsd, g_ple, w_pg, w_p,
      g_fin, ys)[0]


def kernel(x, p, g_mix, w_in, b_in, w_dw, b_dw, g_cln, b_cln, w_conv_out, b_conv_out, w_pool, s_pool,
           w_out, g_ffn, w_router, b_router, w_e_gate, w_e_up, w_e_down, w_s_gate, w_s_up, w_s_down,
           g_ple, w_ple_gate, w_ple, g_final):
    bsz, s, d = x.shape
    t = bsz * s
    depth = w_in.shape[0]
    xt = x.reshape(t, d)
    row = lambda v: v.reshape(1, -1)
    for i in range(depth):
        x1, h2 = _mixer(
            xt, s, row(g_mix[i]), w_in[i].astype(BF16), row(b_in[i]), w_dw[i], row(b_dw[i]),
            row(g_cln[i]), row(b_cln[i]), w_conv_out[i].astype(BF16), row(b_conv_out[i]),
            w_pool[i].astype(BF16), row(s_pool[i]), w_out[i].astype(BF16), row(g_ffn[i]))
        gate, pos, cnt = _router(h2, w_router[i].T.astype(BF16), b_router[i].reshape(N_EXPERTS, 1))
        plan = _dispatch_plan(cnt, t)
        xs = _dispatch(plan, h2, pos)
        ys = _experts(plan, xs, w_e_gate[i], w_e_up[i], w_e_down[i])
        xt = _combine(
            plan, ys, x1, h2, p[i].reshape(t, PLE_DIM), pos.T, gate.T,
            w_s_gate[i].astype(BF16), w_s_up[i].astype(BF16), w_s_down[i].astype(BF16),
            row(g_ple[i]), w_ple_gate[i].astype(BF16), w_ple[i].astype(BF16), row(g_final),
            final_norm=(i == depth - 1))
    return xt.reshape(bsz, s, d)
```

```python
import functools

import jax
import jax.numpy as jnp
from jax import lax
from jax.experimental import pallas as pl
from jax.experimental.pallas import tpu as pltpu

D_MODEL = 1024
D_CONV = 1024
D_POOL = 1024
CONV_WIDTH = 31
POOL_WINDOWS = (2, 4, 8, 16)
POOL_GROUP = 256
PLE_DIM = 256
N_EXPERTS = 64
N_GROUPS = 8
GROUP_SIZE = N_EXPERTS // N_GROUPS
TOPK_GROUPS = 4
TOP_K = 8
D_EXPERT = 256
ROUTED_SCALE = 2.5
NORM_EPS = 1e-6

F32 = jnp.float32
BF16 = jnp.bfloat16

MIX_TM = 256
CONV_HALO = 32
POOL_HALO = 16
ROW_CHUNK = 64
LANE = 128

ROUTER_TM = 1024
WIN = 256
SEL_ROWS = 2560
SEL_RG = 64
SEL_MM = 512
EXP_BM = 512
EXP_XDEPTH = 3
EXP_SPLIT = 4
CMB_RG = 32
CMB_LG = 512

VMEM_LIMIT = 56 * 1024 * 1024


def _rms(x, g):
    ms = jnp.mean(x * x, axis=-1, keepdims=True)
    return x * lax.rsqrt(ms + NORM_EPS) * g


def _dot(a, b):
    return jnp.dot(a, b, preferred_element_type=F32)


def _mixer_kernel(x_ref, gmix_ref, win_ref, bin_ref, wdw_ref, bdw_ref, gcln_ref, bcln_ref,
                  wco_ref, bco_ref, wpool_ref, spool_ref, wout_ref, gffn_ref,
                  x1_ref, h2_ref, a_buf, u_buf, c_buf, q_buf, *, tiles_per_seq):
    i = pl.program_id(0) % tiles_per_seq
    tm = MIX_TM

    @pl.when(i == 0)
    def _():
        a_buf[0:CONV_HALO, :] = jnp.zeros((CONV_HALO, D_CONV), F32)
        u_buf[0:POOL_HALO, :] = jnp.zeros((POOL_HALO, D_POOL), F32)

    x = x_ref[...]
    h = _rms(x, gmix_ref[...]).astype(BF16)

    def proj(lo, hi):
        return _dot(h, win_ref[:, lo:hi]) + bin_ref[:, lo:hi]

    a_buf[CONV_HALO:CONV_HALO + tm, :] = proj(0, D_CONV) * jax.nn.sigmoid(proj(D_CONV, 2 * D_CONV))
    u_buf[POOL_HALO:POOL_HALO + tm, :] = proj(2 * D_CONV, 2 * D_CONV + D_POOL)

    off0 = CONV_HALO - (CONV_WIDTH - 1)
    for rc in range(tm // ROW_CHUNK):
        r0 = rc * ROW_CHUNK
        for lc in range(D_CONV // LANE):
            ls = slice(lc * LANE, (lc + 1) * LANE)
            acc = None
            for r in range(8):
                qs = [q for q in range(5) if off0 <= 8 * q + r <= off0 + CONV_WIDTH - 1]
                rows = 8 * max(qs) + ROW_CHUNK
                s = a_buf[r0 + r:r0 + r + rows, ls]
                for q in qs:
                    k = 8 * q + r - off0
                    term = s[8 * q:8 * q + ROW_CHUNK] * wdw_ref[k:k + 1, ls]
                    acc = term if acc is None else acc + term
            c_buf[r0:r0 + ROW_CHUNK, ls] = acc + bdw_ref[:, ls]
    a_buf[0:CONV_HALO, :] = a_buf[tm:tm + CONV_HALO, :]

    c = c_buf[...]
    mu = jnp.mean(c, axis=-1, keepdims=True)
    xc = c - mu
    var = jnp.mean(xc * xc, axis=-1, keepdims=True)
    y = xc * lax.rsqrt(var + NORM_EPS) * gcln_ref[...] + bcln_ref[...]
    y = y * jax.nn.sigmoid(y)
    branch_a = _dot(y.astype(BF16), wco_ref[...]) + bco_ref[...]

    for rc in range(tm // ROW_CHUNK):
        r0 = rc * ROW_CHUNK
        t1 = (i * tm + r0 + 1) + lax.broadcasted_iota(jnp.int32, (ROW_CHUNK, POOL_GROUP), 0)
        for gi, w in enumerate(POOL_WINDOWS):
            ls = slice(gi * POOL_GROUP, (gi + 1) * POOL_GROUP)
            tok = u_buf[POOL_HALO + r0:POOL_HALO + r0 + ROW_CHUNK, ls]
            s = tok
            for j in range(1, w):
                s = s + u_buf[POOL_HALO + r0 - j:POOL_HALO + r0 - j + ROW_CHUNK, ls]
            cnt = jnp.minimum(t1, w).astype(F32)
            q_buf[r0:r0 + ROW_CHUNK, ls] = s / cnt - tok
    u_buf[0:POOL_HALO, :] = u_buf[tm:tm + POOL_HALO, :]

    qs_out = []
    for gi in range(len(POOL_WINDOWS)):
        ls = slice(gi * POOL_GROUP, (gi + 1) * POOL_GROUP)
        qs_out.append(_dot(q_buf[:, ls].astype(BF16), wpool_ref[gi]) * spool_ref[:, ls])
    branch_b = jnp.concatenate(qs_out, axis=-1)

    c2 = 2 * D_CONV + D_POOL
    gate_a = jax.nn.sigmoid(proj(c2, c2 + D_MODEL))
    gate_b = jax.nn.sigmoid(proj(c2 + D_MODEL, c2 + 2 * D_MODEL))
    merged = gate_a * branch_a + gate_b * branch_b
    x1 = x + _dot(merged.astype(BF16), wout_ref[...])
    x1_ref[...] = x1
    h2_ref[...] = _rms(x1, gffn_ref[...]).astype(BF16)


def _const_spec(shape):
    n = len(shape)
    return pl.BlockSpec(shape, lambda i, _n=n: (0,) * _n)


def _mixer(x, seq_len, g_mix, w_in, b_in, w_dw, b_dw, g_cln, b_cln, w_co, b_co, w_pool, s_pool, w_out,
           g_ffn):
    t = x.shape[0]
    tm = MIX_TM
    assert seq_len % tm == 0 and tm >= CONV_HALO
    d_in = w_in.shape[1]
    row = pl.BlockSpec((tm, D_MODEL), lambda i: (i, 0))
    return pl.pallas_call(
        functools.partial(_mixer_kernel, tiles_per_seq=seq_len // tm),
        grid=(t // tm,),
        in_specs=[
            row,
            _const_spec((1, D_MODEL)),
            _const_spec((D_MODEL, d_in)),
            _const_spec((1, d_in)),
            _const_spec((CONV_WIDTH, D_CONV)),
            _const_spec((1, D_CONV)),
            _const_spec((1, D_CONV)),
            _const_spec((1, D_CONV)),
            _const_spec((D_CONV, D_MODEL)),
            _const_spec((1, D_MODEL)),
            _const_spec((len(POOL_WINDOWS), POOL_GROUP, POOL_GROUP)),
            _const_spec((1, D_POOL)),
            _const_spec((D_MODEL, D_MODEL)),
            _const_spec((1, D_MODEL)),
        ],
        out_specs=[row, row],
        out_shape=[jax.ShapeDtypeStruct((t, D_MODEL), F32),
                   jax.ShapeDtypeStruct((t, D_MODEL), BF16)],
        scratch_shapes=[
            pltpu.VMEM((CONV_HALO + tm, D_CONV), F32),
            pltpu.VMEM((POOL_HALO + tm, D_POOL), F32),
            pltpu.VMEM((tm, D_CONV), F32),
            pltpu.VMEM((tm, D_POOL), F32),
        ],
        compiler_params=pltpu.CompilerParams(
            dimension_semantics=("arbitrary",), vmem_limit_bytes=VMEM_LIMIT),
        name="mixer",
    )(x, g_mix, w_in, b_in, w_dw, b_dw, g_cln, b_cln, w_co, b_co, w_pool, s_pool, w_out, g_ffn)


def _beats(v, other, other_is_later):
    v = jnp.broadcast_to(v, other.shape)
    return jnp.where(other_is_later, jnp.where(v >= other, 1, 0), jnp.where(v > other, 1, 0))


def _router_kernel(h2_ref, wrt_ref, br_ref, utri_ref, ltri_ref, gate_ref, pos_ref, cnt_ref):
    tm = ROUTER_TM
    logits = lax.dot_general(wrt_ref[...], h2_ref[...], (((1,), (1,)), ((), ())),
                             preferred_element_type=F32)
    scores = jax.nn.sigmoid(logits)
    sel = scores + br_ref[...]
    shape3 = (N_GROUPS, GROUP_SIZE, tm)
    sel3 = sel.reshape(shape3)
    scores3 = scores.reshape(shape3)
    neg_inf = jnp.float32(-jnp.inf)

    member = lax.broadcasted_iota(jnp.int32, shape3, 1)
    m1 = jnp.max(sel3, axis=1, keepdims=True)
    first = jnp.min(jnp.where(sel3 == m1, member, GROUP_SIZE), axis=1, keepdims=True)
    m2 = jnp.max(jnp.where(member == first, neg_inf, sel3), axis=1, keepdims=True)
    gscore = jnp.broadcast_to(m1 + m2, shape3)

    gidx = lax.broadcasted_iota(jnp.int32, shape3, 0)
    grank = jnp.zeros(shape3, jnp.int32)
    for j in range(N_GROUPS):
        sj = gscore[j:j + 1]
        grank = grank + _beats(sj, gscore, gidx > j)
    masked = jnp.where(grank < TOPK_GROUPS, sel3, neg_inf)

    eidx = gidx * GROUP_SIZE + member
    erank = jnp.zeros(shape3, jnp.int32)
    for gj in range(N_GROUPS):
        for mj in range(GROUP_SIZE):
            v = masked[gj:gj + 1, mj:mj + 1, :]
            erank = erank + _beats(v, masked, eidx > gj * GROUP_SIZE + mj)
    chosen = erank < TOP_K
    top_s = jnp.where(chosen, scores3, 0.0)
    denom = jnp.sum(jnp.sum(top_s, axis=1, keepdims=True), axis=0, keepdims=True)
    gates3 = top_s / denom * ROUTED_SCALE
    chosen2 = jnp.where(chosen, 1.0, 0.0).reshape(N_EXPERTS, tm)

    for w in range(tm // WIN):
        ls = slice(w * WIN, (w + 1) * WIN)
        mw = chosen2[:, ls]
        rank = _dot(mw.astype(BF16), utri_ref[...])
        n = jnp.sum(mw, axis=1, keepdims=True)
        run = jnp.floor((n + 7.0) * 0.125) * 8.0
        start = _dot(ltri_ref[...], jnp.broadcast_to(run, (N_EXPERTS, WIN)).astype(BF16))
        row3 = (rank + start).reshape(N_GROUPS, GROUP_SIZE, WIN)
        er = erank[:, :, ls]
        g3 = gates3[:, :, ls]
        for k in range(TOP_K):
            hit = er == k
            pk = jnp.sum(jnp.sum(jnp.where(hit, row3, 0.0), axis=1, keepdims=True), axis=0, keepdims=True)
            gk = jnp.sum(jnp.sum(jnp.where(hit, g3, 0.0), axis=1, keepdims=True), axis=0, keepdims=True)
            pos_ref[k:k + 1, ls] = pk.reshape(1, WIN).astype(jnp.int32)
            gate_ref[k:k + 1, ls] = gk.reshape(1, WIN)
        cnt_ref[w] = n


def _router(h2, w_rt, b_r):
    t = h2.shape[0]
    tm = ROUTER_TM
    utri = jnp.triu(jnp.ones((WIN, WIN), BF16), k=1)
    ltri = jnp.tril(jnp.ones((N_EXPERTS, N_EXPERTS), BF16), k=-1)
    return pl.pallas_call(
        _router_kernel,
        grid=(t // tm,),
        in_specs=[
            pl.BlockSpec((tm, D_MODEL), lambda i: (i, 0)),
            _const_spec((N_EXPERTS, D_MODEL)),
            _const_spec((N_EXPERTS, 1)),
            _const_spec((WIN, WIN)),
            _const_spec((N_EXPERTS, N_EXPERTS)),
        ],
        out_specs=[
            pl.BlockSpec((TOP_K, tm), lambda i: (0, i)),
            pl.BlockSpec((TOP_K, tm), lambda i: (0, i)),
            pl.BlockSpec((tm // WIN, N_EXPERTS, 1), lambda i: (i, 0, 0)),
        ],
        out_shape=[
            jax.ShapeDtypeStruct((TOP_K, t), F32),
            jax.ShapeDtypeStruct((TOP_K, t), jnp.int32),
            jax.ShapeDtypeStruct((t // WIN, N_EXPERTS, 1), F32),
        ],
        compiler_params=pltpu.CompilerParams(
            dimension_semantics=("arbitrary",), vmem_limit_bytes=VMEM_LIMIT),
        name="router",
    )(h2, w_rt, b_r, utri, ltri)


def _sorted_rows_bound(t):
    rows = t * TOP_K + (t // WIN) * N_EXPERTS * 7 + N_EXPERTS * (EXP_BM - 1)
    return -(-rows // EXP_BM) * EXP_BM


def _dispatch_plan(cnt, t):
    nw = t // WIN
    n = cnt.reshape(nw, N_EXPERTS).astype(jnp.int32)
    run = (n + 7) // 8 * 8
    local_end = jnp.cumsum(run, axis=1)
    local_off = jnp.concatenate([jnp.zeros((nw, 1), jnp.int32), local_end], axis=1)
    total = jnp.sum(run, axis=0)
    region = (total + EXP_BM - 1) // EXP_BM * EXP_BM
    region_end = jnp.cumsum(region)
    base = region_end - region
    global_off = base[None, :] + jnp.cumsum(run, axis=0) - run
    n_blocks = _sorted_rows_bound(t) // EXP_BM
    n_used = region_end[-1] // EXP_BM
    blk = jnp.arange(n_blocks, dtype=jnp.int32)
    blk_expert = jnp.sum((region_end[None, :] <= blk[:, None] * EXP_BM).astype(jnp.int32), axis=1)
    blk_expert = jnp.minimum(blk_expert, N_EXPERTS - 1)
    eid = jnp.arange(N_EXPERTS, dtype=jnp.int32)
    later_nonempty = (eid[None, :] > eid[:, None]) & (region[None, :] > 0)
    next_expert = jnp.min(jnp.where(later_nonempty, eid[None, :], N_EXPERTS), axis=1).astype(jnp.int32)
    return dict(
        local_off=local_off.reshape(-1), global_off=global_off.reshape(-1),
        fill_off=base + total, fill_cnt=region - total,
        blk_expert=blk_expert.astype(jnp.int32), next_expert=next_expert,
        n_used=n_used.reshape(1).astype(jnp.int32))


def _run_copy(local_ref, global_ref, win, e, vmem_buf, slot, hbm_buf, sem, to_hbm):
    lo = pl.multiple_of(local_ref[win * (N_EXPERTS + 1) + e], 8)
    cnt = pl.multiple_of(local_ref[win * (N_EXPERTS + 1) + e + 1] - lo, 8)
    go = pl.multiple_of(global_ref[win * N_EXPERTS + e], 8)
    v = vmem_buf.at[pl.ds(pl.multiple_of(slot * SEL_ROWS + lo, 8), cnt)]
    h = hbm_buf.at[pl.ds(go, cnt)]
    cp = pltpu.make_async_copy(v, h, sem.at[slot]) if to_hbm else pltpu.make_async_copy(h, v, sem.at[slot])
    return cnt, cp


def _start_runs(local_ref, global_ref, win, vmem_buf, slot, hbm_buf, sem, to_hbm):
    def body(e, carry):
        cnt, cp = _run_copy(local_ref, global_ref, win, e, vmem_buf, slot, hbm_buf, sem, to_hbm)

        @pl.when(cnt > 0)
        def _():
            cp.start()
        return carry
    lax.fori_loop(0, N_EXPERTS, body, 0)


def _wait_runs(local_ref, win, vmem_buf, slot, hbm_buf, sem, to_hbm):
    total = pl.multiple_of(local_ref[win * (N_EXPERTS + 1) + N_EXPERTS], 8)
    v = vmem_buf.at[pl.ds(pl.multiple_of(slot * SEL_ROWS, 8), total)]
    h = hbm_buf.at[pl.ds(0, total)]
    cp = pltpu.make_async_copy(v, h, sem.at[slot]) if to_hbm else pltpu.make_async_copy(h, v, sem.at[slot])

    @pl.when(total > 0)
    def _():
        cp.wait()


def _dispatch_kernel(local_ref, global_ref, fill_off_ref, fill_cnt_ref, h2_ref, pos_ref, xs_hbm, sbuf,
                     s_ref, sem, zsem, *, n_win):
    w = pl.program_id(0)
    slot = w % 2
    pos = pos_ref[...]

    h2 = h2_ref[...]
    for g in range(SEL_ROWS // SEL_MM):
        for sg in range(SEL_MM // SEL_RG):
            r0 = g * SEL_MM + sg * SEL_RG
            rid = r0 + lax.broadcasted_iota(jnp.int32, (SEL_RG, WIN), 0)
            acc = jnp.zeros((SEL_RG, WIN), F32)
            for k in range(TOP_K):
                acc = jnp.where(rid == pos[k:k + 1, :], 1.0, acc)
            s_ref[r0:r0 + SEL_RG, :] = acc.astype(BF16)
        rows = slice(g * SEL_MM, (g + 1) * SEL_MM)
        dst = pl.multiple_of(slot * SEL_ROWS + g * SEL_MM, SEL_MM)
        sbuf[pl.ds(dst, SEL_MM), :] = _dot(s_ref[rows, :], h2).astype(BF16)

    _start_runs(local_ref, global_ref, w, sbuf, slot, xs_hbm, sem, True)

    @pl.when(w > 0)
    def _():
        _wait_runs(local_ref, w - 1, sbuf, 1 - slot, xs_hbm, sem, True)

    @pl.when(w == n_win - 1)
    def _():
        sbuf[2 * SEL_ROWS:, :] = jnp.zeros((EXP_BM, D_MODEL), BF16)

        def fill(e, wait):
            cnt = pl.multiple_of(fill_cnt_ref[e], 8)
            off = pl.multiple_of(fill_off_ref[e], 8)
            cp = pltpu.make_async_copy(sbuf.at[pl.ds(2 * SEL_ROWS, cnt)], xs_hbm.at[pl.ds(off, cnt)], zsem)

            @pl.when(cnt > 0)
            def _():
                if wait:
                    cp.wait()
                else:
                    cp.start()

        def start_body(e, carry):
            fill(e, False)
            return carry

        def wait_body(e, carry):
            fill(e, True)
            return carry
        lax.fori_loop(0, N_EXPERTS, start_body, 0)
        _wait_runs(local_ref, w, sbuf, slot, xs_hbm, sem, True)
        lax.fori_loop(0, N_EXPERTS, wait_body, 0)


def _staging_shape(extra_rows):
    return jax.ShapeDtypeStruct((2 * SEL_ROWS + extra_rows, D_MODEL), BF16)


def _staging_spec(extra_rows):
    return pl.BlockSpec((2 * SEL_ROWS + extra_rows, D_MODEL), lambda w, *_: (0, 0))


def _dispatch(plan, h2, pos):
    t = h2.shape[0]
    n_win = t // WIN
    return pl.pallas_call(
        functools.partial(_dispatch_kernel, n_win=n_win),
        grid_spec=pltpu.PrefetchScalarGridSpec(
            num_scalar_prefetch=4,
            grid=(n_win,),
            in_specs=[
                pl.BlockSpec((WIN, D_MODEL), lambda w, *_: (w, 0)),
                pl.BlockSpec((TOP_K, WIN), lambda w, *_: (0, w)),
            ],
            out_specs=[pl.BlockSpec(memory_space=pl.ANY), _staging_spec(EXP_BM)],
            scratch_shapes=[
                pltpu.VMEM((SEL_ROWS, WIN), BF16),
                pltpu.SemaphoreType.DMA((2,)),
                pltpu.SemaphoreType.DMA,
            ]),
        out_shape=[jax.ShapeDtypeStruct((_sorted_rows_bound(t), D_MODEL), BF16), _staging_shape(EXP_BM)],
        compiler_params=pltpu.CompilerParams(
            dimension_semantics=("arbitrary",), vmem_limit_bytes=VMEM_LIMIT),
        name="dispatch",
    )(plan['local_off'], plan['global_off'], plan['fill_off'], plan['fill_cnt'], h2, pos)[0]


def _expert_kernel(blk_expert_ref, next_expert_ref, n_used_ref, xs_hbm, wg_hbm, wu_hbm, wd_hbm, ys_hbm,
                   xbuf, ybuf, wg_st, wu_st, wd_st, wg_bf, wu_bf, wd_bf, xsem, ysem, wsem):
    n_used = n_used_ref[0]
    part = EXP_BM // EXP_SPLIT

    def row_copies(b, slot, fetch):
        out = []
        for q in range(EXP_SPLIT):
            hbm_rows = pl.ds(pl.multiple_of(b * EXP_BM + q * part, part), part)
            if fetch:
                out.append(pltpu.make_async_copy(xs_hbm.at[hbm_rows], xbuf.at[slot, q * part:(q + 1) * part],
                                                 xsem.at[slot]))
            else:
                out.append(pltpu.make_async_copy(ybuf.at[slot, q * part:(q + 1) * part], ys_hbm.at[hbm_rows],
                                                 ysem.at[slot]))
        return out

    def weight_copies(e, slot):
        return [pltpu.make_async_copy(wg_hbm.at[e], wg_st.at[slot], wsem.at[slot]),
                pltpu.make_async_copy(wu_hbm.at[e], wu_st.at[slot], wsem.at[slot]),
                pltpu.make_async_copy(wd_hbm.at[e], wd_st.at[slot], wsem.at[slot])]

    def start(copies):
        for c in copies:
            c.start()

    def wait(copies):
        for c in copies:
            c.wait()

    for ahead in range(EXP_XDEPTH - 1):
        @pl.when(ahead < n_used)
        def _(ahead=ahead):
            start(row_copies(ahead, ahead, True))

    @pl.when(n_used > 0)
    def _():
        start(weight_copies(blk_expert_ref[0], 0))

    def body(b, wslot):
        e = blk_expert_ref[b]
        new_expert = jnp.logical_or(b == 0, e != blk_expert_ref[jnp.maximum(b - 1, 0)])

        @pl.when(new_expert)
        def _():
            wait(weight_copies(e, wslot))
            wg_bf[...] = wg_st[wslot].astype(BF16)
            wu_bf[...] = wu_st[wslot].astype(BF16)
            wd_bf[...] = wd_st[wslot].astype(BF16)
            nxt = next_expert_ref[e]

            @pl.when(nxt < N_EXPERTS)
            def _():
                start(weight_copies(nxt, 1 - wslot))

        @pl.when(b + EXP_XDEPTH - 1 < n_used)
        def _():
            start(row_copies(b + EXP_XDEPTH - 1, (b + EXP_XDEPTH - 1) % EXP_XDEPTH, True))

        xslot = b % EXP_XDEPTH
        yslot = b % 2
        wait(row_copies(b, xslot, True))

        @pl.when(b >= 2)
        def _():
            wait(row_copies(b - 2, yslot, False))

        x = xbuf[xslot]
        hg = _dot(x, wg_bf[...])
        hb = hg * jax.nn.sigmoid(hg) * _dot(x, wu_bf[...])
        ybuf[yslot] = _dot(hb.astype(BF16), wd_bf[...]).astype(BF16)
        start(row_copies(b, yslot, False))
        return jnp.where(new_expert, 1 - wslot, wslot)

    lax.fori_loop(0, n_used, body, jnp.int32(0))

    for back in (2, 1):
        @pl.when(n_used >= back)
        def _(back=back):
            wait(row_copies(n_used - back, (n_used - back) % 2, False))


def _experts(plan, xs, w_gate, w_up, w_down):
    any_spec = pl.BlockSpec(memory_space=pl.ANY)
    return pl.pallas_call(
        _expert_kernel,
        grid_spec=pltpu.PrefetchScalarGridSpec(
            num_scalar_prefetch=3,
            grid=(1,),
            in_specs=[any_spec, any_spec, any_spec, any_spec],
            out_specs=any_spec,
            scratch_shapes=[
                pltpu.VMEM((EXP_XDEPTH, EXP_BM, D_MODEL), BF16),
                pltpu.VMEM((2, EXP_BM, D_MODEL), BF16),
                pltpu.VMEM((2, D_MODEL, D_EXPERT), F32),
                pltpu.VMEM((2, D_MODEL, D_EXPERT), F32),
                pltpu.VMEM((2, D_EXPERT, D_MODEL), F32),
                pltpu.VMEM((D_MODEL, D_EXPERT), BF16),
                pltpu.VMEM((D_MODEL, D_EXPERT), BF16),
                pltpu.VMEM((D_EXPERT, D_MODEL), BF16),
                pltpu.SemaphoreType.DMA((EXP_XDEPTH,)),
                pltpu.SemaphoreType.DMA((2,)),
                pltpu.SemaphoreType.DMA((2,)),
            ]),
        out_shape=jax.ShapeDtypeStruct(xs.shape, BF16),
        compiler_params=pltpu.CompilerParams(
            dimension_semantics=("arbitrary",), vmem_limit_bytes=VMEM_LIMIT),
        name="experts",
    )(plan['blk_expert'], plan['next_expert'], plan['n_used'], xs, w_gate, w_up, w_down)


def _combine_kernel(local_ref, global_ref, x1_ref, h2_ref, p_ref, pos_ref, gate_ref, wsg_ref, wsu_ref,
                    wsd_ref, gple_ref, wpg_ref, wp_ref, gfin_ref, ys_hbm, o_ref, ybuf, st_ref, sem,
                    *, n_win, final_norm):
    w = pl.program_id(0)
    slot = w % 2

    @pl.when(w == 0)
    def _():
        ybuf[...] = jnp.zeros(ybuf.shape, BF16)
        _start_runs(local_ref, global_ref, w, ybuf, slot, ys_hbm, sem, False)

    @pl.when(w + 1 < n_win)
    def _():
        _start_runs(local_ref, global_ref, w + 1, ybuf, 1 - slot, ys_hbm, sem, False)

    def build_group(lg):
        cols = slice(lg * CMB_LG, (lg + 1) * CMB_LG)
        for rg in range(WIN // CMB_RG):
            rows = slice(rg * CMB_RG, (rg + 1) * CMB_RG)
            cid = lg * CMB_LG + lax.broadcasted_iota(jnp.int32, (CMB_RG, CMB_LG), 1)
            acc = jnp.zeros((CMB_RG, CMB_LG), F32)
            for k in range(TOP_K):
                acc = jnp.where(cid == pos_ref[rows, k:k + 1], gate_ref[rows, k:k + 1], acc)
            st_ref[rows, cols] = acc.astype(BF16)

    build_group(0)
    h2 = h2_ref[...]
    hs = _dot(h2, wsg_ref[...])
    hs = hs * jax.nn.sigmoid(hs) * _dot(h2, wsu_ref[...])
    shared = _dot(hs.astype(BF16), wsd_ref[...])

    _wait_runs(local_ref, w, ybuf, slot, ys_hbm, sem, False)
    routed = None
    n_groups = SEL_ROWS // CMB_LG
    for lg in range(n_groups):
        if lg + 1 < n_groups:
            build_group(lg + 1)
        src = pl.multiple_of(slot * SEL_ROWS + lg * CMB_LG, CMB_LG)
        part = _dot(st_ref[:, lg * CMB_LG:(lg + 1) * CMB_LG], ybuf[pl.ds(src, CMB_LG), :])
        routed = part if routed is None else routed + part
    x2 = x1_ref[...] + routed + shared

    hp = _rms(x2, gple_ref[...]).astype(BF16)
    gate = jax.nn.sigmoid(_dot(hp, wpg_ref[...]))
    x3 = x2 + gate * _dot(p_ref[...].astype(BF16), wp_ref[...])
    o_ref[...] = _rms(x3, gfin_ref[...]) if final_norm else x3


def _combine(plan, ys, x1, h2, p, pos_tm, gate_tm, wsg, wsu, wsd, g_ple, w_pg, w_p, g_fin, final_norm):
    t = x1.shape[0]
    n_win = t // WIN
    row = lambda width: pl.BlockSpec((WIN, width), lambda w, *_: (w, 0))
    const = lambda shape: pl.BlockSpec(shape, lambda w, *_: (0,) * len(shape))
    return pl.pallas_call(
        functools.partial(_combine_kernel, n_win=n_win, final_norm=final_norm),
        grid_spec=pltpu.PrefetchScalarGridSpec(
            num_scalar_prefetch=2,
            grid=(n_win,),
            in_specs=[
                row(D_MODEL), row(D_MODEL), row(PLE_DIM), row(TOP_K), row(TOP_K),
                const((D_MODEL, D_EXPERT)), const((D_MODEL, D_EXPERT)), const((D_EXPERT, D_MODEL)),
                const((1, D_MODEL)), const((D_MODEL, D_MODEL)), const((PLE_DIM, D_MODEL)),
                const((1, D_MODEL)),
                pl.BlockSpec(memory_space=pl.ANY),
            ],
            out_specs=[row(D_MODEL), _staging_spec(0)],
            scratch_shapes=[
                pltpu.VMEM((WIN, SEL_ROWS), BF16),
                pltpu.SemaphoreType.DMA((2,)),
            ]),
        out_shape=[jax.ShapeDtypeStruct((t, D_MODEL), F32), _staging_shape(0)],
        compiler_params=pltpu.CompilerParams(
            dimension_semantics=("arbitrary",), vmem_limit_bytes=VMEM_LIMIT),
        name="combine",
    )(plan['local_off'], plan['global_off'], x1, h2, p, pos_tm, gate_tm, wsg, wsu, wsd, g_ple, w_pg, w_p,
      g_fin, ys)[0]


def kernel(x, p, g_mix, w_in, b_in, w_dw, b_dw, g_cln, b_cln, w_conv_out, b_conv_out, w_pool, s_pool,
           w_out, g_ffn, w_router, b_router, w_e_gate, w_e_up, w_e_down, w_s_gate, w_s_up, w_s_down,
           g_ple, w_ple_gate, w_ple, g_final):
    bsz, s, d = x.shape
    t = bsz * s
    depth = w_in.shape[0]
    xt = x.reshape(t, d)
    row = lambda v: v.reshape(1, -1)
    for i in range(depth):
        x1, h2 = _mixer(
            xt, s, row(g_mix[i]), w_in[i].astype(BF16), row(b_in[i]), w_dw[i], row(b_dw[i]),
            row(g_cln[i]), row(b_cln[i]), w_conv_out[i].astype(BF16), row(b_conv_out[i]),
            w_pool[i].astype(BF16), row(s_pool[i]), w_out[i].astype(BF16), row(g_ffn[i]))
        gate, pos, cnt = _router(h2, w_router[i].T.astype(BF16), b_router[i].reshape(N_EXPERTS, 1))
        plan = _dispatch_plan(cnt, t)
        xs = _dispatch(plan, h2, pos)
        ys = _experts(plan, xs, w_e_gate[i], w_e_up[i], w_e_down[i])
        xt = _combine(
            plan, ys, x1, h2, p[i].reshape(t, PLE_DIM), pos.T, gate.T,
            w_s_gate[i].astype(BF16), w_s_up[i].astype(BF16), w_s_down[i].astype(BF16),
            row(g_ple[i]), w_ple_gate[i].astype(BF16), w_ple[i].astype(BF16), row(g_final),
            final_norm=(i == depth - 1))
    return xt.reshape(bsz, s, d)
```

```python
import functools

import jax
import jax.numpy as jnp
from jax import lax
from jax.experimental import pallas as pl
from jax.experimental.pallas import tpu as pltpu

D_MODEL = 1024
D_CONV = 1024
D_POOL = 1024
CONV_WIDTH = 31
POOL_WINDOWS = (2, 4, 8, 16)
POOL_GROUP = 256
PLE_DIM = 256
N_EXPERTS = 64
N_GROUPS = 8
GROUP_SIZE = N_EXPERTS // N_GROUPS
TOPK_GROUPS = 4
TOP_K = 8
D_EXPERT = 256
ROUTED_SCALE = 2.5
NORM_EPS = 1e-6

F32 = jnp.float32
BF16 = jnp.bfloat16

MIX_TM = 256
MIX_NV = MIX_TM // 8
CONV_MG = 8
ROW_CHUNK = 64
LANE = 128

ROUTER_TM = 1024
WIN = 256
SEL_ROWS = 2560
SEL_RG = 64
SEL_MM = 512
EXP_BM = 512
EXP_XDEPTH = 3
EXP_SPLIT = 4
CMB_RG = 32
CMB_LG = 512

VMEM_LIMIT = 56 * 1024 * 1024


def _rms(x, g):
    ms = jnp.mean(x * x, axis=-1, keepdims=True)
    return x * lax.rsqrt(ms + NORM_EPS) * g


def _dot(a, b):
    return jnp.dot(a, b, preferred_element_type=F32)


def _mixer_kernel(x_ref, gmix_ref, win_ref, bin_ref, wdw_ref, bdw_ref, gcln_ref, bcln_ref,
                  wco_ref, bco_ref, wpool_ref, spool_ref, wout_ref, gffn_ref, perm_ref, unperm_ref,
                  x1_ref, h2_ref, a_ext, a_prev, u_ext, u_prev, c_buf, q_buf, *, tiles_per_seq):
    i = pl.program_id(0) % tiles_per_seq
    tm = MIX_TM
    nv = MIX_NV

    @pl.when(i == 0)
    def _():
        a_prev[...] = jnp.zeros(a_prev.shape, F32)
        u_prev[...] = jnp.zeros(u_prev.shape, F32)

    x = x_ref[...]
    h = _dot(perm_ref[...], _rms(x, gmix_ref[...]).astype(BF16)).astype(BF16)

    def proj(lo, hi):
        return _dot(h, win_ref[:, lo:hi]) + bin_ref[:, lo:hi]

    glu = proj(0, D_CONV) * jax.nn.sigmoid(proj(D_CONV, 2 * D_CONV))
    for lc in range(D_CONV // LANE):
        a_ext[lc, tm:2 * tm, :] = glu[:, lc * LANE:(lc + 1) * LANE]
    u_ext[tm:2 * tm, :] = proj(2 * D_CONV, 2 * D_CONV + D_POOL)

    def delayed_groups(ext, prev, first_group):
        last_row = lax.broadcasted_iota(jnp.int32, (8, ext.shape[-1]), 0) == 7
        for g in range(first_group, nv):
            rows = slice(8 * g, 8 * g + 8)
            mixed = jnp.where(last_row, prev[rows, :], ext[tm + 8 * g:tm + 8 * g + 8, :])
            ext[rows, :] = pltpu.roll(mixed, 1, axis=0)
            prev[rows, :] = ext[tm + 8 * g:tm + 8 * g + 8, :]

    delayed_groups(u_ext, u_prev, nv - (max(POOL_WINDOWS) - 1))

    def conv_column(lc, carry):
        a_col = a_ext.at[lc]
        delayed_groups(a_col, a_prev.at[lc], nv - (CONV_WIDTH - 1))
        w_col = wdw_ref.at[lc]
        for g0 in range(0, nv, CONV_MG):
            acc = None
            for k in range(CONV_WIDTH):
                src = nv + g0 + k - (CONV_WIDTH - 1)
                term = a_col[8 * src:8 * (src + CONV_MG), :] * w_col[k:k + 1, :]
                acc = term if acc is None else acc + term
            c_buf[lc, 8 * g0:8 * (g0 + CONV_MG), :] = acc + bdw_ref[lc]
        return carry
    lax.fori_loop(0, D_CONV // LANE, conv_column, 0)

    c = jnp.concatenate([c_buf[lc] for lc in range(D_CONV // LANE)], axis=-1)
    mu = jnp.mean(c, axis=-1, keepdims=True)
    xc = c - mu
    var = jnp.mean(xc * xc, axis=-1, keepdims=True)
    y = xc * lax.rsqrt(var + NORM_EPS) * gcln_ref[...] + bcln_ref[...]
    y = y * jax.nn.sigmoid(y)
    branch_a = _dot(y.astype(BF16), wco_ref[...]) + bco_ref[...]

    for r0 in range(0, tm, ROW_CHUNK):
        row = r0 + lax.broadcasted_iota(jnp.int32, (ROW_CHUNK, POOL_GROUP), 0)
        t1 = i * tm + (row % 8) * nv + row // 8 + 1
        for gi, w in enumerate(POOL_WINDOWS):
            ls = slice(gi * POOL_GROUP, (gi + 1) * POOL_GROUP)
            tok = u_ext[tm + r0:tm + r0 + ROW_CHUNK, ls]
            s = tok
            for j in range(1, w):
                s = s + u_ext[tm + r0 - 8 * j:tm + r0 - 8 * j + ROW_CHUNK, ls]
            cnt = jnp.minimum(t1, w).astype(F32)
            q_buf[r0:r0 + ROW_CHUNK, ls] = s / cnt - tok

    qs_out = []
    for gi in range(len(POOL_WINDOWS)):
        ls = slice(gi * POOL_GROUP, (gi + 1) * POOL_GROUP)
        qs_out.append(_dot(q_buf[:, ls].astype(BF16), wpool_ref[gi]) * spool_ref[:, ls])
    branch_b = jnp.concatenate(qs_out, axis=-1)

    c2 = 2 * D_CONV + D_POOL
    gate_a = jax.nn.sigmoid(proj(c2, c2 + D_MODEL))
    gate_b = jax.nn.sigmoid(proj(c2 + D_MODEL, c2 + 2 * D_MODEL))
    merged = gate_a * branch_a + gate_b * branch_b
    merged = _dot(unperm_ref[...], merged.astype(BF16)).astype(BF16)
    x1 = x + _dot(merged, wout_ref[...])
    x1_ref[...] = x1
    h2_ref[...] = _rms(x1, gffn_ref[...]).astype(BF16)


def _const_spec(shape):
    n = len(shape)
    return pl.BlockSpec(shape, lambda i, _n=n: (0,) * _n)


def _mixer(x, seq_len, g_mix, w_in, b_in, w_dw, b_dw, g_cln, b_cln, w_co, b_co, w_pool, s_pool, w_out,
           g_ffn):
    t = x.shape[0]
    tm = MIX_TM
    assert seq_len % tm == 0 and MIX_NV >= CONV_WIDTH and MIX_NV >= max(POOL_WINDOWS)
    d_in = w_in.shape[1]
    row = pl.BlockSpec((tm, D_MODEL), lambda i: (i, 0))
    n_col = D_CONV // LANE
    w_dw = w_dw.reshape(CONV_WIDTH, n_col, LANE).transpose(1, 0, 2)
    b_dw = b_dw.reshape(n_col, 1, LANE)
    r = jnp.arange(tm)
    perm = ((r % 8) * MIX_NV + r // 8)[:, None] == jnp.arange(tm)[None, :]
    perm = perm.astype(BF16)
    return pl.pallas_call(
        functools.partial(_mixer_kernel, tiles_per_seq=seq_len // tm),
        grid=(t // tm,),
        in_specs=[
            row,
            _const_spec((1, D_MODEL)),
            _const_spec((D_MODEL, d_in)),
            _const_spec((1, d_in)),
            _const_spec((n_col, CONV_WIDTH, LANE)),
            _const_spec((n_col, 1, LANE)),
            _const_spec((1, D_CONV)),
            _const_spec((1, D_CONV)),
            _const_spec((D_CONV, D_MODEL)),
            _const_spec((1, D_MODEL)),
            _const_spec((len(POOL_WINDOWS), POOL_GROUP, POOL_GROUP)),
            _const_spec((1, D_POOL)),
            _const_spec((D_MODEL, D_MODEL)),
            _const_spec((1, D_MODEL)),
            _const_spec((tm, tm)),
            _const_spec((tm, tm)),
        ],
        out_specs=[row, row],
        out_shape=[jax.ShapeDtypeStruct((t, D_MODEL), F32),
                   jax.ShapeDtypeStruct((t, D_MODEL), BF16)],
        scratch_shapes=[
            pltpu.VMEM((n_col, 2 * tm, LANE), F32),
            pltpu.VMEM((n_col, tm, LANE), F32),
            pltpu.VMEM((2 * tm, D_POOL), F32),
            pltpu.VMEM((tm, D_POOL), F32),
            pltpu.VMEM((n_col, tm, LANE), F32),
            pltpu.VMEM((tm, D_POOL), F32),
        ],
        compiler_params=pltpu.CompilerParams(
            dimension_semantics=("arbitrary",), vmem_limit_bytes=VMEM_LIMIT),
        name="mixer",
    )(x, g_mix, w_in, b_in, w_dw, b_dw, g_cln, b_cln, w_co, b_co, w_pool, s_pool, w_out, g_ffn, perm, perm.T)


def _beats(v, other, other_is_later):
    v = jnp.broadcast_to(v, other.shape)
    return jnp.where(other_is_later, jnp.where(v >= other, 1, 0), jnp.where(v > other, 1, 0))


def _router_kernel(h2_ref, wrt_ref, br_ref, utri_ref, ltri_ref, gate_ref, pos_ref, cnt_ref):
    tm = ROUTER_TM
    logits = lax.dot_general(wrt_ref[...], h2_ref[...], (((1,), (1,)), ((), ())),
                             preferred_element_type=F32)
    scores = jax.nn.sigmoid(logits)
    sel = scores + br_ref[...]
    shape3 = (N_GROUPS, GROUP_SIZE, tm)
    sel3 = sel.reshape(shape3)
    scores3 = scores.reshape(shape3)
    neg_inf = jnp.float32(-jnp.inf)

    member = lax.broadcasted_iota(jnp.int32, shape3, 1)
    m1 = jnp.max(sel3, axis=1, keepdims=True)
    first = jnp.min(jnp.where(sel3 == m1, member, GROUP_SIZE), axis=1, keepdims=True)
    m2 = jnp.max(jnp.where(member == first, neg_inf, sel3), axis=1, keepdims=True)
    gscore = jnp.broadcast_to(m1 + m2, shape3)

    gidx = lax.broadcasted_iota(jnp.int32, shape3, 0)
    grank = jnp.zeros(shape3, jnp.int32)
    for j in range(N_GROUPS):
        sj = gscore[j:j + 1]
        grank = grank + _beats(sj, gscore, gidx > j)
    masked = jnp.where(grank < TOPK_GROUPS, sel3, neg_inf)

    eidx = gidx * GROUP_SIZE + member
    erank = jnp.zeros(shape3, jnp.int32)
    for gj in range(N_GROUPS):
        for mj in range(GROUP_SIZE):
            v = masked[gj:gj + 1, mj:mj + 1, :]
            erank = erank + _beats(v, masked, eidx > gj * GROUP_SIZE + mj)
    chosen = erank < TOP_K
    top_s = jnp.where(chosen, scores3, 0.0)
    denom = jnp.sum(jnp.sum(top_s, axis=1, keepdims=True), axis=0, keepdims=True)
    gates3 = top_s / denom * ROUTED_SCALE
    chosen2 = jnp.where(chosen, 1.0, 0.0).reshape(N_EXPERTS, tm)

    for w in range(tm // WIN):
        ls = slice(w * WIN, (w + 1) * WIN)
        mw = chosen2[:, ls]
        rank = _dot(mw.astype(BF16), utri_ref[...])
        n = jnp.sum(mw, axis=1, keepdims=True)
        run = jnp.floor((n + 7.0) * 0.125) * 8.0
        start = _dot(ltri_ref[...], jnp.broadcast_to(run, (N_EXPERTS, WIN)).astype(BF16))
        row3 = (rank + start).reshape(N_GROUPS, GROUP_SIZE, WIN)
        er = erank[:, :, ls]
        g3 = gates3[:, :, ls]
        for k in range(TOP_K):
            hit = er == k
            pk = jnp.sum(jnp.sum(jnp.where(hit, row3, 0.0), axis=1, keepdims=True), axis=0, keepdims=True)
            gk = jnp.sum(jnp.sum(jnp.where(hit, g3, 0.0), axis=1, keepdims=True), axis=0, keepdims=True)
            pos_ref[k:k + 1, ls] = pk.reshape(1, WIN).astype(jnp.int32)
            gate_ref[k:k + 1, ls] = gk.reshape(1, WIN)
        cnt_ref[w] = n


def _router(h2, w_rt, b_r):
    t = h2.shape[0]
    tm = ROUTER_TM
    utri = jnp.triu(jnp.ones((WIN, WIN), BF16), k=1)
    ltri = jnp.tril(jnp.ones((N_EXPERTS, N_EXPERTS), BF16), k=-1)
    return pl.pallas_call(
        _router_kernel,
        grid=(t // tm,),
        in_specs=[
            pl.BlockSpec((tm, D_MODEL), lambda i: (i, 0)),
            _const_spec((N_EXPERTS, D_MODEL)),
            _const_spec((N_EXPERTS, 1)),
            _const_spec((WIN, WIN)),
            _const_spec((N_EXPERTS, N_EXPERTS)),
        ],
        out_specs=[
            pl.BlockSpec((TOP_K, tm), lambda i: (0, i)),
            pl.BlockSpec((TOP_K, tm), lambda i: (0, i)),
            pl.BlockSpec((tm // WIN, N_EXPERTS, 1), lambda i: (i, 0, 0)),
        ],
        out_shape=[
            jax.ShapeDtypeStruct((TOP_K, t), F32),
            jax.ShapeDtypeStruct((TOP_K, t), jnp.int32),
            jax.ShapeDtypeStruct((t // WIN, N_EXPERTS, 1), F32),
        ],
        compiler_params=pltpu.CompilerParams(
            dimension_semantics=("arbitrary",), vmem_limit_bytes=VMEM_LIMIT),
        name="router",
    )(h2, w_rt, b_r, utri, ltri)


def _sorted_rows_bound(t):
    rows = t * TOP_K + (t // WIN) * N_EXPERTS * 7 + N_EXPERTS * (EXP_BM - 1)
    return -(-rows // EXP_BM) * EXP_BM


def _dispatch_plan(cnt, t):
    nw = t // WIN
    n = cnt.reshape(nw, N_EXPERTS).astype(jnp.int32)
    run = (n + 7) // 8 * 8
    local_end = jnp.cumsum(run, axis=1)
    local_off = jnp.concatenate([jnp.zeros((nw, 1), jnp.int32), local_end], axis=1)
    total = jnp.sum(run, axis=0)
    region = (total + EXP_BM - 1) // EXP_BM * EXP_BM
    region_end = jnp.cumsum(region)
    base = region_end - region
    global_off = base[None, :] + jnp.cumsum(run, axis=0) - run
    n_blocks = _sorted_rows_bound(t) // EXP_BM
    n_used = region_end[-1] // EXP_BM
    blk = jnp.arange(n_blocks, dtype=jnp.int32)
    blk_expert = jnp.sum((region_end[None, :] <= blk[:, None] * EXP_BM).astype(jnp.int32), axis=1)
    blk_expert = jnp.minimum(blk_expert, N_EXPERTS - 1)
    eid = jnp.arange(N_EXPERTS, dtype=jnp.int32)
    later_nonempty = (eid[None, :] > eid[:, None]) & (region[None, :] > 0)
    next_expert = jnp.min(jnp.where(later_nonempty, eid[None, :], N_EXPERTS), axis=1).astype(jnp.int32)
    return dict(
        local_off=local_off.reshape(-1), global_off=global_off.reshape(-1),
        fill_off=base + total, fill_cnt=region - total,
        blk_expert=blk_expert.astype(jnp.int32), next_expert=next_expert,
        n_used=n_used.reshape(1).astype(jnp.int32))


def _run_copy(local_ref, global_ref, win, e, vmem_buf, slot, hbm_buf, sem, to_hbm):
    lo = pl.multiple_of(local_ref[win * (N_EXPERTS + 1) + e], 8)
    cnt = pl.multiple_of(local_ref[win * (N_EXPERTS + 1) + e + 1] - lo, 8)
    go = pl.multiple_of(global_ref[win * N_EXPERTS + e], 8)
    v = vmem_buf.at[pl.ds(pl.multiple_of(slot * SEL_ROWS + lo, 8), cnt)]
    h = hbm_buf.at[pl.ds(go, cnt)]
    cp = pltpu.make_async_copy(v, h, sem.at[slot]) if to_hbm else pltpu.make_async_copy(h, v, sem.at[slot])
    return cnt, cp


def _start_runs(local_ref, global_ref, win, vmem_buf, slot, hbm_buf, sem, to_hbm):
    def body(e, carry):
        cnt, cp = _run_copy(local_ref, global_ref, win, e, vmem_buf, slot, hbm_buf, sem, to_hbm)

        @pl.when(cnt > 0)
        def _():
            cp.start()
        return carry
    lax.fori_loop(0, N_EXPERTS, body, 0)


def _wait_runs(local_ref, win, vmem_buf, slot, hbm_buf, sem, to_hbm):
    total = pl.multiple_of(local_ref[win * (N_EXPERTS + 1) + N_EXPERTS], 8)
    v = vmem_buf.at[pl.ds(pl.multiple_of(slot * SEL_ROWS, 8), total)]
    h = hbm_buf.at[pl.ds(0, total)]
    cp = pltpu.make_async_copy(v, h, sem.at[slot]) if to_hbm else pltpu.make_async_copy(h, v, sem.at[slot])

    @pl.when(total > 0)
    def _():
        cp.wait()


def _dispatch_kernel(local_ref, global_ref, fill_off_ref, fill_cnt_ref, h2_ref, pos_ref, xs_hbm, sbuf,
                     s_ref, sem, zsem, *, n_win):
    w = pl.program_id(0)
    slot = w % 2
    pos = pos_ref[...]

    h2 = h2_ref[...]
    for g in range(SEL_ROWS // SEL_MM):
        for sg in range(SEL_MM // SEL_RG):
            r0 = g * SEL_MM + sg * SEL_RG
            rid = r0 + lax.broadcasted_iota(jnp.int32, (SEL_RG, WIN), 0)
            acc = jnp.zeros((SEL_RG, WIN), F32)
            for k in range(TOP_K):
                acc = jnp.where(rid == pos[k:k + 1, :], 1.0, acc)
            s_ref[r0:r0 + SEL_RG, :] = acc.astype(BF16)
        rows = slice(g * SEL_MM, (g + 1) * SEL_MM)
        dst = pl.multiple_of(slot * SEL_ROWS + g * SEL_MM, SEL_MM)
        sbuf[pl.ds(dst, SEL_MM), :] = _dot(s_ref[rows, :], h2).astype(BF16)

    _start_runs(local_ref, global_ref, w, sbuf, slot, xs_hbm, sem, True)

    @pl.when(w > 0)
    def _():
        _wait_runs(local_ref, w - 1, sbuf, 1 - slot, xs_hbm, sem, True)

    @pl.when(w == n_win - 1)
    def _():
        sbuf[2 * SEL_ROWS:, :] = jnp.zeros((EXP_BM, D_MODEL), BF16)

        def fill(e, wait):
            cnt = pl.multiple_of(fill_cnt_ref[e], 8)
            off = pl.multiple_of(fill_off_ref[e], 8)
            cp = pltpu.make_async_copy(sbuf.at[pl.ds(2 * SEL_ROWS, cnt)], xs_hbm.at[pl.ds(off, cnt)], zsem)

            @pl.when(cnt > 0)
            def _():
                if wait:
                    cp.wait()
                else:
                    cp.start()

        def start_body(e, carry):
            fill(e, False)
            return carry

        def wait_body(e, carry):
            fill(e, True)
            return carry
        lax.fori_loop(0, N_EXPERTS, start_body, 0)
        _wait_runs(local_ref, w, sbuf, slot, xs_hbm, sem, True)
        lax.fori_loop(0, N_EXPERTS, wait_body, 0)


def _staging_shape(extra_rows):
    return jax.ShapeDtypeStruct((2 * SEL_ROWS + extra_rows, D_MODEL), BF16)


def _staging_spec(extra_rows):
    return pl.BlockSpec((2 * SEL_ROWS + extra_rows, D_MODEL), lambda w, *_: (0, 0))


def _dispatch(plan, h2, pos):
    t = h2.shape[0]
    n_win = t // WIN
    return pl.pallas_call(
        functools.partial(_dispatch_kernel, n_win=n_win),
        grid_spec=pltpu.PrefetchScalarGridSpec(
            num_scalar_prefetch=4,
            grid=(n_win,),
            in_specs=[
                pl.BlockSpec((WIN, D_MODEL), lambda w, *_: (w, 0)),
                pl.BlockSpec((TOP_K, WIN), lambda w, *_: (0, w)),
            ],
            out_specs=[pl.BlockSpec(memory_space=pl.ANY), _staging_spec(EXP_BM)],
            scratch_shapes=[
                pltpu.VMEM((SEL_ROWS, WIN), BF16),
                pltpu.SemaphoreType.DMA((2,)),
                pltpu.SemaphoreType.DMA,
            ]),
        out_shape=[jax.ShapeDtypeStruct((_sorted_rows_bound(t), D_MODEL), BF16), _staging_shape(EXP_BM)],
        compiler_params=pltpu.CompilerParams(
            dimension_semantics=("arbitrary",), vmem_limit_bytes=VMEM_LIMIT),
        name="dispatch",
    )(plan['local_off'], plan['global_off'], plan['fill_off'], plan['fill_cnt'], h2, pos)[0]


def _expert_kernel(blk_expert_ref, next_expert_ref, n_used_ref, xs_hbm, wg_hbm, wu_hbm, wd_hbm, ys_hbm,
                   xbuf, ybuf, wg_st, wu_st, wd_st, wg_bf, wu_bf, wd_bf, xsem, ysem, wsem):
    n_used = n_used_ref[0]
    part = EXP_BM // EXP_SPLIT

    def row_copies(b, slot, fetch):
        out = []
        for q in range(EXP_SPLIT):
            hbm_rows = pl.ds(pl.multiple_of(b * EXP_BM + q * part, part), part)
            if fetch:
                out.append(pltpu.make_async_copy(xs_hbm.at[hbm_rows], xbuf.at[slot, q * part:(q + 1) * part],
                                                 xsem.at[slot]))
            else:
                out.append(pltpu.make_async_copy(ybuf.at[slot, q * part:(q + 1) * part], ys_hbm.at[hbm_rows],
                                                 ysem.at[slot]))
        return out

    def weight_copies(e, slot):
        return [pltpu.make_async_copy(wg_hbm.at[e], wg_st.at[slot], wsem.at[slot]),
                pltpu.make_async_copy(wu_hbm.at[e], wu_st.at[slot], wsem.at[slot]),
                pltpu.make_async_copy(wd_hbm.at[e], wd_st.at[slot], wsem.at[slot])]

    def start(copies):
        for c in copies:
            c.start()

    def wait(copies):
        for c in copies:
            c.wait()

    for ahead in range(EXP_XDEPTH - 1):
        @pl.when(ahead < n_used)
        def _(ahead=ahead):
            start(row_copies(ahead, ahead, True))

    @pl.when(n_used > 0)
    def _():
        start(weight_copies(blk_expert_ref[0], 0))

    def body(b, wslot):
        e = blk_expert_ref[b]
        new_expert = jnp.logical_or(b == 0, e != blk_expert_ref[jnp.maximum(b - 1, 0)])

        @pl.when(new_expert)
        def _():
            wait(weight_copies(e, wslot))
            wg_bf[...] = wg_st[wslot].astype(BF16)
            wu_bf[...] = wu_st[wslot].astype(BF16)
            wd_bf[...] = wd_st[wslot].astype(BF16)
            nxt = next_expert_ref[e]

            @pl.when(nxt < N_EXPERTS)
            def _():
                start(weight_copies(nxt, 1 - wslot))

        @pl.when(b + EXP_XDEPTH - 1 < n_used)
        def _():
            start(row_copies(b + EXP_XDEPTH - 1, (b + EXP_XDEPTH - 1) % EXP_XDEPTH, True))

        xslot = b % EXP_XDEPTH
        yslot = b % 2
        wait(row_copies(b, xslot, True))

        @pl.when(b >= 2)
        def _():
            wait(row_copies(b - 2, yslot, False))

        x = xbuf[xslot]
        hg = _dot(x, wg_bf[...])
        hb = hg * jax.nn.sigmoid(hg) * _dot(x, wu_bf[...])
        ybuf[yslot] = _dot(hb.astype(BF16), wd_bf[...]).astype(BF16)
        start(row_copies(b, yslot, False))
        return jnp.where(new_expert, 1 - wslot, wslot)

    lax.fori_loop(0, n_used, body, jnp.int32(0))

    for back in (2, 1):
        @pl.when(n_used >= back)
        def _(back=back):
            wait(row_copies(n_used - back, (n_used - back) % 2, False))


def _experts(plan, xs, w_gate, w_up, w_down):
    any_spec = pl.BlockSpec(memory_space=pl.ANY)
    return pl.pallas_call(
        _expert_kernel,
        grid_spec=pltpu.PrefetchScalarGridSpec(
            num_scalar_prefetch=3,
            grid=(1,),
            in_specs=[any_spec, any_spec, any_spec, any_spec],
            out_specs=any_spec,
            scratch_shapes=[
                pltpu.VMEM((EXP_XDEPTH, EXP_BM, D_MODEL), BF16),
                pltpu.VMEM((2, EXP_BM, D_MODEL), BF16),
                pltpu.VMEM((2, D_MODEL, D_EXPERT), F32),
                pltpu.VMEM((2, D_MODEL, D_EXPERT), F32),
                pltpu.VMEM((2, D_EXPERT, D_MODEL), F32),
                pltpu.VMEM((D_MODEL, D_EXPERT), BF16),
                pltpu.VMEM((D_MODEL, D_EXPERT), BF16),
                pltpu.VMEM((D_EXPERT, D_MODEL), BF16),
                pltpu.SemaphoreType.DMA((EXP_XDEPTH,)),
                pltpu.SemaphoreType.DMA((2,)),
                pltpu.SemaphoreType.DMA((2,)),
            ]),
        out_shape=jax.ShapeDtypeStruct(xs.shape, BF16),
        compiler_params=pltpu.CompilerParams(
            dimension_semantics=("arbitrary",), vmem_limit_bytes=VMEM_LIMIT),
        name="experts",
    )(plan['blk_expert'], plan['next_expert'], plan['n_used'], xs, w_gate, w_up, w_down)


def _combine_kernel(local_ref, global_ref, x1_ref, h2_ref, p_ref, pos_ref, gate_ref, wsg_ref, wsu_ref,
                    wsd_ref, gple_ref, wpg_ref, wp_ref, gfin_ref, ys_hbm, o_ref, ybuf, st_ref, sem,
                    *, n_win, final_norm):
    w = pl.program_id(0)
    slot = w % 2

    @pl.when(w == 0)
    def _():
        ybuf[...] = jnp.zeros(ybuf.shape, BF16)
        _start_runs(local_ref, global_ref, w, ybuf, slot, ys_hbm, sem, False)

    @pl.when(w + 1 < n_win)
    def _():
        _start_runs(local_ref, global_ref, w + 1, ybuf, 1 - slot, ys_hbm, sem, False)

    def build_group(lg):
        cols = slice(lg * CMB_LG, (lg + 1) * CMB_LG)
        for rg in range(WIN // CMB_RG):
            rows = slice(rg * CMB_RG, (rg + 1) * CMB_RG)
            cid = lg * CMB_LG + lax.broadcasted_iota(jnp.int32, (CMB_RG, CMB_LG), 1)
            acc = jnp.zeros((CMB_RG, CMB_LG), F32)
            for k in range(TOP_K):
                acc = jnp.where(cid == pos_ref[rows, k:k + 1], gate_ref[rows, k:k + 1], acc)
            st_ref[rows, cols] = acc.astype(BF16)

    build_group(0)
    h2 = h2_ref[...]
    hs = _dot(h2, wsg_ref[...])
    hs = hs * jax.nn.sigmoid(hs) * _dot(h2, wsu_ref[...])
    shared = _dot(hs.astype(BF16), wsd_ref[...])

    _wait_runs(local_ref, w, ybuf, slot, ys_hbm, sem, False)
    routed = None
    n_groups = SEL_ROWS // CMB_LG
    for lg in range(n_groups):
        if lg + 1 < n_groups:
            build_group(lg + 1)
        src = pl.multiple_of(slot * SEL_ROWS + lg * CMB_LG, CMB_LG)
        part = _dot(st_ref[:, lg * CMB_LG:(lg + 1) * CMB_LG], ybuf[pl.ds(src, CMB_LG), :])
        routed = part if routed is None else routed + part
    x2 = x1_ref[...] + routed + shared

    hp = _rms(x2, gple_ref[...]).astype(BF16)
    gate = jax.nn.sigmoid(_dot(hp, wpg_ref[...]))
    x3 = x2 + gate * _dot(p_ref[...].astype(BF16), wp_ref[...])
    o_ref[...] = _rms(x3, gfin_ref[...]) if final_norm else x3


def _combine(plan, ys, x1, h2, p, pos_tm, gate_tm, wsg, wsu, wsd, g_ple, w_pg, w_p, g_fin, final_norm):
    t = x1.shape[0]
    n_win = t // WIN
    row = lambda width: pl.BlockSpec((WIN, width), lambda w, *_: (w, 0))
    const = lambda shape: pl.BlockSpec(shape, lambda w, *_: (0,) * len(shape))
    return pl.pallas_call(
        functools.partial(_combine_kernel, n_win=n_win, final_norm=final_norm),
        grid_spec=pltpu.PrefetchScalarGridSpec(
            num_scalar_prefetch=2,
            grid=(n_win,),
            in_specs=[
                row(D_MODEL), row(D_MODEL), row(PLE_DIM), row(TOP_K), row(TOP_K),
                const((D_MODEL, D_EXPERT)), const((D_MODEL, D_EXPERT)), const((D_EXPERT, D_MODEL)),
                const((1, D_MODEL)), const((D_MODEL, D_MODEL)), const((PLE_DIM, D_MODEL)),
                const((1, D_MODEL)),
                pl.BlockSpec(memory_space=pl.ANY),
            ],
            out_specs=[row(D_MODEL), _staging_spec(0)],
            scratch_shapes=[
                pltpu.VMEM((WIN, SEL_ROWS), BF16),
                pltpu.SemaphoreType.DMA((2,)),
            ]),
        out_shape=[jax.ShapeDtypeStruct((t, D_MODEL), F32), _staging_shape(0)],
        compiler_params=pltpu.CompilerParams(
            dimension_semantics=("arbitrary",), vmem_limit_bytes=VMEM_LIMIT),
        name="combine",
    )(plan['local_off'], plan['global_off'], x1, h2, p, pos_tm, gate_tm, wsg, wsu, wsd, g_ple, w_pg, w_p,
      g_fin, ys)[0]


def kernel(x, p, g_mix, w_in, b_in, w_dw, b_dw, g_cln, b_cln, w_conv_out, b_conv_out, w_pool, s_pool,
           w_out, g_ffn, w_router, b_router, w_e_gate, w_e_up, w_e_down, w_s_gate, w_s_up, w_s_down,
           g_ple, w_ple_gate, w_ple, g_final):
    bsz, s, d = x.shape
    t = bsz * s
    depth = w_in.shape[0]
    xt = x.reshape(t, d)
    row = lambda v: v.reshape(1, -1)
    for i in range(depth):
        x1, h2 = _mixer(
            xt, s, row(g_mix[i]), w_in[i].astype(BF16), row(b_in[i]), w_dw[i], row(b_dw[i]),
            row(g_cln[i]), row(b_cln[i]), w_conv_out[i].astype(BF16), row(b_conv_out[i]),
            w_pool[i].astype(BF16), row(s_pool[i]), w_out[i].astype(BF16), row(g_ffn[i]))
        gate, pos, cnt = _router(h2, w_router[i].T.astype(BF16), b_router[i].reshape(N_EXPERTS, 1))
        plan = _dispatch_plan(cnt, t)
        xs = _dispatch(plan, h2, pos)
        ys = _experts(plan, xs, w_e_gate[i], w_e_up[i], w_e_down[i])
        xt = _combine(
            plan, ys, x1, h2, p[i].reshape(t, PLE_DIM), pos.T, gate.T,
            w_s_gate[i].astype(BF16), w_s_up[i].astype(BF16), w_s_down[i].astype(BF16),
            row(g_ple[i]), w_ple_gate[i].astype(BF16), w_ple[i].astype(BF16), row(g_final),
            final_norm=(i == depth - 1))
    return xt.reshape(bsz, s, d)
```

```python
import functools

import jax
import jax.numpy as jnp
from jax import lax
from jax.experimental import pallas as pl
from jax.experimental.pallas import tpu as pltpu

D_MODEL = 1024
D_CONV = 1024
D_POOL = 1024
CONV_WIDTH = 31
POOL_WINDOWS = (2, 4, 8, 16)
POOL_GROUP = 256
PLE_DIM = 256
N_EXPERTS = 64
N_GROUPS = 8
GROUP_SIZE = N_EXPERTS // N_GROUPS
TOPK_GROUPS = 4
TOP_K = 8
D_EXPERT = 256
ROUTED_SCALE = 2.5
NORM_EPS = 1e-6

F32 = jnp.float32
BF16 = jnp.bfloat16

MIX_TM = 256
MIX_NV = MIX_TM // 8
CONV_MG = 8
ROW_CHUNK = 64
LANE = 128

ROUTER_TM = 1024
WIN = 256
SEL_ROWS = 2560
SEL_RG = 64
SEL_MM = 512
EXP_BM = 512
EXP_XDEPTH = 3
EXP_SPLIT = 4
CMB_LG = 512

VMEM_LIMIT = 56 * 1024 * 1024


def _rms(x, g):
    ms = jnp.mean(x * x, axis=-1, keepdims=True)
    return x * lax.rsqrt(ms + NORM_EPS) * g


def _dot(a, b):
    return jnp.dot(a, b, preferred_element_type=F32)


def _mixer_kernel(x_ref, gmix_ref, win_ref, bin_ref, wdw_ref, bdw_ref, gcln_ref, bcln_ref,
                  wco_ref, bco_ref, wpool_ref, spool_ref, wout_ref, gffn_ref, perm_ref, unperm_ref,
                  x1_ref, h2_ref, a_ext, a_prev, u_ext, u_prev, c_buf, q_buf, *, tiles_per_seq):
    i = pl.program_id(0) % tiles_per_seq
    tm = MIX_TM
    nv = MIX_NV

    @pl.when(i == 0)
    def _():
        a_prev[...] = jnp.zeros(a_prev.shape, F32)
        u_prev[...] = jnp.zeros(u_prev.shape, F32)

    x = x_ref[...]
    h = _dot(perm_ref[...], _rms(x, gmix_ref[...]).astype(BF16)).astype(BF16)

    def proj(lo, hi):
        return _dot(h, win_ref[:, lo:hi]) + bin_ref[:, lo:hi]

    glu = proj(0, D_CONV) * jax.nn.sigmoid(proj(D_CONV, 2 * D_CONV))
    for lc in range(D_CONV // LANE):
        a_ext[lc, tm:2 * tm, :] = glu[:, lc * LANE:(lc + 1) * LANE]
    u_ext[tm:2 * tm, :] = proj(2 * D_CONV, 2 * D_CONV + D_POOL)

    def delayed_groups(ext, prev, first_group):
        last_row = lax.broadcasted_iota(jnp.int32, (8, ext.shape[-1]), 0) == 7
        for g in range(first_group, nv):
            rows = slice(8 * g, 8 * g + 8)
            mixed = jnp.where(last_row, prev[rows, :], ext[tm + 8 * g:tm + 8 * g + 8, :])
            ext[rows, :] = pltpu.roll(mixed, 1, axis=0)
            prev[rows, :] = ext[tm + 8 * g:tm + 8 * g + 8, :]

    delayed_groups(u_ext, u_prev, nv - (max(POOL_WINDOWS) - 1))

    def conv_column(lc, carry):
        a_col = a_ext.at[lc]
        delayed_groups(a_col, a_prev.at[lc], nv - (CONV_WIDTH - 1))
        w_col = wdw_ref.at[lc]
        for g0 in range(0, nv, CONV_MG):
            acc = None
            for k in range(CONV_WIDTH):
                src = nv + g0 + k - (CONV_WIDTH - 1)
                term = a_col[8 * src:8 * (src + CONV_MG), :] * w_col[k:k + 1, :]
                acc = term if acc is None else acc + term
            c_buf[lc, 8 * g0:8 * (g0 + CONV_MG), :] = acc + bdw_ref[lc]
        return carry
    lax.fori_loop(0, D_CONV // LANE, conv_column, 0)

    c = jnp.concatenate([c_buf[lc] for lc in range(D_CONV // LANE)], axis=-1)
    mu = jnp.mean(c, axis=-1, keepdims=True)
    xc = c - mu
    var = jnp.mean(xc * xc, axis=-1, keepdims=True)
    y = xc * lax.rsqrt(var + NORM_EPS) * gcln_ref[...] + bcln_ref[...]
    y = y * jax.nn.sigmoid(y)
    branch_a = _dot(y.astype(BF16), wco_ref[...]) + bco_ref[...]

    for r0 in range(0, tm, ROW_CHUNK):
        row = r0 + lax.broadcasted_iota(jnp.int32, (ROW_CHUNK, POOL_GROUP), 0)
        t1 = i * tm + (row % 8) * nv + row // 8 + 1
        for gi, w in enumerate(POOL_WINDOWS):
            ls = slice(gi * POOL_GROUP, (gi + 1) * POOL_GROUP)
            tok = u_ext[tm + r0:tm + r0 + ROW_CHUNK, ls]
            s = tok
            for j in range(1, w):
                s = s + u_ext[tm + r0 - 8 * j:tm + r0 - 8 * j + ROW_CHUNK, ls]
            cnt = jnp.minimum(t1, w).astype(F32)
            q_buf[r0:r0 + ROW_CHUNK, ls] = s / cnt - tok

    qs_out = []
    for gi in range(len(POOL_WINDOWS)):
        ls = slice(gi * POOL_GROUP, (gi + 1) * POOL_GROUP)
        qs_out.append(_dot(q_buf[:, ls].astype(BF16), wpool_ref[gi]) * spool_ref[:, ls])
    branch_b = jnp.concatenate(qs_out, axis=-1)

    c2 = 2 * D_CONV + D_POOL
    gate_a = jax.nn.sigmoid(proj(c2, c2 + D_MODEL))
    gate_b = jax.nn.sigmoid(proj(c2 + D_MODEL, c2 + 2 * D_MODEL))
    merged = gate_a * branch_a + gate_b * branch_b
    merged = _dot(unperm_ref[...], merged.astype(BF16)).astype(BF16)
    x1 = x + _dot(merged, wout_ref[...])
    x1_ref[...] = x1
    h2_ref[...] = _rms(x1, gffn_ref[...]).astype(BF16)


def _const_spec(shape):
    n = len(shape)
    return pl.BlockSpec(shape, lambda i, _n=n: (0,) * _n)


def _mixer(x, seq_len, g_mix, w_in, b_in, w_dw, b_dw, g_cln, b_cln, w_co, b_co, w_pool, s_pool, w_out,
           g_ffn):
    t = x.shape[0]
    tm = MIX_TM
    assert seq_len % tm == 0 and MIX_NV >= CONV_WIDTH and MIX_NV >= max(POOL_WINDOWS)
    d_in = w_in.shape[1]
    row = pl.BlockSpec((tm, D_MODEL), lambda i: (i, 0))
    n_col = D_CONV // LANE
    w_dw = w_dw.reshape(CONV_WIDTH, n_col, LANE).transpose(1, 0, 2)
    b_dw = b_dw.reshape(n_col, 1, LANE)
    r = jnp.arange(tm)
    perm = ((r % 8) * MIX_NV + r // 8)[:, None] == jnp.arange(tm)[None, :]
    perm = perm.astype(BF16)
    return pl.pallas_call(
        functools.partial(_mixer_kernel, tiles_per_seq=seq_len // tm),
        grid=(t // tm,),
        in_specs=[
            row,
            _const_spec((1, D_MODEL)),
            _const_spec((D_MODEL, d_in)),
            _const_spec((1, d_in)),
            _const_spec((n_col, CONV_WIDTH, LANE)),
            _const_spec((n_col, 1, LANE)),
            _const_spec((1, D_CONV)),
            _const_spec((1, D_CONV)),
            _const_spec((D_CONV, D_MODEL)),
            _const_spec((1, D_MODEL)),
            _const_spec((len(POOL_WINDOWS), POOL_GROUP, POOL_GROUP)),
            _const_spec((1, D_POOL)),
            _const_spec((D_MODEL, D_MODEL)),
            _const_spec((1, D_MODEL)),
            _const_spec((tm, tm)),
            _const_spec((tm, tm)),
        ],
        out_specs=[row, row],
        out_shape=[jax.ShapeDtypeStruct((t, D_MODEL), F32),
                   jax.ShapeDtypeStruct((t, D_MODEL), BF16)],
        scratch_shapes=[
            pltpu.VMEM((n_col, 2 * tm, LANE), F32),
            pltpu.VMEM((n_col, tm, LANE), F32),
            pltpu.VMEM((2 * tm, D_POOL), F32),
            pltpu.VMEM((tm, D_POOL), F32),
            pltpu.VMEM((n_col, tm, LANE), F32),
            pltpu.VMEM((tm, D_POOL), F32),
        ],
        compiler_params=pltpu.CompilerParams(
            dimension_semantics=("arbitrary",), vmem_limit_bytes=VMEM_LIMIT),
        name="mixer",
    )(x, g_mix, w_in, b_in, w_dw, b_dw, g_cln, b_cln, w_co, b_co, w_pool, s_pool, w_out, g_ffn, perm, perm.T)


def _beats(v, other, other_is_later):
    v = jnp.broadcast_to(v, other.shape)
    return jnp.where(other_is_later, jnp.where(v >= other, 1, 0), jnp.where(v > other, 1, 0))


def _router_kernel(h2_ref, wrt_ref, br_ref, utri_ref, ltri_ref, gate_ref, rank_ref, pos_ref, cnt_ref):
    tm = ROUTER_TM
    logits = lax.dot_general(wrt_ref[...], h2_ref[...], (((1,), (1,)), ((), ())),
                             preferred_element_type=F32)
    scores = jax.nn.sigmoid(logits)
    sel = scores + br_ref[...]
    shape3 = (N_GROUPS, GROUP_SIZE, tm)
    sel3 = sel.reshape(shape3)
    scores3 = scores.reshape(shape3)
    neg_inf = jnp.float32(-jnp.inf)

    member = lax.broadcasted_iota(jnp.int32, shape3, 1)
    m1 = jnp.max(sel3, axis=1, keepdims=True)
    first = jnp.min(jnp.where(sel3 == m1, member, GROUP_SIZE), axis=1, keepdims=True)
    m2 = jnp.max(jnp.where(member == first, neg_inf, sel3), axis=1, keepdims=True)
    gscore = jnp.broadcast_to(m1 + m2, shape3)

    gidx = lax.broadcasted_iota(jnp.int32, shape3, 0)
    grank = jnp.zeros(shape3, jnp.int32)
    for j in range(N_GROUPS):
        sj = gscore[j:j + 1]
        grank = grank + _beats(sj, gscore, gidx > j)
    masked = jnp.where(grank < TOPK_GROUPS, sel3, neg_inf)

    eidx = gidx * GROUP_SIZE + member
    erank = jnp.zeros(shape3, jnp.int32)
    for gj in range(N_GROUPS):
        for mj in range(GROUP_SIZE):
            v = masked[gj:gj + 1, mj:mj + 1, :]
            erank = erank + _beats(v, masked, eidx > gj * GROUP_SIZE + mj)
    chosen = erank < TOP_K
    top_s = jnp.where(chosen, scores3, 0.0)
    denom = jnp.sum(jnp.sum(top_s, axis=1, keepdims=True), axis=0, keepdims=True)
    gates3 = top_s / denom * ROUTED_SCALE
    chosen2 = jnp.where(chosen, 1.0, 0.0).reshape(N_EXPERTS, tm)
    gate_ref[...] = gates3.reshape(N_EXPERTS, tm).astype(BF16)

    for w in range(tm // WIN):
        ls = slice(w * WIN, (w + 1) * WIN)
        mw = chosen2[:, ls]
        rank = _dot(mw.astype(BF16), utri_ref[...])
        n = jnp.sum(mw, axis=1, keepdims=True)
        run = jnp.floor((n + 7.0) * 0.125) * 8.0
        start = _dot(ltri_ref[...], jnp.broadcast_to(run, (N_EXPERTS, WIN)).astype(BF16))
        rank_ref[:, ls] = jnp.where(mw > 0.5, rank, -1.0).astype(BF16)
        row3 = (rank + start).reshape(N_GROUPS, GROUP_SIZE, WIN)
        er = erank[:, :, ls]
        for k in range(TOP_K):
            pk = jnp.sum(jnp.sum(jnp.where(er == k, row3, 0.0), axis=1, keepdims=True), axis=0, keepdims=True)
            pos_ref[k:k + 1, ls] = pk.reshape(1, WIN).astype(jnp.int32)
        cnt_ref[w] = n


def _router(h2, w_rt, b_r):
    t = h2.shape[0]
    tm = ROUTER_TM
    utri = jnp.triu(jnp.ones((WIN, WIN), BF16), k=1)
    ltri = jnp.tril(jnp.ones((N_EXPERTS, N_EXPERTS), BF16), k=-1)
    return pl.pallas_call(
        _router_kernel,
        grid=(t // tm,),
        in_specs=[
            pl.BlockSpec((tm, D_MODEL), lambda i: (i, 0)),
            _const_spec((N_EXPERTS, D_MODEL)),
            _const_spec((N_EXPERTS, 1)),
            _const_spec((WIN, WIN)),
            _const_spec((N_EXPERTS, N_EXPERTS)),
        ],
        out_specs=[
            pl.BlockSpec((N_EXPERTS, tm), lambda i: (0, i)),
            pl.BlockSpec((N_EXPERTS, tm), lambda i: (0, i)),
            pl.BlockSpec((TOP_K, tm), lambda i: (0, i)),
            pl.BlockSpec((tm // WIN, N_EXPERTS, 1), lambda i: (i, 0, 0)),
        ],
        out_shape=[
            jax.ShapeDtypeStruct((N_EXPERTS, t), BF16),
            jax.ShapeDtypeStruct((N_EXPERTS, t), BF16),
            jax.ShapeDtypeStruct((TOP_K, t), jnp.int32),
            jax.ShapeDtypeStruct((t // WIN, N_EXPERTS, 1), F32),
        ],
        compiler_params=pltpu.CompilerParams(
            dimension_semantics=("arbitrary",), vmem_limit_bytes=VMEM_LIMIT),
        name="router",
    )(h2, w_rt, b_r, utri, ltri)


def _sorted_rows_bound(t):
    rows = t * TOP_K + (t // WIN) * N_EXPERTS * 7 + N_EXPERTS * (EXP_BM - 1)
    return -(-rows // EXP_BM) * EXP_BM


def _dispatch_plan(cnt, t):
    nw = t // WIN
    n = cnt.reshape(nw, N_EXPERTS).astype(jnp.int32)
    run = (n + 7) // 8 * 8
    local_end = jnp.cumsum(run, axis=1)
    local_off = jnp.concatenate([jnp.zeros((nw, 1), jnp.int32), local_end], axis=1)
    total = jnp.sum(run, axis=0)
    region = (total + EXP_BM - 1) // EXP_BM * EXP_BM
    region_end = jnp.cumsum(region)
    base = region_end - region
    global_off = base[None, :] + jnp.cumsum(run, axis=0) - run
    n_blocks = _sorted_rows_bound(t) // EXP_BM
    n_used = region_end[-1] // EXP_BM
    blk = jnp.arange(n_blocks, dtype=jnp.int32)
    blk_expert = jnp.sum((region_end[None, :] <= blk[:, None] * EXP_BM).astype(jnp.int32), axis=1)
    blk_expert = jnp.minimum(blk_expert, N_EXPERTS - 1)
    eid = jnp.arange(N_EXPERTS, dtype=jnp.int32)
    later_nonempty = (eid[None, :] > eid[:, None]) & (region[None, :] > 0)
    next_expert = jnp.min(jnp.where(later_nonempty, eid[None, :], N_EXPERTS), axis=1).astype(jnp.int32)
    return dict(
        run_lo=local_off[:, :N_EXPERTS].reshape(nw, N_EXPERTS, 1),
        run_hi=local_off[:, 1:].reshape(nw, N_EXPERTS, 1),
        local_off=local_off.reshape(-1), global_off=global_off.reshape(-1),
        fill_off=base + total, fill_cnt=region - total,
        blk_expert=blk_expert.astype(jnp.int32), next_expert=next_expert,
        n_used=n_used.reshape(1).astype(jnp.int32))


def _run_copy(local_ref, global_ref, win, e, vmem_buf, slot, hbm_buf, sem, to_hbm):
    lo = pl.multiple_of(local_ref[win * (N_EXPERTS + 1) + e], 8)
    cnt = pl.multiple_of(local_ref[win * (N_EXPERTS + 1) + e + 1] - lo, 8)
    go = pl.multiple_of(global_ref[win * N_EXPERTS + e], 8)
    v = vmem_buf.at[pl.ds(pl.multiple_of(slot * SEL_ROWS + lo, 8), cnt)]
    h = hbm_buf.at[pl.ds(go, cnt)]
    cp = pltpu.make_async_copy(v, h, sem.at[slot]) if to_hbm else pltpu.make_async_copy(h, v, sem.at[slot])
    return cnt, cp


def _start_runs(local_ref, global_ref, win, vmem_buf, slot, hbm_buf, sem, to_hbm):
    def body(e, carry):
        cnt, cp = _run_copy(local_ref, global_ref, win, e, vmem_buf, slot, hbm_buf, sem, to_hbm)

        @pl.when(cnt > 0)
        def _():
            cp.start()
        return carry
    lax.fori_loop(0, N_EXPERTS, body, 0)


def _wait_runs(local_ref, win, vmem_buf, slot, hbm_buf, sem, to_hbm):
    total = pl.multiple_of(local_ref[win * (N_EXPERTS + 1) + N_EXPERTS], 8)
    v = vmem_buf.at[pl.ds(pl.multiple_of(slot * SEL_ROWS, 8), total)]
    h = hbm_buf.at[pl.ds(0, total)]
    cp = pltpu.make_async_copy(v, h, sem.at[slot]) if to_hbm else pltpu.make_async_copy(h, v, sem.at[slot])

    @pl.when(total > 0)
    def _():
        cp.wait()


def _dispatch_kernel(local_ref, global_ref, fill_off_ref, fill_cnt_ref, h2_ref, pos_ref, xs_hbm, sbuf,
                     s_ref, sem, zsem, *, n_win):
    w = pl.program_id(0)
    slot = w % 2
    pos = pos_ref[...]

    h2 = h2_ref[...]
    for g in range(SEL_ROWS // SEL_MM):
        for sg in range(SEL_MM // SEL_RG):
            r0 = g * SEL_MM + sg * SEL_RG
            rid = r0 + lax.broadcasted_iota(jnp.int32, (SEL_RG, WIN), 0)
            acc = jnp.zeros((SEL_RG, WIN), F32)
            for k in range(TOP_K):
                acc = jnp.where(rid == pos[k:k + 1, :], 1.0, acc)
            s_ref[r0:r0 + SEL_RG, :] = acc.astype(BF16)
        rows = slice(g * SEL_MM, (g + 1) * SEL_MM)
        dst = pl.multiple_of(slot * SEL_ROWS + g * SEL_MM, SEL_MM)
        sbuf[pl.ds(dst, SEL_MM), :] = _dot(s_ref[rows, :], h2).astype(BF16)

    _start_runs(local_ref, global_ref, w, sbuf, slot, xs_hbm, sem, True)

    @pl.when(w > 0)
    def _():
        _wait_runs(local_ref, w - 1, sbuf, 1 - slot, xs_hbm, sem, True)

    @pl.when(w == n_win - 1)
    def _():
        sbuf[2 * SEL_ROWS:, :] = jnp.zeros((EXP_BM, D_MODEL), BF16)

        def fill(e, wait):
            cnt = pl.multiple_of(fill_cnt_ref[e], 8)
            off = pl.multiple_of(fill_off_ref[e], 8)
            cp = pltpu.make_async_copy(sbuf.at[pl.ds(2 * SEL_ROWS, cnt)], xs_hbm.at[pl.ds(off, cnt)], zsem)

            @pl.when(cnt > 0)
            def _():
                if wait:
                    cp.wait()
                else:
                    cp.start()

        def start_body(e, carry):
            fill(e, False)
            return carry

        def wait_body(e, carry):
            fill(e, True)
            return carry
        lax.fori_loop(0, N_EXPERTS, start_body, 0)
        _wait_runs(local_ref, w, sbuf, slot, xs_hbm, sem, True)
        lax.fori_loop(0, N_EXPERTS, wait_body, 0)


def _staging_shape(extra_rows):
    return jax.ShapeDtypeStruct((2 * SEL_ROWS + extra_rows, D_MODEL), BF16)


def _staging_spec(extra_rows):
    return pl.BlockSpec((2 * SEL_ROWS + extra_rows, D_MODEL), lambda w, *_: (0, 0))


def _dispatch(plan, h2, pos):
    t = h2.shape[0]
    n_win = t // WIN
    return pl.pallas_call(
        functools.partial(_dispatch_kernel, n_win=n_win),
        grid_spec=pltpu.PrefetchScalarGridSpec(
            num_scalar_prefetch=4,
            grid=(n_win,),
            in_specs=[
                pl.BlockSpec((WIN, D_MODEL), lambda w, *_: (w, 0)),
                pl.BlockSpec((TOP_K, WIN), lambda w, *_: (0, w)),
            ],
            out_specs=[pl.BlockSpec(memory_space=pl.ANY), _staging_spec(EXP_BM)],
            scratch_shapes=[
                pltpu.VMEM((SEL_ROWS, WIN), BF16),
                pltpu.SemaphoreType.DMA((2,)),
                pltpu.SemaphoreType.DMA,
            ]),
        out_shape=[jax.ShapeDtypeStruct((_sorted_rows_bound(t), D_MODEL), BF16), _staging_shape(EXP_BM)],
        compiler_params=pltpu.CompilerParams(
            dimension_semantics=("arbitrary",), vmem_limit_bytes=VMEM_LIMIT),
        name="dispatch",
    )(plan['local_off'], plan['global_off'], plan['fill_off'], plan['fill_cnt'], h2, pos)[0]


def _expert_kernel(blk_expert_ref, next_expert_ref, n_used_ref, xs_hbm, wg_hbm, wu_hbm, wd_hbm, ys_hbm,
                   xbuf, ybuf, wg_st, wu_st, wd_st, wg_bf, wu_bf, wd_bf, xsem, ysem, wsem):
    n_used = n_used_ref[0]
    part = EXP_BM // EXP_SPLIT

    def row_copies(b, slot, fetch):
        out = []
        for q in range(EXP_SPLIT):
            hbm_rows = pl.ds(pl.multiple_of(b * EXP_BM + q * part, part), part)
            if fetch:
                out.append(pltpu.make_async_copy(xs_hbm.at[hbm_rows], xbuf.at[slot, q * part:(q + 1) * part],
                                                 xsem.at[slot]))
            else:
                out.append(pltpu.make_async_copy(ybuf.at[slot, q * part:(q + 1) * part], ys_hbm.at[hbm_rows],
                                                 ysem.at[slot]))
        return out

    def weight_copies(e, slot):
        return [pltpu.make_async_copy(wg_hbm.at[e], wg_st.at[slot], wsem.at[slot]),
                pltpu.make_async_copy(wu_hbm.at[e], wu_st.at[slot], wsem.at[slot]),
                pltpu.make_async_copy(wd_hbm.at[e], wd_st.at[slot], wsem.at[slot])]

    def start(copies):
        for c in copies:
            c.start()

    def wait(copies):
        for c in copies:
            c.wait()

    for ahead in range(EXP_XDEPTH - 1):
        @pl.when(ahead < n_used)
        def _(ahead=ahead):
            start(row_copies(ahead, ahead, True))

    @pl.when(n_used > 0)
    def _():
        start(weight_copies(blk_expert_ref[0], 0))

    def body(b, wslot):
        e = blk_expert_ref[b]
        new_expert = jnp.logical_or(b == 0, e != blk_expert_ref[jnp.maximum(b - 1, 0)])

        @pl.when(new_expert)
        def _():
            wait(weight_copies(e, wslot))
            wg_bf[...] = wg_st[wslot].astype(BF16)
            wu_bf[...] = wu_st[wslot].astype(BF16)
            wd_bf[...] = wd_st[wslot].astype(BF16)
            nxt = next_expert_ref[e]

            @pl.when(nxt < N_EXPERTS)
            def _():
                start(weight_copies(nxt, 1 - wslot))

        @pl.when(b + EXP_XDEPTH - 1 < n_used)
        def _():
            start(row_copies(b + EXP_XDEPTH - 1, (b + EXP_XDEPTH - 1) % EXP_XDEPTH, True))

        xslot = b % EXP_XDEPTH
        yslot = b % 2
        wait(row_copies(b, xslot, True))

        @pl.when(b >= 2)
        def _():
            wait(row_copies(b - 2, yslot, False))

        x = xbuf[xslot]
        hg = _dot(x, wg_bf[...])
        hb = hg * jax.nn.sigmoid(hg) * _dot(x, wu_bf[...])
        ybuf[yslot] = _dot(hb.astype(BF16), wd_bf[...]).astype(BF16)
        start(row_copies(b, yslot, False))
        return jnp.where(new_expert, 1 - wslot, wslot)

    lax.fori_loop(0, n_used, body, jnp.int32(0))

    for back in (2, 1):
        @pl.when(n_used >= back)
        def _(back=back):
            wait(row_copies(n_used - back, (n_used - back) % 2, False))


def _experts(plan, xs, w_gate, w_up, w_down):
    any_spec = pl.BlockSpec(memory_space=pl.ANY)
    return pl.pallas_call(
        _expert_kernel,
        grid_spec=pltpu.PrefetchScalarGridSpec(
            num_scalar_prefetch=3,
            grid=(1,),
            in_specs=[any_spec, any_spec, any_spec, any_spec],
            out_specs=any_spec,
            scratch_shapes=[
                pltpu.VMEM((EXP_XDEPTH, EXP_BM, D_MODEL), BF16),
                pltpu.VMEM((2, EXP_BM, D_MODEL), BF16),
                pltpu.VMEM((2, D_MODEL, D_EXPERT), F32),
                pltpu.VMEM((2, D_MODEL, D_EXPERT), F32),
                pltpu.VMEM((2, D_EXPERT, D_MODEL), F32),
                pltpu.VMEM((D_MODEL, D_EXPERT), BF16),
                pltpu.VMEM((D_MODEL, D_EXPERT), BF16),
                pltpu.VMEM((D_EXPERT, D_MODEL), BF16),
                pltpu.SemaphoreType.DMA((EXP_XDEPTH,)),
                pltpu.SemaphoreType.DMA((2,)),
                pltpu.SemaphoreType.DMA((2,)),
            ]),
        out_shape=jax.ShapeDtypeStruct(xs.shape, BF16),
        compiler_params=pltpu.CompilerParams(
            dimension_semantics=("arbitrary",), vmem_limit_bytes=VMEM_LIMIT),
        name="experts",
    )(plan['blk_expert'], plan['next_expert'], plan['n_used'], xs, w_gate, w_up, w_down)


def _combine_kernel(local_ref, global_ref, x1_ref, h2_ref, p_ref, rank_ref, gate_ref, lo_ref, hi_ref,
                    wsg_ref, wsu_ref, wsd_ref, gple_ref, wpg_ref, wp_ref, gfin_ref, ys_hbm, o_ref, ybuf, st_ref, sem,
                    *, n_win, final_norm):
    w = pl.program_id(0)
    slot = w % 2

    @pl.when(w == 0)
    def _():
        ybuf[...] = jnp.zeros(ybuf.shape, BF16)
        _start_runs(local_ref, global_ref, w, ybuf, slot, ys_hbm, sem, False)

    @pl.when(w + 1 < n_win)
    def _():
        _start_runs(local_ref, global_ref, w + 1, ybuf, 1 - slot, ys_hbm, sem, False)

    lo = lo_ref[0]
    hi = hi_ref[0]
    lo_f = lo.astype(F32)
    rank_tm = rank_ref[...]
    gate_tm = gate_ref[...]

    def build_group(lg):
        cols = slice(lg * CMB_LG, (lg + 1) * CMB_LG)
        rid = lg * CMB_LG + lax.broadcasted_iota(jnp.int32, (N_EXPERTS, CMB_LG), 1)
        owner = jnp.where(rid >= lo, jnp.where(rid < hi, 1.0, 0.0), 0.0)
        run_row = rid[0:1, :].astype(F32) - jnp.sum(owner * lo_f, axis=0, keepdims=True)
        owner = owner.astype(BF16)
        hit = _dot(rank_tm, owner) == run_row
        st_ref[:, cols] = jnp.where(hit, _dot(gate_tm, owner), 0.0).astype(BF16)

    build_group(0)
    h2 = h2_ref[...]
    hs = _dot(h2, wsg_ref[...])
    hs = hs * jax.nn.sigmoid(hs) * _dot(h2, wsu_ref[...])
    shared = _dot(hs.astype(BF16), wsd_ref[...])

    _wait_runs(local_ref, w, ybuf, slot, ys_hbm, sem, False)
    routed = None
    n_groups = SEL_ROWS // CMB_LG
    for lg in range(n_groups):
        if lg + 1 < n_groups:
            build_group(lg + 1)
        src = pl.multiple_of(slot * SEL_ROWS + lg * CMB_LG, CMB_LG)
        part = _dot(st_ref[:, lg * CMB_LG:(lg + 1) * CMB_LG], ybuf[pl.ds(src, CMB_LG), :])
        routed = part if routed is None else routed + part
    x2 = x1_ref[...] + routed + shared

    hp = _rms(x2, gple_ref[...]).astype(BF16)
    gate = jax.nn.sigmoid(_dot(hp, wpg_ref[...]))
    x3 = x2 + gate * _dot(p_ref[...].astype(BF16), wp_ref[...])
    o_ref[...] = _rms(x3, gfin_ref[...]) if final_norm else x3


def _combine(plan, ys, x1, h2, p, rank_tm, gate_tm, wsg, wsu, wsd, g_ple, w_pg, w_p, g_fin, final_norm):
    t = x1.shape[0]
    n_win = t // WIN
    row = lambda width: pl.BlockSpec((WIN, width), lambda w, *_: (w, 0))
    const = lambda shape: pl.BlockSpec(shape, lambda w, *_: (0,) * len(shape))
    return pl.pallas_call(
        functools.partial(_combine_kernel, n_win=n_win, final_norm=final_norm),
        grid_spec=pltpu.PrefetchScalarGridSpec(
            num_scalar_prefetch=2,
            grid=(n_win,),
            in_specs=[
                row(D_MODEL), row(D_MODEL), row(PLE_DIM), row(N_EXPERTS), row(N_EXPERTS),
                pl.BlockSpec((1, N_EXPERTS, 1), lambda w, *_: (w, 0, 0)),
                pl.BlockSpec((1, N_EXPERTS, 1), lambda w, *_: (w, 0, 0)),
                const((D_MODEL, D_EXPERT)), const((D_MODEL, D_EXPERT)), const((D_EXPERT, D_MODEL)),
                const((1, D_MODEL)), const((D_MODEL, D_MODEL)), const((PLE_DIM, D_MODEL)),
                const((1, D_MODEL)),
                pl.BlockSpec(memory_space=pl.ANY),
            ],
            out_specs=[row(D_MODEL), _staging_spec(0)],
            scratch_shapes=[
                pltpu.VMEM((WIN, SEL_ROWS), BF16),
                pltpu.SemaphoreType.DMA((2,)),
            ]),
        out_shape=[jax.ShapeDtypeStruct((t, D_MODEL), F32), _staging_shape(0)],
        compiler_params=pltpu.CompilerParams(
            dimension_semantics=("arbitrary",), vmem_limit_bytes=VMEM_LIMIT),
        name="combine",
    )(plan['local_off'], plan['global_off'], x1, h2, p, rank_tm, gate_tm, plan['run_lo'], plan['run_hi'],
      wsg, wsu, wsd, g_ple, w_pg, w_p, g_fin, ys)[0]


def kernel(x, p, g_mix, w_in, b_in, w_dw, b_dw, g_cln, b_cln, w_conv_out, b_conv_out, w_pool, s_pool,
           w_out, g_ffn, w_router, b_router, w_e_gate, w_e_up, w_e_down, w_s_gate, w_s_up, w_s_down,
           g_ple, w_ple_gate, w_ple, g_final):
    bsz, s, d = x.shape
    t = bsz * s
    depth = w_in.shape[0]
    xt = x.reshape(t, d)
    row = lambda v: v.reshape(1, -1)
    for i in range(depth):
        x1, h2 = _mixer(
            xt, s, row(g_mix[i]), w_in[i].astype(BF16), row(b_in[i]), w_dw[i], row(b_dw[i]),
            row(g_cln[i]), row(b_cln[i]), w_conv_out[i].astype(BF16), row(b_conv_out[i]),
            w_pool[i].astype(BF16), row(s_pool[i]), w_out[i].astype(BF16), row(g_ffn[i]))
        gate, rank, pos, cnt = _router(h2, w_router[i].T.astype(BF16), b_router[i].reshape(N_EXPERTS, 1))
        plan = _dispatch_plan(cnt, t)
        xs = _dispatch(plan, h2, pos)
        ys = _experts(plan, xs, w_e_gate[i], w_e_up[i], w_e_down[i])
        xt = _combine(
            plan, ys, x1, h2, p[i].reshape(t, PLE_DIM), rank.T, gate.T,
            w_s_gate[i].astype(BF16), w_s_up[i].astype(BF16), w_s_down[i].astype(BF16),
            row(g_ple[i]), w_ple_gate[i].astype(BF16), w_ple[i].astype(BF16), row(g_final),
            final_norm=(i == depth - 1))
    return xt.reshape(bsz, s, d)
```

```python
import functools

import jax
import jax.numpy as jnp
from jax import lax
from jax.experimental import pallas as pl
from jax.experimental.pallas import tpu as pltpu

D_MODEL = 1024
D_CONV = 1024
D_POOL = 1024
CONV_WIDTH = 31
POOL_WINDOWS = (2, 4, 8, 16)
POOL_GROUP = 256
PLE_DIM = 256
N_EXPERTS = 64
N_GROUPS = 8
GROUP_SIZE = N_EXPERTS // N_GROUPS
TOPK_GROUPS = 4
TOP_K = 8
D_EXPERT = 256
ROUTED_SCALE = 2.5
NORM_EPS = 1e-6

F32 = jnp.float32
BF16 = jnp.bfloat16

MIX_TM = 256
MIX_NV = MIX_TM // 8
CONV_MG = 8
ROW_CHUNK = 64
LANE = 128

ROUTER_TM = 1024
WIN = 256
SEL_ROWS = 2560
SEL_RG = 64
SEL_MM = 512
EXP_BM = 512
EXP_XDEPTH = 3
EXP_SPLIT = 4
CMB_LG = 512

VMEM_LIMIT = 56 * 1024 * 1024


def _rms(x, g):
    ms = jnp.mean(x * x, axis=-1, keepdims=True)
    return x * lax.rsqrt(ms + NORM_EPS) * g


def _dot(a, b):
    return jnp.dot(a, b, preferred_element_type=F32)


def _mixer_kernel(x_ref, gmix_ref, win_ref, bin_ref, wdw_ref, bdw_ref, gcln_ref, bcln_ref,
                  wco_ref, bco_ref, wpool_ref, spool_ref, wout_ref, gffn_ref, perm_ref, unperm_ref,
                  x1_ref, h2_ref, a_ext, a_prev, u_ext, u_prev, c_buf, q_buf, *, tiles_per_seq):
    i = pl.program_id(0) % tiles_per_seq
    tm = MIX_TM
    nv = MIX_NV

    @pl.when(i == 0)
    def _():
        a_prev[...] = jnp.zeros(a_prev.shape, F32)
        u_prev[...] = jnp.zeros(u_prev.shape, F32)

    x = x_ref[...]
    h = _dot(perm_ref[...], _rms(x, gmix_ref[...]).astype(BF16)).astype(BF16)

    def proj(lo, hi):
        return _dot(h, win_ref[:, lo:hi]) + bin_ref[:, lo:hi]

    glu = proj(0, D_CONV) * jax.nn.sigmoid(proj(D_CONV, 2 * D_CONV))
    for lc in range(D_CONV // LANE):
        a_ext[lc, tm:2 * tm, :] = glu[:, lc * LANE:(lc + 1) * LANE]
    u_ext[tm:2 * tm, :] = proj(2 * D_CONV, 2 * D_CONV + D_POOL)

    def delayed_groups(ext, prev, first_group):
        last_row = lax.broadcasted_iota(jnp.int32, (8, ext.shape[-1]), 0) == 7
        for g in range(first_group, nv):
            rows = slice(8 * g, 8 * g + 8)
            mixed = jnp.where(last_row, prev[rows, :], ext[tm + 8 * g:tm + 8 * g + 8, :])
            ext[rows, :] = pltpu.roll(mixed, 1, axis=0)
            prev[rows, :] = ext[tm + 8 * g:tm + 8 * g + 8, :]

    delayed_groups(u_ext, u_prev, nv - (max(POOL_WINDOWS) - 1))

    def conv_column(lc, carry):
        a_col = a_ext.at[lc]
        delayed_groups(a_col, a_prev.at[lc], nv - (CONV_WIDTH - 1))
        w_col = wdw_ref.at[lc]
        for g0 in range(0, nv, CONV_MG):
            acc = None
            for k in range(CONV_WIDTH):
                src = nv + g0 + k - (CONV_WIDTH - 1)
                term = a_col[8 * src:8 * (src + CONV_MG), :] * w_col[k:k + 1, :]
                acc = term if acc is None else acc + term
            c_buf[lc, 8 * g0:8 * (g0 + CONV_MG), :] = acc + bdw_ref[lc]
        return carry
    lax.fori_loop(0, D_CONV // LANE, conv_column, 0)

    c = jnp.concatenate([c_buf[lc] for lc in range(D_CONV // LANE)], axis=-1)
    mu = jnp.mean(c, axis=-1, keepdims=True)
    xc = c - mu
    var = jnp.mean(xc * xc, axis=-1, keepdims=True)
    y = xc * lax.rsqrt(var + NORM_EPS) * gcln_ref[...] + bcln_ref[...]
    y = y * jax.nn.sigmoid(y)
    branch_a = _dot(y.astype(BF16), wco_ref[...]) + bco_ref[...]

    for r0 in range(0, tm, ROW_CHUNK):
        row = r0 + lax.broadcasted_iota(jnp.int32, (ROW_CHUNK, POOL_GROUP), 0)
        t1 = i * tm + (row % 8) * nv + row // 8 + 1
        for gi, w in enumerate(POOL_WINDOWS):
            ls = slice(gi * POOL_GROUP, (gi + 1) * POOL_GROUP)
            tok = u_ext[tm + r0:tm + r0 + ROW_CHUNK, ls]
            s = tok
            for j in range(1, w):
                s = s + u_ext[tm + r0 - 8 * j:tm + r0 - 8 * j + ROW_CHUNK, ls]
            cnt = jnp.minimum(t1, w).astype(F32)
            q_buf[r0:r0 + ROW_CHUNK, ls] = s / cnt - tok

    qs_out = []
    for gi in range(len(POOL_WINDOWS)):
        ls = slice(gi * POOL_GROUP, (gi + 1) * POOL_GROUP)
        qs_out.append(_dot(q_buf[:, ls].astype(BF16), wpool_ref[gi]) * spool_ref[:, ls])
    branch_b = jnp.concatenate(qs_out, axis=-1)

    c2 = 2 * D_CONV + D_POOL
    gate_a = jax.nn.sigmoid(proj(c2, c2 + D_MODEL))
    gate_b = jax.nn.sigmoid(proj(c2 + D_MODEL, c2 + 2 * D_MODEL))
    merged = gate_a * branch_a + gate_b * branch_b
    merged = _dot(unperm_ref[...], merged.astype(BF16)).astype(BF16)
    x1 = x + _dot(merged, wout_ref[...])
    x1_ref[...] = x1
    h2_ref[...] = _rms(x1, gffn_ref[...]).astype(BF16)


def _const_spec(shape):
    n = len(shape)
    return pl.BlockSpec(shape, lambda i, _n=n: (0,) * _n)


def _mixer(x, seq_len, g_mix, w_in, b_in, w_dw, b_dw, g_cln, b_cln, w_co, b_co, w_pool, s_pool, w_out,
           g_ffn):
    t = x.shape[0]
    tm = MIX_TM
    assert seq_len % tm == 0 and MIX_NV >= CONV_WIDTH and MIX_NV >= max(POOL_WINDOWS)
    d_in = w_in.shape[1]
    row = pl.BlockSpec((tm, D_MODEL), lambda i: (i, 0))
    n_col = D_CONV // LANE
    w_dw = w_dw.reshape(CONV_WIDTH, n_col, LANE).transpose(1, 0, 2)
    b_dw = b_dw.reshape(n_col, 1, LANE)
    r = jnp.arange(tm)
    perm = ((r % 8) * MIX_NV + r // 8)[:, None] == jnp.arange(tm)[None, :]
    perm = perm.astype(BF16)
    return pl.pallas_call(
        functools.partial(_mixer_kernel, tiles_per_seq=seq_len // tm),
        grid=(t // tm,),
        in_specs=[
            row,
            _const_spec((1, D_MODEL)),
            _const_spec((D_MODEL, d_in)),
            _const_spec((1, d_in)),
            _const_spec((n_col, CONV_WIDTH, LANE)),
            _const_spec((n_col, 1, LANE)),
            _const_spec((1, D_CONV)),
            _const_spec((1, D_CONV)),
            _const_spec((D_CONV, D_MODEL)),
            _const_spec((1, D_MODEL)),
            _const_spec((len(POOL_WINDOWS), POOL_GROUP, POOL_GROUP)),
            _const_spec((1, D_POOL)),
            _const_spec((D_MODEL, D_MODEL)),
            _const_spec((1, D_MODEL)),
            _const_spec((tm, tm)),
            _const_spec((tm, tm)),
        ],
        out_specs=[row, row],
        out_shape=[jax.ShapeDtypeStruct((t, D_MODEL), F32),
                   jax.ShapeDtypeStruct((t, D_MODEL), BF16)],
        scratch_shapes=[
            pltpu.VMEM((n_col, 2 * tm, LANE), F32),
            pltpu.VMEM((n_col, tm, LANE), F32),
            pltpu.VMEM((2 * tm, D_POOL), F32),
            pltpu.VMEM((tm, D_POOL), F32),
            pltpu.VMEM((n_col, tm, LANE), F32),
            pltpu.VMEM((tm, D_POOL), F32),
        ],
        compiler_params=pltpu.CompilerParams(
            dimension_semantics=("arbitrary",), vmem_limit_bytes=VMEM_LIMIT),
        name="mixer",
    )(x, g_mix, w_in, b_in, w_dw, b_dw, g_cln, b_cln, w_co, b_co, w_pool, s_pool, w_out, g_ffn, perm, perm.T)


def _beats(v, other, other_is_later):
    v = jnp.broadcast_to(v, other.shape)
    return jnp.where(other_is_later, jnp.where(v >= other, 1, 0), jnp.where(v > other, 1, 0))


def _router_kernel(h2_ref, wrt_ref, br_ref, utri_ref, ltri_ref, gate_ref, rank_ref, pos_ref, cnt_ref):
    tm = ROUTER_TM
    logits = lax.dot_general(wrt_ref[...], h2_ref[...], (((1,), (1,)), ((), ())),
                             preferred_element_type=F32)
    scores = jax.nn.sigmoid(logits)
    sel = scores + br_ref[...]
    shape3 = (N_GROUPS, GROUP_SIZE, tm)
    sel3 = sel.reshape(shape3)
    scores3 = scores.reshape(shape3)
    neg_inf = jnp.float32(-jnp.inf)

    member = lax.broadcasted_iota(jnp.int32, shape3, 1)
    m1 = jnp.max(sel3, axis=1, keepdims=True)
    first = jnp.min(jnp.where(sel3 == m1, member, GROUP_SIZE), axis=1, keepdims=True)
    m2 = jnp.max(jnp.where(member == first, neg_inf, sel3), axis=1, keepdims=True)
    gscore = jnp.broadcast_to(m1 + m2, shape3)

    gidx = lax.broadcasted_iota(jnp.int32, shape3, 0)
    grank = jnp.zeros(shape3, jnp.int32)
    for j in range(N_GROUPS):
        sj = gscore[j:j + 1]
        grank = grank + _beats(sj, gscore, gidx > j)
    masked = jnp.where(grank < TOPK_GROUPS, sel3, neg_inf)

    eidx = gidx * GROUP_SIZE + member
    work = masked
    erank = jnp.full(shape3, TOP_K, jnp.int32)
    for k in range(TOP_K):
        best = jnp.max(jnp.max(work, axis=0, keepdims=True), axis=1, keepdims=True)
        cand = jnp.where(work == best, eidx, N_EXPERTS)
        pick = jnp.min(jnp.min(cand, axis=0, keepdims=True), axis=1, keepdims=True)
        hit = eidx == pick
        work = jnp.where(hit, neg_inf, work)
        erank = jnp.where(hit, k, erank)
    chosen = erank < TOP_K
    top_s = jnp.where(chosen, scores3, 0.0)
    denom = jnp.sum(jnp.sum(top_s, axis=0, keepdims=True), axis=1, keepdims=True)
    gates3 = top_s / denom * ROUTED_SCALE
    chosen2 = jnp.where(chosen, 1.0, 0.0).reshape(N_EXPERTS, tm)
    gate_ref[...] = gates3.reshape(N_EXPERTS, tm).astype(BF16)

    for w in range(tm // WIN):
        ls = slice(w * WIN, (w + 1) * WIN)
        mw = chosen2[:, ls]
        rank = _dot(mw.astype(BF16), utri_ref[...])
        n = jnp.sum(mw, axis=1, keepdims=True)
        run = jnp.floor((n + 7.0) * 0.125) * 8.0
        start = _dot(ltri_ref[...], jnp.broadcast_to(run, (N_EXPERTS, WIN)).astype(BF16))
        rank_ref[:, ls] = jnp.where(mw > 0.5, rank, -1.0).astype(BF16)
        row3 = (rank + start).reshape(N_GROUPS, GROUP_SIZE, WIN)
        er = erank[:, :, ls]
        for k in range(TOP_K):
            pk = jnp.sum(jnp.sum(jnp.where(er == k, row3, 0.0), axis=0, keepdims=True), axis=1, keepdims=True)
            pos_ref[k:k + 1, ls] = pk.reshape(1, WIN).astype(jnp.int32)
        cnt_ref[w] = n


def _router(h2, w_rt, b_r):
    t = h2.shape[0]
    tm = ROUTER_TM
    utri = jnp.triu(jnp.ones((WIN, WIN), BF16), k=1)
    ltri = jnp.tril(jnp.ones((N_EXPERTS, N_EXPERTS), BF16), k=-1)
    return pl.pallas_call(
        _router_kernel,
        grid=(t // tm,),
        in_specs=[
            pl.BlockSpec((tm, D_MODEL), lambda i: (i, 0)),
            _const_spec((N_EXPERTS, D_MODEL)),
            _const_spec((N_EXPERTS, 1)),
            _const_spec((WIN, WIN)),
            _const_spec((N_EXPERTS, N_EXPERTS)),
        ],
        out_specs=[
            pl.BlockSpec((N_EXPERTS, tm), lambda i: (0, i)),
            pl.BlockSpec((N_EXPERTS, tm), lambda i: (0, i)),
            pl.BlockSpec((TOP_K, tm), lambda i: (0, i)),
            pl.BlockSpec((tm // WIN, N_EXPERTS, 1), lambda i: (i, 0, 0)),
        ],
        out_shape=[
            jax.ShapeDtypeStruct((N_EXPERTS, t), BF16),
            jax.ShapeDtypeStruct((N_EXPERTS, t), BF16),
            jax.ShapeDtypeStruct((TOP_K, t), jnp.int32),
            jax.ShapeDtypeStruct((t // WIN, N_EXPERTS, 1), F32),
        ],
        compiler_params=pltpu.CompilerParams(
            dimension_semantics=("arbitrary",), vmem_limit_bytes=VMEM_LIMIT),
        name="router",
    )(h2, w_rt, b_r, utri, ltri)


def _sorted_rows_bound(t):
    rows = t * TOP_K + (t // WIN) * N_EXPERTS * 7 + N_EXPERTS * (EXP_BM - 1)
    return -(-rows // EXP_BM) * EXP_BM


def _dispatch_plan(cnt, t):
    nw = t // WIN
    n = cnt.reshape(nw, N_EXPERTS).astype(jnp.int32)
    run = (n + 7) // 8 * 8
    local_end = jnp.cumsum(run, axis=1)
    local_off = jnp.concatenate([jnp.zeros((nw, 1), jnp.int32), local_end], axis=1)
    total = jnp.sum(run, axis=0)
    region = (total + EXP_BM - 1) // EXP_BM * EXP_BM
    region_end = jnp.cumsum(region)
    base = region_end - region
    global_off = base[None, :] + jnp.cumsum(run, axis=0) - run
    n_blocks = _sorted_rows_bound(t) // EXP_BM
    n_used = region_end[-1] // EXP_BM
    blk = jnp.arange(n_blocks, dtype=jnp.int32)
    blk_expert = jnp.sum((region_end[None, :] <= blk[:, None] * EXP_BM).astype(jnp.int32), axis=1)
    blk_expert = jnp.minimum(blk_expert, N_EXPERTS - 1)
    eid = jnp.arange(N_EXPERTS, dtype=jnp.int32)
    later_nonempty = (eid[None, :] > eid[:, None]) & (region[None, :] > 0)
    next_expert = jnp.min(jnp.where(later_nonempty, eid[None, :], N_EXPERTS), axis=1).astype(jnp.int32)
    return dict(
        run_lo=local_off[:, :N_EXPERTS].reshape(nw, N_EXPERTS, 1),
        run_hi=local_off[:, 1:].reshape(nw, N_EXPERTS, 1),
        local_off=local_off.reshape(-1), global_off=global_off.reshape(-1),
        fill_off=base + total, fill_cnt=region - total,
        blk_expert=blk_expert.astype(jnp.int32), next_expert=next_expert,
        n_used=n_used.reshape(1).astype(jnp.int32))


def _run_copy(local_ref, global_ref, win, e, vmem_buf, slot, hbm_buf, sem, to_hbm):
    lo = pl.multiple_of(local_ref[win * (N_EXPERTS + 1) + e], 8)
    cnt = pl.multiple_of(local_ref[win * (N_EXPERTS + 1) + e + 1] - lo, 8)
    go = pl.multiple_of(global_ref[win * N_EXPERTS + e], 8)
    v = vmem_buf.at[pl.ds(pl.multiple_of(slot * SEL_ROWS + lo, 8), cnt)]
    h = hbm_buf.at[pl.ds(go, cnt)]
    cp = pltpu.make_async_copy(v, h, sem.at[slot]) if to_hbm else pltpu.make_async_copy(h, v, sem.at[slot])
    return cnt, cp


def _start_runs(local_ref, global_ref, win, vmem_buf, slot, hbm_buf, sem, to_hbm):
    def body(e, carry):
        cnt, cp = _run_copy(local_ref, global_ref, win, e, vmem_buf, slot, hbm_buf, sem, to_hbm)

        @pl.when(cnt > 0)
        def _():
            cp.start()
        return carry
    lax.fori_loop(0, N_EXPERTS, body, 0)


def _wait_runs(local_ref, win, vmem_buf, slot, hbm_buf, sem, to_hbm):
    total = pl.multiple_of(local_ref[win * (N_EXPERTS + 1) + N_EXPERTS], 8)
    v = vmem_buf.at[pl.ds(pl.multiple_of(slot * SEL_ROWS, 8), total)]
    h = hbm_buf.at[pl.ds(0, total)]
    cp = pltpu.make_async_copy(v, h, sem.at[slot]) if to_hbm else pltpu.make_async_copy(h, v, sem.at[slot])

    @pl.when(total > 0)
    def _():
        cp.wait()


def _dispatch_kernel(local_ref, global_ref, fill_off_ref, fill_cnt_ref, h2_ref, pos_ref, xs_hbm, sbuf,
                     s_ref, sem, zsem, *, n_win):
    w = pl.program_id(0)
    slot = w % 2
    pos = pos_ref[...]

    h2 = h2_ref[...]
    for g in range(SEL_ROWS // SEL_MM):
        for sg in range(SEL_MM // SEL_RG):
            r0 = g * SEL_MM + sg * SEL_RG
            rid = r0 + lax.broadcasted_iota(jnp.int32, (SEL_RG, WIN), 0)
            acc = jnp.zeros((SEL_RG, WIN), F32)
            for k in range(TOP_K):
                acc = jnp.where(rid == pos[k:k + 1, :], 1.0, acc)
            s_ref[r0:r0 + SEL_RG, :] = acc.astype(BF16)
        rows = slice(g * SEL_MM, (g + 1) * SEL_MM)
        dst = pl.multiple_of(slot * SEL_ROWS + g * SEL_MM, SEL_MM)
        sbuf[pl.ds(dst, SEL_MM), :] = _dot(s_ref[rows, :], h2).astype(BF16)

    _start_runs(local_ref, global_ref, w, sbuf, slot, xs_hbm, sem, True)

    @pl.when(w > 0)
    def _():
        _wait_runs(local_ref, w - 1, sbuf, 1 - slot, xs_hbm, sem, True)

    @pl.when(w == n_win - 1)
    def _():
        sbuf[2 * SEL_ROWS:, :] = jnp.zeros((EXP_BM, D_MODEL), BF16)

        def fill(e, wait):
            cnt = pl.multiple_of(fill_cnt_ref[e], 8)
            off = pl.multiple_of(fill_off_ref[e], 8)
            cp = pltpu.make_async_copy(sbuf.at[pl.ds(2 * SEL_ROWS, cnt)], xs_hbm.at[pl.ds(off, cnt)], zsem)

            @pl.when(cnt > 0)
            def _():
                if wait:
                    cp.wait()
                else:
                    cp.start()

        def start_body(e, carry):
            fill(e, False)
            return carry

        def wait_body(e, carry):
            fill(e, True)
            return carry
        lax.fori_loop(0, N_EXPERTS, start_body, 0)
        _wait_runs(local_ref, w, sbuf, slot, xs_hbm, sem, True)
        lax.fori_loop(0, N_EXPERTS, wait_body, 0)


def _staging_shape(extra_rows):
    return jax.ShapeDtypeStruct((2 * SEL_ROWS + extra_rows, D_MODEL), BF16)


def _staging_spec(extra_rows):
    return pl.BlockSpec((2 * SEL_ROWS + extra_rows, D_MODEL), lambda w, *_: (0, 0))


def _dispatch(plan, h2, pos):
    t = h2.shape[0]
    n_win = t // WIN
    return pl.pallas_call(
        functools.partial(_dispatch_kernel, n_win=n_win),
        grid_spec=pltpu.PrefetchScalarGridSpec(
            num_scalar_prefetch=4,
            grid=(n_win,),
            in_specs=[
                pl.BlockSpec((WIN, D_MODEL), lambda w, *_: (w, 0)),
                pl.BlockSpec((TOP_K, WIN), lambda w, *_: (0, w)),
            ],
            out_specs=[pl.BlockSpec(memory_space=pl.ANY), _staging_spec(EXP_BM)],
            scratch_shapes=[
                pltpu.VMEM((SEL_ROWS, WIN), BF16),
                pltpu.SemaphoreType.DMA((2,)),
                pltpu.SemaphoreType.DMA,
            ]),
        out_shape=[jax.ShapeDtypeStruct((_sorted_rows_bound(t), D_MODEL), BF16), _staging_shape(EXP_BM)],
        compiler_params=pltpu.CompilerParams(
            dimension_semantics=("arbitrary",), vmem_limit_bytes=VMEM_LIMIT),
        name="dispatch",
    )(plan['local_off'], plan['global_off'], plan['fill_off'], plan['fill_cnt'], h2, pos)[0]


def _expert_kernel(blk_expert_ref, next_expert_ref, n_used_ref, xs_hbm, wg_hbm, wu_hbm, wd_hbm, ys_hbm,
                   xbuf, ybuf, wg_st, wu_st, wd_st, wg_bf, wu_bf, wd_bf, xsem, ysem, wsem):
    n_used = n_used_ref[0]
    part = EXP_BM // EXP_SPLIT

    def row_copies(b, slot, fetch):
        out = []
        for q in range(EXP_SPLIT):
            hbm_rows = pl.ds(pl.multiple_of(b * EXP_BM + q * part, part), part)
            if fetch:
                out.append(pltpu.make_async_copy(xs_hbm.at[hbm_rows], xbuf.at[slot, q * part:(q + 1) * part],
                                                 xsem.at[slot]))
            else:
                out.append(pltpu.make_async_copy(ybuf.at[slot, q * part:(q + 1) * part], ys_hbm.at[hbm_rows],
                                                 ysem.at[slot]))
        return out

    def weight_copies(e, slot):
        return [pltpu.make_async_copy(wg_hbm.at[e], wg_st.at[slot], wsem.at[slot]),
                pltpu.make_async_copy(wu_hbm.at[e], wu_st.at[slot], wsem.at[slot]),
                pltpu.make_async_copy(wd_hbm.at[e], wd_st.at[slot], wsem.at[slot])]

    def start(copies):
        for c in copies:
            c.start()

    def wait(copies):
        for c in copies:
            c.wait()

    for ahead in range(EXP_XDEPTH - 1):
        @pl.when(ahead < n_used)
        def _(ahead=ahead):
            start(row_copies(ahead, ahead, True))

    @pl.when(n_used > 0)
    def _():
        start(weight_copies(blk_expert_ref[0], 0))

    def body(b, wslot):
        e = blk_expert_ref[b]
        new_expert = jnp.logical_or(b == 0, e != blk_expert_ref[jnp.maximum(b - 1, 0)])

        @pl.when(new_expert)
        def _():
            wait(weight_copies(e, wslot))
            wg_bf[...] = wg_st[wslot].astype(BF16)
            wu_bf[...] = wu_st[wslot].astype(BF16)
            wd_bf[...] = wd_st[wslot].astype(BF16)
            nxt = next_expert_ref[e]

            @pl.when(nxt < N_EXPERTS)
            def _():
                start(weight_copies(nxt, 1 - wslot))

        @pl.when(b + EXP_XDEPTH - 1 < n_used)
        def _():
            start(row_copies(b + EXP_XDEPTH - 1, (b + EXP_XDEPTH - 1) % EXP_XDEPTH, True))

        xslot = b % EXP_XDEPTH
        yslot = b % 2
        wait(row_copies(b, xslot, True))

        @pl.when(b >= 2)
        def _():
            wait(row_copies(b - 2, yslot, False))

        x = xbuf[xslot]
        hg = _dot(x, wg_bf[...])
        hb = hg * jax.nn.sigmoid(hg) * _dot(x, wu_bf[...])
        ybuf[yslot] = _dot(hb.astype(BF16), wd_bf[...]).astype(BF16)
        start(row_copies(b, yslot, False))
        return jnp.where(new_expert, 1 - wslot, wslot)

    lax.fori_loop(0, n_used, body, jnp.int32(0))

    for back in (2, 1):
        @pl.when(n_used >= back)
        def _(back=back):
            wait(row_copies(n_used - back, (n_used - back) % 2, False))


def _experts(plan, xs, w_gate, w_up, w_down):
    any_spec = pl.BlockSpec(memory_space=pl.ANY)
    return pl.pallas_call(
        _expert_kernel,
        grid_spec=pltpu.PrefetchScalarGridSpec(
            num_scalar_prefetch=3,
            grid=(1,),
            in_specs=[any_spec, any_spec, any_spec, any_spec],
            out_specs=any_spec,
            scratch_shapes=[
                pltpu.VMEM((EXP_XDEPTH, EXP_BM, D_MODEL), BF16),
                pltpu.VMEM((2, EXP_BM, D_MODEL), BF16),
                pltpu.VMEM((2, D_MODEL, D_EXPERT), F32),
                pltpu.VMEM((2, D_MODEL, D_EXPERT), F32),
                pltpu.VMEM((2, D_EXPERT, D_MODEL), F32),
                pltpu.VMEM((D_MODEL, D_EXPERT), BF16),
                pltpu.VMEM((D_MODEL, D_EXPERT), BF16),
                pltpu.VMEM((D_EXPERT, D_MODEL), BF16),
                pltpu.SemaphoreType.DMA((EXP_XDEPTH,)),
                pltpu.SemaphoreType.DMA((2,)),
                pltpu.SemaphoreType.DMA((2,)),
            ]),
        out_shape=jax.ShapeDtypeStruct(xs.shape, BF16),
        compiler_params=pltpu.CompilerParams(
            dimension_semantics=("arbitrary",), vmem_limit_bytes=VMEM_LIMIT),
        name="experts",
    )(plan['blk_expert'], plan['next_expert'], plan['n_used'], xs, w_gate, w_up, w_down)


def _combine_kernel(local_ref, global_ref, x1_ref, h2_ref, p_ref, rank_ref, gate_ref, lo_ref, hi_ref,
                    wsg_ref, wsu_ref, wsd_ref, gple_ref, wpg_ref, wp_ref, gfin_ref, ys_hbm, o_ref, ybuf, st_ref, sem,
                    *, n_win, final_norm):
    w = pl.program_id(0)
    slot = w % 2

    @pl.when(w == 0)
    def _():
        ybuf[...] = jnp.zeros(ybuf.shape, BF16)
        _start_runs(local_ref, global_ref, w, ybuf, slot, ys_hbm, sem, False)

    @pl.when(w + 1 < n_win)
    def _():
        _start_runs(local_ref, global_ref, w + 1, ybuf, 1 - slot, ys_hbm, sem, False)

    lo = lo_ref[0]
    hi = hi_ref[0]
    lo_f = lo.astype(F32)
    rank_tm = rank_ref[...]
    gate_tm = gate_ref[...]

    def build_group(lg):
        cols = slice(lg * CMB_LG, (lg + 1) * CMB_LG)
        rid = lg * CMB_LG + lax.broadcasted_iota(jnp.int32, (N_EXPERTS, CMB_LG), 1)
        owner = jnp.where(rid >= lo, jnp.where(rid < hi, 1.0, 0.0), 0.0)
        run_row = rid[0:1, :].astype(F32) - jnp.sum(owner * lo_f, axis=0, keepdims=True)
        owner = owner.astype(BF16)
        hit = _dot(rank_tm, owner) == run_row
        st_ref[:, cols] = jnp.where(hit, _dot(gate_tm, owner), 0.0).astype(BF16)

    build_group(0)
    h2 = h2_ref[...]
    hs = _dot(h2, wsg_ref[...])
    hs = hs * jax.nn.sigmoid(hs) * _dot(h2, wsu_ref[...])
    shared = _dot(hs.astype(BF16), wsd_ref[...])

    _wait_runs(local_ref, w, ybuf, slot, ys_hbm, sem, False)
    routed = None
    n_groups = SEL_ROWS // CMB_LG
    for lg in range(n_groups):
        if lg + 1 < n_groups:
            build_group(lg + 1)
        src = pl.multiple_of(slot * SEL_ROWS + lg * CMB_LG, CMB_LG)
        part = _dot(st_ref[:, lg * CMB_LG:(lg + 1) * CMB_LG], ybuf[pl.ds(src, CMB_LG), :])
        routed = part if routed is None else routed + part
    x2 = x1_ref[...] + routed + shared

    hp = _rms(x2, gple_ref[...]).astype(BF16)
    gate = jax.nn.sigmoid(_dot(hp, wpg_ref[...]))
    x3 = x2 + gate * _dot(p_ref[...].astype(BF16), wp_ref[...])
    o_ref[...] = _rms(x3, gfin_ref[...]) if final_norm else x3


def _combine(plan, ys, x1, h2, p, rank_tm, gate_tm, wsg, wsu, wsd, g_ple, w_pg, w_p, g_fin, final_norm):
    t = x1.shape[0]
    n_win = t // WIN
    row = lambda width: pl.BlockSpec((WIN, width), lambda w, *_: (w, 0))
    const = lambda shape: pl.BlockSpec(shape, lambda w, *_: (0,) * len(shape))
    return pl.pallas_call(
        functools.partial(_combine_kernel, n_win=n_win, final_norm=final_norm),
        grid_spec=pltpu.PrefetchScalarGridSpec(
            num_scalar_prefetch=2,
            grid=(n_win,),
            in_specs=[
                row(D_MODEL), row(D_MODEL), row(PLE_DIM), row(N_EXPERTS), row(N_EXPERTS),
                pl.BlockSpec((1, N_EXPERTS, 1), lambda w, *_: (w, 0, 0)),
                pl.BlockSpec((1, N_EXPERTS, 1), lambda w, *_: (w, 0, 0)),
                const((D_MODEL, D_EXPERT)), const((D_MODEL, D_EXPERT)), const((D_EXPERT, D_MODEL)),
                const((1, D_MODEL)), const((D_MODEL, D_MODEL)), const((PLE_DIM, D_MODEL)),
                const((1, D_MODEL)),
                pl.BlockSpec(memory_space=pl.ANY),
            ],
            out_specs=[row(D_MODEL), _staging_spec(0)],
            scratch_shapes=[
                pltpu.VMEM((WIN, SEL_ROWS), BF16),
                pltpu.SemaphoreType.DMA((2,)),
            ]),
        out_shape=[jax.ShapeDtypeStruct((t, D_MODEL), F32), _staging_shape(0)],
        compiler_params=pltpu.CompilerParams(
            dimension_semantics=("arbitrary",), vmem_limit_bytes=VMEM_LIMIT),
        name="combine",
    )(plan['local_off'], plan['global_off'], x1, h2, p, rank_tm, gate_tm, plan['run_lo'], plan['run_hi'],
      wsg, wsu, wsd, g_ple, w_pg, w_p, g_fin, ys)[0]


def kernel(x, p, g_mix, w_in, b_in, w_dw, b_dw, g_cln, b_cln, w_conv_out, b_conv_out, w_pool, s_pool,
           w_out, g_ffn, w_router, b_router, w_e_gate, w_e_up, w_e_down, w_s_gate, w_s_up, w_s_down,
           g_ple, w_ple_gate, w_ple, g_final):
    bsz, s, d = x.shape
    t = bsz * s
    depth = w_in.shape[0]
    xt = x.reshape(t, d)
    row = lambda v: v.reshape(1, -1)
    for i in range(depth):
        x1, h2 = _mixer(
            xt, s, row(g_mix[i]), w_in[i].astype(BF16), row(b_in[i]), w_dw[i], row(b_dw[i]),
            row(g_cln[i]), row(b_cln[i]), w_conv_out[i].astype(BF16), row(b_conv_out[i]),
            w_pool[i].astype(BF16), row(s_pool[i]), w_out[i].astype(BF16), row(g_ffn[i]))
        gate, rank, pos, cnt = _router(h2, w_router[i].T.astype(BF16), b_router[i].reshape(N_EXPERTS, 1))
        plan = _dispatch_plan(cnt, t)
        xs = _dispatch(plan, h2, pos)
        ys = _experts(plan, xs, w_e_gate[i], w_e_up[i], w_e_down[i])
        xt = _combine(
            plan, ys, x1, h2, p[i].reshape(t, PLE_DIM), rank.T, gate.T,
            w_s_gate[i].astype(BF16), w_s_up[i].astype(BF16), w_s_down[i].astype(BF16),
            row(g_ple[i]), w_ple_gate[i].astype(BF16), w_ple[i].astype(BF16), row(g_final),
            final_norm=(i == depth - 1))
    return xt.reshape(bsz, s, d)
```

```python
import functools

import jax
import jax.numpy as jnp
from jax import lax
from jax.experimental import pallas as pl
from jax.experimental.pallas import tpu as pltpu

D_MODEL = 1024
D_CONV = 1024
D_POOL = 1024
CONV_WIDTH = 31
POOL_WINDOWS = (2, 4, 8, 16)
POOL_GROUP = 256
PLE_DIM = 256
N_EXPERTS = 64
N_GROUPS = 8
GROUP_SIZE = N_EXPERTS // N_GROUPS
TOPK_GROUPS = 4
TOP_K = 8
D_EXPERT = 256
ROUTED_SCALE = 2.5
NORM_EPS = 1e-6

F32 = jnp.float32
BF16 = jnp.bfloat16

MIX_TM = 256
MIX_NV = MIX_TM // 8
CONV_MG = 8
ROW_CHUNK = 64
LANE = 128

ROUTER_TM = 1024
WIN = 256
SEL_ROWS = 2560
SEL_RG = 64
SEL_MM = 512
EXP_BM = 512
EXP_XDEPTH = 4
EXP_SPLIT = 8
CMB_LG = 512

VMEM_LIMIT = 56 * 1024 * 1024


def _rms(x, g):
    ms = jnp.mean(x * x, axis=-1, keepdims=True)
    return x * lax.rsqrt(ms + NORM_EPS) * g


def _dot(a, b):
    return jnp.dot(a, b, preferred_element_type=F32)


def _mixer_kernel(x_ref, gmix_ref, win_ref, bin_ref, wdw_ref, bdw_ref, gcln_ref, bcln_ref,
                  wco_ref, bco_ref, wpool_ref, spool_ref, wout_ref, gffn_ref, perm_ref, unperm_ref,
                  x1_ref, h2_ref, a_ext, a_prev, u_ext, u_prev, c_buf, q_buf, *, tiles_per_seq):
    i = pl.program_id(0) % tiles_per_seq
    tm = MIX_TM
    nv = MIX_NV

    @pl.when(i == 0)
    def _():
        a_prev[...] = jnp.zeros(a_prev.shape, F32)
        u_prev[...] = jnp.zeros(u_prev.shape, F32)

    x = x_ref[...]
    h = _dot(perm_ref[...], _rms(x, gmix_ref[...]).astype(BF16)).astype(BF16)

    def proj(lo, hi):
        return _dot(h, win_ref[:, lo:hi]) + bin_ref[:, lo:hi]

    glu = proj(0, D_CONV) * jax.nn.sigmoid(proj(D_CONV, 2 * D_CONV))
    for lc in range(D_CONV // LANE):
        a_ext[lc, tm:2 * tm, :] = glu[:, lc * LANE:(lc + 1) * LANE]
    u_ext[tm:2 * tm, :] = proj(2 * D_CONV, 2 * D_CONV + D_POOL)

    def delayed_groups(ext, prev, first_group):
        last_row = lax.broadcasted_iota(jnp.int32, (8, ext.shape[-1]), 0) == 7
        for g in range(first_group, nv):
            rows = slice(8 * g, 8 * g + 8)
            mixed = jnp.where(last_row, prev[rows, :], ext[tm + 8 * g:tm + 8 * g + 8, :])
            ext[rows, :] = pltpu.roll(mixed, 1, axis=0)
            prev[rows, :] = ext[tm + 8 * g:tm + 8 * g + 8, :]

    delayed_groups(u_ext, u_prev, nv - (max(POOL_WINDOWS) - 1))

    def conv_column(lc, carry):
        a_col = a_ext.at[lc]
        delayed_groups(a_col, a_prev.at[lc], nv - (CONV_WIDTH - 1))
        w_col = wdw_ref.at[lc]
        for g0 in range(0, nv, CONV_MG):
            acc = None
            for k in range(CONV_WIDTH):
                src = nv + g0 + k - (CONV_WIDTH - 1)
                term = a_col[8 * src:8 * (src + CONV_MG), :] * w_col[k:k + 1, :]
                acc = term if acc is None else acc + term
            c_buf[lc, 8 * g0:8 * (g0 + CONV_MG), :] = acc + bdw_ref[lc]
        return carry
    lax.fori_loop(0, D_CONV // LANE, conv_column, 0)

    c = jnp.concatenate([c_buf[lc] for lc in range(D_CONV // LANE)], axis=-1)
    mu = jnp.mean(c, axis=-1, keepdims=True)
    xc = c - mu
    var = jnp.mean(xc * xc, axis=-1, keepdims=True)
    y = xc * lax.rsqrt(var + NORM_EPS) * gcln_ref[...] + bcln_ref[...]
    y = y * jax.nn.sigmoid(y)
    branch_a = _dot(y.astype(BF16), wco_ref[...]) + bco_ref[...]

    for r0 in range(0, tm, ROW_CHUNK):
        row = r0 + lax.broadcasted_iota(jnp.int32, (ROW_CHUNK, POOL_GROUP), 0)
        t1 = i * tm + (row % 8) * nv + row // 8 + 1
        for gi, w in enumerate(POOL_WINDOWS):
            ls = slice(gi * POOL_GROUP, (gi + 1) * POOL_GROUP)
            tok = u_ext[tm + r0:tm + r0 + ROW_CHUNK, ls]
            s = tok
            for j in range(1, w):
                s = s + u_ext[tm + r0 - 8 * j:tm + r0 - 8 * j + ROW_CHUNK, ls]
            cnt = jnp.minimum(t1, w).astype(F32)
            q_buf[r0:r0 + ROW_CHUNK, ls] = s / cnt - tok

    qs_out = []
    for gi in range(len(POOL_WINDOWS)):
        ls = slice(gi * POOL_GROUP, (gi + 1) * POOL_GROUP)
        qs_out.append(_dot(q_buf[:, ls].astype(BF16), wpool_ref[gi]) * spool_ref[:, ls])
    branch_b = jnp.concatenate(qs_out, axis=-1)

    c2 = 2 * D_CONV + D_POOL
    gate_a = jax.nn.sigmoid(proj(c2, c2 + D_MODEL))
    gate_b = jax.nn.sigmoid(proj(c2 + D_MODEL, c2 + 2 * D_MODEL))
    merged = gate_a * branch_a + gate_b * branch_b
    merged = _dot(unperm_ref[...], merged.astype(BF16)).astype(BF16)
    x1 = x + _dot(merged, wout_ref[...])
    x1_ref[...] = x1
    h2_ref[...] = _rms(x1, gffn_ref[...]).astype(BF16)


def _const_spec(shape):
    n = len(shape)
    return pl.BlockSpec(shape, lambda i, _n=n: (0,) * _n)


def _mixer(x, seq_len, g_mix, w_in, b_in, w_dw, b_dw, g_cln, b_cln, w_co, b_co, w_pool, s_pool, w_out,
           g_ffn):
    t = x.shape[0]
    tm = MIX_TM
    assert seq_len % tm == 0 and MIX_NV >= CONV_WIDTH and MIX_NV >= max(POOL_WINDOWS)
    d_in = w_in.shape[1]
    row = pl.BlockSpec((tm, D_MODEL), lambda i: (i, 0))
    n_col = D_CONV // LANE
    w_dw = w_dw.reshape(CONV_WIDTH, n_col, LANE).transpose(1, 0, 2)
    b_dw = b_dw.reshape(n_col, 1, LANE)
    r = jnp.arange(tm)
    perm = ((r % 8) * MIX_NV + r // 8)[:, None] == jnp.arange(tm)[None, :]
    perm = perm.astype(BF16)
    return pl.pallas_call(
        functools.partial(_mixer_kernel, tiles_per_seq=seq_len // tm),
        grid=(t // tm,),
        in_specs=[
            row,
            _const_spec((1, D_MODEL)),
            _const_spec((D_MODEL, d_in)),
            _const_spec((1, d_in)),
            _const_spec((n_col, CONV_WIDTH, LANE)),
            _const_spec((n_col, 1, LANE)),
            _const_spec((1, D_CONV)),
            _const_spec((1, D_CONV)),
            _const_spec((D_CONV, D_MODEL)),
            _const_spec((1, D_MODEL)),
            _const_spec((len(POOL_WINDOWS), POOL_GROUP, POOL_GROUP)),
            _const_spec((1, D_POOL)),
            _const_spec((D_MODEL, D_MODEL)),
            _const_spec((1, D_MODEL)),
            _const_spec((tm, tm)),
            _const_spec((tm, tm)),
        ],
        out_specs=[row, row],
        out_shape=[jax.ShapeDtypeStruct((t, D_MODEL), F32),
                   jax.ShapeDtypeStruct((t, D_MODEL), BF16)],
        scratch_shapes=[
            pltpu.VMEM((n_col, 2 * tm, LANE), F32),
            pltpu.VMEM((n_col, tm, LANE), F32),
            pltpu.VMEM((2 * tm, D_POOL), F32),
            pltpu.VMEM((tm, D_POOL), F32),
            pltpu.VMEM((n_col, tm, LANE), F32),
            pltpu.VMEM((tm, D_POOL), F32),
        ],
        compiler_params=pltpu.CompilerParams(
            dimension_semantics=("arbitrary",), vmem_limit_bytes=VMEM_LIMIT),
        name="mixer",
    )(x, g_mix, w_in, b_in, w_dw, b_dw, g_cln, b_cln, w_co, b_co, w_pool, s_pool, w_out, g_ffn, perm, perm.T)


def _beats(v, other, other_is_later):
    v = jnp.broadcast_to(v, other.shape)
    return jnp.where(other_is_later, jnp.where(v >= other, 1, 0), jnp.where(v > other, 1, 0))


def _router_kernel(h2_ref, wrt_ref, br_ref, utri_ref, ltri_ref, gate_ref, rank_ref, pos_ref, cnt_ref):
    tm = ROUTER_TM
    logits = lax.dot_general(wrt_ref[...], h2_ref[...], (((1,), (1,)), ((), ())),
                             preferred_element_type=F32)
    scores = jax.nn.sigmoid(logits)
    sel = scores + br_ref[...]
    shape3 = (N_GROUPS, GROUP_SIZE, tm)
    sel3 = sel.reshape(shape3)
    scores3 = scores.reshape(shape3)
    neg_inf = jnp.float32(-jnp.inf)

    member = lax.broadcasted_iota(jnp.int32, shape3, 1)
    m1 = jnp.max(sel3, axis=1, keepdims=True)
    first = jnp.min(jnp.where(sel3 == m1, member, GROUP_SIZE), axis=1, keepdims=True)
    m2 = jnp.max(jnp.where(member == first, neg_inf, sel3), axis=1, keepdims=True)
    gscore = jnp.broadcast_to(m1 + m2, shape3)

    gidx = lax.broadcasted_iota(jnp.int32, shape3, 0)
    grank = jnp.zeros(shape3, jnp.int32)
    for j in range(N_GROUPS):
        sj = gscore[j:j + 1]
        grank = grank + _beats(sj, gscore, gidx > j)
    masked = jnp.where(grank < TOPK_GROUPS, sel3, neg_inf)

    eidx = gidx * GROUP_SIZE + member
    work = masked
    erank = jnp.full(shape3, TOP_K, jnp.int32)
    for k in range(TOP_K):
        best = jnp.max(jnp.max(work, axis=0, keepdims=True), axis=1, keepdims=True)
        cand = jnp.where(work == best, eidx, N_EXPERTS)
        pick = jnp.min(jnp.min(cand, axis=0, keepdims=True), axis=1, keepdims=True)
        hit = eidx == pick
        work = jnp.where(hit, neg_inf, work)
        erank = jnp.where(hit, k, erank)
    chosen = erank < TOP_K
    top_s = jnp.where(chosen, scores3, 0.0)
    denom = jnp.sum(jnp.sum(top_s, axis=0, keepdims=True), axis=1, keepdims=True)
    gates3 = top_s / denom * ROUTED_SCALE
    chosen2 = jnp.where(chosen, 1.0, 0.0).reshape(N_EXPERTS, tm)
    gate_ref[...] = gates3.reshape(N_EXPERTS, tm).astype(BF16)

    for w in range(tm // WIN):
        ls = slice(w * WIN, (w + 1) * WIN)
        mw = chosen2[:, ls]
        rank = _dot(mw.astype(BF16), utri_ref[...])
        n = jnp.sum(mw, axis=1, keepdims=True)
        run = jnp.floor((n + 7.0) * 0.125) * 8.0
        start = _dot(ltri_ref[...], jnp.broadcast_to(run, (N_EXPERTS, WIN)).astype(BF16))
        rank_ref[:, ls] = jnp.where(mw > 0.5, rank, -1.0).astype(BF16)
        row3 = (rank + start).reshape(N_GROUPS, GROUP_SIZE, WIN)
        er = erank[:, :, ls]
        for k in range(TOP_K):
            pk = jnp.sum(jnp.sum(jnp.where(er == k, row3, 0.0), axis=0, keepdims=True), axis=1, keepdims=True)
            pos_ref[k:k + 1, ls] = pk.reshape(1, WIN).astype(jnp.int32)
        cnt_ref[w] = n


def _router(h2, w_rt, b_r):
    t = h2.shape[0]
    tm = ROUTER_TM
    utri = jnp.triu(jnp.ones((WIN, WIN), BF16), k=1)
    ltri = jnp.tril(jnp.ones((N_EXPERTS, N_EXPERTS), BF16), k=-1)
    return pl.pallas_call(
        _router_kernel,
        grid=(t // tm,),
        in_specs=[
            pl.BlockSpec((tm, D_MODEL), lambda i: (i, 0)),
            _const_spec((N_EXPERTS, D_MODEL)),
            _const_spec((N_EXPERTS, 1)),
            _const_spec((WIN, WIN)),
            _const_spec((N_EXPERTS, N_EXPERTS)),
        ],
        out_specs=[
            pl.BlockSpec((N_EXPERTS, tm), lambda i: (0, i)),
            pl.BlockSpec((N_EXPERTS, tm), lambda i: (0, i)),
            pl.BlockSpec((TOP_K, tm), lambda i: (0, i)),
            pl.BlockSpec((tm // WIN, N_EXPERTS, 1), lambda i: (i, 0, 0)),
        ],
        out_shape=[
            jax.ShapeDtypeStruct((N_EXPERTS, t), BF16),
            jax.ShapeDtypeStruct((N_EXPERTS, t), BF16),
            jax.ShapeDtypeStruct((TOP_K, t), jnp.int32),
            jax.ShapeDtypeStruct((t // WIN, N_EXPERTS, 1), F32),
        ],
        compiler_params=pltpu.CompilerParams(
            dimension_semantics=("arbitrary",), vmem_limit_bytes=VMEM_LIMIT),
        name="router",
    )(h2, w_rt, b_r, utri, ltri)


def _sorted_rows_bound(t):
    rows = t * TOP_K + (t // WIN) * N_EXPERTS * 7 + N_EXPERTS * (EXP_BM - 1)
    return -(-rows // EXP_BM) * EXP_BM


def _dispatch_plan(cnt, t):
    nw = t // WIN
    n = cnt.reshape(nw, N_EXPERTS).astype(jnp.int32)
    run = (n + 7) // 8 * 8
    local_end = jnp.cumsum(run, axis=1)
    local_off = jnp.concatenate([jnp.zeros((nw, 1), jnp.int32), local_end], axis=1)
    total = jnp.sum(run, axis=0)
    region = (total + EXP_BM - 1) // EXP_BM * EXP_BM
    region_end = jnp.cumsum(region)
    base = region_end - region
    global_off = base[None, :] + jnp.cumsum(run, axis=0) - run
    n_blocks = _sorted_rows_bound(t) // EXP_BM
    n_used = region_end[-1] // EXP_BM
    blk = jnp.arange(n_blocks, dtype=jnp.int32)
    blk_expert = jnp.sum((region_end[None, :] <= blk[:, None] * EXP_BM).astype(jnp.int32), axis=1)
    blk_expert = jnp.minimum(blk_expert, N_EXPERTS - 1)
    eid = jnp.arange(N_EXPERTS, dtype=jnp.int32)
    later_nonempty = (eid[None, :] > eid[:, None]) & (region[None, :] > 0)
    next_expert = jnp.min(jnp.where(later_nonempty, eid[None, :], N_EXPERTS), axis=1).astype(jnp.int32)
    return dict(
        run_lo=local_off[:, :N_EXPERTS].reshape(nw, N_EXPERTS, 1),
        run_hi=local_off[:, 1:].reshape(nw, N_EXPERTS, 1),
        local_off=local_off.reshape(-1), global_off=global_off.reshape(-1),
        fill_off=base + total, fill_cnt=region - total,
        blk_expert=blk_expert.astype(jnp.int32), next_expert=next_expert,
        n_used=n_used.reshape(1).astype(jnp.int32))


def _run_copy(local_ref, global_ref, win, e, vmem_buf, slot, hbm_buf, sem, to_hbm):
    lo = pl.multiple_of(local_ref[win * (N_EXPERTS + 1) + e], 8)
    cnt = pl.multiple_of(local_ref[win * (N_EXPERTS + 1) + e + 1] - lo, 8)
    go = pl.multiple_of(global_ref[win * N_EXPERTS + e], 8)
    v = vmem_buf.at[pl.ds(pl.multiple_of(slot * SEL_ROWS + lo, 8), cnt)]
    h = hbm_buf.at[pl.ds(go, cnt)]
    cp = pltpu.make_async_copy(v, h, sem.at[slot]) if to_hbm else pltpu.make_async_copy(h, v, sem.at[slot])
    return cnt, cp


def _start_runs(local_ref, global_ref, win, vmem_buf, slot, hbm_buf, sem, to_hbm):
    def body(e, carry):
        cnt, cp = _run_copy(local_ref, global_ref, win, e, vmem_buf, slot, hbm_buf, sem, to_hbm)

        @pl.when(cnt > 0)
        def _():
            cp.start()
        return carry
    lax.fori_loop(0, N_EXPERTS, body, 0)


def _wait_runs(local_ref, win, vmem_buf, slot, hbm_buf, sem, to_hbm):
    total = pl.multiple_of(local_ref[win * (N_EXPERTS + 1) + N_EXPERTS], 8)
    v = vmem_buf.at[pl.ds(pl.multiple_of(slot * SEL_ROWS, 8), total)]
    h = hbm_buf.at[pl.ds(0, total)]
    cp = pltpu.make_async_copy(v, h, sem.at[slot]) if to_hbm else pltpu.make_async_copy(h, v, sem.at[slot])

    @pl.when(total > 0)
    def _():
        cp.wait()


def _dispatch_kernel(local_ref, global_ref, fill_off_ref, fill_cnt_ref, h2_ref, pos_ref, xs_hbm, sbuf,
                     s_ref, sem, zsem, *, n_win):
    w = pl.program_id(0)
    slot = w % 2
    pos = pos_ref[...]

    h2 = h2_ref[...]
    for g in range(SEL_ROWS // SEL_MM):
        for sg in range(SEL_MM // SEL_RG):
            r0 = g * SEL_MM + sg * SEL_RG
            rid = r0 + lax.broadcasted_iota(jnp.int32, (SEL_RG, WIN), 0)
            acc = jnp.zeros((SEL_RG, WIN), F32)
            for k in range(TOP_K):
                acc = jnp.where(rid == pos[k:k + 1, :], 1.0, acc)
            s_ref[r0:r0 + SEL_RG, :] = acc.astype(BF16)
        rows = slice(g * SEL_MM, (g + 1) * SEL_MM)
        dst = pl.multiple_of(slot * SEL_ROWS + g * SEL_MM, SEL_MM)
        sbuf[pl.ds(dst, SEL_MM), :] = _dot(s_ref[rows, :], h2).astype(BF16)

    _start_runs(local_ref, global_ref, w, sbuf, slot, xs_hbm, sem, True)

    @pl.when(w > 0)
    def _():
        _wait_runs(local_ref, w - 1, sbuf, 1 - slot, xs_hbm, sem, True)

    @pl.when(w == n_win - 1)
    def _():
        sbuf[2 * SEL_ROWS:, :] = jnp.zeros((EXP_BM, D_MODEL), BF16)

        def fill(e, wait):
            cnt = pl.multiple_of(fill_cnt_ref[e], 8)
            off = pl.multiple_of(fill_off_ref[e], 8)
            cp = pltpu.make_async_copy(sbuf.at[pl.ds(2 * SEL_ROWS, cnt)], xs_hbm.at[pl.ds(off, cnt)], zsem)

            @pl.when(cnt > 0)
            def _():
                if wait:
                    cp.wait()
                else:
                    cp.start()

        def start_body(e, carry):
            fill(e, False)
            return carry

        def wait_body(e, carry):
            fill(e, True)
            return carry
        lax.fori_loop(0, N_EXPERTS, start_body, 0)
        _wait_runs(local_ref, w, sbuf, slot, xs_hbm, sem, True)
        lax.fori_loop(0, N_EXPERTS, wait_body, 0)


def _staging_shape(extra_rows):
    return jax.ShapeDtypeStruct((2 * SEL_ROWS + extra_rows, D_MODEL), BF16)


def _staging_spec(extra_rows):
    return pl.BlockSpec((2 * SEL_ROWS + extra_rows, D_MODEL), lambda w, *_: (0, 0))


def _dispatch(plan, h2, pos):
    t = h2.shape[0]
    n_win = t // WIN
    return pl.pallas_call(
        functools.partial(_dispatch_kernel, n_win=n_win),
        grid_spec=pltpu.PrefetchScalarGridSpec(
            num_scalar_prefetch=4,
            grid=(n_win,),
            in_specs=[
                pl.BlockSpec((WIN, D_MODEL), lambda w, *_: (w, 0)),
                pl.BlockSpec((TOP_K, WIN), lambda w, *_: (0, w)),
            ],
            out_specs=[pl.BlockSpec(memory_space=pl.ANY), _staging_spec(EXP_BM)],
            scratch_shapes=[
                pltpu.VMEM((SEL_ROWS, WIN), BF16),
                pltpu.SemaphoreType.DMA((2,)),
                pltpu.SemaphoreType.DMA,
            ]),
        out_shape=[jax.ShapeDtypeStruct((_sorted_rows_bound(t), D_MODEL), BF16), _staging_shape(EXP_BM)],
        compiler_params=pltpu.CompilerParams(
            dimension_semantics=("arbitrary",), vmem_limit_bytes=VMEM_LIMIT),
        name="dispatch",
    )(plan['local_off'], plan['global_off'], plan['fill_off'], plan['fill_cnt'], h2, pos)[0]


def _expert_kernel(blk_expert_ref, next_expert_ref, n_used_ref, xs_hbm, wg_hbm, wu_hbm, wd_hbm, ys_hbm,
                   xbuf, ybuf, wg_st, wu_st, wd_st, wg_bf, wu_bf, wd_bf, xsem, ysem, wsem):
    n_used = n_used_ref[0]
    part = EXP_BM // EXP_SPLIT

    def row_copies(b, slot, fetch):
        out = []
        for q in range(EXP_SPLIT):
            hbm_rows = pl.ds(pl.multiple_of(b * EXP_BM + q * part, part), part)
            if fetch:
                out.append(pltpu.make_async_copy(xs_hbm.at[hbm_rows], xbuf.at[slot, q * part:(q + 1) * part],
                                                 xsem.at[slot]))
            else:
                out.append(pltpu.make_async_copy(ybuf.at[slot, q * part:(q + 1) * part], ys_hbm.at[hbm_rows],
                                                 ysem.at[slot]))
        return out

    def weight_copies(e, slot):
        return [pltpu.make_async_copy(wg_hbm.at[e], wg_st.at[slot], wsem.at[slot]),
                pltpu.make_async_copy(wu_hbm.at[e], wu_st.at[slot], wsem.at[slot]),
                pltpu.make_async_copy(wd_hbm.at[e], wd_st.at[slot], wsem.at[slot])]

    def start(copies):
        for c in copies:
            c.start()

    def wait(copies):
        for c in copies:
            c.wait()

    for ahead in range(EXP_XDEPTH - 1):
        @pl.when(ahead < n_used)
        def _(ahead=ahead):
            start(row_copies(ahead, ahead, True))

    @pl.when(n_used > 0)
    def _():
        start(weight_copies(blk_expert_ref[0], 0))

    def body(b, wslot):
        e = blk_expert_ref[b]
        new_expert = jnp.logical_or(b == 0, e != blk_expert_ref[jnp.maximum(b - 1, 0)])

        @pl.when(new_expert)
        def _():
            wait(weight_copies(e, wslot))
            wg_bf[...] = wg_st[wslot].astype(BF16)
            wu_bf[...] = wu_st[wslot].astype(BF16)
            wd_bf[...] = wd_st[wslot].astype(BF16)
            nxt = next_expert_ref[e]

            @pl.when(nxt < N_EXPERTS)
            def _():
                start(weight_copies(nxt, 1 - wslot))

        @pl.when(b + EXP_XDEPTH - 1 < n_used)
        def _():
            start(row_copies(b + EXP_XDEPTH - 1, (b + EXP_XDEPTH - 1) % EXP_XDEPTH, True))

        xslot = b % EXP_XDEPTH
        yslot = b % 2
        wait(row_copies(b, xslot, True))

        @pl.when(b >= 2)
        def _():
            wait(row_copies(b - 2, yslot, False))

        x = xbuf[xslot]
        hg = _dot(x, wg_bf[...])
        hb = hg * jax.nn.sigmoid(hg) * _dot(x, wu_bf[...])
        ybuf[yslot] = _dot(hb.astype(BF16), wd_bf[...]).astype(BF16)
        start(row_copies(b, yslot, False))
        return jnp.where(new_expert, 1 - wslot, wslot)

    lax.fori_loop(0, n_used, body, jnp.int32(0))

    for back in (2, 1):
        @pl.when(n_used >= back)
        def _(back=back):
            wait(row_copies(n_used - back, (n_used - back) % 2, False))


def _experts(plan, xs, w_gate, w_up, w_down):
    any_spec = pl.BlockSpec(memory_space=pl.ANY)
    return pl.pallas_call(
        _expert_kernel,
        grid_spec=pltpu.PrefetchScalarGridSpec(
            num_scalar_prefetch=3,
            grid=(1,),
            in_specs=[any_spec, any_spec, any_spec, any_spec],
            out_specs=any_spec,
            scratch_shapes=[
                pltpu.VMEM((EXP_XDEPTH, EXP_BM, D_MODEL), BF16),
                pltpu.VMEM((2, EXP_BM, D_MODEL), BF16),
                pltpu.VMEM((2, D_MODEL, D_EXPERT), F32),
                pltpu.VMEM((2, D_MODEL, D_EXPERT), F32),
                pltpu.VMEM((2, D_EXPERT, D_MODEL), F32),
                pltpu.VMEM((D_MODEL, D_EXPERT), BF16),
                pltpu.VMEM((D_MODEL, D_EXPERT), BF16),
                pltpu.VMEM((D_EXPERT, D_MODEL), BF16),
                pltpu.SemaphoreType.DMA((EXP_XDEPTH,)),
                pltpu.SemaphoreType.DMA((2,)),
                pltpu.SemaphoreType.DMA((2,)),
            ]),
        out_shape=jax.ShapeDtypeStruct(xs.shape, BF16),
        compiler_params=pltpu.CompilerParams(
            dimension_semantics=("arbitrary",), vmem_limit_bytes=VMEM_LIMIT),
        name="experts",
    )(plan['blk_expert'], plan['next_expert'], plan['n_used'], xs, w_gate, w_up, w_down)


def _combine_kernel(local_ref, global_ref, x1_ref, h2_ref, p_ref, rank_ref, gate_ref, lo_ref, hi_ref,
                    wsg_ref, wsu_ref, wsd_ref, gple_ref, wpg_ref, wp_ref, gfin_ref, ys_hbm, o_ref, ybuf, st_ref, sem,
                    *, n_win, final_norm):
    w = pl.program_id(0)
    slot = w % 2

    @pl.when(w == 0)
    def _():
        ybuf[...] = jnp.zeros(ybuf.shape, BF16)
        _start_runs(local_ref, global_ref, w, ybuf, slot, ys_hbm, sem, False)

    @pl.when(w + 1 < n_win)
    def _():
        _start_runs(local_ref, global_ref, w + 1, ybuf, 1 - slot, ys_hbm, sem, False)

    lo = lo_ref[0]
    hi = hi_ref[0]
    lo_f = lo.astype(F32)
    rank_tm = rank_ref[...]
    gate_tm = gate_ref[...]

    def build_group(lg):
        cols = slice(lg * CMB_LG, (lg + 1) * CMB_LG)
        rid = lg * CMB_LG + lax.broadcasted_iota(jnp.int32, (N_EXPERTS, CMB_LG), 1)
        owner = jnp.where(rid >= lo, jnp.where(rid < hi, 1.0, 0.0), 0.0)
        run_row = rid[0:1, :].astype(F32) - jnp.sum(owner * lo_f, axis=0, keepdims=True)
        owner = owner.astype(BF16)
        hit = _dot(rank_tm, owner) == run_row
        st_ref[:, cols] = jnp.where(hit, _dot(gate_tm, owner), 0.0).astype(BF16)

    build_group(0)
    h2 = h2_ref[...]
    hs = _dot(h2, wsg_ref[...])
    hs = hs * jax.nn.sigmoid(hs) * _dot(h2, wsu_ref[...])
    shared = _dot(hs.astype(BF16), wsd_ref[...])

    _wait_runs(local_ref, w, ybuf, slot, ys_hbm, sem, False)
    routed = None
    n_groups = SEL_ROWS // CMB_LG
    for lg in range(n_groups):
        if lg + 1 < n_groups:
            build_group(lg + 1)
        src = pl.multiple_of(slot * SEL_ROWS + lg * CMB_LG, CMB_LG)
        part = _dot(st_ref[:, lg * CMB_LG:(lg + 1) * CMB_LG], ybuf[pl.ds(src, CMB_LG), :])
        routed = part if routed is None else routed + part
    x2 = x1_ref[...] + routed + shared

    hp = _rms(x2, gple_ref[...]).astype(BF16)
    gate = jax.nn.sigmoid(_dot(hp, wpg_ref[...]))
    x3 = x2 + gate * _dot(p_ref[...].astype(BF16), wp_ref[...])
    o_ref[...] = _rms(x3, gfin_ref[...]) if final_norm else x3


def _combine(plan, ys, x1, h2, p, rank_tm, gate_tm, wsg, wsu, wsd, g_ple, w_pg, w_p, g_fin, final_norm):
    t = x1.shape[0]
    n_win = t // WIN
    row = lambda width: pl.BlockSpec((WIN, width), lambda w, *_: (w, 0))
    const = lambda shape: pl.BlockSpec(shape, lambda w, *_: (0,) * len(shape))
    return pl.pallas_call(
        functools.partial(_combine_kernel, n_win=n_win, final_norm=final_norm),
        grid_spec=pltpu.PrefetchScalarGridSpec(
            num_scalar_prefetch=2,
            grid=(n_win,),
            in_specs=[
                row(D_MODEL), row(D_MODEL), row(PLE_DIM), row(N_EXPERTS), row(N_EXPERTS),
                pl.BlockSpec((1, N_EXPERTS, 1), lambda w, *_: (w, 0, 0)),
                pl.BlockSpec((1, N_EXPERTS, 1), lambda w, *_: (w, 0, 0)),
                const((D_MODEL, D_EXPERT)), const((D_MODEL, D_EXPERT)), const((D_EXPERT, D_MODEL)),
                const((1, D_MODEL)), const((D_MODEL, D_MODEL)), const((PLE_DIM, D_MODEL)),
                const((1, D_MODEL)),
                pl.BlockSpec(memory_space=pl.ANY),
            ],
            out_specs=[row(D_MODEL), _staging_spec(0)],
            scratch_shapes=[
                pltpu.VMEM((WIN, SEL_ROWS), BF16),
                pltpu.SemaphoreType.DMA((2,)),
            ]),
        out_shape=[jax.ShapeDtypeStruct((t, D_MODEL), F32), _staging_shape(0)],
        compiler_params=pltpu.CompilerParams(
            dimension_semantics=("arbitrary",), vmem_limit_bytes=VMEM_LIMIT),
        name="combine",
    )(plan['local_off'], plan['global_off'], x1, h2, p, rank_tm, gate_tm, plan['run_lo'], plan['run_hi'],
      wsg, wsu, wsd, g_ple, w_pg, w_p, g_fin, ys)[0]


def kernel(x, p, g_mix, w_in, b_in, w_dw, b_dw, g_cln, b_cln, w_conv_out, b_conv_out, w_pool, s_pool,
           w_out, g_ffn, w_router, b_router, w_e_gate, w_e_up, w_e_down, w_s_gate, w_s_up, w_s_down,
           g_ple, w_ple_gate, w_ple, g_final):
    bsz, s, d = x.shape
    t = bsz * s
    depth = w_in.shape[0]
    xt = x.reshape(t, d)
    row = lambda v: v.reshape(1, -1)
    for i in range(depth):
        x1, h2 = _mixer(
            xt, s, row(g_mix[i]), w_in[i].astype(BF16), row(b_in[i]), w_dw[i], row(b_dw[i]),
            row(g_cln[i]), row(b_cln[i]), w_conv_out[i].astype(BF16), row(b_conv_out[i]),
            w_pool[i].astype(BF16), row(s_pool[i]), w_out[i].astype(BF16), row(g_ffn[i]))
        gate, rank, pos, cnt = _router(h2, w_router[i].T.astype(BF16), b_router[i].reshape(N_EXPERTS, 1))
        plan = _dispatch_plan(cnt, t)
        xs = _dispatch(plan, h2, pos)
        ys = _experts(plan, xs, w_e_gate[i], w_e_up[i], w_e_down[i])
        xt = _combine(
            plan, ys, x1, h2, p[i].reshape(t, PLE_DIM), rank.T, gate.T,
            w_s_gate[i].astype(BF16), w_s_up[i].astype(BF16), w_s_down[i].astype(BF16),
            row(g_ple[i]), w_ple_gate[i].astype(BF16), w_ple[i].astype(BF16), row(g_final),
            final_norm=(i == depth - 1))
    return xt.reshape(bsz, s, d)
```

```python
import functools

import jax
import jax.numpy as jnp
from jax import lax
from jax.experimental import pallas as pl
from jax.experimental.pallas import tpu as pltpu

D_MODEL = 1024
D_CONV = 1024
D_POOL = 1024
CONV_WIDTH = 31
POOL_WINDOWS = (2, 4, 8, 16)
POOL_GROUP = 256
PLE_DIM = 256
N_EXPERTS = 64
N_GROUPS = 8
GROUP_SIZE = N_EXPERTS // N_GROUPS
TOPK_GROUPS = 4
TOP_K = 8
D_EXPERT = 256
ROUTED_SCALE = 2.5
NORM_EPS = 1e-6

F32 = jnp.float32
BF16 = jnp.bfloat16

MIX_TM = 256
MIX_NV = MIX_TM // 8
CONV_MG = 8
ROW_CHUNK = 64
LANE = 128

ROUTER_TM = 1024
WIN = 256
SEL_ROWS = 2560
SEL_RG = 64
SEL_MM = 512
EXP_BM = 512
EXP_XDEPTH = 5
EXP_YDEPTH = 3
EXP_SPLIT = 8
CMB_LG = 512

VMEM_LIMIT = 56 * 1024 * 1024


def _rms(x, g):
    ms = jnp.mean(x * x, axis=-1, keepdims=True)
    return x * lax.rsqrt(ms + NORM_EPS) * g


def _dot(a, b):
    return jnp.dot(a, b, preferred_element_type=F32)


def _mixer_kernel(x_ref, gmix_ref, win_ref, bin_ref, wdw_ref, bdw_ref, gcln_ref, bcln_ref,
                  wco_ref, bco_ref, wpool_ref, spool_ref, wout_ref, gffn_ref, perm_ref, unperm_ref,
                  x1_ref, h2_ref, a_ext, a_prev, u_ext, u_prev, c_buf, q_buf, *, tiles_per_seq):
    i = pl.program_id(0) % tiles_per_seq
    tm = MIX_TM
    nv = MIX_NV

    @pl.when(i == 0)
    def _():
        a_prev[...] = jnp.zeros(a_prev.shape, F32)
        u_prev[...] = jnp.zeros(u_prev.shape, F32)

    x = x_ref[...]
    h = _dot(perm_ref[...], _rms(x, gmix_ref[...]).astype(BF16)).astype(BF16)

    def proj(lo, hi):
        return _dot(h, win_ref[:, lo:hi]) + bin_ref[:, lo:hi]

    glu = proj(0, D_CONV) * jax.nn.sigmoid(proj(D_CONV, 2 * D_CONV))
    for lc in range(D_CONV // LANE):
        a_ext[lc, tm:2 * tm, :] = glu[:, lc * LANE:(lc + 1) * LANE]
    u_ext[tm:2 * tm, :] = proj(2 * D_CONV, 2 * D_CONV + D_POOL)

    def delayed_groups(ext, prev, first_group):
        last_row = lax.broadcasted_iota(jnp.int32, (8, ext.shape[-1]), 0) == 7
        for g in range(first_group, nv):
            rows = slice(8 * g, 8 * g + 8)
            mixed = jnp.where(last_row, prev[rows, :], ext[tm + 8 * g:tm + 8 * g + 8, :])
            ext[rows, :] = pltpu.roll(mixed, 1, axis=0)
            prev[rows, :] = ext[tm + 8 * g:tm + 8 * g + 8, :]

    delayed_groups(u_ext, u_prev, nv - (max(POOL_WINDOWS) - 1))

    def conv_column(lc, carry):
        a_col = a_ext.at[lc]
        delayed_groups(a_col, a_prev.at[lc], nv - (CONV_WIDTH - 1))
        w_col = wdw_ref.at[lc]
        for g0 in range(0, nv, CONV_MG):
            acc = None
            for k in range(CONV_WIDTH):
                src = nv + g0 + k - (CONV_WIDTH - 1)
                term = a_col[8 * src:8 * (src + CONV_MG), :] * w_col[k:k + 1, :]
                acc = term if acc is None else acc + term
            c_buf[lc, 8 * g0:8 * (g0 + CONV_MG), :] = acc + bdw_ref[lc]
        return carry
    lax.fori_loop(0, D_CONV // LANE, conv_column, 0)

    c = jnp.concatenate([c_buf[lc] for lc in range(D_CONV // LANE)], axis=-1)
    mu = jnp.mean(c, axis=-1, keepdims=True)
    xc = c - mu
    var = jnp.mean(xc * xc, axis=-1, keepdims=True)
    y = xc * lax.rsqrt(var + NORM_EPS) * gcln_ref[...] + bcln_ref[...]
    y = y * jax.nn.sigmoid(y)
    branch_a = _dot(y.astype(BF16), wco_ref[...]) + bco_ref[...]

    for r0 in range(0, tm, ROW_CHUNK):
        row = r0 + lax.broadcasted_iota(jnp.int32, (ROW_CHUNK, POOL_GROUP), 0)
        t1 = i * tm + (row % 8) * nv + row // 8 + 1
        for gi, w in enumerate(POOL_WINDOWS):
            ls = slice(gi * POOL_GROUP, (gi + 1) * POOL_GROUP)
            tok = u_ext[tm + r0:tm + r0 + ROW_CHUNK, ls]
            s = tok
            for j in range(1, w):
                s = s + u_ext[tm + r0 - 8 * j:tm + r0 - 8 * j + ROW_CHUNK, ls]
            cnt = jnp.minimum(t1, w).astype(F32)
            q_buf[r0:r0 + ROW_CHUNK, ls] = s / cnt - tok

    qs_out = []
    for gi in range(len(POOL_WINDOWS)):
        ls = slice(gi * POOL_GROUP, (gi + 1) * POOL_GROUP)
        qs_out.append(_dot(q_buf[:, ls].astype(BF16), wpool_ref[gi]) * spool_ref[:, ls])
    branch_b = jnp.concatenate(qs_out, axis=-1)

    c2 = 2 * D_CONV + D_POOL
    gate_a = jax.nn.sigmoid(proj(c2, c2 + D_MODEL))
    gate_b = jax.nn.sigmoid(proj(c2 + D_MODEL, c2 + 2 * D_MODEL))
    merged = gate_a * branch_a + gate_b * branch_b
    merged = _dot(unperm_ref[...], merged.astype(BF16)).astype(BF16)
    x1 = x + _dot(merged, wout_ref[...])
    x1_ref[...] = x1
    h2_ref[...] = _rms(x1, gffn_ref[...]).astype(BF16)


def _const_spec(shape):
    n = len(shape)
    return pl.BlockSpec(shape, lambda i, _n=n: (0,) * _n)


def _mixer(x, seq_len, g_mix, w_in, b_in, w_dw, b_dw, g_cln, b_cln, w_co, b_co, w_pool, s_pool, w_out,
           g_ffn):
    t = x.shape[0]
    tm = MIX_TM
    assert seq_len % tm == 0 and MIX_NV >= CONV_WIDTH and MIX_NV >= max(POOL_WINDOWS)
    d_in = w_in.shape[1]
    row = pl.BlockSpec((tm, D_MODEL), lambda i: (i, 0))
    n_col = D_CONV // LANE
    w_dw = w_dw.reshape(CONV_WIDTH, n_col, LANE).transpose(1, 0, 2)
    b_dw = b_dw.reshape(n_col, 1, LANE)
    r = jnp.arange(tm)
    perm = ((r % 8) * MIX_NV + r // 8)[:, None] == jnp.arange(tm)[None, :]
    perm = perm.astype(BF16)
    return pl.pallas_call(
        functools.partial(_mixer_kernel, tiles_per_seq=seq_len // tm),
        grid=(t // tm,),
        in_specs=[
            row,
            _const_spec((1, D_MODEL)),
            _const_spec((D_MODEL, d_in)),
            _const_spec((1, d_in)),
            _const_spec((n_col, CONV_WIDTH, LANE)),
            _const_spec((n_col, 1, LANE)),
            _const_spec((1, D_CONV)),
            _const_spec((1, D_CONV)),
            _const_spec((D_CONV, D_MODEL)),
            _const_spec((1, D_MODEL)),
            _const_spec((len(POOL_WINDOWS), POOL_GROUP, POOL_GROUP)),
            _const_spec((1, D_POOL)),
            _const_spec((D_MODEL, D_MODEL)),
            _const_spec((1, D_MODEL)),
            _const_spec((tm, tm)),
            _const_spec((tm, tm)),
        ],
        out_specs=[row, row],
        out_shape=[jax.ShapeDtypeStruct((t, D_MODEL), F32),
                   jax.ShapeDtypeStruct((t, D_MODEL), BF16)],
        scratch_shapes=[
            pltpu.VMEM((n_col, 2 * tm, LANE), F32),
            pltpu.VMEM((n_col, tm, LANE), F32),
            pltpu.VMEM((2 * tm, D_POOL), F32),
            pltpu.VMEM((tm, D_POOL), F32),
            pltpu.VMEM((n_col, tm, LANE), F32),
            pltpu.VMEM((tm, D_POOL), F32),
        ],
        compiler_params=pltpu.CompilerParams(
            dimension_semantics=("arbitrary",), vmem_limit_bytes=VMEM_LIMIT),
        name="mixer",
    )(x, g_mix, w_in, b_in, w_dw, b_dw, g_cln, b_cln, w_co, b_co, w_pool, s_pool, w_out, g_ffn, perm, perm.T)


def _beats(v, other, other_is_later):
    v = jnp.broadcast_to(v, other.shape)
    return jnp.where(other_is_later, jnp.where(v >= other, 1, 0), jnp.where(v > other, 1, 0))


def _router_kernel(h2_ref, wrt_ref, br_ref, utri_ref, ltri_ref, gate_ref, rank_ref, pos_ref, cnt_ref):
    tm = ROUTER_TM
    logits = lax.dot_general(wrt_ref[...], h2_ref[...], (((1,), (1,)), ((), ())),
                             preferred_element_type=F32)
    scores = jax.nn.sigmoid(logits)
    sel = scores + br_ref[...]
    shape3 = (N_GROUPS, GROUP_SIZE, tm)
    sel3 = sel.reshape(shape3)
    scores3 = scores.reshape(shape3)
    neg_inf = jnp.float32(-jnp.inf)

    member = lax.broadcasted_iota(jnp.int32, shape3, 1)
    m1 = jnp.max(sel3, axis=1, keepdims=True)
    first = jnp.min(jnp.where(sel3 == m1, member, GROUP_SIZE), axis=1, keepdims=True)
    m2 = jnp.max(jnp.where(member == first, neg_inf, sel3), axis=1, keepdims=True)
    gscore = jnp.broadcast_to(m1 + m2, shape3)

    gidx = lax.broadcasted_iota(jnp.int32, shape3, 0)
    grank = jnp.zeros(shape3, jnp.int32)
    for j in range(N_GROUPS):
        sj = gscore[j:j + 1]
        grank = grank + _beats(sj, gscore, gidx > j)
    masked = jnp.where(grank < TOPK_GROUPS, sel3, neg_inf)

    eidx = gidx * GROUP_SIZE + member
    work = masked
    erank = jnp.full(shape3, TOP_K, jnp.int32)
    for k in range(TOP_K):
        best = jnp.max(jnp.max(work, axis=0, keepdims=True), axis=1, keepdims=True)
        cand = jnp.where(work == best, eidx, N_EXPERTS)
        pick = jnp.min(jnp.min(cand, axis=0, keepdims=True), axis=1, keepdims=True)
        hit = eidx == pick
        work = jnp.where(hit, neg_inf, work)
        erank = jnp.where(hit, k, erank)
    chosen = erank < TOP_K
    top_s = jnp.where(chosen, scores3, 0.0)
    denom = jnp.sum(jnp.sum(top_s, axis=0, keepdims=True), axis=1, keepdims=True)
    gates3 = top_s / denom * ROUTED_SCALE
    chosen2 = jnp.where(chosen, 1.0, 0.0).reshape(N_EXPERTS, tm)
    gate_ref[...] = gates3.reshape(N_EXPERTS, tm).astype(BF16)

    for w in range(tm // WIN):
        ls = slice(w * WIN, (w + 1) * WIN)
        mw = chosen2[:, ls]
        rank = _dot(mw.astype(BF16), utri_ref[...])
        n = jnp.sum(mw, axis=1, keepdims=True)
        run = jnp.floor((n + 7.0) * 0.125) * 8.0
        start = _dot(ltri_ref[...], jnp.broadcast_to(run, (N_EXPERTS, WIN)).astype(BF16))
        rank_ref[:, ls] = jnp.where(mw > 0.5, rank, -1.0).astype(BF16)
        row3 = (rank + start).reshape(N_GROUPS, GROUP_SIZE, WIN)
        er = erank[:, :, ls]
        for k in range(TOP_K):
            pk = jnp.sum(jnp.sum(jnp.where(er == k, row3, 0.0), axis=0, keepdims=True), axis=1, keepdims=True)
            pos_ref[k:k + 1, ls] = pk.reshape(1, WIN).astype(jnp.int32)
        cnt_ref[w] = n


def _router(h2, w_rt, b_r):
    t = h2.shape[0]
    tm = ROUTER_TM
    utri = jnp.triu(jnp.ones((WIN, WIN), BF16), k=1)
    ltri = jnp.tril(jnp.ones((N_EXPERTS, N_EXPERTS), BF16), k=-1)
    return pl.pallas_call(
        _router_kernel,
        grid=(t // tm,),
        in_specs=[
            pl.BlockSpec((tm, D_MODEL), lambda i: (i, 0)),
            _const_spec((N_EXPERTS, D_MODEL)),
            _const_spec((N_EXPERTS, 1)),
            _const_spec((WIN, WIN)),
            _const_spec((N_EXPERTS, N_EXPERTS)),
        ],
        out_specs=[
            pl.BlockSpec((N_EXPERTS, tm), lambda i: (0, i)),
            pl.BlockSpec((N_EXPERTS, tm), lambda i: (0, i)),
            pl.BlockSpec((TOP_K, tm), lambda i: (0, i)),
            pl.BlockSpec((tm // WIN, N_EXPERTS, 1), lambda i: (i, 0, 0)),
        ],
        out_shape=[
            jax.ShapeDtypeStruct((N_EXPERTS, t), BF16),
            jax.ShapeDtypeStruct((N_EXPERTS, t), BF16),
            jax.ShapeDtypeStruct((TOP_K, t), jnp.int32),
            jax.ShapeDtypeStruct((t // WIN, N_EXPERTS, 1), F32),
        ],
        compiler_params=pltpu.CompilerParams(
            dimension_semantics=("arbitrary",), vmem_limit_bytes=VMEM_LIMIT),
        name="router",
    )(h2, w_rt, b_r, utri, ltri)


def _sorted_rows_bound(t):
    rows = t * TOP_K + (t // WIN) * N_EXPERTS * 7 + N_EXPERTS * (EXP_BM - 1)
    return -(-rows // EXP_BM) * EXP_BM


def _dispatch_plan(cnt, t):
    nw = t // WIN
    n = cnt.reshape(nw, N_EXPERTS).astype(jnp.int32)
    run = (n + 7) // 8 * 8
    local_end = jnp.cumsum(run, axis=1)
    local_off = jnp.concatenate([jnp.zeros((nw, 1), jnp.int32), local_end], axis=1)
    total = jnp.sum(run, axis=0)
    region = (total + EXP_BM - 1) // EXP_BM * EXP_BM
    region_end = jnp.cumsum(region)
    base = region_end - region
    global_off = base[None, :] + jnp.cumsum(run, axis=0) - run
    n_blocks = _sorted_rows_bound(t) // EXP_BM
    n_used = region_end[-1] // EXP_BM
    blk = jnp.arange(n_blocks, dtype=jnp.int32)
    blk_expert = jnp.sum((region_end[None, :] <= blk[:, None] * EXP_BM).astype(jnp.int32), axis=1)
    blk_expert = jnp.minimum(blk_expert, N_EXPERTS - 1)
    eid = jnp.arange(N_EXPERTS, dtype=jnp.int32)
    later_nonempty = (eid[None, :] > eid[:, None]) & (region[None, :] > 0)
    next_expert = jnp.min(jnp.where(later_nonempty, eid[None, :], N_EXPERTS), axis=1).astype(jnp.int32)
    return dict(
        run_lo=local_off[:, :N_EXPERTS].reshape(nw, N_EXPERTS, 1),
        run_hi=local_off[:, 1:].reshape(nw, N_EXPERTS, 1),
        local_off=local_off.reshape(-1), global_off=global_off.reshape(-1),
        fill_off=base + total, fill_cnt=region - total,
        blk_expert=blk_expert.astype(jnp.int32), next_expert=next_expert,
        n_used=n_used.reshape(1).astype(jnp.int32))


def _run_copy(local_ref, global_ref, win, e, vmem_buf, slot, hbm_buf, sem, to_hbm):
    lo = pl.multiple_of(local_ref[win * (N_EXPERTS + 1) + e], 8)
    cnt = pl.multiple_of(local_ref[win * (N_EXPERTS + 1) + e + 1] - lo, 8)
    go = pl.multiple_of(global_ref[win * N_EXPERTS + e], 8)
    v = vmem_buf.at[pl.ds(pl.multiple_of(slot * SEL_ROWS + lo, 8), cnt)]
    h = hbm_buf.at[pl.ds(go, cnt)]
    cp = pltpu.make_async_copy(v, h, sem.at[slot]) if to_hbm else pltpu.make_async_copy(h, v, sem.at[slot])
    return cnt, cp


def _start_runs(local_ref, global_ref, win, vmem_buf, slot, hbm_buf, sem, to_hbm):
    def body(e, carry):
        cnt, cp = _run_copy(local_ref, global_ref, win, e, vmem_buf, slot, hbm_buf, sem, to_hbm)

        @pl.when(cnt > 0)
        def _():
            cp.start()
        return carry
    lax.fori_loop(0, N_EXPERTS, body, 0)


def _wait_runs(local_ref, win, vmem_buf, slot, hbm_buf, sem, to_hbm):
    total = pl.multiple_of(local_ref[win * (N_EXPERTS + 1) + N_EXPERTS], 8)
    v = vmem_buf.at[pl.ds(pl.multiple_of(slot * SEL_ROWS, 8), total)]
    h = hbm_buf.at[pl.ds(0, total)]
    cp = pltpu.make_async_copy(v, h, sem.at[slot]) if to_hbm else pltpu.make_async_copy(h, v, sem.at[slot])

    @pl.when(total > 0)
    def _():
        cp.wait()


def _dispatch_kernel(local_ref, global_ref, fill_off_ref, fill_cnt_ref, h2_ref, pos_ref, xs_hbm, sbuf,
                     s_ref, sem, zsem, *, n_win):
    w = pl.program_id(0)
    slot = w % 2
    pos = pos_ref[...]

    h2 = h2_ref[...]
    for g in range(SEL_ROWS // SEL_MM):
        for sg in range(SEL_MM // SEL_RG):
            r0 = g * SEL_MM + sg * SEL_RG
            rid = r0 + lax.broadcasted_iota(jnp.int32, (SEL_RG, WIN), 0)
            acc = jnp.zeros((SEL_RG, WIN), F32)
            for k in range(TOP_K):
                acc = jnp.where(rid == pos[k:k + 1, :], 1.0, acc)
            s_ref[r0:r0 + SEL_RG, :] = acc.astype(BF16)
        rows = slice(g * SEL_MM, (g + 1) * SEL_MM)
        dst = pl.multiple_of(slot * SEL_ROWS + g * SEL_MM, SEL_MM)
        sbuf[pl.ds(dst, SEL_MM), :] = _dot(s_ref[rows, :], h2).astype(BF16)

    _start_runs(local_ref, global_ref, w, sbuf, slot, xs_hbm, sem, True)

    @pl.when(w > 0)
    def _():
        _wait_runs(local_ref, w - 1, sbuf, 1 - slot, xs_hbm, sem, True)

    @pl.when(w == n_win - 1)
    def _():
        sbuf[2 * SEL_ROWS:, :] = jnp.zeros((EXP_BM, D_MODEL), BF16)

        def fill(e, wait):
            cnt = pl.multiple_of(fill_cnt_ref[e], 8)
            off = pl.multiple_of(fill_off_ref[e], 8)
            cp = pltpu.make_async_copy(sbuf.at[pl.ds(2 * SEL_ROWS, cnt)], xs_hbm.at[pl.ds(off, cnt)], zsem)

            @pl.when(cnt > 0)
            def _():
                if wait:
                    cp.wait()
                else:
                    cp.start()

        def start_body(e, carry):
            fill(e, False)
            return carry

        def wait_body(e, carry):
            fill(e, True)
            return carry
        lax.fori_loop(0, N_EXPERTS, start_body, 0)
        _wait_runs(local_ref, w, sbuf, slot, xs_hbm, sem, True)
        lax.fori_loop(0, N_EXPERTS, wait_body, 0)


def _staging_shape(extra_rows):
    return jax.ShapeDtypeStruct((2 * SEL_ROWS + extra_rows, D_MODEL), BF16)


def _staging_spec(extra_rows):
    return pl.BlockSpec((2 * SEL_ROWS + extra_rows, D_MODEL), lambda w, *_: (0, 0))


def _dispatch(plan, h2, pos):
    t = h2.shape[0]
    n_win = t // WIN
    return pl.pallas_call(
        functools.partial(_dispatch_kernel, n_win=n_win),
        grid_spec=pltpu.PrefetchScalarGridSpec(
            num_scalar_prefetch=4,
            grid=(n_win,),
            in_specs=[
                pl.BlockSpec((WIN, D_MODEL), lambda w, *_: (w, 0)),
                pl.BlockSpec((TOP_K, WIN), lambda w, *_: (0, w)),
            ],
            out_specs=[pl.BlockSpec(memory_space=pl.ANY), _staging_spec(EXP_BM)],
            scratch_shapes=[
                pltpu.VMEM((SEL_ROWS, WIN), BF16),
                pltpu.SemaphoreType.DMA((2,)),
                pltpu.SemaphoreType.DMA,
            ]),
        out_shape=[jax.ShapeDtypeStruct((_sorted_rows_bound(t), D_MODEL), BF16), _staging_shape(EXP_BM)],
        compiler_params=pltpu.CompilerParams(
            dimension_semantics=("arbitrary",), vmem_limit_bytes=VMEM_LIMIT),
        name="dispatch",
    )(plan['local_off'], plan['global_off'], plan['fill_off'], plan['fill_cnt'], h2, pos)[0]


def _expert_kernel(blk_expert_ref, next_expert_ref, n_used_ref, xs_hbm, wg_hbm, wu_hbm, wd_hbm, ys_hbm,
                   xbuf, ybuf, wg_st, wu_st, wd_st, wg_bf, wu_bf, wd_bf, xsem, ysem, wsem):
    n_used = n_used_ref[0]
    part = EXP_BM // EXP_SPLIT

    def row_copies(b, slot, fetch):
        out = []
        for q in range(EXP_SPLIT):
            hbm_rows = pl.ds(pl.multiple_of(b * EXP_BM + q * part, part), part)
            if fetch:
                out.append(pltpu.make_async_copy(xs_hbm.at[hbm_rows], xbuf.at[slot, q * part:(q + 1) * part],
                                                 xsem.at[slot]))
            else:
                out.append(pltpu.make_async_copy(ybuf.at[slot, q * part:(q + 1) * part], ys_hbm.at[hbm_rows],
                                                 ysem.at[slot]))
        return out

    def weight_copies(e, slot):
        return [pltpu.make_async_copy(wg_hbm.at[e], wg_st.at[slot], wsem.at[slot]),
                pltpu.make_async_copy(wu_hbm.at[e], wu_st.at[slot], wsem.at[slot]),
                pltpu.make_async_copy(wd_hbm.at[e], wd_st.at[slot], wsem.at[slot])]

    def start(copies):
        for c in copies:
            c.start()

    def wait(copies):
        for c in copies:
            c.wait()

    for ahead in range(EXP_XDEPTH - 1):
        @pl.when(ahead < n_used)
        def _(ahead=ahead):
            start(row_copies(ahead, ahead, True))

    @pl.when(n_used > 0)
    def _():
        start(weight_copies(blk_expert_ref[0], 0))

    def body(b, wslot):
        e = blk_expert_ref[b]
        new_expert = jnp.logical_or(b == 0, e != blk_expert_ref[jnp.maximum(b - 1, 0)])

        @pl.when(new_expert)
        def _():
            wait(weight_copies(e, wslot))
            wg_bf[...] = wg_st[wslot].astype(BF16)
            wu_bf[...] = wu_st[wslot].astype(BF16)
            wd_bf[...] = wd_st[wslot].astype(BF16)
            nxt = next_expert_ref[e]

            @pl.when(nxt < N_EXPERTS)
            def _():
                start(weight_copies(nxt, 1 - wslot))

        @pl.when(b + EXP_XDEPTH - 1 < n_used)
        def _():
            start(row_copies(b + EXP_XDEPTH - 1, (b + EXP_XDEPTH - 1) % EXP_XDEPTH, True))

        xslot = b % EXP_XDEPTH
        yslot = b % EXP_YDEPTH
        wait(row_copies(b, xslot, True))

        @pl.when(b >= EXP_YDEPTH)
        def _():
            wait(row_copies(b - EXP_YDEPTH, yslot, False))

        x = xbuf[xslot]
        hg = _dot(x, wg_bf[...])
        hb = hg * jax.nn.sigmoid(hg) * _dot(x, wu_bf[...])
        ybuf[yslot] = _dot(hb.astype(BF16), wd_bf[...]).astype(BF16)
        start(row_copies(b, yslot, False))
        return jnp.where(new_expert, 1 - wslot, wslot)

    lax.fori_loop(0, n_used, body, jnp.int32(0))

    for back in range(EXP_YDEPTH, 0, -1):
        @pl.when(n_used >= back)
        def _(back=back):
            wait(row_copies(n_used - back, (n_used - back) % EXP_YDEPTH, False))


def _experts(plan, xs, w_gate, w_up, w_down):
    any_spec = pl.BlockSpec(memory_space=pl.ANY)
    return pl.pallas_call(
        _expert_kernel,
        grid_spec=pltpu.PrefetchScalarGridSpec(
            num_scalar_prefetch=3,
            grid=(1,),
            in_specs=[any_spec, any_spec, any_spec, any_spec],
            out_specs=any_spec,
            scratch_shapes=[
                pltpu.VMEM((EXP_XDEPTH, EXP_BM, D_MODEL), BF16),
                pltpu.VMEM((EXP_YDEPTH, EXP_BM, D_MODEL), BF16),
                pltpu.VMEM((2, D_MODEL, D_EXPERT), F32),
                pltpu.VMEM((2, D_MODEL, D_EXPERT), F32),
                pltpu.VMEM((2, D_EXPERT, D_MODEL), F32),
                pltpu.VMEM((D_MODEL, D_EXPERT), BF16),
                pltpu.VMEM((D_MODEL, D_EXPERT), BF16),
                pltpu.VMEM((D_EXPERT, D_MODEL), BF16),
                pltpu.SemaphoreType.DMA((EXP_XDEPTH,)),
                pltpu.SemaphoreType.DMA((EXP_YDEPTH,)),
                pltpu.SemaphoreType.DMA((2,)),
            ]),
        out_shape=jax.ShapeDtypeStruct(xs.shape, BF16),
        compiler_params=pltpu.CompilerParams(
            dimension_semantics=("arbitrary",), vmem_limit_bytes=VMEM_LIMIT),
        name="experts",
    )(plan['blk_expert'], plan['next_expert'], plan['n_used'], xs, w_gate, w_up, w_down)


def _combine_kernel(local_ref, global_ref, x1_ref, h2_ref, p_ref, rank_ref, gate_ref, lo_ref, hi_ref,
                    wsg_ref, wsu_ref, wsd_ref, gple_ref, wpg_ref, wp_ref, gfin_ref, ys_hbm, o_ref, ybuf, st_ref, sem,
                    *, n_win, final_norm):
    w = pl.program_id(0)
    slot = w % 2

    @pl.when(w == 0)
    def _():
        ybuf[...] = jnp.zeros(ybuf.shape, BF16)
        _start_runs(local_ref, global_ref, w, ybuf, slot, ys_hbm, sem, False)

    @pl.when(w + 1 < n_win)
    def _():
        _start_runs(local_ref, global_ref, w + 1, ybuf, 1 - slot, ys_hbm, sem, False)

    lo = lo_ref[0]
    hi = hi_ref[0]
    lo_f = lo.astype(F32)
    rank_tm = rank_ref[...]
    gate_tm = gate_ref[...]

    def build_group(lg):
        cols = slice(lg * CMB_LG, (lg + 1) * CMB_LG)
        rid = lg * CMB_LG + lax.broadcasted_iota(jnp.int32, (N_EXPERTS, CMB_LG), 1)
        owner = jnp.where(rid >= lo, jnp.where(rid < hi, 1.0, 0.0), 0.0)
        run_row = rid[0:1, :].astype(F32) - jnp.sum(owner * lo_f, axis=0, keepdims=True)
        owner = owner.astype(BF16)
        hit = _dot(rank_tm, owner) == run_row
        st_ref[:, cols] = jnp.where(hit, _dot(gate_tm, owner), 0.0).astype(BF16)

    build_group(0)
    h2 = h2_ref[...]
    hs = _dot(h2, wsg_ref[...])
    hs = hs * jax.nn.sigmoid(hs) * _dot(h2, wsu_ref[...])
    shared = _dot(hs.astype(BF16), wsd_ref[...])

    _wait_runs(local_ref, w, ybuf, slot, ys_hbm, sem, False)
    routed = None
    n_groups = SEL_ROWS // CMB_LG
    for lg in range(n_groups):
        if lg + 1 < n_groups:
            build_group(lg + 1)
        src = pl.multiple_of(slot * SEL_ROWS + lg * CMB_LG, CMB_LG)
        part = _dot(st_ref[:, lg * CMB_LG:(lg + 1) * CMB_LG], ybuf[pl.ds(src, CMB_LG), :])
        routed = part if routed is None else routed + part
    x2 = x1_ref[...] + routed + shared

    hp = _rms(x2, gple_ref[...]).astype(BF16)
    gate = jax.nn.sigmoid(_dot(hp, wpg_ref[...]))
    x3 = x2 + gate * _dot(p_ref[...].astype(BF16), wp_ref[...])
    o_ref[...] = _rms(x3, gfin_ref[...]) if final_norm else x3


def _combine(plan, ys, x1, h2, p, rank_tm, gate_tm, wsg, wsu, wsd, g_ple, w_pg, w_p, g_fin, final_norm):
    t = x1.shape[0]
    n_win = t // WIN
    row = lambda width: pl.BlockSpec((WIN, width), lambda w, *_: (w, 0))
    const = lambda shape: pl.BlockSpec(shape, lambda w, *_: (0,) * len(shape))
    return pl.pallas_call(
        functools.partial(_combine_kernel, n_win=n_win, final_norm=final_norm),
        grid_spec=pltpu.PrefetchScalarGridSpec(
            num_scalar_prefetch=2,
            grid=(n_win,),
            in_specs=[
                row(D_MODEL), row(D_MODEL), row(PLE_DIM), row(N_EXPERTS), row(N_EXPERTS),
                pl.BlockSpec((1, N_EXPERTS, 1), lambda w, *_: (w, 0, 0)),
                pl.BlockSpec((1, N_EXPERTS, 1), lambda w, *_: (w, 0, 0)),
                const((D_MODEL, D_EXPERT)), const((D_MODEL, D_EXPERT)), const((D_EXPERT, D_MODEL)),
                const((1, D_MODEL)), const((D_MODEL, D_MODEL)), const((PLE_DIM, D_MODEL)),
                const((1, D_MODEL)),
                pl.BlockSpec(memory_space=pl.ANY),
            ],
            out_specs=[row(D_MODEL), _staging_spec(0)],
            scratch_shapes=[
                pltpu.VMEM((WIN, SEL_ROWS), BF16),
                pltpu.SemaphoreType.DMA((2,)),
            ]),
        out_shape=[jax.ShapeDtypeStruct((t, D_MODEL), F32), _staging_shape(0)],
        compiler_params=pltpu.CompilerParams(
            dimension_semantics=("arbitrary",), vmem_limit_bytes=VMEM_LIMIT),
        name="combine",
    )(plan['local_off'], plan['global_off'], x1, h2, p, rank_tm, gate_tm, plan['run_lo'], plan['run_hi'],
      wsg, wsu, wsd, g_ple, w_pg, w_p, g_fin, ys)[0]


def kernel(x, p, g_mix, w_in, b_in, w_dw, b_dw, g_cln, b_cln, w_conv_out, b_conv_out, w_pool, s_pool,
           w_out, g_ffn, w_router, b_router, w_e_gate, w_e_up, w_e_down, w_s_gate, w_s_up, w_s_down,
           g_ple, w_ple_gate, w_ple, g_final):
    bsz, s, d = x.shape
    t = bsz * s
    depth = w_in.shape[0]
    xt = x.reshape(t, d)
    row = lambda v: v.reshape(1, -1)
    for i in range(depth):
        x1, h2 = _mixer(
            xt, s, row(g_mix[i]), w_in[i].astype(BF16), row(b_in[i]), w_dw[i], row(b_dw[i]),
            row(g_cln[i]), row(b_cln[i]), w_conv_out[i].astype(BF16), row(b_conv_out[i]),
            w_pool[i].astype(BF16), row(s_pool[i]), w_out[i].astype(BF16), row(g_ffn[i]))
        gate, rank, pos, cnt = _router(h2, w_router[i].T.astype(BF16), b_router[i].reshape(N_EXPERTS, 1))
        plan = _dispatch_plan(cnt, t)
        xs = _dispatch(plan, h2, pos)
        ys = _experts(plan, xs, w_e_gate[i], w_e_up[i], w_e_down[i])
        xt = _combine(
            plan, ys, x1, h2, p[i].reshape(t, PLE_DIM), rank.T, gate.T,
            w_s_gate[i].astype(BF16), w_s_up[i].astype(BF16), w_s_down[i].astype(BF16),
            row(g_ple[i]), w_ple_gate[i].astype(BF16), w_ple[i].astype(BF16), row(g_final),
            final_norm=(i == depth - 1))
    return xt.reshape(bsz, s, d)
```

```python
import functools

import jax
import jax.numpy as jnp
from jax import lax
from jax.experimental import pallas as pl
from jax.experimental.pallas import tpu as pltpu

D_MODEL = 1024
D_CONV = 1024
D_POOL = 1024
CONV_WIDTH = 31
POOL_WINDOWS = (2, 4, 8, 16)
POOL_GROUP = 256
PLE_DIM = 256
N_EXPERTS = 64
N_GROUPS = 8
GROUP_SIZE = N_EXPERTS // N_GROUPS
TOPK_GROUPS = 4
TOP_K = 8
D_EXPERT = 256
ROUTED_SCALE = 2.5
NORM_EPS = 1e-6

F32 = jnp.float32
BF16 = jnp.bfloat16

MIX_TM = 256
MIX_NV = MIX_TM // 8
CONV_MG = 8
ROW_CHUNK = 64
LANE = 128

ROUTER_TM = 1024
WIN = 256
SEL_ROWS = 2560
SEL_RG = 64
SEL_MM = 512
EXP_BM = 512
EXP_XDEPTH = 6
EXP_YDEPTH = 3
EXP_SPLIT = 4
CMB_LG = 512

VMEM_LIMIT = 56 * 1024 * 1024


def _rms(x, g):
    ms = jnp.mean(x * x, axis=-1, keepdims=True)
    return x * lax.rsqrt(ms + NORM_EPS) * g


def _dot(a, b):
    return jnp.dot(a, b, preferred_element_type=F32)


def _mixer_kernel(x_ref, gmix_ref, win_ref, bin_ref, wdw_ref, bdw_ref, gcln_ref, bcln_ref,
                  wco_ref, bco_ref, wpool_ref, spool_ref, wout_ref, gffn_ref, perm_ref, unperm_ref,
                  x1_ref, h2_ref, a_ext, a_prev, u_ext, u_prev, c_buf, q_buf, *, tiles_per_seq):
    i = pl.program_id(0) % tiles_per_seq
    tm = MIX_TM
    nv = MIX_NV

    @pl.when(i == 0)
    def _():
        a_prev[...] = jnp.zeros(a_prev.shape, F32)
        u_prev[...] = jnp.zeros(u_prev.shape, F32)

    x = x_ref[...]
    h = _dot(perm_ref[...], _rms(x, gmix_ref[...]).astype(BF16)).astype(BF16)

    def proj(lo, hi):
        return _dot(h, win_ref[:, lo:hi]) + bin_ref[:, lo:hi]

    glu = proj(0, D_CONV) * jax.nn.sigmoid(proj(D_CONV, 2 * D_CONV))
    for lc in range(D_CONV // LANE):
        a_ext[lc, tm:2 * tm, :] = glu[:, lc * LANE:(lc + 1) * LANE]
    u_ext[tm:2 * tm, :] = proj(2 * D_CONV, 2 * D_CONV + D_POOL)

    def delayed_groups(ext, prev, first_group):
        last_row = lax.broadcasted_iota(jnp.int32, (8, ext.shape[-1]), 0) == 7
        for g in range(first_group, nv):
            rows = slice(8 * g, 8 * g + 8)
            mixed = jnp.where(last_row, prev[rows, :], ext[tm + 8 * g:tm + 8 * g + 8, :])
            ext[rows, :] = pltpu.roll(mixed, 1, axis=0)
            prev[rows, :] = ext[tm + 8 * g:tm + 8 * g + 8, :]

    delayed_groups(u_ext, u_prev, nv - (max(POOL_WINDOWS) - 1))

    def conv_column(lc, carry):
        a_col = a_ext.at[lc]
        delayed_groups(a_col, a_prev.at[lc], nv - (CONV_WIDTH - 1))
        w_col = wdw_ref.at[lc]
        for g0 in range(0, nv, CONV_MG):
            acc = None
            for k in range(CONV_WIDTH):
                src = nv + g0 + k - (CONV_WIDTH - 1)
                term = a_col[8 * src:8 * (src + CONV_MG), :] * w_col[k:k + 1, :]
                acc = term if acc is None else acc + term
            c_buf[lc, 8 * g0:8 * (g0 + CONV_MG), :] = acc + bdw_ref[lc]
        return carry
    lax.fori_loop(0, D_CONV // LANE, conv_column, 0)

    c = jnp.concatenate([c_buf[lc] for lc in range(D_CONV // LANE)], axis=-1)
    mu = jnp.mean(c, axis=-1, keepdims=True)
    xc = c - mu
    var = jnp.mean(xc * xc, axis=-1, keepdims=True)
    y = xc * lax.rsqrt(var + NORM_EPS) * gcln_ref[...] + bcln_ref[...]
    y = y * jax.nn.sigmoid(y)
    branch_a = _dot(y.astype(BF16), wco_ref[...]) + bco_ref[...]

    for r0 in range(0, tm, ROW_CHUNK):
        row = r0 + lax.broadcasted_iota(jnp.int32, (ROW_CHUNK, POOL_GROUP), 0)
        t1 = i * tm + (row % 8) * nv + row // 8 + 1
        for gi, w in enumerate(POOL_WINDOWS):
            ls = slice(gi * POOL_GROUP, (gi + 1) * POOL_GROUP)
            tok = u_ext[tm + r0:tm + r0 + ROW_CHUNK, ls]
            s = tok
            for j in range(1, w):
                s = s + u_ext[tm + r0 - 8 * j:tm + r0 - 8 * j + ROW_CHUNK, ls]
            cnt = jnp.minimum(t1, w).astype(F32)
            q_buf[r0:r0 + ROW_CHUNK, ls] = s / cnt - tok

    qs_out = []
    for gi in range(len(POOL_WINDOWS)):
        ls = slice(gi * POOL_GROUP, (gi + 1) * POOL_GROUP)
        qs_out.append(_dot(q_buf[:, ls].astype(BF16), wpool_ref[gi]) * spool_ref[:, ls])
    branch_b = jnp.concatenate(qs_out, axis=-1)

    c2 = 2 * D_CONV + D_POOL
    gate_a = jax.nn.sigmoid(proj(c2, c2 + D_MODEL))
    gate_b = jax.nn.sigmoid(proj(c2 + D_MODEL, c2 + 2 * D_MODEL))
    merged = gate_a * branch_a + gate_b * branch_b
    merged = _dot(unperm_ref[...], merged.astype(BF16)).astype(BF16)
    x1 = x + _dot(merged, wout_ref[...])
    x1_ref[...] = x1
    h2_ref[...] = _rms(x1, gffn_ref[...]).astype(BF16)


def _const_spec(shape):
    n = len(shape)
    return pl.BlockSpec(shape, lambda i, _n=n: (0,) * _n)


def _mixer(x, seq_len, g_mix, w_in, b_in, w_dw, b_dw, g_cln, b_cln, w_co, b_co, w_pool, s_pool, w_out,
           g_ffn):
    t = x.shape[0]
    tm = MIX_TM
    assert seq_len % tm == 0 and MIX_NV >= CONV_WIDTH and MIX_NV >= max(POOL_WINDOWS)
    d_in = w_in.shape[1]
    row = pl.BlockSpec((tm, D_MODEL), lambda i: (i, 0))
    n_col = D_CONV // LANE
    w_dw = w_dw.reshape(CONV_WIDTH, n_col, LANE).transpose(1, 0, 2)
    b_dw = b_dw.reshape(n_col, 1, LANE)
    r = jnp.arange(tm)
    perm = ((r % 8) * MIX_NV + r // 8)[:, None] == jnp.arange(tm)[None, :]
    perm = perm.astype(BF16)
    return pl.pallas_call(
        functools.partial(_mixer_kernel, tiles_per_seq=seq_len // tm),
        grid=(t // tm,),
        in_specs=[
            row,
            _const_spec((1, D_MODEL)),
            _const_spec((D_MODEL, d_in)),
            _const_spec((1, d_in)),
            _const_spec((n_col, CONV_WIDTH, LANE)),
            _const_spec((n_col, 1, LANE)),
            _const_spec((1, D_CONV)),
            _const_spec((1, D_CONV)),
            _const_spec((D_CONV, D_MODEL)),
            _const_spec((1, D_MODEL)),
            _const_spec((len(POOL_WINDOWS), POOL_GROUP, POOL_GROUP)),
            _const_spec((1, D_POOL)),
            _const_spec((D_MODEL, D_MODEL)),
            _const_spec((1, D_MODEL)),
            _const_spec((tm, tm)),
            _const_spec((tm, tm)),
        ],
        out_specs=[row, row],
        out_shape=[jax.ShapeDtypeStruct((t, D_MODEL), F32),
                   jax.ShapeDtypeStruct((t, D_MODEL), BF16)],
        scratch_shapes=[
            pltpu.VMEM((n_col, 2 * tm, LANE), F32),
            pltpu.VMEM((n_col, tm, LANE), F32),
            pltpu.VMEM((2 * tm, D_POOL), F32),
            pltpu.VMEM((tm, D_POOL), F32),
            pltpu.VMEM((n_col, tm, LANE), F32),
            pltpu.VMEM((tm, D_POOL), F32),
        ],
        compiler_params=pltpu.CompilerParams(
            dimension_semantics=("arbitrary",), vmem_limit_bytes=VMEM_LIMIT),
        name="mixer",
    )(x, g_mix, w_in, b_in, w_dw, b_dw, g_cln, b_cln, w_co, b_co, w_pool, s_pool, w_out, g_ffn, perm, perm.T)


def _beats(v, other, other_is_later):
    v = jnp.broadcast_to(v, other.shape)
    return jnp.where(other_is_later, jnp.where(v >= other, 1, 0), jnp.where(v > other, 1, 0))


def _router_kernel(h2_ref, wrt_ref, br_ref, utri_ref, ltri_ref, gate_ref, rank_ref, pos_ref, cnt_ref):
    tm = ROUTER_TM
    logits = lax.dot_general(wrt_ref[...], h2_ref[...], (((1,), (1,)), ((), ())),
                             preferred_element_type=F32)
    scores = jax.nn.sigmoid(logits)
    sel = scores + br_ref[...]
    shape3 = (N_GROUPS, GROUP_SIZE, tm)
    sel3 = sel.reshape(shape3)
    scores3 = scores.reshape(shape3)
    neg_inf = jnp.float32(-jnp.inf)

    member = lax.broadcasted_iota(jnp.int32, shape3, 1)
    m1 = jnp.max(sel3, axis=1, keepdims=True)
    first = jnp.min(jnp.where(sel3 == m1, member, GROUP_SIZE), axis=1, keepdims=True)
    m2 = jnp.max(jnp.where(member == first, neg_inf, sel3), axis=1, keepdims=True)
    gscore = jnp.broadcast_to(m1 + m2, shape3)

    gidx = lax.broadcasted_iota(jnp.int32, shape3, 0)
    grank = jnp.zeros(shape3, jnp.int32)
    for j in range(N_GROUPS):
        sj = gscore[j:j + 1]
        grank = grank + _beats(sj, gscore, gidx > j)
    masked = jnp.where(grank < TOPK_GROUPS, sel3, neg_inf)

    eidx = gidx * GROUP_SIZE + member
    work = masked
    erank = jnp.full(shape3, TOP_K, jnp.int32)
    for k in range(TOP_K):
        best = jnp.max(jnp.max(work, axis=0, keepdims=True), axis=1, keepdims=True)
        cand = jnp.where(work == best, eidx, N_EXPERTS)
        pick = jnp.min(jnp.min(cand, axis=0, keepdims=True), axis=1, keepdims=True)
        hit = eidx == pick
        work = jnp.where(hit, neg_inf, work)
        erank = jnp.where(hit, k, erank)
    chosen = erank < TOP_K
    top_s = jnp.where(chosen, scores3, 0.0)
    denom = jnp.sum(jnp.sum(top_s, axis=0, keepdims=True), axis=1, keepdims=True)
    gates3 = top_s / denom * ROUTED_SCALE
    chosen2 = jnp.where(chosen, 1.0, 0.0).reshape(N_EXPERTS, tm)
    gate_ref[...] = gates3.reshape(N_EXPERTS, tm).astype(BF16)

    for w in range(tm // WIN):
        ls = slice(w * WIN, (w + 1) * WIN)
        mw = chosen2[:, ls]
        rank = _dot(mw.astype(BF16), utri_ref[...])
        n = jnp.sum(mw, axis=1, keepdims=True)
        run = jnp.floor((n + 7.0) * 0.125) * 8.0
        start = _dot(ltri_ref[...], jnp.broadcast_to(run, (N_EXPERTS, WIN)).astype(BF16))
        rank_ref[:, ls] = jnp.where(mw > 0.5, rank, -1.0).astype(BF16)
        row3 = (rank + start).reshape(N_GROUPS, GROUP_SIZE, WIN)
        er = erank[:, :, ls]
        for k in range(TOP_K):
            pk = jnp.sum(jnp.sum(jnp.where(er == k, row3, 0.0), axis=0, keepdims=True), axis=1, keepdims=True)
            pos_ref[k:k + 1, ls] = pk.reshape(1, WIN).astype(jnp.int32)
        cnt_ref[w] = n


def _router(h2, w_rt, b_r):
    t = h2.shape[0]
    tm = ROUTER_TM
    utri = jnp.triu(jnp.ones((WIN, WIN), BF16), k=1)
    ltri = jnp.tril(jnp.ones((N_EXPERTS, N_EXPERTS), BF16), k=-1)
    return pl.pallas_call(
        _router_kernel,
        grid=(t // tm,),
        in_specs=[
            pl.BlockSpec((tm, D_MODEL), lambda i: (i, 0)),
            _const_spec((N_EXPERTS, D_MODEL)),
            _const_spec((N_EXPERTS, 1)),
            _const_spec((WIN, WIN)),
            _const_spec((N_EXPERTS, N_EXPERTS)),
        ],
        out_specs=[
            pl.BlockSpec((N_EXPERTS, tm), lambda i: (0, i)),
            pl.BlockSpec((N_EXPERTS, tm), lambda i: (0, i)),
            pl.BlockSpec((TOP_K, tm), lambda i: (0, i)),
            pl.BlockSpec((tm // WIN, N_EXPERTS, 1), lambda i: (i, 0, 0)),
        ],
        out_shape=[
            jax.ShapeDtypeStruct((N_EXPERTS, t), BF16),
            jax.ShapeDtypeStruct((N_EXPERTS, t), BF16),
            jax.ShapeDtypeStruct((TOP_K, t), jnp.int32),
            jax.ShapeDtypeStruct((t // WIN, N_EXPERTS, 1), F32),
        ],
        compiler_params=pltpu.CompilerParams(
            dimension_semantics=("arbitrary",), vmem_limit_bytes=VMEM_LIMIT),
        name="router",
    )(h2, w_rt, b_r, utri, ltri)


def _sorted_rows_bound(t):
    rows = t * TOP_K + (t // WIN) * N_EXPERTS * 7 + N_EXPERTS * (EXP_BM - 1)
    return -(-rows // EXP_BM) * EXP_BM


def _dispatch_plan(cnt, t):
    nw = t // WIN
    n = cnt.reshape(nw, N_EXPERTS).astype(jnp.int32)
    run = (n + 7) // 8 * 8
    local_end = jnp.cumsum(run, axis=1)
    local_off = jnp.concatenate([jnp.zeros((nw, 1), jnp.int32), local_end], axis=1)
    total = jnp.sum(run, axis=0)
    region = (total + EXP_BM - 1) // EXP_BM * EXP_BM
    region_end = jnp.cumsum(region)
    base = region_end - region
    global_off = base[None, :] + jnp.cumsum(run, axis=0) - run
    n_blocks = _sorted_rows_bound(t) // EXP_BM
    n_used = region_end[-1] // EXP_BM
    blk = jnp.arange(n_blocks, dtype=jnp.int32)
    blk_expert = jnp.sum((region_end[None, :] <= blk[:, None] * EXP_BM).astype(jnp.int32), axis=1)
    blk_expert = jnp.minimum(blk_expert, N_EXPERTS - 1)
    eid = jnp.arange(N_EXPERTS, dtype=jnp.int32)
    later_nonempty = (eid[None, :] > eid[:, None]) & (region[None, :] > 0)
    next_expert = jnp.min(jnp.where(later_nonempty, eid[None, :], N_EXPERTS), axis=1).astype(jnp.int32)
    return dict(
        run_lo=local_off[:, :N_EXPERTS].reshape(nw, N_EXPERTS, 1),
        run_hi=local_off[:, 1:].reshape(nw, N_EXPERTS, 1),
        local_off=local_off.reshape(-1), global_off=global_off.reshape(-1),
        fill_off=base + total, fill_cnt=region - total,
        blk_expert=blk_expert.astype(jnp.int32), next_expert=next_expert,
        n_used=n_used.reshape(1).astype(jnp.int32))


def _run_copy(local_ref, global_ref, win, e, vmem_buf, slot, hbm_buf, sem, to_hbm):
    lo = pl.multiple_of(local_ref[win * (N_EXPERTS + 1) + e], 8)
    cnt = pl.multiple_of(local_ref[win * (N_EXPERTS + 1) + e + 1] - lo, 8)
    go = pl.multiple_of(global_ref[win * N_EXPERTS + e], 8)
    v = vmem_buf.at[pl.ds(pl.multiple_of(slot * SEL_ROWS + lo, 8), cnt)]
    h = hbm_buf.at[pl.ds(go, cnt)]
    cp = pltpu.make_async_copy(v, h, sem.at[slot]) if to_hbm else pltpu.make_async_copy(h, v, sem.at[slot])
    return cnt, cp


def _start_runs(local_ref, global_ref, win, vmem_buf, slot, hbm_buf, sem, to_hbm):
    def body(e, carry):
        cnt, cp = _run_copy(local_ref, global_ref, win, e, vmem_buf, slot, hbm_buf, sem, to_hbm)

        @pl.when(cnt > 0)
        def _():
            cp.start()
        return carry
    lax.fori_loop(0, N_EXPERTS, body, 0)


def _wait_runs(local_ref, win, vmem_buf, slot, hbm_buf, sem, to_hbm):
    total = pl.multiple_of(local_ref[win * (N_EXPERTS + 1) + N_EXPERTS], 8)
    v = vmem_buf.at[pl.ds(pl.multiple_of(slot * SEL_ROWS, 8), total)]
    h = hbm_buf.at[pl.ds(0, total)]
    cp = pltpu.make_async_copy(v, h, sem.at[slot]) if to_hbm else pltpu.make_async_copy(h, v, sem.at[slot])

    @pl.when(total > 0)
    def _():
        cp.wait()


def _dispatch_kernel(local_ref, global_ref, fill_off_ref, fill_cnt_ref, h2_ref, pos_ref, xs_hbm, sbuf,
                     s_ref, sem, zsem, *, n_win):
    w = pl.program_id(0)
    slot = w % 2
    pos = pos_ref[...]

    h2 = h2_ref[...]
    for g in range(SEL_ROWS // SEL_MM):
        for sg in range(SEL_MM // SEL_RG):
            r0 = g * SEL_MM + sg * SEL_RG
            rid = r0 + lax.broadcasted_iota(jnp.int32, (SEL_RG, WIN), 0)
            acc = jnp.zeros((SEL_RG, WIN), F32)
            for k in range(TOP_K):
                acc = jnp.where(rid == pos[k:k + 1, :], 1.0, acc)
            s_ref[r0:r0 + SEL_RG, :] = acc.astype(BF16)
        rows = slice(g * SEL_MM, (g + 1) * SEL_MM)
        dst = pl.multiple_of(slot * SEL_ROWS + g * SEL_MM, SEL_MM)
        sbuf[pl.ds(dst, SEL_MM), :] = _dot(s_ref[rows, :], h2).astype(BF16)

    _start_runs(local_ref, global_ref, w, sbuf, slot, xs_hbm, sem, True)

    @pl.when(w > 0)
    def _():
        _wait_runs(local_ref, w - 1, sbuf, 1 - slot, xs_hbm, sem, True)

    @pl.when(w == n_win - 1)
    def _():
        sbuf[2 * SEL_ROWS:, :] = jnp.zeros((EXP_BM, D_MODEL), BF16)

        def fill(e, wait):
            cnt = pl.multiple_of(fill_cnt_ref[e], 8)
            off = pl.multiple_of(fill_off_ref[e], 8)
            cp = pltpu.make_async_copy(sbuf.at[pl.ds(2 * SEL_ROWS, cnt)], xs_hbm.at[pl.ds(off, cnt)], zsem)

            @pl.when(cnt > 0)
            def _():
                if wait:
                    cp.wait()
                else:
                    cp.start()

        def start_body(e, carry):
            fill(e, False)
            return carry

        def wait_body(e, carry):
            fill(e, True)
            return carry
        lax.fori_loop(0, N_EXPERTS, start_body, 0)
        _wait_runs(local_ref, w, sbuf, slot, xs_hbm, sem, True)
        lax.fori_loop(0, N_EXPERTS, wait_body, 0)


def _staging_shape(extra_rows):
    return jax.ShapeDtypeStruct((2 * SEL_ROWS + extra_rows, D_MODEL), BF16)


def _staging_spec(extra_rows):
    return pl.BlockSpec((2 * SEL_ROWS + extra_rows, D_MODEL), lambda w, *_: (0, 0))


def _dispatch(plan, h2, pos):
    t = h2.shape[0]
    n_win = t // WIN
    return pl.pallas_call(
        functools.partial(_dispatch_kernel, n_win=n_win),
        grid_spec=pltpu.PrefetchScalarGridSpec(
            num_scalar_prefetch=4,
            grid=(n_win,),
            in_specs=[
                pl.BlockSpec((WIN, D_MODEL), lambda w, *_: (w, 0)),
                pl.BlockSpec((TOP_K, WIN), lambda w, *_: (0, w)),
            ],
            out_specs=[pl.BlockSpec(memory_space=pl.ANY), _staging_spec(EXP_BM)],
            scratch_shapes=[
                pltpu.VMEM((SEL_ROWS, WIN), BF16),
                pltpu.SemaphoreType.DMA((2,)),
                pltpu.SemaphoreType.DMA,
            ]),
        out_shape=[jax.ShapeDtypeStruct((_sorted_rows_bound(t), D_MODEL), BF16), _staging_shape(EXP_BM)],
        compiler_params=pltpu.CompilerParams(
            dimension_semantics=("arbitrary",), vmem_limit_bytes=VMEM_LIMIT),
        name="dispatch",
    )(plan['local_off'], plan['global_off'], plan['fill_off'], plan['fill_cnt'], h2, pos)[0]


def _expert_kernel(blk_expert_ref, next_expert_ref, n_used_ref, xs_hbm, wg_hbm, wu_hbm, wd_hbm, ys_hbm,
                   xbuf, ybuf, wg_st, wu_st, wd_st, wg_bf, wu_bf, wd_bf, xsem, ysem, wsem):
    n_used = n_used_ref[0]
    part = EXP_BM // EXP_SPLIT

    def row_copies(b, slot, fetch):
        out = []
        for q in range(EXP_SPLIT):
            hbm_rows = pl.ds(pl.multiple_of(b * EXP_BM + q * part, part), part)
            if fetch:
                out.append(pltpu.make_async_copy(xs_hbm.at[hbm_rows], xbuf.at[slot, q * part:(q + 1) * part],
                                                 xsem.at[slot]))
            else:
                out.append(pltpu.make_async_copy(ybuf.at[slot, q * part:(q + 1) * part], ys_hbm.at[hbm_rows],
                                                 ysem.at[slot]))
        return out

    def weight_copies(e, slot):
        return [pltpu.make_async_copy(wg_hbm.at[e], wg_st.at[slot], wsem.at[slot]),
                pltpu.make_async_copy(wu_hbm.at[e], wu_st.at[slot], wsem.at[slot]),
                pltpu.make_async_copy(wd_hbm.at[e], wd_st.at[slot], wsem.at[slot])]

    def start(copies):
        for c in copies:
            c.start()

    def wait(copies):
        for c in copies:
            c.wait()

    for ahead in range(EXP_XDEPTH - 1):
        @pl.when(ahead < n_used)
        def _(ahead=ahead):
            start(row_copies(ahead, ahead, True))

    @pl.when(n_used > 0)
    def _():
        start(weight_copies(blk_expert_ref[0], 0))

    def body(b, wslot):
        e = blk_expert_ref[b]
        new_expert = jnp.logical_or(b == 0, e != blk_expert_ref[jnp.maximum(b - 1, 0)])

        @pl.when(new_expert)
        def _():
            wait(weight_copies(e, wslot))
            wg_bf[...] = wg_st[wslot].astype(BF16)
            wu_bf[...] = wu_st[wslot].astype(BF16)
            wd_bf[...] = wd_st[wslot].astype(BF16)
            nxt = next_expert_ref[e]

            @pl.when(nxt < N_EXPERTS)
            def _():
                start(weight_copies(nxt, 1 - wslot))

        @pl.when(b + EXP_XDEPTH - 1 < n_used)
        def _():
            start(row_copies(b + EXP_XDEPTH - 1, (b + EXP_XDEPTH - 1) % EXP_XDEPTH, True))

        xslot = b % EXP_XDEPTH
        yslot = b % EXP_YDEPTH
        wait(row_copies(b, xslot, True))

        @pl.when(b >= EXP_YDEPTH)
        def _():
            wait(row_copies(b - EXP_YDEPTH, yslot, False))

        x = xbuf[xslot]
        hg = _dot(x, wg_bf[...])
        hb = hg * jax.nn.sigmoid(hg) * _dot(x, wu_bf[...])
        ybuf[yslot] = _dot(hb.astype(BF16), wd_bf[...]).astype(BF16)
        start(row_copies(b, yslot, False))
        return jnp.where(new_expert, 1 - wslot, wslot)

    lax.fori_loop(0, n_used, body, jnp.int32(0))

    for back in range(EXP_YDEPTH, 0, -1):
        @pl.when(n_used >= back)
        def _(back=back):
            wait(row_copies(n_used - back, (n_used - back) % EXP_YDEPTH, False))


def _experts(plan, xs, w_gate, w_up, w_down):
    any_spec = pl.BlockSpec(memory_space=pl.ANY)
    return pl.pallas_call(
        _expert_kernel,
        grid_spec=pltpu.PrefetchScalarGridSpec(
            num_scalar_prefetch=3,
            grid=(1,),
            in_specs=[any_spec, any_spec, any_spec, any_spec],
            out_specs=any_spec,
            scratch_shapes=[
                pltpu.VMEM((EXP_XDEPTH, EXP_BM, D_MODEL), BF16),
                pltpu.VMEM((EXP_YDEPTH, EXP_BM, D_MODEL), BF16),
                pltpu.VMEM((2, D_MODEL, D_EXPERT), F32),
                pltpu.VMEM((2, D_MODEL, D_EXPERT), F32),
                pltpu.VMEM((2, D_EXPERT, D_MODEL), F32),
                pltpu.VMEM((D_MODEL, D_EXPERT), BF16),
                pltpu.VMEM((D_MODEL, D_EXPERT), BF16),
                pltpu.VMEM((D_EXPERT, D_MODEL), BF16),
                pltpu.SemaphoreType.DMA((EXP_XDEPTH,)),
                pltpu.SemaphoreType.DMA((EXP_YDEPTH,)),
                pltpu.SemaphoreType.DMA((2,)),
            ]),
        out_shape=jax.ShapeDtypeStruct(xs.shape, BF16),
        compiler_params=pltpu.CompilerParams(
            dimension_semantics=("arbitrary",), vmem_limit_bytes=VMEM_LIMIT),
        name="experts",
    )(plan['blk_expert'], plan['next_expert'], plan['n_used'], xs, w_gate, w_up, w_down)


def _combine_kernel(local_ref, global_ref, x1_ref, h2_ref, p_ref, rank_ref, gate_ref, lo_ref, hi_ref,
                    wsg_ref, wsu_ref, wsd_ref, gple_ref, wpg_ref, wp_ref, gfin_ref, ys_hbm, o_ref, ybuf, st_ref, sem,
                    *, n_win, final_norm):
    w = pl.program_id(0)
    slot = w % 2

    @pl.when(w == 0)
    def _():
        ybuf[...] = jnp.zeros(ybuf.shape, BF16)
        _start_runs(local_ref, global_ref, w, ybuf, slot, ys_hbm, sem, False)

    @pl.when(w + 1 < n_win)
    def _():
        _start_runs(local_ref, global_ref, w + 1, ybuf, 1 - slot, ys_hbm, sem, False)

    lo = lo_ref[0]
    hi = hi_ref[0]
    lo_f = lo.astype(F32)
    rank_tm = rank_ref[...]
    gate_tm = gate_ref[...]

    def build_group(lg):
        cols = slice(lg * CMB_LG, (lg + 1) * CMB_LG)
        rid = lg * CMB_LG + lax.broadcasted_iota(jnp.int32, (N_EXPERTS, CMB_LG), 1)
        owner = jnp.where(rid >= lo, jnp.where(rid < hi, 1.0, 0.0), 0.0)
        run_row = rid[0:1, :].astype(F32) - jnp.sum(owner * lo_f, axis=0, keepdims=True)
        owner = owner.astype(BF16)
        hit = _dot(rank_tm, owner) == run_row
        st_ref[:, cols] = jnp.where(hit, _dot(gate_tm, owner), 0.0).astype(BF16)

    build_group(0)
    h2 = h2_ref[...]
    hs = _dot(h2, wsg_ref[...])
    hs = hs * jax.nn.sigmoid(hs) * _dot(h2, wsu_ref[...])
    shared = _dot(hs.astype(BF16), wsd_ref[...])

    _wait_runs(local_ref, w, ybuf, slot, ys_hbm, sem, False)
    routed = None
    n_groups = SEL_ROWS // CMB_LG
    for lg in range(n_groups):
        if lg + 1 < n_groups:
            build_group(lg + 1)
        src = pl.multiple_of(slot * SEL_ROWS + lg * CMB_LG, CMB_LG)
        part = _dot(st_ref[:, lg * CMB_LG:(lg + 1) * CMB_LG], ybuf[pl.ds(src, CMB_LG), :])
        routed = part if routed is None else routed + part
    x2 = x1_ref[...] + routed + shared

    hp = _rms(x2, gple_ref[...]).astype(BF16)
    gate = jax.nn.sigmoid(_dot(hp, wpg_ref[...]))
    x3 = x2 + gate * _dot(p_ref[...].astype(BF16), wp_ref[...])
    o_ref[...] = _rms(x3, gfin_ref[...]) if final_norm else x3


def _combine(plan, ys, x1, h2, p, rank_tm, gate_tm, wsg, wsu, wsd, g_ple, w_pg, w_p, g_fin, final_norm):
    t = x1.shape[0]
    n_win = t // WIN
    row = lambda width: pl.BlockSpec((WIN, width), lambda w, *_: (w, 0))
    const = lambda shape: pl.BlockSpec(shape, lambda w, *_: (0,) * len(shape))
    return pl.pallas_call(
        functools.partial(_combine_kernel, n_win=n_win, final_norm=final_norm),
        grid_spec=pltpu.PrefetchScalarGridSpec(
            num_scalar_prefetch=2,
            grid=(n_win,),
            in_specs=[
                row(D_MODEL), row(D_MODEL), row(PLE_DIM), row(N_EXPERTS), row(N_EXPERTS),
                pl.BlockSpec((1, N_EXPERTS, 1), lambda w, *_: (w, 0, 0)),
                pl.BlockSpec((1, N_EXPERTS, 1), lambda w, *_: (w, 0, 0)),
                const((D_MODEL, D_EXPERT)), const((D_MODEL, D_EXPERT)), const((D_EXPERT, D_MODEL)),
                const((1, D_MODEL)), const((D_MODEL, D_MODEL)), const((PLE_DIM, D_MODEL)),
                const((1, D_MODEL)),
                pl.BlockSpec(memory_space=pl.ANY),
            ],
            out_specs=[row(D_MODEL), _staging_spec(0)],
            scratch_shapes=[
                pltpu.VMEM((WIN, SEL_ROWS), BF16),
                pltpu.SemaphoreType.DMA((2,)),
            ]),
        out_shape=[jax.ShapeDtypeStruct((t, D_MODEL), F32), _staging_shape(0)],
        compiler_params=pltpu.CompilerParams(
            dimension_semantics=("arbitrary",), vmem_limit_bytes=VMEM_LIMIT),
        name="combine",
    )(plan['local_off'], plan['global_off'], x1, h2, p, rank_tm, gate_tm, plan['run_lo'], plan['run_hi'],
      wsg, wsu, wsd, g_ple, w_pg, w_p, g_fin, ys)[0]


def kernel(x, p, g_mix, w_in, b_in, w_dw, b_dw, g_cln, b_cln, w_conv_out, b_conv_out, w_pool, s_pool,
           w_out, g_ffn, w_router, b_router, w_e_gate, w_e_up, w_e_down, w_s_gate, w_s_up, w_s_down,
           g_ple, w_ple_gate, w_ple, g_final):
    bsz, s, d = x.shape
    t = bsz * s
    depth = w_in.shape[0]
    xt = x.reshape(t, d)
    row = lambda v: v.reshape(1, -1)
    for i in range(depth):
        x1, h2 = _mixer(
            xt, s, row(g_mix[i]), w_in[i].astype(BF16), row(b_in[i]), w_dw[i], row(b_dw[i]),
            row(g_cln[i]), row(b_cln[i]), w_conv_out[i].astype(BF16), row(b_conv_out[i]),
            w_pool[i].astype(BF16), row(s_pool[i]), w_out[i].astype(BF16), row(g_ffn[i]))
        gate, rank, pos, cnt = _router(h2, w_router[i].T.astype(BF16), b_router[i].reshape(N_EXPERTS, 1))
        plan = _dispatch_plan(cnt, t)
        xs = _dispatch(plan, h2, pos)
        ys = _experts(plan, xs, w_e_gate[i], w_e_up[i], w_e_down[i])
        xt = _combine(
            plan, ys, x1, h2, p[i].reshape(t, PLE_DIM), rank.T, gate.T,
            w_s_gate[i].astype(BF16), w_s_up[i].astype(BF16), w_s_down[i].astype(BF16),
            row(g_ple[i]), w_ple_gate[i].astype(BF16), w_ple[i].astype(BF16), row(g_final),
            final_norm=(i == depth - 1))
    return xt.reshape(bsz, s, d)
```

```python
import functools

import jax
import jax.numpy as jnp
from jax import lax
from jax.experimental import pallas as pl
from jax.experimental.pallas import tpu as pltpu

D_MODEL = 1024
D_CONV = 1024
D_POOL = 1024
CONV_WIDTH = 31
POOL_WINDOWS = (2, 4, 8, 16)
POOL_GROUP = 256
PLE_DIM = 256
N_EXPERTS = 64
N_GROUPS = 8
GROUP_SIZE = N_EXPERTS // N_GROUPS
TOPK_GROUPS = 4
TOP_K = 8
D_EXPERT = 256
ROUTED_SCALE = 2.5
NORM_EPS = 1e-6

F32 = jnp.float32
BF16 = jnp.bfloat16

MIX_TM = 256
MIX_NV = MIX_TM // 8
CONV_MG = 8
ROW_CHUNK = 64
LANE = 128

ROUTER_TM = 1024
WIN = 256
SEL_ROWS = 2560
SEL_RG = 64
SEL_MM = 512
EXP_BM = 512
EXP_XDEPTH = 6
EXP_YDEPTH = 3
EXP_SPLIT = 4
CMB_LG = 512

VMEM_LIMIT = 56 * 1024 * 1024


def _rms(x, g):
    ms = jnp.mean(x * x, axis=-1, keepdims=True)
    return x * lax.rsqrt(ms + NORM_EPS) * g


def _dot(a, b):
    return jnp.dot(a, b, preferred_element_type=F32)


def _mixer_kernel(x_ref, gmix_ref, win_ref, bin_ref, wdw_ref, bdw_ref, gcln_ref, bcln_ref,
                  wco_ref, bco_ref, wpool_ref, spool_ref, wout_ref, gffn_ref, perm_ref, unperm_ref,
                  x1_ref, h2_ref, a_ext, a_prev, u_ext, u_prev, c_buf, q_buf, *, tiles_per_seq):
    i = pl.program_id(0) % tiles_per_seq
    tm = MIX_TM
    nv = MIX_NV

    @pl.when(i == 0)
    def _():
        a_prev[...] = jnp.zeros(a_prev.shape, F32)
        u_prev[...] = jnp.zeros(u_prev.shape, F32)

    x = x_ref[...]
    h = _dot(perm_ref[...], _rms(x, gmix_ref[...]).astype(BF16)).astype(BF16)

    def proj(lo, hi):
        return _dot(h, win_ref[:, lo:hi]) + bin_ref[:, lo:hi]

    glu = proj(0, D_CONV) * jax.nn.sigmoid(proj(D_CONV, 2 * D_CONV))
    for lc in range(D_CONV // LANE):
        a_ext[lc, tm:2 * tm, :] = glu[:, lc * LANE:(lc + 1) * LANE]
    u_ext[tm:2 * tm, :] = proj(2 * D_CONV, 2 * D_CONV + D_POOL)

    def delayed_groups(ext, prev, first_group):
        last_row = lax.broadcasted_iota(jnp.int32, (8, ext.shape[-1]), 0) == 7
        for g in range(first_group, nv):
            rows = slice(8 * g, 8 * g + 8)
            mixed = jnp.where(last_row, prev[rows, :], ext[tm + 8 * g:tm + 8 * g + 8, :])
            ext[rows, :] = pltpu.roll(mixed, 1, axis=0)
            prev[rows, :] = ext[tm + 8 * g:tm + 8 * g + 8, :]

    delayed_groups(u_ext, u_prev, nv - (max(POOL_WINDOWS) - 1))

    def conv_column(lc, carry):
        a_col = a_ext.at[lc]
        delayed_groups(a_col, a_prev.at[lc], nv - (CONV_WIDTH - 1))
        w_col = wdw_ref.at[lc]
        for g0 in range(0, nv, CONV_MG):
            acc = None
            for k in range(CONV_WIDTH):
                src = nv + g0 + k - (CONV_WIDTH - 1)
                term = a_col[8 * src:8 * (src + CONV_MG), :] * w_col[k:k + 1, :]
                acc = term if acc is None else acc + term
            c_buf[lc, 8 * g0:8 * (g0 + CONV_MG), :] = acc + bdw_ref[lc]
        return carry
    lax.fori_loop(0, D_CONV // LANE, conv_column, 0)

    c = jnp.concatenate([c_buf[lc] for lc in range(D_CONV // LANE)], axis=-1)
    mu = jnp.mean(c, axis=-1, keepdims=True)
    xc = c - mu
    var = jnp.mean(xc * xc, axis=-1, keepdims=True)
    y = xc * lax.rsqrt(var + NORM_EPS) * gcln_ref[...] + bcln_ref[...]
    y = y * jax.nn.sigmoid(y)
    branch_a = _dot(y.astype(BF16), wco_ref[...]) + bco_ref[...]

    for r0 in range(0, tm, ROW_CHUNK):
        row = r0 + lax.broadcasted_iota(jnp.int32, (ROW_CHUNK, POOL_GROUP), 0)
        t1 = i * tm + (row % 8) * nv + row // 8 + 1
        for gi, w in enumerate(POOL_WINDOWS):
            ls = slice(gi * POOL_GROUP, (gi + 1) * POOL_GROUP)
            tok = u_ext[tm + r0:tm + r0 + ROW_CHUNK, ls]
            s = tok
            for j in range(1, w):
                s = s + u_ext[tm + r0 - 8 * j:tm + r0 - 8 * j + ROW_CHUNK, ls]
            cnt = jnp.minimum(t1, w).astype(F32)
            q_buf[r0:r0 + ROW_CHUNK, ls] = s / cnt - tok

    qs_out = []
    for gi in range(len(POOL_WINDOWS)):
        ls = slice(gi * POOL_GROUP, (gi + 1) * POOL_GROUP)
        qs_out.append(_dot(q_buf[:, ls].astype(BF16), wpool_ref[gi]) * spool_ref[:, ls])
    branch_b = jnp.concatenate(qs_out, axis=-1)

    c2 = 2 * D_CONV + D_POOL
    gate_a = jax.nn.sigmoid(proj(c2, c2 + D_MODEL))
    gate_b = jax.nn.sigmoid(proj(c2 + D_MODEL, c2 + 2 * D_MODEL))
    merged = gate_a * branch_a + gate_b * branch_b
    merged = _dot(unperm_ref[...], merged.astype(BF16)).astype(BF16)
    x1 = x + _dot(merged, wout_ref[...])
    x1_ref[...] = x1
    h2_ref[...] = _rms(x1, gffn_ref[...]).astype(BF16)


def _const_spec(shape):
    n = len(shape)
    return pl.BlockSpec(shape, lambda i, _n=n: (0,) * _n)


def _mixer(x, seq_len, g_mix, w_in, b_in, w_dw, b_dw, g_cln, b_cln, w_co, b_co, w_pool, s_pool, w_out,
           g_ffn):
    t = x.shape[0]
    tm = MIX_TM
    assert seq_len % tm == 0 and MIX_NV >= CONV_WIDTH and MIX_NV >= max(POOL_WINDOWS)
    d_in = w_in.shape[1]
    row = pl.BlockSpec((tm, D_MODEL), lambda i: (i, 0))
    n_col = D_CONV // LANE
    w_dw = w_dw.reshape(CONV_WIDTH, n_col, LANE).transpose(1, 0, 2)
    b_dw = b_dw.reshape(n_col, 1, LANE)
    r = jnp.arange(tm)
    perm = ((r % 8) * MIX_NV + r // 8)[:, None] == jnp.arange(tm)[None, :]
    perm = perm.astype(BF16)
    return pl.pallas_call(
        functools.partial(_mixer_kernel, tiles_per_seq=seq_len // tm),
        grid=(t // tm,),
        in_specs=[
            row,
            _const_spec((1, D_MODEL)),
            _const_spec((D_MODEL, d_in)),
            _const_spec((1, d_in)),
            _const_spec((n_col, CONV_WIDTH, LANE)),
            _const_spec((n_col, 1, LANE)),
            _const_spec((1, D_CONV)),
            _const_spec((1, D_CONV)),
            _const_spec((D_CONV, D_MODEL)),
            _const_spec((1, D_MODEL)),
            _const_spec((len(POOL_WINDOWS), POOL_GROUP, POOL_GROUP)),
            _const_spec((1, D_POOL)),
            _const_spec((D_MODEL, D_MODEL)),
            _const_spec((1, D_MODEL)),
            _const_spec((tm, tm)),
            _const_spec((tm, tm)),
        ],
        out_specs=[row, row],
        out_shape=[jax.ShapeDtypeStruct((t, D_MODEL), F32),
                   jax.ShapeDtypeStruct((t, D_MODEL), BF16)],
        scratch_shapes=[
            pltpu.VMEM((n_col, 2 * tm, LANE), F32),
            pltpu.VMEM((n_col, tm, LANE), F32),
            pltpu.VMEM((2 * tm, D_POOL), F32),
            pltpu.VMEM((tm, D_POOL), F32),
            pltpu.VMEM((n_col, tm, LANE), F32),
            pltpu.VMEM((tm, D_POOL), F32),
        ],
        compiler_params=pltpu.CompilerParams(
            dimension_semantics=("arbitrary",), vmem_limit_bytes=VMEM_LIMIT),
        name="mixer",
    )(x, g_mix, w_in, b_in, w_dw, b_dw, g_cln, b_cln, w_co, b_co, w_pool, s_pool, w_out, g_ffn, perm, perm.T)


def _beats(v, other, other_is_later):
    v = jnp.broadcast_to(v, other.shape)
    return jnp.where(other_is_later, jnp.where(v >= other, 1, 0), jnp.where(v > other, 1, 0))


def _router_kernel(h2_ref, wrt_ref, br_ref, utri_ref, ltri_ref, gate_ref, rank_ref, pos_ref, cnt_ref):
    tm = ROUTER_TM
    logits = lax.dot_general(wrt_ref[...], h2_ref[...], (((1,), (1,)), ((), ())),
                             preferred_element_type=F32)
    scores = jax.nn.sigmoid(logits)
    sel = scores + br_ref[...]
    shape3 = (N_GROUPS, GROUP_SIZE, tm)
    sel3 = sel.reshape(shape3)
    scores3 = scores.reshape(shape3)
    neg_inf = jnp.float32(-jnp.inf)

    member = lax.broadcasted_iota(jnp.int32, shape3, 1)
    m1 = jnp.max(sel3, axis=1, keepdims=True)
    first = jnp.min(jnp.where(sel3 == m1, member, GROUP_SIZE), axis=1, keepdims=True)
    m2 = jnp.max(jnp.where(member == first, neg_inf, sel3), axis=1, keepdims=True)
    gscore = jnp.broadcast_to(m1 + m2, shape3)

    gidx = lax.broadcasted_iota(jnp.int32, shape3, 0)
    grank = jnp.zeros(shape3, jnp.int32)
    for j in range(N_GROUPS):
        sj = gscore[j:j + 1]
        grank = grank + _beats(sj, gscore, gidx > j)
    masked = jnp.where(grank < TOPK_GROUPS, sel3, neg_inf)

    eidx = gidx * GROUP_SIZE + member
    work = masked
    erank = jnp.full(shape3, TOP_K, jnp.int32)
    for k in range(TOP_K):
        best = jnp.max(jnp.max(work, axis=0, keepdims=True), axis=1, keepdims=True)
        cand = jnp.where(work == best, eidx, N_EXPERTS)
        pick = jnp.min(jnp.min(cand, axis=0, keepdims=True), axis=1, keepdims=True)
        hit = eidx == pick
        work = jnp.where(hit, neg_inf, work)
        erank = jnp.where(hit, k, erank)
    chosen = erank < TOP_K
    top_s = jnp.where(chosen, scores3, 0.0)
    denom = jnp.sum(jnp.sum(top_s, axis=0, keepdims=True), axis=1, keepdims=True)
    gates3 = top_s / denom * ROUTED_SCALE
    chosen2 = jnp.where(chosen, 1.0, 0.0).reshape(N_EXPERTS, tm)
    gate_ref[...] = gates3.reshape(N_EXPERTS, tm).astype(BF16)

    for w in range(tm // WIN):
        ls = slice(w * WIN, (w + 1) * WIN)
        mw = chosen2[:, ls]
        rank = _dot(mw.astype(BF16), utri_ref[...])
        n = jnp.sum(mw, axis=1, keepdims=True)
        run = jnp.floor((n + 7.0) * 0.125) * 8.0
        start = _dot(ltri_ref[...], jnp.broadcast_to(run, (N_EXPERTS, WIN)).astype(BF16))
        rank_ref[:, ls] = jnp.where(mw > 0.5, rank, -1.0).astype(BF16)
        row3 = (rank + start).reshape(N_GROUPS, GROUP_SIZE, WIN)
        er = erank[:, :, ls]
        for k in range(TOP_K):
            pk = jnp.sum(jnp.sum(jnp.where(er == k, row3, 0.0), axis=0, keepdims=True), axis=1, keepdims=True)
            pos_ref[k:k + 1, ls] = pk.reshape(1, WIN).astype(jnp.int32)
        cnt_ref[w] = n


def _router(h2, w_rt, b_r):
    t = h2.shape[0]
    tm = ROUTER_TM
    utri = jnp.triu(jnp.ones((WIN, WIN), BF16), k=1)
    ltri = jnp.tril(jnp.ones((N_EXPERTS, N_EXPERTS), BF16), k=-1)
    return pl.pallas_call(
        _router_kernel,
        grid=(t // tm,),
        in_specs=[
            pl.BlockSpec((tm, D_MODEL), lambda i: (i, 0)),
            _const_spec((N_EXPERTS, D_MODEL)),
            _const_spec((N_EXPERTS, 1)),
            _const_spec((WIN, WIN)),
            _const_spec((N_EXPERTS, N_EXPERTS)),
        ],
        out_specs=[
            pl.BlockSpec((N_EXPERTS, tm), lambda i: (0, i)),
            pl.BlockSpec((N_EXPERTS, tm), lambda i: (0, i)),
            pl.BlockSpec((TOP_K, tm), lambda i: (0, i)),
            pl.BlockSpec((tm // WIN, N_EXPERTS, 1), lambda i: (i, 0, 0)),
        ],
        out_shape=[
            jax.ShapeDtypeStruct((N_EXPERTS, t), BF16),
            jax.ShapeDtypeStruct((N_EXPERTS, t), BF16),
            jax.ShapeDtypeStruct((TOP_K, t), jnp.int32),
            jax.ShapeDtypeStruct((t // WIN, N_EXPERTS, 1), F32),
        ],
        compiler_params=pltpu.CompilerParams(
            dimension_semantics=("arbitrary",), vmem_limit_bytes=VMEM_LIMIT),
        name="router",
    )(h2, w_rt, b_r, utri, ltri)


def _sorted_rows_bound(t):
    rows = t * TOP_K + (t // WIN) * N_EXPERTS * 7 + N_EXPERTS * (EXP_BM - 1)
    return -(-rows // EXP_BM) * EXP_BM


def _dispatch_plan(cnt, t):
    nw = t // WIN
    n = cnt.reshape(nw, N_EXPERTS).astype(jnp.int32)
    run = (n + 7) // 8 * 8
    local_end = jnp.cumsum(run, axis=1)
    local_off = jnp.concatenate([jnp.zeros((nw, 1), jnp.int32), local_end], axis=1)
    total = jnp.sum(run, axis=0)
    region = (total + EXP_BM - 1) // EXP_BM * EXP_BM
    region_end = jnp.cumsum(region)
    base = region_end - region
    global_off = base[None, :] + jnp.cumsum(run, axis=0) - run
    n_blocks = _sorted_rows_bound(t) // EXP_BM
    n_used = region_end[-1] // EXP_BM
    blk = jnp.arange(n_blocks, dtype=jnp.int32)
    blk_expert = jnp.sum((region_end[None, :] <= blk[:, None] * EXP_BM).astype(jnp.int32), axis=1)
    blk_expert = jnp.minimum(blk_expert, N_EXPERTS - 1)
    eid = jnp.arange(N_EXPERTS, dtype=jnp.int32)
    later_nonempty = (eid[None, :] > eid[:, None]) & (region[None, :] > 0)
    next_expert = jnp.min(jnp.where(later_nonempty, eid[None, :], N_EXPERTS), axis=1).astype(jnp.int32)
    return dict(
        run_lo=local_off[:, :N_EXPERTS].reshape(nw, N_EXPERTS, 1),
        run_hi=local_off[:, 1:].reshape(nw, N_EXPERTS, 1),
        local_off=local_off.reshape(-1), global_off=global_off.reshape(-1),
        fill_off=base + total, fill_cnt=region - total,
        blk_expert=blk_expert.astype(jnp.int32), next_expert=next_expert,
        n_used=n_used.reshape(1).astype(jnp.int32))


def _run_copy(local_ref, global_ref, win, e, vmem_buf, slot, hbm_buf, sem, to_hbm):
    lo = pl.multiple_of(local_ref[win * (N_EXPERTS + 1) + e], 8)
    cnt = pl.multiple_of(local_ref[win * (N_EXPERTS + 1) + e + 1] - lo, 8)
    go = pl.multiple_of(global_ref[win * N_EXPERTS + e], 8)
    v = vmem_buf.at[pl.ds(pl.multiple_of(slot * SEL_ROWS + lo, 8), cnt)]
    h = hbm_buf.at[pl.ds(go, cnt)]
    cp = pltpu.make_async_copy(v, h, sem.at[slot]) if to_hbm else pltpu.make_async_copy(h, v, sem.at[slot])
    return cnt, cp


def _start_runs(local_ref, global_ref, win, vmem_buf, slot, hbm_buf, sem, to_hbm):
    def body(e, carry):
        cnt, cp = _run_copy(local_ref, global_ref, win, e, vmem_buf, slot, hbm_buf, sem, to_hbm)

        @pl.when(cnt > 0)
        def _():
            cp.start()
        return carry
    lax.fori_loop(0, N_EXPERTS, body, 0)


def _wait_runs(local_ref, win, vmem_buf, slot, hbm_buf, sem, to_hbm):
    total = pl.multiple_of(local_ref[win * (N_EXPERTS + 1) + N_EXPERTS], 8)
    v = vmem_buf.at[pl.ds(pl.multiple_of(slot * SEL_ROWS, 8), total)]
    h = hbm_buf.at[pl.ds(0, total)]
    cp = pltpu.make_async_copy(v, h, sem.at[slot]) if to_hbm else pltpu.make_async_copy(h, v, sem.at[slot])

    @pl.when(total > 0)
    def _():
        cp.wait()


def _dispatch_kernel(local_ref, global_ref, fill_off_ref, fill_cnt_ref, h2_ref, pos_ref, xs_hbm, sbuf,
                     s_ref, sem, zsem, *, n_win):
    w = pl.program_id(0)
    slot = w % 2
    pos = pos_ref[...]

    h2 = h2_ref[...]
    assert SEL_RG <= 256
    rid_b = lax.broadcasted_iota(jnp.int32, (SEL_RG, WIN), 0).astype(F32).astype(BF16)
    one_b = jnp.ones((SEL_RG, WIN), BF16)
    for g in range(SEL_ROWS // SEL_MM):
        for sg in range(SEL_MM // SEL_RG):
            r0 = g * SEL_MM + sg * SEL_RG
            acc = jnp.zeros((SEL_RG, WIN), BF16)
            for k in range(TOP_K):
                off = (pos[k:k + 1, :] - r0).astype(F32)
                off = jnp.broadcast_to(off, (SEL_RG, WIN)).astype(BF16)
                acc = jnp.where(rid_b == off, one_b, acc)
            s_ref[r0:r0 + SEL_RG, :] = acc
        rows = slice(g * SEL_MM, (g + 1) * SEL_MM)
        dst = pl.multiple_of(slot * SEL_ROWS + g * SEL_MM, SEL_MM)
        sbuf[pl.ds(dst, SEL_MM), :] = _dot(s_ref[rows, :], h2).astype(BF16)

    _start_runs(local_ref, global_ref, w, sbuf, slot, xs_hbm, sem, True)

    @pl.when(w > 0)
    def _():
        _wait_runs(local_ref, w - 1, sbuf, 1 - slot, xs_hbm, sem, True)

    @pl.when(w == n_win - 1)
    def _():
        sbuf[2 * SEL_ROWS:, :] = jnp.zeros((EXP_BM, D_MODEL), BF16)

        def fill(e, wait):
            cnt = pl.multiple_of(fill_cnt_ref[e], 8)
            off = pl.multiple_of(fill_off_ref[e], 8)
            cp = pltpu.make_async_copy(sbuf.at[pl.ds(2 * SEL_ROWS, cnt)], xs_hbm.at[pl.ds(off, cnt)], zsem)

            @pl.when(cnt > 0)
            def _():
                if wait:
                    cp.wait()
                else:
                    cp.start()

        def start_body(e, carry):
            fill(e, False)
            return carry

        def wait_body(e, carry):
            fill(e, True)
            return carry
        lax.fori_loop(0, N_EXPERTS, start_body, 0)
        _wait_runs(local_ref, w, sbuf, slot, xs_hbm, sem, True)
        lax.fori_loop(0, N_EXPERTS, wait_body, 0)


def _staging_shape(extra_rows):
    return jax.ShapeDtypeStruct((2 * SEL_ROWS + extra_rows, D_MODEL), BF16)


def _staging_spec(extra_rows):
    return pl.BlockSpec((2 * SEL_ROWS + extra_rows, D_MODEL), lambda w, *_: (0, 0))


def _dispatch(plan, h2, pos):
    t = h2.shape[0]
    n_win = t // WIN
    return pl.pallas_call(
        functools.partial(_dispatch_kernel, n_win=n_win),
        grid_spec=pltpu.PrefetchScalarGridSpec(
            num_scalar_prefetch=4,
            grid=(n_win,),
            in_specs=[
                pl.BlockSpec((WIN, D_MODEL), lambda w, *_: (w, 0)),
                pl.BlockSpec((TOP_K, WIN), lambda w, *_: (0, w)),
            ],
            out_specs=[pl.BlockSpec(memory_space=pl.ANY), _staging_spec(EXP_BM)],
            scratch_shapes=[
                pltpu.VMEM((SEL_ROWS, WIN), BF16),
                pltpu.SemaphoreType.DMA((2,)),
                pltpu.SemaphoreType.DMA,
            ]),
        out_shape=[jax.ShapeDtypeStruct((_sorted_rows_bound(t), D_MODEL), BF16), _staging_shape(EXP_BM)],
        compiler_params=pltpu.CompilerParams(
            dimension_semantics=("arbitrary",), vmem_limit_bytes=VMEM_LIMIT),
        name="dispatch",
    )(plan['local_off'], plan['global_off'], plan['fill_off'], plan['fill_cnt'], h2, pos)[0]


def _expert_kernel(blk_expert_ref, next_expert_ref, n_used_ref, xs_hbm, wg_hbm, wu_hbm, wd_hbm, ys_hbm,
                   xbuf, ybuf, wg_st, wu_st, wd_st, wg_bf, wu_bf, wd_bf, xsem, ysem, wsem):
    n_used = n_used_ref[0]
    part = EXP_BM // EXP_SPLIT

    def row_copies(b, slot, fetch):
        out = []
        for q in range(EXP_SPLIT):
            hbm_rows = pl.ds(pl.multiple_of(b * EXP_BM + q * part, part), part)
            if fetch:
                out.append(pltpu.make_async_copy(xs_hbm.at[hbm_rows], xbuf.at[slot, q * part:(q + 1) * part],
                                                 xsem.at[slot]))
            else:
                out.append(pltpu.make_async_copy(ybuf.at[slot, q * part:(q + 1) * part], ys_hbm.at[hbm_rows],
                                                 ysem.at[slot]))
        return out

    def weight_copies(e, slot):
        return [pltpu.make_async_copy(wg_hbm.at[e], wg_st.at[slot], wsem.at[slot]),
                pltpu.make_async_copy(wu_hbm.at[e], wu_st.at[slot], wsem.at[slot]),
                pltpu.make_async_copy(wd_hbm.at[e], wd_st.at[slot], wsem.at[slot])]

    def start(copies):
        for c in copies:
            c.start()

    def wait(copies):
        for c in copies:
            c.wait()

    for ahead in range(EXP_XDEPTH - 1):
        @pl.when(ahead < n_used)
        def _(ahead=ahead):
            start(row_copies(ahead, ahead, True))

    @pl.when(n_used > 0)
    def _():
        start(weight_copies(blk_expert_ref[0], 0))

    def body(b, wslot):
        e = blk_expert_ref[b]
        new_expert = jnp.logical_or(b == 0, e != blk_expert_ref[jnp.maximum(b - 1, 0)])

        @pl.when(new_expert)
        def _():
            wait(weight_copies(e, wslot))
            wg_bf[...] = wg_st[wslot].astype(BF16)
            wu_bf[...] = wu_st[wslot].astype(BF16)
            wd_bf[...] = wd_st[wslot].astype(BF16)
            nxt = next_expert_ref[e]

            @pl.when(nxt < N_EXPERTS)
            def _():
                start(weight_copies(nxt, 1 - wslot))

        @pl.when(b + EXP_XDEPTH - 1 < n_used)
        def _():
            start(row_copies(b + EXP_XDEPTH - 1, (b + EXP_XDEPTH - 1) % EXP_XDEPTH, True))

        xslot = b % EXP_XDEPTH
        yslot = b % EXP_YDEPTH
        wait(row_copies(b, xslot, True))

        @pl.when(b >= EXP_YDEPTH)
        def _():
            wait(row_copies(b - EXP_YDEPTH, yslot, False))

        x = xbuf[xslot]
        hg = _dot(x, wg_bf[...])
        hb = hg * jax.nn.sigmoid(hg) * _dot(x, wu_bf[...])
        ybuf[yslot] = _dot(hb.astype(BF16), wd_bf[...]).astype(BF16)
        start(row_copies(b, yslot, False))
        return jnp.where(new_expert, 1 - wslot, wslot)

    lax.fori_loop(0, n_used, body, jnp.int32(0))

    for back in range(EXP_YDEPTH, 0, -1):
        @pl.when(n_used >= back)
        def _(back=back):
            wait(row_copies(n_used - back, (n_used - back) % EXP_YDEPTH, False))


def _experts(plan, xs, w_gate, w_up, w_down):
    any_spec = pl.BlockSpec(memory_space=pl.ANY)
    return pl.pallas_call(
        _expert_kernel,
        grid_spec=pltpu.PrefetchScalarGridSpec(
            num_scalar_prefetch=3,
            grid=(1,),
            in_specs=[any_spec, any_spec, any_spec, any_spec],
            out_specs=any_spec,
            scratch_shapes=[
                pltpu.VMEM((EXP_XDEPTH, EXP_BM, D_MODEL), BF16),
                pltpu.VMEM((EXP_YDEPTH, EXP_BM, D_MODEL), BF16),
                pltpu.VMEM((2, D_MODEL, D_EXPERT), F32),
                pltpu.VMEM((2, D_MODEL, D_EXPERT), F32),
                pltpu.VMEM((2, D_EXPERT, D_MODEL), F32),
                pltpu.VMEM((D_MODEL, D_EXPERT), BF16),
                pltpu.VMEM((D_MODEL, D_EXPERT), BF16),
                pltpu.VMEM((D_EXPERT, D_MODEL), BF16),
                pltpu.SemaphoreType.DMA((EXP_XDEPTH,)),
                pltpu.SemaphoreType.DMA((EXP_YDEPTH,)),
                pltpu.SemaphoreType.DMA((2,)),
            ]),
        out_shape=jax.ShapeDtypeStruct(xs.shape, BF16),
        compiler_params=pltpu.CompilerParams(
            dimension_semantics=("arbitrary",), vmem_limit_bytes=VMEM_LIMIT),
        name="experts",
    )(plan['blk_expert'], plan['next_expert'], plan['n_used'], xs, w_gate, w_up, w_down)


def _combine_kernel(local_ref, global_ref, x1_ref, h2_ref, p_ref, rank_ref, gate_ref, lo_ref, hi_ref,
                    wsg_ref, wsu_ref, wsd_ref, gple_ref, wpg_ref, wp_ref, gfin_ref, ys_hbm, o_ref, ybuf, st_ref, sem,
                    *, n_win, final_norm):
    w = pl.program_id(0)
    slot = w % 2

    @pl.when(w == 0)
    def _():
        ybuf[...] = jnp.zeros(ybuf.shape, BF16)
        _start_runs(local_ref, global_ref, w, ybuf, slot, ys_hbm, sem, False)

    @pl.when(w + 1 < n_win)
    def _():
        _start_runs(local_ref, global_ref, w + 1, ybuf, 1 - slot, ys_hbm, sem, False)

    lo = lo_ref[0]
    hi = hi_ref[0]
    lo_f = lo.astype(F32)
    rank_tm = rank_ref[...]
    gate_tm = gate_ref[...]

    def build_group(lg):
        cols = slice(lg * CMB_LG, (lg + 1) * CMB_LG)
        rid = lg * CMB_LG + lax.broadcasted_iota(jnp.int32, (N_EXPERTS, CMB_LG), 1)
        owner = jnp.where(rid >= lo, jnp.where(rid < hi, 1.0, 0.0), 0.0)
        run_row = rid[0:1, :].astype(F32) - jnp.sum(owner * lo_f, axis=0, keepdims=True)
        owner = owner.astype(BF16)
        hit = _dot(rank_tm, owner) == run_row
        st_ref[:, cols] = jnp.where(hit, _dot(gate_tm, owner), 0.0).astype(BF16)

    build_group(0)
    h2 = h2_ref[...]
    hs = _dot(h2, wsg_ref[...])
    hs = hs * jax.nn.sigmoid(hs) * _dot(h2, wsu_ref[...])
    shared = _dot(hs.astype(BF16), wsd_ref[...])

    _wait_runs(local_ref, w, ybuf, slot, ys_hbm, sem, False)
    routed = None
    n_groups = SEL_ROWS // CMB_LG
    for lg in range(n_groups):
        if lg + 1 < n_groups:
            build_group(lg + 1)
        src = pl.multiple_of(slot * SEL_ROWS + lg * CMB_LG, CMB_LG)
        part = _dot(st_ref[:, lg * CMB_LG:(lg + 1) * CMB_LG], ybuf[pl.ds(src, CMB_LG), :])
        routed = part if routed is None else routed + part
    x2 = x1_ref[...] + routed + shared

    hp = _rms(x2, gple_ref[...]).astype(BF16)
    gate = jax.nn.sigmoid(_dot(hp, wpg_ref[...]))
    x3 = x2 + gate * _dot(p_ref[...].astype(BF16), wp_ref[...])
    o_ref[...] = _rms(x3, gfin_ref[...]) if final_norm else x3


def _combine(plan, ys, x1, h2, p, rank_tm, gate_tm, wsg, wsu, wsd, g_ple, w_pg, w_p, g_fin, final_norm):
    t = x1.shape[0]
    n_win = t // WIN
    row = lambda width: pl.BlockSpec((WIN, width), lambda w, *_: (w, 0))
    const = lambda shape: pl.BlockSpec(shape, lambda w, *_: (0,) * len(shape))
    return pl.pallas_call(
        functools.partial(_combine_kernel, n_win=n_win, final_norm=final_norm),
        grid_spec=pltpu.PrefetchScalarGridSpec(
            num_scalar_prefetch=2,
            grid=(n_win,),
            in_specs=[
                row(D_MODEL), row(D_MODEL), row(PLE_DIM), row(N_EXPERTS), row(N_EXPERTS),
                pl.BlockSpec((1, N_EXPERTS, 1), lambda w, *_: (w, 0, 0)),
                pl.BlockSpec((1, N_EXPERTS, 1), lambda w, *_: (w, 0, 0)),
                const((D_MODEL, D_EXPERT)), const((D_MODEL, D_EXPERT)), const((D_EXPERT, D_MODEL)),
                const((1, D_MODEL)), const((D_MODEL, D_MODEL)), const((PLE_DIM, D_MODEL)),
                const((1, D_MODEL)),
                pl.BlockSpec(memory_space=pl.ANY),
            ],
            out_specs=[row(D_MODEL), _staging_spec(0)],
            scratch_shapes=[
                pltpu.VMEM((WIN, SEL_ROWS), BF16),
                pltpu.SemaphoreType.DMA((2,)),
            ]),
        out_shape=[jax.ShapeDtypeStruct((t, D_MODEL), F32), _staging_shape(0)],
        compiler_params=pltpu.CompilerParams(
            dimension_semantics=("arbitrary",), vmem_limit_bytes=VMEM_LIMIT),
        name="combine",
    )(plan['local_off'], plan['global_off'], x1, h2, p, rank_tm, gate_tm, plan['run_lo'], plan['run_hi'],
      wsg, wsu, wsd, g_ple, w_pg, w_p, g_fin, ys)[0]


def kernel(x, p, g_mix, w_in, b_in, w_dw, b_dw, g_cln, b_cln, w_conv_out, b_conv_out, w_pool, s_pool,
           w_out, g_ffn, w_router, b_router, w_e_gate, w_e_up, w_e_down, w_s_gate, w_s_up, w_s_down,
           g_ple, w_ple_gate, w_ple, g_final):
    bsz, s, d = x.shape
    t = bsz * s
    depth = w_in.shape[0]
    xt = x.reshape(t, d)
    row = lambda v: v.reshape(1, -1)
    for i in range(depth):
        x1, h2 = _mixer(
            xt, s, row(g_mix[i]), w_in[i].astype(BF16), row(b_in[i]), w_dw[i], row(b_dw[i]),
            row(g_cln[i]), row(b_cln[i]), w_conv_out[i].astype(BF16), row(b_conv_out[i]),
            w_pool[i].astype(BF16), row(s_pool[i]), w_out[i].astype(BF16), row(g_ffn[i]))
        gate, rank, pos, cnt = _router(h2, w_router[i].T.astype(BF16), b_router[i].reshape(N_EXPERTS, 1))
        plan = _dispatch_plan(cnt, t)
        xs = _dispatch(plan, h2, pos)
        ys = _experts(plan, xs, w_e_gate[i], w_e_up[i], w_e_down[i])
        xt = _combine(
            plan, ys, x1, h2, p[i].reshape(t, PLE_DIM), rank.T, gate.T,
            w_s_gate[i].astype(BF16), w_s_up[i].astype(BF16), w_s_down[i].astype(BF16),
            row(g_ple[i]), w_ple_gate[i].astype(BF16), w_ple[i].astype(BF16), row(g_final),
            final_norm=(i == depth - 1))
    return xt.reshape(bsz, s, d)
```

```python
import functools

import jax
import jax.numpy as jnp
from jax import lax
from jax.experimental import pallas as pl
from jax.experimental.pallas import tpu as pltpu

D_MODEL = 1024
D_CONV = 1024
D_POOL = 1024
CONV_WIDTH = 31
POOL_WINDOWS = (2, 4, 8, 16)
POOL_GROUP = 256
PLE_DIM = 256
N_EXPERTS = 64
N_GROUPS = 8
GROUP_SIZE = N_EXPERTS // N_GROUPS
TOPK_GROUPS = 4
TOP_K = 8
D_EXPERT = 256
ROUTED_SCALE = 2.5
NORM_EPS = 1e-6

F32 = jnp.float32
BF16 = jnp.bfloat16

MIX_TM = 256
MIX_NV = MIX_TM // 8
CONV_MG = 8
ROW_CHUNK = 64
LANE = 128

ROUTER_TM = 1024
WIN = 256
SEL_ROWS = 2560
SEL_RG = 64
SEL_MM = 512
EXP_BM = 576
EXP_XDEPTH = 6
EXP_YDEPTH = 3
EXP_SPLIT = 4
CMB_LG = 512

VMEM_LIMIT = 56 * 1024 * 1024


def _rms(x, g):
    ms = jnp.mean(x * x, axis=-1, keepdims=True)
    return x * lax.rsqrt(ms + NORM_EPS) * g


def _dot(a, b):
    return jnp.dot(a, b, preferred_element_type=F32)


def _dot_t(a, b):
    return lax.dot_general(a, b, (((0,), (0,)), ((), ())), preferred_element_type=F32)


def _mixer_kernel(x_ref, gmix_ref, win_ref, bin_ref, wdw_ref, bdw_ref, gcln_ref, bcln_ref,
                  wco_ref, bco_ref, wpool_ref, spool_ref, wout_ref, gffn_ref, perm_ref, unperm_ref,
                  x1_ref, h2_ref, a_ext, a_prev, u_ext, u_prev, c_buf, q_buf, *, tiles_per_seq):
    i = pl.program_id(0) % tiles_per_seq
    tm = MIX_TM
    nv = MIX_NV

    @pl.when(i == 0)
    def _():
        a_prev[...] = jnp.zeros(a_prev.shape, F32)
        u_prev[...] = jnp.zeros(u_prev.shape, F32)

    x = x_ref[...]
    h = _dot(perm_ref[...], _rms(x, gmix_ref[...]).astype(BF16)).astype(BF16)

    def proj(lo, hi):
        return _dot(h, win_ref[:, lo:hi]) + bin_ref[:, lo:hi]

    glu = proj(0, D_CONV) * jax.nn.sigmoid(proj(D_CONV, 2 * D_CONV))
    for lc in range(D_CONV // LANE):
        a_ext[lc, tm:2 * tm, :] = glu[:, lc * LANE:(lc + 1) * LANE]
    u_ext[tm:2 * tm, :] = proj(2 * D_CONV, 2 * D_CONV + D_POOL)

    def delayed_groups(ext, prev, first_group):
        last_row = lax.broadcasted_iota(jnp.int32, (8, ext.shape[-1]), 0) == 7
        for g in range(first_group, nv):
            rows = slice(8 * g, 8 * g + 8)
            mixed = jnp.where(last_row, prev[rows, :], ext[tm + 8 * g:tm + 8 * g + 8, :])
            ext[rows, :] = pltpu.roll(mixed, 1, axis=0)
            prev[rows, :] = ext[tm + 8 * g:tm + 8 * g + 8, :]

    delayed_groups(u_ext, u_prev, nv - (max(POOL_WINDOWS) - 1))

    def conv_column(lc, carry):
        a_col = a_ext.at[lc]
        delayed_groups(a_col, a_prev.at[lc], nv - (CONV_WIDTH - 1))
        w_col = wdw_ref.at[lc]
        for g0 in range(0, nv, CONV_MG):
            acc = None
            for k in range(CONV_WIDTH):
                src = nv + g0 + k - (CONV_WIDTH - 1)
                term = a_col[8 * src:8 * (src + CONV_MG), :] * w_col[k:k + 1, :]
                acc = term if acc is None else acc + term
            c_buf[lc, 8 * g0:8 * (g0 + CONV_MG), :] = acc + bdw_ref[lc]
        return carry
    lax.fori_loop(0, D_CONV // LANE, conv_column, 0)

    c = jnp.concatenate([c_buf[lc] for lc in range(D_CONV // LANE)], axis=-1)
    mu = jnp.mean(c, axis=-1, keepdims=True)
    xc = c - mu
    var = jnp.mean(xc * xc, axis=-1, keepdims=True)
    y = xc * lax.rsqrt(var + NORM_EPS) * gcln_ref[...] + bcln_ref[...]
    y = y * jax.nn.sigmoid(y)
    branch_a = _dot(y.astype(BF16), wco_ref[...]) + bco_ref[...]

    for r0 in range(0, tm, ROW_CHUNK):
        row = r0 + lax.broadcasted_iota(jnp.int32, (ROW_CHUNK, POOL_GROUP), 0)
        t1 = i * tm + (row % 8) * nv + row // 8 + 1
        for gi, w in enumerate(POOL_WINDOWS):
            ls = slice(gi * POOL_GROUP, (gi + 1) * POOL_GROUP)
            tok = u_ext[tm + r0:tm + r0 + ROW_CHUNK, ls]
            s = tok
            for j in range(1, w):
                s = s + u_ext[tm + r0 - 8 * j:tm + r0 - 8 * j + ROW_CHUNK, ls]
            cnt = jnp.minimum(t1, w).astype(F32)
            q_buf[r0:r0 + ROW_CHUNK, ls] = s / cnt - tok

    qs_out = []
    for gi in range(len(POOL_WINDOWS)):
        ls = slice(gi * POOL_GROUP, (gi + 1) * POOL_GROUP)
        qs_out.append(_dot(q_buf[:, ls].astype(BF16), wpool_ref[gi]) * spool_ref[:, ls])
    branch_b = jnp.concatenate(qs_out, axis=-1)

    c2 = 2 * D_CONV + D_POOL
    gate_a = jax.nn.sigmoid(proj(c2, c2 + D_MODEL))
    gate_b = jax.nn.sigmoid(proj(c2 + D_MODEL, c2 + 2 * D_MODEL))
    merged = gate_a * branch_a + gate_b * branch_b
    merged = _dot(unperm_ref[...], merged.astype(BF16)).astype(BF16)
    x1 = x + _dot(merged, wout_ref[...])
    x1_ref[...] = x1
    h2_ref[...] = _rms(x1, gffn_ref[...]).astype(BF16)


def _const_spec(shape):
    n = len(shape)
    return pl.BlockSpec(shape, lambda i, _n=n: (0,) * _n)


def _mixer(x, seq_len, g_mix, w_in, b_in, w_dw, b_dw, g_cln, b_cln, w_co, b_co, w_pool, s_pool, w_out,
           g_ffn):
    t = x.shape[0]
    tm = MIX_TM
    assert seq_len % tm == 0 and MIX_NV >= CONV_WIDTH and MIX_NV >= max(POOL_WINDOWS)
    d_in = w_in.shape[1]
    row = pl.BlockSpec((tm, D_MODEL), lambda i: (i, 0))
    n_col = D_CONV // LANE
    w_dw = w_dw.reshape(CONV_WIDTH, n_col, LANE).transpose(1, 0, 2)
    b_dw = b_dw.reshape(n_col, 1, LANE)
    r = jnp.arange(tm)
    perm = ((r % 8) * MIX_NV + r // 8)[:, None] == jnp.arange(tm)[None, :]
    perm = perm.astype(BF16)
    return pl.pallas_call(
        functools.partial(_mixer_kernel, tiles_per_seq=seq_len // tm),
        grid=(t // tm,),
        in_specs=[
            row,
            _const_spec((1, D_MODEL)),
            _const_spec((D_MODEL, d_in)),
            _const_spec((1, d_in)),
            _const_spec((n_col, CONV_WIDTH, LANE)),
            _const_spec((n_col, 1, LANE)),
            _const_spec((1, D_CONV)),
            _const_spec((1, D_CONV)),
            _const_spec((D_CONV, D_MODEL)),
            _const_spec((1, D_MODEL)),
            _const_spec((len(POOL_WINDOWS), POOL_GROUP, POOL_GROUP)),
            _const_spec((1, D_POOL)),
            _const_spec((D_MODEL, D_MODEL)),
            _const_spec((1, D_MODEL)),
            _const_spec((tm, tm)),
            _const_spec((tm, tm)),
        ],
        out_specs=[row, row],
        out_shape=[jax.ShapeDtypeStruct((t, D_MODEL), F32),
                   jax.ShapeDtypeStruct((t, D_MODEL), BF16)],
        scratch_shapes=[
            pltpu.VMEM((n_col, 2 * tm, LANE), F32),
            pltpu.VMEM((n_col, tm, LANE), F32),
            pltpu.VMEM((2 * tm, D_POOL), F32),
            pltpu.VMEM((tm, D_POOL), F32),
            pltpu.VMEM((n_col, tm, LANE), F32),
            pltpu.VMEM((tm, D_POOL), F32),
        ],
        compiler_params=pltpu.CompilerParams(
            dimension_semantics=("arbitrary",), vmem_limit_bytes=VMEM_LIMIT),
        name="mixer",
    )(x, g_mix, w_in, b_in, w_dw, b_dw, g_cln, b_cln, w_co, b_co, w_pool, s_pool, w_out, g_ffn, perm, perm.T)


def _beats(v, other, other_is_later):
    v = jnp.broadcast_to(v, other.shape)
    return jnp.where(other_is_later, jnp.where(v >= other, 1, 0), jnp.where(v > other, 1, 0))


def _router_kernel(h2_ref, wrt_ref, br_ref, utri_ref, ltri_ref, gate_ref, rank_ref, pos_ref, cnt_ref):
    tm = ROUTER_TM
    logits = lax.dot_general(wrt_ref[...], h2_ref[...], (((1,), (1,)), ((), ())),
                             preferred_element_type=F32)
    scores = jax.nn.sigmoid(logits)
    sel = scores + br_ref[...]
    shape3 = (N_GROUPS, GROUP_SIZE, tm)
    sel3 = sel.reshape(shape3)
    scores3 = scores.reshape(shape3)
    neg_inf = jnp.float32(-jnp.inf)

    member = lax.broadcasted_iota(jnp.int32, shape3, 1)
    m1 = jnp.max(sel3, axis=1, keepdims=True)
    first = jnp.min(jnp.where(sel3 == m1, member, GROUP_SIZE), axis=1, keepdims=True)
    m2 = jnp.max(jnp.where(member == first, neg_inf, sel3), axis=1, keepdims=True)
    gscore = jnp.broadcast_to(m1 + m2, shape3)

    gidx = lax.broadcasted_iota(jnp.int32, shape3, 0)
    grank = jnp.zeros(shape3, jnp.int32)
    for j in range(N_GROUPS):
        sj = gscore[j:j + 1]
        grank = grank + _beats(sj, gscore, gidx > j)
    masked = jnp.where(grank < TOPK_GROUPS, sel3, neg_inf)

    eidx = gidx * GROUP_SIZE + member
    work = masked
    erank = jnp.full(shape3, TOP_K, jnp.int32)
    for k in range(TOP_K):
        best = jnp.max(jnp.max(work, axis=0, keepdims=True), axis=1, keepdims=True)
        cand = jnp.where(work == best, eidx, N_EXPERTS)
        pick = jnp.min(jnp.min(cand, axis=0, keepdims=True), axis=1, keepdims=True)
        hit = eidx == pick
        work = jnp.where(hit, neg_inf, work)
        erank = jnp.where(hit, k, erank)
    chosen = erank < TOP_K
    top_s = jnp.where(chosen, scores3, 0.0)
    denom = jnp.sum(jnp.sum(top_s, axis=0, keepdims=True), axis=1, keepdims=True)
    gates3 = top_s / denom * ROUTED_SCALE
    chosen2 = jnp.where(chosen, 1.0, 0.0).reshape(N_EXPERTS, tm)
    gate_ref[...] = gates3.reshape(N_EXPERTS, tm).astype(BF16)

    for w in range(tm // WIN):
        ls = slice(w * WIN, (w + 1) * WIN)
        mw = chosen2[:, ls]
        rank = _dot(mw.astype(BF16), utri_ref[...])
        n = jnp.sum(mw, axis=1, keepdims=True)
        run = jnp.floor((n + 7.0) * 0.125) * 8.0
        start = _dot(ltri_ref[...], jnp.broadcast_to(run, (N_EXPERTS, WIN)).astype(BF16))
        rank_ref[:, ls] = jnp.where(mw > 0.5, rank, -1.0).astype(BF16)
        row3 = (rank + start).reshape(N_GROUPS, GROUP_SIZE, WIN)
        er = erank[:, :, ls]
        for k in range(TOP_K):
            pk = jnp.sum(jnp.sum(jnp.where(er == k, row3, 0.0), axis=0, keepdims=True), axis=1, keepdims=True)
            pos_ref[k:k + 1, ls] = pk.reshape(1, WIN).astype(jnp.int32)
        cnt_ref[w] = n


def _router(h2, w_rt, b_r):
    t = h2.shape[0]
    tm = ROUTER_TM
    utri = jnp.triu(jnp.ones((WIN, WIN), BF16), k=1)
    ltri = jnp.tril(jnp.ones((N_EXPERTS, N_EXPERTS), BF16), k=-1)
    return pl.pallas_call(
        _router_kernel,
        grid=(t // tm,),
        in_specs=[
            pl.BlockSpec((tm, D_MODEL), lambda i: (i, 0)),
            _const_spec((N_EXPERTS, D_MODEL)),
            _const_spec((N_EXPERTS, 1)),
            _const_spec((WIN, WIN)),
            _const_spec((N_EXPERTS, N_EXPERTS)),
        ],
        out_specs=[
            pl.BlockSpec((N_EXPERTS, tm), lambda i: (0, i)),
            pl.BlockSpec((N_EXPERTS, tm), lambda i: (0, i)),
            pl.BlockSpec((TOP_K, tm), lambda i: (0, i)),
            pl.BlockSpec((tm // WIN, N_EXPERTS, 1), lambda i: (i, 0, 0)),
        ],
        out_shape=[
            jax.ShapeDtypeStruct((N_EXPERTS, t), BF16),
            jax.ShapeDtypeStruct((N_EXPERTS, t), BF16),
            jax.ShapeDtypeStruct((TOP_K, t), jnp.int32),
            jax.ShapeDtypeStruct((t // WIN, N_EXPERTS, 1), F32),
        ],
        compiler_params=pltpu.CompilerParams(
            dimension_semantics=("arbitrary",), vmem_limit_bytes=VMEM_LIMIT),
        name="router",
    )(h2, w_rt, b_r, utri, ltri)


def _sorted_rows_bound(t):
    rows = t * TOP_K + (t // WIN) * N_EXPERTS * 7 + N_EXPERTS * (EXP_BM - 1)
    return -(-rows // EXP_BM) * EXP_BM


def _dispatch_plan(cnt, t):
    nw = t // WIN
    n = cnt.reshape(nw, N_EXPERTS).astype(jnp.int32)
    run = (n + 7) // 8 * 8
    local_end = jnp.cumsum(run, axis=1)
    local_off = jnp.concatenate([jnp.zeros((nw, 1), jnp.int32), local_end], axis=1)
    total = jnp.sum(run, axis=0)
    region = (total + EXP_BM - 1) // EXP_BM * EXP_BM
    region_end = jnp.cumsum(region)
    base = region_end - region
    global_off = base[None, :] + jnp.cumsum(run, axis=0) - run
    n_blocks = _sorted_rows_bound(t) // EXP_BM
    n_used = region_end[-1] // EXP_BM
    blk = jnp.arange(n_blocks, dtype=jnp.int32)
    blk_expert = jnp.sum((region_end[None, :] <= blk[:, None] * EXP_BM).astype(jnp.int32), axis=1)
    blk_expert = jnp.minimum(blk_expert, N_EXPERTS - 1)
    eid = jnp.arange(N_EXPERTS, dtype=jnp.int32)
    later_nonempty = (eid[None, :] > eid[:, None]) & (region[None, :] > 0)
    next_expert = jnp.min(jnp.where(later_nonempty, eid[None, :], N_EXPERTS), axis=1).astype(jnp.int32)
    return dict(
        run_lo=local_off[:, :N_EXPERTS].reshape(nw, N_EXPERTS, 1),
        run_hi=local_off[:, 1:].reshape(nw, N_EXPERTS, 1),
        local_off=local_off.reshape(-1), global_off=global_off.reshape(-1),
        fill_off=base + total, fill_cnt=region - total,
        blk_expert=blk_expert.astype(jnp.int32), next_expert=next_expert,
        n_used=n_used.reshape(1).astype(jnp.int32))


def _run_copy(local_ref, global_ref, win, e, vmem_buf, slot, hbm_buf, sem, to_hbm):
    lo = pl.multiple_of(local_ref[win * (N_EXPERTS + 1) + e], 8)
    cnt = pl.multiple_of(local_ref[win * (N_EXPERTS + 1) + e + 1] - lo, 8)
    go = pl.multiple_of(global_ref[win * N_EXPERTS + e], 8)
    v = vmem_buf.at[pl.ds(pl.multiple_of(slot * SEL_ROWS + lo, 8), cnt)]
    h = hbm_buf.at[pl.ds(go, cnt)]
    cp = pltpu.make_async_copy(v, h, sem.at[slot]) if to_hbm else pltpu.make_async_copy(h, v, sem.at[slot])
    return cnt, cp


def _start_runs(local_ref, global_ref, win, vmem_buf, slot, hbm_buf, sem, to_hbm):
    def body(e, carry):
        cnt, cp = _run_copy(local_ref, global_ref, win, e, vmem_buf, slot, hbm_buf, sem, to_hbm)

        @pl.when(cnt > 0)
        def _():
            cp.start()
        return carry
    lax.fori_loop(0, N_EXPERTS, body, 0)


def _wait_runs(local_ref, win, vmem_buf, slot, hbm_buf, sem, to_hbm):
    total = pl.multiple_of(local_ref[win * (N_EXPERTS + 1) + N_EXPERTS], 8)
    v = vmem_buf.at[pl.ds(pl.multiple_of(slot * SEL_ROWS, 8), total)]
    h = hbm_buf.at[pl.ds(0, total)]
    cp = pltpu.make_async_copy(v, h, sem.at[slot]) if to_hbm else pltpu.make_async_copy(h, v, sem.at[slot])

    @pl.when(total > 0)
    def _():
        cp.wait()


def _dispatch_kernel(local_ref, global_ref, fill_off_ref, fill_cnt_ref, h2_ref, pos_ref, xs_hbm, sbuf,
                     s_ref, sem, zsem, *, n_win):
    w = pl.program_id(0)
    slot = w % 2
    pos = pos_ref[...]

    h2 = h2_ref[...]
    assert SEL_RG <= 256
    rid_b = lax.broadcasted_iota(jnp.int32, (SEL_RG, WIN), 0).astype(F32).astype(BF16)
    one_b = jnp.ones((SEL_RG, WIN), BF16)
    for g in range(SEL_ROWS // SEL_MM):
        for sg in range(SEL_MM // SEL_RG):
            r0 = g * SEL_MM + sg * SEL_RG
            acc = jnp.zeros((SEL_RG, WIN), BF16)
            for k in range(TOP_K):
                off = (pos[k:k + 1, :] - r0).astype(F32)
                off = jnp.broadcast_to(off, (SEL_RG, WIN)).astype(BF16)
                acc = jnp.where(rid_b == off, one_b, acc)
            s_ref[r0:r0 + SEL_RG, :] = acc
        rows = slice(g * SEL_MM, (g + 1) * SEL_MM)
        dst = pl.multiple_of(slot * SEL_ROWS + g * SEL_MM, SEL_MM)
        sbuf[pl.ds(dst, SEL_MM), :] = _dot(s_ref[rows, :], h2).astype(BF16)

    _start_runs(local_ref, global_ref, w, sbuf, slot, xs_hbm, sem, True)

    @pl.when(w > 0)
    def _():
        _wait_runs(local_ref, w - 1, sbuf, 1 - slot, xs_hbm, sem, True)

    @pl.when(w == n_win - 1)
    def _():
        sbuf[2 * SEL_ROWS:, :] = jnp.zeros((EXP_BM, D_MODEL), BF16)

        def fill(e, wait):
            cnt = pl.multiple_of(fill_cnt_ref[e], 8)
            off = pl.multiple_of(fill_off_ref[e], 8)
            cp = pltpu.make_async_copy(sbuf.at[pl.ds(2 * SEL_ROWS, cnt)], xs_hbm.at[pl.ds(off, cnt)], zsem)

            @pl.when(cnt > 0)
            def _():
                if wait:
                    cp.wait()
                else:
                    cp.start()

        def start_body(e, carry):
            fill(e, False)
            return carry

        def wait_body(e, carry):
            fill(e, True)
            return carry
        lax.fori_loop(0, N_EXPERTS, start_body, 0)
        _wait_runs(local_ref, w, sbuf, slot, xs_hbm, sem, True)
        lax.fori_loop(0, N_EXPERTS, wait_body, 0)


def _staging_shape(extra_rows):
    return jax.ShapeDtypeStruct((2 * SEL_ROWS + extra_rows, D_MODEL), BF16)


def _staging_spec(extra_rows):
    return pl.BlockSpec((2 * SEL_ROWS + extra_rows, D_MODEL), lambda w, *_: (0, 0))


def _dispatch(plan, h2, pos):
    t = h2.shape[0]
    n_win = t // WIN
    return pl.pallas_call(
        functools.partial(_dispatch_kernel, n_win=n_win),
        grid_spec=pltpu.PrefetchScalarGridSpec(
            num_scalar_prefetch=4,
            grid=(n_win,),
            in_specs=[
                pl.BlockSpec((WIN, D_MODEL), lambda w, *_: (w, 0)),
                pl.BlockSpec((TOP_K, WIN), lambda w, *_: (0, w)),
            ],
            out_specs=[pl.BlockSpec(memory_space=pl.ANY), _staging_spec(EXP_BM)],
            scratch_shapes=[
                pltpu.VMEM((SEL_ROWS, WIN), BF16),
                pltpu.SemaphoreType.DMA((2,)),
                pltpu.SemaphoreType.DMA,
            ]),
        out_shape=[jax.ShapeDtypeStruct((_sorted_rows_bound(t), D_MODEL), BF16), _staging_shape(EXP_BM)],
        compiler_params=pltpu.CompilerParams(
            dimension_semantics=("arbitrary",), vmem_limit_bytes=VMEM_LIMIT),
        name="dispatch",
    )(plan['local_off'], plan['global_off'], plan['fill_off'], plan['fill_cnt'], h2, pos)[0]


def _expert_kernel(blk_expert_ref, next_expert_ref, n_used_ref, xs_hbm, wg_hbm, wu_hbm, wd_hbm, ys_hbm,
                   xbuf, ybuf, wg_st, wu_st, wd_st, wg_bf, wu_bf, wd_bf, xsem, ysem, wsem):
    n_used = n_used_ref[0]
    part = EXP_BM // EXP_SPLIT

    def row_copies(b, slot, fetch):
        out = []
        for q in range(EXP_SPLIT):
            hbm_rows = pl.ds(pl.multiple_of(b * EXP_BM + q * part, part), part)
            if fetch:
                out.append(pltpu.make_async_copy(xs_hbm.at[hbm_rows], xbuf.at[slot, q * part:(q + 1) * part],
                                                 xsem.at[slot]))
            else:
                out.append(pltpu.make_async_copy(ybuf.at[slot, q * part:(q + 1) * part], ys_hbm.at[hbm_rows],
                                                 ysem.at[slot]))
        return out

    def weight_copies(e, slot):
        return [pltpu.make_async_copy(wg_hbm.at[e], wg_st.at[slot], wsem.at[slot]),
                pltpu.make_async_copy(wu_hbm.at[e], wu_st.at[slot], wsem.at[slot]),
                pltpu.make_async_copy(wd_hbm.at[e], wd_st.at[slot], wsem.at[slot])]

    def start(copies):
        for c in copies:
            c.start()

    def wait(copies):
        for c in copies:
            c.wait()

    for ahead in range(EXP_XDEPTH - 1):
        @pl.when(ahead < n_used)
        def _(ahead=ahead):
            start(row_copies(ahead, ahead, True))

    @pl.when(n_used > 0)
    def _():
        start(weight_copies(blk_expert_ref[0], 0))

    def body(b, wslot):
        e = blk_expert_ref[b]
        new_expert = jnp.logical_or(b == 0, e != blk_expert_ref[jnp.maximum(b - 1, 0)])

        @pl.when(new_expert)
        def _():
            wait(weight_copies(e, wslot))
            wg_bf[...] = wg_st[wslot].astype(BF16)
            wu_bf[...] = wu_st[wslot].astype(BF16)
            wd_bf[...] = wd_st[wslot].astype(BF16)
            nxt = next_expert_ref[e]

            @pl.when(nxt < N_EXPERTS)
            def _():
                start(weight_copies(nxt, 1 - wslot))

        @pl.when(b + EXP_XDEPTH - 1 < n_used)
        def _():
            start(row_copies(b + EXP_XDEPTH - 1, (b + EXP_XDEPTH - 1) % EXP_XDEPTH, True))

        xslot = b % EXP_XDEPTH
        yslot = b % EXP_YDEPTH
        wait(row_copies(b, xslot, True))

        @pl.when(b >= EXP_YDEPTH)
        def _():
            wait(row_copies(b - EXP_YDEPTH, yslot, False))

        x = xbuf[xslot]
        hg = _dot(x, wg_bf[...])
        hb = hg * jax.nn.sigmoid(hg) * _dot(x, wu_bf[...])
        ybuf[yslot] = _dot(hb.astype(BF16), wd_bf[...]).astype(BF16)
        start(row_copies(b, yslot, False))
        return jnp.where(new_expert, 1 - wslot, wslot)

    lax.fori_loop(0, n_used, body, jnp.int32(0))

    for back in range(EXP_YDEPTH, 0, -1):
        @pl.when(n_used >= back)
        def _(back=back):
            wait(row_copies(n_used - back, (n_used - back) % EXP_YDEPTH, False))


def _experts(plan, xs, w_gate, w_up, w_down):
    any_spec = pl.BlockSpec(memory_space=pl.ANY)
    return pl.pallas_call(
        _expert_kernel,
        grid_spec=pltpu.PrefetchScalarGridSpec(
            num_scalar_prefetch=3,
            grid=(1,),
            in_specs=[any_spec, any_spec, any_spec, any_spec],
            out_specs=any_spec,
            scratch_shapes=[
                pltpu.VMEM((EXP_XDEPTH, EXP_BM, D_MODEL), BF16),
                pltpu.VMEM((EXP_YDEPTH, EXP_BM, D_MODEL), BF16),
                pltpu.VMEM((2, D_MODEL, D_EXPERT), F32),
                pltpu.VMEM((2, D_MODEL, D_EXPERT), F32),
                pltpu.VMEM((2, D_EXPERT, D_MODEL), F32),
                pltpu.VMEM((D_MODEL, D_EXPERT), BF16),
                pltpu.VMEM((D_MODEL, D_EXPERT), BF16),
                pltpu.VMEM((D_EXPERT, D_MODEL), BF16),
                pltpu.SemaphoreType.DMA((EXP_XDEPTH,)),
                pltpu.SemaphoreType.DMA((EXP_YDEPTH,)),
                pltpu.SemaphoreType.DMA((2,)),
            ]),
        out_shape=jax.ShapeDtypeStruct(xs.shape, BF16),
        compiler_params=pltpu.CompilerParams(
            dimension_semantics=("arbitrary",), vmem_limit_bytes=VMEM_LIMIT),
        name="experts",
    )(plan['blk_expert'], plan['next_expert'], plan['n_used'], xs, w_gate, w_up, w_down)


def _combine_kernel(local_ref, global_ref, x1_ref, h2_ref, p_ref, rank_ref, gate_ref, lo_ref, hi_ref,
                    wsg_ref, wsu_ref, wsd_ref, gple_ref, wpg_ref, wp_ref, gfin_ref, ys_hbm, o_ref, ybuf, st_ref, sem,
                    *, n_win, final_norm):
    w = pl.program_id(0)
    slot = w % 2

    @pl.when(w == 0)
    def _():
        ybuf[...] = jnp.zeros(ybuf.shape, BF16)
        _start_runs(local_ref, global_ref, w, ybuf, slot, ys_hbm, sem, False)

    @pl.when(w + 1 < n_win)
    def _():
        _start_runs(local_ref, global_ref, w + 1, ybuf, 1 - slot, ys_hbm, sem, False)

    lo = lo_ref[0]
    hi = hi_ref[0]
    lo_f = lo.astype(F32)
    rank_tbl = rank_ref[...]
    gate_tbl = gate_ref[...]

    def build_group(lg):
        cols = slice(lg * CMB_LG, (lg + 1) * CMB_LG)
        rid = lg * CMB_LG + lax.broadcasted_iota(jnp.int32, (N_EXPERTS, CMB_LG), 1)
        owner = jnp.where(rid >= lo, jnp.where(rid < hi, 1.0, 0.0), 0.0)
        run_row = rid[0:1, :].astype(F32) - jnp.sum(owner * lo_f, axis=0, keepdims=True)
        owner = owner.astype(BF16)
        hit = _dot_t(rank_tbl, owner) == run_row
        st_ref[:, cols] = jnp.where(hit, _dot_t(gate_tbl, owner), 0.0).astype(BF16)

    build_group(0)
    h2 = h2_ref[...]
    hs = _dot(h2, wsg_ref[...])
    hs = hs * jax.nn.sigmoid(hs) * _dot(h2, wsu_ref[...])
    shared = _dot(hs.astype(BF16), wsd_ref[...])

    _wait_runs(local_ref, w, ybuf, slot, ys_hbm, sem, False)
    routed = None
    n_groups = SEL_ROWS // CMB_LG
    for lg in range(n_groups):
        if lg + 1 < n_groups:
            build_group(lg + 1)
        src = pl.multiple_of(slot * SEL_ROWS + lg * CMB_LG, CMB_LG)
        part = _dot(st_ref[:, lg * CMB_LG:(lg + 1) * CMB_LG], ybuf[pl.ds(src, CMB_LG), :])
        routed = part if routed is None else routed + part
    x2 = x1_ref[...] + routed + shared

    hp = _rms(x2, gple_ref[...]).astype(BF16)
    gate = jax.nn.sigmoid(_dot(hp, wpg_ref[...]))
    x3 = x2 + gate * _dot(p_ref[...].astype(BF16), wp_ref[...])
    o_ref[...] = _rms(x3, gfin_ref[...]) if final_norm else x3


def _combine(plan, ys, x1, h2, p, rank_tbl, gate_tbl, wsg, wsu, wsd, g_ple, w_pg, w_p, g_fin, final_norm):
    t = x1.shape[0]
    n_win = t // WIN
    row = lambda width: pl.BlockSpec((WIN, width), lambda w, *_: (w, 0))
    const = lambda shape: pl.BlockSpec(shape, lambda w, *_: (0,) * len(shape))
    return pl.pallas_call(
        functools.partial(_combine_kernel, n_win=n_win, final_norm=final_norm),
        grid_spec=pltpu.PrefetchScalarGridSpec(
            num_scalar_prefetch=2,
            grid=(n_win,),
            in_specs=[
                row(D_MODEL), row(D_MODEL), row(PLE_DIM),
                pl.BlockSpec((N_EXPERTS, WIN), lambda w, *_: (0, w)),
                pl.BlockSpec((N_EXPERTS, WIN), lambda w, *_: (0, w)),
                pl.BlockSpec((1, N_EXPERTS, 1), lambda w, *_: (w, 0, 0)),
                pl.BlockSpec((1, N_EXPERTS, 1), lambda w, *_: (w, 0, 0)),
                const((D_MODEL, D_EXPERT)), const((D_MODEL, D_EXPERT)), const((D_EXPERT, D_MODEL)),
                const((1, D_MODEL)), const((D_MODEL, D_MODEL)), const((PLE_DIM, D_MODEL)),
                const((1, D_MODEL)),
                pl.BlockSpec(memory_space=pl.ANY),
            ],
            out_specs=[row(D_MODEL), _staging_spec(0)],
            scratch_shapes=[
                pltpu.VMEM((WIN, SEL_ROWS), BF16),
                pltpu.SemaphoreType.DMA((2,)),
            ]),
        out_shape=[jax.ShapeDtypeStruct((t, D_MODEL), F32), _staging_shape(0)],
        compiler_params=pltpu.CompilerParams(
            dimension_semantics=("arbitrary",), vmem_limit_bytes=VMEM_LIMIT),
        name="combine",
    )(plan['local_off'], plan['global_off'], x1, h2, p, rank_tbl, gate_tbl, plan['run_lo'], plan['run_hi'],
      wsg, wsu, wsd, g_ple, w_pg, w_p, g_fin, ys)[0]


def kernel(x, p, g_mix, w_in, b_in, w_dw, b_dw, g_cln, b_cln, w_conv_out, b_conv_out, w_pool, s_pool,
           w_out, g_ffn, w_router, b_router, w_e_gate, w_e_up, w_e_down, w_s_gate, w_s_up, w_s_down,
           g_ple, w_ple_gate, w_ple, g_final):
    bsz, s, d = x.shape
    t = bsz * s
    depth = w_in.shape[0]
    xt = x.reshape(t, d)
    row = lambda v: v.reshape(1, -1)
    for i in range(depth):
        x1, h2 = _mixer(
            xt, s, row(g_mix[i]), w_in[i].astype(BF16), row(b_in[i]), w_dw[i], row(b_dw[i]),
            row(g_cln[i]), row(b_cln[i]), w_conv_out[i].astype(BF16), row(b_conv_out[i]),
            w_pool[i].astype(BF16), row(s_pool[i]), w_out[i].astype(BF16), row(g_ffn[i]))
        gate, rank, pos, cnt = _router(h2, w_router[i].T.astype(BF16), b_router[i].reshape(N_EXPERTS, 1))
        plan = _dispatch_plan(cnt, t)
        xs = _dispatch(plan, h2, pos)
        ys = _experts(plan, xs, w_e_gate[i], w_e_up[i], w_e_down[i])
        xt = _combine(
            plan, ys, x1, h2, p[i].reshape(t, PLE_DIM), rank, gate,
            w_s_gate[i].astype(BF16), w_s_up[i].astype(BF16), w_s_down[i].astype(BF16),
            row(g_ple[i]), w_ple_gate[i].astype(BF16), w_ple[i].astype(BF16), row(g_final),
            final_norm=(i == depth - 1))
    return xt.reshape(bsz, s, d)
```

```python
import functools

import jax
import jax.numpy as jnp
from jax import lax
from jax.experimental import pallas as pl
from jax.experimental.pallas import tpu as pltpu

D_MODEL = 1024
D_CONV = 1024
D_POOL = 1024
CONV_WIDTH = 31
POOL_WINDOWS = (2, 4, 8, 16)
POOL_GROUP = 256
PLE_DIM = 256
N_EXPERTS = 64
N_GROUPS = 8
GROUP_SIZE = N_EXPERTS // N_GROUPS
TOPK_GROUPS = 4
TOP_K = 8
D_EXPERT = 256
ROUTED_SCALE = 2.5
NORM_EPS = 1e-6

F32 = jnp.float32
BF16 = jnp.bfloat16

MIX_TM = 256
MIX_NV = MIX_TM // 8
CONV_MG = 8
ROW_CHUNK = 64
LANE = 128

ROUTER_TM = 1024
WIN = 256
SEL_ROWS = 2560
SEL_RG = 64
SEL_MM = 512
EXP_BM = 576
EXP_XDEPTH = 6
EXP_YDEPTH = 3
EXP_SPLIT = 4
CMB_LG = 512

VMEM_LIMIT = 56 * 1024 * 1024


def _rms(x, g):
    ms = jnp.mean(x * x, axis=-1, keepdims=True)
    return x * lax.rsqrt(ms + NORM_EPS) * g


def _dot(a, b):
    return jnp.dot(a, b, preferred_element_type=F32)


def _dot_t(a, b):
    return lax.dot_general(a, b, (((0,), (0,)), ((), ())), preferred_element_type=F32)


def _mixer_kernel(x_ref, gmix_ref, win_ref, bin_ref, wdw_ref, bdw_ref, gcln_ref, bcln_ref,
                  wco_ref, bco_ref, wpool_ref, spool_ref, wout_ref, gffn_ref, perm_ref, unperm_ref,
                  x1_ref, h2_ref, a_ext, a_prev, u_ext, u_prev, c_buf, q_buf, *, tiles_per_seq):
    i = pl.program_id(0) % tiles_per_seq
    tm = MIX_TM
    nv = MIX_NV

    @pl.when(i == 0)
    def _():
        a_prev[...] = jnp.zeros(a_prev.shape, F32)
        u_prev[...] = jnp.zeros(u_prev.shape, F32)

    x = x_ref[...]
    h = _dot(perm_ref[...], _rms(x, gmix_ref[...]).astype(BF16)).astype(BF16)

    def proj(lo, hi):
        return _dot(h, win_ref[:, lo:hi]) + bin_ref[:, lo:hi]

    glu = proj(0, D_CONV) * jax.nn.sigmoid(proj(D_CONV, 2 * D_CONV))
    for lc in range(D_CONV // LANE):
        a_ext[lc, tm:2 * tm, :] = glu[:, lc * LANE:(lc + 1) * LANE]
    u_ext[tm:2 * tm, :] = proj(2 * D_CONV, 2 * D_CONV + D_POOL)

    def delayed_groups(ext, prev, first_group):
        last_row = lax.broadcasted_iota(jnp.int32, (8, ext.shape[-1]), 0) == 7
        for g in range(first_group, nv):
            rows = slice(8 * g, 8 * g + 8)
            mixed = jnp.where(last_row, prev[rows, :], ext[tm + 8 * g:tm + 8 * g + 8, :])
            ext[rows, :] = pltpu.roll(mixed, 1, axis=0)
            prev[rows, :] = ext[tm + 8 * g:tm + 8 * g + 8, :]

    delayed_groups(u_ext, u_prev, nv - (max(POOL_WINDOWS) - 1))

    def conv_column(lc, carry):
        a_col = a_ext.at[lc]
        delayed_groups(a_col, a_prev.at[lc], nv - (CONV_WIDTH - 1))
        w_col = wdw_ref.at[lc]
        for g0 in range(0, nv, CONV_MG):
            acc = None
            for k in range(CONV_WIDTH):
                src = nv + g0 + k - (CONV_WIDTH - 1)
                term = a_col[8 * src:8 * (src + CONV_MG), :] * w_col[k:k + 1, :]
                acc = term if acc is None else acc + term
            c_buf[lc, 8 * g0:8 * (g0 + CONV_MG), :] = acc + bdw_ref[lc]
        return carry
    lax.fori_loop(0, D_CONV // LANE, conv_column, 0)

    c = jnp.concatenate([c_buf[lc] for lc in range(D_CONV // LANE)], axis=-1)
    mu = jnp.mean(c, axis=-1, keepdims=True)
    xc = c - mu
    var = jnp.mean(xc * xc, axis=-1, keepdims=True)
    y = xc * lax.rsqrt(var + NORM_EPS) * gcln_ref[...] + bcln_ref[...]
    y = y * jax.nn.sigmoid(y)
    branch_a = _dot(y.astype(BF16), wco_ref[...]) + bco_ref[...]

    for r0 in range(0, tm, ROW_CHUNK):
        row = r0 + lax.broadcasted_iota(jnp.int32, (ROW_CHUNK, POOL_GROUP), 0)
        t1 = i * tm + (row % 8) * nv + row // 8 + 1
        for gi, w in enumerate(POOL_WINDOWS):
            ls = slice(gi * POOL_GROUP, (gi + 1) * POOL_GROUP)
            tok = u_ext[tm + r0:tm + r0 + ROW_CHUNK, ls]
            s = tok
            for j in range(1, w):
                s = s + u_ext[tm + r0 - 8 * j:tm + r0 - 8 * j + ROW_CHUNK, ls]
            cnt = jnp.minimum(t1, w).astype(F32)
            q_buf[r0:r0 + ROW_CHUNK, ls] = s / cnt - tok

    qs_out = []
    for gi in range(len(POOL_WINDOWS)):
        ls = slice(gi * POOL_GROUP, (gi + 1) * POOL_GROUP)
        qs_out.append(_dot(q_buf[:, ls].astype(BF16), wpool_ref[gi]) * spool_ref[:, ls])
    branch_b = jnp.concatenate(qs_out, axis=-1)

    c2 = 2 * D_CONV + D_POOL
    gate_a = jax.nn.sigmoid(proj(c2, c2 + D_MODEL))
    gate_b = jax.nn.sigmoid(proj(c2 + D_MODEL, c2 + 2 * D_MODEL))
    merged = gate_a * branch_a + gate_b * branch_b
    merged = _dot(unperm_ref[...], merged.astype(BF16)).astype(BF16)
    x1 = x + _dot(merged, wout_ref[...])
    x1_ref[...] = x1
    h2_ref[...] = _rms(x1, gffn_ref[...]).astype(BF16)


def _const_spec(shape):
    n = len(shape)
    return pl.BlockSpec(shape, lambda i, _n=n: (0,) * _n)


def _mixer(x, seq_len, g_mix, w_in, b_in, w_dw, b_dw, g_cln, b_cln, w_co, b_co, w_pool, s_pool, w_out,
           g_ffn):
    t = x.shape[0]
    tm = MIX_TM
    assert seq_len % tm == 0 and MIX_NV >= CONV_WIDTH and MIX_NV >= max(POOL_WINDOWS)
    d_in = w_in.shape[1]
    row = pl.BlockSpec((tm, D_MODEL), lambda i: (i, 0))
    n_col = D_CONV // LANE
    w_dw = w_dw.reshape(CONV_WIDTH, n_col, LANE).transpose(1, 0, 2)
    b_dw = b_dw.reshape(n_col, 1, LANE)
    r = jnp.arange(tm)
    perm = ((r % 8) * MIX_NV + r // 8)[:, None] == jnp.arange(tm)[None, :]
    perm = perm.astype(BF16)
    return pl.pallas_call(
        functools.partial(_mixer_kernel, tiles_per_seq=seq_len // tm),
        grid=(t // tm,),
        in_specs=[
            row,
            _const_spec((1, D_MODEL)),
            _const_spec((D_MODEL, d_in)),
            _const_spec((1, d_in)),
            _const_spec((n_col, CONV_WIDTH, LANE)),
            _const_spec((n_col, 1, LANE)),
            _const_spec((1, D_CONV)),
            _const_spec((1, D_CONV)),
            _const_spec((D_CONV, D_MODEL)),
            _const_spec((1, D_MODEL)),
            _const_spec((len(POOL_WINDOWS), POOL_GROUP, POOL_GROUP)),
            _const_spec((1, D_POOL)),
            _const_spec((D_MODEL, D_MODEL)),
            _const_spec((1, D_MODEL)),
            _const_spec((tm, tm)),
            _const_spec((tm, tm)),
        ],
        out_specs=[row, row],
        out_shape=[jax.ShapeDtypeStruct((t, D_MODEL), F32),
                   jax.ShapeDtypeStruct((t, D_MODEL), BF16)],
        scratch_shapes=[
            pltpu.VMEM((n_col, 2 * tm, LANE), F32),
            pltpu.VMEM((n_col, tm, LANE), F32),
            pltpu.VMEM((2 * tm, D_POOL), F32),
            pltpu.VMEM((tm, D_POOL), F32),
            pltpu.VMEM((n_col, tm, LANE), F32),
            pltpu.VMEM((tm, D_POOL), F32),
        ],
        compiler_params=pltpu.CompilerParams(
            dimension_semantics=("arbitrary",), vmem_limit_bytes=VMEM_LIMIT),
        name="mixer",
    )(x, g_mix, w_in, b_in, w_dw, b_dw, g_cln, b_cln, w_co, b_co, w_pool, s_pool, w_out, g_ffn, perm, perm.T)


def _beats(v, other, other_is_later):
    v = jnp.broadcast_to(v, other.shape)
    return jnp.where(other_is_later, jnp.where(v >= other, 1, 0), jnp.where(v > other, 1, 0))


def _router_kernel(h2_ref, wrt_ref, br_ref, utri_ref, ltri_ref, gate_ref, rank_ref, pos_ref, cnt_ref):
    tm = ROUTER_TM
    logits = lax.dot_general(wrt_ref[...], h2_ref[...], (((1,), (1,)), ((), ())),
                             preferred_element_type=F32)
    scores = jax.nn.sigmoid(logits)
    sel = scores + br_ref[...]
    shape3 = (N_GROUPS, GROUP_SIZE, tm)
    sel3 = sel.reshape(shape3)
    scores3 = scores.reshape(shape3)
    neg_inf = jnp.float32(-jnp.inf)

    member = lax.broadcasted_iota(jnp.int32, shape3, 1)
    m1 = jnp.max(sel3, axis=1, keepdims=True)
    first = jnp.min(jnp.where(sel3 == m1, member, GROUP_SIZE), axis=1, keepdims=True)
    m2 = jnp.max(jnp.where(member == first, neg_inf, sel3), axis=1, keepdims=True)
    gscore = jnp.broadcast_to(m1 + m2, shape3)

    gidx = lax.broadcasted_iota(jnp.int32, shape3, 0)
    grank = jnp.zeros(shape3, jnp.int32)
    for j in range(N_GROUPS):
        sj = gscore[j:j + 1]
        grank = grank + _beats(sj, gscore, gidx > j)
    masked = jnp.where(grank < TOPK_GROUPS, sel3, neg_inf)

    eidx = gidx * GROUP_SIZE + member
    work = masked
    erank = jnp.full(shape3, TOP_K, jnp.int32)
    for k in range(TOP_K):
        best = jnp.max(jnp.max(work, axis=0, keepdims=True), axis=1, keepdims=True)
        cand = jnp.where(work == best, eidx, N_EXPERTS)
        pick = jnp.min(jnp.min(cand, axis=0, keepdims=True), axis=1, keepdims=True)
        hit = eidx == pick
        work = jnp.where(hit, neg_inf, work)
        erank = jnp.where(hit, k, erank)
    chosen = erank < TOP_K
    top_s = jnp.where(chosen, scores3, 0.0)
    denom = jnp.sum(jnp.sum(top_s, axis=0, keepdims=True), axis=1, keepdims=True)
    gates3 = top_s / denom * ROUTED_SCALE
    chosen2 = jnp.where(chosen, 1.0, 0.0).reshape(N_EXPERTS, tm)
    gate_ref[...] = gates3.reshape(N_EXPERTS, tm).astype(BF16)

    for w in range(tm // WIN):
        ls = slice(w * WIN, (w + 1) * WIN)
        mw = chosen2[:, ls]
        rank = _dot(mw.astype(BF16), utri_ref[...])
        n = jnp.sum(mw, axis=1, keepdims=True)
        run = jnp.floor((n + 7.0) * 0.125) * 8.0
        start = _dot(ltri_ref[...], jnp.broadcast_to(run, (N_EXPERTS, WIN)).astype(BF16))
        rank_ref[:, ls] = jnp.where(mw > 0.5, rank, -1.0).astype(BF16)
        row3 = (rank + start).reshape(N_GROUPS, GROUP_SIZE, WIN)
        er = erank[:, :, ls]
        for k in range(TOP_K):
            pk = jnp.sum(jnp.sum(jnp.where(er == k, row3, 0.0), axis=0, keepdims=True), axis=1, keepdims=True)
            pos_ref[k:k + 1, ls] = pk.reshape(1, WIN).astype(jnp.int32)
        cnt_ref[w] = n


def _router(h2, w_rt, b_r):
    t = h2.shape[0]
    tm = ROUTER_TM
    utri = jnp.triu(jnp.ones((WIN, WIN), BF16), k=1)
    ltri = jnp.tril(jnp.ones((N_EXPERTS, N_EXPERTS), BF16), k=-1)
    return pl.pallas_call(
        _router_kernel,
        grid=(t // tm,),
        in_specs=[
            pl.BlockSpec((tm, D_MODEL), lambda i: (i, 0)),
            _const_spec((N_EXPERTS, D_MODEL)),
            _const_spec((N_EXPERTS, 1)),
            _const_spec((WIN, WIN)),
            _const_spec((N_EXPERTS, N_EXPERTS)),
        ],
        out_specs=[
            pl.BlockSpec((N_EXPERTS, tm), lambda i: (0, i)),
            pl.BlockSpec((N_EXPERTS, tm), lambda i: (0, i)),
            pl.BlockSpec((TOP_K, tm), lambda i: (0, i)),
            pl.BlockSpec((tm // WIN, N_EXPERTS, 1), lambda i: (i, 0, 0)),
        ],
        out_shape=[
            jax.ShapeDtypeStruct((N_EXPERTS, t), BF16),
            jax.ShapeDtypeStruct((N_EXPERTS, t), BF16),
            jax.ShapeDtypeStruct((TOP_K, t), jnp.int32),
            jax.ShapeDtypeStruct((t // WIN, N_EXPERTS, 1), F32),
        ],
        compiler_params=pltpu.CompilerParams(
            dimension_semantics=("arbitrary",), vmem_limit_bytes=VMEM_LIMIT),
        name="router",
    )(h2, w_rt, b_r, utri, ltri)


def _sorted_rows_bound(t):
    rows = t * TOP_K + (t // WIN) * N_EXPERTS * 7 + N_EXPERTS * (EXP_BM - 1)
    return -(-rows // EXP_BM) * EXP_BM


def _dispatch_plan(cnt, t):
    nw = t // WIN
    n = cnt.reshape(nw, N_EXPERTS).astype(jnp.int32)
    run = (n + 7) // 8 * 8
    local_end = jnp.cumsum(run, axis=1)
    local_off = jnp.concatenate([jnp.zeros((nw, 1), jnp.int32), local_end], axis=1)
    total = jnp.sum(run, axis=0)
    region = (total + EXP_BM - 1) // EXP_BM * EXP_BM
    region_end = jnp.cumsum(region)
    base = region_end - region
    global_off = base[None, :] + jnp.cumsum(run, axis=0) - run
    n_blocks = _sorted_rows_bound(t) // EXP_BM
    n_used = region_end[-1] // EXP_BM
    blk = jnp.arange(n_blocks, dtype=jnp.int32)
    blk_expert = jnp.sum((region_end[None, :] <= blk[:, None] * EXP_BM).astype(jnp.int32), axis=1)
    blk_expert = jnp.minimum(blk_expert, N_EXPERTS - 1)
    eid = jnp.arange(N_EXPERTS, dtype=jnp.int32)
    later_nonempty = (eid[None, :] > eid[:, None]) & (region[None, :] > 0)
    next_expert = jnp.min(jnp.where(later_nonempty, eid[None, :], N_EXPERTS), axis=1).astype(jnp.int32)
    return dict(
        run_lo=local_off[:, :N_EXPERTS].reshape(nw, N_EXPERTS, 1),
        run_hi=local_off[:, 1:].reshape(nw, N_EXPERTS, 1),
        local_off=local_off.reshape(-1), global_off=global_off.reshape(-1),
        fill_off=base + total, fill_cnt=region - total,
        blk_expert=blk_expert.astype(jnp.int32), next_expert=next_expert,
        n_used=n_used.reshape(1).astype(jnp.int32))


def _run_copy(local_ref, global_ref, win, e, vmem_buf, slot, hbm_buf, sem, to_hbm):
    lo = pl.multiple_of(local_ref[win * (N_EXPERTS + 1) + e], 8)
    cnt = pl.multiple_of(local_ref[win * (N_EXPERTS + 1) + e + 1] - lo, 8)
    go = pl.multiple_of(global_ref[win * N_EXPERTS + e], 8)
    v = vmem_buf.at[pl.ds(pl.multiple_of(slot * SEL_ROWS + lo, 8), cnt)]
    h = hbm_buf.at[pl.ds(go, cnt)]
    cp = pltpu.make_async_copy(v, h, sem.at[slot]) if to_hbm else pltpu.make_async_copy(h, v, sem.at[slot])
    return cnt, cp


def _start_runs(local_ref, global_ref, win, vmem_buf, slot, hbm_buf, sem, to_hbm):
    def body(e, carry):
        cnt, cp = _run_copy(local_ref, global_ref, win, e, vmem_buf, slot, hbm_buf, sem, to_hbm)

        @pl.when(cnt > 0)
        def _():
            cp.start()
        return carry
    lax.fori_loop(0, N_EXPERTS, body, 0)


def _wait_runs(local_ref, win, vmem_buf, slot, hbm_buf, sem, to_hbm):
    total = pl.multiple_of(local_ref[win * (N_EXPERTS + 1) + N_EXPERTS], 8)
    v = vmem_buf.at[pl.ds(pl.multiple_of(slot * SEL_ROWS, 8), total)]
    h = hbm_buf.at[pl.ds(0, total)]
    cp = pltpu.make_async_copy(v, h, sem.at[slot]) if to_hbm else pltpu.make_async_copy(h, v, sem.at[slot])

    @pl.when(total > 0)
    def _():
        cp.wait()


def _dispatch_kernel(local_ref, global_ref, fill_off_ref, fill_cnt_ref, h2_ref, pos_ref, xs_hbm, sbuf,
                     s_ref, sem, zsem, *, n_win):
    w = pl.program_id(0)
    slot = w % 2
    pos = pos_ref[...]

    h2 = h2_ref[...]
    assert SEL_RG <= 256
    rid_b = lax.broadcasted_iota(jnp.int32, (SEL_RG, WIN), 0).astype(F32).astype(BF16)
    one_b = jnp.ones((SEL_RG, WIN), BF16)
    for g in range(SEL_ROWS // SEL_MM):
        for sg in range(SEL_MM // SEL_RG):
            r0 = g * SEL_MM + sg * SEL_RG
            acc = jnp.zeros((SEL_RG, WIN), BF16)
            for k in range(TOP_K):
                off = (pos[k:k + 1, :] - r0).astype(F32)
                off = jnp.broadcast_to(off, (SEL_RG, WIN)).astype(BF16)
                acc = jnp.where(rid_b == off, one_b, acc)
            s_ref[r0:r0 + SEL_RG, :] = acc
        rows = slice(g * SEL_MM, (g + 1) * SEL_MM)
        dst = pl.multiple_of(slot * SEL_ROWS + g * SEL_MM, SEL_MM)
        sbuf[pl.ds(dst, SEL_MM), :] = _dot(s_ref[rows, :], h2).astype(BF16)

    _start_runs(local_ref, global_ref, w, sbuf, slot, xs_hbm, sem, True)

    @pl.when(w > 0)
    def _():
        _wait_runs(local_ref, w - 1, sbuf, 1 - slot, xs_hbm, sem, True)

    @pl.when(w == n_win - 1)
    def _():
        sbuf[2 * SEL_ROWS:, :] = jnp.zeros((EXP_BM, D_MODEL), BF16)

        def fill(e, wait):
            cnt = pl.multiple_of(fill_cnt_ref[e], 8)
            off = pl.multiple_of(fill_off_ref[e], 8)
            cp = pltpu.make_async_copy(sbuf.at[pl.ds(2 * SEL_ROWS, cnt)], xs_hbm.at[pl.ds(off, cnt)], zsem)

            @pl.when(cnt > 0)
            def _():
                if wait:
                    cp.wait()
                else:
                    cp.start()

        def start_body(e, carry):
            fill(e, False)
            return carry

        def wait_body(e, carry):
            fill(e, True)
            return carry
        lax.fori_loop(0, N_EXPERTS, start_body, 0)
        _wait_runs(local_ref, w, sbuf, slot, xs_hbm, sem, True)
        lax.fori_loop(0, N_EXPERTS, wait_body, 0)


def _staging_shape(extra_rows):
    return jax.ShapeDtypeStruct((2 * SEL_ROWS + extra_rows, D_MODEL), BF16)


def _staging_spec(extra_rows):
    return pl.BlockSpec((2 * SEL_ROWS + extra_rows, D_MODEL), lambda w, *_: (0, 0))


def _dispatch(plan, h2, pos):
    t = h2.shape[0]
    n_win = t // WIN
    return pl.pallas_call(
        functools.partial(_dispatch_kernel, n_win=n_win),
        grid_spec=pltpu.PrefetchScalarGridSpec(
            num_scalar_prefetch=4,
            grid=(n_win,),
            in_specs=[
                pl.BlockSpec((WIN, D_MODEL), lambda w, *_: (w, 0)),
                pl.BlockSpec((TOP_K, WIN), lambda w, *_: (0, w)),
            ],
            out_specs=[pl.BlockSpec(memory_space=pl.ANY), _staging_spec(EXP_BM)],
            scratch_shapes=[
                pltpu.VMEM((SEL_ROWS, WIN), BF16),
                pltpu.SemaphoreType.DMA((2,)),
                pltpu.SemaphoreType.DMA,
            ]),
        out_shape=[jax.ShapeDtypeStruct((_sorted_rows_bound(t), D_MODEL), BF16), _staging_shape(EXP_BM)],
        compiler_params=pltpu.CompilerParams(
            dimension_semantics=("arbitrary",), vmem_limit_bytes=VMEM_LIMIT),
        name="dispatch",
    )(plan['local_off'], plan['global_off'], plan['fill_off'], plan['fill_cnt'], h2, pos)[0]


def _expert_kernel(blk_expert_ref, next_expert_ref, n_used_ref, xs_hbm, wg_hbm, wu_hbm, wd_hbm, ys_hbm,
                   xbuf, ybuf, wg_st, wu_st, wd_st, wg_bf, wu_bf, wd_bf, xsem, ysem, wsem):
    n_used = n_used_ref[0]
    part = EXP_BM // EXP_SPLIT

    def row_copies(b, slot, fetch):
        out = []
        for q in range(EXP_SPLIT):
            hbm_rows = pl.ds(pl.multiple_of(b * EXP_BM + q * part, part), part)
            if fetch:
                out.append(pltpu.make_async_copy(xs_hbm.at[hbm_rows], xbuf.at[slot, q * part:(q + 1) * part],
                                                 xsem.at[slot]))
            else:
                out.append(pltpu.make_async_copy(ybuf.at[slot, q * part:(q + 1) * part], ys_hbm.at[hbm_rows],
                                                 ysem.at[slot]))
        return out

    def weight_copies(e, slot):
        return [pltpu.make_async_copy(wg_hbm.at[e], wg_st.at[slot], wsem.at[slot]),
                pltpu.make_async_copy(wu_hbm.at[e], wu_st.at[slot], wsem.at[slot]),
                pltpu.make_async_copy(wd_hbm.at[e], wd_st.at[slot], wsem.at[slot])]

    def start(copies):
        for c in copies:
            c.start()

    def wait(copies):
        for c in copies:
            c.wait()

    for ahead in range(EXP_XDEPTH - 1):
        @pl.when(ahead < n_used)
        def _(ahead=ahead):
            start(row_copies(ahead, ahead, True))

    @pl.when(n_used > 0)
    def _():
        start(weight_copies(blk_expert_ref[0], 0))

    def body(b, wslot):
        e = blk_expert_ref[b]
        new_expert = jnp.logical_or(b == 0, e != blk_expert_ref[jnp.maximum(b - 1, 0)])

        @pl.when(new_expert)
        def _():
            wait(weight_copies(e, wslot))
            wg_bf[...] = wg_st[wslot].astype(BF16)
            wu_bf[...] = wu_st[wslot].astype(BF16)
            wd_bf[...] = wd_st[wslot].astype(BF16)
            nxt = next_expert_ref[e]

            @pl.when(nxt < N_EXPERTS)
            def _():
                start(weight_copies(nxt, 1 - wslot))

        @pl.when(b + EXP_XDEPTH - 1 < n_used)
        def _():
            start(row_copies(b + EXP_XDEPTH - 1, (b + EXP_XDEPTH - 1) % EXP_XDEPTH, True))

        xslot = b % EXP_XDEPTH
        yslot = b % EXP_YDEPTH
        wait(row_copies(b, xslot, True))

        @pl.when(b >= EXP_YDEPTH)
        def _():
            wait(row_copies(b - EXP_YDEPTH, yslot, False))

        x = xbuf[xslot]
        hg = _dot(x, wg_bf[...])
        hb = hg * jax.nn.sigmoid(hg) * _dot(x, wu_bf[...])
        ybuf[yslot] = _dot(hb.astype(BF16), wd_bf[...]).astype(BF16)
        start(row_copies(b, yslot, False))
        return jnp.where(new_expert, 1 - wslot, wslot)

    lax.fori_loop(0, n_used, body, jnp.int32(0))

    for back in range(EXP_YDEPTH, 0, -1):
        @pl.when(n_used >= back)
        def _(back=back):
            wait(row_copies(n_used - back, (n_used - back) % EXP_YDEPTH, False))


def _experts(plan, xs, w_gate, w_up, w_down):
    any_spec = pl.BlockSpec(memory_space=pl.ANY)
    return pl.pallas_call(
        _expert_kernel,
        grid_spec=pltpu.PrefetchScalarGridSpec(
            num_scalar_prefetch=3,
            grid=(1,),
            in_specs=[any_spec, any_spec, any_spec, any_spec],
            out_specs=any_spec,
            scratch_shapes=[
                pltpu.VMEM((EXP_XDEPTH, EXP_BM, D_MODEL), BF16),
                pltpu.VMEM((EXP_YDEPTH, EXP_BM, D_MODEL), BF16),
                pltpu.VMEM((2, D_MODEL, D_EXPERT), w_gate.dtype),
                pltpu.VMEM((2, D_MODEL, D_EXPERT), w_up.dtype),
                pltpu.VMEM((2, D_EXPERT, D_MODEL), w_down.dtype),
                pltpu.VMEM((D_MODEL, D_EXPERT), BF16),
                pltpu.VMEM((D_MODEL, D_EXPERT), BF16),
                pltpu.VMEM((D_EXPERT, D_MODEL), BF16),
                pltpu.SemaphoreType.DMA((EXP_XDEPTH,)),
                pltpu.SemaphoreType.DMA((EXP_YDEPTH,)),
                pltpu.SemaphoreType.DMA((2,)),
            ]),
        out_shape=jax.ShapeDtypeStruct(xs.shape, BF16),
        compiler_params=pltpu.CompilerParams(
            dimension_semantics=("arbitrary",), vmem_limit_bytes=VMEM_LIMIT),
        name="experts",
    )(plan['blk_expert'], plan['next_expert'], plan['n_used'], xs, w_gate, w_up, w_down)


def _combine_kernel(local_ref, global_ref, x1_ref, h2_ref, p_ref, rank_ref, gate_ref, lo_ref, hi_ref,
                    wsg_ref, wsu_ref, wsd_ref, gple_ref, wpg_ref, wp_ref, gfin_ref, ys_hbm, o_ref, ybuf, st_ref, sem,
                    *, n_win, final_norm):
    w = pl.program_id(0)
    slot = w % 2

    @pl.when(w == 0)
    def _():
        ybuf[...] = jnp.zeros(ybuf.shape, BF16)
        _start_runs(local_ref, global_ref, w, ybuf, slot, ys_hbm, sem, False)

    @pl.when(w + 1 < n_win)
    def _():
        _start_runs(local_ref, global_ref, w + 1, ybuf, 1 - slot, ys_hbm, sem, False)

    lo = lo_ref[0]
    hi = hi_ref[0]
    lo_f = lo.astype(F32)
    rank_tbl = rank_ref[...]
    gate_tbl = gate_ref[...]

    def build_group(lg):
        cols = slice(lg * CMB_LG, (lg + 1) * CMB_LG)
        rid = lg * CMB_LG + lax.broadcasted_iota(jnp.int32, (N_EXPERTS, CMB_LG), 1)
        owner = jnp.where(rid >= lo, jnp.where(rid < hi, 1.0, 0.0), 0.0)
        run_row = rid[0:1, :].astype(F32) - jnp.sum(owner * lo_f, axis=0, keepdims=True)
        owner = owner.astype(BF16)
        hit = _dot_t(rank_tbl, owner) == run_row
        st_ref[:, cols] = jnp.where(hit, _dot_t(gate_tbl, owner), 0.0).astype(BF16)

    build_group(0)
    h2 = h2_ref[...]
    hs = _dot(h2, wsg_ref[...])
    hs = hs * jax.nn.sigmoid(hs) * _dot(h2, wsu_ref[...])
    shared = _dot(hs.astype(BF16), wsd_ref[...])

    _wait_runs(local_ref, w, ybuf, slot, ys_hbm, sem, False)
    routed = None
    n_groups = SEL_ROWS // CMB_LG
    for lg in range(n_groups):
        if lg + 1 < n_groups:
            build_group(lg + 1)
        src = pl.multiple_of(slot * SEL_ROWS + lg * CMB_LG, CMB_LG)
        part = _dot(st_ref[:, lg * CMB_LG:(lg + 1) * CMB_LG], ybuf[pl.ds(src, CMB_LG), :])
        routed = part if routed is None else routed + part
    x2 = x1_ref[...] + routed + shared

    hp = _rms(x2, gple_ref[...]).astype(BF16)
    gate = jax.nn.sigmoid(_dot(hp, wpg_ref[...]))
    x3 = x2 + gate * _dot(p_ref[...].astype(BF16), wp_ref[...])
    o_ref[...] = _rms(x3, gfin_ref[...]) if final_norm else x3


def _combine(plan, ys, x1, h2, p, rank_tbl, gate_tbl, wsg, wsu, wsd, g_ple, w_pg, w_p, g_fin, final_norm):
    t = x1.shape[0]
    n_win = t // WIN
    row = lambda width: pl.BlockSpec((WIN, width), lambda w, *_: (w, 0))
    const = lambda shape: pl.BlockSpec(shape, lambda w, *_: (0,) * len(shape))
    return pl.pallas_call(
        functools.partial(_combine_kernel, n_win=n_win, final_norm=final_norm),
        grid_spec=pltpu.PrefetchScalarGridSpec(
            num_scalar_prefetch=2,
            grid=(n_win,),
            in_specs=[
                row(D_MODEL), row(D_MODEL), row(PLE_DIM),
                pl.BlockSpec((N_EXPERTS, WIN), lambda w, *_: (0, w)),
                pl.BlockSpec((N_EXPERTS, WIN), lambda w, *_: (0, w)),
                pl.BlockSpec((1, N_EXPERTS, 1), lambda w, *_: (w, 0, 0)),
                pl.BlockSpec((1, N_EXPERTS, 1), lambda w, *_: (w, 0, 0)),
                const((D_MODEL, D_EXPERT)), const((D_MODEL, D_EXPERT)), const((D_EXPERT, D_MODEL)),
                const((1, D_MODEL)), const((D_MODEL, D_MODEL)), const((PLE_DIM, D_MODEL)),
                const((1, D_MODEL)),
                pl.BlockSpec(memory_space=pl.ANY),
            ],
            out_specs=[row(D_MODEL), _staging_spec(0)],
            scratch_shapes=[
                pltpu.VMEM((WIN, SEL_ROWS), BF16),
                pltpu.SemaphoreType.DMA((2,)),
            ]),
        out_shape=[jax.ShapeDtypeStruct((t, D_MODEL), F32), _staging_shape(0)],
        compiler_params=pltpu.CompilerParams(
            dimension_semantics=("arbitrary",), vmem_limit_bytes=VMEM_LIMIT),
        name="combine",
    )(plan['local_off'], plan['global_off'], x1, h2, p, rank_tbl, gate_tbl, plan['run_lo'], plan['run_hi'],
      wsg, wsu, wsd, g_ple, w_pg, w_p, g_fin, ys)[0]


def kernel(x, p, g_mix, w_in, b_in, w_dw, b_dw, g_cln, b_cln, w_conv_out, b_conv_out, w_pool, s_pool,
           w_out, g_ffn, w_router, b_router, w_e_gate, w_e_up, w_e_down, w_s_gate, w_s_up, w_s_down,
           g_ple, w_ple_gate, w_ple, g_final):
    bsz, s, d = x.shape
    t = bsz * s
    depth = w_in.shape[0]
    xt = x.reshape(t, d)
    row = lambda v: v.reshape(1, -1)
    for i in range(depth):
        x1, h2 = _mixer(
            xt, s, row(g_mix[i]), w_in[i].astype(BF16), row(b_in[i]), w_dw[i], row(b_dw[i]),
            row(g_cln[i]), row(b_cln[i]), w_conv_out[i].astype(BF16), row(b_conv_out[i]),
            w_pool[i].astype(BF16), row(s_pool[i]), w_out[i].astype(BF16), row(g_ffn[i]))
        gate, rank, pos, cnt = _router(h2, w_router[i].T.astype(BF16), b_router[i].reshape(N_EXPERTS, 1))
        plan = _dispatch_plan(cnt, t)
        xs = _dispatch(plan, h2, pos)
        ys = _experts(plan, xs, w_e_gate[i].astype(BF16), w_e_up[i].astype(BF16), w_e_down[i].astype(BF16))
        xt = _combine(
            plan, ys, x1, h2, p[i].reshape(t, PLE_DIM), rank, gate,
            w_s_gate[i].astype(BF16), w_s_up[i].astype(BF16), w_s_down[i].astype(BF16),
            row(g_ple[i]), w_ple_gate[i].astype(BF16), w_ple[i].astype(BF16), row(g_final),
            final_norm=(i == depth - 1))
    return xt.reshape(bsz, s, d)
```

```python
import functools

import jax
import jax.numpy as jnp
from jax import lax
from jax.experimental import pallas as pl
from jax.experimental.pallas import tpu as pltpu

D_MODEL = 1024
D_CONV = 1024
D_POOL = 1024
CONV_WIDTH = 31
POOL_WINDOWS = (2, 4, 8, 16)
POOL_GROUP = 256
PLE_DIM = 256
N_EXPERTS = 64
N_GROUPS = 8
GROUP_SIZE = N_EXPERTS // N_GROUPS
TOPK_GROUPS = 4
TOP_K = 8
D_EXPERT = 256
ROUTED_SCALE = 2.5
NORM_EPS = 1e-6

F32 = jnp.float32
BF16 = jnp.bfloat16

MIX_TM = 256
MIX_NV = MIX_TM // 8
CONV_MG = 8
ROW_CHUNK = 64
LANE = 128

ROUTER_TM = 1024
WIN = 256
SEL_ROWS = 2560
SEL_RG = 64
SEL_MM = 512
EXP_BM = 576
EXP_XDEPTH = 6
EXP_YDEPTH = 4
EXP_SPLIT = 4
CMB_LG = 512

VMEM_LIMIT = 56 * 1024 * 1024


def _rms(x, g):
    ms = jnp.mean(x * x, axis=-1, keepdims=True)
    return x * lax.rsqrt(ms + NORM_EPS) * g


def _dot(a, b):
    return jnp.dot(a, b, preferred_element_type=F32)


def _dot_t(a, b):
    return lax.dot_general(a, b, (((0,), (0,)), ((), ())), preferred_element_type=F32)


def _mixer_kernel(x_ref, gmix_ref, win_ref, bin_ref, wdw_ref, bdw_ref, gcln_ref, bcln_ref,
                  wco_ref, bco_ref, wpool_ref, spool_ref, wout_ref, gffn_ref, perm_ref, unperm_ref,
                  x1_ref, h2_ref, a_ext, a_prev, u_ext, u_prev, c_buf, q_buf, *, tiles_per_seq):
    i = pl.program_id(0) % tiles_per_seq
    tm = MIX_TM
    nv = MIX_NV

    @pl.when(i == 0)
    def _():
        a_prev[...] = jnp.zeros(a_prev.shape, F32)
        u_prev[...] = jnp.zeros(u_prev.shape, F32)

    x = x_ref[...]
    h = _dot(perm_ref[...], _rms(x, gmix_ref[...]).astype(BF16)).astype(BF16)

    def proj(lo, hi):
        return _dot(h, win_ref[:, lo:hi]) + bin_ref[:, lo:hi]

    glu = proj(0, D_CONV) * jax.nn.sigmoid(proj(D_CONV, 2 * D_CONV))
    for lc in range(D_CONV // LANE):
        a_ext[lc, tm:2 * tm, :] = glu[:, lc * LANE:(lc + 1) * LANE]
    u_ext[tm:2 * tm, :] = proj(2 * D_CONV, 2 * D_CONV + D_POOL)

    def delayed_groups(ext, prev, first_group):
        last_row = lax.broadcasted_iota(jnp.int32, (8, ext.shape[-1]), 0) == 7
        for g in range(first_group, nv):
            rows = slice(8 * g, 8 * g + 8)
            mixed = jnp.where(last_row, prev[rows, :], ext[tm + 8 * g:tm + 8 * g + 8, :])
            ext[rows, :] = pltpu.roll(mixed, 1, axis=0)
            prev[rows, :] = ext[tm + 8 * g:tm + 8 * g + 8, :]

    delayed_groups(u_ext, u_prev, nv - (max(POOL_WINDOWS) - 1))

    def conv_column(lc, carry):
        a_col = a_ext.at[lc]
        delayed_groups(a_col, a_prev.at[lc], nv - (CONV_WIDTH - 1))
        w_col = wdw_ref.at[lc]
        for g0 in range(0, nv, CONV_MG):
            acc = None
            for k in range(CONV_WIDTH):
                src = nv + g0 + k - (CONV_WIDTH - 1)
                term = a_col[8 * src:8 * (src + CONV_MG), :] * w_col[k:k + 1, :]
                acc = term if acc is None else acc + term
            c_buf[lc, 8 * g0:8 * (g0 + CONV_MG), :] = acc + bdw_ref[lc]
        return carry
    lax.fori_loop(0, D_CONV // LANE, conv_column, 0)

    c = jnp.concatenate([c_buf[lc] for lc in range(D_CONV // LANE)], axis=-1)
    mu = jnp.mean(c, axis=-1, keepdims=True)
    xc = c - mu
    var = jnp.mean(xc * xc, axis=-1, keepdims=True)
    y = xc * lax.rsqrt(var + NORM_EPS) * gcln_ref[...] + bcln_ref[...]
    y = y * jax.nn.sigmoid(y)
    branch_a = _dot(y.astype(BF16), wco_ref[...]) + bco_ref[...]

    for r0 in range(0, tm, ROW_CHUNK):
        row = r0 + lax.broadcasted_iota(jnp.int32, (ROW_CHUNK, POOL_GROUP), 0)
        t1 = i * tm + (row % 8) * nv + row // 8 + 1
        for gi, w in enumerate(POOL_WINDOWS):
            ls = slice(gi * POOL_GROUP, (gi + 1) * POOL_GROUP)
            tok = u_ext[tm + r0:tm + r0 + ROW_CHUNK, ls]
            s = tok
            for j in range(1, w):
                s = s + u_ext[tm + r0 - 8 * j:tm + r0 - 8 * j + ROW_CHUNK, ls]
            cnt = jnp.minimum(t1, w).astype(F32)
            q_buf[r0:r0 + ROW_CHUNK, ls] = s / cnt - tok

    qs_out = []
    for gi in range(len(POOL_WINDOWS)):
        ls = slice(gi * POOL_GROUP, (gi + 1) * POOL_GROUP)
        qs_out.append(_dot(q_buf[:, ls].astype(BF16), wpool_ref[gi]) * spool_ref[:, ls])
    branch_b = jnp.concatenate(qs_out, axis=-1)

    c2 = 2 * D_CONV + D_POOL
    gate_a = jax.nn.sigmoid(proj(c2, c2 + D_MODEL))
    gate_b = jax.nn.sigmoid(proj(c2 + D_MODEL, c2 + 2 * D_MODEL))
    merged = gate_a * branch_a + gate_b * branch_b
    merged = _dot(unperm_ref[...], merged.astype(BF16)).astype(BF16)
    x1 = x + _dot(merged, wout_ref[...])
    x1_ref[...] = x1
    h2_ref[...] = _rms(x1, gffn_ref[...]).astype(BF16)


def _const_spec(shape):
    n = len(shape)
    return pl.BlockSpec(shape, lambda i, _n=n: (0,) * _n)


def _mixer(x, seq_len, g_mix, w_in, b_in, w_dw, b_dw, g_cln, b_cln, w_co, b_co, w_pool, s_pool, w_out,
           g_ffn):
    t = x.shape[0]
    tm = MIX_TM
    assert seq_len % tm == 0 and MIX_NV >= CONV_WIDTH and MIX_NV >= max(POOL_WINDOWS)
    d_in = w_in.shape[1]
    row = pl.BlockSpec((tm, D_MODEL), lambda i: (i, 0))
    n_col = D_CONV // LANE
    w_dw = w_dw.reshape(CONV_WIDTH, n_col, LANE).transpose(1, 0, 2)
    b_dw = b_dw.reshape(n_col, 1, LANE)
    r = jnp.arange(tm)
    perm = ((r % 8) * MIX_NV + r // 8)[:, None] == jnp.arange(tm)[None, :]
    perm = perm.astype(BF16)
    return pl.pallas_call(
        functools.partial(_mixer_kernel, tiles_per_seq=seq_len // tm),
        grid=(t // tm,),
        in_specs=[
            row,
            _const_spec((1, D_MODEL)),
            _const_spec((D_MODEL, d_in)),
            _const_spec((1, d_in)),
            _const_spec((n_col, CONV_WIDTH, LANE)),
            _const_spec((n_col, 1, LANE)),
            _const_spec((1, D_CONV)),
            _const_spec((1, D_CONV)),
            _const_spec((D_CONV, D_MODEL)),
            _const_spec((1, D_MODEL)),
            _const_spec((len(POOL_WINDOWS), POOL_GROUP, POOL_GROUP)),
            _const_spec((1, D_POOL)),
            _const_spec((D_MODEL, D_MODEL)),
            _const_spec((1, D_MODEL)),
            _const_spec((tm, tm)),
            _const_spec((tm, tm)),
        ],
        out_specs=[row, row],
        out_shape=[jax.ShapeDtypeStruct((t, D_MODEL), F32),
                   jax.ShapeDtypeStruct((t, D_MODEL), BF16)],
        scratch_shapes=[
            pltpu.VMEM((n_col, 2 * tm, LANE), F32),
            pltpu.VMEM((n_col, tm, LANE), F32),
            pltpu.VMEM((2 * tm, D_POOL), F32),
            pltpu.VMEM((tm, D_POOL), F32),
            pltpu.VMEM((n_col, tm, LANE), F32),
            pltpu.VMEM((tm, D_POOL), F32),
        ],
        compiler_params=pltpu.CompilerParams(
            dimension_semantics=("arbitrary",), vmem_limit_bytes=VMEM_LIMIT),
        name="mixer",
    )(x, g_mix, w_in, b_in, w_dw, b_dw, g_cln, b_cln, w_co, b_co, w_pool, s_pool, w_out, g_ffn, perm, perm.T)


def _beats(v, other, other_is_later):
    v = jnp.broadcast_to(v, other.shape)
    return jnp.where(other_is_later, jnp.where(v >= other, 1, 0), jnp.where(v > other, 1, 0))


def _router_kernel(h2_ref, wrt_ref, br_ref, utri_ref, ltri_ref, gate_ref, rank_ref, pos_ref, cnt_ref):
    tm = ROUTER_TM
    logits = lax.dot_general(wrt_ref[...], h2_ref[...], (((1,), (1,)), ((), ())),
                             preferred_element_type=F32)
    scores = jax.nn.sigmoid(logits)
    sel = scores + br_ref[...]
    shape3 = (N_GROUPS, GROUP_SIZE, tm)
    sel3 = sel.reshape(shape3)
    scores3 = scores.reshape(shape3)
    neg_inf = jnp.float32(-jnp.inf)

    member = lax.broadcasted_iota(jnp.int32, shape3, 1)
    m1 = jnp.max(sel3, axis=1, keepdims=True)
    first = jnp.min(jnp.where(sel3 == m1, member, GROUP_SIZE), axis=1, keepdims=True)
    m2 = jnp.max(jnp.where(member == first, neg_inf, sel3), axis=1, keepdims=True)
    gscore = jnp.broadcast_to(m1 + m2, shape3)

    gidx = lax.broadcasted_iota(jnp.int32, shape3, 0)
    grank = jnp.zeros(shape3, jnp.int32)
    for j in range(N_GROUPS):
        sj = gscore[j:j + 1]
        grank = grank + _beats(sj, gscore, gidx > j)
    masked = jnp.where(grank < TOPK_GROUPS, sel3, neg_inf)

    eidx = gidx * GROUP_SIZE + member
    work = masked
    erank = jnp.full(shape3, TOP_K, jnp.int32)
    for k in range(TOP_K):
        best = jnp.max(jnp.max(work, axis=0, keepdims=True), axis=1, keepdims=True)
        cand = jnp.where(work == best, eidx, N_EXPERTS)
        pick = jnp.min(jnp.min(cand, axis=0, keepdims=True), axis=1, keepdims=True)
        hit = eidx == pick
        work = jnp.where(hit, neg_inf, work)
        erank = jnp.where(hit, k, erank)
    chosen = erank < TOP_K
    top_s = jnp.where(chosen, scores3, 0.0)
    denom = jnp.sum(jnp.sum(top_s, axis=0, keepdims=True), axis=1, keepdims=True)
    gates3 = top_s / denom * ROUTED_SCALE
    chosen2 = jnp.where(chosen, 1.0, 0.0).reshape(N_EXPERTS, tm)
    gate_ref[...] = gates3.reshape(N_EXPERTS, tm).astype(BF16)

    for w in range(tm // WIN):
        ls = slice(w * WIN, (w + 1) * WIN)
        mw = chosen2[:, ls]
        rank = _dot(mw.astype(BF16), utri_ref[...])
        n = jnp.sum(mw, axis=1, keepdims=True)
        run = jnp.floor((n + 7.0) * 0.125) * 8.0
        start = _dot(ltri_ref[...], jnp.broadcast_to(run, (N_EXPERTS, WIN)).astype(BF16))
        rank_ref[:, ls] = jnp.where(mw > 0.5, rank, -1.0).astype(BF16)
        row3 = (rank + start).reshape(N_GROUPS, GROUP_SIZE, WIN)
        er = erank[:, :, ls]
        for k in range(TOP_K):
            pk = jnp.sum(jnp.sum(jnp.where(er == k, row3, 0.0), axis=0, keepdims=True), axis=1, keepdims=True)
            pos_ref[k:k + 1, ls] = pk.reshape(1, WIN).astype(jnp.int32)
        cnt_ref[w] = n


def _router(h2, w_rt, b_r):
    t = h2.shape[0]
    tm = ROUTER_TM
    utri = jnp.triu(jnp.ones((WIN, WIN), BF16), k=1)
    ltri = jnp.tril(jnp.ones((N_EXPERTS, N_EXPERTS), BF16), k=-1)
    return pl.pallas_call(
        _router_kernel,
        grid=(t // tm,),
        in_specs=[
            pl.BlockSpec((tm, D_MODEL), lambda i: (i, 0)),
            _const_spec((N_EXPERTS, D_MODEL)),
            _const_spec((N_EXPERTS, 1)),
            _const_spec((WIN, WIN)),
            _const_spec((N_EXPERTS, N_EXPERTS)),
        ],
        out_specs=[
            pl.BlockSpec((N_EXPERTS, tm), lambda i: (0, i)),
            pl.BlockSpec((N_EXPERTS, tm), lambda i: (0, i)),
            pl.BlockSpec((TOP_K, tm), lambda i: (0, i)),
            pl.BlockSpec((tm // WIN, N_EXPERTS, 1), lambda i: (i, 0, 0)),
        ],
        out_shape=[
            jax.ShapeDtypeStruct((N_EXPERTS, t), BF16),
            jax.ShapeDtypeStruct((N_EXPERTS, t), BF16),
            jax.ShapeDtypeStruct((TOP_K, t), jnp.int32),
            jax.ShapeDtypeStruct((t // WIN, N_EXPERTS, 1), F32),
        ],
        compiler_params=pltpu.CompilerParams(
            dimension_semantics=("arbitrary",), vmem_limit_bytes=VMEM_LIMIT),
        name="router",
    )(h2, w_rt, b_r, utri, ltri)


def _sorted_rows_bound(t):
    rows = t * TOP_K + (t // WIN) * N_EXPERTS * 7 + N_EXPERTS * (EXP_BM - 1)
    blocks = -(-rows // EXP_BM)
    return (blocks + blocks % 2) * EXP_BM


def _dispatch_plan(cnt, t):
    nw = t // WIN
    n = cnt.reshape(nw, N_EXPERTS).astype(jnp.int32)
    run = (n + 7) // 8 * 8
    local_end = jnp.cumsum(run, axis=1)
    local_off = jnp.concatenate([jnp.zeros((nw, 1), jnp.int32), local_end], axis=1)
    total = jnp.sum(run, axis=0)
    region = (total + EXP_BM - 1) // EXP_BM * EXP_BM
    eid = jnp.arange(N_EXPERTS, dtype=jnp.int32)
    last_owner = jnp.max(jnp.where(region > 0, eid, 0))
    odd = (jnp.sum(region) // EXP_BM) % 2
    region = region + jnp.where(eid == last_owner, odd * EXP_BM, 0)
    region_end = jnp.cumsum(region)
    base = region_end - region
    global_off = base[None, :] + jnp.cumsum(run, axis=0) - run
    n_blocks = _sorted_rows_bound(t) // EXP_BM
    n_used = region_end[-1] // EXP_BM
    blk = jnp.arange(n_blocks, dtype=jnp.int32)
    blk_expert = jnp.sum((region_end[None, :] <= blk[:, None] * EXP_BM).astype(jnp.int32), axis=1)
    blk_expert = jnp.minimum(blk_expert, N_EXPERTS - 1)
    later_nonempty = (eid[None, :] > eid[:, None]) & (region[None, :] > 0)
    next_expert = jnp.min(jnp.where(later_nonempty, eid[None, :], N_EXPERTS), axis=1).astype(jnp.int32)
    return dict(
        run_lo=local_off[:, :N_EXPERTS].reshape(nw, N_EXPERTS, 1),
        run_hi=local_off[:, 1:].reshape(nw, N_EXPERTS, 1),
        local_off=local_off.reshape(-1), global_off=global_off.reshape(-1),
        fill_off=base + total, fill_cnt=region - total,
        blk_expert=blk_expert.astype(jnp.int32), next_expert=next_expert,
        n_used=n_used.reshape(1).astype(jnp.int32))


def _run_copy(local_ref, global_ref, win, e, vmem_buf, slot, hbm_buf, sem, to_hbm):
    lo = pl.multiple_of(local_ref[win * (N_EXPERTS + 1) + e], 8)
    cnt = pl.multiple_of(local_ref[win * (N_EXPERTS + 1) + e + 1] - lo, 8)
    go = pl.multiple_of(global_ref[win * N_EXPERTS + e], 8)
    v = vmem_buf.at[pl.ds(pl.multiple_of(slot * SEL_ROWS + lo, 8), cnt)]
    h = hbm_buf.at[pl.ds(go, cnt)]
    cp = pltpu.make_async_copy(v, h, sem.at[slot]) if to_hbm else pltpu.make_async_copy(h, v, sem.at[slot])
    return cnt, cp


def _start_runs(local_ref, global_ref, win, vmem_buf, slot, hbm_buf, sem, to_hbm):
    def body(e, carry):
        cnt, cp = _run_copy(local_ref, global_ref, win, e, vmem_buf, slot, hbm_buf, sem, to_hbm)

        @pl.when(cnt > 0)
        def _():
            cp.start()
        return carry
    lax.fori_loop(0, N_EXPERTS, body, 0)


def _wait_runs(local_ref, win, vmem_buf, slot, hbm_buf, sem, to_hbm):
    total = pl.multiple_of(local_ref[win * (N_EXPERTS + 1) + N_EXPERTS], 8)
    v = vmem_buf.at[pl.ds(pl.multiple_of(slot * SEL_ROWS, 8), total)]
    h = hbm_buf.at[pl.ds(0, total)]
    cp = pltpu.make_async_copy(v, h, sem.at[slot]) if to_hbm else pltpu.make_async_copy(h, v, sem.at[slot])

    @pl.when(total > 0)
    def _():
        cp.wait()


def _dispatch_kernel(local_ref, global_ref, fill_off_ref, fill_cnt_ref, h2_ref, pos_ref, xs_hbm, sbuf,
                     s_ref, sem, zsem, *, n_win):
    w = pl.program_id(0)
    slot = w % 2
    pos = pos_ref[...]

    h2 = h2_ref[...]
    assert SEL_RG <= 256
    rid_b = lax.broadcasted_iota(jnp.int32, (SEL_RG, WIN), 0).astype(F32).astype(BF16)
    one_b = jnp.ones((SEL_RG, WIN), BF16)
    for g in range(SEL_ROWS // SEL_MM):
        for sg in range(SEL_MM // SEL_RG):
            r0 = g * SEL_MM + sg * SEL_RG
            acc = jnp.zeros((SEL_RG, WIN), BF16)
            for k in range(TOP_K):
                off = (pos[k:k + 1, :] - r0).astype(F32)
                off = jnp.broadcast_to(off, (SEL_RG, WIN)).astype(BF16)
                acc = jnp.where(rid_b == off, one_b, acc)
            s_ref[r0:r0 + SEL_RG, :] = acc
        rows = slice(g * SEL_MM, (g + 1) * SEL_MM)
        dst = pl.multiple_of(slot * SEL_ROWS + g * SEL_MM, SEL_MM)
        sbuf[pl.ds(dst, SEL_MM), :] = _dot(s_ref[rows, :], h2).astype(BF16)

    _start_runs(local_ref, global_ref, w, sbuf, slot, xs_hbm, sem, True)

    @pl.when(w > 0)
    def _():
        _wait_runs(local_ref, w - 1, sbuf, 1 - slot, xs_hbm, sem, True)

    @pl.when(w == n_win - 1)
    def _():
        sbuf[2 * SEL_ROWS:, :] = jnp.zeros((2 * EXP_BM, D_MODEL), BF16)

        def fill(e, wait):
            cnt = pl.multiple_of(fill_cnt_ref[e], 8)
            off = pl.multiple_of(fill_off_ref[e], 8)
            cp = pltpu.make_async_copy(sbuf.at[pl.ds(2 * SEL_ROWS, cnt)], xs_hbm.at[pl.ds(off, cnt)], zsem)

            @pl.when(cnt > 0)
            def _():
                if wait:
                    cp.wait()
                else:
                    cp.start()

        def start_body(e, carry):
            fill(e, False)
            return carry

        def wait_body(e, carry):
            fill(e, True)
            return carry
        lax.fori_loop(0, N_EXPERTS, start_body, 0)
        _wait_runs(local_ref, w, sbuf, slot, xs_hbm, sem, True)
        lax.fori_loop(0, N_EXPERTS, wait_body, 0)


def _staging_shape(extra_rows):
    return jax.ShapeDtypeStruct((2 * SEL_ROWS + extra_rows, D_MODEL), BF16)


def _staging_spec(extra_rows):
    return pl.BlockSpec((2 * SEL_ROWS + extra_rows, D_MODEL), lambda w, *_: (0, 0))


def _dispatch(plan, h2, pos):
    t = h2.shape[0]
    n_win = t // WIN
    return pl.pallas_call(
        functools.partial(_dispatch_kernel, n_win=n_win),
        grid_spec=pltpu.PrefetchScalarGridSpec(
            num_scalar_prefetch=4,
            grid=(n_win,),
            in_specs=[
                pl.BlockSpec((WIN, D_MODEL), lambda w, *_: (w, 0)),
                pl.BlockSpec((TOP_K, WIN), lambda w, *_: (0, w)),
            ],
            out_specs=[pl.BlockSpec(memory_space=pl.ANY), _staging_spec(2 * EXP_BM)],
            scratch_shapes=[
                pltpu.VMEM((SEL_ROWS, WIN), BF16),
                pltpu.SemaphoreType.DMA((2,)),
                pltpu.SemaphoreType.DMA,
            ]),
        out_shape=[jax.ShapeDtypeStruct((_sorted_rows_bound(t), D_MODEL), BF16), _staging_shape(2 * EXP_BM)],
        compiler_params=pltpu.CompilerParams(
            dimension_semantics=("arbitrary",), vmem_limit_bytes=VMEM_LIMIT),
        name="dispatch",
    )(plan['local_off'], plan['global_off'], plan['fill_off'], plan['fill_cnt'], h2, pos)[0]


def _expert_kernel(blk_expert_ref, next_expert_ref, n_used_ref, xs_hbm, wg_hbm, wu_hbm, wd_hbm, ys_hbm,
                   xbuf, ybuf, wg_st, wu_st, wd_st, wg_bf, wu_bf, wd_bf, xsem, ysem, wsem):
    n_used = n_used_ref[0]
    part = EXP_BM // EXP_SPLIT

    def row_copies(b, slot, fetch):
        out = []
        for q in range(EXP_SPLIT):
            hbm_rows = pl.ds(pl.multiple_of(b * EXP_BM + q * part, part), part)
            if fetch:
                out.append(pltpu.make_async_copy(xs_hbm.at[hbm_rows], xbuf.at[slot, q * part:(q + 1) * part],
                                                 xsem.at[slot]))
            else:
                out.append(pltpu.make_async_copy(ybuf.at[slot, q * part:(q + 1) * part], ys_hbm.at[hbm_rows],
                                                 ysem.at[slot]))
        return out

    def weight_copies(e, slot):
        return [pltpu.make_async_copy(wg_hbm.at[e], wg_st.at[slot], wsem.at[slot]),
                pltpu.make_async_copy(wu_hbm.at[e], wu_st.at[slot], wsem.at[slot]),
                pltpu.make_async_copy(wd_hbm.at[e], wd_st.at[slot], wsem.at[slot])]

    def start(copies):
        for c in copies:
            c.start()

    def wait(copies):
        for c in copies:
            c.wait()

    for ahead in range(EXP_XDEPTH - 2):
        @pl.when(ahead < n_used)
        def _(ahead=ahead):
            start(row_copies(ahead, ahead, True))

    @pl.when(n_used > 0)
    def _():
        start(weight_copies(blk_expert_ref[0], 0))

    def enter_block(b, wset):
        e = blk_expert_ref[b]
        new_expert = jnp.logical_or(b == 0, e != blk_expert_ref[jnp.maximum(b - 1, 0)])
        wset = jnp.where(new_expert, 1 - wset, wset)

        @pl.when(new_expert)
        def _():
            wait(weight_copies(e, wset))
            wg_bf[wset] = wg_st[wset].astype(BF16)
            wu_bf[wset] = wu_st[wset].astype(BF16)
            wd_bf[wset] = wd_st[wset].astype(BF16)
            nxt = next_expert_ref[e]

            @pl.when(nxt < N_EXPERTS)
            def _():
                start(weight_copies(nxt, 1 - wset))

        ahead = b + EXP_XDEPTH - 2

        @pl.when(ahead < n_used)
        def _():
            start(row_copies(ahead, ahead % EXP_XDEPTH, True))

        wait(row_copies(b, b % EXP_XDEPTH, True))

        @pl.when(b >= EXP_YDEPTH)
        def _():
            wait(row_copies(b - EXP_YDEPTH, b % EXP_YDEPTH, False))
        return wset

    def compute(b, wset):
        x = xbuf[b % EXP_XDEPTH]
        hg = _dot(x, wg_bf[wset])
        hb = hg * jax.nn.sigmoid(hg) * _dot(x, wu_bf[wset])
        ybuf[b % EXP_YDEPTH] = _dot(hb.astype(BF16), wd_bf[wset]).astype(BF16)

    def body(p, wset):
        b0 = 2 * p
        w0 = enter_block(b0, wset)
        w1 = enter_block(b0 + 1, w0)
        compute(b0, w0)
        compute(b0 + 1, w1)
        start(row_copies(b0, b0 % EXP_YDEPTH, False))
        start(row_copies(b0 + 1, (b0 + 1) % EXP_YDEPTH, False))
        return w1

    lax.fori_loop(0, n_used // 2, body, jnp.int32(1))

    for back in range(EXP_YDEPTH, 0, -1):
        @pl.when(n_used >= back)
        def _(back=back):
            wait(row_copies(n_used - back, (n_used - back) % EXP_YDEPTH, False))


def _experts(plan, xs, w_gate, w_up, w_down):
    any_spec = pl.BlockSpec(memory_space=pl.ANY)
    return pl.pallas_call(
        _expert_kernel,
        grid_spec=pltpu.PrefetchScalarGridSpec(
            num_scalar_prefetch=3,
            grid=(1,),
            in_specs=[any_spec, any_spec, any_spec, any_spec],
            out_specs=any_spec,
            scratch_shapes=[
                pltpu.VMEM((EXP_XDEPTH, EXP_BM, D_MODEL), BF16),
                pltpu.VMEM((EXP_YDEPTH, EXP_BM, D_MODEL), BF16),
                pltpu.VMEM((2, D_MODEL, D_EXPERT), F32),
                pltpu.VMEM((2, D_MODEL, D_EXPERT), F32),
                pltpu.VMEM((2, D_EXPERT, D_MODEL), F32),
                pltpu.VMEM((2, D_MODEL, D_EXPERT), BF16),
                pltpu.VMEM((2, D_MODEL, D_EXPERT), BF16),
                pltpu.VMEM((2, D_EXPERT, D_MODEL), BF16),
                pltpu.SemaphoreType.DMA((EXP_XDEPTH,)),
                pltpu.SemaphoreType.DMA((EXP_YDEPTH,)),
                pltpu.SemaphoreType.DMA((2,)),
            ]),
        out_shape=jax.ShapeDtypeStruct(xs.shape, BF16),
        compiler_params=pltpu.CompilerParams(
            dimension_semantics=("arbitrary",), vmem_limit_bytes=VMEM_LIMIT),
        name="experts",
    )(plan['blk_expert'], plan['next_expert'], plan['n_used'], xs, w_gate, w_up, w_down)


def _combine_kernel(local_ref, global_ref, x1_ref, h2_ref, p_ref, rank_ref, gate_ref, lo_ref, hi_ref,
                    wsg_ref, wsu_ref, wsd_ref, gple_ref, wpg_ref, wp_ref, gfin_ref, ys_hbm, o_ref, ybuf, st_ref, sem,
                    *, n_win, final_norm):
    w = pl.program_id(0)
    slot = w % 2

    @pl.when(w == 0)
    def _():
        ybuf[...] = jnp.zeros(ybuf.shape, BF16)
        _start_runs(local_ref, global_ref, w, ybuf, slot, ys_hbm, sem, False)

    @pl.when(w + 1 < n_win)
    def _():
        _start_runs(local_ref, global_ref, w + 1, ybuf, 1 - slot, ys_hbm, sem, False)

    lo = lo_ref[0]
    hi = hi_ref[0]
    lo_f = lo.astype(F32)
    rank_tbl = rank_ref[...]
    gate_tbl = gate_ref[...]

    def build_group(lg):
        cols = slice(lg * CMB_LG, (lg + 1) * CMB_LG)
        rid = lg * CMB_LG + lax.broadcasted_iota(jnp.int32, (N_EXPERTS, CMB_LG), 1)
        owner = jnp.where(rid >= lo, jnp.where(rid < hi, 1.0, 0.0), 0.0)
        run_row = rid[0:1, :].astype(F32) - jnp.sum(owner * lo_f, axis=0, keepdims=True)
        owner = owner.astype(BF16)
        hit = _dot_t(rank_tbl, owner) == run_row
        st_ref[:, cols] = jnp.where(hit, _dot_t(gate_tbl, owner), 0.0).astype(BF16)

    build_group(0)
    h2 = h2_ref[...]
    hs = _dot(h2, wsg_ref[...])
    hs = hs * jax.nn.sigmoid(hs) * _dot(h2, wsu_ref[...])
    shared = _dot(hs.astype(BF16), wsd_ref[...])

    _wait_runs(local_ref, w, ybuf, slot, ys_hbm, sem, False)
    routed = None
    n_groups = SEL_ROWS // CMB_LG
    for lg in range(n_groups):
        if lg + 1 < n_groups:
            build_group(lg + 1)
        src = pl.multiple_of(slot * SEL_ROWS + lg * CMB_LG, CMB_LG)
        part = _dot(st_ref[:, lg * CMB_LG:(lg + 1) * CMB_LG], ybuf[pl.ds(src, CMB_LG), :])
        routed = part if routed is None else routed + part
    x2 = x1_ref[...] + routed + shared

    hp = _rms(x2, gple_ref[...]).astype(BF16)
    gate = jax.nn.sigmoid(_dot(hp, wpg_ref[...]))
    x3 = x2 + gate * _dot(p_ref[...].astype(BF16), wp_ref[...])
    o_ref[...] = _rms(x3, gfin_ref[...]) if final_norm else x3


def _combine(plan, ys, x1, h2, p, rank_tbl, gate_tbl, wsg, wsu, wsd, g_ple, w_pg, w_p, g_fin, final_norm):
    t = x1.shape[0]
    n_win = t // WIN
    row = lambda width: pl.BlockSpec((WIN, width), lambda w, *_: (w, 0))
    const = lambda shape: pl.BlockSpec(shape, lambda w, *_: (0,) * len(shape))
    return pl.pallas_call(
        functools.partial(_combine_kernel, n_win=n_win, final_norm=final_norm),
        grid_spec=pltpu.PrefetchScalarGridSpec(
            num_scalar_prefetch=2,
            grid=(n_win,),
            in_specs=[
                row(D_MODEL), row(D_MODEL), row(PLE_DIM),
                pl.BlockSpec((N_EXPERTS, WIN), lambda w, *_: (0, w)),
                pl.BlockSpec((N_EXPERTS, WIN), lambda w, *_: (0, w)),
                pl.BlockSpec((1, N_EXPERTS, 1), lambda w, *_: (w, 0, 0)),
                pl.BlockSpec((1, N_EXPERTS, 1), lambda w, *_: (w, 0, 0)),
                const((D_MODEL, D_EXPERT)), const((D_MODEL, D_EXPERT)), const((D_EXPERT, D_MODEL)),
                const((1, D_MODEL)), const((D_MODEL, D_MODEL)), const((PLE_DIM, D_MODEL)),
                const((1, D_MODEL)),
                pl.BlockSpec(memory_space=pl.ANY),
            ],
            out_specs=[row(D_MODEL), _staging_spec(0)],
            scratch_shapes=[
                pltpu.VMEM((WIN, SEL_ROWS), BF16),
                pltpu.SemaphoreType.DMA((2,)),
            ]),
        out_shape=[jax.ShapeDtypeStruct((t, D_MODEL), F32), _staging_shape(0)],
        compiler_params=pltpu.CompilerParams(
            dimension_semantics=("arbitrary",), vmem_limit_bytes=VMEM_LIMIT),
        name="combine",
    )(plan['local_off'], plan['global_off'], x1, h2, p, rank_tbl, gate_tbl, plan['run_lo'], plan['run_hi'],
      wsg, wsu, wsd, g_ple, w_pg, w_p, g_fin, ys)[0]


def kernel(x, p, g_mix, w_in, b_in, w_dw, b_dw, g_cln, b_cln, w_conv_out, b_conv_out, w_pool, s_pool,
           w_out, g_ffn, w_router, b_router, w_e_gate, w_e_up, w_e_down, w_s_gate, w_s_up, w_s_down,
           g_ple, w_ple_gate, w_ple, g_final):
    bsz, s, d = x.shape
    t = bsz * s
    depth = w_in.shape[0]
    xt = x.reshape(t, d)
    row = lambda v: v.reshape(1, -1)
    for i in range(depth):
        x1, h2 = _mixer(
            xt, s, row(g_mix[i]), w_in[i].astype(BF16), row(b_in[i]), w_dw[i], row(b_dw[i]),
            row(g_cln[i]), row(b_cln[i]), w_conv_out[i].astype(BF16), row(b_conv_out[i]),
            w_pool[i].astype(BF16), row(s_pool[i]), w_out[i].astype(BF16), row(g_ffn[i]))
        gate, rank, pos, cnt = _router(h2, w_router[i].T.astype(BF16), b_router[i].reshape(N_EXPERTS, 1))
        plan = _dispatch_plan(cnt, t)
        xs = _dispatch(plan, h2, pos)
        ys = _experts(plan, xs, w_e_gate[i], w_e_up[i], w_e_down[i])
        xt = _combine(
            plan, ys, x1, h2, p[i].reshape(t, PLE_DIM), rank, gate,
            w_s_gate[i].astype(BF16), w_s_up[i].astype(BF16), w_s_down[i].astype(BF16),
            row(g_ple[i]), w_ple_gate[i].astype(BF16), w_ple[i].astype(BF16), row(g_final),
            final_norm=(i == depth - 1))
    return xt.reshape(bsz, s, d)
```

```python
import functools

import jax
import jax.numpy as jnp
from jax import lax
from jax.experimental import pallas as pl
from jax.experimental.pallas import tpu as pltpu

D_MODEL = 1024
D_CONV = 1024
D_POOL = 1024
CONV_WIDTH = 31
POOL_WINDOWS = (2, 4, 8, 16)
POOL_GROUP = 256
PLE_DIM = 256
N_EXPERTS = 64
N_GROUPS = 8
GROUP_SIZE = N_EXPERTS // N_GROUPS
TOPK_GROUPS = 4
TOP_K = 8
D_EXPERT = 256
ROUTED_SCALE = 2.5
NORM_EPS = 1e-6

F32 = jnp.float32
BF16 = jnp.bfloat16

MIX_TM = 256
MIX_NV = MIX_TM // 8
CONV_MG = 8
ROW_CHUNK = 64
LANE = 128

ROUTER_TM = 1024
WIN = 256
SEL_ROWS = 2560
SEL_RG = 64
SEL_MM = 512
EXP_BM = 576
EXP_GROUP = 4
EXP_XDEPTH = 8
EXP_YDEPTH = 8
EXP_SPLIT = 4
CMB_LG = 512

VMEM_LIMIT = 56 * 1024 * 1024


def _rms(x, g):
    ms = jnp.mean(x * x, axis=-1, keepdims=True)
    return x * lax.rsqrt(ms + NORM_EPS) * g


def _dot(a, b):
    return jnp.dot(a, b, preferred_element_type=F32)


def _dot_t(a, b):
    return lax.dot_general(a, b, (((0,), (0,)), ((), ())), preferred_element_type=F32)


def _mixer_kernel(x_ref, gmix_ref, win_ref, bin_ref, wdw_ref, bdw_ref, gcln_ref, bcln_ref,
                  wco_ref, bco_ref, wpool_ref, spool_ref, wout_ref, gffn_ref, perm_ref, unperm_ref,
                  x1_ref, h2_ref, a_ext, a_prev, u_ext, u_prev, c_buf, q_buf, *, tiles_per_seq):
    i = pl.program_id(0) % tiles_per_seq
    tm = MIX_TM
    nv = MIX_NV

    @pl.when(i == 0)
    def _():
        a_prev[...] = jnp.zeros(a_prev.shape, F32)
        u_prev[...] = jnp.zeros(u_prev.shape, F32)

    x = x_ref[...]
    h = _dot(perm_ref[...], _rms(x, gmix_ref[...]).astype(BF16)).astype(BF16)

    def proj(lo, hi):
        return _dot(h, win_ref[:, lo:hi]) + bin_ref[:, lo:hi]

    glu = proj(0, D_CONV) * jax.nn.sigmoid(proj(D_CONV, 2 * D_CONV))
    for lc in range(D_CONV // LANE):
        a_ext[lc, tm:2 * tm, :] = glu[:, lc * LANE:(lc + 1) * LANE]
    u_ext[tm:2 * tm, :] = proj(2 * D_CONV, 2 * D_CONV + D_POOL)

    def delayed_groups(ext, prev, first_group):
        last_row = lax.broadcasted_iota(jnp.int32, (8, ext.shape[-1]), 0) == 7
        for g in range(first_group, nv):
            rows = slice(8 * g, 8 * g + 8)
            mixed = jnp.where(last_row, prev[rows, :], ext[tm + 8 * g:tm + 8 * g + 8, :])
            ext[rows, :] = pltpu.roll(mixed, 1, axis=0)
            prev[rows, :] = ext[tm + 8 * g:tm + 8 * g + 8, :]

    delayed_groups(u_ext, u_prev, nv - (max(POOL_WINDOWS) - 1))

    def conv_column(lc, carry):
        a_col = a_ext.at[lc]
        delayed_groups(a_col, a_prev.at[lc], nv - (CONV_WIDTH - 1))
        w_col = wdw_ref.at[lc]
        for g0 in range(0, nv, CONV_MG):
            acc = None
            for k in range(CONV_WIDTH):
                src = nv + g0 + k - (CONV_WIDTH - 1)
                term = a_col[8 * src:8 * (src + CONV_MG), :] * w_col[k:k + 1, :]
                acc = term if acc is None else acc + term
            c_buf[lc, 8 * g0:8 * (g0 + CONV_MG), :] = acc + bdw_ref[lc]
        return carry
    lax.fori_loop(0, D_CONV // LANE, conv_column, 0)

    c = jnp.concatenate([c_buf[lc] for lc in range(D_CONV // LANE)], axis=-1)
    mu = jnp.mean(c, axis=-1, keepdims=True)
    xc = c - mu
    var = jnp.mean(xc * xc, axis=-1, keepdims=True)
    y = xc * lax.rsqrt(var + NORM_EPS) * gcln_ref[...] + bcln_ref[...]
    y = y * jax.nn.sigmoid(y)
    branch_a = _dot(y.astype(BF16), wco_ref[...]) + bco_ref[...]

    for r0 in range(0, tm, ROW_CHUNK):
        row = r0 + lax.broadcasted_iota(jnp.int32, (ROW_CHUNK, POOL_GROUP), 0)
        t1 = i * tm + (row % 8) * nv + row // 8 + 1
        for gi, w in enumerate(POOL_WINDOWS):
            ls = slice(gi * POOL_GROUP, (gi + 1) * POOL_GROUP)
            tok = u_ext[tm + r0:tm + r0 + ROW_CHUNK, ls]
            s = tok
            for j in range(1, w):
                s = s + u_ext[tm + r0 - 8 * j:tm + r0 - 8 * j + ROW_CHUNK, ls]
            cnt = jnp.minimum(t1, w).astype(F32)
            q_buf[r0:r0 + ROW_CHUNK, ls] = s / cnt - tok

    qs_out = []
    for gi in range(len(POOL_WINDOWS)):
        ls = slice(gi * POOL_GROUP, (gi + 1) * POOL_GROUP)
        qs_out.append(_dot(q_buf[:, ls].astype(BF16), wpool_ref[gi]) * spool_ref[:, ls])
    branch_b = jnp.concatenate(qs_out, axis=-1)

    c2 = 2 * D_CONV + D_POOL
    gate_a = jax.nn.sigmoid(proj(c2, c2 + D_MODEL))
    gate_b = jax.nn.sigmoid(proj(c2 + D_MODEL, c2 + 2 * D_MODEL))
    merged = gate_a * branch_a + gate_b * branch_b
    merged = _dot(unperm_ref[...], merged.astype(BF16)).astype(BF16)
    x1 = x + _dot(merged, wout_ref[...])
    x1_ref[...] = x1
    h2_ref[...] = _rms(x1, gffn_ref[...]).astype(BF16)


def _const_spec(shape):
    n = len(shape)
    return pl.BlockSpec(shape, lambda i, _n=n: (0,) * _n)


def _mixer(x, seq_len, g_mix, w_in, b_in, w_dw, b_dw, g_cln, b_cln, w_co, b_co, w_pool, s_pool, w_out,
           g_ffn):
    t = x.shape[0]
    tm = MIX_TM
    assert seq_len % tm == 0 and MIX_NV >= CONV_WIDTH and MIX_NV >= max(POOL_WINDOWS)
    d_in = w_in.shape[1]
    row = pl.BlockSpec((tm, D_MODEL), lambda i: (i, 0))
    n_col = D_CONV // LANE
    w_dw = w_dw.reshape(CONV_WIDTH, n_col, LANE).transpose(1, 0, 2)
    b_dw = b_dw.reshape(n_col, 1, LANE)
    r = jnp.arange(tm)
    perm = ((r % 8) * MIX_NV + r // 8)[:, None] == jnp.arange(tm)[None, :]
    perm = perm.astype(BF16)
    return pl.pallas_call(
        functools.partial(_mixer_kernel, tiles_per_seq=seq_len // tm),
        grid=(t // tm,),
        in_specs=[
            row,
            _const_spec((1, D_MODEL)),
            _const_spec((D_MODEL, d_in)),
            _const_spec((1, d_in)),
            _const_spec((n_col, CONV_WIDTH, LANE)),
            _const_spec((n_col, 1, LANE)),
            _const_spec((1, D_CONV)),
            _const_spec((1, D_CONV)),
            _const_spec((D_CONV, D_MODEL)),
            _const_spec((1, D_MODEL)),
            _const_spec((len(POOL_WINDOWS), POOL_GROUP, POOL_GROUP)),
            _const_spec((1, D_POOL)),
            _const_spec((D_MODEL, D_MODEL)),
            _const_spec((1, D_MODEL)),
            _const_spec((tm, tm)),
            _const_spec((tm, tm)),
        ],
        out_specs=[row, row],
        out_shape=[jax.ShapeDtypeStruct((t, D_MODEL), F32),
                   jax.ShapeDtypeStruct((t, D_MODEL), BF16)],
        scratch_shapes=[
            pltpu.VMEM((n_col, 2 * tm, LANE), F32),
            pltpu.VMEM((n_col, tm, LANE), F32),
            pltpu.VMEM((2 * tm, D_POOL), F32),
            pltpu.VMEM((tm, D_POOL), F32),
            pltpu.VMEM((n_col, tm, LANE), F32),
            pltpu.VMEM((tm, D_POOL), F32),
        ],
        compiler_params=pltpu.CompilerParams(
            dimension_semantics=("arbitrary",), vmem_limit_bytes=VMEM_LIMIT),
        name="mixer",
    )(x, g_mix, w_in, b_in, w_dw, b_dw, g_cln, b_cln, w_co, b_co, w_pool, s_pool, w_out, g_ffn, perm, perm.T)


def _beats(v, other, other_is_later):
    v = jnp.broadcast_to(v, other.shape)
    return jnp.where(other_is_later, jnp.where(v >= other, 1, 0), jnp.where(v > other, 1, 0))


def _router_kernel(h2_ref, wrt_ref, br_ref, utri_ref, ltri_ref, gate_ref, rank_ref, pos_ref, cnt_ref):
    tm = ROUTER_TM
    logits = lax.dot_general(wrt_ref[...], h2_ref[...], (((1,), (1,)), ((), ())),
                             preferred_element_type=F32)
    scores = jax.nn.sigmoid(logits)
    sel = scores + br_ref[...]
    shape3 = (N_GROUPS, GROUP_SIZE, tm)
    sel3 = sel.reshape(shape3)
    scores3 = scores.reshape(shape3)
    neg_inf = jnp.float32(-jnp.inf)

    member = lax.broadcasted_iota(jnp.int32, shape3, 1)
    m1 = jnp.max(sel3, axis=1, keepdims=True)
    first = jnp.min(jnp.where(sel3 == m1, member, GROUP_SIZE), axis=1, keepdims=True)
    m2 = jnp.max(jnp.where(member == first, neg_inf, sel3), axis=1, keepdims=True)
    gscore = jnp.broadcast_to(m1 + m2, shape3)

    gidx = lax.broadcasted_iota(jnp.int32, shape3, 0)
    grank = jnp.zeros(shape3, jnp.int32)
    for j in range(N_GROUPS):
        sj = gscore[j:j + 1]
        grank = grank + _beats(sj, gscore, gidx > j)
    masked = jnp.where(grank < TOPK_GROUPS, sel3, neg_inf)

    eidx = gidx * GROUP_SIZE + member
    work = masked
    erank = jnp.full(shape3, TOP_K, jnp.int32)
    for k in range(TOP_K):
        best = jnp.max(jnp.max(work, axis=0, keepdims=True), axis=1, keepdims=True)
        cand = jnp.where(work == best, eidx, N_EXPERTS)
        pick = jnp.min(jnp.min(cand, axis=0, keepdims=True), axis=1, keepdims=True)
        hit = eidx == pick
        work = jnp.where(hit, neg_inf, work)
        erank = jnp.where(hit, k, erank)
    chosen = erank < TOP_K
    top_s = jnp.where(chosen, scores3, 0.0)
    denom = jnp.sum(jnp.sum(top_s, axis=0, keepdims=True), axis=1, keepdims=True)
    gates3 = top_s / denom * ROUTED_SCALE
    chosen2 = jnp.where(chosen, 1.0, 0.0).reshape(N_EXPERTS, tm)
    gate_ref[...] = gates3.reshape(N_EXPERTS, tm).astype(BF16)

    for w in range(tm // WIN):
        ls = slice(w * WIN, (w + 1) * WIN)
        mw = chosen2[:, ls]
        rank = _dot(mw.astype(BF16), utri_ref[...])
        n = jnp.sum(mw, axis=1, keepdims=True)
        run = jnp.floor((n + 7.0) * 0.125) * 8.0
        start = _dot(ltri_ref[...], jnp.broadcast_to(run, (N_EXPERTS, WIN)).astype(BF16))
        rank_ref[:, ls] = jnp.where(mw > 0.5, rank, -1.0).astype(BF16)
        row3 = (rank + start).reshape(N_GROUPS, GROUP_SIZE, WIN)
        er = erank[:, :, ls]
        for k in range(TOP_K):
            pk = jnp.sum(jnp.sum(jnp.where(er == k, row3, 0.0), axis=0, keepdims=True), axis=1, keepdims=True)
            pos_ref[k:k + 1, ls] = pk.reshape(1, WIN).astype(jnp.int32)
        cnt_ref[w] = n


def _router(h2, w_rt, b_r):
    t = h2.shape[0]
    tm = ROUTER_TM
    utri = jnp.triu(jnp.ones((WIN, WIN), BF16), k=1)
    ltri = jnp.tril(jnp.ones((N_EXPERTS, N_EXPERTS), BF16), k=-1)
    return pl.pallas_call(
        _router_kernel,
        grid=(t // tm,),
        in_specs=[
            pl.BlockSpec((tm, D_MODEL), lambda i: (i, 0)),
            _const_spec((N_EXPERTS, D_MODEL)),
            _const_spec((N_EXPERTS, 1)),
            _const_spec((WIN, WIN)),
            _const_spec((N_EXPERTS, N_EXPERTS)),
        ],
        out_specs=[
            pl.BlockSpec((N_EXPERTS, tm), lambda i: (0, i)),
            pl.BlockSpec((N_EXPERTS, tm), lambda i: (0, i)),
            pl.BlockSpec((TOP_K, tm), lambda i: (0, i)),
            pl.BlockSpec((tm // WIN, N_EXPERTS, 1), lambda i: (i, 0, 0)),
        ],
        out_shape=[
            jax.ShapeDtypeStruct((N_EXPERTS, t), BF16),
            jax.ShapeDtypeStruct((N_EXPERTS, t), BF16),
            jax.ShapeDtypeStruct((TOP_K, t), jnp.int32),
            jax.ShapeDtypeStruct((t // WIN, N_EXPERTS, 1), F32),
        ],
        compiler_params=pltpu.CompilerParams(
            dimension_semantics=("arbitrary",), vmem_limit_bytes=VMEM_LIMIT),
        name="router",
    )(h2, w_rt, b_r, utri, ltri)


def _sorted_rows_bound(t):
    rows = t * TOP_K + (t // WIN) * N_EXPERTS * 7 + N_EXPERTS * (EXP_BM - 1)
    group_rows = EXP_GROUP * EXP_BM
    return -(-rows // group_rows) * group_rows


def _dispatch_plan(cnt, t):
    nw = t // WIN
    n = cnt.reshape(nw, N_EXPERTS).astype(jnp.int32)
    run = (n + 7) // 8 * 8
    local_end = jnp.cumsum(run, axis=1)
    local_off = jnp.concatenate([jnp.zeros((nw, 1), jnp.int32), local_end], axis=1)
    total = jnp.sum(run, axis=0)
    region = (total + EXP_BM - 1) // EXP_BM * EXP_BM
    eid = jnp.arange(N_EXPERTS, dtype=jnp.int32)
    last_owner = jnp.max(jnp.where(region > 0, eid, 0))
    short = (-(jnp.sum(region) // EXP_BM)) % EXP_GROUP
    region = region + jnp.where(eid == last_owner, short * EXP_BM, 0)
    region_end = jnp.cumsum(region)
    base = region_end - region
    global_off = base[None, :] + jnp.cumsum(run, axis=0) - run
    n_blocks = _sorted_rows_bound(t) // EXP_BM
    n_used = region_end[-1] // EXP_BM
    blk = jnp.arange(n_blocks, dtype=jnp.int32)
    blk_expert = jnp.sum((region_end[None, :] <= blk[:, None] * EXP_BM).astype(jnp.int32), axis=1)
    blk_expert = jnp.minimum(blk_expert, N_EXPERTS - 1)
    later_nonempty = (eid[None, :] > eid[:, None]) & (region[None, :] > 0)
    next_expert = jnp.min(jnp.where(later_nonempty, eid[None, :], N_EXPERTS), axis=1).astype(jnp.int32)
    return dict(
        run_lo=local_off[:, :N_EXPERTS].reshape(nw, N_EXPERTS, 1),
        run_hi=local_off[:, 1:].reshape(nw, N_EXPERTS, 1),
        local_off=local_off.reshape(-1), global_off=global_off.reshape(-1),
        fill_off=base + total, fill_cnt=region - total,
        blk_expert=blk_expert.astype(jnp.int32), next_expert=next_expert,
        n_used=n_used.reshape(1).astype(jnp.int32))


def _run_copy(local_ref, global_ref, win, e, vmem_buf, slot, hbm_buf, sem, to_hbm):
    lo = pl.multiple_of(local_ref[win * (N_EXPERTS + 1) + e], 8)
    cnt = pl.multiple_of(local_ref[win * (N_EXPERTS + 1) + e + 1] - lo, 8)
    go = pl.multiple_of(global_ref[win * N_EXPERTS + e], 8)
    v = vmem_buf.at[pl.ds(pl.multiple_of(slot * SEL_ROWS + lo, 8), cnt)]
    h = hbm_buf.at[pl.ds(go, cnt)]
    cp = pltpu.make_async_copy(v, h, sem.at[slot]) if to_hbm else pltpu.make_async_copy(h, v, sem.at[slot])
    return cnt, cp


def _start_runs(local_ref, global_ref, win, vmem_buf, slot, hbm_buf, sem, to_hbm):
    def body(e, carry):
        cnt, cp = _run_copy(local_ref, global_ref, win, e, vmem_buf, slot, hbm_buf, sem, to_hbm)

        @pl.when(cnt > 0)
        def _():
            cp.start()
        return carry
    lax.fori_loop(0, N_EXPERTS, body, 0)


def _wait_runs(local_ref, win, vmem_buf, slot, hbm_buf, sem, to_hbm):
    total = pl.multiple_of(local_ref[win * (N_EXPERTS + 1) + N_EXPERTS], 8)
    v = vmem_buf.at[pl.ds(pl.multiple_of(slot * SEL_ROWS, 8), total)]
    h = hbm_buf.at[pl.ds(0, total)]
    cp = pltpu.make_async_copy(v, h, sem.at[slot]) if to_hbm else pltpu.make_async_copy(h, v, sem.at[slot])

    @pl.when(total > 0)
    def _():
        cp.wait()


def _dispatch_kernel(local_ref, global_ref, fill_off_ref, fill_cnt_ref, h2_ref, pos_ref, xs_hbm, sbuf,
                     s_ref, sem, zsem, *, n_win):
    w = pl.program_id(0)
    slot = w % 2
    pos = pos_ref[...]

    h2 = h2_ref[...]
    assert SEL_RG <= 256
    rid_b = lax.broadcasted_iota(jnp.int32, (SEL_RG, WIN), 0).astype(F32).astype(BF16)
    one_b = jnp.ones((SEL_RG, WIN), BF16)
    for g in range(SEL_ROWS // SEL_MM):
        for sg in range(SEL_MM // SEL_RG):
            r0 = g * SEL_MM + sg * SEL_RG
            acc = jnp.zeros((SEL_RG, WIN), BF16)
            for k in range(TOP_K):
                off = (pos[k:k + 1, :] - r0).astype(F32)
                off = jnp.broadcast_to(off, (SEL_RG, WIN)).astype(BF16)
                acc = jnp.where(rid_b == off, one_b, acc)
            s_ref[r0:r0 + SEL_RG, :] = acc
        rows = slice(g * SEL_MM, (g + 1) * SEL_MM)
        dst = pl.multiple_of(slot * SEL_ROWS + g * SEL_MM, SEL_MM)
        sbuf[pl.ds(dst, SEL_MM), :] = _dot(s_ref[rows, :], h2).astype(BF16)

    _start_runs(local_ref, global_ref, w, sbuf, slot, xs_hbm, sem, True)

    @pl.when(w > 0)
    def _():
        _wait_runs(local_ref, w - 1, sbuf, 1 - slot, xs_hbm, sem, True)

    @pl.when(w == n_win - 1)
    def _():
        sbuf[2 * SEL_ROWS:, :] = jnp.zeros((EXP_GROUP * EXP_BM, D_MODEL), BF16)

        def fill(e, wait):
            cnt = pl.multiple_of(fill_cnt_ref[e], 8)
            off = pl.multiple_of(fill_off_ref[e], 8)
            cp = pltpu.make_async_copy(sbuf.at[pl.ds(2 * SEL_ROWS, cnt)], xs_hbm.at[pl.ds(off, cnt)], zsem)

            @pl.when(cnt > 0)
            def _():
                if wait:
                    cp.wait()
                else:
                    cp.start()

        def start_body(e, carry):
            fill(e, False)
            return carry

        def wait_body(e, carry):
            fill(e, True)
            return carry
        lax.fori_loop(0, N_EXPERTS, start_body, 0)
        _wait_runs(local_ref, w, sbuf, slot, xs_hbm, sem, True)
        lax.fori_loop(0, N_EXPERTS, wait_body, 0)


def _staging_shape(extra_rows):
    return jax.ShapeDtypeStruct((2 * SEL_ROWS + extra_rows, D_MODEL), BF16)


def _staging_spec(extra_rows):
    return pl.BlockSpec((2 * SEL_ROWS + extra_rows, D_MODEL), lambda w, *_: (0, 0))


def _dispatch(plan, h2, pos):
    t = h2.shape[0]
    n_win = t // WIN
    return pl.pallas_call(
        functools.partial(_dispatch_kernel, n_win=n_win),
        grid_spec=pltpu.PrefetchScalarGridSpec(
            num_scalar_prefetch=4,
            grid=(n_win,),
            in_specs=[
                pl.BlockSpec((WIN, D_MODEL), lambda w, *_: (w, 0)),
                pl.BlockSpec((TOP_K, WIN), lambda w, *_: (0, w)),
            ],
            out_specs=[pl.BlockSpec(memory_space=pl.ANY), _staging_spec(EXP_GROUP * EXP_BM)],
            scratch_shapes=[
                pltpu.VMEM((SEL_ROWS, WIN), BF16),
                pltpu.SemaphoreType.DMA((2,)),
                pltpu.SemaphoreType.DMA,
            ]),
        out_shape=[jax.ShapeDtypeStruct((_sorted_rows_bound(t), D_MODEL), BF16),
                   _staging_shape(EXP_GROUP * EXP_BM)],
        compiler_params=pltpu.CompilerParams(
            dimension_semantics=("arbitrary",), vmem_limit_bytes=VMEM_LIMIT),
        name="dispatch",
    )(plan['local_off'], plan['global_off'], plan['fill_off'], plan['fill_cnt'], h2, pos)[0]


def _expert_kernel(blk_expert_ref, next_expert_ref, n_used_ref, xs_hbm, wg_hbm, wu_hbm, wd_hbm, ys_hbm,
                   xbuf, ybuf, wg_st, wu_st, wd_st, wg_bf, wu_bf, wd_bf, xsem, ysem, wsem):
    n_used = n_used_ref[0]
    part = EXP_BM // EXP_SPLIT

    def row_copies(b, slot, fetch):
        out = []
        for q in range(EXP_SPLIT):
            hbm_rows = pl.ds(pl.multiple_of(b * EXP_BM + q * part, part), part)
            if fetch:
                out.append(pltpu.make_async_copy(xs_hbm.at[hbm_rows], xbuf.at[slot, q * part:(q + 1) * part],
                                                 xsem.at[slot]))
            else:
                out.append(pltpu.make_async_copy(ybuf.at[slot, q * part:(q + 1) * part], ys_hbm.at[hbm_rows],
                                                 ysem.at[slot]))
        return out

    def weight_copies(e, slot):
        return [pltpu.make_async_copy(wg_hbm.at[e], wg_st.at[slot], wsem.at[slot]),
                pltpu.make_async_copy(wu_hbm.at[e], wu_st.at[slot], wsem.at[slot]),
                pltpu.make_async_copy(wd_hbm.at[e], wd_st.at[slot], wsem.at[slot])]

    def start(copies):
        for c in copies:
            c.start()

    def wait(copies):
        for c in copies:
            c.wait()

    for ahead in range(EXP_XDEPTH - EXP_GROUP):
        @pl.when(ahead < n_used)
        def _(ahead=ahead):
            start(row_copies(ahead, ahead, True))

    @pl.when(n_used > 0)
    def _():
        start(weight_copies(blk_expert_ref[0], 0))

    def enter_block(b, changes):
        e = blk_expert_ref[b]
        new_expert = jnp.logical_or(b == 0, e != blk_expert_ref[jnp.maximum(b - 1, 0)])
        changes = changes + new_expert.astype(jnp.int32)

        @pl.when(new_expert)
        def _():
            stage = changes % 2
            wset = changes % EXP_GROUP
            wait(weight_copies(e, stage))
            wg_bf[wset] = wg_st[stage].astype(BF16)
            wu_bf[wset] = wu_st[stage].astype(BF16)
            wd_bf[wset] = wd_st[stage].astype(BF16)
            nxt = next_expert_ref[e]

            @pl.when(nxt < N_EXPERTS)
            def _():
                start(weight_copies(nxt, 1 - stage))

        ahead = b + EXP_XDEPTH - EXP_GROUP

        @pl.when(ahead < n_used)
        def _():
            start(row_copies(ahead, ahead % EXP_XDEPTH, True))

        wait(row_copies(b, b % EXP_XDEPTH, True))

        @pl.when(b >= EXP_YDEPTH)
        def _():
            wait(row_copies(b - EXP_YDEPTH, b % EXP_YDEPTH, False))
        return changes

    def compute(b, changes):
        wset = changes % EXP_GROUP
        x = xbuf[b % EXP_XDEPTH]
        hg = _dot(x, wg_bf[wset])
        hb = hg * jax.nn.sigmoid(hg) * _dot(x, wu_bf[wset])
        ybuf[b % EXP_YDEPTH] = _dot(hb.astype(BF16), wd_bf[wset]).astype(BF16)

    def body(g, changes):
        blocks = [g * EXP_GROUP + j for j in range(EXP_GROUP)]
        seen = []
        for b in blocks:
            changes = enter_block(b, changes)
            seen.append(changes)
        for b, c in zip(blocks, seen):
            compute(b, c)
        for b in blocks:
            start(row_copies(b, b % EXP_YDEPTH, False))
        return changes

    lax.fori_loop(0, n_used // EXP_GROUP, body, jnp.int32(-1))

    for back in range(EXP_YDEPTH, 0, -1):
        @pl.when(n_used >= back)
        def _(back=back):
            wait(row_copies(n_used - back, (n_used - back) % EXP_YDEPTH, False))


def _experts(plan, xs, w_gate, w_up, w_down):
    any_spec = pl.BlockSpec(memory_space=pl.ANY)
    return pl.pallas_call(
        _expert_kernel,
        grid_spec=pltpu.PrefetchScalarGridSpec(
            num_scalar_prefetch=3,
            grid=(1,),
            in_specs=[any_spec, any_spec, any_spec, any_spec],
            out_specs=any_spec,
            scratch_shapes=[
                pltpu.VMEM((EXP_XDEPTH, EXP_BM, D_MODEL), BF16),
                pltpu.VMEM((EXP_YDEPTH, EXP_BM, D_MODEL), BF16),
                pltpu.VMEM((2, D_MODEL, D_EXPERT), F32),
                pltpu.VMEM((2, D_MODEL, D_EXPERT), F32),
                pltpu.VMEM((2, D_EXPERT, D_MODEL), F32),
                pltpu.VMEM((EXP_GROUP, D_MODEL, D_EXPERT), BF16),
                pltpu.VMEM((EXP_GROUP, D_MODEL, D_EXPERT), BF16),
                pltpu.VMEM((EXP_GROUP, D_EXPERT, D_MODEL), BF16),
                pltpu.SemaphoreType.DMA((EXP_XDEPTH,)),
                pltpu.SemaphoreType.DMA((EXP_YDEPTH,)),
                pltpu.SemaphoreType.DMA((2,)),
            ]),
        out_shape=jax.ShapeDtypeStruct(xs.shape, BF16),
        compiler_params=pltpu.CompilerParams(
            dimension_semantics=("arbitrary",), vmem_limit_bytes=VMEM_LIMIT),
        name="experts",
    )(plan['blk_expert'], plan['next_expert'], plan['n_used'], xs, w_gate, w_up, w_down)


def _combine_kernel(local_ref, global_ref, x1_ref, h2_ref, p_ref, rank_ref, gate_ref, lo_ref, hi_ref,
                    wsg_ref, wsu_ref, wsd_ref, gple_ref, wpg_ref, wp_ref, gfin_ref, ys_hbm, o_ref, ybuf, st_ref, sem,
                    *, n_win, final_norm):
    w = pl.program_id(0)
    slot = w % 2

    @pl.when(w == 0)
    def _():
        ybuf[...] = jnp.zeros(ybuf.shape, BF16)
        _start_runs(local_ref, global_ref, w, ybuf, slot, ys_hbm, sem, False)

    @pl.when(w + 1 < n_win)
    def _():
        _start_runs(local_ref, global_ref, w + 1, ybuf, 1 - slot, ys_hbm, sem, False)

    lo = lo_ref[0]
    hi = hi_ref[0]
    lo_f = lo.astype(F32)
    rank_tbl = rank_ref[...]
    gate_tbl = gate_ref[...]

    def build_group(lg):
        cols = slice(lg * CMB_LG, (lg + 1) * CMB_LG)
        rid = lg * CMB_LG + lax.broadcasted_iota(jnp.int32, (N_EXPERTS, CMB_LG), 1)
        owner = jnp.where(rid >= lo, jnp.where(rid < hi, 1.0, 0.0), 0.0)
        run_row = rid[0:1, :].astype(F32) - jnp.sum(owner * lo_f, axis=0, keepdims=True)
        owner = owner.astype(BF16)
        hit = _dot_t(rank_tbl, owner) == run_row
        st_ref[:, cols] = jnp.where(hit, _dot_t(gate_tbl, owner), 0.0).astype(BF16)

    build_group(0)
    h2 = h2_ref[...]
    hs = _dot(h2, wsg_ref[...])
    hs = hs * jax.nn.sigmoid(hs) * _dot(h2, wsu_ref[...])
    shared = _dot(hs.astype(BF16), wsd_ref[...])

    _wait_runs(local_ref, w, ybuf, slot, ys_hbm, sem, False)
    routed = None
    n_groups = SEL_ROWS // CMB_LG
    for lg in range(n_groups):
        if lg + 1 < n_groups:
            build_group(lg + 1)
        src = pl.multiple_of(slot * SEL_ROWS + lg * CMB_LG, CMB_LG)
        part = _dot(st_ref[:, lg * CMB_LG:(lg + 1) * CMB_LG], ybuf[pl.ds(src, CMB_LG), :])
        routed = part if routed is None else routed + part
    x2 = x1_ref[...] + routed + shared

    hp = _rms(x2, gple_ref[...]).astype(BF16)
    gate = jax.nn.sigmoid(_dot(hp, wpg_ref[...]))
    x3 = x2 + gate * _dot(p_ref[...].astype(BF16), wp_ref[...])
    o_ref[...] = _rms(x3, gfin_ref[...]) if final_norm else x3


def _combine(plan, ys, x1, h2, p, rank_tbl, gate_tbl, wsg, wsu, wsd, g_ple, w_pg, w_p, g_fin, final_norm):
    t = x1.shape[0]
    n_win = t // WIN
    row = lambda width: pl.BlockSpec((WIN, width), lambda w, *_: (w, 0))
    const = lambda shape: pl.BlockSpec(shape, lambda w, *_: (0,) * len(shape))
    return pl.pallas_call(
        functools.partial(_combine_kernel, n_win=n_win, final_norm=final_norm),
        grid_spec=pltpu.PrefetchScalarGridSpec(
            num_scalar_prefetch=2,
            grid=(n_win,),
            in_specs=[
                row(D_MODEL), row(D_MODEL), row(PLE_DIM),
                pl.BlockSpec((N_EXPERTS, WIN), lambda w, *_: (0, w)),
                pl.BlockSpec((N_EXPERTS, WIN), lambda w, *_: (0, w)),
                pl.BlockSpec((1, N_EXPERTS, 1), lambda w, *_: (w, 0, 0)),
                pl.BlockSpec((1, N_EXPERTS, 1), lambda w, *_: (w, 0, 0)),
                const((D_MODEL, D_EXPERT)), const((D_MODEL, D_EXPERT)), const((D_EXPERT, D_MODEL)),
                const((1, D_MODEL)), const((D_MODEL, D_MODEL)), const((PLE_DIM, D_MODEL)),
                const((1, D_MODEL)),
                pl.BlockSpec(memory_space=pl.ANY),
            ],
            out_specs=[row(D_MODEL), _staging_spec(0)],
            scratch_shapes=[
                pltpu.VMEM((WIN, SEL_ROWS), BF16),
                pltpu.SemaphoreType.DMA((2,)),
            ]),
        out_shape=[jax.ShapeDtypeStruct((t, D_MODEL), F32), _staging_shape(0)],
        compiler_params=pltpu.CompilerParams(
            dimension_semantics=("arbitrary",), vmem_limit_bytes=VMEM_LIMIT),
        name="combine",
    )(plan['local_off'], plan['global_off'], x1, h2, p, rank_tbl, gate_tbl, plan['run_lo'], plan['run_hi'],
      wsg, wsu, wsd, g_ple, w_pg, w_p, g_fin, ys)[0]


def kernel(x, p, g_mix, w_in, b_in, w_dw, b_dw, g_cln, b_cln, w_conv_out, b_conv_out, w_pool, s_pool,
           w_out, g_ffn, w_router, b_router, w_e_gate, w_e_up, w_e_down, w_s_gate, w_s_up, w_s_down,
           g_ple, w_ple_gate, w_ple, g_final):
    bsz, s, d = x.shape
    t = bsz * s
    depth = w_in.shape[0]
    xt = x.reshape(t, d)
    row = lambda v: v.reshape(1, -1)
    for i in range(depth):
        x1, h2 = _mixer(
            xt, s, row(g_mix[i]), w_in[i].astype(BF16), row(b_in[i]), w_dw[i], row(b_dw[i]),
            row(g_cln[i]), row(b_cln[i]), w_conv_out[i].astype(BF16), row(b_conv_out[i]),
            w_pool[i].astype(BF16), row(s_pool[i]), w_out[i].astype(BF16), row(g_ffn[i]))
        gate, rank, pos, cnt = _router(h2, w_router[i].T.astype(BF16), b_router[i].reshape(N_EXPERTS, 1))
        plan = _dispatch_plan(cnt, t)
        xs = _dispatch(plan, h2, pos)
        ys = _experts(plan, xs, w_e_gate[i], w_e_up[i], w_e_down[i])
        xt = _combine(
            plan, ys, x1, h2, p[i].reshape(t, PLE_DIM), rank, gate,
            w_s_gate[i].astype(BF16), w_s_up[i].astype(BF16), w_s_down[i].astype(BF16),
            row(g_ple[i]), w_ple_gate[i].astype(BF16), w_ple[i].astype(BF16), row(g_final),
            final_norm=(i == depth - 1))
    return xt.reshape(bsz, s, d)
```

```python
import functools

import jax
import jax.numpy as jnp
from jax import lax
from jax.experimental import pallas as pl
from jax.experimental.pallas import tpu as pltpu

D_MODEL = 1024
D_CONV = 1024
D_POOL = 1024
CONV_WIDTH = 31
POOL_WINDOWS = (2, 4, 8, 16)
POOL_GROUP = 256
PLE_DIM = 256
N_EXPERTS = 64
N_GROUPS = 8
GROUP_SIZE = N_EXPERTS // N_GROUPS
TOPK_GROUPS = 4
TOP_K = 8
D_EXPERT = 256
ROUTED_SCALE = 2.5
NORM_EPS = 1e-6

F32 = jnp.float32
BF16 = jnp.bfloat16

MIX_TM = 256
MIX_NV = MIX_TM // 8
CONV_MG = 8
ROW_CHUNK = 64
LANE = 128

ROUTER_TM = 1024
WIN = 256
SEL_ROWS = 2560
SEL_RG = 64
SEL_MM = 512
EXP_BM = 576
EXP_GROUP = 2
EXP_XDEPTH = 6
EXP_YDEPTH = 4
EXP_SPLIT = 4
CMB_LG = 512

VMEM_LIMIT = 56 * 1024 * 1024


def _rms(x, g):
    ms = jnp.mean(x * x, axis=-1, keepdims=True)
    return x * lax.rsqrt(ms + NORM_EPS) * g


def _dot(a, b):
    return jnp.dot(a, b, preferred_element_type=F32)


def _dot_t(a, b):
    return lax.dot_general(a, b, (((0,), (0,)), ((), ())), preferred_element_type=F32)


def _mixer_kernel(x_ref, gmix_ref, win_ref, bin_ref, wdw_ref, bdw_ref, gcln_ref, bcln_ref,
                  wco_ref, bco_ref, wpool_ref, spool_ref, wout_ref, gffn_ref, perm_ref, unperm_ref,
                  x1_ref, h2_ref, a_ext, a_prev, u_ext, u_prev, c_buf, q_buf, *, tiles_per_seq):
    i = pl.program_id(0) % tiles_per_seq
    tm = MIX_TM
    nv = MIX_NV

    @pl.when(i == 0)
    def _():
        a_prev[...] = jnp.zeros(a_prev.shape, F32)
        u_prev[...] = jnp.zeros(u_prev.shape, F32)

    x = x_ref[...]
    h = _dot(perm_ref[...], _rms(x, gmix_ref[...]).astype(BF16)).astype(BF16)

    def proj(lo, hi):
        return _dot(h, win_ref[:, lo:hi]) + bin_ref[:, lo:hi]

    glu = proj(0, D_CONV) * jax.nn.sigmoid(proj(D_CONV, 2 * D_CONV))
    for lc in range(D_CONV // LANE):
        a_ext[lc, tm:2 * tm, :] = glu[:, lc * LANE:(lc + 1) * LANE]
    u_ext[tm:2 * tm, :] = proj(2 * D_CONV, 2 * D_CONV + D_POOL)

    def delayed_groups(ext, prev, first_group):
        last_row = lax.broadcasted_iota(jnp.int32, (8, ext.shape[-1]), 0) == 7
        for g in range(first_group, nv):
            rows = slice(8 * g, 8 * g + 8)
            mixed = jnp.where(last_row, prev[rows, :], ext[tm + 8 * g:tm + 8 * g + 8, :])
            ext[rows, :] = pltpu.roll(mixed, 1, axis=0)
            prev[rows, :] = ext[tm + 8 * g:tm + 8 * g + 8, :]

    delayed_groups(u_ext, u_prev, nv - (max(POOL_WINDOWS) - 1))

    def conv_column(lc, carry):
        a_col = a_ext.at[lc]
        delayed_groups(a_col, a_prev.at[lc], nv - (CONV_WIDTH - 1))
        w_col = wdw_ref.at[lc]
        for g0 in range(0, nv, CONV_MG):
            acc = None
            for k in range(CONV_WIDTH):
                src = nv + g0 + k - (CONV_WIDTH - 1)
                term = a_col[8 * src:8 * (src + CONV_MG), :] * w_col[k:k + 1, :]
                acc = term if acc is None else acc + term
            c_buf[lc, 8 * g0:8 * (g0 + CONV_MG), :] = acc + bdw_ref[lc]
        return carry
    lax.fori_loop(0, D_CONV // LANE, conv_column, 0)

    c = jnp.concatenate([c_buf[lc] for lc in range(D_CONV // LANE)], axis=-1)
    mu = jnp.mean(c, axis=-1, keepdims=True)
    xc = c - mu
    var = jnp.mean(xc * xc, axis=-1, keepdims=True)
    y = xc * lax.rsqrt(var + NORM_EPS) * gcln_ref[...] + bcln_ref[...]
    y = y * jax.nn.sigmoid(y)
    branch_a = _dot(y.astype(BF16), wco_ref[...]) + bco_ref[...]

    for r0 in range(0, tm, ROW_CHUNK):
        row = r0 + lax.broadcasted_iota(jnp.int32, (ROW_CHUNK, POOL_GROUP), 0)
        t1 = i * tm + (row % 8) * nv + row // 8 + 1
        for gi, w in enumerate(POOL_WINDOWS):
            ls = slice(gi * POOL_GROUP, (gi + 1) * POOL_GROUP)
            tok = u_ext[tm + r0:tm + r0 + ROW_CHUNK, ls]
            s = tok
            for j in range(1, w):
                s = s + u_ext[tm + r0 - 8 * j:tm + r0 - 8 * j + ROW_CHUNK, ls]
            cnt = jnp.minimum(t1, w).astype(F32)
            q_buf[r0:r0 + ROW_CHUNK, ls] = s / cnt - tok

    qs_out = []
    for gi in range(len(POOL_WINDOWS)):
        ls = slice(gi * POOL_GROUP, (gi + 1) * POOL_GROUP)
        qs_out.append(_dot(q_buf[:, ls].astype(BF16), wpool_ref[gi]) * spool_ref[:, ls])
    branch_b = jnp.concatenate(qs_out, axis=-1)

    c2 = 2 * D_CONV + D_POOL
    gate_a = jax.nn.sigmoid(proj(c2, c2 + D_MODEL))
    gate_b = jax.nn.sigmoid(proj(c2 + D_MODEL, c2 + 2 * D_MODEL))
    merged = gate_a * branch_a + gate_b * branch_b
    merged = _dot(unperm_ref[...], merged.astype(BF16)).astype(BF16)
    x1 = x + _dot(merged, wout_ref[...])
    x1_ref[...] = x1
    h2_ref[...] = _rms(x1, gffn_ref[...]).astype(BF16)


def _const_spec(shape):
    n = len(shape)
    return pl.BlockSpec(shape, lambda i, _n=n: (0,) * _n)


def _mixer(x, seq_len, g_mix, w_in, b_in, w_dw, b_dw, g_cln, b_cln, w_co, b_co, w_pool, s_pool, w_out,
           g_ffn):
    t = x.shape[0]
    tm = MIX_TM
    assert seq_len % tm == 0 and MIX_NV >= CONV_WIDTH and MIX_NV >= max(POOL_WINDOWS)
    d_in = w_in.shape[1]
    row = pl.BlockSpec((tm, D_MODEL), lambda i: (i, 0))
    n_col = D_CONV // LANE
    w_dw = w_dw.reshape(CONV_WIDTH, n_col, LANE).transpose(1, 0, 2)
    b_dw = b_dw.reshape(n_col, 1, LANE)
    r = jnp.arange(tm)
    perm = ((r % 8) * MIX_NV + r // 8)[:, None] == jnp.arange(tm)[None, :]
    perm = perm.astype(BF16)
    return pl.pallas_call(
        functools.partial(_mixer_kernel, tiles_per_seq=seq_len // tm),
        grid=(t // tm,),
        in_specs=[
            row,
            _const_spec((1, D_MODEL)),
            _const_spec((D_MODEL, d_in)),
            _const_spec((1, d_in)),
            _const_spec((n_col, CONV_WIDTH, LANE)),
            _const_spec((n_col, 1, LANE)),
            _const_spec((1, D_CONV)),
            _const_spec((1, D_CONV)),
            _const_spec((D_CONV, D_MODEL)),
            _const_spec((1, D_MODEL)),
            _const_spec((len(POOL_WINDOWS), POOL_GROUP, POOL_GROUP)),
            _const_spec((1, D_POOL)),
            _const_spec((D_MODEL, D_MODEL)),
            _const_spec((1, D_MODEL)),
            _const_spec((tm, tm)),
            _const_spec((tm, tm)),
        ],
        out_specs=[row, row],
        out_shape=[jax.ShapeDtypeStruct((t, D_MODEL), F32),
                   jax.ShapeDtypeStruct((t, D_MODEL), BF16)],
        scratch_shapes=[
            pltpu.VMEM((n_col, 2 * tm, LANE), F32),
            pltpu.VMEM((n_col, tm, LANE), F32),
            pltpu.VMEM((2 * tm, D_POOL), F32),
            pltpu.VMEM((tm, D_POOL), F32),
            pltpu.VMEM((n_col, tm, LANE), F32),
            pltpu.VMEM((tm, D_POOL), F32),
        ],
        compiler_params=pltpu.CompilerParams(
            dimension_semantics=("arbitrary",), vmem_limit_bytes=VMEM_LIMIT),
        name="mixer",
    )(x, g_mix, w_in, b_in, w_dw, b_dw, g_cln, b_cln, w_co, b_co, w_pool, s_pool, w_out, g_ffn, perm, perm.T)


def _beats(v, other, other_is_later):
    v = jnp.broadcast_to(v, other.shape)
    return jnp.where(other_is_later, jnp.where(v >= other, 1, 0), jnp.where(v > other, 1, 0))


def _router_kernel(h2_ref, wrt_ref, br_ref, utri_ref, ltri_ref, gate_ref, rank_ref, pos_ref, cnt_ref):
    tm = ROUTER_TM
    logits = lax.dot_general(wrt_ref[...], h2_ref[...], (((1,), (1,)), ((), ())),
                             preferred_element_type=F32)
    scores = jax.nn.sigmoid(logits)
    sel = scores + br_ref[...]
    shape3 = (N_GROUPS, GROUP_SIZE, tm)
    sel3 = sel.reshape(shape3)
    scores3 = scores.reshape(shape3)
    neg_inf = jnp.float32(-jnp.inf)

    member = lax.broadcasted_iota(jnp.int32, shape3, 1)
    m1 = jnp.max(sel3, axis=1, keepdims=True)
    first = jnp.min(jnp.where(sel3 == m1, member, GROUP_SIZE), axis=1, keepdims=True)
    m2 = jnp.max(jnp.where(member == first, neg_inf, sel3), axis=1, keepdims=True)
    gscore = jnp.broadcast_to(m1 + m2, shape3)

    gidx = lax.broadcasted_iota(jnp.int32, shape3, 0)
    grank = jnp.zeros(shape3, jnp.int32)
    for j in range(N_GROUPS):
        sj = gscore[j:j + 1]
        grank = grank + _beats(sj, gscore, gidx > j)
    masked = jnp.where(grank < TOPK_GROUPS, sel3, neg_inf)

    eidx = gidx * GROUP_SIZE + member
    work = masked
    erank = jnp.full(shape3, TOP_K, jnp.int32)
    for k in range(TOP_K):
        best = jnp.max(jnp.max(work, axis=0, keepdims=True), axis=1, keepdims=True)
        cand = jnp.where(work == best, eidx, N_EXPERTS)
        pick = jnp.min(jnp.min(cand, axis=0, keepdims=True), axis=1, keepdims=True)
        hit = eidx == pick
        work = jnp.where(hit, neg_inf, work)
        erank = jnp.where(hit, k, erank)
    chosen = erank < TOP_K
    top_s = jnp.where(chosen, scores3, 0.0)
    denom = jnp.sum(jnp.sum(top_s, axis=0, keepdims=True), axis=1, keepdims=True)
    gates3 = top_s / denom * ROUTED_SCALE
    chosen2 = jnp.where(chosen, 1.0, 0.0).reshape(N_EXPERTS, tm)
    gate_ref[...] = gates3.reshape(N_EXPERTS, tm).astype(BF16)

    for w in range(tm // WIN):
        ls = slice(w * WIN, (w + 1) * WIN)
        mw = chosen2[:, ls]
        rank = _dot(mw.astype(BF16), utri_ref[...])
        n = jnp.sum(mw, axis=1, keepdims=True)
        run = jnp.floor((n + 7.0) * 0.125) * 8.0
        start = _dot(ltri_ref[...], jnp.broadcast_to(run, (N_EXPERTS, WIN)).astype(BF16))
        rank_ref[:, ls] = jnp.where(mw > 0.5, rank, -1.0).astype(BF16)
        row3 = (rank + start).reshape(N_GROUPS, GROUP_SIZE, WIN)
        er = erank[:, :, ls]
        for k in range(TOP_K):
            pk = jnp.sum(jnp.sum(jnp.where(er == k, row3, 0.0), axis=0, keepdims=True), axis=1, keepdims=True)
            pos_ref[k:k + 1, ls] = pk.reshape(1, WIN).astype(jnp.int32)
        cnt_ref[w] = n


def _router(h2, w_rt, b_r):
    t = h2.shape[0]
    tm = ROUTER_TM
    utri = jnp.triu(jnp.ones((WIN, WIN), BF16), k=1)
    ltri = jnp.tril(jnp.ones((N_EXPERTS, N_EXPERTS), BF16), k=-1)
    return pl.pallas_call(
        _router_kernel,
        grid=(t // tm,),
        in_specs=[
            pl.BlockSpec((tm, D_MODEL), lambda i: (i, 0)),
            _const_spec((N_EXPERTS, D_MODEL)),
            _const_spec((N_EXPERTS, 1)),
            _const_spec((WIN, WIN)),
            _const_spec((N_EXPERTS, N_EXPERTS)),
        ],
        out_specs=[
            pl.BlockSpec((N_EXPERTS, tm), lambda i: (0, i)),
            pl.BlockSpec((N_EXPERTS, tm), lambda i: (0, i)),
            pl.BlockSpec((TOP_K, tm), lambda i: (0, i)),
            pl.BlockSpec((tm // WIN, N_EXPERTS, 1), lambda i: (i, 0, 0)),
        ],
        out_shape=[
            jax.ShapeDtypeStruct((N_EXPERTS, t), BF16),
            jax.ShapeDtypeStruct((N_EXPERTS, t), BF16),
            jax.ShapeDtypeStruct((TOP_K, t), jnp.int32),
            jax.ShapeDtypeStruct((t // WIN, N_EXPERTS, 1), F32),
        ],
        compiler_params=pltpu.CompilerParams(
            dimension_semantics=("arbitrary",), vmem_limit_bytes=VMEM_LIMIT),
        name="router",
    )(h2, w_rt, b_r, utri, ltri)


def _sorted_rows_bound(t):
    rows = t * TOP_K + (t // WIN) * N_EXPERTS * 7 + N_EXPERTS * (EXP_BM - 1)
    group_rows = EXP_GROUP * EXP_BM
    return -(-rows // group_rows) * group_rows


def _dispatch_plan(cnt, t):
    nw = t // WIN
    n = cnt.reshape(nw, N_EXPERTS).astype(jnp.int32)
    run = (n + 7) // 8 * 8
    local_end = jnp.cumsum(run, axis=1)
    local_off = jnp.concatenate([jnp.zeros((nw, 1), jnp.int32), local_end], axis=1)
    total = jnp.sum(run, axis=0)
    region = (total + EXP_BM - 1) // EXP_BM * EXP_BM
    eid = jnp.arange(N_EXPERTS, dtype=jnp.int32)
    last_owner = jnp.max(jnp.where(region > 0, eid, 0))
    short = (-(jnp.sum(region) // EXP_BM)) % EXP_GROUP
    region = region + jnp.where(eid == last_owner, short * EXP_BM, 0)
    region_end = jnp.cumsum(region)
    base = region_end - region
    global_off = base[None, :] + jnp.cumsum(run, axis=0) - run
    n_blocks = _sorted_rows_bound(t) // EXP_BM
    n_used = region_end[-1] // EXP_BM
    blk = jnp.arange(n_blocks, dtype=jnp.int32)
    blk_expert = jnp.sum((region_end[None, :] <= blk[:, None] * EXP_BM).astype(jnp.int32), axis=1)
    blk_expert = jnp.minimum(blk_expert, N_EXPERTS - 1)
    later_nonempty = (eid[None, :] > eid[:, None]) & (region[None, :] > 0)
    next_expert = jnp.min(jnp.where(later_nonempty, eid[None, :], N_EXPERTS), axis=1).astype(jnp.int32)
    return dict(
        run_lo=local_off[:, :N_EXPERTS].reshape(nw, N_EXPERTS, 1),
        run_hi=local_off[:, 1:].reshape(nw, N_EXPERTS, 1),
        local_off=local_off.reshape(-1), global_off=global_off.reshape(-1),
        fill_off=base + total, fill_cnt=region - total,
        blk_expert=blk_expert.astype(jnp.int32), next_expert=next_expert,
        n_used=n_used.reshape(1).astype(jnp.int32))


def _run_copy(local_ref, global_ref, win, e, vmem_buf, slot, hbm_buf, sem, to_hbm):
    lo = pl.multiple_of(local_ref[win * (N_EXPERTS + 1) + e], 8)
    cnt = pl.multiple_of(local_ref[win * (N_EXPERTS + 1) + e + 1] - lo, 8)
    go = pl.multiple_of(global_ref[win * N_EXPERTS + e], 8)
    v = vmem_buf.at[pl.ds(pl.multiple_of(slot * SEL_ROWS + lo, 8), cnt)]
    h = hbm_buf.at[pl.ds(go, cnt)]
    cp = pltpu.make_async_copy(v, h, sem.at[slot]) if to_hbm else pltpu.make_async_copy(h, v, sem.at[slot])
    return cnt, cp


def _start_runs(local_ref, global_ref, win, vmem_buf, slot, hbm_buf, sem, to_hbm):
    def body(e, carry):
        cnt, cp = _run_copy(local_ref, global_ref, win, e, vmem_buf, slot, hbm_buf, sem, to_hbm)

        @pl.when(cnt > 0)
        def _():
            cp.start()
        return carry
    lax.fori_loop(0, N_EXPERTS, body, 0)


def _wait_runs(local_ref, win, vmem_buf, slot, hbm_buf, sem, to_hbm):
    total = pl.multiple_of(local_ref[win * (N_EXPERTS + 1) + N_EXPERTS], 8)
    v = vmem_buf.at[pl.ds(pl.multiple_of(slot * SEL_ROWS, 8), total)]
    h = hbm_buf.at[pl.ds(0, total)]
    cp = pltpu.make_async_copy(v, h, sem.at[slot]) if to_hbm else pltpu.make_async_copy(h, v, sem.at[slot])

    @pl.when(total > 0)
    def _():
        cp.wait()


def _dispatch_kernel(local_ref, global_ref, fill_off_ref, fill_cnt_ref, h2_ref, pos_ref, xs_hbm, sbuf,
                     s_ref, sem, zsem, *, n_win):
    w = pl.program_id(0)
    slot = w % 2
    pos = pos_ref[...]

    h2 = h2_ref[...]
    assert SEL_RG <= 256
    rid_b = lax.broadcasted_iota(jnp.int32, (SEL_RG, WIN), 0).astype(F32).astype(BF16)
    one_b = jnp.ones((SEL_RG, WIN), BF16)
    for g in range(SEL_ROWS // SEL_MM):
        for sg in range(SEL_MM // SEL_RG):
            r0 = g * SEL_MM + sg * SEL_RG
            acc = jnp.zeros((SEL_RG, WIN), BF16)
            for k in range(TOP_K):
                off = (pos[k:k + 1, :] - r0).astype(F32)
                off = jnp.broadcast_to(off, (SEL_RG, WIN)).astype(BF16)
                acc = jnp.where(rid_b == off, one_b, acc)
            s_ref[r0:r0 + SEL_RG, :] = acc
        rows = slice(g * SEL_MM, (g + 1) * SEL_MM)
        dst = pl.multiple_of(slot * SEL_ROWS + g * SEL_MM, SEL_MM)
        sbuf[pl.ds(dst, SEL_MM), :] = _dot(s_ref[rows, :], h2).astype(BF16)

    _start_runs(local_ref, global_ref, w, sbuf, slot, xs_hbm, sem, True)

    @pl.when(w > 0)
    def _():
        _wait_runs(local_ref, w - 1, sbuf, 1 - slot, xs_hbm, sem, True)

    @pl.when(w == n_win - 1)
    def _():
        sbuf[2 * SEL_ROWS:, :] = jnp.zeros((EXP_GROUP * EXP_BM, D_MODEL), BF16)

        def fill(e, wait):
            cnt = pl.multiple_of(fill_cnt_ref[e], 8)
            off = pl.multiple_of(fill_off_ref[e], 8)
            cp = pltpu.make_async_copy(sbuf.at[pl.ds(2 * SEL_ROWS, cnt)], xs_hbm.at[pl.ds(off, cnt)], zsem)

            @pl.when(cnt > 0)
            def _():
                if wait:
                    cp.wait()
                else:
                    cp.start()

        def start_body(e, carry):
            fill(e, False)
            return carry

        def wait_body(e, carry):
            fill(e, True)
            return carry
        lax.fori_loop(0, N_EXPERTS, start_body, 0)
        _wait_runs(local_ref, w, sbuf, slot, xs_hbm, sem, True)
        lax.fori_loop(0, N_EXPERTS, wait_body, 0)


def _staging_shape(extra_rows):
    return jax.ShapeDtypeStruct((2 * SEL_ROWS + extra_rows, D_MODEL), BF16)


def _staging_spec(extra_rows):
    return pl.BlockSpec((2 * SEL_ROWS + extra_rows, D_MODEL), lambda w, *_: (0, 0))


def _dispatch(plan, h2, pos):
    t = h2.shape[0]
    n_win = t // WIN
    return pl.pallas_call(
        functools.partial(_dispatch_kernel, n_win=n_win),
        grid_spec=pltpu.PrefetchScalarGridSpec(
            num_scalar_prefetch=4,
            grid=(n_win,),
            in_specs=[
                pl.BlockSpec((WIN, D_MODEL), lambda w, *_: (w, 0)),
                pl.BlockSpec((TOP_K, WIN), lambda w, *_: (0, w)),
            ],
            out_specs=[pl.BlockSpec(memory_space=pl.ANY), _staging_spec(EXP_GROUP * EXP_BM)],
            scratch_shapes=[
                pltpu.VMEM((SEL_ROWS, WIN), BF16),
                pltpu.SemaphoreType.DMA((2,)),
                pltpu.SemaphoreType.DMA,
            ]),
        out_shape=[jax.ShapeDtypeStruct((_sorted_rows_bound(t), D_MODEL), BF16),
                   _staging_shape(EXP_GROUP * EXP_BM)],
        compiler_params=pltpu.CompilerParams(
            dimension_semantics=("arbitrary",), vmem_limit_bytes=VMEM_LIMIT),
        name="dispatch",
    )(plan['local_off'], plan['global_off'], plan['fill_off'], plan['fill_cnt'], h2, pos)[0]


def _expert_kernel(blk_expert_ref, next_expert_ref, n_used_ref, xs_hbm, wg_hbm, wu_hbm, wd_hbm, ys_hbm,
                   xbuf, ybuf, wg_st, wu_st, wd_st, wg_bf, wu_bf, wd_bf, xsem, ysem, wsem):
    n_used = n_used_ref[0]
    part = EXP_BM // EXP_SPLIT

    def row_copies(b, slot, fetch):
        out = []
        for q in range(EXP_SPLIT):
            hbm_rows = pl.ds(pl.multiple_of(b * EXP_BM + q * part, part), part)
            if fetch:
                out.append(pltpu.make_async_copy(xs_hbm.at[hbm_rows], xbuf.at[slot, q * part:(q + 1) * part],
                                                 xsem.at[slot]))
            else:
                out.append(pltpu.make_async_copy(ybuf.at[slot, q * part:(q + 1) * part], ys_hbm.at[hbm_rows],
                                                 ysem.at[slot]))
        return out

    def weight_copies(e, slot):
        return [pltpu.make_async_copy(wg_hbm.at[e], wg_st.at[slot], wsem.at[slot]),
                pltpu.make_async_copy(wu_hbm.at[e], wu_st.at[slot], wsem.at[slot]),
                pltpu.make_async_copy(wd_hbm.at[e], wd_st.at[slot], wsem.at[slot])]

    def start(copies):
        for c in copies:
            c.start()

    def wait(copies):
        for c in copies:
            c.wait()

    for ahead in range(EXP_XDEPTH - EXP_GROUP):
        @pl.when(ahead < n_used)
        def _(ahead=ahead):
            start(row_copies(ahead, ahead, True))

    @pl.when(n_used > 0)
    def _():
        start(weight_copies(blk_expert_ref[0], 0))

    def enter_block(b, changes):
        e = blk_expert_ref[b]
        new_expert = jnp.logical_or(b == 0, e != blk_expert_ref[jnp.maximum(b - 1, 0)])
        changes = changes + new_expert.astype(jnp.int32)

        @pl.when(new_expert)
        def _():
            stage = changes % 2
            wset = changes % EXP_GROUP
            wait(weight_copies(e, stage))
            wg_bf[wset] = wg_st[stage].astype(BF16)
            wu_bf[wset] = wu_st[stage].astype(BF16)
            wd_bf[wset] = wd_st[stage].astype(BF16)
            nxt = next_expert_ref[e]

            @pl.when(nxt < N_EXPERTS)
            def _():
                start(weight_copies(nxt, 1 - stage))

        ahead = b + EXP_XDEPTH - EXP_GROUP

        @pl.when(ahead < n_used)
        def _():
            start(row_copies(ahead, ahead % EXP_XDEPTH, True))

        wait(row_copies(b, b % EXP_XDEPTH, True))

        @pl.when(b >= EXP_YDEPTH)
        def _():
            wait(row_copies(b - EXP_YDEPTH, b % EXP_YDEPTH, False))
        return changes

    def compute(b, changes):
        wset = changes % EXP_GROUP
        x = xbuf[b % EXP_XDEPTH]
        hg = _dot(x, wg_bf[wset])
        hb = hg * jax.nn.sigmoid(hg) * _dot(x, wu_bf[wset])
        ybuf[b % EXP_YDEPTH] = _dot(hb.astype(BF16), wd_bf[wset]).astype(BF16)

    def body(g, changes):
        blocks = [g * EXP_GROUP + j for j in range(EXP_GROUP)]
        seen = []
        for b in blocks:
            changes = enter_block(b, changes)
            seen.append(changes)
        for b, c in zip(blocks, seen):
            compute(b, c)
        for b in blocks:
            start(row_copies(b, b % EXP_YDEPTH, False))
        return changes

    lax.fori_loop(0, n_used // EXP_GROUP, body, jnp.int32(-1))

    for back in range(EXP_YDEPTH, 0, -1):
        @pl.when(n_used >= back)
        def _(back=back):
            wait(row_copies(n_used - back, (n_used - back) % EXP_YDEPTH, False))


def _experts(plan, xs, w_gate, w_up, w_down):
    any_spec = pl.BlockSpec(memory_space=pl.ANY)
    return pl.pallas_call(
        _expert_kernel,
        grid_spec=pltpu.PrefetchScalarGridSpec(
            num_scalar_prefetch=3,
            grid=(1,),
            in_specs=[any_spec, any_spec, any_spec, any_spec],
            out_specs=any_spec,
            scratch_shapes=[
                pltpu.VMEM((EXP_XDEPTH, EXP_BM, D_MODEL), BF16),
                pltpu.VMEM((EXP_YDEPTH, EXP_BM, D_MODEL), BF16),
                pltpu.VMEM((2, D_MODEL, D_EXPERT), F32),
                pltpu.VMEM((2, D_MODEL, D_EXPERT), F32),
                pltpu.VMEM((2, D_EXPERT, D_MODEL), F32),
                pltpu.VMEM((EXP_GROUP, D_MODEL, D_EXPERT), BF16),
                pltpu.VMEM((EXP_GROUP, D_MODEL, D_EXPERT), BF16),
                pltpu.VMEM((EXP_GROUP, D_EXPERT, D_MODEL), BF16),
                pltpu.SemaphoreType.DMA((EXP_XDEPTH,)),
                pltpu.SemaphoreType.DMA((EXP_YDEPTH,)),
                pltpu.SemaphoreType.DMA((2,)),
            ]),
        out_shape=jax.ShapeDtypeStruct(xs.shape, BF16),
        compiler_params=pltpu.CompilerParams(
            dimension_semantics=("arbitrary",), vmem_limit_bytes=VMEM_LIMIT),
        name="experts",
    )(plan['blk_expert'], plan['next_expert'], plan['n_used'], xs, w_gate, w_up, w_down)


def _combine_kernel(local_ref, global_ref, x1_ref, h2_ref, p_ref, rank_ref, gate_ref, lo_ref, hi_ref,
                    wsg_ref, wsu_ref, wsd_ref, gple_ref, wpg_ref, wp_ref, gfin_ref, ys_hbm, o_ref, ybuf, st_ref, sem,
                    *, n_win, final_norm):
    w = pl.program_id(0)
    slot = w % 2

    @pl.when(w == 0)
    def _():
        ybuf[...] = jnp.zeros(ybuf.shape, BF16)
        _start_runs(local_ref, global_ref, w, ybuf, slot, ys_hbm, sem, False)

    @pl.when(w + 1 < n_win)
    def _():
        _start_runs(local_ref, global_ref, w + 1, ybuf, 1 - slot, ys_hbm, sem, False)

    lo = lo_ref[0]
    hi = hi_ref[0]
    lo_f = lo.astype(F32)
    rank_tbl = rank_ref[...]
    gate_tbl = gate_ref[...]

    def build_group(lg):
        cols = slice(lg * CMB_LG, (lg + 1) * CMB_LG)
        rid = lg * CMB_LG + lax.broadcasted_iota(jnp.int32, (N_EXPERTS, CMB_LG), 1)
        owner = jnp.where(rid >= lo, jnp.where(rid < hi, 1.0, 0.0), 0.0)
        run_row = rid[0:1, :].astype(F32) - jnp.sum(owner * lo_f, axis=0, keepdims=True)
        owner = owner.astype(BF16)
        hit = _dot_t(rank_tbl, owner) == run_row
        st_ref[:, cols] = jnp.where(hit, _dot_t(gate_tbl, owner), 0.0).astype(BF16)

    build_group(0)
    h2 = h2_ref[...]
    hs = _dot(h2, wsg_ref[...])
    hs = hs * jax.nn.sigmoid(hs) * _dot(h2, wsu_ref[...])
    shared = _dot(hs.astype(BF16), wsd_ref[...])

    _wait_runs(local_ref, w, ybuf, slot, ys_hbm, sem, False)
    routed = None
    n_groups = SEL_ROWS // CMB_LG
    for lg in range(n_groups):
        if lg + 1 < n_groups:
            build_group(lg + 1)
        src = pl.multiple_of(slot * SEL_ROWS + lg * CMB_LG, CMB_LG)
        part = _dot(st_ref[:, lg * CMB_LG:(lg + 1) * CMB_LG], ybuf[pl.ds(src, CMB_LG), :])
        routed = part if routed is None else routed + part
    x2 = x1_ref[...] + routed + shared

    hp = _rms(x2, gple_ref[...]).astype(BF16)
    gate = jax.nn.sigmoid(_dot(hp, wpg_ref[...]))
    x3 = x2 + gate * _dot(p_ref[...].astype(BF16), wp_ref[...])
    o_ref[...] = _rms(x3, gfin_ref[...]) if final_norm else x3


def _combine(plan, ys, x1, h2, p, rank_tbl, gate_tbl, wsg, wsu, wsd, g_ple, w_pg, w_p, g_fin, final_norm):
    t = x1.shape[0]
    n_win = t // WIN
    row = lambda width: pl.BlockSpec((WIN, width), lambda w, *_: (w, 0))
    const = lambda shape: pl.BlockSpec(shape, lambda w, *_: (0,) * len(shape))
    return pl.pallas_call(
        functools.partial(_combine_kernel, n_win=n_win, final_norm=final_norm),
        grid_spec=pltpu.PrefetchScalarGridSpec(
            num_scalar_prefetch=2,
            grid=(n_win,),
            in_specs=[
                row(D_MODEL), row(D_MODEL), row(PLE_DIM),
                pl.BlockSpec((N_EXPERTS, WIN), lambda w, *_: (0, w)),
                pl.BlockSpec((N_EXPERTS, WIN), lambda w, *_: (0, w)),
                pl.BlockSpec((1, N_EXPERTS, 1), lambda w, *_: (w, 0, 0)),
                pl.BlockSpec((1, N_EXPERTS, 1), lambda w, *_: (w, 0, 0)),
                const((D_MODEL, D_EXPERT)), const((D_MODEL, D_EXPERT)), const((D_EXPERT, D_MODEL)),
                const((1, D_MODEL)), const((D_MODEL, D_MODEL)), const((PLE_DIM, D_MODEL)),
                const((1, D_MODEL)),
                pl.BlockSpec(memory_space=pl.ANY),
            ],
            out_specs=[row(D_MODEL), _staging_spec(0)],
            scratch_shapes=[
                pltpu.VMEM((WIN, SEL_ROWS), BF16),
                pltpu.SemaphoreType.DMA((2,)),
            ]),
        out_shape=[jax.ShapeDtypeStruct((t, D_MODEL), F32), _staging_shape(0)],
        compiler_params=pltpu.CompilerParams(
            dimension_semantics=("arbitrary",), vmem_limit_bytes=VMEM_LIMIT),
        name="combine",
    )(plan['local_off'], plan['global_off'], x1, h2, p, rank_tbl, gate_tbl, plan['run_lo'], plan['run_hi'],
      wsg, wsu, wsd, g_ple, w_pg, w_p, g_fin, ys)[0]


def kernel(x, p, g_mix, w_in, b_in, w_dw, b_dw, g_cln, b_cln, w_conv_out, b_conv_out, w_pool, s_pool,
           w_out, g_ffn, w_router, b_router, w_e_gate, w_e_up, w_e_down, w_s_gate, w_s_up, w_s_down,
           g_ple, w_ple_gate, w_ple, g_final):
    bsz, s, d = x.shape
    t = bsz * s
    depth = w_in.shape[0]
    xt = x.reshape(t, d)
    row = lambda v: v.reshape(1, -1)
    for i in range(depth):
        x1, h2 = _mixer(
            xt, s, row(g_mix[i]), w_in[i].astype(BF16), row(b_in[i]), w_dw[i], row(b_dw[i]),
            row(g_cln[i]), row(b_cln[i]), w_conv_out[i].astype(BF16), row(b_conv_out[i]),
            w_pool[i].astype(BF16), row(s_pool[i]), w_out[i].astype(BF16), row(g_ffn[i]))
        gate, rank, pos, cnt = _router(h2, w_router[i].T.astype(BF16), b_router[i].reshape(N_EXPERTS, 1))
        plan = _dispatch_plan(cnt, t)
        xs = _dispatch(plan, h2, pos)
        ys = _experts(plan, xs, w_e_gate[i], w_e_up[i], w_e_down[i])
        xt = _combine(
            plan, ys, x1, h2, p[i].reshape(t, PLE_DIM), rank, gate,
            w_s_gate[i].astype(BF16), w_s_up[i].astype(BF16), w_s_down[i].astype(BF16),
            row(g_ple[i]), w_ple_gate[i].astype(BF16), w_ple[i].astype(BF16), row(g_final),
            final_norm=(i == depth - 1))
    return xt.reshape(bsz, s, d)
```

```python
import functools

import jax
import jax.numpy as jnp
from jax import lax
from jax.experimental import pallas as pl
from jax.experimental.pallas import tpu as pltpu

D_MODEL = 1024
D_CONV = 1024
D_POOL = 1024
CONV_WIDTH = 31
POOL_WINDOWS = (2, 4, 8, 16)
POOL_GROUP = 256
PLE_DIM = 256
N_EXPERTS = 64
N_GROUPS = 8
GROUP_SIZE = N_EXPERTS // N_GROUPS
TOPK_GROUPS = 4
TOP_K = 8
D_EXPERT = 256
ROUTED_SCALE = 2.5
NORM_EPS = 1e-6

F32 = jnp.float32
BF16 = jnp.bfloat16

MIX_TM = 256
MIX_NV = MIX_TM // 8
CONV_MG = 8
ROW_CHUNK = 64
LANE = 128

ROUTER_TM = 1024
WIN = 256
SEL_ROWS = 2560
SEL_RG = 64
SEL_MM = 512
EXP_BM = 592
EXP_XDEPTH = 8
EXP_YDEPTH = 4
EXP_SPLIT = 1
CMB_LG = 512

VMEM_LIMIT = 56 * 1024 * 1024


def _rms(x, g):
    ms = jnp.mean(x * x, axis=-1, keepdims=True)
    return x * lax.rsqrt(ms + NORM_EPS) * g


def _dot(a, b):
    return jnp.dot(a, b, preferred_element_type=F32)


def _dot_t(a, b):
    return lax.dot_general(a, b, (((0,), (0,)), ((), ())), preferred_element_type=F32)


def _mixer_kernel(x_ref, gmix_ref, win_ref, bin_ref, wdw_ref, bdw_ref, gcln_ref, bcln_ref,
                  wco_ref, bco_ref, wpool_ref, spool_ref, wout_ref, gffn_ref, perm_ref, unperm_ref,
                  x1_ref, h2_ref, a_ext, a_prev, u_ext, u_prev, c_buf, q_buf, *, tiles_per_seq):
    i = pl.program_id(0) % tiles_per_seq
    tm = MIX_TM
    nv = MIX_NV

    @pl.when(i == 0)
    def _():
        a_prev[...] = jnp.zeros(a_prev.shape, F32)
        u_prev[...] = jnp.zeros(u_prev.shape, F32)

    x = x_ref[...]
    h = _dot(perm_ref[...], _rms(x, gmix_ref[...]).astype(BF16)).astype(BF16)

    def proj(lo, hi):
        return _dot(h, win_ref[:, lo:hi]) + bin_ref[:, lo:hi]

    glu = proj(0, D_CONV) * jax.nn.sigmoid(proj(D_CONV, 2 * D_CONV))
    for lc in range(D_CONV // LANE):
        a_ext[lc, tm:2 * tm, :] = glu[:, lc * LANE:(lc + 1) * LANE]
    u_ext[tm:2 * tm, :] = proj(2 * D_CONV, 2 * D_CONV + D_POOL)

    def delayed_groups(ext, prev, first_group):
        last_row = lax.broadcasted_iota(jnp.int32, (8, ext.shape[-1]), 0) == 7
        for g in range(first_group, nv):
            rows = slice(8 * g, 8 * g + 8)
            mixed = jnp.where(last_row, prev[rows, :], ext[tm + 8 * g:tm + 8 * g + 8, :])
            ext[rows, :] = pltpu.roll(mixed, 1, axis=0)
            prev[rows, :] = ext[tm + 8 * g:tm + 8 * g + 8, :]

    delayed_groups(u_ext, u_prev, nv - (max(POOL_WINDOWS) - 1))

    def conv_column(lc, carry):
        a_col = a_ext.at[lc]
        delayed_groups(a_col, a_prev.at[lc], nv - (CONV_WIDTH - 1))
        w_col = wdw_ref.at[lc]
        for g0 in range(0, nv, CONV_MG):
            acc = None
            for k in range(CONV_WIDTH):
                src = nv + g0 + k - (CONV_WIDTH - 1)
                term = a_col[8 * src:8 * (src + CONV_MG), :] * w_col[k:k + 1, :]
                acc = term if acc is None else acc + term
            c_buf[lc, 8 * g0:8 * (g0 + CONV_MG), :] = acc + bdw_ref[lc]
        return carry
    lax.fori_loop(0, D_CONV // LANE, conv_column, 0)

    c = jnp.concatenate([c_buf[lc] for lc in range(D_CONV // LANE)], axis=-1)
    mu = jnp.mean(c, axis=-1, keepdims=True)
    xc = c - mu
    var = jnp.mean(xc * xc, axis=-1, keepdims=True)
    y = xc * lax.rsqrt(var + NORM_EPS) * gcln_ref[...] + bcln_ref[...]
    y = y * jax.nn.sigmoid(y)
    branch_a = _dot(y.astype(BF16), wco_ref[...]) + bco_ref[...]

    for r0 in range(0, tm, ROW_CHUNK):
        row = r0 + lax.broadcasted_iota(jnp.int32, (ROW_CHUNK, POOL_GROUP), 0)
        t1 = i * tm + (row % 8) * nv + row // 8 + 1
        for gi, w in enumerate(POOL_WINDOWS):
            ls = slice(gi * POOL_GROUP, (gi + 1) * POOL_GROUP)
            tok = u_ext[tm + r0:tm + r0 + ROW_CHUNK, ls]
            s = tok
            for j in range(1, w):
                s = s + u_ext[tm + r0 - 8 * j:tm + r0 - 8 * j + ROW_CHUNK, ls]
            cnt = jnp.minimum(t1, w).astype(F32)
            q_buf[r0:r0 + ROW_CHUNK, ls] = s / cnt - tok

    qs_out = []
    for gi in range(len(POOL_WINDOWS)):
        ls = slice(gi * POOL_GROUP, (gi + 1) * POOL_GROUP)
        qs_out.append(_dot(q_buf[:, ls].astype(BF16), wpool_ref[gi]) * spool_ref[:, ls])
    branch_b = jnp.concatenate(qs_out, axis=-1)

    c2 = 2 * D_CONV + D_POOL
    gate_a = jax.nn.sigmoid(proj(c2, c2 + D_MODEL))
    gate_b = jax.nn.sigmoid(proj(c2 + D_MODEL, c2 + 2 * D_MODEL))
    merged = gate_a * branch_a + gate_b * branch_b
    merged = _dot(unperm_ref[...], merged.astype(BF16)).astype(BF16)
    x1 = x + _dot(merged, wout_ref[...])
    x1_ref[...] = x1
    h2_ref[...] = _rms(x1, gffn_ref[...]).astype(BF16)


def _const_spec(shape):
    n = len(shape)
    return pl.BlockSpec(shape, lambda i, _n=n: (0,) * _n)


def _mixer(x, seq_len, g_mix, w_in, b_in, w_dw, b_dw, g_cln, b_cln, w_co, b_co, w_pool, s_pool, w_out,
           g_ffn):
    t = x.shape[0]
    tm = MIX_TM
    assert seq_len % tm == 0 and MIX_NV >= CONV_WIDTH and MIX_NV >= max(POOL_WINDOWS)
    d_in = w_in.shape[1]
    row = pl.BlockSpec((tm, D_MODEL), lambda i: (i, 0))
    n_col = D_CONV // LANE
    w_dw = w_dw.reshape(CONV_WIDTH, n_col, LANE).transpose(1, 0, 2)
    b_dw = b_dw.reshape(n_col, 1, LANE)
    r = jnp.arange(tm)
    perm = ((r % 8) * MIX_NV + r // 8)[:, None] == jnp.arange(tm)[None, :]
    perm = perm.astype(BF16)
    return pl.pallas_call(
        functools.partial(_mixer_kernel, tiles_per_seq=seq_len // tm),
        grid=(t // tm,),
        in_specs=[
            row,
            _const_spec((1, D_MODEL)),
            _const_spec((D_MODEL, d_in)),
            _const_spec((1, d_in)),
            _const_spec((n_col, CONV_WIDTH, LANE)),
            _const_spec((n_col, 1, LANE)),
            _const_spec((1, D_CONV)),
            _const_spec((1, D_CONV)),
            _const_spec((D_CONV, D_MODEL)),
            _const_spec((1, D_MODEL)),
            _const_spec((len(POOL_WINDOWS), POOL_GROUP, POOL_GROUP)),
            _const_spec((1, D_POOL)),
            _const_spec((D_MODEL, D_MODEL)),
            _const_spec((1, D_MODEL)),
            _const_spec((tm, tm)),
            _const_spec((tm, tm)),
        ],
        out_specs=[row, row],
        out_shape=[jax.ShapeDtypeStruct((t, D_MODEL), F32),
                   jax.ShapeDtypeStruct((t, D_MODEL), BF16)],
        scratch_shapes=[
            pltpu.VMEM((n_col, 2 * tm, LANE), F32),
            pltpu.VMEM((n_col, tm, LANE), F32),
            pltpu.VMEM((2 * tm, D_POOL), F32),
            pltpu.VMEM((tm, D_POOL), F32),
            pltpu.VMEM((n_col, tm, LANE), F32),
            pltpu.VMEM((tm, D_POOL), F32),
        ],
        compiler_params=pltpu.CompilerParams(
            dimension_semantics=("arbitrary",), vmem_limit_bytes=VMEM_LIMIT),
        name="mixer",
    )(x, g_mix, w_in, b_in, w_dw, b_dw, g_cln, b_cln, w_co, b_co, w_pool, s_pool, w_out, g_ffn, perm, perm.T)


def _beats(v, other, other_is_later):
    v = jnp.broadcast_to(v, other.shape)
    return jnp.where(other_is_later, jnp.where(v >= other, 1, 0), jnp.where(v > other, 1, 0))


def _router_kernel(h2_ref, wrt_ref, br_ref, utri_ref, ltri_ref, gate_ref, rank_ref, pos_ref, cnt_ref):
    tm = ROUTER_TM
    logits = lax.dot_general(wrt_ref[...], h2_ref[...], (((1,), (1,)), ((), ())),
                             preferred_element_type=F32)
    scores = jax.nn.sigmoid(logits)
    sel = scores + br_ref[...]
    shape3 = (N_GROUPS, GROUP_SIZE, tm)
    sel3 = sel.reshape(shape3)
    scores3 = scores.reshape(shape3)
    neg_inf = jnp.float32(-jnp.inf)

    member = lax.broadcasted_iota(jnp.int32, shape3, 1)
    m1 = jnp.max(sel3, axis=1, keepdims=True)
    first = jnp.min(jnp.where(sel3 == m1, member, GROUP_SIZE), axis=1, keepdims=True)
    m2 = jnp.max(jnp.where(member == first, neg_inf, sel3), axis=1, keepdims=True)
    gscore = jnp.broadcast_to(m1 + m2, shape3)

    gidx = lax.broadcasted_iota(jnp.int32, shape3, 0)
    grank = jnp.zeros(shape3, jnp.int32)
    for j in range(N_GROUPS):
        sj = gscore[j:j + 1]
        grank = grank + _beats(sj, gscore, gidx > j)
    masked = jnp.where(grank < TOPK_GROUPS, sel3, neg_inf)

    eidx = gidx * GROUP_SIZE + member
    work = masked
    erank = jnp.full(shape3, TOP_K, jnp.int32)
    for k in range(TOP_K):
        best = jnp.max(jnp.max(work, axis=0, keepdims=True), axis=1, keepdims=True)
        cand = jnp.where(work == best, eidx, N_EXPERTS)
        pick = jnp.min(jnp.min(cand, axis=0, keepdims=True), axis=1, keepdims=True)
        hit = eidx == pick
        work = jnp.where(hit, neg_inf, work)
        erank = jnp.where(hit, k, erank)
    chosen = erank < TOP_K
    top_s = jnp.where(chosen, scores3, 0.0)
    denom = jnp.sum(jnp.sum(top_s, axis=0, keepdims=True), axis=1, keepdims=True)
    gates3 = top_s / denom * ROUTED_SCALE
    chosen2 = jnp.where(chosen, 1.0, 0.0).reshape(N_EXPERTS, tm)
    gate_ref[...] = gates3.reshape(N_EXPERTS, tm).astype(BF16)

    for w in range(tm // WIN):
        ls = slice(w * WIN, (w + 1) * WIN)
        mw = chosen2[:, ls]
        rank = _dot(mw.astype(BF16), utri_ref[...])
        n = jnp.sum(mw, axis=1, keepdims=True)
        run = jnp.floor((n + 7.0) * 0.125) * 8.0
        start = _dot(ltri_ref[...], jnp.broadcast_to(run, (N_EXPERTS, WIN)).astype(BF16))
        rank_ref[:, ls] = jnp.where(mw > 0.5, rank, -1.0).astype(BF16)
        row3 = (rank + start).reshape(N_GROUPS, GROUP_SIZE, WIN)
        er = erank[:, :, ls]
        for k in range(TOP_K):
            pk = jnp.sum(jnp.sum(jnp.where(er == k, row3, 0.0), axis=0, keepdims=True), axis=1, keepdims=True)
            pos_ref[k:k + 1, ls] = pk.reshape(1, WIN).astype(jnp.int32)
        cnt_ref[w] = n


def _router(h2, w_rt, b_r):
    t = h2.shape[0]
    tm = ROUTER_TM
    utri = jnp.triu(jnp.ones((WIN, WIN), BF16), k=1)
    ltri = jnp.tril(jnp.ones((N_EXPERTS, N_EXPERTS), BF16), k=-1)
    return pl.pallas_call(
        _router_kernel,
        grid=(t // tm,),
        in_specs=[
            pl.BlockSpec((tm, D_MODEL), lambda i: (i, 0)),
            _const_spec((N_EXPERTS, D_MODEL)),
            _const_spec((N_EXPERTS, 1)),
            _const_spec((WIN, WIN)),
            _const_spec((N_EXPERTS, N_EXPERTS)),
        ],
        out_specs=[
            pl.BlockSpec((N_EXPERTS, tm), lambda i: (0, i)),
            pl.BlockSpec((N_EXPERTS, tm), lambda i: (0, i)),
            pl.BlockSpec((TOP_K, tm), lambda i: (0, i)),
            pl.BlockSpec((tm // WIN, N_EXPERTS, 1), lambda i: (i, 0, 0)),
        ],
        out_shape=[
            jax.ShapeDtypeStruct((N_EXPERTS, t), BF16),
            jax.ShapeDtypeStruct((N_EXPERTS, t), BF16),
            jax.ShapeDtypeStruct((TOP_K, t), jnp.int32),
            jax.ShapeDtypeStruct((t // WIN, N_EXPERTS, 1), F32),
        ],
        compiler_params=pltpu.CompilerParams(
            dimension_semantics=("arbitrary",), vmem_limit_bytes=VMEM_LIMIT),
        name="router",
    )(h2, w_rt, b_r, utri, ltri)


def _sorted_rows_bound(t):
    rows = t * TOP_K + (t // WIN) * N_EXPERTS * 7 + N_EXPERTS * (EXP_BM - 1)
    blocks = -(-rows // EXP_BM)
    return (blocks + blocks % 2) * EXP_BM


def _dispatch_plan(cnt, t):
    nw = t // WIN
    n = cnt.reshape(nw, N_EXPERTS).astype(jnp.int32)
    run = (n + 7) // 8 * 8
    local_end = jnp.cumsum(run, axis=1)
    local_off = jnp.concatenate([jnp.zeros((nw, 1), jnp.int32), local_end], axis=1)
    total = jnp.sum(run, axis=0)
    region = (total + EXP_BM - 1) // EXP_BM * EXP_BM
    eid = jnp.arange(N_EXPERTS, dtype=jnp.int32)
    last_owner = jnp.max(jnp.where(region > 0, eid, 0))
    odd = (jnp.sum(region) // EXP_BM) % 2
    region = region + jnp.where(eid == last_owner, odd * EXP_BM, 0)
    region_end = jnp.cumsum(region)
    base = region_end - region
    global_off = base[None, :] + jnp.cumsum(run, axis=0) - run
    n_blocks = _sorted_rows_bound(t) // EXP_BM
    n_used = region_end[-1] // EXP_BM
    blk = jnp.arange(n_blocks, dtype=jnp.int32)
    blk_expert = jnp.sum((region_end[None, :] <= blk[:, None] * EXP_BM).astype(jnp.int32), axis=1)
    blk_expert = jnp.minimum(blk_expert, N_EXPERTS - 1)
    later_nonempty = (eid[None, :] > eid[:, None]) & (region[None, :] > 0)
    next_expert = jnp.min(jnp.where(later_nonempty, eid[None, :], N_EXPERTS), axis=1).astype(jnp.int32)
    return dict(
        run_lo=local_off[:, :N_EXPERTS].reshape(nw, N_EXPERTS, 1),
        run_hi=local_off[:, 1:].reshape(nw, N_EXPERTS, 1),
        local_off=local_off.reshape(-1), global_off=global_off.reshape(-1),
        fill_off=base + total, fill_cnt=region - total,
        blk_expert=blk_expert.astype(jnp.int32), next_expert=next_expert,
        n_used=n_used.reshape(1).astype(jnp.int32))


def _run_copy(local_ref, global_ref, win, e, vmem_buf, slot, hbm_buf, sem, to_hbm):
    lo = pl.multiple_of(local_ref[win * (N_EXPERTS + 1) + e], 8)
    cnt = pl.multiple_of(local_ref[win * (N_EXPERTS + 1) + e + 1] - lo, 8)
    go = pl.multiple_of(global_ref[win * N_EXPERTS + e], 8)
    v = vmem_buf.at[pl.ds(pl.multiple_of(slot * SEL_ROWS + lo, 8), cnt)]
    h = hbm_buf.at[pl.ds(go, cnt)]
    cp = pltpu.make_async_copy(v, h, sem.at[slot]) if to_hbm else pltpu.make_async_copy(h, v, sem.at[slot])
    return cnt, cp


def _start_runs(local_ref, global_ref, win, vmem_buf, slot, hbm_buf, sem, to_hbm):
    def body(e, carry):
        cnt, cp = _run_copy(local_ref, global_ref, win, e, vmem_buf, slot, hbm_buf, sem, to_hbm)

        @pl.when(cnt > 0)
        def _():
            cp.start()
        return carry
    lax.fori_loop(0, N_EXPERTS, body, 0)


def _wait_runs(local_ref, win, vmem_buf, slot, hbm_buf, sem, to_hbm):
    total = pl.multiple_of(local_ref[win * (N_EXPERTS + 1) + N_EXPERTS], 8)
    v = vmem_buf.at[pl.ds(pl.multiple_of(slot * SEL_ROWS, 8), total)]
    h = hbm_buf.at[pl.ds(0, total)]
    cp = pltpu.make_async_copy(v, h, sem.at[slot]) if to_hbm else pltpu.make_async_copy(h, v, sem.at[slot])

    @pl.when(total > 0)
    def _():
        cp.wait()


def _dispatch_kernel(local_ref, global_ref, fill_off_ref, fill_cnt_ref, h2_ref, pos_ref, xs_hbm, sbuf,
                     s_ref, sem, zsem, *, n_win):
    w = pl.program_id(0)
    slot = w % 2
    pos = pos_ref[...]

    h2 = h2_ref[...]
    assert SEL_RG <= 256
    rid_b = lax.broadcasted_iota(jnp.int32, (SEL_RG, WIN), 0).astype(F32).astype(BF16)
    one_b = jnp.ones((SEL_RG, WIN), BF16)
    for g in range(SEL_ROWS // SEL_MM):
        for sg in range(SEL_MM // SEL_RG):
            r0 = g * SEL_MM + sg * SEL_RG
            acc = jnp.zeros((SEL_RG, WIN), BF16)
            for k in range(TOP_K):
                off = (pos[k:k + 1, :] - r0).astype(F32)
                off = jnp.broadcast_to(off, (SEL_RG, WIN)).astype(BF16)
                acc = jnp.where(rid_b == off, one_b, acc)
            s_ref[r0:r0 + SEL_RG, :] = acc
        rows = slice(g * SEL_MM, (g + 1) * SEL_MM)
        dst = pl.multiple_of(slot * SEL_ROWS + g * SEL_MM, SEL_MM)
        sbuf[pl.ds(dst, SEL_MM), :] = _dot(s_ref[rows, :], h2).astype(BF16)

    _start_runs(local_ref, global_ref, w, sbuf, slot, xs_hbm, sem, True)

    @pl.when(w > 0)
    def _():
        _wait_runs(local_ref, w - 1, sbuf, 1 - slot, xs_hbm, sem, True)

    @pl.when(w == n_win - 1)
    def _():
        sbuf[2 * SEL_ROWS:, :] = jnp.zeros((2 * EXP_BM, D_MODEL), BF16)

        def fill(e, wait):
            cnt = pl.multiple_of(fill_cnt_ref[e], 8)
            off = pl.multiple_of(fill_off_ref[e], 8)
            cp = pltpu.make_async_copy(sbuf.at[pl.ds(2 * SEL_ROWS, cnt)], xs_hbm.at[pl.ds(off, cnt)], zsem)

            @pl.when(cnt > 0)
            def _():
                if wait:
                    cp.wait()
                else:
                    cp.start()

        def start_body(e, carry):
            fill(e, False)
            return carry

        def wait_body(e, carry):
            fill(e, True)
            return carry
        lax.fori_loop(0, N_EXPERTS, start_body, 0)
        _wait_runs(local_ref, w, sbuf, slot, xs_hbm, sem, True)
        lax.fori_loop(0, N_EXPERTS, wait_body, 0)


def _staging_shape(extra_rows):
    return jax.ShapeDtypeStruct((2 * SEL_ROWS + extra_rows, D_MODEL), BF16)


def _staging_spec(extra_rows):
    return pl.BlockSpec((2 * SEL_ROWS + extra_rows, D_MODEL), lambda w, *_: (0, 0))


def _dispatch(plan, h2, pos):
    t = h2.shape[0]
    n_win = t // WIN
    return pl.pallas_call(
        functools.partial(_dispatch_kernel, n_win=n_win),
        grid_spec=pltpu.PrefetchScalarGridSpec(
            num_scalar_prefetch=4,
            grid=(n_win,),
            in_specs=[
                pl.BlockSpec((WIN, D_MODEL), lambda w, *_: (w, 0)),
                pl.BlockSpec((TOP_K, WIN), lambda w, *_: (0, w)),
            ],
            out_specs=[pl.BlockSpec(memory_space=pl.ANY), _staging_spec(2 * EXP_BM)],
            scratch_shapes=[
                pltpu.VMEM((SEL_ROWS, WIN), BF16),
                pltpu.SemaphoreType.DMA((2,)),
                pltpu.SemaphoreType.DMA,
            ]),
        out_shape=[jax.ShapeDtypeStruct((_sorted_rows_bound(t), D_MODEL), BF16), _staging_shape(2 * EXP_BM)],
        compiler_params=pltpu.CompilerParams(
            dimension_semantics=("arbitrary",), vmem_limit_bytes=VMEM_LIMIT),
        name="dispatch",
    )(plan['local_off'], plan['global_off'], plan['fill_off'], plan['fill_cnt'], h2, pos)[0]


def _expert_kernel(blk_expert_ref, next_expert_ref, n_used_ref, xs_hbm, wg_hbm, wu_hbm, wd_hbm, ys_hbm,
                   xbuf, ybuf, wg_st, wu_st, wd_st, wg_bf, wu_bf, wd_bf, xsem, ysem, wsem):
    n_used = n_used_ref[0]
    part = EXP_BM // EXP_SPLIT

    def row_copies(b, slot, fetch):
        out = []
        for q in range(EXP_SPLIT):
            hbm_rows = pl.ds(pl.multiple_of(b * EXP_BM + q * part, part), part)
            if fetch:
                out.append(pltpu.make_async_copy(xs_hbm.at[hbm_rows], xbuf.at[slot, q * part:(q + 1) * part],
                                                 xsem.at[slot]))
            else:
                out.append(pltpu.make_async_copy(ybuf.at[slot, q * part:(q + 1) * part], ys_hbm.at[hbm_rows],
                                                 ysem.at[slot]))
        return out

    def weight_copies(e, slot):
        return [pltpu.make_async_copy(wg_hbm.at[e], wg_st.at[slot], wsem.at[slot]),
                pltpu.make_async_copy(wu_hbm.at[e], wu_st.at[slot], wsem.at[slot]),
                pltpu.make_async_copy(wd_hbm.at[e], wd_st.at[slot], wsem.at[slot])]

    def start(copies):
        for c in copies:
            c.start()

    def wait(copies):
        for c in copies:
            c.wait()

    for ahead in range(EXP_XDEPTH - 2):
        @pl.when(ahead < n_used)
        def _(ahead=ahead):
            start(row_copies(ahead, ahead, True))

    @pl.when(n_used > 0)
    def _():
        start(weight_copies(blk_expert_ref[0], 0))

    def enter_block(b, wset):
        e = blk_expert_ref[b]
        new_expert = jnp.logical_or(b == 0, e != blk_expert_ref[jnp.maximum(b - 1, 0)])
        wset = jnp.where(new_expert, 1 - wset, wset)

        @pl.when(new_expert)
        def _():
            wait(weight_copies(e, wset))
            wg_bf[wset] = wg_st[wset].astype(BF16)
            wu_bf[wset] = wu_st[wset].astype(BF16)
            wd_bf[wset] = wd_st[wset].astype(BF16)
            nxt = next_expert_ref[e]

            @pl.when(nxt < N_EXPERTS)
            def _():
                start(weight_copies(nxt, 1 - wset))

        ahead = b + EXP_XDEPTH - 2

        @pl.when(ahead < n_used)
        def _():
            start(row_copies(ahead, ahead % EXP_XDEPTH, True))

        wait(row_copies(b, b % EXP_XDEPTH, True))

        @pl.when(b >= EXP_YDEPTH)
        def _():
            wait(row_copies(b - EXP_YDEPTH, b % EXP_YDEPTH, False))
        return wset

    def compute(b, wset):
        x = xbuf[b % EXP_XDEPTH]
        hg = _dot(x, wg_bf[wset])
        hb = hg * jax.nn.sigmoid(hg) * _dot(x, wu_bf[wset])
        ybuf[b % EXP_YDEPTH] = _dot(hb.astype(BF16), wd_bf[wset]).astype(BF16)

    def body(p, wset):
        b0 = 2 * p
        w0 = enter_block(b0, wset)
        w1 = enter_block(b0 + 1, w0)
        compute(b0, w0)
        compute(b0 + 1, w1)
        start(row_copies(b0, b0 % EXP_YDEPTH, False))
        start(row_copies(b0 + 1, (b0 + 1) % EXP_YDEPTH, False))
        return w1

    lax.fori_loop(0, n_used // 2, body, jnp.int32(1))

    for back in range(EXP_YDEPTH, 0, -1):
        @pl.when(n_used >= back)
        def _(back=back):
            wait(row_copies(n_used - back, (n_used - back) % EXP_YDEPTH, False))


def _experts(plan, xs, w_gate, w_up, w_down):
    any_spec = pl.BlockSpec(memory_space=pl.ANY)
    return pl.pallas_call(
        _expert_kernel,
        grid_spec=pltpu.PrefetchScalarGridSpec(
            num_scalar_prefetch=3,
            grid=(1,),
            in_specs=[any_spec, any_spec, any_spec, any_spec],
            out_specs=any_spec,
            scratch_shapes=[
                pltpu.VMEM((EXP_XDEPTH, EXP_BM, D_MODEL), BF16),
                pltpu.VMEM((EXP_YDEPTH, EXP_BM, D_MODEL), BF16),
                pltpu.VMEM((2, D_MODEL, D_EXPERT), F32),
                pltpu.VMEM((2, D_MODEL, D_EXPERT), F32),
                pltpu.VMEM((2, D_EXPERT, D_MODEL), F32),
                pltpu.VMEM((2, D_MODEL, D_EXPERT), BF16),
                pltpu.VMEM((2, D_MODEL, D_EXPERT), BF16),
                pltpu.VMEM((2, D_EXPERT, D_MODEL), BF16),
                pltpu.SemaphoreType.DMA((EXP_XDEPTH,)),
                pltpu.SemaphoreType.DMA((EXP_YDEPTH,)),
                pltpu.SemaphoreType.DMA((2,)),
            ]),
        out_shape=jax.ShapeDtypeStruct(xs.shape, BF16),
        compiler_params=pltpu.CompilerParams(
            dimension_semantics=("arbitrary",), vmem_limit_bytes=VMEM_LIMIT),
        name="experts",
    )(plan['blk_expert'], plan['next_expert'], plan['n_used'], xs, w_gate, w_up, w_down)


def _combine_kernel(local_ref, global_ref, x1_ref, h2_ref, p_ref, rank_ref, gate_ref, lo_ref, hi_ref,
                    wsg_ref, wsu_ref, wsd_ref, gple_ref, wpg_ref, wp_ref, gfin_ref, ys_hbm, o_ref, ybuf, st_ref, sem,
                    *, n_win, final_norm):
    w = pl.program_id(0)
    slot = w % 2

    @pl.when(w == 0)
    def _():
        ybuf[...] = jnp.zeros(ybuf.shape, BF16)
        _start_runs(local_ref, global_ref, w, ybuf, slot, ys_hbm, sem, False)

    @pl.when(w + 1 < n_win)
    def _():
        _start_runs(local_ref, global_ref, w + 1, ybuf, 1 - slot, ys_hbm, sem, False)

    lo = lo_ref[0]
    hi = hi_ref[0]
    lo_f = lo.astype(F32)
    rank_tbl = rank_ref[...]
    gate_tbl = gate_ref[...]

    def build_group(lg):
        cols = slice(lg * CMB_LG, (lg + 1) * CMB_LG)
        rid = lg * CMB_LG + lax.broadcasted_iota(jnp.int32, (N_EXPERTS, CMB_LG), 1)
        owner = jnp.where(rid >= lo, jnp.where(rid < hi, 1.0, 0.0), 0.0)
        run_row = rid[0:1, :].astype(F32) - jnp.sum(owner * lo_f, axis=0, keepdims=True)
        owner = owner.astype(BF16)
        hit = _dot_t(rank_tbl, owner) == run_row
        st_ref[:, cols] = jnp.where(hit, _dot_t(gate_tbl, owner), 0.0).astype(BF16)

    build_group(0)
    h2 = h2_ref[...]
    hs = _dot(h2, wsg_ref[...])
    hs = hs * jax.nn.sigmoid(hs) * _dot(h2, wsu_ref[...])
    shared = _dot(hs.astype(BF16), wsd_ref[...])

    _wait_runs(local_ref, w, ybuf, slot, ys_hbm, sem, False)
    routed = None
    n_groups = SEL_ROWS // CMB_LG
    for lg in range(n_groups):
        if lg + 1 < n_groups:
            build_group(lg + 1)
        src = pl.multiple_of(slot * SEL_ROWS + lg * CMB_LG, CMB_LG)
        part = _dot(st_ref[:, lg * CMB_LG:(lg + 1) * CMB_LG], ybuf[pl.ds(src, CMB_LG), :])
        routed = part if routed is None else routed + part
    x2 = x1_ref[...] + routed + shared

    hp = _rms(x2, gple_ref[...]).astype(BF16)
    gate = jax.nn.sigmoid(_dot(hp, wpg_ref[...]))
    x3 = x2 + gate * _dot(p_ref[...].astype(BF16), wp_ref[...])
    o_ref[...] = _rms(x3, gfin_ref[...]) if final_norm else x3


def _combine(plan, ys, x1, h2, p, rank_tbl, gate_tbl, wsg, wsu, wsd, g_ple, w_pg, w_p, g_fin, final_norm):
    t = x1.shape[0]
    n_win = t // WIN
    row = lambda width: pl.BlockSpec((WIN, width), lambda w, *_: (w, 0))
    const = lambda shape: pl.BlockSpec(shape, lambda w, *_: (0,) * len(shape))
    return pl.pallas_call(
        functools.partial(_combine_kernel, n_win=n_win, final_norm=final_norm),
        grid_spec=pltpu.PrefetchScalarGridSpec(
            num_scalar_prefetch=2,
            grid=(n_win,),
            in_specs=[
                row(D_MODEL), row(D_MODEL), row(PLE_DIM),
                pl.BlockSpec((N_EXPERTS, WIN), lambda w, *_: (0, w)),
                pl.BlockSpec((N_EXPERTS, WIN), lambda w, *_: (0, w)),
                pl.BlockSpec((1, N_EXPERTS, 1), lambda w, *_: (w, 0, 0)),
                pl.BlockSpec((1, N_EXPERTS, 1), lambda w, *_: (w, 0, 0)),
                const((D_MODEL, D_EXPERT)), const((D_MODEL, D_EXPERT)), const((D_EXPERT, D_MODEL)),
                const((1, D_MODEL)), const((D_MODEL, D_MODEL)), const((PLE_DIM, D_MODEL)),
                const((1, D_MODEL)),
                pl.BlockSpec(memory_space=pl.ANY),
            ],
            out_specs=[row(D_MODEL), _staging_spec(0)],
            scratch_shapes=[
                pltpu.VMEM((WIN, SEL_ROWS), BF16),
                pltpu.SemaphoreType.DMA((2,)),
            ]),
        out_shape=[jax.ShapeDtypeStruct((t, D_MODEL), F32), _staging_shape(0)],
        compiler_params=pltpu.CompilerParams(
            dimension_semantics=("arbitrary",), vmem_limit_bytes=VMEM_LIMIT),
        name="combine",
    )(plan['local_off'], plan['global_off'], x1, h2, p, rank_tbl, gate_tbl, plan['run_lo'], plan['run_hi'],
      wsg, wsu, wsd, g_ple, w_pg, w_p, g_fin, ys)[0]


def kernel(x, p, g_mix, w_in, b_in, w_dw, b_dw, g_cln, b_cln, w_conv_out, b_conv_out, w_pool, s_pool,
           w_out, g_ffn, w_router, b_router, w_e_gate, w_e_up, w_e_down, w_s_gate, w_s_up, w_s_down,
           g_ple, w_ple_gate, w_ple, g_final):
    bsz, s, d = x.shape
    t = bsz * s
    depth = w_in.shape[0]
    xt = x.reshape(t, d)
    row = lambda v: v.reshape(1, -1)
    for i in range(depth):
        x1, h2 = _mixer(
            xt, s, row(g_mix[i]), w_in[i].astype(BF16), row(b_in[i]), w_dw[i], row(b_dw[i]),
            row(g_cln[i]), row(b_cln[i]), w_conv_out[i].astype(BF16), row(b_conv_out[i]),
            w_pool[i].astype(BF16), row(s_pool[i]), w_out[i].astype(BF16), row(g_ffn[i]))
        gate, rank, pos, cnt = _router(h2, w_router[i].T.astype(BF16), b_router[i].reshape(N_EXPERTS, 1))
        plan = _dispatch_plan(cnt, t)
        xs = _dispatch(plan, h2, pos)
        ys = _experts(plan, xs, w_e_gate[i], w_e_up[i], w_e_down[i])
        xt = _combine(
            plan, ys, x1, h2, p[i].reshape(t, PLE_DIM), rank, gate,
            w_s_gate[i].astype(BF16), w_s_up[i].astype(BF16), w_s_down[i].astype(BF16),
            row(g_ple[i]), w_ple_gate[i].astype(BF16), w_ple[i].astype(BF16), row(g_final),
            final_norm=(i == depth - 1))
    return xt.reshape(bsz, s, d)
```

```python
import functools

import jax
import jax.numpy as jnp
from jax import lax
from jax.experimental import pallas as pl
from jax.experimental.pallas import tpu as pltpu

D_MODEL = 1024
D_CONV = 1024
D_POOL = 1024
CONV_WIDTH = 31
POOL_WINDOWS = (2, 4, 8, 16)
POOL_GROUP = 256
PLE_DIM = 256
N_EXPERTS = 64
N_GROUPS = 8
GROUP_SIZE = N_EXPERTS // N_GROUPS
TOPK_GROUPS = 4
TOP_K = 8
D_EXPERT = 256
ROUTED_SCALE = 2.5
NORM_EPS = 1e-6

F32 = jnp.float32
BF16 = jnp.bfloat16

MIX_TM = 512
MIX_NV = MIX_TM // 8
CONV_MG = 8
ROW_CHUNK = 64
LANE = 128

ROUTER_TM = 1024
WIN = 256
SEL_ROWS = 2560
SEL_RG = 64
SEL_MM = 512
SEL_TAIL = 256
EXP_BM = 576
EXP_XDEPTH = 6
EXP_YDEPTH = 4
EXP_SPLIT = 4
RUN_FIELDS = 4
CMB_LG = 512

VMEM_LIMIT = 56 * 1024 * 1024


def _rms(x, g):
    ms = jnp.mean(x * x, axis=-1, keepdims=True)
    return x * lax.rsqrt(ms + NORM_EPS) * g


def _dot(a, b):
    return jnp.dot(a, b, preferred_element_type=F32)


def _dot_t(a, b):
    return lax.dot_general(a, b, (((0,), (0,)), ((), ())), preferred_element_type=F32)


def _mixer_kernel(x_ref, gmix_ref, win_ref, bin_ref, wdw_ref, bdw_ref, gcln_ref, bcln_ref,
                  wco_ref, bco_ref, wpool_ref, spool_ref, wout_ref, gffn_ref, perm_ref, unperm_ref,
                  x1_ref, h2_ref, a_ext, a_prev, u_ext, u_prev, c_buf, q_buf, *, tiles_per_seq):
    i = pl.program_id(0) % tiles_per_seq
    tm = MIX_TM
    nv = MIX_NV

    @pl.when(i == 0)
    def _():
        a_prev[...] = jnp.zeros(a_prev.shape, F32)
        u_prev[...] = jnp.zeros(u_prev.shape, F32)

    x = x_ref[...]
    h = _dot(perm_ref[...], _rms(x, gmix_ref[...]).astype(BF16)).astype(BF16)

    def proj(lo, hi):
        return _dot(h, win_ref[:, lo:hi]) + bin_ref[:, lo:hi]

    glu = proj(0, D_CONV) * jax.nn.sigmoid(proj(D_CONV, 2 * D_CONV))
    for lc in range(D_CONV // LANE):
        a_ext[lc, tm:2 * tm, :] = glu[:, lc * LANE:(lc + 1) * LANE]
    u_ext[tm:2 * tm, :] = proj(2 * D_CONV, 2 * D_CONV + D_POOL)

    def delayed_groups(ext, prev, first_group):
        last_row = lax.broadcasted_iota(jnp.int32, (8, ext.shape[-1]), 0) == 7
        for g in range(first_group, nv):
            rows = slice(8 * g, 8 * g + 8)
            mixed = jnp.where(last_row, prev[rows, :], ext[tm + 8 * g:tm + 8 * g + 8, :])
            ext[rows, :] = pltpu.roll(mixed, 1, axis=0)
            prev[rows, :] = ext[tm + 8 * g:tm + 8 * g + 8, :]

    delayed_groups(u_ext, u_prev, nv - (max(POOL_WINDOWS) - 1))

    def conv_column(lc, carry):
        a_col = a_ext.at[lc]
        delayed_groups(a_col, a_prev.at[lc], nv - (CONV_WIDTH - 1))
        w_col = wdw_ref.at[lc]
        for g0 in range(0, nv, CONV_MG):
            acc = None
            for k in range(CONV_WIDTH):
                src = nv + g0 + k - (CONV_WIDTH - 1)
                term = a_col[8 * src:8 * (src + CONV_MG), :] * w_col[k:k + 1, :]
                acc = term if acc is None else acc + term
            c_buf[lc, 8 * g0:8 * (g0 + CONV_MG), :] = acc + bdw_ref[lc]
        return carry
    lax.fori_loop(0, D_CONV // LANE, conv_column, 0)

    c = jnp.concatenate([c_buf[lc] for lc in range(D_CONV // LANE)], axis=-1)
    mu = jnp.mean(c, axis=-1, keepdims=True)
    xc = c - mu
    var = jnp.mean(xc * xc, axis=-1, keepdims=True)
    y = xc * lax.rsqrt(var + NORM_EPS) * gcln_ref[...] + bcln_ref[...]
    y = y * jax.nn.sigmoid(y)
    branch_a = _dot(y.astype(BF16), wco_ref[...]) + bco_ref[...]

    for r0 in range(0, tm, ROW_CHUNK):
        row = r0 + lax.broadcasted_iota(jnp.int32, (ROW_CHUNK, POOL_GROUP), 0)
        t1 = i * tm + (row % 8) * nv + row // 8 + 1
        for gi, w in enumerate(POOL_WINDOWS):
            ls = slice(gi * POOL_GROUP, (gi + 1) * POOL_GROUP)
            tok = u_ext[tm + r0:tm + r0 + ROW_CHUNK, ls]
            s = tok
            for j in range(1, w):
                s = s + u_ext[tm + r0 - 8 * j:tm + r0 - 8 * j + ROW_CHUNK, ls]
            cnt = jnp.minimum(t1, w).astype(F32)
            q_buf[r0:r0 + ROW_CHUNK, ls] = s / cnt - tok

    qs_out = []
    for gi in range(len(POOL_WINDOWS)):
        ls = slice(gi * POOL_GROUP, (gi + 1) * POOL_GROUP)
        qs_out.append(_dot(q_buf[:, ls].astype(BF16), wpool_ref[gi]) * spool_ref[:, ls])
    branch_b = jnp.concatenate(qs_out, axis=-1)

    c2 = 2 * D_CONV + D_POOL
    gate_a = jax.nn.sigmoid(proj(c2, c2 + D_MODEL))
    gate_b = jax.nn.sigmoid(proj(c2 + D_MODEL, c2 + 2 * D_MODEL))
    merged = gate_a * branch_a + gate_b * branch_b
    merged = _dot(unperm_ref[...], merged.astype(BF16)).astype(BF16)
    x1 = x + _dot(merged, wout_ref[...])
    x1_ref[...] = x1
    h2_ref[...] = _rms(x1, gffn_ref[...]).astype(BF16)


def _const_spec(shape):
    n = len(shape)
    return pl.BlockSpec(shape, lambda i, _n=n: (0,) * _n, pipeline_mode=pl.Buffered(1))


def _mixer(x, seq_len, g_mix, w_in, b_in, w_dw, b_dw, g_cln, b_cln, w_co, b_co, w_pool, s_pool, w_out,
           g_ffn):
    t = x.shape[0]
    tm = MIX_TM
    assert seq_len % tm == 0 and MIX_NV >= CONV_WIDTH and MIX_NV >= max(POOL_WINDOWS)
    d_in = w_in.shape[1]
    row = pl.BlockSpec((tm, D_MODEL), lambda i: (i, 0))
    n_col = D_CONV // LANE
    w_dw = w_dw.reshape(CONV_WIDTH, n_col, LANE).transpose(1, 0, 2)
    b_dw = b_dw.reshape(n_col, 1, LANE)
    r = jnp.arange(tm)
    perm = ((r % 8) * MIX_NV + r // 8)[:, None] == jnp.arange(tm)[None, :]
    perm = perm.astype(BF16)
    return pl.pallas_call(
        functools.partial(_mixer_kernel, tiles_per_seq=seq_len // tm),
        grid=(t // tm,),
        in_specs=[
            row,
            _const_spec((1, D_MODEL)),
            _const_spec((D_MODEL, d_in)),
            _const_spec((1, d_in)),
            _const_spec((n_col, CONV_WIDTH, LANE)),
            _const_spec((n_col, 1, LANE)),
            _const_spec((1, D_CONV)),
            _const_spec((1, D_CONV)),
            _const_spec((D_CONV, D_MODEL)),
            _const_spec((1, D_MODEL)),
            _const_spec((len(POOL_WINDOWS), POOL_GROUP, POOL_GROUP)),
            _const_spec((1, D_POOL)),
            _const_spec((D_MODEL, D_MODEL)),
            _const_spec((1, D_MODEL)),
            _const_spec((tm, tm)),
            _const_spec((tm, tm)),
        ],
        out_specs=[row, row],
        out_shape=[jax.ShapeDtypeStruct((t, D_MODEL), F32),
                   jax.ShapeDtypeStruct((t, D_MODEL), BF16)],
        scratch_shapes=[
            pltpu.VMEM((n_col, 2 * tm, LANE), F32),
            pltpu.VMEM((n_col, tm, LANE), F32),
            pltpu.VMEM((2 * tm, D_POOL), F32),
            pltpu.VMEM((tm, D_POOL), F32),
            pltpu.VMEM((n_col, tm, LANE), F32),
            pltpu.VMEM((tm, D_POOL), F32),
        ],
        compiler_params=pltpu.CompilerParams(
            dimension_semantics=("arbitrary",), vmem_limit_bytes=VMEM_LIMIT),
        name="mixer",
    )(x, g_mix, w_in, b_in, w_dw, b_dw, g_cln, b_cln, w_co, b_co, w_pool, s_pool, w_out, g_ffn, perm, perm.T)


def _beats(v, other, other_is_later):
    v = jnp.broadcast_to(v, other.shape)
    return jnp.where(other_is_later, jnp.where(v >= other, 1, 0), jnp.where(v > other, 1, 0))


def _router_kernel(h2_ref, wrt_ref, br_ref, utri_ref, ltri_ref, gate_ref, rank_ref, pos_ref, cnt_ref):
    tm = ROUTER_TM
    logits = lax.dot_general(wrt_ref[...], h2_ref[...], (((1,), (1,)), ((), ())),
                             preferred_element_type=F32)
    scores = jax.nn.sigmoid(logits)
    sel = scores + br_ref[...]
    shape3 = (N_GROUPS, GROUP_SIZE, tm)
    sel3 = sel.reshape(shape3)
    scores3 = scores.reshape(shape3)
    neg_inf = jnp.float32(-jnp.inf)

    member = lax.broadcasted_iota(jnp.int32, shape3, 1)
    m1 = jnp.max(sel3, axis=1, keepdims=True)
    first = jnp.min(jnp.where(sel3 == m1, member, GROUP_SIZE), axis=1, keepdims=True)
    m2 = jnp.max(jnp.where(member == first, neg_inf, sel3), axis=1, keepdims=True)
    gscore = jnp.broadcast_to(m1 + m2, shape3)

    gidx = lax.broadcasted_iota(jnp.int32, shape3, 0)
    grank = jnp.zeros(shape3, jnp.int32)
    for j in range(N_GROUPS):
        sj = gscore[j:j + 1]
        grank = grank + _beats(sj, gscore, gidx > j)
    masked = jnp.where(grank < TOPK_GROUPS, sel3, neg_inf)

    eidx = gidx * GROUP_SIZE + member
    work = masked
    erank = jnp.full(shape3, TOP_K, jnp.int32)
    for k in range(TOP_K):
        best = jnp.max(jnp.max(work, axis=0, keepdims=True), axis=1, keepdims=True)
        cand = jnp.where(work == best, eidx, N_EXPERTS)
        pick = jnp.min(jnp.min(cand, axis=0, keepdims=True), axis=1, keepdims=True)
        hit = eidx == pick
        work = jnp.where(hit, neg_inf, work)
        erank = jnp.where(hit, k, erank)
    chosen = erank < TOP_K
    top_s = jnp.where(chosen, scores3, 0.0)
    denom = jnp.sum(jnp.sum(top_s, axis=0, keepdims=True), axis=1, keepdims=True)
    gates3 = top_s / denom * ROUTED_SCALE
    chosen2 = jnp.where(chosen, 1.0, 0.0).reshape(N_EXPERTS, tm)
    gate_ref[...] = gates3.reshape(N_EXPERTS, tm).astype(BF16)

    for w in range(tm // WIN):
        ls = slice(w * WIN, (w + 1) * WIN)
        mw = chosen2[:, ls]
        rank = _dot(mw.astype(BF16), utri_ref[...])
        n = jnp.sum(mw, axis=1, keepdims=True)
        run = jnp.floor((n + 7.0) * 0.125) * 8.0
        start = _dot(ltri_ref[...], jnp.broadcast_to(run, (N_EXPERTS, WIN)).astype(BF16))
        rank_ref[:, ls] = jnp.where(mw > 0.5, rank, -1.0).astype(BF16)
        row3 = (rank + start).reshape(N_GROUPS, GROUP_SIZE, WIN)
        er = erank[:, :, ls]
        for k in range(TOP_K):
            pk = jnp.sum(jnp.sum(jnp.where(er == k, row3, 0.0), axis=0, keepdims=True), axis=1, keepdims=True)
            pos_ref[k:k + 1, ls] = pk.reshape(1, WIN).astype(jnp.int32)
        cnt_ref[w] = n


def _router(h2, w_rt, b_r):
    t = h2.shape[0]
    tm = ROUTER_TM
    utri = jnp.triu(jnp.ones((WIN, WIN), BF16), k=1)
    ltri = jnp.tril(jnp.ones((N_EXPERTS, N_EXPERTS), BF16), k=-1)
    return pl.pallas_call(
        _router_kernel,
        grid=(t // tm,),
        in_specs=[
            pl.BlockSpec((tm, D_MODEL), lambda i: (i, 0)),
            _const_spec((N_EXPERTS, D_MODEL)),
            _const_spec((N_EXPERTS, 1)),
            _const_spec((WIN, WIN)),
            _const_spec((N_EXPERTS, N_EXPERTS)),
        ],
        out_specs=[
            pl.BlockSpec((N_EXPERTS, tm), lambda i: (0, i)),
            pl.BlockSpec((N_EXPERTS, tm), lambda i: (0, i)),
            pl.BlockSpec((TOP_K, tm), lambda i: (0, i)),
            pl.BlockSpec((tm // WIN, N_EXPERTS, 1), lambda i: (i, 0, 0)),
        ],
        out_shape=[
            jax.ShapeDtypeStruct((N_EXPERTS, t), BF16),
            jax.ShapeDtypeStruct((N_EXPERTS, t), BF16),
            jax.ShapeDtypeStruct((TOP_K, t), jnp.int32),
            jax.ShapeDtypeStruct((t // WIN, N_EXPERTS, 1), F32),
        ],
        compiler_params=pltpu.CompilerParams(
            dimension_semantics=("arbitrary",), vmem_limit_bytes=VMEM_LIMIT),
        name="router",
    )(h2, w_rt, b_r, utri, ltri)


def _sorted_rows_bound(t):
    rows = t * TOP_K + (t // WIN) * N_EXPERTS * 7 + N_EXPERTS * (EXP_BM - 1)
    blocks = -(-rows // EXP_BM)
    return (blocks + blocks % 2) * EXP_BM


def _dispatch_plan(cnt, t):
    nw = t // WIN
    n = cnt.reshape(nw, N_EXPERTS).astype(jnp.int32)
    run = (n + 7) // 8 * 8
    local_end = jnp.cumsum(run, axis=1)
    local_off = jnp.concatenate([jnp.zeros((nw, 1), jnp.int32), local_end], axis=1)
    total = jnp.sum(run, axis=0)
    region = (total + EXP_BM - 1) // EXP_BM * EXP_BM
    eid = jnp.arange(N_EXPERTS, dtype=jnp.int32)
    last_owner = jnp.max(jnp.where(region > 0, eid, 0))
    odd = (jnp.sum(region) // EXP_BM) % 2
    region = region + jnp.where(eid == last_owner, odd * EXP_BM, 0)
    region_end = jnp.cumsum(region)
    base = region_end - region
    global_off = base[None, :] + jnp.cumsum(run, axis=0) - run
    n_blocks = _sorted_rows_bound(t) // EXP_BM
    n_used = region_end[-1] // EXP_BM
    blk = jnp.arange(n_blocks, dtype=jnp.int32)
    blk_expert = jnp.sum((region_end[None, :] <= blk[:, None] * EXP_BM).astype(jnp.int32), axis=1)
    blk_expert = jnp.minimum(blk_expert, N_EXPERTS - 1)
    later_nonempty = (eid[None, :] > eid[:, None]) & (region[None, :] > 0)
    next_expert = jnp.min(jnp.where(later_nonempty, eid[None, :], N_EXPERTS), axis=1).astype(jnp.int32)
    return dict(
        run_lo=local_off[:, :N_EXPERTS].reshape(nw, N_EXPERTS, 1),
        run_hi=local_off[:, 1:].reshape(nw, N_EXPERTS, 1),
        local_off=local_off.reshape(-1),
        runs=jnp.stack([local_off[:, :N_EXPERTS], run, global_off, jnp.zeros_like(run)], axis=-1).reshape(-1),
        fill_off=base + total, fill_cnt=region - total,
        blk_expert=blk_expert.astype(jnp.int32), next_expert=next_expert,
        n_used=n_used.reshape(1).astype(jnp.int32))


def _run_copy(runs_ref, win, e, vmem_buf, slot, hbm_buf, sem, to_hbm):
    p = (win * N_EXPERTS + e) * RUN_FIELDS
    lo = pl.multiple_of(runs_ref[p], 8)
    cnt = pl.multiple_of(runs_ref[p + 1], 8)
    go = pl.multiple_of(runs_ref[p + 2], 8)
    v = vmem_buf.at[pl.ds(pl.multiple_of(slot * SEL_ROWS + lo, 8), cnt)]
    h = hbm_buf.at[pl.ds(go, cnt)]
    cp = pltpu.make_async_copy(v, h, sem.at[slot]) if to_hbm else pltpu.make_async_copy(h, v, sem.at[slot])
    return cnt, cp


def _start_runs(runs_ref, win, vmem_buf, slot, hbm_buf, sem, to_hbm):
    def body(e, carry):
        cnt, cp = _run_copy(runs_ref, win, e, vmem_buf, slot, hbm_buf, sem, to_hbm)

        @pl.when(cnt > 0)
        def _():
            cp.start()
        return carry
    lax.fori_loop(0, N_EXPERTS, body, 0)


def _wait_runs(local_ref, win, vmem_buf, slot, hbm_buf, sem, to_hbm):
    total = pl.multiple_of(local_ref[win * (N_EXPERTS + 1) + N_EXPERTS], 8)
    v = vmem_buf.at[pl.ds(pl.multiple_of(slot * SEL_ROWS, 8), total)]
    h = hbm_buf.at[pl.ds(0, total)]
    cp = pltpu.make_async_copy(v, h, sem.at[slot]) if to_hbm else pltpu.make_async_copy(h, v, sem.at[slot])

    @pl.when(total > 0)
    def _():
        cp.wait()


def _dispatch_kernel(local_ref, runs_ref, fill_off_ref, fill_cnt_ref, h2_ref, pos_ref, xs_hbm, sbuf,
                     s_ref, sem, zsem, *, n_win):
    w = pl.program_id(0)
    slot = w % 2
    pos = pos_ref[...]

    h2 = h2_ref[...]
    assert SEL_RG <= 256
    rid_b = lax.broadcasted_iota(jnp.int32, (SEL_RG, WIN), 0).astype(F32).astype(BF16)
    one_b = jnp.ones((SEL_RG, WIN), BF16)
    def compact(first, n_rows):
        for r0 in range(first, first + n_rows, SEL_RG):
            acc = jnp.zeros((SEL_RG, WIN), BF16)
            for k in range(TOP_K):
                off = (pos[k:k + 1, :] - r0).astype(F32)
                off = jnp.broadcast_to(off, (SEL_RG, WIN)).astype(BF16)
                acc = jnp.where(rid_b == off, one_b, acc)
            s_ref[r0:r0 + SEL_RG, :] = acc
        dst = pl.multiple_of(slot * SEL_ROWS + first, SEL_TAIL)
        sbuf[pl.ds(dst, n_rows), :] = _dot(s_ref[first:first + n_rows, :], h2).astype(BF16)

    for first in range(0, SEL_ROWS - 2 * SEL_TAIL, SEL_MM):
        compact(first, SEL_MM)
    compact(SEL_ROWS - 2 * SEL_TAIL, SEL_TAIL)

    @pl.when(local_ref[w * (N_EXPERTS + 1) + N_EXPERTS] > SEL_ROWS - SEL_TAIL)
    def _():
        compact(SEL_ROWS - SEL_TAIL, SEL_TAIL)

    _start_runs(runs_ref, w, sbuf, slot, xs_hbm, sem, True)

    @pl.when(w > 0)
    def _():
        _wait_runs(local_ref, w - 1, sbuf, 1 - slot, xs_hbm, sem, True)

    @pl.when(w == n_win - 1)
    def _():
        sbuf[2 * SEL_ROWS:, :] = jnp.zeros((2 * EXP_BM, D_MODEL), BF16)

        def fill(e, wait):
            cnt = pl.multiple_of(fill_cnt_ref[e], 8)
            off = pl.multiple_of(fill_off_ref[e], 8)
            cp = pltpu.make_async_copy(sbuf.at[pl.ds(2 * SEL_ROWS, cnt)], xs_hbm.at[pl.ds(off, cnt)], zsem)

            @pl.when(cnt > 0)
            def _():
                if wait:
                    cp.wait()
                else:
                    cp.start()

        def start_body(e, carry):
            fill(e, False)
            return carry

        def wait_body(e, carry):
            fill(e, True)
            return carry
        lax.fori_loop(0, N_EXPERTS, start_body, 0)
        _wait_runs(local_ref, w, sbuf, slot, xs_hbm, sem, True)
        lax.fori_loop(0, N_EXPERTS, wait_body, 0)


def _staging_shape(extra_rows):
    return jax.ShapeDtypeStruct((2 * SEL_ROWS + extra_rows, D_MODEL), BF16)


def _staging_spec(extra_rows):
    return pl.BlockSpec((2 * SEL_ROWS + extra_rows, D_MODEL), lambda w, *_: (0, 0))


def _dispatch(plan, h2, pos):
    t = h2.shape[0]
    n_win = t // WIN
    return pl.pallas_call(
        functools.partial(_dispatch_kernel, n_win=n_win),
        grid_spec=pltpu.PrefetchScalarGridSpec(
            num_scalar_prefetch=4,
            grid=(n_win,),
            in_specs=[
                pl.BlockSpec((WIN, D_MODEL), lambda w, *_: (w, 0)),
                pl.BlockSpec((TOP_K, WIN), lambda w, *_: (0, w)),
            ],
            out_specs=[pl.BlockSpec(memory_space=pl.ANY), _staging_spec(2 * EXP_BM)],
            scratch_shapes=[
                pltpu.VMEM((SEL_ROWS, WIN), BF16),
                pltpu.SemaphoreType.DMA((2,)),
                pltpu.SemaphoreType.DMA,
            ]),
        out_shape=[jax.ShapeDtypeStruct((_sorted_rows_bound(t), D_MODEL), BF16), _staging_shape(2 * EXP_BM)],
        compiler_params=pltpu.CompilerParams(
            dimension_semantics=("arbitrary",), vmem_limit_bytes=VMEM_LIMIT),
        name="dispatch",
    )(plan['local_off'], plan['runs'], plan['fill_off'], plan['fill_cnt'], h2, pos)[0]


def _expert_kernel(blk_expert_ref, next_expert_ref, n_used_ref, xs_hbm, wg_hbm, wu_hbm, wd_hbm, ys_hbm,
                   xbuf, ybuf, wg_st, wu_st, wd_st, wg_bf, wu_bf, wd_bf, xsem, ysem, wsem):
    n_used = n_used_ref[0]
    part = EXP_BM // EXP_SPLIT

    def row_copies(b, slot, fetch):
        out = []
        for q in range(EXP_SPLIT):
            hbm_rows = pl.ds(pl.multiple_of(b * EXP_BM + q * part, part), part)
            if fetch:
                out.append(pltpu.make_async_copy(xs_hbm.at[hbm_rows], xbuf.at[slot, q * part:(q + 1) * part],
                                                 xsem.at[slot]))
            else:
                out.append(pltpu.make_async_copy(ybuf.at[slot, q * part:(q + 1) * part], ys_hbm.at[hbm_rows],
                                                 ysem.at[slot]))
        return out

    def weight_copies(e, slot):
        return [pltpu.make_async_copy(wg_hbm.at[e], wg_st.at[slot], wsem.at[slot]),
                pltpu.make_async_copy(wu_hbm.at[e], wu_st.at[slot], wsem.at[slot]),
                pltpu.make_async_copy(wd_hbm.at[e], wd_st.at[slot], wsem.at[slot])]

    def start(copies):
        for c in copies:
            c.start()

    def wait(copies):
        for c in copies:
            c.wait()

    for ahead in range(EXP_XDEPTH - 2):
        @pl.when(ahead < n_used)
        def _(ahead=ahead):
            start(row_copies(ahead, ahead, True))

    @pl.when(n_used > 0)
    def _():
        start(weight_copies(blk_expert_ref[0], 0))

    def enter_block(b, wset):
        e = blk_expert_ref[b]
        new_expert = jnp.logical_or(b == 0, e != blk_expert_ref[jnp.maximum(b - 1, 0)])
        wset = jnp.where(new_expert, 1 - wset, wset)

        @pl.when(new_expert)
        def _():
            wait(weight_copies(e, wset))
            wg_bf[wset] = wg_st[wset].astype(BF16)
            wu_bf[wset] = wu_st[wset].astype(BF16)
            wd_bf[wset] = wd_st[wset].astype(BF16)
            nxt = next_expert_ref[e]

            @pl.when(nxt < N_EXPERTS)
            def _():
                start(weight_copies(nxt, 1 - wset))

        ahead = b + EXP_XDEPTH - 2

        @pl.when(ahead < n_used)
        def _():
            start(row_copies(ahead, ahead % EXP_XDEPTH, True))

        wait(row_copies(b, b % EXP_XDEPTH, True))

        @pl.when(b >= EXP_YDEPTH)
        def _():
            wait(row_copies(b - EXP_YDEPTH, b % EXP_YDEPTH, False))
        return wset

    def compute(b, wset):
        x = xbuf[b % EXP_XDEPTH]
        hg = _dot(x, wg_bf[wset])
        hb = hg * jax.nn.sigmoid(hg) * _dot(x, wu_bf[wset])
        ybuf[b % EXP_YDEPTH] = _dot(hb.astype(BF16), wd_bf[wset]).astype(BF16)

    def body(p, wset):
        b0 = 2 * p
        w0 = enter_block(b0, wset)
        w1 = enter_block(b0 + 1, w0)
        compute(b0, w0)
        compute(b0 + 1, w1)
        start(row_copies(b0, b0 % EXP_YDEPTH, False))
        start(row_copies(b0 + 1, (b0 + 1) % EXP_YDEPTH, False))
        return w1

    lax.fori_loop(0, n_used // 2, body, jnp.int32(1))

    for back in range(EXP_YDEPTH, 0, -1):
        @pl.when(n_used >= back)
        def _(back=back):
            wait(row_copies(n_used - back, (n_used - back) % EXP_YDEPTH, False))


def _experts(plan, xs, w_gate, w_up, w_down):
    any_spec = pl.BlockSpec(memory_space=pl.ANY)
    return pl.pallas_call(
        _expert_kernel,
        grid_spec=pltpu.PrefetchScalarGridSpec(
            num_scalar_prefetch=3,
            grid=(1,),
            in_specs=[any_spec, any_spec, any_spec, any_spec],
            out_specs=any_spec,
            scratch_shapes=[
                pltpu.VMEM((EXP_XDEPTH, EXP_BM, D_MODEL), BF16),
                pltpu.VMEM((EXP_YDEPTH, EXP_BM, D_MODEL), BF16),
                pltpu.VMEM((2, D_MODEL, D_EXPERT), F32),
                pltpu.VMEM((2, D_MODEL, D_EXPERT), F32),
                pltpu.VMEM((2, D_EXPERT, D_MODEL), F32),
                pltpu.VMEM((2, D_MODEL, D_EXPERT), BF16),
                pltpu.VMEM((2, D_MODEL, D_EXPERT), BF16),
                pltpu.VMEM((2, D_EXPERT, D_MODEL), BF16),
                pltpu.SemaphoreType.DMA((EXP_XDEPTH,)),
                pltpu.SemaphoreType.DMA((EXP_YDEPTH,)),
                pltpu.SemaphoreType.DMA((2,)),
            ]),
        out_shape=jax.ShapeDtypeStruct(xs.shape, BF16),
        compiler_params=pltpu.CompilerParams(
            dimension_semantics=("arbitrary",), vmem_limit_bytes=VMEM_LIMIT),
        name="experts",
    )(plan['blk_expert'], plan['next_expert'], plan['n_used'], xs, w_gate, w_up, w_down)


def _combine_kernel(local_ref, runs_ref, x1_ref, h2_ref, p_ref, rank_ref, gate_ref, lo_ref, hi_ref,
                    wsg_ref, wsu_ref, wsd_ref, gple_ref, wpg_ref, wp_ref, gfin_ref, ys_hbm, o_ref, ybuf, st_ref, sem,
                    *, n_win, final_norm):
    w = pl.program_id(0)
    slot = w % 2

    @pl.when(w == 0)
    def _():
        ybuf[...] = jnp.zeros(ybuf.shape, BF16)
        _start_runs(runs_ref, w, ybuf, slot, ys_hbm, sem, False)

    @pl.when(w + 1 < n_win)
    def _():
        _start_runs(runs_ref, w + 1, ybuf, 1 - slot, ys_hbm, sem, False)

    lo = lo_ref[0]
    hi = hi_ref[0]
    lo_f = lo.astype(F32)
    rank_tbl = rank_ref[...]
    gate_tbl = gate_ref[...]

    def build_group(lg):
        cols = slice(lg * CMB_LG, (lg + 1) * CMB_LG)
        rid = lg * CMB_LG + lax.broadcasted_iota(jnp.int32, (N_EXPERTS, CMB_LG), 1)
        owner = jnp.where(rid >= lo, jnp.where(rid < hi, 1.0, 0.0), 0.0)
        run_row = rid[0:1, :].astype(F32) - jnp.sum(owner * lo_f, axis=0, keepdims=True)
        owner = owner.astype(BF16)
        hit = _dot_t(rank_tbl, owner) == run_row
        st_ref[:, cols] = jnp.where(hit, _dot_t(gate_tbl, owner), 0.0).astype(BF16)

    build_group(0)
    h2 = h2_ref[...]
    hs = _dot(h2, wsg_ref[...])
    hs = hs * jax.nn.sigmoid(hs) * _dot(h2, wsu_ref[...])
    shared = _dot(hs.astype(BF16), wsd_ref[...])

    _wait_runs(local_ref, w, ybuf, slot, ys_hbm, sem, False)
    routed = None
    n_groups = SEL_ROWS // CMB_LG
    for lg in range(n_groups):
        if lg + 1 < n_groups:
            build_group(lg + 1)
        src = pl.multiple_of(slot * SEL_ROWS + lg * CMB_LG, CMB_LG)
        part = _dot(st_ref[:, lg * CMB_LG:(lg + 1) * CMB_LG], ybuf[pl.ds(src, CMB_LG), :])
        routed = part if routed is None else routed + part
    x2 = x1_ref[...] + routed + shared

    hp = _rms(x2, gple_ref[...]).astype(BF16)
    gate = jax.nn.sigmoid(_dot(hp, wpg_ref[...]))
    x3 = x2 + gate * _dot(p_ref[...].astype(BF16), wp_ref[...])
    o_ref[...] = _rms(x3, gfin_ref[...]) if final_norm else x3


def _combine(plan, ys, x1, h2, p, rank_tbl, gate_tbl, wsg, wsu, wsd, g_ple, w_pg, w_p, g_fin, final_norm):
    t = x1.shape[0]
    n_win = t // WIN
    row = lambda width: pl.BlockSpec((WIN, width), lambda w, *_: (w, 0))
    const = lambda shape: pl.BlockSpec(shape, lambda w, *_: (0,) * len(shape))
    return pl.pallas_call(
        functools.partial(_combine_kernel, n_win=n_win, final_norm=final_norm),
        grid_spec=pltpu.PrefetchScalarGridSpec(
            num_scalar_prefetch=2,
            grid=(n_win,),
            in_specs=[
                row(D_MODEL), row(D_MODEL), row(PLE_DIM),
                pl.BlockSpec((N_EXPERTS, WIN), lambda w, *_: (0, w)),
                pl.BlockSpec((N_EXPERTS, WIN), lambda w, *_: (0, w)),
                pl.BlockSpec((1, N_EXPERTS, 1), lambda w, *_: (w, 0, 0)),
                pl.BlockSpec((1, N_EXPERTS, 1), lambda w, *_: (w, 0, 0)),
                const((D_MODEL, D_EXPERT)), const((D_MODEL, D_EXPERT)), const((D_EXPERT, D_MODEL)),
                const((1, D_MODEL)), const((D_MODEL, D_MODEL)), const((PLE_DIM, D_MODEL)),
                const((1, D_MODEL)),
                pl.BlockSpec(memory_space=pl.ANY),
            ],
            out_specs=[row(D_MODEL), _staging_spec(0)],
            scratch_shapes=[
                pltpu.VMEM((WIN, SEL_ROWS), BF16),
                pltpu.SemaphoreType.DMA((2,)),
            ]),
        out_shape=[jax.ShapeDtypeStruct((t, D_MODEL), F32), _staging_shape(0)],
        compiler_params=pltpu.CompilerParams(
            dimension_semantics=("arbitrary",), vmem_limit_bytes=VMEM_LIMIT),
        name="combine",
    )(plan['local_off'], plan['runs'], x1, h2, p, rank_tbl, gate_tbl, plan['run_lo'], plan['run_hi'],
      wsg, wsu, wsd, g_ple, w_pg, w_p, g_fin, ys)[0]


def kernel(x, p, g_mix, w_in, b_in, w_dw, b_dw, g_cln, b_cln, w_conv_out, b_conv_out, w_pool, s_pool,
           w_out, g_ffn, w_router, b_router, w_e_gate, w_e_up, w_e_down, w_s_gate, w_s_up, w_s_down,
           g_ple, w_ple_gate, w_ple, g_final):
    bsz, s, d = x.shape
    t = bsz * s
    depth = w_in.shape[0]
    xt = x.reshape(t, d)
    row = lambda v: v.reshape(1, -1)
    for i in range(depth):
        x1, h2 = _mixer(
            xt, s, row(g_mix[i]), w_in[i].astype(BF16), row(b_in[i]), w_dw[i], row(b_dw[i]),
            row(g_cln[i]), row(b_cln[i]), w_conv_out[i].astype(BF16), row(b_conv_out[i]),
            w_pool[i].astype(BF16), row(s_pool[i]), w_out[i].astype(BF16), row(g_ffn[i]))
        gate, rank, pos, cnt = _router(h2, w_router[i].T.astype(BF16), b_router[i].reshape(N_EXPERTS, 1))
        plan = _dispatch_plan(cnt, t)
        xs = _dispatch(plan, h2, pos)
        ys = _experts(plan, xs, w_e_gate[i], w_e_up[i], w_e_down[i])
        xt = _combine(
            plan, ys, x1, h2, p[i].reshape(t, PLE_DIM), rank, gate,
            w_s_gate[i].astype(BF16), w_s_up[i].astype(BF16), w_s_down[i].astype(BF16),
            row(g_ple[i]), w_ple_gate[i].astype(BF16), w_ple[i].astype(BF16), row(g_final),
            final_norm=(i == depth - 1))
    return xt.reshape(bsz, s, d)
```

```python
import functools

import jax
import jax.numpy as jnp
from jax import lax
from jax.experimental import pallas as pl
from jax.experimental.pallas import tpu as pltpu

D_MODEL = 1024
D_CONV = 1024
D_POOL = 1024
CONV_WIDTH = 31
POOL_WINDOWS = (2, 4, 8, 16)
POOL_GROUP = 256
PLE_DIM = 256
N_EXPERTS = 64
N_GROUPS = 8
GROUP_SIZE = N_EXPERTS // N_GROUPS
TOPK_GROUPS = 4
TOP_K = 8
D_EXPERT = 256
ROUTED_SCALE = 2.5
NORM_EPS = 1e-6

F32 = jnp.float32
BF16 = jnp.bfloat16

MIX_TM = 512
MIX_NV = MIX_TM // 8
CONV_MG = 8
ROW_CHUNK = 64
LANE = 128

ROUTER_TM = 1024
WIN = 256
SEL_ROWS = 2560
SEL_RG = 64
SEL_MM = 512
SEL_TAIL = 256
EXP_BM = 576
EXP_XDEPTH = 6
EXP_YDEPTH = 4
EXP_SPLIT = 4
RUN_FIELDS = 4
RUN_UNROLL = 4
CMB_LG = 512

VMEM_LIMIT = 56 * 1024 * 1024


def _rms(x, g):
    ms = jnp.mean(x * x, axis=-1, keepdims=True)
    return x * lax.rsqrt(ms + NORM_EPS) * g


def _dot(a, b):
    return jnp.dot(a, b, preferred_element_type=F32)


def _dot_t(a, b):
    return lax.dot_general(a, b, (((0,), (0,)), ((), ())), preferred_element_type=F32)


def _mixer_kernel(x_ref, gmix_ref, win_ref, bin_ref, wdw_ref, bdw_ref, gcln_ref, bcln_ref,
                  wco_ref, bco_ref, wpool_ref, spool_ref, wout_ref, gffn_ref, perm_ref, unperm_ref,
                  x1_ref, h2_ref, a_ext, a_prev, u_ext, u_prev, c_buf, q_buf, *, tiles_per_seq):
    i = pl.program_id(0) % tiles_per_seq
    tm = MIX_TM
    nv = MIX_NV

    @pl.when(i == 0)
    def _():
        a_prev[...] = jnp.zeros(a_prev.shape, F32)
        u_prev[...] = jnp.zeros(u_prev.shape, F32)

    x = x_ref[...]
    h = _dot(perm_ref[...], _rms(x, gmix_ref[...]).astype(BF16)).astype(BF16)

    def proj(lo, hi):
        return _dot(h, win_ref[:, lo:hi]) + bin_ref[:, lo:hi]

    glu = proj(0, D_CONV) * jax.nn.sigmoid(proj(D_CONV, 2 * D_CONV))
    for lc in range(D_CONV // LANE):
        a_ext[lc, tm:2 * tm, :] = glu[:, lc * LANE:(lc + 1) * LANE]
    u_ext[tm:2 * tm, :] = proj(2 * D_CONV, 2 * D_CONV + D_POOL)

    def delayed_groups(ext, prev, first_group):
        last_row = lax.broadcasted_iota(jnp.int32, (8, ext.shape[-1]), 0) == 7
        for g in range(first_group, nv):
            rows = slice(8 * g, 8 * g + 8)
            mixed = jnp.where(last_row, prev[rows, :], ext[tm + 8 * g:tm + 8 * g + 8, :])
            ext[rows, :] = pltpu.roll(mixed, 1, axis=0)
            prev[rows, :] = ext[tm + 8 * g:tm + 8 * g + 8, :]

    delayed_groups(u_ext, u_prev, nv - (max(POOL_WINDOWS) - 1))

    def conv_column(lc, carry):
        a_col = a_ext.at[lc]
        delayed_groups(a_col, a_prev.at[lc], nv - (CONV_WIDTH - 1))
        w_col = wdw_ref.at[lc]
        for g0 in range(0, nv, CONV_MG):
            acc = None
            for k in range(CONV_WIDTH):
                src = nv + g0 + k - (CONV_WIDTH - 1)
                term = a_col[8 * src:8 * (src + CONV_MG), :] * w_col[k:k + 1, :]
                acc = term if acc is None else acc + term
            c_buf[lc, 8 * g0:8 * (g0 + CONV_MG), :] = acc + bdw_ref[lc]
        return carry
    lax.fori_loop(0, D_CONV // LANE, conv_column, 0)

    c = jnp.concatenate([c_buf[lc] for lc in range(D_CONV // LANE)], axis=-1)
    mu = jnp.mean(c, axis=-1, keepdims=True)
    xc = c - mu
    var = jnp.mean(xc * xc, axis=-1, keepdims=True)
    y = xc * lax.rsqrt(var + NORM_EPS) * gcln_ref[...] + bcln_ref[...]
    y = y * jax.nn.sigmoid(y)
    branch_a = _dot(y.astype(BF16), wco_ref[...]) + bco_ref[...]

    for r0 in range(0, tm, ROW_CHUNK):
        row = r0 + lax.broadcasted_iota(jnp.int32, (ROW_CHUNK, POOL_GROUP), 0)
        t1 = i * tm + (row % 8) * nv + row // 8 + 1
        for gi, w in enumerate(POOL_WINDOWS):
            ls = slice(gi * POOL_GROUP, (gi + 1) * POOL_GROUP)
            tok = u_ext[tm + r0:tm + r0 + ROW_CHUNK, ls]
            s = tok
            for j in range(1, w):
                s = s + u_ext[tm + r0 - 8 * j:tm + r0 - 8 * j + ROW_CHUNK, ls]
            cnt = jnp.minimum(t1, w).astype(F32)
            q_buf[r0:r0 + ROW_CHUNK, ls] = s / cnt - tok

    qs_out = []
    for gi in range(len(POOL_WINDOWS)):
        ls = slice(gi * POOL_GROUP, (gi + 1) * POOL_GROUP)
        qs_out.append(_dot(q_buf[:, ls].astype(BF16), wpool_ref[gi]) * spool_ref[:, ls])
    branch_b = jnp.concatenate(qs_out, axis=-1)

    c2 = 2 * D_CONV + D_POOL
    gate_a = jax.nn.sigmoid(proj(c2, c2 + D_MODEL))
    gate_b = jax.nn.sigmoid(proj(c2 + D_MODEL, c2 + 2 * D_MODEL))
    merged = gate_a * branch_a + gate_b * branch_b
    merged = _dot(unperm_ref[...], merged.astype(BF16)).astype(BF16)
    x1 = x + _dot(merged, wout_ref[...])
    x1_ref[...] = x1
    h2_ref[...] = _rms(x1, gffn_ref[...]).astype(BF16)


def _const_spec(shape):
    n = len(shape)
    return pl.BlockSpec(shape, lambda i, _n=n: (0,) * _n, pipeline_mode=pl.Buffered(1))


def _mixer(x, seq_len, g_mix, w_in, b_in, w_dw, b_dw, g_cln, b_cln, w_co, b_co, w_pool, s_pool, w_out,
           g_ffn):
    t = x.shape[0]
    tm = MIX_TM
    assert seq_len % tm == 0 and MIX_NV >= CONV_WIDTH and MIX_NV >= max(POOL_WINDOWS)
    d_in = w_in.shape[1]
    row = pl.BlockSpec((tm, D_MODEL), lambda i: (i, 0))
    n_col = D_CONV // LANE
    w_dw = w_dw.reshape(CONV_WIDTH, n_col, LANE).transpose(1, 0, 2)
    b_dw = b_dw.reshape(n_col, 1, LANE)
    r = jnp.arange(tm)
    perm = ((r % 8) * MIX_NV + r // 8)[:, None] == jnp.arange(tm)[None, :]
    perm = perm.astype(BF16)
    return pl.pallas_call(
        functools.partial(_mixer_kernel, tiles_per_seq=seq_len // tm),
        grid=(t // tm,),
        in_specs=[
            row,
            _const_spec((1, D_MODEL)),
            _const_spec((D_MODEL, d_in)),
            _const_spec((1, d_in)),
            _const_spec((n_col, CONV_WIDTH, LANE)),
            _const_spec((n_col, 1, LANE)),
            _const_spec((1, D_CONV)),
            _const_spec((1, D_CONV)),
            _const_spec((D_CONV, D_MODEL)),
            _const_spec((1, D_MODEL)),
            _const_spec((len(POOL_WINDOWS), POOL_GROUP, POOL_GROUP)),
            _const_spec((1, D_POOL)),
            _const_spec((D_MODEL, D_MODEL)),
            _const_spec((1, D_MODEL)),
            _const_spec((tm, tm)),
            _const_spec((tm, tm)),
        ],
        out_specs=[row, row],
        out_shape=[jax.ShapeDtypeStruct((t, D_MODEL), F32),
                   jax.ShapeDtypeStruct((t, D_MODEL), BF16)],
        scratch_shapes=[
            pltpu.VMEM((n_col, 2 * tm, LANE), F32),
            pltpu.VMEM((n_col, tm, LANE), F32),
            pltpu.VMEM((2 * tm, D_POOL), F32),
            pltpu.VMEM((tm, D_POOL), F32),
            pltpu.VMEM((n_col, tm, LANE), F32),
            pltpu.VMEM((tm, D_POOL), F32),
        ],
        compiler_params=pltpu.CompilerParams(
            dimension_semantics=("arbitrary",), vmem_limit_bytes=VMEM_LIMIT),
        name="mixer",
    )(x, g_mix, w_in, b_in, w_dw, b_dw, g_cln, b_cln, w_co, b_co, w_pool, s_pool, w_out, g_ffn, perm, perm.T)


def _beats(v, other, other_is_later):
    v = jnp.broadcast_to(v, other.shape)
    return jnp.where(other_is_later, jnp.where(v >= other, 1, 0), jnp.where(v > other, 1, 0))


def _router_kernel(h2_ref, wrt_ref, br_ref, utri_ref, ltri_ref, gate_ref, rank_ref, pos_ref, cnt_ref):
    tm = ROUTER_TM
    logits = lax.dot_general(wrt_ref[...], h2_ref[...], (((1,), (1,)), ((), ())),
                             preferred_element_type=F32)
    scores = jax.nn.sigmoid(logits)
    sel = scores + br_ref[...]
    shape3 = (N_GROUPS, GROUP_SIZE, tm)
    sel3 = sel.reshape(shape3)
    scores3 = scores.reshape(shape3)
    neg_inf = jnp.float32(-jnp.inf)

    member = lax.broadcasted_iota(jnp.int32, shape3, 1)
    m1 = jnp.max(sel3, axis=1, keepdims=True)
    first = jnp.min(jnp.where(sel3 == m1, member, GROUP_SIZE), axis=1, keepdims=True)
    m2 = jnp.max(jnp.where(member == first, neg_inf, sel3), axis=1, keepdims=True)
    gscore = jnp.broadcast_to(m1 + m2, shape3)

    gidx = lax.broadcasted_iota(jnp.int32, shape3, 0)
    grank = jnp.zeros(shape3, jnp.int32)
    for j in range(N_GROUPS):
        sj = gscore[j:j + 1]
        grank = grank + _beats(sj, gscore, gidx > j)
    masked = jnp.where(grank < TOPK_GROUPS, sel3, neg_inf)

    eidx = gidx * GROUP_SIZE + member
    work = masked
    erank = jnp.full(shape3, TOP_K, jnp.int32)
    for k in range(TOP_K):
        best = jnp.max(jnp.max(work, axis=0, keepdims=True), axis=1, keepdims=True)
        cand = jnp.where(work == best, eidx, N_EXPERTS)
        pick = jnp.min(jnp.min(cand, axis=0, keepdims=True), axis=1, keepdims=True)
        hit = eidx == pick
        work = jnp.where(hit, neg_inf, work)
        erank = jnp.where(hit, k, erank)
    chosen = erank < TOP_K
    top_s = jnp.where(chosen, scores3, 0.0)
    denom = jnp.sum(jnp.sum(top_s, axis=0, keepdims=True), axis=1, keepdims=True)
    gates3 = top_s / denom * ROUTED_SCALE
    chosen2 = jnp.where(chosen, 1.0, 0.0).reshape(N_EXPERTS, tm)
    gate_ref[...] = gates3.reshape(N_EXPERTS, tm).astype(BF16)

    for w in range(tm // WIN):
        ls = slice(w * WIN, (w + 1) * WIN)
        mw = chosen2[:, ls]
        rank = _dot(mw.astype(BF16), utri_ref[...])
        n = jnp.sum(mw, axis=1, keepdims=True)
        run = jnp.floor((n + 7.0) * 0.125) * 8.0
        start = _dot(ltri_ref[...], jnp.broadcast_to(run, (N_EXPERTS, WIN)).astype(BF16))
        rank_ref[:, ls] = jnp.where(mw > 0.5, rank, -1.0).astype(BF16)
        row3 = (rank + start).reshape(N_GROUPS, GROUP_SIZE, WIN)
        er = erank[:, :, ls]
        for k in range(TOP_K):
            pk = jnp.sum(jnp.sum(jnp.where(er == k, row3, 0.0), axis=0, keepdims=True), axis=1, keepdims=True)
            pos_ref[k:k + 1, ls] = pk.reshape(1, WIN).astype(jnp.int32)
        cnt_ref[w] = n


def _router(h2, w_rt, b_r):
    t = h2.shape[0]
    tm = ROUTER_TM
    utri = jnp.triu(jnp.ones((WIN, WIN), BF16), k=1)
    ltri = jnp.tril(jnp.ones((N_EXPERTS, N_EXPERTS), BF16), k=-1)
    return pl.pallas_call(
        _router_kernel,
        grid=(t // tm,),
        in_specs=[
            pl.BlockSpec((tm, D_MODEL), lambda i: (i, 0)),
            _const_spec((N_EXPERTS, D_MODEL)),
            _const_spec((N_EXPERTS, 1)),
            _const_spec((WIN, WIN)),
            _const_spec((N_EXPERTS, N_EXPERTS)),
        ],
        out_specs=[
            pl.BlockSpec((N_EXPERTS, tm), lambda i: (0, i)),
            pl.BlockSpec((N_EXPERTS, tm), lambda i: (0, i)),
            pl.BlockSpec((TOP_K, tm), lambda i: (0, i)),
            pl.BlockSpec((tm // WIN, N_EXPERTS, 1), lambda i: (i, 0, 0)),
        ],
        out_shape=[
            jax.ShapeDtypeStruct((N_EXPERTS, t), BF16),
            jax.ShapeDtypeStruct((N_EXPERTS, t), BF16),
            jax.ShapeDtypeStruct((TOP_K, t), jnp.int32),
            jax.ShapeDtypeStruct((t // WIN, N_EXPERTS, 1), F32),
        ],
        compiler_params=pltpu.CompilerParams(
            dimension_semantics=("arbitrary",), vmem_limit_bytes=VMEM_LIMIT),
        name="router",
    )(h2, w_rt, b_r, utri, ltri)


def _sorted_rows_bound(t):
    rows = t * TOP_K + (t // WIN) * N_EXPERTS * 7 + N_EXPERTS * (EXP_BM - 1)
    blocks = -(-rows // EXP_BM)
    return (blocks + blocks % 2) * EXP_BM


def _dispatch_plan(cnt, t):
    nw = t // WIN
    n = cnt.reshape(nw, N_EXPERTS).astype(jnp.int32)
    run = (n + 7) // 8 * 8
    local_end = jnp.cumsum(run, axis=1)
    local_off = jnp.concatenate([jnp.zeros((nw, 1), jnp.int32), local_end], axis=1)
    total = jnp.sum(run, axis=0)
    region = (total + EXP_BM - 1) // EXP_BM * EXP_BM
    eid = jnp.arange(N_EXPERTS, dtype=jnp.int32)
    last_owner = jnp.max(jnp.where(region > 0, eid, 0))
    odd = (jnp.sum(region) // EXP_BM) % 2
    region = region + jnp.where(eid == last_owner, odd * EXP_BM, 0)
    region_end = jnp.cumsum(region)
    base = region_end - region
    global_off = base[None, :] + jnp.cumsum(run, axis=0) - run
    n_blocks = _sorted_rows_bound(t) // EXP_BM
    n_used = region_end[-1] // EXP_BM
    blk = jnp.arange(n_blocks, dtype=jnp.int32)
    blk_expert = jnp.sum((region_end[None, :] <= blk[:, None] * EXP_BM).astype(jnp.int32), axis=1)
    blk_expert = jnp.minimum(blk_expert, N_EXPERTS - 1)
    later_nonempty = (eid[None, :] > eid[:, None]) & (region[None, :] > 0)
    next_expert = jnp.min(jnp.where(later_nonempty, eid[None, :], N_EXPERTS), axis=1).astype(jnp.int32)
    return dict(
        run_lo=local_off[:, :N_EXPERTS].reshape(nw, N_EXPERTS, 1),
        run_hi=local_off[:, 1:].reshape(nw, N_EXPERTS, 1),
        local_off=local_off.reshape(-1),
        runs=jnp.stack([local_off[:, :N_EXPERTS], run, global_off, jnp.zeros_like(run)], axis=-1).reshape(-1),
        fill_off=base + total, fill_cnt=region - total,
        blk_expert=blk_expert.astype(jnp.int32), next_expert=next_expert,
        n_used=n_used.reshape(1).astype(jnp.int32))


def _run_copy(runs_ref, win, e, vmem_buf, slot, hbm_buf, sem, to_hbm):
    p = (win * N_EXPERTS + e) * RUN_FIELDS
    lo = pl.multiple_of(runs_ref[p], 8)
    cnt = pl.multiple_of(runs_ref[p + 1], 8)
    go = pl.multiple_of(runs_ref[p + 2], 8)
    v = vmem_buf.at[pl.ds(pl.multiple_of(slot * SEL_ROWS + lo, 8), cnt)]
    h = hbm_buf.at[pl.ds(go, cnt)]
    cp = pltpu.make_async_copy(v, h, sem.at[slot]) if to_hbm else pltpu.make_async_copy(h, v, sem.at[slot])
    return cnt, cp


def _start_runs(runs_ref, win, vmem_buf, slot, hbm_buf, sem, to_hbm):
    def body(i, carry):
        copies = [_run_copy(runs_ref, win, i * RUN_UNROLL + j, vmem_buf, slot, hbm_buf, sem, to_hbm)
                  for j in range(RUN_UNROLL)]
        for cnt, cp in copies:
            @pl.when(cnt > 0)
            def _(cp=cp):
                cp.start()
        return carry
    lax.fori_loop(0, N_EXPERTS // RUN_UNROLL, body, 0)


def _wait_runs(local_ref, win, vmem_buf, slot, hbm_buf, sem, to_hbm):
    total = pl.multiple_of(local_ref[win * (N_EXPERTS + 1) + N_EXPERTS], 8)
    v = vmem_buf.at[pl.ds(pl.multiple_of(slot * SEL_ROWS, 8), total)]
    h = hbm_buf.at[pl.ds(0, total)]
    cp = pltpu.make_async_copy(v, h, sem.at[slot]) if to_hbm else pltpu.make_async_copy(h, v, sem.at[slot])

    @pl.when(total > 0)
    def _():
        cp.wait()


def _dispatch_kernel(local_ref, runs_ref, fill_off_ref, fill_cnt_ref, h2_ref, pos_ref, xs_hbm, sbuf,
                     s_ref, sem, zsem, *, n_win):
    w = pl.program_id(0)
    slot = w % 2
    pos = pos_ref[...]

    h2 = h2_ref[...]
    assert SEL_RG <= 256
    rid_b = lax.broadcasted_iota(jnp.int32, (SEL_RG, WIN), 0).astype(F32).astype(BF16)
    one_b = jnp.ones((SEL_RG, WIN), BF16)
    def compact(first, n_rows):
        for r0 in range(first, first + n_rows, SEL_RG):
            acc = jnp.zeros((SEL_RG, WIN), BF16)
            for k in range(TOP_K):
                off = (pos[k:k + 1, :] - r0).astype(F32)
                off = jnp.broadcast_to(off, (SEL_RG, WIN)).astype(BF16)
                acc = jnp.where(rid_b == off, one_b, acc)
            s_ref[r0:r0 + SEL_RG, :] = acc
        dst = pl.multiple_of(slot * SEL_ROWS + first, SEL_TAIL)
        sbuf[pl.ds(dst, n_rows), :] = _dot(s_ref[first:first + n_rows, :], h2).astype(BF16)

    @pl.when(w == 0)
    def _():
        for s in range(2):
            sbuf[(s + 1) * SEL_ROWS - SEL_TAIL:(s + 1) * SEL_ROWS, :] = jnp.zeros((SEL_TAIL, D_MODEL), BF16)

    for first in range(0, SEL_ROWS - 2 * SEL_TAIL, SEL_MM):
        compact(first, SEL_MM)
    compact(SEL_ROWS - 2 * SEL_TAIL, SEL_TAIL)

    @pl.when(local_ref[w * (N_EXPERTS + 1) + N_EXPERTS] > SEL_ROWS - SEL_TAIL)
    def _():
        compact(SEL_ROWS - SEL_TAIL, SEL_TAIL)

    _start_runs(runs_ref, w, sbuf, slot, xs_hbm, sem, True)

    @pl.when(w > 0)
    def _():
        _wait_runs(local_ref, w - 1, sbuf, 1 - slot, xs_hbm, sem, True)

    @pl.when(w == n_win - 1)
    def _():
        sbuf[2 * SEL_ROWS:, :] = jnp.zeros((2 * EXP_BM, D_MODEL), BF16)

        def fill(e, wait):
            cnt = pl.multiple_of(fill_cnt_ref[e], 8)
            off = pl.multiple_of(fill_off_ref[e], 8)
            cp = pltpu.make_async_copy(sbuf.at[pl.ds(2 * SEL_ROWS, cnt)], xs_hbm.at[pl.ds(off, cnt)], zsem)

            @pl.when(cnt > 0)
            def _():
                if wait:
                    cp.wait()
                else:
                    cp.start()

        def start_body(e, carry):
            fill(e, False)
            return carry

        def wait_body(e, carry):
            fill(e, True)
            return carry
        lax.fori_loop(0, N_EXPERTS, start_body, 0)
        _wait_runs(local_ref, w, sbuf, slot, xs_hbm, sem, True)
        lax.fori_loop(0, N_EXPERTS, wait_body, 0)


def _staging_shape(extra_rows):
    return jax.ShapeDtypeStruct((2 * SEL_ROWS + extra_rows, D_MODEL), BF16)


def _staging_spec(extra_rows):
    return pl.BlockSpec((2 * SEL_ROWS + extra_rows, D_MODEL), lambda w, *_: (0, 0))


def _dispatch(plan, h2, pos):
    t = h2.shape[0]
    n_win = t // WIN
    return pl.pallas_call(
        functools.partial(_dispatch_kernel, n_win=n_win),
        grid_spec=pltpu.PrefetchScalarGridSpec(
            num_scalar_prefetch=4,
            grid=(n_win,),
            in_specs=[
                pl.BlockSpec((WIN, D_MODEL), lambda w, *_: (w, 0)),
                pl.BlockSpec((TOP_K, WIN), lambda w, *_: (0, w)),
            ],
            out_specs=[pl.BlockSpec(memory_space=pl.ANY), _staging_spec(2 * EXP_BM)],
            scratch_shapes=[
                pltpu.VMEM((SEL_ROWS, WIN), BF16),
                pltpu.SemaphoreType.DMA((2,)),
                pltpu.SemaphoreType.DMA,
            ]),
        out_shape=[jax.ShapeDtypeStruct((_sorted_rows_bound(t), D_MODEL), BF16), _staging_shape(2 * EXP_BM)],
        compiler_params=pltpu.CompilerParams(
            dimension_semantics=("arbitrary",), vmem_limit_bytes=VMEM_LIMIT),
        name="dispatch",
    )(plan['local_off'], plan['runs'], plan['fill_off'], plan['fill_cnt'], h2, pos)[0]


def _expert_kernel(blk_expert_ref, next_expert_ref, n_used_ref, xs_hbm, wg_hbm, wu_hbm, wd_hbm, ys_hbm,
                   xbuf, ybuf, wg_st, wu_st, wd_st, wg_bf, wu_bf, wd_bf, xsem, ysem, wsem):
    n_used = n_used_ref[0]
    part = EXP_BM // EXP_SPLIT

    def row_copies(b, slot, fetch):
        out = []
        for q in range(EXP_SPLIT):
            hbm_rows = pl.ds(pl.multiple_of(b * EXP_BM + q * part, part), part)
            if fetch:
                out.append(pltpu.make_async_copy(xs_hbm.at[hbm_rows], xbuf.at[slot, q * part:(q + 1) * part],
                                                 xsem.at[slot]))
            else:
                out.append(pltpu.make_async_copy(ybuf.at[slot, q * part:(q + 1) * part], ys_hbm.at[hbm_rows],
                                                 ysem.at[slot]))
        return out

    def weight_copies(e, slot):
        return [pltpu.make_async_copy(wg_hbm.at[e], wg_st.at[slot], wsem.at[slot]),
                pltpu.make_async_copy(wu_hbm.at[e], wu_st.at[slot], wsem.at[slot]),
                pltpu.make_async_copy(wd_hbm.at[e], wd_st.at[slot], wsem.at[slot])]

    def start(copies):
        for c in copies:
            c.start()

    def wait(copies):
        for c in copies:
            c.wait()

    for ahead in range(EXP_XDEPTH - 2):
        @pl.when(ahead < n_used)
        def _(ahead=ahead):
            start(row_copies(ahead, ahead, True))

    @pl.when(n_used > 0)
    def _():
        start(weight_copies(blk_expert_ref[0], 0))

    def enter_block(b, wset):
        e = blk_expert_ref[b]
        new_expert = jnp.logical_or(b == 0, e != blk_expert_ref[jnp.maximum(b - 1, 0)])
        wset = jnp.where(new_expert, 1 - wset, wset)

        @pl.when(new_expert)
        def _():
            wait(weight_copies(e, wset))
            wg_bf[wset] = wg_st[wset].astype(BF16)
            wu_bf[wset] = wu_st[wset].astype(BF16)
            wd_bf[wset] = wd_st[wset].astype(BF16)
            nxt = next_expert_ref[e]

            @pl.when(nxt < N_EXPERTS)
            def _():
                start(weight_copies(nxt, 1 - wset))

        ahead = b + EXP_XDEPTH - 2

        @pl.when(ahead < n_used)
        def _():
            start(row_copies(ahead, ahead % EXP_XDEPTH, True))

        wait(row_copies(b, b % EXP_XDEPTH, True))

        @pl.when(b >= EXP_YDEPTH)
        def _():
            wait(row_copies(b - EXP_YDEPTH, b % EXP_YDEPTH, False))
        return wset

    def compute(b, wset):
        x = xbuf[b % EXP_XDEPTH]
        hg = _dot(x, wg_bf[wset])
        hb = hg * jax.nn.sigmoid(hg) * _dot(x, wu_bf[wset])
        ybuf[b % EXP_YDEPTH] = _dot(hb.astype(BF16), wd_bf[wset]).astype(BF16)

    def body(p, wset):
        b0 = 2 * p
        w0 = enter_block(b0, wset)
        w1 = enter_block(b0 + 1, w0)
        compute(b0, w0)
        compute(b0 + 1, w1)
        start(row_copies(b0, b0 % EXP_YDEPTH, False))
        start(row_copies(b0 + 1, (b0 + 1) % EXP_YDEPTH, False))
        return w1

    lax.fori_loop(0, n_used // 2, body, jnp.int32(1))

    for back in range(EXP_YDEPTH, 0, -1):
        @pl.when(n_used >= back)
        def _(back=back):
            wait(row_copies(n_used - back, (n_used - back) % EXP_YDEPTH, False))


def _experts(plan, xs, w_gate, w_up, w_down):
    any_spec = pl.BlockSpec(memory_space=pl.ANY)
    return pl.pallas_call(
        _expert_kernel,
        grid_spec=pltpu.PrefetchScalarGridSpec(
            num_scalar_prefetch=3,
            grid=(1,),
            in_specs=[any_spec, any_spec, any_spec, any_spec],
            out_specs=any_spec,
            scratch_shapes=[
                pltpu.VMEM((EXP_XDEPTH, EXP_BM, D_MODEL), BF16),
                pltpu.VMEM((EXP_YDEPTH, EXP_BM, D_MODEL), BF16),
                pltpu.VMEM((2, D_MODEL, D_EXPERT), F32),
                pltpu.VMEM((2, D_MODEL, D_EXPERT), F32),
                pltpu.VMEM((2, D_EXPERT, D_MODEL), F32),
                pltpu.VMEM((2, D_MODEL, D_EXPERT), BF16),
                pltpu.VMEM((2, D_MODEL, D_EXPERT), BF16),
                pltpu.VMEM((2, D_EXPERT, D_MODEL), BF16),
                pltpu.SemaphoreType.DMA((EXP_XDEPTH,)),
                pltpu.SemaphoreType.DMA((EXP_YDEPTH,)),
                pltpu.SemaphoreType.DMA((2,)),
            ]),
        out_shape=jax.ShapeDtypeStruct(xs.shape, BF16),
        compiler_params=pltpu.CompilerParams(
            dimension_semantics=("arbitrary",), vmem_limit_bytes=VMEM_LIMIT),
        name="experts",
    )(plan['blk_expert'], plan['next_expert'], plan['n_used'], xs, w_gate, w_up, w_down)


def _combine_kernel(local_ref, runs_ref, x1_ref, h2_ref, p_ref, rank_ref, gate_ref, lo_ref, hi_ref,
                    wsg_ref, wsu_ref, wsd_ref, gple_ref, wpg_ref, wp_ref, gfin_ref, ys_hbm, o_ref, ybuf, st_ref, sem,
                    *, n_win, final_norm):
    w = pl.program_id(0)
    slot = w % 2

    @pl.when(w == 0)
    def _():
        ybuf[...] = jnp.zeros(ybuf.shape, BF16)
        _start_runs(runs_ref, w, ybuf, slot, ys_hbm, sem, False)

    @pl.when(w + 1 < n_win)
    def _():
        _start_runs(runs_ref, w + 1, ybuf, 1 - slot, ys_hbm, sem, False)

    lo = lo_ref[0]
    hi = hi_ref[0]
    lo_f = lo.astype(F32)
    rank_tbl = rank_ref[...]
    gate_tbl = gate_ref[...]

    def build_group(lg):
        cols = slice(lg * CMB_LG, (lg + 1) * CMB_LG)
        rid = lg * CMB_LG + lax.broadcasted_iota(jnp.int32, (N_EXPERTS, CMB_LG), 1)
        owner = jnp.where(rid >= lo, jnp.where(rid < hi, 1.0, 0.0), 0.0)
        run_row = rid[0:1, :].astype(F32) - jnp.sum(owner * lo_f, axis=0, keepdims=True)
        owner = owner.astype(BF16)
        hit = _dot_t(rank_tbl, owner) == run_row
        st_ref[:, cols] = jnp.where(hit, _dot_t(gate_tbl, owner), 0.0).astype(BF16)

    build_group(0)
    h2 = h2_ref[...]
    hs = _dot(h2, wsg_ref[...])
    hs = hs * jax.nn.sigmoid(hs) * _dot(h2, wsu_ref[...])
    shared = _dot(hs.astype(BF16), wsd_ref[...])

    _wait_runs(local_ref, w, ybuf, slot, ys_hbm, sem, False)
    routed = None
    n_groups = SEL_ROWS // CMB_LG
    for lg in range(n_groups):
        if lg + 1 < n_groups:
            build_group(lg + 1)
        src = pl.multiple_of(slot * SEL_ROWS + lg * CMB_LG, CMB_LG)
        part = _dot(st_ref[:, lg * CMB_LG:(lg + 1) * CMB_LG], ybuf[pl.ds(src, CMB_LG), :])
        routed = part if routed is None else routed + part
    x2 = x1_ref[...] + routed + shared

    hp = _rms(x2, gple_ref[...]).astype(BF16)
    gate = jax.nn.sigmoid(_dot(hp, wpg_ref[...]))
    x3 = x2 + gate * _dot(p_ref[...].astype(BF16), wp_ref[...])
    o_ref[...] = _rms(x3, gfin_ref[...]) if final_norm else x3


def _combine(plan, ys, x1, h2, p, rank_tbl, gate_tbl, wsg, wsu, wsd, g_ple, w_pg, w_p, g_fin, final_norm):
    t = x1.shape[0]
    n_win = t // WIN
    row = lambda width: pl.BlockSpec((WIN, width), lambda w, *_: (w, 0))
    const = lambda shape: pl.BlockSpec(shape, lambda w, *_: (0,) * len(shape))
    return pl.pallas_call(
        functools.partial(_combine_kernel, n_win=n_win, final_norm=final_norm),
        grid_spec=pltpu.PrefetchScalarGridSpec(
            num_scalar_prefetch=2,
            grid=(n_win,),
            in_specs=[
                row(D_MODEL), row(D_MODEL), row(PLE_DIM),
                pl.BlockSpec((N_EXPERTS, WIN), lambda w, *_: (0, w)),
                pl.BlockSpec((N_EXPERTS, WIN), lambda w, *_: (0, w)),
                pl.BlockSpec((1, N_EXPERTS, 1), lambda w, *_: (w, 0, 0)),
                pl.BlockSpec((1, N_EXPERTS, 1), lambda w, *_: (w, 0, 0)),
                const((D_MODEL, D_EXPERT)), const((D_MODEL, D_EXPERT)), const((D_EXPERT, D_MODEL)),
                const((1, D_MODEL)), const((D_MODEL, D_MODEL)), const((PLE_DIM, D_MODEL)),
                const((1, D_MODEL)),
                pl.BlockSpec(memory_space=pl.ANY),
            ],
            out_specs=[row(D_MODEL), _staging_spec(0)],
            scratch_shapes=[
                pltpu.VMEM((WIN, SEL_ROWS), BF16),
                pltpu.SemaphoreType.DMA((2,)),
            ]),
        out_shape=[jax.ShapeDtypeStruct((t, D_MODEL), F32), _staging_shape(0)],
        compiler_params=pltpu.CompilerParams(
            dimension_semantics=("arbitrary",), vmem_limit_bytes=VMEM_LIMIT),
        name="combine",
    )(plan['local_off'], plan['runs'], x1, h2, p, rank_tbl, gate_tbl, plan['run_lo'], plan['run_hi'],
      wsg, wsu, wsd, g_ple, w_pg, w_p, g_fin, ys)[0]


def kernel(x, p, g_mix, w_in, b_in, w_dw, b_dw, g_cln, b_cln, w_conv_out, b_conv_out, w_pool, s_pool,
           w_out, g_ffn, w_router, b_router, w_e_gate, w_e_up, w_e_down, w_s_gate, w_s_up, w_s_down,
           g_ple, w_ple_gate, w_ple, g_final):
    bsz, s, d = x.shape
    t = bsz * s
    depth = w_in.shape[0]
    xt = x.reshape(t, d)
    row = lambda v: v.reshape(1, -1)
    for i in range(depth):
        x1, h2 = _mixer(
            xt, s, row(g_mix[i]), w_in[i].astype(BF16), row(b_in[i]), w_dw[i], row(b_dw[i]),
            row(g_cln[i]), row(b_cln[i]), w_conv_out[i].astype(BF16), row(b_conv_out[i]),
            w_pool[i].astype(BF16), row(s_pool[i]), w_out[i].astype(BF16), row(g_ffn[i]))
        gate, rank, pos, cnt = _router(h2, w_router[i].T.astype(BF16), b_router[i].reshape(N_EXPERTS, 1))
        plan = _dispatch_plan(cnt, t)
        xs = _dispatch(plan, h2, pos)
        ys = _experts(plan, xs, w_e_gate[i], w_e_up[i], w_e_down[i])
        xt = _combine(
            plan, ys, x1, h2, p[i].reshape(t, PLE_DIM), rank, gate,
            w_s_gate[i].astype(BF16), w_s_up[i].astype(BF16), w_s_down[i].astype(BF16),
            row(g_ple[i]), w_ple_gate[i].astype(BF16), w_ple[i].astype(BF16), row(g_final),
            final_norm=(i == depth - 1))
    return xt.reshape(bsz, s, d)
```

```python
import functools

import jax
import jax.numpy as jnp
from jax import lax
from jax.experimental import pallas as pl
from jax.experimental.pallas import tpu as pltpu

D_MODEL = 1024
D_CONV = 1024
D_POOL = 1024
CONV_WIDTH = 31
POOL_WINDOWS = (2, 4, 8, 16)
POOL_GROUP = 256
PLE_DIM = 256
N_EXPERTS = 64
N_GROUPS = 8
GROUP_SIZE = N_EXPERTS // N_GROUPS
TOPK_GROUPS = 4
TOP_K = 8
D_EXPERT = 256
ROUTED_SCALE = 2.5
NORM_EPS = 1e-6

F32 = jnp.float32
BF16 = jnp.bfloat16

MIX_TM = 512
MIX_NV = MIX_TM // 8
CONV_MG = 8
ROW_CHUNK = 64
LANE = 128

ROUTER_TM = 1024
WIN = 256
SEL_ROWS = 2560
SEL_RG = 64
SEL_MM = 512
SEL_TAIL = 256
EXP_BM = 576
EXP_XDEPTH = 6
EXP_YDEPTH = 4
EXP_SPLIT = 4
RUN_FIELDS = 4
RUN_UNROLL = 4
CMB_LG = 512

V7X_VMEM_BYTES = 64 * 1024 * 1024
VMEM_LIMIT = V7X_VMEM_BYTES - 8 * 1024 * 1024

assert SEL_ROWS >= TOP_K * WIN + 7 * N_EXPERTS
assert (SEL_ROWS - 2 * SEL_TAIL) % SEL_MM == 0 and SEL_TAIL % SEL_RG == 0 and SEL_ROWS % CMB_LG == 0
assert N_EXPERTS % RUN_UNROLL == 0 and EXP_BM % (16 * EXP_SPLIT) == 0 and EXP_XDEPTH > 2 and EXP_YDEPTH >= 2


def _rms(x, g):
    ms = jnp.mean(x * x, axis=-1, keepdims=True)
    return x * lax.rsqrt(ms + NORM_EPS) * g


def _dot(a, b):
    return jnp.dot(a, b, preferred_element_type=F32)


def _dot_t(a, b):
    return lax.dot_general(a, b, (((0,), (0,)), ((), ())), preferred_element_type=F32)


def _mixer_kernel(x_ref, gmix_ref, win_ref, bin_ref, wdw_ref, bdw_ref, gcln_ref, bcln_ref,
                  wco_ref, bco_ref, wpool_ref, spool_ref, wout_ref, gffn_ref, perm_ref, unperm_ref,
                  x1_ref, h2_ref, a_ext, a_prev, u_ext, u_prev, c_buf, q_buf, *, tiles_per_seq):
    i = pl.program_id(0) % tiles_per_seq
    tm = MIX_TM
    nv = MIX_NV

    @pl.when(i == 0)
    def _():
        a_prev[...] = jnp.zeros(a_prev.shape, F32)
        u_prev[...] = jnp.zeros(u_prev.shape, F32)

    x = x_ref[...]
    h = _dot(perm_ref[...], _rms(x, gmix_ref[...]).astype(BF16)).astype(BF16)

    def proj(lo, hi):
        return _dot(h, win_ref[:, lo:hi]) + bin_ref[:, lo:hi]

    glu = proj(0, D_CONV) * jax.nn.sigmoid(proj(D_CONV, 2 * D_CONV))
    for lc in range(D_CONV // LANE):
        a_ext[lc, tm:2 * tm, :] = glu[:, lc * LANE:(lc + 1) * LANE]
    u_ext[tm:2 * tm, :] = proj(2 * D_CONV, 2 * D_CONV + D_POOL)

    def delayed_groups(ext, prev, first_group):
        last_row = lax.broadcasted_iota(jnp.int32, (8, ext.shape[-1]), 0) == 7
        for g in range(first_group, nv):
            rows = slice(8 * g, 8 * g + 8)
            mixed = jnp.where(last_row, prev[rows, :], ext[tm + 8 * g:tm + 8 * g + 8, :])
            ext[rows, :] = pltpu.roll(mixed, 1, axis=0)
            prev[rows, :] = ext[tm + 8 * g:tm + 8 * g + 8, :]

    delayed_groups(u_ext, u_prev, nv - (max(POOL_WINDOWS) - 1))

    def conv_column(lc, carry):
        a_col = a_ext.at[lc]
        delayed_groups(a_col, a_prev.at[lc], nv - (CONV_WIDTH - 1))
        w_col = wdw_ref.at[lc]
        for g0 in range(0, nv, CONV_MG):
            acc = None
            for k in range(CONV_WIDTH):
                src = nv + g0 + k - (CONV_WIDTH - 1)
                term = a_col[8 * src:8 * (src + CONV_MG), :] * w_col[k:k + 1, :]
                acc = term if acc is None else acc + term
            c_buf[lc, 8 * g0:8 * (g0 + CONV_MG), :] = acc + bdw_ref[lc]
        return carry
    lax.fori_loop(0, D_CONV // LANE, conv_column, 0)

    c = jnp.concatenate([c_buf[lc] for lc in range(D_CONV // LANE)], axis=-1)
    mu = jnp.mean(c, axis=-1, keepdims=True)
    xc = c - mu
    var = jnp.mean(xc * xc, axis=-1, keepdims=True)
    y = xc * lax.rsqrt(var + NORM_EPS) * gcln_ref[...] + bcln_ref[...]
    y = y * jax.nn.sigmoid(y)
    branch_a = _dot(y.astype(BF16), wco_ref[...]) + bco_ref[...]

    for r0 in range(0, tm, ROW_CHUNK):
        row = r0 + lax.broadcasted_iota(jnp.int32, (ROW_CHUNK, POOL_GROUP), 0)
        t1 = i * tm + (row % 8) * nv + row // 8 + 1
        for gi, w in enumerate(POOL_WINDOWS):
            ls = slice(gi * POOL_GROUP, (gi + 1) * POOL_GROUP)
            tok = u_ext[tm + r0:tm + r0 + ROW_CHUNK, ls]
            s = tok
            for j in range(1, w):
                s = s + u_ext[tm + r0 - 8 * j:tm + r0 - 8 * j + ROW_CHUNK, ls]
            cnt = jnp.minimum(t1, w).astype(F32)
            q_buf[r0:r0 + ROW_CHUNK, ls] = s / cnt - tok

    qs_out = []
    for gi in range(len(POOL_WINDOWS)):
        ls = slice(gi * POOL_GROUP, (gi + 1) * POOL_GROUP)
        qs_out.append(_dot(q_buf[:, ls].astype(BF16), wpool_ref[gi]) * spool_ref[:, ls])
    branch_b = jnp.concatenate(qs_out, axis=-1)

    c2 = 2 * D_CONV + D_POOL
    gate_a = jax.nn.sigmoid(proj(c2, c2 + D_MODEL))
    gate_b = jax.nn.sigmoid(proj(c2 + D_MODEL, c2 + 2 * D_MODEL))
    merged = gate_a * branch_a + gate_b * branch_b
    merged = _dot(unperm_ref[...], merged.astype(BF16)).astype(BF16)
    x1 = x + _dot(merged, wout_ref[...])
    x1_ref[...] = x1
    h2_ref[...] = _rms(x1, gffn_ref[...]).astype(BF16)


def _const_spec(shape):
    n = len(shape)
    return pl.BlockSpec(shape, lambda i, _n=n: (0,) * _n, pipeline_mode=pl.Buffered(1))


def _mixer(x, seq_len, g_mix, w_in, b_in, w_dw, b_dw, g_cln, b_cln, w_co, b_co, w_pool, s_pool, w_out,
           g_ffn):
    t = x.shape[0]
    tm = MIX_TM
    assert seq_len % tm == 0 and MIX_NV >= CONV_WIDTH and MIX_NV >= max(POOL_WINDOWS)
    d_in = w_in.shape[1]
    row = pl.BlockSpec((tm, D_MODEL), lambda i: (i, 0))
    n_col = D_CONV // LANE
    w_dw = w_dw.reshape(CONV_WIDTH, n_col, LANE).transpose(1, 0, 2)
    b_dw = b_dw.reshape(n_col, 1, LANE)
    r = jnp.arange(tm)
    perm = ((r % 8) * MIX_NV + r // 8)[:, None] == jnp.arange(tm)[None, :]
    perm = perm.astype(BF16)
    return pl.pallas_call(
        functools.partial(_mixer_kernel, tiles_per_seq=seq_len // tm),
        grid=(t // tm,),
        in_specs=[
            row,
            _const_spec((1, D_MODEL)),
            _const_spec((D_MODEL, d_in)),
            _const_spec((1, d_in)),
            _const_spec((n_col, CONV_WIDTH, LANE)),
            _const_spec((n_col, 1, LANE)),
            _const_spec((1, D_CONV)),
            _const_spec((1, D_CONV)),
            _const_spec((D_CONV, D_MODEL)),
            _const_spec((1, D_MODEL)),
            _const_spec((len(POOL_WINDOWS), POOL_GROUP, POOL_GROUP)),
            _const_spec((1, D_POOL)),
            _const_spec((D_MODEL, D_MODEL)),
            _const_spec((1, D_MODEL)),
            _const_spec((tm, tm)),
            _const_spec((tm, tm)),
        ],
        out_specs=[row, row],
        out_shape=[jax.ShapeDtypeStruct((t, D_MODEL), F32),
                   jax.ShapeDtypeStruct((t, D_MODEL), BF16)],
        scratch_shapes=[
            pltpu.VMEM((n_col, 2 * tm, LANE), F32),
            pltpu.VMEM((n_col, tm, LANE), F32),
            pltpu.VMEM((2 * tm, D_POOL), F32),
            pltpu.VMEM((tm, D_POOL), F32),
            pltpu.VMEM((n_col, tm, LANE), F32),
            pltpu.VMEM((tm, D_POOL), F32),
        ],
        compiler_params=pltpu.CompilerParams(
            dimension_semantics=("arbitrary",), vmem_limit_bytes=VMEM_LIMIT),
        name="mixer",
    )(x, g_mix, w_in, b_in, w_dw, b_dw, g_cln, b_cln, w_co, b_co, w_pool, s_pool, w_out, g_ffn, perm, perm.T)


def _beats(v, other, other_is_later):
    v = jnp.broadcast_to(v, other.shape)
    return jnp.where(other_is_later, jnp.where(v >= other, 1, 0), jnp.where(v > other, 1, 0))


def _router_kernel(h2_ref, wrt_ref, br_ref, utri_ref, ltri_ref, gate_ref, rank_ref, pos_ref, cnt_ref):
    tm = ROUTER_TM
    logits = lax.dot_general(wrt_ref[...], h2_ref[...], (((1,), (1,)), ((), ())),
                             preferred_element_type=F32)
    scores = jax.nn.sigmoid(logits)
    sel = scores + br_ref[...]
    shape3 = (N_GROUPS, GROUP_SIZE, tm)
    sel3 = sel.reshape(shape3)
    scores3 = scores.reshape(shape3)
    neg_inf = jnp.float32(-jnp.inf)

    member = lax.broadcasted_iota(jnp.int32, shape3, 1)
    m1 = jnp.max(sel3, axis=1, keepdims=True)
    first = jnp.min(jnp.where(sel3 == m1, member, GROUP_SIZE), axis=1, keepdims=True)
    m2 = jnp.max(jnp.where(member == first, neg_inf, sel3), axis=1, keepdims=True)
    gscore = jnp.broadcast_to(m1 + m2, shape3)

    gidx = lax.broadcasted_iota(jnp.int32, shape3, 0)
    grank = jnp.zeros(shape3, jnp.int32)
    for j in range(N_GROUPS):
        sj = gscore[j:j + 1]
        grank = grank + _beats(sj, gscore, gidx > j)
    masked = jnp.where(grank < TOPK_GROUPS, sel3, neg_inf)

    eidx = gidx * GROUP_SIZE + member
    work = masked
    erank = jnp.full(shape3, TOP_K, jnp.int32)
    for k in range(TOP_K):
        best = jnp.max(jnp.max(work, axis=0, keepdims=True), axis=1, keepdims=True)
        cand = jnp.where(work == best, eidx, N_EXPERTS)
        pick = jnp.min(jnp.min(cand, axis=0, keepdims=True), axis=1, keepdims=True)
        hit = eidx == pick
        work = jnp.where(hit, neg_inf, work)
        erank = jnp.where(hit, k, erank)
    chosen = erank < TOP_K
    top_s = jnp.where(chosen, scores3, 0.0)
    denom = jnp.sum(jnp.sum(top_s, axis=0, keepdims=True), axis=1, keepdims=True)
    gates3 = top_s / denom * ROUTED_SCALE
    chosen2 = jnp.where(chosen, 1.0, 0.0).reshape(N_EXPERTS, tm)
    gate_ref[...] = gates3.reshape(N_EXPERTS, tm).astype(BF16)

    for w in range(tm // WIN):
        ls = slice(w * WIN, (w + 1) * WIN)
        mw = chosen2[:, ls]
        rank = _dot(mw.astype(BF16), utri_ref[...])
        n = jnp.sum(mw, axis=1, keepdims=True)
        run = jnp.floor((n + 7.0) * 0.125) * 8.0
        start = _dot(ltri_ref[...], jnp.broadcast_to(run, (N_EXPERTS, WIN)).astype(BF16))
        rank_ref[:, ls] = jnp.where(mw > 0.5, rank, -1.0).astype(BF16)
        row3 = (rank + start).reshape(N_GROUPS, GROUP_SIZE, WIN)
        er = erank[:, :, ls]
        for k in range(TOP_K):
            pk = jnp.sum(jnp.sum(jnp.where(er == k, row3, 0.0), axis=0, keepdims=True), axis=1, keepdims=True)
            pos_ref[k:k + 1, ls] = pk.reshape(1, WIN).astype(jnp.int32)
        cnt_ref[w] = n


def _router(h2, w_rt, b_r):
    t = h2.shape[0]
    tm = ROUTER_TM
    utri = jnp.triu(jnp.ones((WIN, WIN), BF16), k=1)
    ltri = jnp.tril(jnp.ones((N_EXPERTS, N_EXPERTS), BF16), k=-1)
    return pl.pallas_call(
        _router_kernel,
        grid=(t // tm,),
        in_specs=[
            pl.BlockSpec((tm, D_MODEL), lambda i: (i, 0)),
            _const_spec((N_EXPERTS, D_MODEL)),
            _const_spec((N_EXPERTS, 1)),
            _const_spec((WIN, WIN)),
            _const_spec((N_EXPERTS, N_EXPERTS)),
        ],
        out_specs=[
            pl.BlockSpec((N_EXPERTS, tm), lambda i: (0, i)),
            pl.BlockSpec((N_EXPERTS, tm), lambda i: (0, i)),
            pl.BlockSpec((TOP_K, tm), lambda i: (0, i)),
            pl.BlockSpec((tm // WIN, N_EXPERTS, 1), lambda i: (i, 0, 0)),
        ],
        out_shape=[
            jax.ShapeDtypeStruct((N_EXPERTS, t), BF16),
            jax.ShapeDtypeStruct((N_EXPERTS, t), BF16),
            jax.ShapeDtypeStruct((TOP_K, t), jnp.int32),
            jax.ShapeDtypeStruct((t // WIN, N_EXPERTS, 1), F32),
        ],
        compiler_params=pltpu.CompilerParams(
            dimension_semantics=("arbitrary",), vmem_limit_bytes=VMEM_LIMIT),
        name="router",
    )(h2, w_rt, b_r, utri, ltri)


def _sorted_rows_bound(t):
    rows = t * TOP_K + (t // WIN) * N_EXPERTS * 7 + N_EXPERTS * (EXP_BM - 1)
    blocks = -(-rows // EXP_BM)
    return (blocks + blocks % 2) * EXP_BM


def _dispatch_plan(cnt, t):
    nw = t // WIN
    n = cnt.reshape(nw, N_EXPERTS).astype(jnp.int32)
    run = (n + 7) // 8 * 8
    local_end = jnp.cumsum(run, axis=1)
    local_off = jnp.concatenate([jnp.zeros((nw, 1), jnp.int32), local_end], axis=1)
    total = jnp.sum(run, axis=0)
    region = (total + EXP_BM - 1) // EXP_BM * EXP_BM
    eid = jnp.arange(N_EXPERTS, dtype=jnp.int32)
    last_owner = jnp.max(jnp.where(region > 0, eid, 0))
    odd = (jnp.sum(region) // EXP_BM) % 2
    region = region + jnp.where(eid == last_owner, odd * EXP_BM, 0)
    region_end = jnp.cumsum(region)
    base = region_end - region
    global_off = base[None, :] + jnp.cumsum(run, axis=0) - run
    n_blocks = _sorted_rows_bound(t) // EXP_BM
    n_used = region_end[-1] // EXP_BM
    blk = jnp.arange(n_blocks, dtype=jnp.int32)
    blk_expert = jnp.sum((region_end[None, :] <= blk[:, None] * EXP_BM).astype(jnp.int32), axis=1)
    blk_expert = jnp.minimum(blk_expert, N_EXPERTS - 1)
    later_nonempty = (eid[None, :] > eid[:, None]) & (region[None, :] > 0)
    next_expert = jnp.min(jnp.where(later_nonempty, eid[None, :], N_EXPERTS), axis=1).astype(jnp.int32)
    return dict(
        run_lo=local_off[:, :N_EXPERTS].reshape(nw, N_EXPERTS, 1),
        run_hi=local_off[:, 1:].reshape(nw, N_EXPERTS, 1),
        local_off=local_off.reshape(-1),
        runs=jnp.stack([local_off[:, :N_EXPERTS], run, global_off, jnp.zeros_like(run)], axis=-1).reshape(-1),
        fill_off=base + total, fill_cnt=region - total,
        blk_expert=blk_expert.astype(jnp.int32), next_expert=next_expert,
        n_used=n_used.reshape(1).astype(jnp.int32))


def _run_copy(runs_ref, win, e, vmem_buf, slot, hbm_buf, sem, to_hbm):
    p = (win * N_EXPERTS + e) * RUN_FIELDS
    lo = pl.multiple_of(runs_ref[p], 8)
    cnt = pl.multiple_of(runs_ref[p + 1], 8)
    go = pl.multiple_of(runs_ref[p + 2], 8)
    v = vmem_buf.at[pl.ds(pl.multiple_of(slot * SEL_ROWS + lo, 8), cnt)]
    h = hbm_buf.at[pl.ds(go, cnt)]
    cp = pltpu.make_async_copy(v, h, sem.at[slot]) if to_hbm else pltpu.make_async_copy(h, v, sem.at[slot])
    return cnt, cp


def _start_runs(runs_ref, win, vmem_buf, slot, hbm_buf, sem, to_hbm):
    def body(i, carry):
        copies = [_run_copy(runs_ref, win, i * RUN_UNROLL + j, vmem_buf, slot, hbm_buf, sem, to_hbm)
                  for j in range(RUN_UNROLL)]
        for cnt, cp in copies:
            @pl.when(cnt > 0)
            def _(cp=cp):
                cp.start()
        return carry
    lax.fori_loop(0, N_EXPERTS // RUN_UNROLL, body, 0)


def _wait_runs(local_ref, win, vmem_buf, slot, hbm_buf, sem, to_hbm):
    total = pl.multiple_of(local_ref[win * (N_EXPERTS + 1) + N_EXPERTS], 8)
    v = vmem_buf.at[pl.ds(pl.multiple_of(slot * SEL_ROWS, 8), total)]
    h = hbm_buf.at[pl.ds(0, total)]
    cp = pltpu.make_async_copy(v, h, sem.at[slot]) if to_hbm else pltpu.make_async_copy(h, v, sem.at[slot])

    @pl.when(total > 0)
    def _():
        cp.wait()


def _dispatch_kernel(local_ref, runs_ref, fill_off_ref, fill_cnt_ref, h2_ref, pos_ref, xs_hbm, sbuf,
                     s_ref, sem, zsem, *, n_win):
    w = pl.program_id(0)
    slot = w % 2
    pos = pos_ref[...]

    h2 = h2_ref[...]
    assert SEL_RG <= 256
    rid_b = lax.broadcasted_iota(jnp.int32, (SEL_RG, WIN), 0).astype(F32).astype(BF16)
    one_b = jnp.ones((SEL_RG, WIN), BF16)
    def compact(first, n_rows):
        for r0 in range(first, first + n_rows, SEL_RG):
            acc = jnp.zeros((SEL_RG, WIN), BF16)
            for k in range(TOP_K):
                off = (pos[k:k + 1, :] - r0).astype(F32)
                off = jnp.broadcast_to(off, (SEL_RG, WIN)).astype(BF16)
                acc = jnp.where(rid_b == off, one_b, acc)
            s_ref[r0:r0 + SEL_RG, :] = acc
        dst = pl.multiple_of(slot * SEL_ROWS + first, SEL_TAIL)
        sbuf[pl.ds(dst, n_rows), :] = _dot(s_ref[first:first + n_rows, :], h2).astype(BF16)

    @pl.when(w == 0)
    def _():
        for s in range(2):
            sbuf[(s + 1) * SEL_ROWS - SEL_TAIL:(s + 1) * SEL_ROWS, :] = jnp.zeros((SEL_TAIL, D_MODEL), BF16)

    for first in range(0, SEL_ROWS - 2 * SEL_TAIL, SEL_MM):
        compact(first, SEL_MM)
    compact(SEL_ROWS - 2 * SEL_TAIL, SEL_TAIL)

    @pl.when(local_ref[w * (N_EXPERTS + 1) + N_EXPERTS] > SEL_ROWS - SEL_TAIL)
    def _():
        compact(SEL_ROWS - SEL_TAIL, SEL_TAIL)

    _start_runs(runs_ref, w, sbuf, slot, xs_hbm, sem, True)

    @pl.when(w > 0)
    def _():
        _wait_runs(local_ref, w - 1, sbuf, 1 - slot, xs_hbm, sem, True)

    @pl.when(w == n_win - 1)
    def _():
        sbuf[2 * SEL_ROWS:, :] = jnp.zeros((2 * EXP_BM, D_MODEL), BF16)

        def fill(e, wait):
            cnt = pl.multiple_of(fill_cnt_ref[e], 8)
            off = pl.multiple_of(fill_off_ref[e], 8)
            cp = pltpu.make_async_copy(sbuf.at[pl.ds(2 * SEL_ROWS, cnt)], xs_hbm.at[pl.ds(off, cnt)], zsem)

            @pl.when(cnt > 0)
            def _():
                if wait:
                    cp.wait()
                else:
                    cp.start()

        def start_body(e, carry):
            fill(e, False)
            return carry

        def wait_body(e, carry):
            fill(e, True)
            return carry
        lax.fori_loop(0, N_EXPERTS, start_body, 0)
        _wait_runs(local_ref, w, sbuf, slot, xs_hbm, sem, True)
        lax.fori_loop(0, N_EXPERTS, wait_body, 0)


def _staging_shape(extra_rows):
    return jax.ShapeDtypeStruct((2 * SEL_ROWS + extra_rows, D_MODEL), BF16)


def _staging_spec(extra_rows):
    return pl.BlockSpec((2 * SEL_ROWS + extra_rows, D_MODEL), lambda w, *_: (0, 0))


def _dispatch(plan, h2, pos):
    t = h2.shape[0]
    n_win = t // WIN
    return pl.pallas_call(
        functools.partial(_dispatch_kernel, n_win=n_win),
        grid_spec=pltpu.PrefetchScalarGridSpec(
            num_scalar_prefetch=4,
            grid=(n_win,),
            in_specs=[
                pl.BlockSpec((WIN, D_MODEL), lambda w, *_: (w, 0)),
                pl.BlockSpec((TOP_K, WIN), lambda w, *_: (0, w)),
            ],
            out_specs=[pl.BlockSpec(memory_space=pl.ANY), _staging_spec(2 * EXP_BM)],
            scratch_shapes=[
                pltpu.VMEM((SEL_ROWS, WIN), BF16),
                pltpu.SemaphoreType.DMA((2,)),
                pltpu.SemaphoreType.DMA,
            ]),
        out_shape=[jax.ShapeDtypeStruct((_sorted_rows_bound(t), D_MODEL), BF16), _staging_shape(2 * EXP_BM)],
        compiler_params=pltpu.CompilerParams(
            dimension_semantics=("arbitrary",), vmem_limit_bytes=VMEM_LIMIT),
        name="dispatch",
    )(plan['local_off'], plan['runs'], plan['fill_off'], plan['fill_cnt'], h2, pos)[0]


def _expert_kernel(blk_expert_ref, next_expert_ref, n_used_ref, xs_hbm, wg_hbm, wu_hbm, wd_hbm, ys_hbm,
                   xbuf, ybuf, wg_st, wu_st, wd_st, wg_bf, wu_bf, wd_bf, xsem, ysem, wsem):
    n_used = n_used_ref[0]
    part = EXP_BM // EXP_SPLIT

    def row_copies(b, slot, fetch):
        out = []
        for q in range(EXP_SPLIT):
            hbm_rows = pl.ds(pl.multiple_of(b * EXP_BM + q * part, part), part)
            if fetch:
                out.append(pltpu.make_async_copy(xs_hbm.at[hbm_rows], xbuf.at[slot, q * part:(q + 1) * part],
                                                 xsem.at[slot]))
            else:
                out.append(pltpu.make_async_copy(ybuf.at[slot, q * part:(q + 1) * part], ys_hbm.at[hbm_rows],
                                                 ysem.at[slot]))
        return out

    def weight_copies(e, slot):
        return [pltpu.make_async_copy(wg_hbm.at[e], wg_st.at[slot], wsem.at[slot]),
                pltpu.make_async_copy(wu_hbm.at[e], wu_st.at[slot], wsem.at[slot]),
                pltpu.make_async_copy(wd_hbm.at[e], wd_st.at[slot], wsem.at[slot])]

    def start(copies):
        for c in copies:
            c.start()

    def wait(copies):
        for c in copies:
            c.wait()

    for ahead in range(EXP_XDEPTH - 2):
        @pl.when(ahead < n_used)
        def _(ahead=ahead):
            start(row_copies(ahead, ahead, True))

    @pl.when(n_used > 0)
    def _():
        start(weight_copies(blk_expert_ref[0], 0))

    def enter_block(b, wset):
        e = blk_expert_ref[b]
        new_expert = jnp.logical_or(b == 0, e != blk_expert_ref[jnp.maximum(b - 1, 0)])
        wset = jnp.where(new_expert, 1 - wset, wset)

        @pl.when(new_expert)
        def _():
            wait(weight_copies(e, wset))
            wg_bf[wset] = wg_st[wset].astype(BF16)
            wu_bf[wset] = wu_st[wset].astype(BF16)
            wd_bf[wset] = wd_st[wset].astype(BF16)
            nxt = next_expert_ref[e]

            @pl.when(nxt < N_EXPERTS)
            def _():
                start(weight_copies(nxt, 1 - wset))

        ahead = b + EXP_XDEPTH - 2

        @pl.when(ahead < n_used)
        def _():
            start(row_copies(ahead, ahead % EXP_XDEPTH, True))

        wait(row_copies(b, b % EXP_XDEPTH, True))

        @pl.when(b >= EXP_YDEPTH)
        def _():
            wait(row_copies(b - EXP_YDEPTH, b % EXP_YDEPTH, False))
        return wset

    def compute(b, wset):
        x = xbuf[b % EXP_XDEPTH]
        hg = _dot(x, wg_bf[wset])
        hb = hg * jax.nn.sigmoid(hg) * _dot(x, wu_bf[wset])
        ybuf[b % EXP_YDEPTH] = _dot(hb.astype(BF16), wd_bf[wset]).astype(BF16)

    def body(p, wset):
        b0 = 2 * p
        w0 = enter_block(b0, wset)
        w1 = enter_block(b0 + 1, w0)
        compute(b0, w0)
        compute(b0 + 1, w1)
        start(row_copies(b0, b0 % EXP_YDEPTH, False))
        start(row_copies(b0 + 1, (b0 + 1) % EXP_YDEPTH, False))
        return w1

    lax.fori_loop(0, n_used // 2, body, jnp.int32(1))

    for back in range(EXP_YDEPTH, 0, -1):
        @pl.when(n_used >= back)
        def _(back=back):
            wait(row_copies(n_used - back, (n_used - back) % EXP_YDEPTH, False))


def _experts(plan, xs, w_gate, w_up, w_down):
    any_spec = pl.BlockSpec(memory_space=pl.ANY)
    return pl.pallas_call(
        _expert_kernel,
        grid_spec=pltpu.PrefetchScalarGridSpec(
            num_scalar_prefetch=3,
            grid=(1,),
            in_specs=[any_spec, any_spec, any_spec, any_spec],
            out_specs=any_spec,
            scratch_shapes=[
                pltpu.VMEM((EXP_XDEPTH, EXP_BM, D_MODEL), BF16),
                pltpu.VMEM((EXP_YDEPTH, EXP_BM, D_MODEL), BF16),
                pltpu.VMEM((2, D_MODEL, D_EXPERT), F32),
                pltpu.VMEM((2, D_MODEL, D_EXPERT), F32),
                pltpu.VMEM((2, D_EXPERT, D_MODEL), F32),
                pltpu.VMEM((2, D_MODEL, D_EXPERT), BF16),
                pltpu.VMEM((2, D_MODEL, D_EXPERT), BF16),
                pltpu.VMEM((2, D_EXPERT, D_MODEL), BF16),
                pltpu.SemaphoreType.DMA((EXP_XDEPTH,)),
                pltpu.SemaphoreType.DMA((EXP_YDEPTH,)),
                pltpu.SemaphoreType.DMA((2,)),
            ]),
        out_shape=jax.ShapeDtypeStruct(xs.shape, BF16),
        compiler_params=pltpu.CompilerParams(
            dimension_semantics=("arbitrary",), vmem_limit_bytes=VMEM_LIMIT),
        name="experts",
    )(plan['blk_expert'], plan['next_expert'], plan['n_used'], xs, w_gate, w_up, w_down)


def _combine_kernel(local_ref, runs_ref, x1_ref, h2_ref, p_ref, rank_ref, gate_ref, lo_ref, hi_ref,
                    wsg_ref, wsu_ref, wsd_ref, gple_ref, wpg_ref, wp_ref, gfin_ref, ys_hbm, o_ref, ybuf, st_ref, sem,
                    *, n_win, final_norm):
    w = pl.program_id(0)
    slot = w % 2

    @pl.when(w == 0)
    def _():
        ybuf[...] = jnp.zeros(ybuf.shape, BF16)
        _start_runs(runs_ref, w, ybuf, slot, ys_hbm, sem, False)

    @pl.when(w + 1 < n_win)
    def _():
        _start_runs(runs_ref, w + 1, ybuf, 1 - slot, ys_hbm, sem, False)

    lo = lo_ref[0]
    hi = hi_ref[0]
    lo_f = lo.astype(F32)
    rank_tbl = rank_ref[...]
    gate_tbl = gate_ref[...]

    def build_group(lg):
        cols = slice(lg * CMB_LG, (lg + 1) * CMB_LG)
        rid = lg * CMB_LG + lax.broadcasted_iota(jnp.int32, (N_EXPERTS, CMB_LG), 1)
        owner = jnp.where(rid >= lo, jnp.where(rid < hi, 1.0, 0.0), 0.0)
        run_row = rid[0:1, :].astype(F32) - jnp.sum(owner * lo_f, axis=0, keepdims=True)
        owner = owner.astype(BF16)
        hit = _dot_t(rank_tbl, owner) == run_row
        st_ref[:, cols] = jnp.where(hit, _dot_t(gate_tbl, owner), 0.0).astype(BF16)

    build_group(0)
    h2 = h2_ref[...]
    hs = _dot(h2, wsg_ref[...])
    hs = hs * jax.nn.sigmoid(hs) * _dot(h2, wsu_ref[...])
    shared = _dot(hs.astype(BF16), wsd_ref[...])

    _wait_runs(local_ref, w, ybuf, slot, ys_hbm, sem, False)
    routed = None
    n_groups = SEL_ROWS // CMB_LG
    for lg in range(n_groups):
        if lg + 1 < n_groups:
            build_group(lg + 1)
        src = pl.multiple_of(slot * SEL_ROWS + lg * CMB_LG, CMB_LG)
        part = _dot(st_ref[:, lg * CMB_LG:(lg + 1) * CMB_LG], ybuf[pl.ds(src, CMB_LG), :])
        routed = part if routed is None else routed + part
    x2 = x1_ref[...] + routed + shared

    hp = _rms(x2, gple_ref[...]).astype(BF16)
    gate = jax.nn.sigmoid(_dot(hp, wpg_ref[...]))
    x3 = x2 + gate * _dot(p_ref[...].astype(BF16), wp_ref[...])
    o_ref[...] = _rms(x3, gfin_ref[...]) if final_norm else x3


def _combine(plan, ys, x1, h2, p, rank_tbl, gate_tbl, wsg, wsu, wsd, g_ple, w_pg, w_p, g_fin, final_norm):
    t = x1.shape[0]
    n_win = t // WIN
    row = lambda width: pl.BlockSpec((WIN, width), lambda w, *_: (w, 0))
    const = lambda shape: pl.BlockSpec(shape, lambda w, *_: (0,) * len(shape))
    return pl.pallas_call(
        functools.partial(_combine_kernel, n_win=n_win, final_norm=final_norm),
        grid_spec=pltpu.PrefetchScalarGridSpec(
            num_scalar_prefetch=2,
            grid=(n_win,),
            in_specs=[
                row(D_MODEL), row(D_MODEL), row(PLE_DIM),
                pl.BlockSpec((N_EXPERTS, WIN), lambda w, *_: (0, w)),
                pl.BlockSpec((N_EXPERTS, WIN), lambda w, *_: (0, w)),
                pl.BlockSpec((1, N_EXPERTS, 1), lambda w, *_: (w, 0, 0)),
                pl.BlockSpec((1, N_EXPERTS, 1), lambda w, *_: (w, 0, 0)),
                const((D_MODEL, D_EXPERT)), const((D_MODEL, D_EXPERT)), const((D_EXPERT, D_MODEL)),
                const((1, D_MODEL)), const((D_MODEL, D_MODEL)), const((PLE_DIM, D_MODEL)),
                const((1, D_MODEL)),
                pl.BlockSpec(memory_space=pl.ANY),
            ],
            out_specs=[row(D_MODEL), _staging_spec(0)],
            scratch_shapes=[
                pltpu.VMEM((WIN, SEL_ROWS), BF16),
                pltpu.SemaphoreType.DMA((2,)),
            ]),
        out_shape=[jax.ShapeDtypeStruct((t, D_MODEL), F32), _staging_shape(0)],
        compiler_params=pltpu.CompilerParams(
            dimension_semantics=("arbitrary",), vmem_limit_bytes=VMEM_LIMIT),
        name="combine",
    )(plan['local_off'], plan['runs'], x1, h2, p, rank_tbl, gate_tbl, plan['run_lo'], plan['run_hi'],
      wsg, wsu, wsd, g_ple, w_pg, w_p, g_fin, ys)[0]


def kernel(x, p, g_mix, w_in, b_in, w_dw, b_dw, g_cln, b_cln, w_conv_out, b_conv_out, w_pool, s_pool,
           w_out, g_ffn, w_router, b_router, w_e_gate, w_e_up, w_e_down, w_s_gate, w_s_up, w_s_down,
           g_ple, w_ple_gate, w_ple, g_final):
    bsz, s, d = x.shape
    t = bsz * s
    depth = w_in.shape[0]
    xt = x.reshape(t, d)
    row = lambda v: v.reshape(1, -1)
    for i in range(depth):
        x1, h2 = _mixer(
            xt, s, row(g_mix[i]), w_in[i].astype(BF16), row(b_in[i]), w_dw[i], row(b_dw[i]),
            row(g_cln[i]), row(b_cln[i]), w_conv_out[i].astype(BF16), row(b_conv_out[i]),
            w_pool[i].astype(BF16), row(s_pool[i]), w_out[i].astype(BF16), row(g_ffn[i]))
        gate, rank, pos, cnt = _router(h2, w_router[i].T.astype(BF16), b_router[i].reshape(N_EXPERTS, 1))
        plan = _dispatch_plan(cnt, t)
        xs = _dispatch(plan, h2, pos)
        ys = _experts(plan, xs, w_e_gate[i], w_e_up[i], w_e_down[i])
        xt = _combine(
            plan, ys, x1, h2, p[i].reshape(t, PLE_DIM), rank, gate,
            w_s_gate[i].astype(BF16), w_s_up[i].astype(BF16), w_s_down[i].astype(BF16),
            row(g_ple[i]), w_ple_gate[i].astype(BF16), w_ple[i].astype(BF16), row(g_final),
            final_norm=(i == depth - 1))
    return xt.reshape(bsz, s, d)
```

```python
import functools

import jax
import jax.numpy as jnp
from jax import lax
from jax.experimental import pallas as pl
from jax.experimental.pallas import tpu as pltpu

D_MODEL = 1024
D_CONV = 1024
D_POOL = 1024
CONV_WIDTH = 31
POOL_WINDOWS = (2, 4, 8, 16)
POOL_GROUP = 256
PLE_DIM = 256
N_EXPERTS = 64
N_GROUPS = 8
GROUP_SIZE = N_EXPERTS // N_GROUPS
TOPK_GROUPS = 4
TOP_K = 8
D_EXPERT = 256
ROUTED_SCALE = 2.5
NORM_EPS = 1e-6

F32 = jnp.float32
BF16 = jnp.bfloat16

MIX_TM = 512
MIX_NV = MIX_TM // 8
CONV_MG = 8
ROW_CHUNK = 64
LANE = 128

ROUTER_TM = 1024
WIN = 256
SEL_ROWS = 2560
SEL_RG = 64
SEL_MM = 512
SEL_TAIL = 256
EXP_BM = 576
EXP_XDEPTH = 6
EXP_YDEPTH = 4
EXP_SPLIT = 4
RUN_FIELDS = 4
RUN_UNROLL = 4
CMB_LG = 512

V7X_VMEM_BYTES = 64 * 1024 * 1024
VMEM_LIMIT = V7X_VMEM_BYTES - 8 * 1024 * 1024

assert SEL_ROWS >= TOP_K * WIN + 7 * N_EXPERTS
assert (SEL_ROWS - 2 * SEL_TAIL) % SEL_MM == 0 and SEL_TAIL % SEL_RG == 0 and SEL_ROWS % CMB_LG == 0
assert N_EXPERTS % RUN_UNROLL == 0 and EXP_BM % (16 * EXP_SPLIT) == 0 and EXP_XDEPTH > 2 and EXP_YDEPTH >= 2


def _rms(x, g):
    ms = jnp.mean(x * x, axis=-1, keepdims=True)
    return x * lax.rsqrt(ms + NORM_EPS) * g


def _dot(a, b):
    return jnp.dot(a, b, preferred_element_type=F32)


def _dot_t(a, b):
    return lax.dot_general(a, b, (((0,), (0,)), ((), ())), preferred_element_type=F32)


def _mixer_kernel(x_ref, gmix_ref, win_ref, bin_ref, wdw_ref, bdw_ref, gcln_ref, bcln_ref,
                  wco_ref, bco_ref, wpool_ref, spool_ref, wout_ref, gffn_ref, perm_ref, unperm_ref,
                  x1_ref, h2_ref, a_ext, a_prev, u_ext, u_prev, c_buf, q_buf, *, tiles_per_seq):
    i = pl.program_id(0) % tiles_per_seq
    tm = MIX_TM
    nv = MIX_NV

    @pl.when(i == 0)
    def _():
        a_prev[...] = jnp.zeros(a_prev.shape, F32)
        u_prev[...] = jnp.zeros(u_prev.shape, F32)

    x = x_ref[...]
    h = _dot(perm_ref[...], _rms(x, gmix_ref[...]).astype(BF16)).astype(BF16)

    def proj(lo, hi):
        return _dot(h, win_ref[:, lo:hi]) + bin_ref[:, lo:hi]

    glu = proj(0, D_CONV) * jax.nn.sigmoid(proj(D_CONV, 2 * D_CONV))
    for lc in range(D_CONV // LANE):
        a_ext[lc, tm:2 * tm, :] = glu[:, lc * LANE:(lc + 1) * LANE]
    u_ext[tm:2 * tm, :] = proj(2 * D_CONV, 2 * D_CONV + D_POOL)

    def delayed_groups(ext, prev, first_group):
        last_row = lax.broadcasted_iota(jnp.int32, (8, ext.shape[-1]), 0) == 7
        for g in range(first_group, nv):
            rows = slice(8 * g, 8 * g + 8)
            mixed = jnp.where(last_row, prev[rows, :], ext[tm + 8 * g:tm + 8 * g + 8, :])
            ext[rows, :] = pltpu.roll(mixed, 1, axis=0)
            prev[rows, :] = ext[tm + 8 * g:tm + 8 * g + 8, :]

    delayed_groups(u_ext, u_prev, nv - (max(POOL_WINDOWS) - 1))

    def conv_column(lc, carry):
        a_col = a_ext.at[lc]
        delayed_groups(a_col, a_prev.at[lc], nv - (CONV_WIDTH - 1))
        w_col = wdw_ref.at[lc]
        for g0 in range(0, nv, CONV_MG):
            acc = None
            for k in range(CONV_WIDTH):
                src = nv + g0 + k - (CONV_WIDTH - 1)
                term = a_col[8 * src:8 * (src + CONV_MG), :] * w_col[k:k + 1, :]
                acc = term if acc is None else acc + term
            c_buf[lc, 8 * g0:8 * (g0 + CONV_MG), :] = acc + bdw_ref[lc]
        return carry
    lax.fori_loop(0, D_CONV // LANE, conv_column, 0)

    for r0 in range(0, tm, ROW_CHUNK):
        row = r0 + lax.broadcasted_iota(jnp.int32, (ROW_CHUNK, POOL_GROUP), 0)
        t1 = i * tm + (row % 8) * nv + row // 8 + 1
        for gi, w in enumerate(POOL_WINDOWS):
            ls = slice(gi * POOL_GROUP, (gi + 1) * POOL_GROUP)
            tok = u_ext[tm + r0:tm + r0 + ROW_CHUNK, ls]
            s = tok
            for j in range(1, w):
                s = s + u_ext[tm + r0 - 8 * j:tm + r0 - 8 * j + ROW_CHUNK, ls]
            cnt = jnp.minimum(t1, w).astype(F32)
            q_buf[r0:r0 + ROW_CHUNK, ls] = s / cnt - tok

    qs_out = []
    for gi in range(len(POOL_WINDOWS)):
        ls = slice(gi * POOL_GROUP, (gi + 1) * POOL_GROUP)
        qs_out.append(_dot(q_buf[:, ls].astype(BF16), wpool_ref[gi]) * spool_ref[:, ls])
    branch_b = jnp.concatenate(qs_out, axis=-1)

    c2 = 2 * D_CONV + D_POOL
    gate_a = jax.nn.sigmoid(proj(c2, c2 + D_MODEL))
    gate_b = jax.nn.sigmoid(proj(c2 + D_MODEL, c2 + 2 * D_MODEL))

    c = jnp.concatenate([c_buf[lc] for lc in range(D_CONV // LANE)], axis=-1)
    mu = jnp.mean(c, axis=-1, keepdims=True)
    xc = c - mu
    var = jnp.mean(xc * xc, axis=-1, keepdims=True)
    y = xc * lax.rsqrt(var + NORM_EPS) * gcln_ref[...] + bcln_ref[...]
    y = y * jax.nn.sigmoid(y)
    branch_a = _dot(y.astype(BF16), wco_ref[...]) + bco_ref[...]

    merged = gate_a * branch_a + gate_b * branch_b
    merged = _dot(unperm_ref[...], merged.astype(BF16)).astype(BF16)
    x1 = x + _dot(merged, wout_ref[...])
    x1_ref[...] = x1
    h2_ref[...] = _rms(x1, gffn_ref[...]).astype(BF16)


def _const_spec(shape):
    n = len(shape)
    return pl.BlockSpec(shape, lambda i, _n=n: (0,) * _n, pipeline_mode=pl.Buffered(1))


def _mixer(x, seq_len, g_mix, w_in, b_in, w_dw, b_dw, g_cln, b_cln, w_co, b_co, w_pool, s_pool, w_out,
           g_ffn):
    t = x.shape[0]
    tm = MIX_TM
    assert seq_len % tm == 0 and MIX_NV >= CONV_WIDTH and MIX_NV >= max(POOL_WINDOWS)
    d_in = w_in.shape[1]
    row = pl.BlockSpec((tm, D_MODEL), lambda i: (i, 0))
    n_col = D_CONV // LANE
    w_dw = w_dw.reshape(CONV_WIDTH, n_col, LANE).transpose(1, 0, 2)
    b_dw = b_dw.reshape(n_col, 1, LANE)
    r = jnp.arange(tm)
    perm = ((r % 8) * MIX_NV + r // 8)[:, None] == jnp.arange(tm)[None, :]
    perm = perm.astype(BF16)
    return pl.pallas_call(
        functools.partial(_mixer_kernel, tiles_per_seq=seq_len // tm),
        grid=(t // tm,),
        in_specs=[
            row,
            _const_spec((1, D_MODEL)),
            _const_spec((D_MODEL, d_in)),
            _const_spec((1, d_in)),
            _const_spec((n_col, CONV_WIDTH, LANE)),
            _const_spec((n_col, 1, LANE)),
            _const_spec((1, D_CONV)),
            _const_spec((1, D_CONV)),
            _const_spec((D_CONV, D_MODEL)),
            _const_spec((1, D_MODEL)),
            _const_spec((len(POOL_WINDOWS), POOL_GROUP, POOL_GROUP)),
            _const_spec((1, D_POOL)),
            _const_spec((D_MODEL, D_MODEL)),
            _const_spec((1, D_MODEL)),
            _const_spec((tm, tm)),
            _const_spec((tm, tm)),
        ],
        out_specs=[row, row],
        out_shape=[jax.ShapeDtypeStruct((t, D_MODEL), F32),
                   jax.ShapeDtypeStruct((t, D_MODEL), BF16)],
        scratch_shapes=[
            pltpu.VMEM((n_col, 2 * tm, LANE), F32),
            pltpu.VMEM((n_col, tm, LANE), F32),
            pltpu.VMEM((2 * tm, D_POOL), F32),
            pltpu.VMEM((tm, D_POOL), F32),
            pltpu.VMEM((n_col, tm, LANE), F32),
            pltpu.VMEM((tm, D_POOL), F32),
        ],
        compiler_params=pltpu.CompilerParams(
            dimension_semantics=("arbitrary",), vmem_limit_bytes=VMEM_LIMIT),
        name="mixer",
    )(x, g_mix, w_in, b_in, w_dw, b_dw, g_cln, b_cln, w_co, b_co, w_pool, s_pool, w_out, g_ffn, perm, perm.T)


def _beats(v, other, other_is_later):
    v = jnp.broadcast_to(v, other.shape)
    return jnp.where(other_is_later, jnp.where(v >= other, 1, 0), jnp.where(v > other, 1, 0))


def _router_kernel(h2_ref, wrt_ref, br_ref, utri_ref, ltri_ref, gate_ref, rank_ref, pos_ref, cnt_ref):
    tm = ROUTER_TM
    logits = lax.dot_general(wrt_ref[...], h2_ref[...], (((1,), (1,)), ((), ())),
                             preferred_element_type=F32)
    scores = jax.nn.sigmoid(logits)
    sel = scores + br_ref[...]
    shape3 = (N_GROUPS, GROUP_SIZE, tm)
    sel3 = sel.reshape(shape3)
    scores3 = scores.reshape(shape3)
    neg_inf = jnp.float32(-jnp.inf)

    member = lax.broadcasted_iota(jnp.int32, shape3, 1)
    m1 = jnp.max(sel3, axis=1, keepdims=True)
    first = jnp.min(jnp.where(sel3 == m1, member, GROUP_SIZE), axis=1, keepdims=True)
    m2 = jnp.max(jnp.where(member == first, neg_inf, sel3), axis=1, keepdims=True)
    gscore = jnp.broadcast_to(m1 + m2, shape3)

    gidx = lax.broadcasted_iota(jnp.int32, shape3, 0)
    grank = jnp.zeros(shape3, jnp.int32)
    for j in range(N_GROUPS):
        sj = gscore[j:j + 1]
        grank = grank + _beats(sj, gscore, gidx > j)
    masked = jnp.where(grank < TOPK_GROUPS, sel3, neg_inf)

    eidx = gidx * GROUP_SIZE + member
    work = masked
    erank = jnp.full(shape3, TOP_K, jnp.int32)
    for k in range(TOP_K):
        best = jnp.max(jnp.max(work, axis=0, keepdims=True), axis=1, keepdims=True)
        cand = jnp.where(work == best, eidx, N_EXPERTS)
        pick = jnp.min(jnp.min(cand, axis=0, keepdims=True), axis=1, keepdims=True)
        hit = eidx == pick
        work = jnp.where(hit, neg_inf, work)
        erank = jnp.where(hit, k, erank)
    chosen = erank < TOP_K
    top_s = jnp.where(chosen, scores3, 0.0)
    denom = jnp.sum(jnp.sum(top_s, axis=0, keepdims=True), axis=1, keepdims=True)
    gates3 = top_s / denom * ROUTED_SCALE
    chosen2 = jnp.where(chosen, 1.0, 0.0).reshape(N_EXPERTS, tm)
    gate_ref[...] = gates3.reshape(N_EXPERTS, tm).astype(BF16)

    for w in range(tm // WIN):
        ls = slice(w * WIN, (w + 1) * WIN)
        mw = chosen2[:, ls]
        rank = _dot(mw.astype(BF16), utri_ref[...])
        n = jnp.sum(mw, axis=1, keepdims=True)
        run = jnp.floor((n + 7.0) * 0.125) * 8.0
        start = _dot(ltri_ref[...], jnp.broadcast_to(run, (N_EXPERTS, WIN)).astype(BF16))
        rank_ref[:, ls] = jnp.where(mw > 0.5, rank, -1.0).astype(BF16)
        row3 = (rank + start).reshape(N_GROUPS, GROUP_SIZE, WIN)
        er = erank[:, :, ls]
        for k in range(TOP_K):
            pk = jnp.sum(jnp.sum(jnp.where(er == k, row3, 0.0), axis=0, keepdims=True), axis=1, keepdims=True)
            pos_ref[k:k + 1, ls] = pk.reshape(1, WIN).astype(jnp.int32)
        cnt_ref[w] = n


def _router(h2, w_rt, b_r):
    t = h2.shape[0]
    tm = ROUTER_TM
    utri = jnp.triu(jnp.ones((WIN, WIN), BF16), k=1)
    ltri = jnp.tril(jnp.ones((N_EXPERTS, N_EXPERTS), BF16), k=-1)
    return pl.pallas_call(
        _router_kernel,
        grid=(t // tm,),
        in_specs=[
            pl.BlockSpec((tm, D_MODEL), lambda i: (i, 0)),
            _const_spec((N_EXPERTS, D_MODEL)),
            _const_spec((N_EXPERTS, 1)),
            _const_spec((WIN, WIN)),
            _const_spec((N_EXPERTS, N_EXPERTS)),
        ],
        out_specs=[
            pl.BlockSpec((N_EXPERTS, tm), lambda i: (0, i)),
            pl.BlockSpec((N_EXPERTS, tm), lambda i: (0, i)),
            pl.BlockSpec((TOP_K, tm), lambda i: (0, i)),
            pl.BlockSpec((tm // WIN, N_EXPERTS, 1), lambda i: (i, 0, 0)),
        ],
        out_shape=[
            jax.ShapeDtypeStruct((N_EXPERTS, t), BF16),
            jax.ShapeDtypeStruct((N_EXPERTS, t), BF16),
            jax.ShapeDtypeStruct((TOP_K, t), jnp.int32),
            jax.ShapeDtypeStruct((t // WIN, N_EXPERTS, 1), F32),
        ],
        compiler_params=pltpu.CompilerParams(
            dimension_semantics=("arbitrary",), vmem_limit_bytes=VMEM_LIMIT),
        name="router",
    )(h2, w_rt, b_r, utri, ltri)


def _sorted_rows_bound(t):
    rows = t * TOP_K + (t // WIN) * N_EXPERTS * 7 + N_EXPERTS * (EXP_BM - 1)
    blocks = -(-rows // EXP_BM)
    return (blocks + blocks % 2) * EXP_BM


def _dispatch_plan(cnt, t):
    nw = t // WIN
    n = cnt.reshape(nw, N_EXPERTS).astype(jnp.int32)
    run = (n + 7) // 8 * 8
    local_end = jnp.cumsum(run, axis=1)
    local_off = jnp.concatenate([jnp.zeros((nw, 1), jnp.int32), local_end], axis=1)
    total = jnp.sum(run, axis=0)
    region = (total + EXP_BM - 1) // EXP_BM * EXP_BM
    eid = jnp.arange(N_EXPERTS, dtype=jnp.int32)
    last_owner = jnp.max(jnp.where(region > 0, eid, 0))
    odd = (jnp.sum(region) // EXP_BM) % 2
    region = region + jnp.where(eid == last_owner, odd * EXP_BM, 0)
    region_end = jnp.cumsum(region)
    base = region_end - region
    global_off = base[None, :] + jnp.cumsum(run, axis=0) - run
    n_blocks = _sorted_rows_bound(t) // EXP_BM
    n_used = region_end[-1] // EXP_BM
    blk = jnp.arange(n_blocks, dtype=jnp.int32)
    blk_expert = jnp.sum((region_end[None, :] <= blk[:, None] * EXP_BM).astype(jnp.int32), axis=1)
    blk_expert = jnp.minimum(blk_expert, N_EXPERTS - 1)
    later_nonempty = (eid[None, :] > eid[:, None]) & (region[None, :] > 0)
    next_expert = jnp.min(jnp.where(later_nonempty, eid[None, :], N_EXPERTS), axis=1).astype(jnp.int32)
    return dict(
        run_lo=local_off[:, :N_EXPERTS].reshape(nw, N_EXPERTS, 1),
        run_hi=local_off[:, 1:].reshape(nw, N_EXPERTS, 1),
        local_off=local_off.reshape(-1),
        runs=jnp.stack([local_off[:, :N_EXPERTS], run, global_off, jnp.zeros_like(run)], axis=-1).reshape(-1),
        fill_off=base + total, fill_cnt=region - total,
        blk_expert=blk_expert.astype(jnp.int32), next_expert=next_expert,
        n_used=n_used.reshape(1).astype(jnp.int32))


def _run_copy(runs_ref, win, e, vmem_buf, slot, hbm_buf, sem, to_hbm):
    p = (win * N_EXPERTS + e) * RUN_FIELDS
    lo = pl.multiple_of(runs_ref[p], 8)
    cnt = pl.multiple_of(runs_ref[p + 1], 8)
    go = pl.multiple_of(runs_ref[p + 2], 8)
    v = vmem_buf.at[pl.ds(pl.multiple_of(slot * SEL_ROWS + lo, 8), cnt)]
    h = hbm_buf.at[pl.ds(go, cnt)]
    cp = pltpu.make_async_copy(v, h, sem.at[slot]) if to_hbm else pltpu.make_async_copy(h, v, sem.at[slot])
    return cnt, cp


def _start_runs(runs_ref, win, vmem_buf, slot, hbm_buf, sem, to_hbm):
    def body(i, carry):
        copies = [_run_copy(runs_ref, win, i * RUN_UNROLL + j, vmem_buf, slot, hbm_buf, sem, to_hbm)
                  for j in range(RUN_UNROLL)]
        for cnt, cp in copies:
            @pl.when(cnt > 0)
            def _(cp=cp):
                cp.start()
        return carry
    lax.fori_loop(0, N_EXPERTS // RUN_UNROLL, body, 0)


def _wait_runs(local_ref, win, vmem_buf, slot, hbm_buf, sem, to_hbm):
    total = pl.multiple_of(local_ref[win * (N_EXPERTS + 1) + N_EXPERTS], 8)
    v = vmem_buf.at[pl.ds(pl.multiple_of(slot * SEL_ROWS, 8), total)]
    h = hbm_buf.at[pl.ds(0, total)]
    cp = pltpu.make_async_copy(v, h, sem.at[slot]) if to_hbm else pltpu.make_async_copy(h, v, sem.at[slot])

    @pl.when(total > 0)
    def _():
        cp.wait()


def _dispatch_kernel(local_ref, runs_ref, fill_off_ref, fill_cnt_ref, h2_ref, pos_ref, xs_hbm, sbuf,
                     s_ref, sem, zsem, *, n_win):
    w = pl.program_id(0)
    slot = w % 2
    pos = pos_ref[...]

    h2 = h2_ref[...]
    assert SEL_RG <= 256
    rid_b = lax.broadcasted_iota(jnp.int32, (SEL_RG, WIN), 0).astype(F32).astype(BF16)
    one_b = jnp.ones((SEL_RG, WIN), BF16)
    def compact(first, n_rows):
        for r0 in range(first, first + n_rows, SEL_RG):
            acc = jnp.zeros((SEL_RG, WIN), BF16)
            for k in range(TOP_K):
                off = (pos[k:k + 1, :] - r0).astype(F32)
                off = jnp.broadcast_to(off, (SEL_RG, WIN)).astype(BF16)
                acc = jnp.where(rid_b == off, one_b, acc)
            s_ref[r0:r0 + SEL_RG, :] = acc
        dst = pl.multiple_of(slot * SEL_ROWS + first, SEL_TAIL)
        sbuf[pl.ds(dst, n_rows), :] = _dot(s_ref[first:first + n_rows, :], h2).astype(BF16)

    @pl.when(w == 0)
    def _():
        for s in range(2):
            sbuf[(s + 1) * SEL_ROWS - SEL_TAIL:(s + 1) * SEL_ROWS, :] = jnp.zeros((SEL_TAIL, D_MODEL), BF16)

    for first in range(0, SEL_ROWS - 2 * SEL_TAIL, SEL_MM):
        compact(first, SEL_MM)
    compact(SEL_ROWS - 2 * SEL_TAIL, SEL_TAIL)

    @pl.when(local_ref[w * (N_EXPERTS + 1) + N_EXPERTS] > SEL_ROWS - SEL_TAIL)
    def _():
        compact(SEL_ROWS - SEL_TAIL, SEL_TAIL)

    _start_runs(runs_ref, w, sbuf, slot, xs_hbm, sem, True)

    @pl.when(w > 0)
    def _():
        _wait_runs(local_ref, w - 1, sbuf, 1 - slot, xs_hbm, sem, True)

    @pl.when(w == n_win - 1)
    def _():
        sbuf[2 * SEL_ROWS:, :] = jnp.zeros((2 * EXP_BM, D_MODEL), BF16)

        def fill(e, wait):
            cnt = pl.multiple_of(fill_cnt_ref[e], 8)
            off = pl.multiple_of(fill_off_ref[e], 8)
            cp = pltpu.make_async_copy(sbuf.at[pl.ds(2 * SEL_ROWS, cnt)], xs_hbm.at[pl.ds(off, cnt)], zsem)

            @pl.when(cnt > 0)
            def _():
                if wait:
                    cp.wait()
                else:
                    cp.start()

        def start_body(e, carry):
            fill(e, False)
            return carry

        def wait_body(e, carry):
            fill(e, True)
            return carry
        lax.fori_loop(0, N_EXPERTS, start_body, 0)
        _wait_runs(local_ref, w, sbuf, slot, xs_hbm, sem, True)
        lax.fori_loop(0, N_EXPERTS, wait_body, 0)


def _staging_shape(extra_rows):
    return jax.ShapeDtypeStruct((2 * SEL_ROWS + extra_rows, D_MODEL), BF16)


def _staging_spec(extra_rows):
    return pl.BlockSpec((2 * SEL_ROWS + extra_rows, D_MODEL), lambda w, *_: (0, 0))


def _dispatch(plan, h2, pos):
    t = h2.shape[0]
    n_win = t // WIN
    return pl.pallas_call(
        functools.partial(_dispatch_kernel, n_win=n_win),
        grid_spec=pltpu.PrefetchScalarGridSpec(
            num_scalar_prefetch=4,
            grid=(n_win,),
            in_specs=[
                pl.BlockSpec((WIN, D_MODEL), lambda w, *_: (w, 0)),
                pl.BlockSpec((TOP_K, WIN), lambda w, *_: (0, w)),
            ],
            out_specs=[pl.BlockSpec(memory_space=pl.ANY), _staging_spec(2 * EXP_BM)],
            scratch_shapes=[
                pltpu.VMEM((SEL_ROWS, WIN), BF16),
                pltpu.SemaphoreType.DMA((2,)),
                pltpu.SemaphoreType.DMA,
            ]),
        out_shape=[jax.ShapeDtypeStruct((_sorted_rows_bound(t), D_MODEL), BF16), _staging_shape(2 * EXP_BM)],
        compiler_params=pltpu.CompilerParams(
            dimension_semantics=("arbitrary",), vmem_limit_bytes=VMEM_LIMIT),
        name="dispatch",
    )(plan['local_off'], plan['runs'], plan['fill_off'], plan['fill_cnt'], h2, pos)[0]


def _expert_kernel(blk_expert_ref, next_expert_ref, n_used_ref, xs_hbm, wg_hbm, wu_hbm, wd_hbm, ys_hbm,
                   xbuf, ybuf, wg_st, wu_st, wd_st, wg_bf, wu_bf, wd_bf, xsem, ysem, wsem):
    n_used = n_used_ref[0]
    part = EXP_BM // EXP_SPLIT

    def row_copies(b, slot, fetch):
        out = []
        for q in range(EXP_SPLIT):
            hbm_rows = pl.ds(pl.multiple_of(b * EXP_BM + q * part, part), part)
            if fetch:
                out.append(pltpu.make_async_copy(xs_hbm.at[hbm_rows], xbuf.at[slot, q * part:(q + 1) * part],
                                                 xsem.at[slot]))
            else:
                out.append(pltpu.make_async_copy(ybuf.at[slot, q * part:(q + 1) * part], ys_hbm.at[hbm_rows],
                                                 ysem.at[slot]))
        return out

    def weight_copies(e, slot):
        return [pltpu.make_async_copy(wg_hbm.at[e], wg_st.at[slot], wsem.at[slot]),
                pltpu.make_async_copy(wu_hbm.at[e], wu_st.at[slot], wsem.at[slot]),
                pltpu.make_async_copy(wd_hbm.at[e], wd_st.at[slot], wsem.at[slot])]

    def start(copies):
        for c in copies:
            c.start()

    def wait(copies):
        for c in copies:
            c.wait()

    for ahead in range(EXP_XDEPTH - 2):
        @pl.when(ahead < n_used)
        def _(ahead=ahead):
            start(row_copies(ahead, ahead, True))

    @pl.when(n_used > 0)
    def _():
        start(weight_copies(blk_expert_ref[0], 0))

    def enter_block(b, wset):
        e = blk_expert_ref[b]
        new_expert = jnp.logical_or(b == 0, e != blk_expert_ref[jnp.maximum(b - 1, 0)])
        wset = jnp.where(new_expert, 1 - wset, wset)

        @pl.when(new_expert)
        def _():
            wait(weight_copies(e, wset))
            wg_bf[wset] = wg_st[wset].astype(BF16)
            wu_bf[wset] = wu_st[wset].astype(BF16)
            wd_bf[wset] = wd_st[wset].astype(BF16)
            nxt = next_expert_ref[e]

            @pl.when(nxt < N_EXPERTS)
            def _():
                start(weight_copies(nxt, 1 - wset))

        ahead = b + EXP_XDEPTH - 2

        @pl.when(ahead < n_used)
        def _():
            start(row_copies(ahead, ahead % EXP_XDEPTH, True))

        wait(row_copies(b, b % EXP_XDEPTH, True))

        @pl.when(b >= EXP_YDEPTH)
        def _():
            wait(row_copies(b - EXP_YDEPTH, b % EXP_YDEPTH, False))
        return wset

    def compute(b, wset):
        x = xbuf[b % EXP_XDEPTH]
        hg = _dot(x, wg_bf[wset])
        hb = hg * jax.nn.sigmoid(hg) * _dot(x, wu_bf[wset])
        ybuf[b % EXP_YDEPTH] = _dot(hb.astype(BF16), wd_bf[wset]).astype(BF16)

    def body(p, wset):
        b0 = 2 * p
        w0 = enter_block(b0, wset)
        w1 = enter_block(b0 + 1, w0)
        compute(b0, w0)
        compute(b0 + 1, w1)
        start(row_copies(b0, b0 % EXP_YDEPTH, False))
        start(row_copies(b0 + 1, (b0 + 1) % EXP_YDEPTH, False))
        return w1

    lax.fori_loop(0, n_used // 2, body, jnp.int32(1))

    for back in range(EXP_YDEPTH, 0, -1):
        @pl.when(n_used >= back)
        def _(back=back):
            wait(row_copies(n_used - back, (n_used - back) % EXP_YDEPTH, False))


def _experts(plan, xs, w_gate, w_up, w_down):
    any_spec = pl.BlockSpec(memory_space=pl.ANY)
    return pl.pallas_call(
        _expert_kernel,
        grid_spec=pltpu.PrefetchScalarGridSpec(
            num_scalar_prefetch=3,
            grid=(1,),
            in_specs=[any_spec, any_spec, any_spec, any_spec],
            out_specs=any_spec,
            scratch_shapes=[
                pltpu.VMEM((EXP_XDEPTH, EXP_BM, D_MODEL), BF16),
                pltpu.VMEM((EXP_YDEPTH, EXP_BM, D_MODEL), BF16),
                pltpu.VMEM((2, D_MODEL, D_EXPERT), F32),
                pltpu.VMEM((2, D_MODEL, D_EXPERT), F32),
                pltpu.VMEM((2, D_EXPERT, D_MODEL), F32),
                pltpu.VMEM((2, D_MODEL, D_EXPERT), BF16),
                pltpu.VMEM((2, D_MODEL, D_EXPERT), BF16),
                pltpu.VMEM((2, D_EXPERT, D_MODEL), BF16),
                pltpu.SemaphoreType.DMA((EXP_XDEPTH,)),
                pltpu.SemaphoreType.DMA((EXP_YDEPTH,)),
                pltpu.SemaphoreType.DMA((2,)),
            ]),
        out_shape=jax.ShapeDtypeStruct(xs.shape, BF16),
        compiler_params=pltpu.CompilerParams(
            dimension_semantics=("arbitrary",), vmem_limit_bytes=VMEM_LIMIT),
        name="experts",
    )(plan['blk_expert'], plan['next_expert'], plan['n_used'], xs, w_gate, w_up, w_down)


def _combine_kernel(local_ref, runs_ref, x1_ref, h2_ref, p_ref, rank_ref, gate_ref, lo_ref, hi_ref,
                    wsg_ref, wsu_ref, wsd_ref, gple_ref, wpg_ref, wp_ref, gfin_ref, ys_hbm, o_ref, ybuf, st_ref, sem,
                    *, n_win, final_norm):
    w = pl.program_id(0)
    slot = w % 2

    @pl.when(w == 0)
    def _():
        ybuf[...] = jnp.zeros(ybuf.shape, BF16)
        _start_runs(runs_ref, w, ybuf, slot, ys_hbm, sem, False)

    @pl.when(w + 1 < n_win)
    def _():
        _start_runs(runs_ref, w + 1, ybuf, 1 - slot, ys_hbm, sem, False)

    lo = lo_ref[0]
    hi = hi_ref[0]
    lo_f = lo.astype(F32)
    rank_tbl = rank_ref[...]
    gate_tbl = gate_ref[...]

    def build_group(lg):
        cols = slice(lg * CMB_LG, (lg + 1) * CMB_LG)
        rid = lg * CMB_LG + lax.broadcasted_iota(jnp.int32, (N_EXPERTS, CMB_LG), 1)
        owner = jnp.where(rid >= lo, jnp.where(rid < hi, 1.0, 0.0), 0.0)
        run_row = rid[0:1, :].astype(F32) - jnp.sum(owner * lo_f, axis=0, keepdims=True)
        owner = owner.astype(BF16)
        hit = _dot_t(rank_tbl, owner) == run_row
        st_ref[:, cols] = jnp.where(hit, _dot_t(gate_tbl, owner), 0.0).astype(BF16)

    build_group(0)
    h2 = h2_ref[...]
    hs = _dot(h2, wsg_ref[...])
    hs = hs * jax.nn.sigmoid(hs) * _dot(h2, wsu_ref[...])
    shared = _dot(hs.astype(BF16), wsd_ref[...])

    _wait_runs(local_ref, w, ybuf, slot, ys_hbm, sem, False)
    routed = None
    n_groups = SEL_ROWS // CMB_LG
    for lg in range(n_groups):
        if lg + 1 < n_groups:
            build_group(lg + 1)
        src = pl.multiple_of(slot * SEL_ROWS + lg * CMB_LG, CMB_LG)
        part = _dot(st_ref[:, lg * CMB_LG:(lg + 1) * CMB_LG], ybuf[pl.ds(src, CMB_LG), :])
        routed = part if routed is None else routed + part
    x2 = x1_ref[...] + routed + shared

    hp = _rms(x2, gple_ref[...]).astype(BF16)
    gate = jax.nn.sigmoid(_dot(hp, wpg_ref[...]))
    x3 = x2 + gate * _dot(p_ref[...].astype(BF16), wp_ref[...])
    o_ref[...] = _rms(x3, gfin_ref[...]) if final_norm else x3


def _combine(plan, ys, x1, h2, p, rank_tbl, gate_tbl, wsg, wsu, wsd, g_ple, w_pg, w_p, g_fin, final_norm):
    t = x1.shape[0]
    n_win = t // WIN
    row = lambda width: pl.BlockSpec((WIN, width), lambda w, *_: (w, 0))
    const = lambda shape: pl.BlockSpec(shape, lambda w, *_: (0,) * len(shape))
    return pl.pallas_call(
        functools.partial(_combine_kernel, n_win=n_win, final_norm=final_norm),
        grid_spec=pltpu.PrefetchScalarGridSpec(
            num_scalar_prefetch=2,
            grid=(n_win,),
            in_specs=[
                row(D_MODEL), row(D_MODEL), row(PLE_DIM),
                pl.BlockSpec((N_EXPERTS, WIN), lambda w, *_: (0, w)),
                pl.BlockSpec((N_EXPERTS, WIN), lambda w, *_: (0, w)),
                pl.BlockSpec((1, N_EXPERTS, 1), lambda w, *_: (w, 0, 0)),
                pl.BlockSpec((1, N_EXPERTS, 1), lambda w, *_: (w, 0, 0)),
                const((D_MODEL, D_EXPERT)), const((D_MODEL, D_EXPERT)), const((D_EXPERT, D_MODEL)),
                const((1, D_MODEL)), const((D_MODEL, D_MODEL)), const((PLE_DIM, D_MODEL)),
                const((1, D_MODEL)),
                pl.BlockSpec(memory_space=pl.ANY),
            ],
            out_specs=[row(D_MODEL), _staging_spec(0)],
            scratch_shapes=[
                pltpu.VMEM((WIN, SEL_ROWS), BF16),
                pltpu.SemaphoreType.DMA((2,)),
            ]),
        out_shape=[jax.ShapeDtypeStruct((t, D_MODEL), F32), _staging_shape(0)],
        compiler_params=pltpu.CompilerParams(
            dimension_semantics=("arbitrary",), vmem_limit_bytes=VMEM_LIMIT),
        name="combine",
    )(plan['local_off'], plan['runs'], x1, h2, p, rank_tbl, gate_tbl, plan['run_lo'], plan['run_hi'],
      wsg, wsu, wsd, g_ple, w_pg, w_p, g_fin, ys)[0]


def kernel(x, p, g_mix, w_in, b_in, w_dw, b_dw, g_cln, b_cln, w_conv_out, b_conv_out, w_pool, s_pool,
           w_out, g_ffn, w_router, b_router, w_e_gate, w_e_up, w_e_down, w_s_gate, w_s_up, w_s_down,
           g_ple, w_ple_gate, w_ple, g_final):
    bsz, s, d = x.shape
    t = bsz * s
    depth = w_in.shape[0]
    xt = x.reshape(t, d)
    row = lambda v: v.reshape(1, -1)
    for i in range(depth):
        x1, h2 = _mixer(
            xt, s, row(g_mix[i]), w_in[i].astype(BF16), row(b_in[i]), w_dw[i], row(b_dw[i]),
            row(g_cln[i]), row(b_cln[i]), w_conv_out[i].astype(BF16), row(b_conv_out[i]),
            w_pool[i].astype(BF16), row(s_pool[i]), w_out[i].astype(BF16), row(g_ffn[i]))
        gate, rank, pos, cnt = _router(h2, w_router[i].T.astype(BF16), b_router[i].reshape(N_EXPERTS, 1))
        plan = _dispatch_plan(cnt, t)
        xs = _dispatch(plan, h2, pos)
        ys = _experts(plan, xs, w_e_gate[i], w_e_up[i], w_e_down[i])
        xt = _combine(
            plan, ys, x1, h2, p[i].reshape(t, PLE_DIM), rank, gate,
            w_s_gate[i].astype(BF16), w_s_up[i].astype(BF16), w_s_down[i].astype(BF16),
            row(g_ple[i]), w_ple_gate[i].astype(BF16), w_ple[i].astype(BF16), row(g_final),
            final_norm=(i == depth - 1))
    return xt.reshape(bsz, s, d)
```

```python
import functools

import jax
import jax.numpy as jnp
from jax import lax
from jax.experimental import pallas as pl
from jax.experimental.pallas import tpu as pltpu

D_MODEL = 1024
D_CONV = 1024
D_POOL = 1024
CONV_WIDTH = 31
POOL_WINDOWS = (2, 4, 8, 16)
POOL_GROUP = 256
PLE_DIM = 256
N_EXPERTS = 64
N_GROUPS = 8
GROUP_SIZE = N_EXPERTS // N_GROUPS
TOPK_GROUPS = 4
TOP_K = 8
D_EXPERT = 256
ROUTED_SCALE = 2.5
NORM_EPS = 1e-6

F32 = jnp.float32
BF16 = jnp.bfloat16

MIX_TM = 512
MIX_NV = MIX_TM // 8
CONV_MG = 8
ROW_CHUNK = 64
LANE = 128

ROUTER_TM = 1024
WIN = 256
SEL_ROWS = 2560
SEL_RG = 64
SEL_MM = 512
SEL_TAIL = 256
EXP_BM = 576
EXP_XDEPTH = 6
EXP_YDEPTH = 4
EXP_SPLIT = 4
RUN_FIELDS = 4
RUN_UNROLL = 4
CMB_LG = 512

V7X_VMEM_BYTES = 64 * 1024 * 1024
VMEM_LIMIT = V7X_VMEM_BYTES - 8 * 1024 * 1024

assert SEL_ROWS >= TOP_K * WIN + 7 * N_EXPERTS
assert (SEL_ROWS - 2 * SEL_TAIL) % SEL_MM == 0 and SEL_TAIL % SEL_RG == 0 and SEL_ROWS % CMB_LG == 0
assert N_EXPERTS % RUN_UNROLL == 0 and EXP_BM % (16 * EXP_SPLIT) == 0 and EXP_XDEPTH > 2 and EXP_YDEPTH >= 2


def _rms(x, g):
    ms = jnp.mean(x * x, axis=-1, keepdims=True)
    return x * lax.rsqrt(ms + NORM_EPS) * g


def _dot(a, b):
    return jnp.dot(a, b, preferred_element_type=F32)


def _dot_t(a, b):
    return lax.dot_general(a, b, (((0,), (0,)), ((), ())), preferred_element_type=F32)


def _mixer_kernel(x_ref, gmix_ref, win_ref, bin_ref, wdw_ref, bdw_ref, gcln_ref, bcln_ref,
                  wco_ref, bco_ref, wpool_ref, spool_ref, wout_ref, gffn_ref, perm_ref, unperm_ref,
                  x1_ref, h2_ref, a_ext, a_prev, u_ext, u_prev, c_buf, q_buf, *, tiles_per_seq):
    i = pl.program_id(0) % tiles_per_seq
    tm = MIX_TM
    nv = MIX_NV

    @pl.when(i == 0)
    def _():
        a_prev[...] = jnp.zeros(a_prev.shape, F32)
        u_prev[...] = jnp.zeros(u_prev.shape, F32)

    x = x_ref[...]
    h = _dot(perm_ref[...], _rms(x, gmix_ref[...]).astype(BF16)).astype(BF16)

    def proj(lo, hi):
        return _dot(h, win_ref[:, lo:hi]) + bin_ref[:, lo:hi]

    glu = proj(0, D_CONV) * jax.nn.sigmoid(proj(D_CONV, 2 * D_CONV))
    for lc in range(D_CONV // LANE):
        a_ext[lc, tm:2 * tm, :] = glu[:, lc * LANE:(lc + 1) * LANE]
    u_ext[tm:2 * tm, :] = proj(2 * D_CONV, 2 * D_CONV + D_POOL)

    def delayed_groups(ext, prev, first_group):
        last_row = lax.broadcasted_iota(jnp.int32, (8, ext.shape[-1]), 0) == 7
        for g in range(first_group, nv):
            rows = slice(8 * g, 8 * g + 8)
            mixed = jnp.where(last_row, prev[rows, :], ext[tm + 8 * g:tm + 8 * g + 8, :])
            ext[rows, :] = pltpu.roll(mixed, 1, axis=0)
            prev[rows, :] = ext[tm + 8 * g:tm + 8 * g + 8, :]

    delayed_groups(u_ext, u_prev, nv - (max(POOL_WINDOWS) - 1))

    for r0 in range(0, tm, ROW_CHUNK):
        row = r0 + lax.broadcasted_iota(jnp.int32, (ROW_CHUNK, POOL_GROUP), 0)
        t1 = i * tm + (row % 8) * nv + row // 8 + 1
        for gi, w in enumerate(POOL_WINDOWS):
            ls = slice(gi * POOL_GROUP, (gi + 1) * POOL_GROUP)
            tok = u_ext[tm + r0:tm + r0 + ROW_CHUNK, ls]
            s = tok
            for j in range(1, w):
                s = s + u_ext[tm + r0 - 8 * j:tm + r0 - 8 * j + ROW_CHUNK, ls]
            cnt = jnp.minimum(t1, w).astype(F32)
            q_buf[r0:r0 + ROW_CHUNK, ls] = s / cnt - tok

    qs_out = []
    for gi in range(len(POOL_WINDOWS)):
        ls = slice(gi * POOL_GROUP, (gi + 1) * POOL_GROUP)
        qs_out.append(_dot(q_buf[:, ls].astype(BF16), wpool_ref[gi]) * spool_ref[:, ls])
    branch_b = jnp.concatenate(qs_out, axis=-1)

    def conv_column(lc, carry):
        a_col = a_ext.at[lc]
        delayed_groups(a_col, a_prev.at[lc], nv - (CONV_WIDTH - 1))
        w_col = wdw_ref.at[lc]
        for g0 in range(0, nv, CONV_MG):
            acc = None
            for k in range(CONV_WIDTH):
                src = nv + g0 + k - (CONV_WIDTH - 1)
                term = a_col[8 * src:8 * (src + CONV_MG), :] * w_col[k:k + 1, :]
                acc = term if acc is None else acc + term
            c_buf[lc, 8 * g0:8 * (g0 + CONV_MG), :] = acc + bdw_ref[lc]
        return carry
    lax.fori_loop(0, D_CONV // LANE, conv_column, 0)

    c2 = 2 * D_CONV + D_POOL
    gate_a = jax.nn.sigmoid(proj(c2, c2 + D_MODEL))
    gate_b = jax.nn.sigmoid(proj(c2 + D_MODEL, c2 + 2 * D_MODEL))

    c = jnp.concatenate([c_buf[lc] for lc in range(D_CONV // LANE)], axis=-1)
    mu = jnp.mean(c, axis=-1, keepdims=True)
    xc = c - mu
    var = jnp.mean(xc * xc, axis=-1, keepdims=True)
    y = xc * lax.rsqrt(var + NORM_EPS) * gcln_ref[...] + bcln_ref[...]
    y = y * jax.nn.sigmoid(y)
    branch_a = _dot(y.astype(BF16), wco_ref[...]) + bco_ref[...]

    merged = gate_a * branch_a + gate_b * branch_b
    merged = _dot(unperm_ref[...], merged.astype(BF16)).astype(BF16)
    x1 = x + _dot(merged, wout_ref[...])
    x1_ref[...] = x1
    h2_ref[...] = _rms(x1, gffn_ref[...]).astype(BF16)


def _const_spec(shape):
    n = len(shape)
    return pl.BlockSpec(shape, lambda i, _n=n: (0,) * _n, pipeline_mode=pl.Buffered(1))


def _mixer(x, seq_len, g_mix, w_in, b_in, w_dw, b_dw, g_cln, b_cln, w_co, b_co, w_pool, s_pool, w_out,
           g_ffn):
    t = x.shape[0]
    tm = MIX_TM
    assert seq_len % tm == 0 and MIX_NV >= CONV_WIDTH and MIX_NV >= max(POOL_WINDOWS)
    d_in = w_in.shape[1]
    row = pl.BlockSpec((tm, D_MODEL), lambda i: (i, 0))
    n_col = D_CONV // LANE
    w_dw = w_dw.reshape(CONV_WIDTH, n_col, LANE).transpose(1, 0, 2)
    b_dw = b_dw.reshape(n_col, 1, LANE)
    r = jnp.arange(tm)
    perm = ((r % 8) * MIX_NV + r // 8)[:, None] == jnp.arange(tm)[None, :]
    perm = perm.astype(BF16)
    return pl.pallas_call(
        functools.partial(_mixer_kernel, tiles_per_seq=seq_len // tm),
        grid=(t // tm,),
        in_specs=[
            row,
            _const_spec((1, D_MODEL)),
            _const_spec((D_MODEL, d_in)),
            _const_spec((1, d_in)),
            _const_spec((n_col, CONV_WIDTH, LANE)),
            _const_spec((n_col, 1, LANE)),
            _const_spec((1, D_CONV)),
            _const_spec((1, D_CONV)),
            _const_spec((D_CONV, D_MODEL)),
            _const_spec((1, D_MODEL)),
            _const_spec((len(POOL_WINDOWS), POOL_GROUP, POOL_GROUP)),
            _const_spec((1, D_POOL)),
            _const_spec((D_MODEL, D_MODEL)),
            _const_spec((1, D_MODEL)),
            _const_spec((tm, tm)),
            _const_spec((tm, tm)),
        ],
        out_specs=[row, row],
        out_shape=[jax.ShapeDtypeStruct((t, D_MODEL), F32),
                   jax.ShapeDtypeStruct((t, D_MODEL), BF16)],
        scratch_shapes=[
            pltpu.VMEM((n_col, 2 * tm, LANE), F32),
            pltpu.VMEM((n_col, tm, LANE), F32),
            pltpu.VMEM((2 * tm, D_POOL), F32),
            pltpu.VMEM((tm, D_POOL), F32),
            pltpu.VMEM((n_col, tm, LANE), F32),
            pltpu.VMEM((tm, D_POOL), F32),
        ],
        compiler_params=pltpu.CompilerParams(
            dimension_semantics=("arbitrary",), vmem_limit_bytes=VMEM_LIMIT),
        name="mixer",
    )(x, g_mix, w_in, b_in, w_dw, b_dw, g_cln, b_cln, w_co, b_co, w_pool, s_pool, w_out, g_ffn, perm, perm.T)


def _beats(v, other, other_is_later):
    v = jnp.broadcast_to(v, other.shape)
    return jnp.where(other_is_later, jnp.where(v >= other, 1, 0), jnp.where(v > other, 1, 0))


def _router_kernel(h2_ref, wrt_ref, br_ref, utri_ref, ltri_ref, gate_ref, rank_ref, pos_ref, cnt_ref):
    tm = ROUTER_TM
    logits = lax.dot_general(wrt_ref[...], h2_ref[...], (((1,), (1,)), ((), ())),
                             preferred_element_type=F32)
    scores = jax.nn.sigmoid(logits)
    sel = scores + br_ref[...]
    shape3 = (N_GROUPS, GROUP_SIZE, tm)
    sel3 = sel.reshape(shape3)
    scores3 = scores.reshape(shape3)
    neg_inf = jnp.float32(-jnp.inf)

    member = lax.broadcasted_iota(jnp.int32, shape3, 1)
    m1 = jnp.max(sel3, axis=1, keepdims=True)
    first = jnp.min(jnp.where(sel3 == m1, member, GROUP_SIZE), axis=1, keepdims=True)
    m2 = jnp.max(jnp.where(member == first, neg_inf, sel3), axis=1, keepdims=True)
    gscore = jnp.broadcast_to(m1 + m2, shape3)

    gidx = lax.broadcasted_iota(jnp.int32, shape3, 0)
    grank = jnp.zeros(shape3, jnp.int32)
    for j in range(N_GROUPS):
        sj = gscore[j:j + 1]
        grank = grank + _beats(sj, gscore, gidx > j)
    masked = jnp.where(grank < TOPK_GROUPS, sel3, neg_inf)

    eidx = gidx * GROUP_SIZE + member
    work = masked
    erank = jnp.full(shape3, TOP_K, jnp.int32)
    for k in range(TOP_K):
        best = jnp.max(jnp.max(work, axis=0, keepdims=True), axis=1, keepdims=True)
        cand = jnp.where(work == best, eidx, N_EXPERTS)
        pick = jnp.min(jnp.min(cand, axis=0, keepdims=True), axis=1, keepdims=True)
        hit = eidx == pick
        work = jnp.where(hit, neg_inf, work)
        erank = jnp.where(hit, k, erank)
    chosen = erank < TOP_K
    top_s = jnp.where(chosen, scores3, 0.0)
    denom = jnp.sum(jnp.sum(top_s, axis=0, keepdims=True), axis=1, keepdims=True)
    gates3 = top_s / denom * ROUTED_SCALE
    chosen2 = jnp.where(chosen, 1.0, 0.0).reshape(N_EXPERTS, tm)
    gate_ref[...] = gates3.reshape(N_EXPERTS, tm).astype(BF16)

    for w in range(tm // WIN):
        ls = slice(w * WIN, (w + 1) * WIN)
        mw = chosen2[:, ls]
        rank = _dot(mw.astype(BF16), utri_ref[...])
        n = jnp.sum(mw, axis=1, keepdims=True)
        run = jnp.floor((n + 7.0) * 0.125) * 8.0
        start = _dot(ltri_ref[...], jnp.broadcast_to(run, (N_EXPERTS, WIN)).astype(BF16))
        rank_ref[:, ls] = jnp.where(mw > 0.5, rank, -1.0).astype(BF16)
        row3 = (rank + start).reshape(N_GROUPS, GROUP_SIZE, WIN)
        er = erank[:, :, ls]
        for k in range(TOP_K):
            pk = jnp.sum(jnp.sum(jnp.where(er == k, row3, 0.0), axis=0, keepdims=True), axis=1, keepdims=True)
            pos_ref[k:k + 1, ls] = pk.reshape(1, WIN).astype(jnp.int32)
        cnt_ref[w] = n


def _router(h2, w_rt, b_r):
    t = h2.shape[0]
    tm = ROUTER_TM
    utri = jnp.triu(jnp.ones((WIN, WIN), BF16), k=1)
    ltri = jnp.tril(jnp.ones((N_EXPERTS, N_EXPERTS), BF16), k=-1)
    return pl.pallas_call(
        _router_kernel,
        grid=(t // tm,),
        in_specs=[
            pl.BlockSpec((tm, D_MODEL), lambda i: (i, 0)),
            _const_spec((N_EXPERTS, D_MODEL)),
            _const_spec((N_EXPERTS, 1)),
            _const_spec((WIN, WIN)),
            _const_spec((N_EXPERTS, N_EXPERTS)),
        ],
        out_specs=[
            pl.BlockSpec((N_EXPERTS, tm), lambda i: (0, i)),
            pl.BlockSpec((N_EXPERTS, tm), lambda i: (0, i)),
            pl.BlockSpec((TOP_K, tm), lambda i: (0, i)),
            pl.BlockSpec((tm // WIN, N_EXPERTS, 1), lambda i: (i, 0, 0)),
        ],
        out_shape=[
            jax.ShapeDtypeStruct((N_EXPERTS, t), BF16),
            jax.ShapeDtypeStruct((N_EXPERTS, t), BF16),
            jax.ShapeDtypeStruct((TOP_K, t), jnp.int32),
            jax.ShapeDtypeStruct((t // WIN, N_EXPERTS, 1), F32),
        ],
        compiler_params=pltpu.CompilerParams(
            dimension_semantics=("arbitrary",), vmem_limit_bytes=VMEM_LIMIT),
        name="router",
    )(h2, w_rt, b_r, utri, ltri)


def _sorted_rows_bound(t):
    rows = t * TOP_K + (t // WIN) * N_EXPERTS * 7 + N_EXPERTS * (EXP_BM - 1)
    blocks = -(-rows // EXP_BM)
    return (blocks + blocks % 2) * EXP_BM


def _dispatch_plan(cnt, t):
    nw = t // WIN
    n = cnt.reshape(nw, N_EXPERTS).astype(jnp.int32)
    run = (n + 7) // 8 * 8
    local_end = jnp.cumsum(run, axis=1)
    local_off = jnp.concatenate([jnp.zeros((nw, 1), jnp.int32), local_end], axis=1)
    total = jnp.sum(run, axis=0)
    region = (total + EXP_BM - 1) // EXP_BM * EXP_BM
    eid = jnp.arange(N_EXPERTS, dtype=jnp.int32)
    last_owner = jnp.max(jnp.where(region > 0, eid, 0))
    odd = (jnp.sum(region) // EXP_BM) % 2
    region = region + jnp.where(eid == last_owner, odd * EXP_BM, 0)
    region_end = jnp.cumsum(region)
    base = region_end - region
    global_off = base[None, :] + jnp.cumsum(run, axis=0) - run
    n_blocks = _sorted_rows_bound(t) // EXP_BM
    n_used = region_end[-1] // EXP_BM
    blk = jnp.arange(n_blocks, dtype=jnp.int32)
    blk_expert = jnp.sum((region_end[None, :] <= blk[:, None] * EXP_BM).astype(jnp.int32), axis=1)
    blk_expert = jnp.minimum(blk_expert, N_EXPERTS - 1)
    later_nonempty = (eid[None, :] > eid[:, None]) & (region[None, :] > 0)
    next_expert = jnp.min(jnp.where(later_nonempty, eid[None, :], N_EXPERTS), axis=1).astype(jnp.int32)
    return dict(
        run_lo=local_off[:, :N_EXPERTS].reshape(nw, N_EXPERTS, 1),
        run_hi=local_off[:, 1:].reshape(nw, N_EXPERTS, 1),
        local_off=local_off.reshape(-1),
        runs=jnp.stack([local_off[:, :N_EXPERTS], run, global_off, jnp.zeros_like(run)], axis=-1).reshape(-1),
        fill_off=base + total, fill_cnt=region - total,
        blk_expert=blk_expert.astype(jnp.int32), next_expert=next_expert,
        n_used=n_used.reshape(1).astype(jnp.int32))


def _run_copy(runs_ref, win, e, vmem_buf, slot, hbm_buf, sem, to_hbm):
    p = (win * N_EXPERTS + e) * RUN_FIELDS
    lo = pl.multiple_of(runs_ref[p], 8)
    cnt = pl.multiple_of(runs_ref[p + 1], 8)
    go = pl.multiple_of(runs_ref[p + 2], 8)
    v = vmem_buf.at[pl.ds(pl.multiple_of(slot * SEL_ROWS + lo, 8), cnt)]
    h = hbm_buf.at[pl.ds(go, cnt)]
    cp = pltpu.make_async_copy(v, h, sem.at[slot]) if to_hbm else pltpu.make_async_copy(h, v, sem.at[slot])
    return cnt, cp


def _start_runs(runs_ref, win, vmem_buf, slot, hbm_buf, sem, to_hbm):
    def body(i, carry):
        copies = [_run_copy(runs_ref, win, i * RUN_UNROLL + j, vmem_buf, slot, hbm_buf, sem, to_hbm)
                  for j in range(RUN_UNROLL)]
        for cnt, cp in copies:
            @pl.when(cnt > 0)
            def _(cp=cp):
                cp.start()
        return carry
    lax.fori_loop(0, N_EXPERTS // RUN_UNROLL, body, 0)


def _wait_runs(local_ref, win, vmem_buf, slot, hbm_buf, sem, to_hbm):
    total = pl.multiple_of(local_ref[win * (N_EXPERTS + 1) + N_EXPERTS], 8)
    v = vmem_buf.at[pl.ds(pl.multiple_of(slot * SEL_ROWS, 8), total)]
    h = hbm_buf.at[pl.ds(0, total)]
    cp = pltpu.make_async_copy(v, h, sem.at[slot]) if to_hbm else pltpu.make_async_copy(h, v, sem.at[slot])

    @pl.when(total > 0)
    def _():
        cp.wait()


def _dispatch_kernel(local_ref, runs_ref, fill_off_ref, fill_cnt_ref, h2_ref, pos_ref, xs_hbm, sbuf,
                     s_ref, sem, zsem, *, n_win):
    w = pl.program_id(0)
    slot = w % 2
    pos = pos_ref[...]

    h2 = h2_ref[...]
    assert SEL_RG <= 256
    rid_b = lax.broadcasted_iota(jnp.int32, (SEL_RG, WIN), 0).astype(F32).astype(BF16)
    one_b = jnp.ones((SEL_RG, WIN), BF16)
    def compact(first, n_rows):
        for r0 in range(first, first + n_rows, SEL_RG):
            acc = jnp.zeros((SEL_RG, WIN), BF16)
            for k in range(TOP_K):
                off = (pos[k:k + 1, :] - r0).astype(F32)
                off = jnp.broadcast_to(off, (SEL_RG, WIN)).astype(BF16)
                acc = jnp.where(rid_b == off, one_b, acc)
            s_ref[r0:r0 + SEL_RG, :] = acc
        dst = pl.multiple_of(slot * SEL_ROWS + first, SEL_TAIL)
        sbuf[pl.ds(dst, n_rows), :] = _dot(s_ref[first:first + n_rows, :], h2).astype(BF16)

    @pl.when(w == 0)
    def _():
        for s in range(2):
            sbuf[(s + 1) * SEL_ROWS - SEL_TAIL:(s + 1) * SEL_ROWS, :] = jnp.zeros((SEL_TAIL, D_MODEL), BF16)

    for first in range(0, SEL_ROWS - 2 * SEL_TAIL, SEL_MM):
        compact(first, SEL_MM)
    compact(SEL_ROWS - 2 * SEL_TAIL, SEL_TAIL)

    @pl.when(local_ref[w * (N_EXPERTS + 1) + N_EXPERTS] > SEL_ROWS - SEL_TAIL)
    def _():
        compact(SEL_ROWS - SEL_TAIL, SEL_TAIL)

    _start_runs(runs_ref, w, sbuf, slot, xs_hbm, sem, True)

    @pl.when(w > 0)
    def _():
        _wait_runs(local_ref, w - 1, sbuf, 1 - slot, xs_hbm, sem, True)

    @pl.when(w == n_win - 1)
    def _():
        sbuf[2 * SEL_ROWS:, :] = jnp.zeros((2 * EXP_BM, D_MODEL), BF16)

        def fill(e, wait):
            cnt = pl.multiple_of(fill_cnt_ref[e], 8)
            off = pl.multiple_of(fill_off_ref[e], 8)
            cp = pltpu.make_async_copy(sbuf.at[pl.ds(2 * SEL_ROWS, cnt)], xs_hbm.at[pl.ds(off, cnt)], zsem)

            @pl.when(cnt > 0)
            def _():
                if wait:
                    cp.wait()
                else:
                    cp.start()

        def start_body(e, carry):
            fill(e, False)
            return carry

        def wait_body(e, carry):
            fill(e, True)
            return carry
        lax.fori_loop(0, N_EXPERTS, start_body, 0)
        _wait_runs(local_ref, w, sbuf, slot, xs_hbm, sem, True)
        lax.fori_loop(0, N_EXPERTS, wait_body, 0)


def _staging_shape(extra_rows):
    return jax.ShapeDtypeStruct((2 * SEL_ROWS + extra_rows, D_MODEL), BF16)


def _staging_spec(extra_rows):
    return pl.BlockSpec((2 * SEL_ROWS + extra_rows, D_MODEL), lambda w, *_: (0, 0))


def _dispatch(plan, h2, pos):
    t = h2.shape[0]
    n_win = t // WIN
    return pl.pallas_call(
        functools.partial(_dispatch_kernel, n_win=n_win),
        grid_spec=pltpu.PrefetchScalarGridSpec(
            num_scalar_prefetch=4,
            grid=(n_win,),
            in_specs=[
                pl.BlockSpec((WIN, D_MODEL), lambda w, *_: (w, 0)),
                pl.BlockSpec((TOP_K, WIN), lambda w, *_: (0, w)),
            ],
            out_specs=[pl.BlockSpec(memory_space=pl.ANY), _staging_spec(2 * EXP_BM)],
            scratch_shapes=[
                pltpu.VMEM((SEL_ROWS, WIN), BF16),
                pltpu.SemaphoreType.DMA((2,)),
                pltpu.SemaphoreType.DMA,
            ]),
        out_shape=[jax.ShapeDtypeStruct((_sorted_rows_bound(t), D_MODEL), BF16), _staging_shape(2 * EXP_BM)],
        compiler_params=pltpu.CompilerParams(
            dimension_semantics=("arbitrary",), vmem_limit_bytes=VMEM_LIMIT),
        name="dispatch",
    )(plan['local_off'], plan['runs'], plan['fill_off'], plan['fill_cnt'], h2, pos)[0]


def _expert_kernel(blk_expert_ref, next_expert_ref, n_used_ref, xs_hbm, wg_hbm, wu_hbm, wd_hbm, ys_hbm,
                   xbuf, ybuf, wg_st, wu_st, wd_st, wg_bf, wu_bf, wd_bf, xsem, ysem, wsem):
    n_used = n_used_ref[0]
    part = EXP_BM // EXP_SPLIT

    def row_copies(b, slot, fetch):
        out = []
        for q in range(EXP_SPLIT):
            hbm_rows = pl.ds(pl.multiple_of(b * EXP_BM + q * part, part), part)
            if fetch:
                out.append(pltpu.make_async_copy(xs_hbm.at[hbm_rows], xbuf.at[slot, q * part:(q + 1) * part],
                                                 xsem.at[slot]))
            else:
                out.append(pltpu.make_async_copy(ybuf.at[slot, q * part:(q + 1) * part], ys_hbm.at[hbm_rows],
                                                 ysem.at[slot]))
        return out

    def weight_copies(e, slot):
        return [pltpu.make_async_copy(wg_hbm.at[e], wg_st.at[slot], wsem.at[slot]),
                pltpu.make_async_copy(wu_hbm.at[e], wu_st.at[slot], wsem.at[slot]),
                pltpu.make_async_copy(wd_hbm.at[e], wd_st.at[slot], wsem.at[slot])]

    def start(copies):
        for c in copies:
            c.start()

    def wait(copies):
        for c in copies:
            c.wait()

    for ahead in range(EXP_XDEPTH - 2):
        @pl.when(ahead < n_used)
        def _(ahead=ahead):
            start(row_copies(ahead, ahead, True))

    @pl.when(n_used > 0)
    def _():
        start(weight_copies(blk_expert_ref[0], 0))

    def enter_block(b, wset):
        e = blk_expert_ref[b]
        new_expert = jnp.logical_or(b == 0, e != blk_expert_ref[jnp.maximum(b - 1, 0)])
        wset = jnp.where(new_expert, 1 - wset, wset)

        @pl.when(new_expert)
        def _():
            wait(weight_copies(e, wset))
            wg_bf[wset] = wg_st[wset].astype(BF16)
            wu_bf[wset] = wu_st[wset].astype(BF16)
            wd_bf[wset] = wd_st[wset].astype(BF16)
            nxt = next_expert_ref[e]

            @pl.when(nxt < N_EXPERTS)
            def _():
                start(weight_copies(nxt, 1 - wset))

        ahead = b + EXP_XDEPTH - 2

        @pl.when(ahead < n_used)
        def _():
            start(row_copies(ahead, ahead % EXP_XDEPTH, True))

        wait(row_copies(b, b % EXP_XDEPTH, True))

        @pl.when(b >= EXP_YDEPTH)
        def _():
            wait(row_copies(b - EXP_YDEPTH, b % EXP_YDEPTH, False))
        return wset

    def compute(b, wset):
        x = xbuf[b % EXP_XDEPTH]
        hg = _dot(x, wg_bf[wset])
        hb = hg * jax.nn.sigmoid(hg) * _dot(x, wu_bf[wset])
        ybuf[b % EXP_YDEPTH] = _dot(hb.astype(BF16), wd_bf[wset]).astype(BF16)

    def body(p, wset):
        b0 = 2 * p
        w0 = enter_block(b0, wset)
        w1 = enter_block(b0 + 1, w0)
        compute(b0, w0)
        compute(b0 + 1, w1)
        start(row_copies(b0, b0 % EXP_YDEPTH, False))
        start(row_copies(b0 + 1, (b0 + 1) % EXP_YDEPTH, False))
        return w1

    lax.fori_loop(0, n_used // 2, body, jnp.int32(1))

    for back in range(EXP_YDEPTH, 0, -1):
        @pl.when(n_used >= back)
        def _(back=back):
            wait(row_copies(n_used - back, (n_used - back) % EXP_YDEPTH, False))


def _experts(plan, xs, w_gate, w_up, w_down):
    any_spec = pl.BlockSpec(memory_space=pl.ANY)
    return pl.pallas_call(
        _expert_kernel,
        grid_spec=pltpu.PrefetchScalarGridSpec(
            num_scalar_prefetch=3,
            grid=(1,),
            in_specs=[any_spec, any_spec, any_spec, any_spec],
            out_specs=any_spec,
            scratch_shapes=[
                pltpu.VMEM((EXP_XDEPTH, EXP_BM, D_MODEL), BF16),
                pltpu.VMEM((EXP_YDEPTH, EXP_BM, D_MODEL), BF16),
                pltpu.VMEM((2, D_MODEL, D_EXPERT), F32),
                pltpu.VMEM((2, D_MODEL, D_EXPERT), F32),
                pltpu.VMEM((2, D_EXPERT, D_MODEL), F32),
                pltpu.VMEM((2, D_MODEL, D_EXPERT), BF16),
                pltpu.VMEM((2, D_MODEL, D_EXPERT), BF16),
                pltpu.VMEM((2, D_EXPERT, D_MODEL), BF16),
                pltpu.SemaphoreType.DMA((EXP_XDEPTH,)),
                pltpu.SemaphoreType.DMA((EXP_YDEPTH,)),
                pltpu.SemaphoreType.DMA((2,)),
            ]),
        out_shape=jax.ShapeDtypeStruct(xs.shape, BF16),
        compiler_params=pltpu.CompilerParams(
            dimension_semantics=("arbitrary",), vmem_limit_bytes=VMEM_LIMIT),
        name="experts",
    )(plan['blk_expert'], plan['next_expert'], plan['n_used'], xs, w_gate, w_up, w_down)


def _combine_kernel(local_ref, runs_ref, x1_ref, h2_ref, p_ref, rank_ref, gate_ref, lo_ref, hi_ref,
                    wsg_ref, wsu_ref, wsd_ref, gple_ref, wpg_ref, wp_ref, gfin_ref, ys_hbm, o_ref, ybuf, st_ref, sem,
                    *, n_win, final_norm):
    w = pl.program_id(0)
    slot = w % 2

    @pl.when(w == 0)
    def _():
        ybuf[...] = jnp.zeros(ybuf.shape, BF16)
        _start_runs(runs_ref, w, ybuf, slot, ys_hbm, sem, False)

    @pl.when(w + 1 < n_win)
    def _():
        _start_runs(runs_ref, w + 1, ybuf, 1 - slot, ys_hbm, sem, False)

    lo = lo_ref[0]
    hi = hi_ref[0]
    lo_f = lo.astype(F32)
    rank_tbl = rank_ref[...]
    gate_tbl = gate_ref[...]

    def build_group(lg):
        cols = slice(lg * CMB_LG, (lg + 1) * CMB_LG)
        rid = lg * CMB_LG + lax.broadcasted_iota(jnp.int32, (N_EXPERTS, CMB_LG), 1)
        owner = jnp.where(rid >= lo, jnp.where(rid < hi, 1.0, 0.0), 0.0)
        run_row = rid[0:1, :].astype(F32) - jnp.sum(owner * lo_f, axis=0, keepdims=True)
        owner = owner.astype(BF16)
        hit = _dot_t(rank_tbl, owner) == run_row
        st_ref[:, cols] = jnp.where(hit, _dot_t(gate_tbl, owner), 0.0).astype(BF16)

    build_group(0)
    h2 = h2_ref[...]
    hs = _dot(h2, wsg_ref[...])
    hs = hs * jax.nn.sigmoid(hs) * _dot(h2, wsu_ref[...])
    shared = _dot(hs.astype(BF16), wsd_ref[...])

    _wait_runs(local_ref, w, ybuf, slot, ys_hbm, sem, False)
    routed = None
    n_groups = SEL_ROWS // CMB_LG
    for lg in range(n_groups):
        if lg + 1 < n_groups:
            build_group(lg + 1)
        src = pl.multiple_of(slot * SEL_ROWS + lg * CMB_LG, CMB_LG)
        part = _dot(st_ref[:, lg * CMB_LG:(lg + 1) * CMB_LG], ybuf[pl.ds(src, CMB_LG), :])
        routed = part if routed is None else routed + part
    x2 = x1_ref[...] + routed + shared

    hp = _rms(x2, gple_ref[...]).astype(BF16)
    gate = jax.nn.sigmoid(_dot(hp, wpg_ref[...]))
    x3 = x2 + gate * _dot(p_ref[...].astype(BF16), wp_ref[...])
    o_ref[...] = _rms(x3, gfin_ref[...]) if final_norm else x3


def _combine(plan, ys, x1, h2, p, rank_tbl, gate_tbl, wsg, wsu, wsd, g_ple, w_pg, w_p, g_fin, final_norm):
    t = x1.shape[0]
    n_win = t // WIN
    row = lambda width: pl.BlockSpec((WIN, width), lambda w, *_: (w, 0))
    const = lambda shape: pl.BlockSpec(shape, lambda w, *_: (0,) * len(shape))
    return pl.pallas_call(
        functools.partial(_combine_kernel, n_win=n_win, final_norm=final_norm),
        grid_spec=pltpu.PrefetchScalarGridSpec(
            num_scalar_prefetch=2,
            grid=(n_win,),
            in_specs=[
                row(D_MODEL), row(D_MODEL), row(PLE_DIM),
                pl.BlockSpec((N_EXPERTS, WIN), lambda w, *_: (0, w)),
                pl.BlockSpec((N_EXPERTS, WIN), lambda w, *_: (0, w)),
                pl.BlockSpec((1, N_EXPERTS, 1), lambda w, *_: (w, 0, 0)),
                pl.BlockSpec((1, N_EXPERTS, 1), lambda w, *_: (w, 0, 0)),
                const((D_MODEL, D_EXPERT)), const((D_MODEL, D_EXPERT)), const((D_EXPERT, D_MODEL)),
                const((1, D_MODEL)), const((D_MODEL, D_MODEL)), const((PLE_DIM, D_MODEL)),
                const((1, D_MODEL)),
                pl.BlockSpec(memory_space=pl.ANY),
            ],
            out_specs=[row(D_MODEL), _staging_spec(0)],
            scratch_shapes=[
                pltpu.VMEM((WIN, SEL_ROWS), BF16),
                pltpu.SemaphoreType.DMA((2,)),
            ]),
        out_shape=[jax.ShapeDtypeStruct((t, D_MODEL), F32), _staging_shape(0)],
        compiler_params=pltpu.CompilerParams(
            dimension_semantics=("arbitrary",), vmem_limit_bytes=VMEM_LIMIT),
        name="combine",
    )(plan['local_off'], plan['runs'], x1, h2, p, rank_tbl, gate_tbl, plan['run_lo'], plan['run_hi'],
      wsg, wsu, wsd, g_ple, w_pg, w_p, g_fin, ys)[0]


def kernel(x, p, g_mix, w_in, b_in, w_dw, b_dw, g_cln, b_cln, w_conv_out, b_conv_out, w_pool, s_pool,
           w_out, g_ffn, w_router, b_router, w_e_gate, w_e_up, w_e_down, w_s_gate, w_s_up, w_s_down,
           g_ple, w_ple_gate, w_ple, g_final):
    bsz, s, d = x.shape
    t = bsz * s
    depth = w_in.shape[0]
    xt = x.reshape(t, d)
    row = lambda v: v.reshape(1, -1)
    for i in range(depth):
        x1, h2 = _mixer(
            xt, s, row(g_mix[i]), w_in[i].astype(BF16), row(b_in[i]), w_dw[i], row(b_dw[i]),
            row(g_cln[i]), row(b_cln[i]), w_conv_out[i].astype(BF16), row(b_conv_out[i]),
            w_pool[i].astype(BF16), row(s_pool[i]), w_out[i].astype(BF16), row(g_ffn[i]))
        gate, rank, pos, cnt = _router(h2, w_router[i].T.astype(BF16), b_router[i].reshape(N_EXPERTS, 1))
        plan = _dispatch_plan(cnt, t)
        xs = _dispatch(plan, h2, pos)
        ys = _experts(plan, xs, w_e_gate[i], w_e_up[i], w_e_down[i])
        xt = _combine(
            plan, ys, x1, h2, p[i].reshape(t, PLE_DIM), rank, gate,
            w_s_gate[i].astype(BF16), w_s_up[i].astype(BF16), w_s_down[i].astype(BF16),
            row(g_ple[i]), w_ple_gate[i].astype(BF16), w_ple[i].astype(BF16), row(g_final),
            final_norm=(i == depth - 1))
    return xt.reshape(bsz, s, d)
```

```python
import functools

import jax
import jax.numpy as jnp
from jax import lax
from jax.experimental import pallas as pl
from jax.experimental.pallas import tpu as pltpu

D_MODEL = 1024
D_CONV = 1024
D_POOL = 1024
CONV_WIDTH = 31
POOL_WINDOWS = (2, 4, 8, 16)
POOL_GROUP = 256
PLE_DIM = 256
N_EXPERTS = 64
N_GROUPS = 8
GROUP_SIZE = N_EXPERTS // N_GROUPS
TOPK_GROUPS = 4
TOP_K = 8
D_EXPERT = 256
ROUTED_SCALE = 2.5
NORM_EPS = 1e-6

F32 = jnp.float32
BF16 = jnp.bfloat16

MIX_TM = 512
MIX_NV = MIX_TM // 8
CONV_MG = 8
ROW_CHUNK = 64
LANE = 128

ROUTER_TM = 1024
WIN = 256
SEL_ROWS = 2560
SEL_RG = 64
SEL_MM = 512
SEL_TAIL = 256
EXP_BM = 576
EXP_XDEPTH = 6
EXP_YDEPTH = 4
EXP_SPLIT = 4
RUN_FIELDS = 4
RUN_UNROLL = 4
CMB_LG = 512

V7X_VMEM_BYTES = 64 * 1024 * 1024
VMEM_LIMIT = V7X_VMEM_BYTES - 8 * 1024 * 1024

assert SEL_ROWS >= TOP_K * WIN + 7 * N_EXPERTS
assert (SEL_ROWS - 2 * SEL_TAIL) % SEL_MM == 0 and SEL_TAIL % SEL_RG == 0 and SEL_ROWS % CMB_LG == 0
assert N_EXPERTS % RUN_UNROLL == 0 and EXP_BM % (16 * EXP_SPLIT) == 0 and EXP_XDEPTH > 2 and EXP_YDEPTH >= 2


def _rms(x, g):
    ms = jnp.mean(x * x, axis=-1, keepdims=True)
    return x * lax.rsqrt(ms + NORM_EPS) * g


def _dot(a, b):
    return jnp.dot(a, b, preferred_element_type=F32)


def _dot_t(a, b):
    return lax.dot_general(a, b, (((0,), (0,)), ((), ())), preferred_element_type=F32)


def _mixer_kernel(x_ref, gmix_ref, win_ref, bin_ref, wdw_ref, bdw_ref, gcln_ref, bcln_ref,
                  wco_ref, bco_ref, wpool_ref, spool_ref, wout_ref, gffn_ref, perm_ref, unperm_ref,
                  x1_ref, h2_ref, a_ext, a_prev, u_ext, u_prev, c_buf, q_buf, *, tiles_per_seq):
    i = pl.program_id(0) % tiles_per_seq
    tm = MIX_TM
    nv = MIX_NV

    @pl.when(i == 0)
    def _():
        a_prev[...] = jnp.zeros(a_prev.shape, F32)
        u_prev[...] = jnp.zeros(u_prev.shape, F32)

    x = x_ref[...]
    h = _dot(perm_ref[...], _rms(x, gmix_ref[...]).astype(BF16)).astype(BF16)

    def proj(lo, hi):
        return _dot(h, win_ref[:, lo:hi]) + bin_ref[:, lo:hi]

    glu = proj(0, D_CONV) * jax.nn.sigmoid(proj(D_CONV, 2 * D_CONV))
    for lc in range(D_CONV // LANE):
        a_ext[lc, tm:2 * tm, :] = glu[:, lc * LANE:(lc + 1) * LANE]
    u_ext[tm:2 * tm, :] = proj(2 * D_CONV, 2 * D_CONV + D_POOL)

    def delayed_groups(ext, prev, first_group):
        last_row = lax.broadcasted_iota(jnp.int32, (8, ext.shape[-1]), 0) == 7
        for g in range(first_group, nv):
            rows = slice(8 * g, 8 * g + 8)
            mixed = jnp.where(last_row, prev[rows, :], ext[tm + 8 * g:tm + 8 * g + 8, :])
            ext[rows, :] = pltpu.roll(mixed, 1, axis=0)
            prev[rows, :] = ext[tm + 8 * g:tm + 8 * g + 8, :]

    delayed_groups(u_ext, u_prev, nv - (max(POOL_WINDOWS) - 1))

    for r0 in range(0, tm, ROW_CHUNK):
        row = r0 + lax.broadcasted_iota(jnp.int32, (ROW_CHUNK, POOL_GROUP), 0)
        t1 = i * tm + (row % 8) * nv + row // 8 + 1
        for gi, w in enumerate(POOL_WINDOWS):
            ls = slice(gi * POOL_GROUP, (gi + 1) * POOL_GROUP)
            tok = u_ext[tm + r0:tm + r0 + ROW_CHUNK, ls]
            s = tok
            for j in range(1, w):
                s = s + u_ext[tm + r0 - 8 * j:tm + r0 - 8 * j + ROW_CHUNK, ls]
            cnt = jnp.minimum(t1, w).astype(F32)
            q_buf[r0:r0 + ROW_CHUNK, ls] = s / cnt - tok

    qs_out = []
    for gi in range(len(POOL_WINDOWS)):
        ls = slice(gi * POOL_GROUP, (gi + 1) * POOL_GROUP)
        qs_out.append(_dot(q_buf[:, ls].astype(BF16), wpool_ref[gi]) * spool_ref[:, ls])
    branch_b = jnp.concatenate(qs_out, axis=-1)

    def conv_column(lc, carry):
        a_col = a_ext.at[lc]
        delayed_groups(a_col, a_prev.at[lc], nv - (CONV_WIDTH - 1))
        w_col = wdw_ref.at[lc]
        for g0 in range(0, nv, CONV_MG):
            acc = None
            for k in range(CONV_WIDTH):
                src = nv + g0 + k - (CONV_WIDTH - 1)
                term = a_col[8 * src:8 * (src + CONV_MG), :] * w_col[k:k + 1, :]
                acc = term if acc is None else acc + term
            c_buf[lc, 8 * g0:8 * (g0 + CONV_MG), :] = acc + bdw_ref[lc]
        return carry
    lax.fori_loop(0, D_CONV // LANE, conv_column, 0)

    c2 = 2 * D_CONV + D_POOL
    gate_a = jax.nn.sigmoid(proj(c2, c2 + D_MODEL))
    gate_b = jax.nn.sigmoid(proj(c2 + D_MODEL, c2 + 2 * D_MODEL))

    c = jnp.concatenate([c_buf[lc] for lc in range(D_CONV // LANE)], axis=-1)
    mu = jnp.mean(c, axis=-1, keepdims=True)
    xc = c - mu
    var = jnp.mean(xc * xc, axis=-1, keepdims=True)
    y = xc * lax.rsqrt(var + NORM_EPS) * gcln_ref[...] + bcln_ref[...]
    y = y * jax.nn.sigmoid(y)
    branch_a = _dot(y.astype(BF16), wco_ref[...]) + bco_ref[...]

    merged = gate_a * branch_a + gate_b * branch_b
    merged = _dot(unperm_ref[...], merged.astype(BF16)).astype(BF16)
    x1 = x + _dot(merged, wout_ref[...])
    x1_ref[...] = x1
    h2_ref[...] = _rms(x1, gffn_ref[...]).astype(BF16)


def _const_spec(shape):
    n = len(shape)
    return pl.BlockSpec(shape, lambda i, _n=n: (0,) * _n, pipeline_mode=pl.Buffered(1))


def _mixer(x, seq_len, g_mix, w_in, b_in, w_dw, b_dw, g_cln, b_cln, w_co, b_co, w_pool, s_pool, w_out,
           g_ffn):
    t = x.shape[0]
    tm = MIX_TM
    assert seq_len % tm == 0 and MIX_NV >= CONV_WIDTH and MIX_NV >= max(POOL_WINDOWS)
    d_in = w_in.shape[1]
    row = pl.BlockSpec((tm, D_MODEL), lambda i: (i, 0))
    n_col = D_CONV // LANE
    w_dw = w_dw.reshape(CONV_WIDTH, n_col, LANE).transpose(1, 0, 2)
    b_dw = b_dw.reshape(n_col, 1, LANE)
    r = jnp.arange(tm)
    perm = ((r % 8) * MIX_NV + r // 8)[:, None] == jnp.arange(tm)[None, :]
    perm = perm.astype(BF16)
    return pl.pallas_call(
        functools.partial(_mixer_kernel, tiles_per_seq=seq_len // tm),
        grid=(t // tm,),
        in_specs=[
            row,
            _const_spec((1, D_MODEL)),
            _const_spec((D_MODEL, d_in)),
            _const_spec((1, d_in)),
            _const_spec((n_col, CONV_WIDTH, LANE)),
            _const_spec((n_col, 1, LANE)),
            _const_spec((1, D_CONV)),
            _const_spec((1, D_CONV)),
            _const_spec((D_CONV, D_MODEL)),
            _const_spec((1, D_MODEL)),
            _const_spec((len(POOL_WINDOWS), POOL_GROUP, POOL_GROUP)),
            _const_spec((1, D_POOL)),
            _const_spec((D_MODEL, D_MODEL)),
            _const_spec((1, D_MODEL)),
            _const_spec((tm, tm)),
            _const_spec((tm, tm)),
        ],
        out_specs=[row, row],
        out_shape=[jax.ShapeDtypeStruct((t, D_MODEL), F32),
                   jax.ShapeDtypeStruct((t, D_MODEL), BF16)],
        scratch_shapes=[
            pltpu.VMEM((n_col, 2 * tm, LANE), F32),
            pltpu.VMEM((n_col, tm, LANE), F32),
            pltpu.VMEM((2 * tm, D_POOL), F32),
            pltpu.VMEM((tm, D_POOL), F32),
            pltpu.VMEM((n_col, tm, LANE), F32),
            pltpu.VMEM((tm, D_POOL), F32),
        ],
        compiler_params=pltpu.CompilerParams(
            dimension_semantics=("arbitrary",), vmem_limit_bytes=VMEM_LIMIT),
        name="mixer",
    )(x, g_mix, w_in, b_in, w_dw, b_dw, g_cln, b_cln, w_co, b_co, w_pool, s_pool, w_out, g_ffn, perm, perm.T)


def _beats(v, other, other_is_later):
    v = jnp.broadcast_to(v, other.shape)
    return jnp.where(other_is_later, jnp.where(v >= other, 1, 0), jnp.where(v > other, 1, 0))


def _router_kernel(h2_ref, wrt_ref, br_ref, utri_ref, ltri_ref, gate_ref, rank_ref, pos_ref, cnt_ref):
    tm = ROUTER_TM
    logits = lax.dot_general(wrt_ref[...], h2_ref[...], (((1,), (1,)), ((), ())),
                             preferred_element_type=F32)
    scores = jax.nn.sigmoid(logits)
    sel = scores + br_ref[...]
    shape3 = (N_GROUPS, GROUP_SIZE, tm)
    sel3 = sel.reshape(shape3)
    scores3 = scores.reshape(shape3)
    neg_inf = jnp.float32(-jnp.inf)

    member = lax.broadcasted_iota(jnp.int32, shape3, 1)
    m1 = jnp.max(sel3, axis=1, keepdims=True)
    first = jnp.min(jnp.where(sel3 == m1, member, GROUP_SIZE), axis=1, keepdims=True)
    m2 = jnp.max(jnp.where(member == first, neg_inf, sel3), axis=1, keepdims=True)
    gscore = jnp.broadcast_to(m1 + m2, shape3)

    gidx = lax.broadcasted_iota(jnp.int32, shape3, 0)
    grank = jnp.zeros(shape3, jnp.int32)
    for j in range(N_GROUPS):
        sj = gscore[j:j + 1]
        grank = grank + _beats(sj, gscore, gidx > j)
    masked = jnp.where(grank < TOPK_GROUPS, sel3, neg_inf)

    eidx = gidx * GROUP_SIZE + member
    work = masked
    erank = jnp.full(shape3, TOP_K, jnp.int32)
    for k in range(TOP_K):
        best = jnp.max(jnp.max(work, axis=0, keepdims=True), axis=1, keepdims=True)
        cand = jnp.where(work == best, eidx, N_EXPERTS)
        pick = jnp.min(jnp.min(cand, axis=0, keepdims=True), axis=1, keepdims=True)
        hit = eidx == pick
        work = jnp.where(hit, neg_inf, work)
        erank = jnp.where(hit, k, erank)
    chosen = erank < TOP_K
    top_s = jnp.where(chosen, scores3, 0.0)
    denom = jnp.sum(jnp.sum(top_s, axis=0, keepdims=True), axis=1, keepdims=True)
    gates3 = top_s / denom * ROUTED_SCALE
    chosen2 = jnp.where(chosen, 1.0, 0.0).reshape(N_EXPERTS, tm)
    gate_ref[...] = gates3.reshape(N_EXPERTS, tm).astype(BF16)

    for w in range(tm // WIN):
        ls = slice(w * WIN, (w + 1) * WIN)
        mw = chosen2[:, ls]
        rank = _dot(mw.astype(BF16), utri_ref[...])
        n = jnp.sum(mw, axis=1, keepdims=True)
        run = jnp.floor((n + 7.0) * 0.125) * 8.0
        start = _dot(ltri_ref[...], jnp.broadcast_to(run, (N_EXPERTS, WIN)).astype(BF16))
        rank_ref[:, ls] = jnp.where(mw > 0.5, rank, -1.0).astype(BF16)
        row3 = (rank + start).reshape(N_GROUPS, GROUP_SIZE, WIN)
        er = erank[:, :, ls]
        for k in range(TOP_K):
            pk = jnp.sum(jnp.sum(jnp.where(er == k, row3, 0.0), axis=0, keepdims=True), axis=1, keepdims=True)
            pos_ref[k:k + 1, ls] = pk.reshape(1, WIN).astype(jnp.int32)
        cnt_ref[w] = n


def _router(h2, w_rt, b_r):
    t = h2.shape[0]
    tm = ROUTER_TM
    utri = jnp.triu(jnp.ones((WIN, WIN), BF16), k=1)
    ltri = jnp.tril(jnp.ones((N_EXPERTS, N_EXPERTS), BF16), k=-1)
    return pl.pallas_call(
        _router_kernel,
        grid=(t // tm,),
        in_specs=[
            pl.BlockSpec((tm, D_MODEL), lambda i: (i, 0)),
            _const_spec((N_EXPERTS, D_MODEL)),
            _const_spec((N_EXPERTS, 1)),
            _const_spec((WIN, WIN)),
            _const_spec((N_EXPERTS, N_EXPERTS)),
        ],
        out_specs=[
            pl.BlockSpec((N_EXPERTS, tm), lambda i: (0, i)),
            pl.BlockSpec((N_EXPERTS, tm), lambda i: (0, i)),
            pl.BlockSpec((TOP_K, tm), lambda i: (0, i)),
            pl.BlockSpec((tm // WIN, N_EXPERTS, 1), lambda i: (i, 0, 0)),
        ],
        out_shape=[
            jax.ShapeDtypeStruct((N_EXPERTS, t), BF16),
            jax.ShapeDtypeStruct((N_EXPERTS, t), BF16),
            jax.ShapeDtypeStruct((TOP_K, t), jnp.int32),
            jax.ShapeDtypeStruct((t // WIN, N_EXPERTS, 1), F32),
        ],
        compiler_params=pltpu.CompilerParams(
            dimension_semantics=("arbitrary",), vmem_limit_bytes=VMEM_LIMIT),
        name="router",
    )(h2, w_rt, b_r, utri, ltri)


def _sorted_rows_bound(t):
    rows = t * TOP_K + (t // WIN) * N_EXPERTS * 7 + N_EXPERTS * (EXP_BM - 1)
    blocks = -(-rows // EXP_BM)
    return (blocks + blocks % 2) * EXP_BM


def _dispatch_plan(cnt, t):
    nw = t // WIN
    n = cnt.reshape(nw, N_EXPERTS).astype(jnp.int32)
    run = (n + 7) // 8 * 8
    local_end = jnp.cumsum(run, axis=1)
    local_off = jnp.concatenate([jnp.zeros((nw, 1), jnp.int32), local_end], axis=1)
    total = jnp.sum(run, axis=0)
    region = (total + EXP_BM - 1) // EXP_BM * EXP_BM
    eid = jnp.arange(N_EXPERTS, dtype=jnp.int32)
    last_owner = jnp.max(jnp.where(region > 0, eid, 0))
    odd = (jnp.sum(region) // EXP_BM) % 2
    region = region + jnp.where(eid == last_owner, odd * EXP_BM, 0)
    region_end = jnp.cumsum(region)
    base = region_end - region
    global_off = base[None, :] + jnp.cumsum(run, axis=0) - run
    n_blocks = _sorted_rows_bound(t) // EXP_BM
    n_used = region_end[-1] // EXP_BM
    blk = jnp.arange(n_blocks, dtype=jnp.int32)
    blk_expert = jnp.sum((region_end[None, :] <= blk[:, None] * EXP_BM).astype(jnp.int32), axis=1)
    blk_expert = jnp.minimum(blk_expert, N_EXPERTS - 1)
    later_nonempty = (eid[None, :] > eid[:, None]) & (region[None, :] > 0)
    next_expert = jnp.min(jnp.where(later_nonempty, eid[None, :], N_EXPERTS), axis=1).astype(jnp.int32)
    return dict(
        run_lo=local_off[:, :N_EXPERTS].reshape(nw, N_EXPERTS, 1),
        run_hi=local_off[:, 1:].reshape(nw, N_EXPERTS, 1),
        local_off=local_off.reshape(-1),
        runs=jnp.stack([local_off[:, :N_EXPERTS], run, global_off, jnp.zeros_like(run)], axis=-1).reshape(-1),
        fill_off=base + total, fill_cnt=region - total,
        blk_expert=blk_expert.astype(jnp.int32), next_expert=next_expert,
        n_used=n_used.reshape(1).astype(jnp.int32))


def _run_copy(runs_ref, win, e, vmem_buf, slot, hbm_buf, sem, to_hbm):
    p = (win * N_EXPERTS + e) * RUN_FIELDS
    lo = pl.multiple_of(runs_ref[p], 8)
    cnt = pl.multiple_of(runs_ref[p + 1], 8)
    go = pl.multiple_of(runs_ref[p + 2], 8)
    v = vmem_buf.at[pl.ds(pl.multiple_of(slot * SEL_ROWS + lo, 8), cnt)]
    h = hbm_buf.at[pl.ds(go, cnt)]
    cp = pltpu.make_async_copy(v, h, sem.at[slot]) if to_hbm else pltpu.make_async_copy(h, v, sem.at[slot])
    return cnt, cp


def _start_runs(runs_ref, win, vmem_buf, slot, hbm_buf, sem, to_hbm):
    def body(i, carry):
        copies = [_run_copy(runs_ref, win, i * RUN_UNROLL + j, vmem_buf, slot, hbm_buf, sem, to_hbm)
                  for j in range(RUN_UNROLL)]
        for cnt, cp in copies:
            @pl.when(cnt > 0)
            def _(cp=cp):
                cp.start()
        return carry
    lax.fori_loop(0, N_EXPERTS // RUN_UNROLL, body, 0)


def _wait_runs(local_ref, win, vmem_buf, slot, hbm_buf, sem, to_hbm):
    total = pl.multiple_of(local_ref[win * (N_EXPERTS + 1) + N_EXPERTS], 8)
    v = vmem_buf.at[pl.ds(pl.multiple_of(slot * SEL_ROWS, 8), total)]
    h = hbm_buf.at[pl.ds(0, total)]
    cp = pltpu.make_async_copy(v, h, sem.at[slot]) if to_hbm else pltpu.make_async_copy(h, v, sem.at[slot])

    @pl.when(total > 0)
    def _():
        cp.wait()


def _dispatch_kernel(local_ref, runs_ref, fill_off_ref, fill_cnt_ref, h2_ref, pos_ref, wsg_ref, wsu_ref,
                     xs_hbm, sbuf, hs_ref, s_ref, sem, zsem, *, n_win):
    w = pl.program_id(0)
    slot = w % 2
    pos = pos_ref[...]

    h2 = h2_ref[...]
    assert SEL_RG <= 256
    rid_b = lax.broadcasted_iota(jnp.int32, (SEL_RG, WIN), 0).astype(F32).astype(BF16)
    one_b = jnp.ones((SEL_RG, WIN), BF16)
    def compact(first, n_rows):
        for r0 in range(first, first + n_rows, SEL_RG):
            acc = jnp.zeros((SEL_RG, WIN), BF16)
            for k in range(TOP_K):
                off = (pos[k:k + 1, :] - r0).astype(F32)
                off = jnp.broadcast_to(off, (SEL_RG, WIN)).astype(BF16)
                acc = jnp.where(rid_b == off, one_b, acc)
            s_ref[r0:r0 + SEL_RG, :] = acc
        dst = pl.multiple_of(slot * SEL_ROWS + first, SEL_TAIL)
        sbuf[pl.ds(dst, n_rows), :] = _dot(s_ref[first:first + n_rows, :], h2).astype(BF16)

    @pl.when(w == 0)
    def _():
        for s in range(2):
            sbuf[(s + 1) * SEL_ROWS - SEL_TAIL:(s + 1) * SEL_ROWS, :] = jnp.zeros((SEL_TAIL, D_MODEL), BF16)

    for first in range(0, SEL_ROWS - 2 * SEL_TAIL, SEL_MM):
        compact(first, SEL_MM)
    hs = _dot(h2, wsg_ref[...])
    hs_ref[...] = (hs * jax.nn.sigmoid(hs) * _dot(h2, wsu_ref[...])).astype(BF16)
    compact(SEL_ROWS - 2 * SEL_TAIL, SEL_TAIL)

    @pl.when(local_ref[w * (N_EXPERTS + 1) + N_EXPERTS] > SEL_ROWS - SEL_TAIL)
    def _():
        compact(SEL_ROWS - SEL_TAIL, SEL_TAIL)

    _start_runs(runs_ref, w, sbuf, slot, xs_hbm, sem, True)

    @pl.when(w > 0)
    def _():
        _wait_runs(local_ref, w - 1, sbuf, 1 - slot, xs_hbm, sem, True)

    @pl.when(w == n_win - 1)
    def _():
        sbuf[2 * SEL_ROWS:, :] = jnp.zeros((2 * EXP_BM, D_MODEL), BF16)

        def fill(e, wait):
            cnt = pl.multiple_of(fill_cnt_ref[e], 8)
            off = pl.multiple_of(fill_off_ref[e], 8)
            cp = pltpu.make_async_copy(sbuf.at[pl.ds(2 * SEL_ROWS, cnt)], xs_hbm.at[pl.ds(off, cnt)], zsem)

            @pl.when(cnt > 0)
            def _():
                if wait:
                    cp.wait()
                else:
                    cp.start()

        def start_body(e, carry):
            fill(e, False)
            return carry

        def wait_body(e, carry):
            fill(e, True)
            return carry
        lax.fori_loop(0, N_EXPERTS, start_body, 0)
        _wait_runs(local_ref, w, sbuf, slot, xs_hbm, sem, True)
        lax.fori_loop(0, N_EXPERTS, wait_body, 0)


def _staging_shape(extra_rows):
    return jax.ShapeDtypeStruct((2 * SEL_ROWS + extra_rows, D_MODEL), BF16)


def _staging_spec(extra_rows):
    return pl.BlockSpec((2 * SEL_ROWS + extra_rows, D_MODEL), lambda w, *_: (0, 0))


def _dispatch(plan, h2, pos, wsg, wsu):
    t = h2.shape[0]
    n_win = t // WIN
    xs, _, hs = pl.pallas_call(
        functools.partial(_dispatch_kernel, n_win=n_win),
        grid_spec=pltpu.PrefetchScalarGridSpec(
            num_scalar_prefetch=4,
            grid=(n_win,),
            in_specs=[
                pl.BlockSpec((WIN, D_MODEL), lambda w, *_: (w, 0)),
                pl.BlockSpec((TOP_K, WIN), lambda w, *_: (0, w)),
                pl.BlockSpec((D_MODEL, D_EXPERT), lambda w, *_: (0, 0)),
                pl.BlockSpec((D_MODEL, D_EXPERT), lambda w, *_: (0, 0)),
            ],
            out_specs=[pl.BlockSpec(memory_space=pl.ANY), _staging_spec(2 * EXP_BM),
                       pl.BlockSpec((WIN, D_EXPERT), lambda w, *_: (w, 0))],
            scratch_shapes=[
                pltpu.VMEM((SEL_ROWS, WIN), BF16),
                pltpu.SemaphoreType.DMA((2,)),
                pltpu.SemaphoreType.DMA,
            ]),
        out_shape=[jax.ShapeDtypeStruct((_sorted_rows_bound(t), D_MODEL), BF16), _staging_shape(2 * EXP_BM),
                   jax.ShapeDtypeStruct((t, D_EXPERT), BF16)],
        compiler_params=pltpu.CompilerParams(
            dimension_semantics=("arbitrary",), vmem_limit_bytes=VMEM_LIMIT),
        name="dispatch",
    )(plan['local_off'], plan['runs'], plan['fill_off'], plan['fill_cnt'], h2, pos, wsg, wsu)
    return xs, hs


def _expert_kernel(blk_expert_ref, next_expert_ref, n_used_ref, xs_hbm, wg_hbm, wu_hbm, wd_hbm, ys_hbm,
                   xbuf, ybuf, wg_st, wu_st, wd_st, wg_bf, wu_bf, wd_bf, xsem, ysem, wsem):
    n_used = n_used_ref[0]
    part = EXP_BM // EXP_SPLIT

    def row_copies(b, slot, fetch):
        out = []
        for q in range(EXP_SPLIT):
            hbm_rows = pl.ds(pl.multiple_of(b * EXP_BM + q * part, part), part)
            if fetch:
                out.append(pltpu.make_async_copy(xs_hbm.at[hbm_rows], xbuf.at[slot, q * part:(q + 1) * part],
                                                 xsem.at[slot]))
            else:
                out.append(pltpu.make_async_copy(ybuf.at[slot, q * part:(q + 1) * part], ys_hbm.at[hbm_rows],
                                                 ysem.at[slot]))
        return out

    def weight_copies(e, slot):
        return [pltpu.make_async_copy(wg_hbm.at[e], wg_st.at[slot], wsem.at[slot]),
                pltpu.make_async_copy(wu_hbm.at[e], wu_st.at[slot], wsem.at[slot]),
                pltpu.make_async_copy(wd_hbm.at[e], wd_st.at[slot], wsem.at[slot])]

    def start(copies):
        for c in copies:
            c.start()

    def wait(copies):
        for c in copies:
            c.wait()

    for ahead in range(EXP_XDEPTH - 2):
        @pl.when(ahead < n_used)
        def _(ahead=ahead):
            start(row_copies(ahead, ahead, True))

    @pl.when(n_used > 0)
    def _():
        start(weight_copies(blk_expert_ref[0], 0))

    def enter_block(b, wset):
        e = blk_expert_ref[b]
        new_expert = jnp.logical_or(b == 0, e != blk_expert_ref[jnp.maximum(b - 1, 0)])
        wset = jnp.where(new_expert, 1 - wset, wset)

        @pl.when(new_expert)
        def _():
            wait(weight_copies(e, wset))
            wg_bf[wset] = wg_st[wset].astype(BF16)
            wu_bf[wset] = wu_st[wset].astype(BF16)
            wd_bf[wset] = wd_st[wset].astype(BF16)
            nxt = next_expert_ref[e]

            @pl.when(nxt < N_EXPERTS)
            def _():
                start(weight_copies(nxt, 1 - wset))

        ahead = b + EXP_XDEPTH - 2

        @pl.when(ahead < n_used)
        def _():
            start(row_copies(ahead, ahead % EXP_XDEPTH, True))

        wait(row_copies(b, b % EXP_XDEPTH, True))

        @pl.when(b >= EXP_YDEPTH)
        def _():
            wait(row_copies(b - EXP_YDEPTH, b % EXP_YDEPTH, False))
        return wset

    def compute(b, wset):
        x = xbuf[b % EXP_XDEPTH]
        hg = _dot(x, wg_bf[wset])
        hb = hg * jax.nn.sigmoid(hg) * _dot(x, wu_bf[wset])
        ybuf[b % EXP_YDEPTH] = _dot(hb.astype(BF16), wd_bf[wset]).astype(BF16)

    def body(p, wset):
        b0 = 2 * p
        w0 = enter_block(b0, wset)
        w1 = enter_block(b0 + 1, w0)
        compute(b0, w0)
        compute(b0 + 1, w1)
        start(row_copies(b0, b0 % EXP_YDEPTH, False))
        start(row_copies(b0 + 1, (b0 + 1) % EXP_YDEPTH, False))
        return w1

    lax.fori_loop(0, n_used // 2, body, jnp.int32(1))

    for back in range(EXP_YDEPTH, 0, -1):
        @pl.when(n_used >= back)
        def _(back=back):
            wait(row_copies(n_used - back, (n_used - back) % EXP_YDEPTH, False))


def _experts(plan, xs, w_gate, w_up, w_down):
    any_spec = pl.BlockSpec(memory_space=pl.ANY)
    return pl.pallas_call(
        _expert_kernel,
        grid_spec=pltpu.PrefetchScalarGridSpec(
            num_scalar_prefetch=3,
            grid=(1,),
            in_specs=[any_spec, any_spec, any_spec, any_spec],
            out_specs=any_spec,
            scratch_shapes=[
                pltpu.VMEM((EXP_XDEPTH, EXP_BM, D_MODEL), BF16),
                pltpu.VMEM((EXP_YDEPTH, EXP_BM, D_MODEL), BF16),
                pltpu.VMEM((2, D_MODEL, D_EXPERT), F32),
                pltpu.VMEM((2, D_MODEL, D_EXPERT), F32),
                pltpu.VMEM((2, D_EXPERT, D_MODEL), F32),
                pltpu.VMEM((2, D_MODEL, D_EXPERT), BF16),
                pltpu.VMEM((2, D_MODEL, D_EXPERT), BF16),
                pltpu.VMEM((2, D_EXPERT, D_MODEL), BF16),
                pltpu.SemaphoreType.DMA((EXP_XDEPTH,)),
                pltpu.SemaphoreType.DMA((EXP_YDEPTH,)),
                pltpu.SemaphoreType.DMA((2,)),
            ]),
        out_shape=jax.ShapeDtypeStruct(xs.shape, BF16),
        compiler_params=pltpu.CompilerParams(
            dimension_semantics=("arbitrary",), vmem_limit_bytes=VMEM_LIMIT),
        name="experts",
    )(plan['blk_expert'], plan['next_expert'], plan['n_used'], xs, w_gate, w_up, w_down)


def _combine_kernel(local_ref, runs_ref, x1_ref, hs_ref, p_ref, rank_ref, gate_ref, lo_ref, hi_ref,
                    wsd_ref, gple_ref, wpg_ref, wp_ref, gfin_ref, ys_hbm, o_ref, ybuf, st_ref, sem,
                    *, n_win, final_norm):
    w = pl.program_id(0)
    slot = w % 2

    @pl.when(w == 0)
    def _():
        ybuf[...] = jnp.zeros(ybuf.shape, BF16)
        _start_runs(runs_ref, w, ybuf, slot, ys_hbm, sem, False)

    @pl.when(w + 1 < n_win)
    def _():
        _start_runs(runs_ref, w + 1, ybuf, 1 - slot, ys_hbm, sem, False)

    lo = lo_ref[0]
    hi = hi_ref[0]
    lo_f = lo.astype(F32)
    rank_tbl = rank_ref[...]
    gate_tbl = gate_ref[...]

    def build_group(lg):
        cols = slice(lg * CMB_LG, (lg + 1) * CMB_LG)
        rid = lg * CMB_LG + lax.broadcasted_iota(jnp.int32, (N_EXPERTS, CMB_LG), 1)
        owner = jnp.where(rid >= lo, jnp.where(rid < hi, 1.0, 0.0), 0.0)
        run_row = rid[0:1, :].astype(F32) - jnp.sum(owner * lo_f, axis=0, keepdims=True)
        owner = owner.astype(BF16)
        hit = _dot_t(rank_tbl, owner) == run_row
        st_ref[:, cols] = jnp.where(hit, _dot_t(gate_tbl, owner), 0.0).astype(BF16)

    build_group(0)
    shared = _dot(hs_ref[...], wsd_ref[...])

    _wait_runs(local_ref, w, ybuf, slot, ys_hbm, sem, False)
    routed = None
    n_groups = SEL_ROWS // CMB_LG
    for lg in range(n_groups):
        if lg + 1 < n_groups:
            build_group(lg + 1)
        src = pl.multiple_of(slot * SEL_ROWS + lg * CMB_LG, CMB_LG)
        part = _dot(st_ref[:, lg * CMB_LG:(lg + 1) * CMB_LG], ybuf[pl.ds(src, CMB_LG), :])
        routed = part if routed is None else routed + part
    x2 = x1_ref[...] + routed + shared

    hp = _rms(x2, gple_ref[...]).astype(BF16)
    gate = jax.nn.sigmoid(_dot(hp, wpg_ref[...]))
    x3 = x2 + gate * _dot(p_ref[...].astype(BF16), wp_ref[...])
    o_ref[...] = _rms(x3, gfin_ref[...]) if final_norm else x3


def _combine(plan, ys, x1, hs, p, rank_tbl, gate_tbl, wsd, g_ple, w_pg, w_p, g_fin, final_norm):
    t = x1.shape[0]
    n_win = t // WIN
    row = lambda width: pl.BlockSpec((WIN, width), lambda w, *_: (w, 0))
    const = lambda shape: pl.BlockSpec(shape, lambda w, *_: (0,) * len(shape))
    return pl.pallas_call(
        functools.partial(_combine_kernel, n_win=n_win, final_norm=final_norm),
        grid_spec=pltpu.PrefetchScalarGridSpec(
            num_scalar_prefetch=2,
            grid=(n_win,),
            in_specs=[
                row(D_MODEL), row(D_EXPERT), row(PLE_DIM),
                pl.BlockSpec((N_EXPERTS, WIN), lambda w, *_: (0, w)),
                pl.BlockSpec((N_EXPERTS, WIN), lambda w, *_: (0, w)),
                pl.BlockSpec((1, N_EXPERTS, 1), lambda w, *_: (w, 0, 0)),
                pl.BlockSpec((1, N_EXPERTS, 1), lambda w, *_: (w, 0, 0)),
                const((D_EXPERT, D_MODEL)),
                const((1, D_MODEL)), const((D_MODEL, D_MODEL)), const((PLE_DIM, D_MODEL)),
                const((1, D_MODEL)),
                pl.BlockSpec(memory_space=pl.ANY),
            ],
            out_specs=[row(D_MODEL), _staging_spec(0)],
            scratch_shapes=[
                pltpu.VMEM((WIN, SEL_ROWS), BF16),
                pltpu.SemaphoreType.DMA((2,)),
            ]),
        out_shape=[jax.ShapeDtypeStruct((t, D_MODEL), F32), _staging_shape(0)],
        compiler_params=pltpu.CompilerParams(
            dimension_semantics=("arbitrary",), vmem_limit_bytes=VMEM_LIMIT),
        name="combine",
    )(plan['local_off'], plan['runs'], x1, hs, p, rank_tbl, gate_tbl, plan['run_lo'], plan['run_hi'],
      wsd, g_ple, w_pg, w_p, g_fin, ys)[0]


def kernel(x, p, g_mix, w_in, b_in, w_dw, b_dw, g_cln, b_cln, w_conv_out, b_conv_out, w_pool, s_pool,
           w_out, g_ffn, w_router, b_router, w_e_gate, w_e_up, w_e_down, w_s_gate, w_s_up, w_s_down,
           g_ple, w_ple_gate, w_ple, g_final):
    bsz, s, d = x.shape
    t = bsz * s
    depth = w_in.shape[0]
    xt = x.reshape(t, d)
    row = lambda v: v.reshape(1, -1)
    for i in range(depth):
        x1, h2 = _mixer(
            xt, s, row(g_mix[i]), w_in[i].astype(BF16), row(b_in[i]), w_dw[i], row(b_dw[i]),
            row(g_cln[i]), row(b_cln[i]), w_conv_out[i].astype(BF16), row(b_conv_out[i]),
            w_pool[i].astype(BF16), row(s_pool[i]), w_out[i].astype(BF16), row(g_ffn[i]))
        gate, rank, pos, cnt = _router(h2, w_router[i].T.astype(BF16), b_router[i].reshape(N_EXPERTS, 1))
        plan = _dispatch_plan(cnt, t)
        xs, hs = _dispatch(plan, h2, pos, w_s_gate[i].astype(BF16), w_s_up[i].astype(BF16))
        ys = _experts(plan, xs, w_e_gate[i], w_e_up[i], w_e_down[i])
        xt = _combine(
            plan, ys, x1, hs, p[i].reshape(t, PLE_DIM), rank, gate, w_s_down[i].astype(BF16),
            row(g_ple[i]), w_ple_gate[i].astype(BF16), w_ple[i].astype(BF16), row(g_final),
            final_norm=(i == depth - 1))
    return xt.reshape(bsz, s, d)
```

```python
import functools

import jax
import jax.numpy as jnp
from jax import lax
from jax.experimental import pallas as pl
from jax.experimental.pallas import tpu as pltpu

D_MODEL = 1024
D_CONV = 1024
D_POOL = 1024
CONV_WIDTH = 31
POOL_WINDOWS = (2, 4, 8, 16)
POOL_GROUP = 256
PLE_DIM = 256
N_EXPERTS = 64
N_GROUPS = 8
GROUP_SIZE = N_EXPERTS // N_GROUPS
TOPK_GROUPS = 4
TOP_K = 8
D_EXPERT = 256
ROUTED_SCALE = 2.5
NORM_EPS = 1e-6

F32 = jnp.float32
BF16 = jnp.bfloat16

MIX_TM = 512
MIX_NV = MIX_TM // 8
CONV_MG = 8
ROW_CHUNK = 64
LANE = 128

ROUTER_TM = 1024
WIN = 256
SEL_ROWS = 2560
SEL_RG = 64
SEL_MM = 512
SEL_TAIL = 256
EXP_BM = 576
EXP_XDEPTH = 6
EXP_YDEPTH = 4
EXP_SPLIT = 4
RUN_FIELDS = 4
RUN_UNROLL = 4
CMB_LG = 512

V7X_VMEM_BYTES = 64 * 1024 * 1024
VMEM_LIMIT = V7X_VMEM_BYTES - 8 * 1024 * 1024

assert SEL_ROWS >= TOP_K * WIN + 7 * N_EXPERTS
assert (SEL_ROWS - 2 * SEL_TAIL) % SEL_MM == 0 and SEL_TAIL % SEL_RG == 0 and SEL_ROWS % CMB_LG == 0
assert N_EXPERTS % RUN_UNROLL == 0 and EXP_BM % (16 * EXP_SPLIT) == 0 and EXP_XDEPTH > 2 and EXP_YDEPTH >= 2


def _rms(x, g):
    ms = jnp.mean(x * x, axis=-1, keepdims=True)
    return x * lax.rsqrt(ms + NORM_EPS) * g


def _dot(a, b):
    return jnp.dot(a, b, preferred_element_type=F32)


def _dot_t(a, b):
    return lax.dot_general(a, b, (((0,), (0,)), ((), ())), preferred_element_type=F32)


def _mixer_kernel(x_ref, gmix_ref, win_ref, bin_ref, wdw_ref, bdw_ref, gcln_ref, bcln_ref,
                  wco_ref, bco_ref, wpool_ref, spool_ref, wout_ref, gffn_ref,
                  x1_ref, h2_ref, a_ext, a_prev, u_ext, u_prev, c_buf, q_buf, p_buf, *, tiles_per_seq):
    i = pl.program_id(0) % tiles_per_seq
    tm = MIX_TM
    nv = MIX_NV

    @pl.when(i == 0)
    def _():
        a_prev[...] = jnp.zeros(a_prev.shape, F32)
        u_prev[...] = jnp.zeros(u_prev.shape, F32)

    n_col = D_MODEL // LANE
    hn = _rms(x_ref[...], gmix_ref[...])
    for lc in range(n_col):
        for s in range(8):
            p_buf[lc, pl.ds(s, nv, stride=8), :] = hn[s * nv:(s + 1) * nv, lc * LANE:(lc + 1) * LANE]
    h = jnp.concatenate([p_buf[lc] for lc in range(n_col)], axis=-1).astype(BF16)

    def proj(lo, hi):
        return _dot(h, win_ref[:, lo:hi]) + bin_ref[:, lo:hi]

    glu = proj(0, D_CONV) * jax.nn.sigmoid(proj(D_CONV, 2 * D_CONV))
    for lc in range(D_CONV // LANE):
        a_ext[lc, tm:2 * tm, :] = glu[:, lc * LANE:(lc + 1) * LANE]
    u_ext[tm:2 * tm, :] = proj(2 * D_CONV, 2 * D_CONV + D_POOL)

    def delayed_groups(ext, prev, first_group):
        last_row = lax.broadcasted_iota(jnp.int32, (8, ext.shape[-1]), 0) == 7
        for g in range(first_group, nv):
            rows = slice(8 * g, 8 * g + 8)
            mixed = jnp.where(last_row, prev[rows, :], ext[tm + 8 * g:tm + 8 * g + 8, :])
            ext[rows, :] = pltpu.roll(mixed, 1, axis=0)
            prev[rows, :] = ext[tm + 8 * g:tm + 8 * g + 8, :]

    delayed_groups(u_ext, u_prev, nv - (max(POOL_WINDOWS) - 1))

    for r0 in range(0, tm, ROW_CHUNK):
        row = r0 + lax.broadcasted_iota(jnp.int32, (ROW_CHUNK, POOL_GROUP), 0)
        t1 = i * tm + (row % 8) * nv + row // 8 + 1
        for gi, w in enumerate(POOL_WINDOWS):
            ls = slice(gi * POOL_GROUP, (gi + 1) * POOL_GROUP)
            tok = u_ext[tm + r0:tm + r0 + ROW_CHUNK, ls]
            s = tok
            for j in range(1, w):
                s = s + u_ext[tm + r0 - 8 * j:tm + r0 - 8 * j + ROW_CHUNK, ls]
            cnt = jnp.minimum(t1, w).astype(F32)
            q_buf[r0:r0 + ROW_CHUNK, ls] = s / cnt - tok

    qs_out = []
    for gi in range(len(POOL_WINDOWS)):
        ls = slice(gi * POOL_GROUP, (gi + 1) * POOL_GROUP)
        qs_out.append(_dot(q_buf[:, ls].astype(BF16), wpool_ref[gi]) * spool_ref[:, ls])
    branch_b = jnp.concatenate(qs_out, axis=-1)

    def conv_column(lc, carry):
        a_col = a_ext.at[lc]
        delayed_groups(a_col, a_prev.at[lc], nv - (CONV_WIDTH - 1))
        w_col = wdw_ref.at[lc]
        for g0 in range(0, nv, CONV_MG):
            acc = None
            for k in range(CONV_WIDTH):
                src = nv + g0 + k - (CONV_WIDTH - 1)
                term = a_col[8 * src:8 * (src + CONV_MG), :] * w_col[k:k + 1, :]
                acc = term if acc is None else acc + term
            c_buf[lc, 8 * g0:8 * (g0 + CONV_MG), :] = acc + bdw_ref[lc]
        return carry
    lax.fori_loop(0, D_CONV // LANE, conv_column, 0)

    c2 = 2 * D_CONV + D_POOL
    gate_a = jax.nn.sigmoid(proj(c2, c2 + D_MODEL))
    gate_b = jax.nn.sigmoid(proj(c2 + D_MODEL, c2 + 2 * D_MODEL))

    c = jnp.concatenate([c_buf[lc] for lc in range(D_CONV // LANE)], axis=-1)
    mu = jnp.mean(c, axis=-1, keepdims=True)
    xc = c - mu
    var = jnp.mean(xc * xc, axis=-1, keepdims=True)
    y = xc * lax.rsqrt(var + NORM_EPS) * gcln_ref[...] + bcln_ref[...]
    y = y * jax.nn.sigmoid(y)
    branch_a = _dot(y.astype(BF16), wco_ref[...]) + bco_ref[...]

    merged = gate_a * branch_a + gate_b * branch_b
    y = _dot(merged.astype(BF16), wout_ref[...])
    for lc in range(n_col):
        p_buf[lc] = y[:, lc * LANE:(lc + 1) * LANE]
    for lc in range(n_col):
        cols = slice(lc * LANE, (lc + 1) * LANE)
        for s in range(8):
            for j in range(nv // 8):
                rows = slice(s * nv + 8 * j, s * nv + 8 * j + 8)
                x1_ref[rows, cols] = x_ref[rows, cols] + p_buf[lc, pl.ds(8 * (8 * j) + s, 8, stride=8), :]
    h2_ref[...] = _rms(x1_ref[...], gffn_ref[...]).astype(BF16)


def _const_spec(shape):
    n = len(shape)
    return pl.BlockSpec(shape, lambda i, _n=n: (0,) * _n, pipeline_mode=pl.Buffered(1))


def _mixer(x, seq_len, g_mix, w_in, b_in, w_dw, b_dw, g_cln, b_cln, w_co, b_co, w_pool, s_pool, w_out,
           g_ffn):
    t = x.shape[0]
    tm = MIX_TM
    assert seq_len % tm == 0 and MIX_NV >= CONV_WIDTH and MIX_NV >= max(POOL_WINDOWS)
    d_in = w_in.shape[1]
    row = pl.BlockSpec((tm, D_MODEL), lambda i: (i, 0))
    n_col = D_CONV // LANE
    w_dw = w_dw.reshape(CONV_WIDTH, n_col, LANE).transpose(1, 0, 2)
    b_dw = b_dw.reshape(n_col, 1, LANE)
    return pl.pallas_call(
        functools.partial(_mixer_kernel, tiles_per_seq=seq_len // tm),
        grid=(t // tm,),
        in_specs=[
            row,
            _const_spec((1, D_MODEL)),
            _const_spec((D_MODEL, d_in)),
            _const_spec((1, d_in)),
            _const_spec((n_col, CONV_WIDTH, LANE)),
            _const_spec((n_col, 1, LANE)),
            _const_spec((1, D_CONV)),
            _const_spec((1, D_CONV)),
            _const_spec((D_CONV, D_MODEL)),
            _const_spec((1, D_MODEL)),
            _const_spec((len(POOL_WINDOWS), POOL_GROUP, POOL_GROUP)),
            _const_spec((1, D_POOL)),
            _const_spec((D_MODEL, D_MODEL)),
            _const_spec((1, D_MODEL)),
        ],
        out_specs=[row, row],
        out_shape=[jax.ShapeDtypeStruct((t, D_MODEL), F32),
                   jax.ShapeDtypeStruct((t, D_MODEL), BF16)],
        scratch_shapes=[
            pltpu.VMEM((n_col, 2 * tm, LANE), F32),
            pltpu.VMEM((n_col, tm, LANE), F32),
            pltpu.VMEM((2 * tm, D_POOL), F32),
            pltpu.VMEM((tm, D_POOL), F32),
            pltpu.VMEM((n_col, tm, LANE), F32),
            pltpu.VMEM((tm, D_POOL), F32),
            pltpu.VMEM((D_MODEL // LANE, tm, LANE), F32),
        ],
        compiler_params=pltpu.CompilerParams(
            dimension_semantics=("arbitrary",), vmem_limit_bytes=VMEM_LIMIT),
        name="mixer",
    )(x, g_mix, w_in, b_in, w_dw, b_dw, g_cln, b_cln, w_co, b_co, w_pool, s_pool, w_out, g_ffn)


def _beats(v, other, other_is_later):
    v = jnp.broadcast_to(v, other.shape)
    return jnp.where(other_is_later, jnp.where(v >= other, 1, 0), jnp.where(v > other, 1, 0))


def _router_kernel(h2_ref, wrt_ref, br_ref, utri_ref, ltri_ref, gate_ref, rank_ref, pos_ref, cnt_ref):
    tm = ROUTER_TM
    logits = lax.dot_general(wrt_ref[...], h2_ref[...], (((1,), (1,)), ((), ())),
                             preferred_element_type=F32)
    scores = jax.nn.sigmoid(logits)
    sel = scores + br_ref[...]
    shape3 = (N_GROUPS, GROUP_SIZE, tm)
    sel3 = sel.reshape(shape3)
    scores3 = scores.reshape(shape3)
    neg_inf = jnp.float32(-jnp.inf)

    member = lax.broadcasted_iota(jnp.int32, shape3, 1)
    m1 = jnp.max(sel3, axis=1, keepdims=True)
    first = jnp.min(jnp.where(sel3 == m1, member, GROUP_SIZE), axis=1, keepdims=True)
    m2 = jnp.max(jnp.where(member == first, neg_inf, sel3), axis=1, keepdims=True)
    gscore = jnp.broadcast_to(m1 + m2, shape3)

    gidx = lax.broadcasted_iota(jnp.int32, shape3, 0)
    grank = jnp.zeros(shape3, jnp.int32)
    for j in range(N_GROUPS):
        sj = gscore[j:j + 1]
        grank = grank + _beats(sj, gscore, gidx > j)
    masked = jnp.where(grank < TOPK_GROUPS, sel3, neg_inf)

    eidx = gidx * GROUP_SIZE + member
    work = masked
    erank = jnp.full(shape3, TOP_K, jnp.int32)
    for k in range(TOP_K):
        best = jnp.max(jnp.max(work, axis=0, keepdims=True), axis=1, keepdims=True)
        cand = jnp.where(work == best, eidx, N_EXPERTS)
        pick = jnp.min(jnp.min(cand, axis=0, keepdims=True), axis=1, keepdims=True)
        hit = eidx == pick
        work = jnp.where(hit, neg_inf, work)
        erank = jnp.where(hit, k, erank)
    chosen = erank < TOP_K
    top_s = jnp.where(chosen, scores3, 0.0)
    denom = jnp.sum(jnp.sum(top_s, axis=0, keepdims=True), axis=1, keepdims=True)
    gates3 = top_s / denom * ROUTED_SCALE
    chosen2 = jnp.where(chosen, 1.0, 0.0).reshape(N_EXPERTS, tm)
    gate_ref[...] = gates3.reshape(N_EXPERTS, tm).astype(BF16)

    for w in range(tm // WIN):
        ls = slice(w * WIN, (w + 1) * WIN)
        mw = chosen2[:, ls]
        rank = _dot(mw.astype(BF16), utri_ref[...])
        n = jnp.sum(mw, axis=1, keepdims=True)
        run = jnp.floor((n + 7.0) * 0.125) * 8.0
        start = _dot(ltri_ref[...], jnp.broadcast_to(run, (N_EXPERTS, WIN)).astype(BF16))
        rank_ref[:, ls] = jnp.where(mw > 0.5, rank, -1.0).astype(BF16)
        row3 = (rank + start).reshape(N_GROUPS, GROUP_SIZE, WIN)
        er = erank[:, :, ls]
        for k in range(TOP_K):
            pk = jnp.sum(jnp.sum(jnp.where(er == k, row3, 0.0), axis=0, keepdims=True), axis=1, keepdims=True)
            pos_ref[k:k + 1, ls] = pk.reshape(1, WIN).astype(jnp.int32)
        cnt_ref[w] = n


def _router(h2, w_rt, b_r):
    t = h2.shape[0]
    tm = ROUTER_TM
    utri = jnp.triu(jnp.ones((WIN, WIN), BF16), k=1)
    ltri = jnp.tril(jnp.ones((N_EXPERTS, N_EXPERTS), BF16), k=-1)
    return pl.pallas_call(
        _router_kernel,
        grid=(t // tm,),
        in_specs=[
            pl.BlockSpec((tm, D_MODEL), lambda i: (i, 0)),
            _const_spec((N_EXPERTS, D_MODEL)),
            _const_spec((N_EXPERTS, 1)),
            _const_spec((WIN, WIN)),
            _const_spec((N_EXPERTS, N_EXPERTS)),
        ],
        out_specs=[
            pl.BlockSpec((N_EXPERTS, tm), lambda i: (0, i)),
            pl.BlockSpec((N_EXPERTS, tm), lambda i: (0, i)),
            pl.BlockSpec((TOP_K, tm), lambda i: (0, i)),
            pl.BlockSpec((tm // WIN, N_EXPERTS, 1), lambda i: (i, 0, 0)),
        ],
        out_shape=[
            jax.ShapeDtypeStruct((N_EXPERTS, t), BF16),
            jax.ShapeDtypeStruct((N_EXPERTS, t), BF16),
            jax.ShapeDtypeStruct((TOP_K, t), jnp.int32),
            jax.ShapeDtypeStruct((t // WIN, N_EXPERTS, 1), F32),
        ],
        compiler_params=pltpu.CompilerParams(
            dimension_semantics=("arbitrary",), vmem_limit_bytes=VMEM_LIMIT),
        name="router",
    )(h2, w_rt, b_r, utri, ltri)


def _sorted_rows_bound(t):
    rows = t * TOP_K + (t // WIN) * N_EXPERTS * 7 + N_EXPERTS * (EXP_BM - 1)
    blocks = -(-rows // EXP_BM)
    return (blocks + blocks % 2) * EXP_BM


def _dispatch_plan(cnt, t):
    nw = t // WIN
    n = cnt.reshape(nw, N_EXPERTS).astype(jnp.int32)
    run = (n + 7) // 8 * 8
    local_end = jnp.cumsum(run, axis=1)
    local_off = jnp.concatenate([jnp.zeros((nw, 1), jnp.int32), local_end], axis=1)
    total = jnp.sum(run, axis=0)
    region = (total + EXP_BM - 1) // EXP_BM * EXP_BM
    eid = jnp.arange(N_EXPERTS, dtype=jnp.int32)
    last_owner = jnp.max(jnp.where(region > 0, eid, 0))
    odd = (jnp.sum(region) // EXP_BM) % 2
    region = region + jnp.where(eid == last_owner, odd * EXP_BM, 0)
    region_end = jnp.cumsum(region)
    base = region_end - region
    global_off = base[None, :] + jnp.cumsum(run, axis=0) - run
    n_blocks = _sorted_rows_bound(t) // EXP_BM
    n_used = region_end[-1] // EXP_BM
    blk = jnp.arange(n_blocks, dtype=jnp.int32)
    blk_expert = jnp.sum((region_end[None, :] <= blk[:, None] * EXP_BM).astype(jnp.int32), axis=1)
    blk_expert = jnp.minimum(blk_expert, N_EXPERTS - 1)
    later_nonempty = (eid[None, :] > eid[:, None]) & (region[None, :] > 0)
    next_expert = jnp.min(jnp.where(later_nonempty, eid[None, :], N_EXPERTS), axis=1).astype(jnp.int32)
    return dict(
        run_lo=local_off[:, :N_EXPERTS].reshape(nw, N_EXPERTS, 1),
        run_hi=local_off[:, 1:].reshape(nw, N_EXPERTS, 1),
        local_off=local_off.reshape(-1),
        runs=jnp.stack([local_off[:, :N_EXPERTS], run, global_off, jnp.zeros_like(run)], axis=-1).reshape(-1),
        fill_off=base + total, fill_cnt=region - total,
        blk_expert=blk_expert.astype(jnp.int32), next_expert=next_expert,
        n_used=n_used.reshape(1).astype(jnp.int32))


def _run_copy(runs_ref, win, e, vmem_buf, slot, hbm_buf, sem, to_hbm):
    p = (win * N_EXPERTS + e) * RUN_FIELDS
    lo = pl.multiple_of(runs_ref[p], 8)
    cnt = pl.multiple_of(runs_ref[p + 1], 8)
    go = pl.multiple_of(runs_ref[p + 2], 8)
    v = vmem_buf.at[pl.ds(pl.multiple_of(slot * SEL_ROWS + lo, 8), cnt)]
    h = hbm_buf.at[pl.ds(go, cnt)]
    cp = pltpu.make_async_copy(v, h, sem.at[slot]) if to_hbm else pltpu.make_async_copy(h, v, sem.at[slot])
    return cnt, cp


def _start_runs(runs_ref, win, vmem_buf, slot, hbm_buf, sem, to_hbm):
    def body(i, carry):
        copies = [_run_copy(runs_ref, win, i * RUN_UNROLL + j, vmem_buf, slot, hbm_buf, sem, to_hbm)
                  for j in range(RUN_UNROLL)]
        for cnt, cp in copies:
            @pl.when(cnt > 0)
            def _(cp=cp):
                cp.start()
        return carry
    lax.fori_loop(0, N_EXPERTS // RUN_UNROLL, body, 0)


def _wait_runs(local_ref, win, vmem_buf, slot, hbm_buf, sem, to_hbm):
    total = pl.multiple_of(local_ref[win * (N_EXPERTS + 1) + N_EXPERTS], 8)
    v = vmem_buf.at[pl.ds(pl.multiple_of(slot * SEL_ROWS, 8), total)]
    h = hbm_buf.at[pl.ds(0, total)]
    cp = pltpu.make_async_copy(v, h, sem.at[slot]) if to_hbm else pltpu.make_async_copy(h, v, sem.at[slot])

    @pl.when(total > 0)
    def _():
        cp.wait()


def _dispatch_kernel(local_ref, runs_ref, fill_off_ref, fill_cnt_ref, h2_ref, pos_ref, wsg_ref, wsu_ref,
                     xs_hbm, sbuf, hs_ref, s_ref, sem, zsem, *, n_win):
    w = pl.program_id(0)
    slot = w % 2
    pos = pos_ref[...]

    h2 = h2_ref[...]
    assert SEL_RG <= 256
    rid_b = lax.broadcasted_iota(jnp.int32, (SEL_RG, WIN), 0).astype(F32).astype(BF16)
    one_b = jnp.ones((SEL_RG, WIN), BF16)
    def compact(first, n_rows):
        for r0 in range(first, first + n_rows, SEL_RG):
            acc = jnp.zeros((SEL_RG, WIN), BF16)
            for k in range(TOP_K):
                off = (pos[k:k + 1, :] - r0).astype(F32)
                off = jnp.broadcast_to(off, (SEL_RG, WIN)).astype(BF16)
                acc = jnp.where(rid_b == off, one_b, acc)
            s_ref[r0:r0 + SEL_RG, :] = acc
        dst = pl.multiple_of(slot * SEL_ROWS + first, SEL_TAIL)
        sbuf[pl.ds(dst, n_rows), :] = _dot(s_ref[first:first + n_rows, :], h2).astype(BF16)

    @pl.when(w == 0)
    def _():
        for s in range(2):
            sbuf[(s + 1) * SEL_ROWS - SEL_TAIL:(s + 1) * SEL_ROWS, :] = jnp.zeros((SEL_TAIL, D_MODEL), BF16)

    for first in range(0, SEL_ROWS - 2 * SEL_TAIL, SEL_MM):
        compact(first, SEL_MM)
    hs = _dot(h2, wsg_ref[...])
    hs_ref[...] = (hs * jax.nn.sigmoid(hs) * _dot(h2, wsu_ref[...])).astype(BF16)
    compact(SEL_ROWS - 2 * SEL_TAIL, SEL_TAIL)

    @pl.when(local_ref[w * (N_EXPERTS + 1) + N_EXPERTS] > SEL_ROWS - SEL_TAIL)
    def _():
        compact(SEL_ROWS - SEL_TAIL, SEL_TAIL)

    _start_runs(runs_ref, w, sbuf, slot, xs_hbm, sem, True)

    @pl.when(w > 0)
    def _():
        _wait_runs(local_ref, w - 1, sbuf, 1 - slot, xs_hbm, sem, True)

    @pl.when(w == n_win - 1)
    def _():
        sbuf[2 * SEL_ROWS:, :] = jnp.zeros((2 * EXP_BM, D_MODEL), BF16)

        def fill(e, wait):
            cnt = pl.multiple_of(fill_cnt_ref[e], 8)
            off = pl.multiple_of(fill_off_ref[e], 8)
            cp = pltpu.make_async_copy(sbuf.at[pl.ds(2 * SEL_ROWS, cnt)], xs_hbm.at[pl.ds(off, cnt)], zsem)

            @pl.when(cnt > 0)
            def _():
                if wait:
                    cp.wait()
                else:
                    cp.start()

        def start_body(e, carry):
            fill(e, False)
            return carry

        def wait_body(e, carry):
            fill(e, True)
            return carry
        lax.fori_loop(0, N_EXPERTS, start_body, 0)
        _wait_runs(local_ref, w, sbuf, slot, xs_hbm, sem, True)
        lax.fori_loop(0, N_EXPERTS, wait_body, 0)


def _staging_shape(extra_rows):
    return jax.ShapeDtypeStruct((2 * SEL_ROWS + extra_rows, D_MODEL), BF16)


def _staging_spec(extra_rows):
    return pl.BlockSpec((2 * SEL_ROWS + extra_rows, D_MODEL), lambda w, *_: (0, 0))


def _dispatch(plan, h2, pos, wsg, wsu):
    t = h2.shape[0]
    n_win = t // WIN
    xs, _, hs = pl.pallas_call(
        functools.partial(_dispatch_kernel, n_win=n_win),
        grid_spec=pltpu.PrefetchScalarGridSpec(
            num_scalar_prefetch=4,
            grid=(n_win,),
            in_specs=[
                pl.BlockSpec((WIN, D_MODEL), lambda w, *_: (w, 0)),
                pl.BlockSpec((TOP_K, WIN), lambda w, *_: (0, w)),
                pl.BlockSpec((D_MODEL, D_EXPERT), lambda w, *_: (0, 0)),
                pl.BlockSpec((D_MODEL, D_EXPERT), lambda w, *_: (0, 0)),
            ],
            out_specs=[pl.BlockSpec(memory_space=pl.ANY), _staging_spec(2 * EXP_BM),
                       pl.BlockSpec((WIN, D_EXPERT), lambda w, *_: (w, 0))],
            scratch_shapes=[
                pltpu.VMEM((SEL_ROWS, WIN), BF16),
                pltpu.SemaphoreType.DMA((2,)),
                pltpu.SemaphoreType.DMA,
            ]),
        out_shape=[jax.ShapeDtypeStruct((_sorted_rows_bound(t), D_MODEL), BF16), _staging_shape(2 * EXP_BM),
                   jax.ShapeDtypeStruct((t, D_EXPERT), BF16)],
        compiler_params=pltpu.CompilerParams(
            dimension_semantics=("arbitrary",), vmem_limit_bytes=VMEM_LIMIT),
        name="dispatch",
    )(plan['local_off'], plan['runs'], plan['fill_off'], plan['fill_cnt'], h2, pos, wsg, wsu)
    return xs, hs


def _expert_kernel(blk_expert_ref, next_expert_ref, n_used_ref, xs_hbm, wg_hbm, wu_hbm, wd_hbm, ys_hbm,
                   xbuf, ybuf, wg_st, wu_st, wd_st, wg_bf, wu_bf, wd_bf, xsem, ysem, wsem):
    n_used = n_used_ref[0]
    part = EXP_BM // EXP_SPLIT

    def row_copies(b, slot, fetch):
        out = []
        for q in range(EXP_SPLIT):
            hbm_rows = pl.ds(pl.multiple_of(b * EXP_BM + q * part, part), part)
            if fetch:
                out.append(pltpu.make_async_copy(xs_hbm.at[hbm_rows], xbuf.at[slot, q * part:(q + 1) * part],
                                                 xsem.at[slot]))
            else:
                out.append(pltpu.make_async_copy(ybuf.at[slot, q * part:(q + 1) * part], ys_hbm.at[hbm_rows],
                                                 ysem.at[slot]))
        return out

    def weight_copies(e, slot):
        return [pltpu.make_async_copy(wg_hbm.at[e], wg_st.at[slot], wsem.at[slot]),
                pltpu.make_async_copy(wu_hbm.at[e], wu_st.at[slot], wsem.at[slot]),
                pltpu.make_async_copy(wd_hbm.at[e], wd_st.at[slot], wsem.at[slot])]

    def start(copies):
        for c in copies:
            c.start()

    def wait(copies):
        for c in copies:
            c.wait()

    for ahead in range(EXP_XDEPTH - 2):
        @pl.when(ahead < n_used)
        def _(ahead=ahead):
            start(row_copies(ahead, ahead, True))

    @pl.when(n_used > 0)
    def _():
        start(weight_copies(blk_expert_ref[0], 0))

    def enter_block(b, wset):
        e = blk_expert_ref[b]
        new_expert = jnp.logical_or(b == 0, e != blk_expert_ref[jnp.maximum(b - 1, 0)])
        wset = jnp.where(new_expert, 1 - wset, wset)

        @pl.when(new_expert)
        def _():
            wait(weight_copies(e, wset))
            wg_bf[wset] = wg_st[wset].astype(BF16)
            wu_bf[wset] = wu_st[wset].astype(BF16)
            wd_bf[wset] = wd_st[wset].astype(BF16)
            nxt = next_expert_ref[e]

            @pl.when(nxt < N_EXPERTS)
            def _():
                start(weight_copies(nxt, 1 - wset))

        ahead = b + EXP_XDEPTH - 2

        @pl.when(ahead < n_used)
        def _():
            start(row_copies(ahead, ahead % EXP_XDEPTH, True))

        wait(row_copies(b, b % EXP_XDEPTH, True))

        @pl.when(b >= EXP_YDEPTH)
        def _():
            wait(row_copies(b - EXP_YDEPTH, b % EXP_YDEPTH, False))
        return wset

    def compute(b, wset):
        x = xbuf[b % EXP_XDEPTH]
        hg = _dot(x, wg_bf[wset])
        hb = hg * jax.nn.sigmoid(hg) * _dot(x, wu_bf[wset])
        ybuf[b % EXP_YDEPTH] = _dot(hb.astype(BF16), wd_bf[wset]).astype(BF16)

    def body(p, wset):
        b0 = 2 * p
        w0 = enter_block(b0, wset)
        w1 = enter_block(b0 + 1, w0)
        compute(b0, w0)
        compute(b0 + 1, w1)
        start(row_copies(b0, b0 % EXP_YDEPTH, False))
        start(row_copies(b0 + 1, (b0 + 1) % EXP_YDEPTH, False))
        return w1

    lax.fori_loop(0, n_used // 2, body, jnp.int32(1))

    for back in range(EXP_YDEPTH, 0, -1):
        @pl.when(n_used >= back)
        def _(back=back):
            wait(row_copies(n_used - back, (n_used - back) % EXP_YDEPTH, False))


def _experts(plan, xs, w_gate, w_up, w_down):
    any_spec = pl.BlockSpec(memory_space=pl.ANY)
    return pl.pallas_call(
        _expert_kernel,
        grid_spec=pltpu.PrefetchScalarGridSpec(
            num_scalar_prefetch=3,
            grid=(1,),
            in_specs=[any_spec, any_spec, any_spec, any_spec],
            out_specs=any_spec,
            scratch_shapes=[
                pltpu.VMEM((EXP_XDEPTH, EXP_BM, D_MODEL), BF16),
                pltpu.VMEM((EXP_YDEPTH, EXP_BM, D_MODEL), BF16),
                pltpu.VMEM((2, D_MODEL, D_EXPERT), F32),
                pltpu.VMEM((2, D_MODEL, D_EXPERT), F32),
                pltpu.VMEM((2, D_EXPERT, D_MODEL), F32),
                pltpu.VMEM((2, D_MODEL, D_EXPERT), BF16),
                pltpu.VMEM((2, D_MODEL, D_EXPERT), BF16),
                pltpu.VMEM((2, D_EXPERT, D_MODEL), BF16),
                pltpu.SemaphoreType.DMA((EXP_XDEPTH,)),
                pltpu.SemaphoreType.DMA((EXP_YDEPTH,)),
                pltpu.SemaphoreType.DMA((2,)),
            ]),
        out_shape=jax.ShapeDtypeStruct(xs.shape, BF16),
        compiler_params=pltpu.CompilerParams(
            dimension_semantics=("arbitrary",), vmem_limit_bytes=VMEM_LIMIT),
        name="experts",
    )(plan['blk_expert'], plan['next_expert'], plan['n_used'], xs, w_gate, w_up, w_down)


def _combine_kernel(local_ref, runs_ref, x1_ref, hs_ref, p_ref, rank_ref, gate_ref, lo_ref, hi_ref,
                    wsd_ref, gple_ref, wpg_ref, wp_ref, gfin_ref, ys_hbm, o_ref, ybuf, st_ref, sem,
                    *, n_win, final_norm):
    w = pl.program_id(0)
    slot = w % 2

    @pl.when(w == 0)
    def _():
        ybuf[...] = jnp.zeros(ybuf.shape, BF16)
        _start_runs(runs_ref, w, ybuf, slot, ys_hbm, sem, False)

    @pl.when(w + 1 < n_win)
    def _():
        _start_runs(runs_ref, w + 1, ybuf, 1 - slot, ys_hbm, sem, False)

    lo = lo_ref[0]
    hi = hi_ref[0]
    lo_f = lo.astype(F32)
    rank_tbl = rank_ref[...]
    gate_tbl = gate_ref[...]

    def build_group(lg):
        cols = slice(lg * CMB_LG, (lg + 1) * CMB_LG)
        rid = lg * CMB_LG + lax.broadcasted_iota(jnp.int32, (N_EXPERTS, CMB_LG), 1)
        owner = jnp.where(rid >= lo, jnp.where(rid < hi, 1.0, 0.0), 0.0)
        run_row = rid[0:1, :].astype(F32) - jnp.sum(owner * lo_f, axis=0, keepdims=True)
        owner = owner.astype(BF16)
        hit = _dot_t(rank_tbl, owner) == run_row
        st_ref[:, cols] = jnp.where(hit, _dot_t(gate_tbl, owner), 0.0).astype(BF16)

    build_group(0)
    shared = _dot(hs_ref[...], wsd_ref[...])

    _wait_runs(local_ref, w, ybuf, slot, ys_hbm, sem, False)
    routed = None
    n_groups = SEL_ROWS // CMB_LG
    for lg in range(n_groups):
        if lg + 1 < n_groups:
            build_group(lg + 1)
        src = pl.multiple_of(slot * SEL_ROWS + lg * CMB_LG, CMB_LG)
        part = _dot(st_ref[:, lg * CMB_LG:(lg + 1) * CMB_LG], ybuf[pl.ds(src, CMB_LG), :])
        routed = part if routed is None else routed + part
    x2 = x1_ref[...] + routed + shared

    hp = _rms(x2, gple_ref[...]).astype(BF16)
    gate = jax.nn.sigmoid(_dot(hp, wpg_ref[...]))
    x3 = x2 + gate * _dot(p_ref[...].astype(BF16), wp_ref[...])
    o_ref[...] = _rms(x3, gfin_ref[...]) if final_norm else x3


def _combine(plan, ys, x1, hs, p, rank_tbl, gate_tbl, wsd, g_ple, w_pg, w_p, g_fin, final_norm):
    t = x1.shape[0]
    n_win = t // WIN
    row = lambda width: pl.BlockSpec((WIN, width), lambda w, *_: (w, 0))
    const = lambda shape: pl.BlockSpec(shape, lambda w, *_: (0,) * len(shape))
    return pl.pallas_call(
        functools.partial(_combine_kernel, n_win=n_win, final_norm=final_norm),
        grid_spec=pltpu.PrefetchScalarGridSpec(
            num_scalar_prefetch=2,
            grid=(n_win,),
            in_specs=[
                row(D_MODEL), row(D_EXPERT), row(PLE_DIM),
                pl.BlockSpec((N_EXPERTS, WIN), lambda w, *_: (0, w)),
                pl.BlockSpec((N_EXPERTS, WIN), lambda w, *_: (0, w)),
                pl.BlockSpec((1, N_EXPERTS, 1), lambda w, *_: (w, 0, 0)),
                pl.BlockSpec((1, N_EXPERTS, 1), lambda w, *_: (w, 0, 0)),
                const((D_EXPERT, D_MODEL)),
                const((1, D_MODEL)), const((D_MODEL, D_MODEL)), const((PLE_DIM, D_MODEL)),
                const((1, D_MODEL)),
                pl.BlockSpec(memory_space=pl.ANY),
            ],
            out_specs=[row(D_MODEL), _staging_spec(0)],
            scratch_shapes=[
                pltpu.VMEM((WIN, SEL_ROWS), BF16),
                pltpu.SemaphoreType.DMA((2,)),
            ]),
        out_shape=[jax.ShapeDtypeStruct((t, D_MODEL), F32), _staging_shape(0)],
        compiler_params=pltpu.CompilerParams(
            dimension_semantics=("arbitrary",), vmem_limit_bytes=VMEM_LIMIT),
        name="combine",
    )(plan['local_off'], plan['runs'], x1, hs, p, rank_tbl, gate_tbl, plan['run_lo'], plan['run_hi'],
      wsd, g_ple, w_pg, w_p, g_fin, ys)[0]


def kernel(x, p, g_mix, w_in, b_in, w_dw, b_dw, g_cln, b_cln, w_conv_out, b_conv_out, w_pool, s_pool,
           w_out, g_ffn, w_router, b_router, w_e_gate, w_e_up, w_e_down, w_s_gate, w_s_up, w_s_down,
           g_ple, w_ple_gate, w_ple, g_final):
    bsz, s, d = x.shape
    t = bsz * s
    depth = w_in.shape[0]
    xt = x.reshape(t, d)
    row = lambda v: v.reshape(1, -1)
    for i in range(depth):
        x1, h2 = _mixer(
            xt, s, row(g_mix[i]), w_in[i].astype(BF16), row(b_in[i]), w_dw[i], row(b_dw[i]),
            row(g_cln[i]), row(b_cln[i]), w_conv_out[i].astype(BF16), row(b_conv_out[i]),
            w_pool[i].astype(BF16), row(s_pool[i]), w_out[i].astype(BF16), row(g_ffn[i]))
        gate, rank, pos, cnt = _router(h2, w_router[i].T.astype(BF16), b_router[i].reshape(N_EXPERTS, 1))
        plan = _dispatch_plan(cnt, t)
        xs, hs = _dispatch(plan, h2, pos, w_s_gate[i].astype(BF16), w_s_up[i].astype(BF16))
        ys = _experts(plan, xs, w_e_gate[i], w_e_up[i], w_e_down[i])
        xt = _combine(
            plan, ys, x1, hs, p[i].reshape(t, PLE_DIM), rank, gate, w_s_down[i].astype(BF16),
            row(g_ple[i]), w_ple_gate[i].astype(BF16), w_ple[i].astype(BF16), row(g_final),
            final_norm=(i == depth - 1))
    return xt.reshape(bsz, s, d)
```

```python
import functools

import jax
import jax.numpy as jnp
from jax import lax
from jax.experimental import pallas as pl
from jax.experimental.pallas import tpu as pltpu

D_MODEL = 1024
D_CONV = 1024
D_POOL = 1024
CONV_WIDTH = 31
POOL_WINDOWS = (2, 4, 8, 16)
POOL_GROUP = 256
PLE_DIM = 256
N_EXPERTS = 64
N_GROUPS = 8
GROUP_SIZE = N_EXPERTS // N_GROUPS
TOPK_GROUPS = 4
TOP_K = 8
D_EXPERT = 256
ROUTED_SCALE = 2.5
NORM_EPS = 1e-6

F32 = jnp.float32
BF16 = jnp.bfloat16

MIX_TM = 512
MIX_NV = MIX_TM // 8
CONV_MG = 8
ROW_CHUNK = 64
LANE = 128

ROUTER_TM = 1024
WIN = 256
SEL_ROWS = 2560
SEL_RG = 64
SEL_MM = 512
SEL_TAIL = 256
EXP_BM = 576
EXP_XDEPTH = 6
EXP_YDEPTH = 4
EXP_SPLIT = 4
RUN_FIELDS = 4
RUN_UNROLL = 4
CMB_LG = 512
CMB_BG = 128

V7X_VMEM_BYTES = 64 * 1024 * 1024
VMEM_LIMIT = V7X_VMEM_BYTES - 8 * 1024 * 1024

assert SEL_ROWS >= TOP_K * WIN + 7 * N_EXPERTS
assert (SEL_ROWS - 2 * SEL_TAIL) % SEL_MM == 0 and SEL_TAIL % SEL_RG == 0 and SEL_ROWS % CMB_LG == 0
assert N_EXPERTS % RUN_UNROLL == 0 and EXP_BM % (16 * EXP_SPLIT) == 0 and EXP_XDEPTH > 2 and EXP_YDEPTH >= 2


def _rms(x, g):
    ms = jnp.mean(x * x, axis=-1, keepdims=True)
    return x * lax.rsqrt(ms + NORM_EPS) * g


def _dot(a, b):
    return jnp.dot(a, b, preferred_element_type=F32)


def _mixer_kernel(x_ref, gmix_ref, win_ref, bin_ref, wdw_ref, bdw_ref, gcln_ref, bcln_ref,
                  wco_ref, bco_ref, wpool_ref, spool_ref, wout_ref, gffn_ref,
                  x1_ref, h2_ref, a_ext, a_prev, u_ext, u_prev, c_buf, q_buf, p_buf, *, tiles_per_seq):
    i = pl.program_id(0) % tiles_per_seq
    tm = MIX_TM
    nv = MIX_NV

    @pl.when(i == 0)
    def _():
        a_prev[...] = jnp.zeros(a_prev.shape, F32)
        u_prev[...] = jnp.zeros(u_prev.shape, F32)

    n_col = D_MODEL // LANE
    hn = _rms(x_ref[...], gmix_ref[...])
    for lc in range(n_col):
        for s in range(8):
            p_buf[lc, pl.ds(s, nv, stride=8), :] = hn[s * nv:(s + 1) * nv, lc * LANE:(lc + 1) * LANE]
    h = jnp.concatenate([p_buf[lc] for lc in range(n_col)], axis=-1).astype(BF16)

    def proj(lo, hi):
        return _dot(h, win_ref[:, lo:hi]) + bin_ref[:, lo:hi]

    glu = proj(0, D_CONV) * jax.nn.sigmoid(proj(D_CONV, 2 * D_CONV))
    for lc in range(D_CONV // LANE):
        a_ext[lc, tm:2 * tm, :] = glu[:, lc * LANE:(lc + 1) * LANE]
    u_ext[tm:2 * tm, :] = proj(2 * D_CONV, 2 * D_CONV + D_POOL)

    def delayed_groups(ext, prev, first_group):
        last_row = lax.broadcasted_iota(jnp.int32, (8, ext.shape[-1]), 0) == 7
        for g in range(first_group, nv):
            rows = slice(8 * g, 8 * g + 8)
            mixed = jnp.where(last_row, prev[rows, :], ext[tm + 8 * g:tm + 8 * g + 8, :])
            ext[rows, :] = pltpu.roll(mixed, 1, axis=0)
            prev[rows, :] = ext[tm + 8 * g:tm + 8 * g + 8, :]

    delayed_groups(u_ext, u_prev, nv - (max(POOL_WINDOWS) - 1))

    for r0 in range(0, tm, ROW_CHUNK):
        row = r0 + lax.broadcasted_iota(jnp.int32, (ROW_CHUNK, POOL_GROUP), 0)
        t1 = i * tm + (row % 8) * nv + row // 8 + 1
        for gi, w in enumerate(POOL_WINDOWS):
            ls = slice(gi * POOL_GROUP, (gi + 1) * POOL_GROUP)
            tok = u_ext[tm + r0:tm + r0 + ROW_CHUNK, ls]
            s = tok
            for j in range(1, w):
                s = s + u_ext[tm + r0 - 8 * j:tm + r0 - 8 * j + ROW_CHUNK, ls]
            cnt = jnp.minimum(t1, w).astype(F32)
            q_buf[r0:r0 + ROW_CHUNK, ls] = s / cnt - tok

    qs_out = []
    for gi in range(len(POOL_WINDOWS)):
        ls = slice(gi * POOL_GROUP, (gi + 1) * POOL_GROUP)
        qs_out.append(_dot(q_buf[:, ls].astype(BF16), wpool_ref[gi]) * spool_ref[:, ls])
    branch_b = jnp.concatenate(qs_out, axis=-1)

    def conv_column(lc, carry):
        a_col = a_ext.at[lc]
        delayed_groups(a_col, a_prev.at[lc], nv - (CONV_WIDTH - 1))
        w_col = wdw_ref.at[lc]
        for g0 in range(0, nv, CONV_MG):
            acc = None
            for k in range(CONV_WIDTH):
                src = nv + g0 + k - (CONV_WIDTH - 1)
                term = a_col[8 * src:8 * (src + CONV_MG), :] * w_col[k:k + 1, :]
                acc = term if acc is None else acc + term
            c_buf[lc, 8 * g0:8 * (g0 + CONV_MG), :] = acc + bdw_ref[lc]
        return carry
    lax.fori_loop(0, D_CONV // LANE, conv_column, 0)

    c2 = 2 * D_CONV + D_POOL
    gate_a = jax.nn.sigmoid(proj(c2, c2 + D_MODEL))
    gate_b = jax.nn.sigmoid(proj(c2 + D_MODEL, c2 + 2 * D_MODEL))

    c = jnp.concatenate([c_buf[lc] for lc in range(D_CONV // LANE)], axis=-1)
    mu = jnp.mean(c, axis=-1, keepdims=True)
    xc = c - mu
    var = jnp.mean(xc * xc, axis=-1, keepdims=True)
    y = xc * lax.rsqrt(var + NORM_EPS) * gcln_ref[...] + bcln_ref[...]
    y = y * jax.nn.sigmoid(y)
    branch_a = _dot(y.astype(BF16), wco_ref[...]) + bco_ref[...]

    merged = gate_a * branch_a + gate_b * branch_b
    y = _dot(merged.astype(BF16), wout_ref[...])
    for lc in range(n_col):
        p_buf[lc] = y[:, lc * LANE:(lc + 1) * LANE]
    for lc in range(n_col):
        cols = slice(lc * LANE, (lc + 1) * LANE)
        for s in range(8):
            for j in range(nv // 8):
                rows = slice(s * nv + 8 * j, s * nv + 8 * j + 8)
                x1_ref[rows, cols] = x_ref[rows, cols] + p_buf[lc, pl.ds(8 * (8 * j) + s, 8, stride=8), :]
    h2_ref[...] = _rms(x1_ref[...], gffn_ref[...]).astype(BF16)


def _const_spec(shape):
    n = len(shape)
    return pl.BlockSpec(shape, lambda i, _n=n: (0,) * _n, pipeline_mode=pl.Buffered(1))


def _mixer(x, seq_len, g_mix, w_in, b_in, w_dw, b_dw, g_cln, b_cln, w_co, b_co, w_pool, s_pool, w_out,
           g_ffn):
    t = x.shape[0]
    tm = MIX_TM
    assert seq_len % tm == 0 and MIX_NV >= CONV_WIDTH and MIX_NV >= max(POOL_WINDOWS)
    d_in = w_in.shape[1]
    row = pl.BlockSpec((tm, D_MODEL), lambda i: (i, 0))
    n_col = D_CONV // LANE
    w_dw = w_dw.reshape(CONV_WIDTH, n_col, LANE).transpose(1, 0, 2)
    b_dw = b_dw.reshape(n_col, 1, LANE)
    return pl.pallas_call(
        functools.partial(_mixer_kernel, tiles_per_seq=seq_len // tm),
        grid=(t // tm,),
        in_specs=[
            row,
            _const_spec((1, D_MODEL)),
            _const_spec((D_MODEL, d_in)),
            _const_spec((1, d_in)),
            _const_spec((n_col, CONV_WIDTH, LANE)),
            _const_spec((n_col, 1, LANE)),
            _const_spec((1, D_CONV)),
            _const_spec((1, D_CONV)),
            _const_spec((D_CONV, D_MODEL)),
            _const_spec((1, D_MODEL)),
            _const_spec((len(POOL_WINDOWS), POOL_GROUP, POOL_GROUP)),
            _const_spec((1, D_POOL)),
            _const_spec((D_MODEL, D_MODEL)),
            _const_spec((1, D_MODEL)),
        ],
        out_specs=[row, row],
        out_shape=[jax.ShapeDtypeStruct((t, D_MODEL), F32),
                   jax.ShapeDtypeStruct((t, D_MODEL), BF16)],
        scratch_shapes=[
            pltpu.VMEM((n_col, 2 * tm, LANE), F32),
            pltpu.VMEM((n_col, tm, LANE), F32),
            pltpu.VMEM((2 * tm, D_POOL), F32),
            pltpu.VMEM((tm, D_POOL), F32),
            pltpu.VMEM((n_col, tm, LANE), F32),
            pltpu.VMEM((tm, D_POOL), F32),
            pltpu.VMEM((D_MODEL // LANE, tm, LANE), F32),
        ],
        compiler_params=pltpu.CompilerParams(
            dimension_semantics=("arbitrary",), vmem_limit_bytes=VMEM_LIMIT),
        name="mixer",
    )(x, g_mix, w_in, b_in, w_dw, b_dw, g_cln, b_cln, w_co, b_co, w_pool, s_pool, w_out, g_ffn)


def _beats(v, other, other_is_later):
    v = jnp.broadcast_to(v, other.shape)
    return jnp.where(other_is_later, jnp.where(v >= other, 1, 0), jnp.where(v > other, 1, 0))


def _router_kernel(h2_ref, wrt_ref, br_ref, utri_ref, ltri_ref, gsel_ref, pos_ref, cnt_ref):
    tm = ROUTER_TM
    logits = lax.dot_general(wrt_ref[...], h2_ref[...], (((1,), (1,)), ((), ())),
                             preferred_element_type=F32)
    scores = jax.nn.sigmoid(logits)
    sel = scores + br_ref[...]
    shape3 = (N_GROUPS, GROUP_SIZE, tm)
    sel3 = sel.reshape(shape3)
    scores3 = scores.reshape(shape3)
    neg_inf = jnp.float32(-jnp.inf)

    member = lax.broadcasted_iota(jnp.int32, shape3, 1)
    m1 = jnp.max(sel3, axis=1, keepdims=True)
    first = jnp.min(jnp.where(sel3 == m1, member, GROUP_SIZE), axis=1, keepdims=True)
    m2 = jnp.max(jnp.where(member == first, neg_inf, sel3), axis=1, keepdims=True)
    gscore = jnp.broadcast_to(m1 + m2, shape3)

    gidx = lax.broadcasted_iota(jnp.int32, shape3, 0)
    grank = jnp.zeros(shape3, jnp.int32)
    for j in range(N_GROUPS):
        sj = gscore[j:j + 1]
        grank = grank + _beats(sj, gscore, gidx > j)
    masked = jnp.where(grank < TOPK_GROUPS, sel3, neg_inf)

    eidx = gidx * GROUP_SIZE + member
    work = masked
    erank = jnp.full(shape3, TOP_K, jnp.int32)
    for k in range(TOP_K):
        best = jnp.max(jnp.max(work, axis=0, keepdims=True), axis=1, keepdims=True)
        cand = jnp.where(work == best, eidx, N_EXPERTS)
        pick = jnp.min(jnp.min(cand, axis=0, keepdims=True), axis=1, keepdims=True)
        hit = eidx == pick
        work = jnp.where(hit, neg_inf, work)
        erank = jnp.where(hit, k, erank)
    chosen = erank < TOP_K
    top_s = jnp.where(chosen, scores3, 0.0)
    denom = jnp.sum(jnp.sum(top_s, axis=0, keepdims=True), axis=1, keepdims=True)
    gates3 = top_s / denom * ROUTED_SCALE
    chosen2 = jnp.where(chosen, 1.0, 0.0).reshape(N_EXPERTS, tm)
    gates3 = gates3.astype(BF16).astype(F32)
    for k in range(TOP_K):
        gk = jnp.sum(jnp.sum(jnp.where(erank == k, gates3, 0.0), axis=0, keepdims=True), axis=1, keepdims=True)
        gsel_ref[k:k + 1, :] = gk.reshape(1, tm)

    for w in range(tm // WIN):
        ls = slice(w * WIN, (w + 1) * WIN)
        mw = chosen2[:, ls]
        rank = _dot(mw.astype(BF16), utri_ref[...])
        n = jnp.sum(mw, axis=1, keepdims=True)
        run = jnp.floor((n + 7.0) * 0.125) * 8.0
        start = _dot(ltri_ref[...], jnp.broadcast_to(run, (N_EXPERTS, WIN)).astype(BF16))
        row3 = (rank + start).reshape(N_GROUPS, GROUP_SIZE, WIN)
        er = erank[:, :, ls]
        for k in range(TOP_K):
            pk = jnp.sum(jnp.sum(jnp.where(er == k, row3, 0.0), axis=0, keepdims=True), axis=1, keepdims=True)
            pos_ref[k:k + 1, ls] = pk.reshape(1, WIN).astype(jnp.int32)
        cnt_ref[w] = n


def _router(h2, w_rt, b_r):
    t = h2.shape[0]
    tm = ROUTER_TM
    utri = jnp.triu(jnp.ones((WIN, WIN), BF16), k=1)
    ltri = jnp.tril(jnp.ones((N_EXPERTS, N_EXPERTS), BF16), k=-1)
    return pl.pallas_call(
        _router_kernel,
        grid=(t // tm,),
        in_specs=[
            pl.BlockSpec((tm, D_MODEL), lambda i: (i, 0)),
            _const_spec((N_EXPERTS, D_MODEL)),
            _const_spec((N_EXPERTS, 1)),
            _const_spec((WIN, WIN)),
            _const_spec((N_EXPERTS, N_EXPERTS)),
        ],
        out_specs=[
            pl.BlockSpec((TOP_K, tm), lambda i: (0, i)),
            pl.BlockSpec((TOP_K, tm), lambda i: (0, i)),
            pl.BlockSpec((tm // WIN, N_EXPERTS, 1), lambda i: (i, 0, 0)),
        ],
        out_shape=[
            jax.ShapeDtypeStruct((TOP_K, t), F32),
            jax.ShapeDtypeStruct((TOP_K, t), jnp.int32),
            jax.ShapeDtypeStruct((t // WIN, N_EXPERTS, 1), F32),
        ],
        compiler_params=pltpu.CompilerParams(
            dimension_semantics=("arbitrary",), vmem_limit_bytes=VMEM_LIMIT),
        name="router",
    )(h2, w_rt, b_r, utri, ltri)


def _sorted_rows_bound(t):
    rows = t * TOP_K + (t // WIN) * N_EXPERTS * 7 + N_EXPERTS * (EXP_BM - 1)
    blocks = -(-rows // EXP_BM)
    return (blocks + blocks % 2) * EXP_BM


def _dispatch_plan(cnt, t):
    nw = t // WIN
    n = cnt.reshape(nw, N_EXPERTS).astype(jnp.int32)
    run = (n + 7) // 8 * 8
    local_end = jnp.cumsum(run, axis=1)
    local_off = jnp.concatenate([jnp.zeros((nw, 1), jnp.int32), local_end], axis=1)
    total = jnp.sum(run, axis=0)
    region = (total + EXP_BM - 1) // EXP_BM * EXP_BM
    eid = jnp.arange(N_EXPERTS, dtype=jnp.int32)
    last_owner = jnp.max(jnp.where(region > 0, eid, 0))
    odd = (jnp.sum(region) // EXP_BM) % 2
    region = region + jnp.where(eid == last_owner, odd * EXP_BM, 0)
    region_end = jnp.cumsum(region)
    base = region_end - region
    global_off = base[None, :] + jnp.cumsum(run, axis=0) - run
    n_blocks = _sorted_rows_bound(t) // EXP_BM
    n_used = region_end[-1] // EXP_BM
    blk = jnp.arange(n_blocks, dtype=jnp.int32)
    blk_expert = jnp.sum((region_end[None, :] <= blk[:, None] * EXP_BM).astype(jnp.int32), axis=1)
    blk_expert = jnp.minimum(blk_expert, N_EXPERTS - 1)
    later_nonempty = (eid[None, :] > eid[:, None]) & (region[None, :] > 0)
    next_expert = jnp.min(jnp.where(later_nonempty, eid[None, :], N_EXPERTS), axis=1).astype(jnp.int32)
    return dict(
        local_off=local_off.reshape(-1),
        runs=jnp.stack([local_off[:, :N_EXPERTS], run, global_off, jnp.zeros_like(run)], axis=-1).reshape(-1),
        fill_off=base + total, fill_cnt=region - total,
        blk_expert=blk_expert.astype(jnp.int32), next_expert=next_expert,
        n_used=n_used.reshape(1).astype(jnp.int32))


def _run_copy(runs_ref, win, e, vmem_buf, slot, hbm_buf, sem, to_hbm):
    p = (win * N_EXPERTS + e) * RUN_FIELDS
    lo = pl.multiple_of(runs_ref[p], 8)
    cnt = pl.multiple_of(runs_ref[p + 1], 8)
    go = pl.multiple_of(runs_ref[p + 2], 8)
    v = vmem_buf.at[pl.ds(pl.multiple_of(slot * SEL_ROWS + lo, 8), cnt)]
    h = hbm_buf.at[pl.ds(go, cnt)]
    cp = pltpu.make_async_copy(v, h, sem.at[slot]) if to_hbm else pltpu.make_async_copy(h, v, sem.at[slot])
    return cnt, cp


def _start_runs(runs_ref, win, vmem_buf, slot, hbm_buf, sem, to_hbm):
    def body(i, carry):
        copies = [_run_copy(runs_ref, win, i * RUN_UNROLL + j, vmem_buf, slot, hbm_buf, sem, to_hbm)
                  for j in range(RUN_UNROLL)]
        for cnt, cp in copies:
            @pl.when(cnt > 0)
            def _(cp=cp):
                cp.start()
        return carry
    lax.fori_loop(0, N_EXPERTS // RUN_UNROLL, body, 0)


def _wait_runs(local_ref, win, vmem_buf, slot, hbm_buf, sem, to_hbm):
    total = pl.multiple_of(local_ref[win * (N_EXPERTS + 1) + N_EXPERTS], 8)
    v = vmem_buf.at[pl.ds(pl.multiple_of(slot * SEL_ROWS, 8), total)]
    h = hbm_buf.at[pl.ds(0, total)]
    cp = pltpu.make_async_copy(v, h, sem.at[slot]) if to_hbm else pltpu.make_async_copy(h, v, sem.at[slot])

    @pl.when(total > 0)
    def _():
        cp.wait()


def _dispatch_kernel(local_ref, runs_ref, fill_off_ref, fill_cnt_ref, h2_ref, pos_ref, wsg_ref, wsu_ref,
                     xs_hbm, sbuf, hs_ref, s_ref, sem, zsem, *, n_win):
    w = pl.program_id(0)
    slot = w % 2
    pos = pos_ref[...]

    h2 = h2_ref[...]
    assert SEL_RG <= 256
    rid_b = lax.broadcasted_iota(jnp.int32, (SEL_RG, WIN), 0).astype(F32).astype(BF16)
    one_b = jnp.ones((SEL_RG, WIN), BF16)
    def compact(first, n_rows):
        for r0 in range(first, first + n_rows, SEL_RG):
            acc = jnp.zeros((SEL_RG, WIN), BF16)
            for k in range(TOP_K):
                off = (pos[k:k + 1, :] - r0).astype(F32)
                off = jnp.broadcast_to(off, (SEL_RG, WIN)).astype(BF16)
                acc = jnp.where(rid_b == off, one_b, acc)
            s_ref[r0:r0 + SEL_RG, :] = acc
        dst = pl.multiple_of(slot * SEL_ROWS + first, SEL_TAIL)
        sbuf[pl.ds(dst, n_rows), :] = _dot(s_ref[first:first + n_rows, :], h2).astype(BF16)

    @pl.when(w == 0)
    def _():
        for s in range(2):
            sbuf[(s + 1) * SEL_ROWS - SEL_TAIL:(s + 1) * SEL_ROWS, :] = jnp.zeros((SEL_TAIL, D_MODEL), BF16)

    for first in range(0, SEL_ROWS - 2 * SEL_TAIL, SEL_MM):
        compact(first, SEL_MM)
    hs = _dot(h2, wsg_ref[...])
    hs_ref[...] = (hs * jax.nn.sigmoid(hs) * _dot(h2, wsu_ref[...])).astype(BF16)
    compact(SEL_ROWS - 2 * SEL_TAIL, SEL_TAIL)

    @pl.when(local_ref[w * (N_EXPERTS + 1) + N_EXPERTS] > SEL_ROWS - SEL_TAIL)
    def _():
        compact(SEL_ROWS - SEL_TAIL, SEL_TAIL)

    _start_runs(runs_ref, w, sbuf, slot, xs_hbm, sem, True)

    @pl.when(w > 0)
    def _():
        _wait_runs(local_ref, w - 1, sbuf, 1 - slot, xs_hbm, sem, True)

    @pl.when(w == n_win - 1)
    def _():
        sbuf[2 * SEL_ROWS:, :] = jnp.zeros((2 * EXP_BM, D_MODEL), BF16)

        def fill(e, wait):
            cnt = pl.multiple_of(fill_cnt_ref[e], 8)
            off = pl.multiple_of(fill_off_ref[e], 8)
            cp = pltpu.make_async_copy(sbuf.at[pl.ds(2 * SEL_ROWS, cnt)], xs_hbm.at[pl.ds(off, cnt)], zsem)

            @pl.when(cnt > 0)
            def _():
                if wait:
                    cp.wait()
                else:
                    cp.start()

        def start_body(e, carry):
            fill(e, False)
            return carry

        def wait_body(e, carry):
            fill(e, True)
            return carry
        lax.fori_loop(0, N_EXPERTS, start_body, 0)
        _wait_runs(local_ref, w, sbuf, slot, xs_hbm, sem, True)
        lax.fori_loop(0, N_EXPERTS, wait_body, 0)


def _staging_shape(extra_rows):
    return jax.ShapeDtypeStruct((2 * SEL_ROWS + extra_rows, D_MODEL), BF16)


def _staging_spec(extra_rows):
    return pl.BlockSpec((2 * SEL_ROWS + extra_rows, D_MODEL), lambda w, *_: (0, 0))


def _dispatch(plan, h2, pos, wsg, wsu):
    t = h2.shape[0]
    n_win = t // WIN
    xs, _, hs = pl.pallas_call(
        functools.partial(_dispatch_kernel, n_win=n_win),
        grid_spec=pltpu.PrefetchScalarGridSpec(
            num_scalar_prefetch=4,
            grid=(n_win,),
            in_specs=[
                pl.BlockSpec((WIN, D_MODEL), lambda w, *_: (w, 0)),
                pl.BlockSpec((TOP_K, WIN), lambda w, *_: (0, w)),
                pl.BlockSpec((D_MODEL, D_EXPERT), lambda w, *_: (0, 0)),
                pl.BlockSpec((D_MODEL, D_EXPERT), lambda w, *_: (0, 0)),
            ],
            out_specs=[pl.BlockSpec(memory_space=pl.ANY), _staging_spec(2 * EXP_BM),
                       pl.BlockSpec((WIN, D_EXPERT), lambda w, *_: (w, 0))],
            scratch_shapes=[
                pltpu.VMEM((SEL_ROWS, WIN), BF16),
                pltpu.SemaphoreType.DMA((2,)),
                pltpu.SemaphoreType.DMA,
            ]),
        out_shape=[jax.ShapeDtypeStruct((_sorted_rows_bound(t), D_MODEL), BF16), _staging_shape(2 * EXP_BM),
                   jax.ShapeDtypeStruct((t, D_EXPERT), BF16)],
        compiler_params=pltpu.CompilerParams(
            dimension_semantics=("arbitrary",), vmem_limit_bytes=VMEM_LIMIT),
        name="dispatch",
    )(plan['local_off'], plan['runs'], plan['fill_off'], plan['fill_cnt'], h2, pos, wsg, wsu)
    return xs, hs


def _expert_kernel(blk_expert_ref, next_expert_ref, n_used_ref, xs_hbm, wg_hbm, wu_hbm, wd_hbm, ys_hbm,
                   xbuf, ybuf, wg_st, wu_st, wd_st, wg_bf, wu_bf, wd_bf, xsem, ysem, wsem):
    n_used = n_used_ref[0]
    part = EXP_BM // EXP_SPLIT

    def row_copies(b, slot, fetch):
        out = []
        for q in range(EXP_SPLIT):
            hbm_rows = pl.ds(pl.multiple_of(b * EXP_BM + q * part, part), part)
            if fetch:
                out.append(pltpu.make_async_copy(xs_hbm.at[hbm_rows], xbuf.at[slot, q * part:(q + 1) * part],
                                                 xsem.at[slot]))
            else:
                out.append(pltpu.make_async_copy(ybuf.at[slot, q * part:(q + 1) * part], ys_hbm.at[hbm_rows],
                                                 ysem.at[slot]))
        return out

    def weight_copies(e, slot):
        return [pltpu.make_async_copy(wg_hbm.at[e], wg_st.at[slot], wsem.at[slot]),
                pltpu.make_async_copy(wu_hbm.at[e], wu_st.at[slot], wsem.at[slot]),
                pltpu.make_async_copy(wd_hbm.at[e], wd_st.at[slot], wsem.at[slot])]

    def start(copies):
        for c in copies:
            c.start()

    def wait(copies):
        for c in copies:
            c.wait()

    for ahead in range(EXP_XDEPTH - 2):
        @pl.when(ahead < n_used)
        def _(ahead=ahead):
            start(row_copies(ahead, ahead, True))

    @pl.when(n_used > 0)
    def _():
        start(weight_copies(blk_expert_ref[0], 0))

    def enter_block(b, wset):
        e = blk_expert_ref[b]
        new_expert = jnp.logical_or(b == 0, e != blk_expert_ref[jnp.maximum(b - 1, 0)])
        wset = jnp.where(new_expert, 1 - wset, wset)

        @pl.when(new_expert)
        def _():
            wait(weight_copies(e, wset))
            wg_bf[wset] = wg_st[wset].astype(BF16)
            wu_bf[wset] = wu_st[wset].astype(BF16)
            wd_bf[wset] = wd_st[wset].astype(BF16)
            nxt = next_expert_ref[e]

            @pl.when(nxt < N_EXPERTS)
            def _():
                start(weight_copies(nxt, 1 - wset))

        ahead = b + EXP_XDEPTH - 2

        @pl.when(ahead < n_used)
        def _():
            start(row_copies(ahead, ahead % EXP_XDEPTH, True))

        wait(row_copies(b, b % EXP_XDEPTH, True))

        @pl.when(b >= EXP_YDEPTH)
        def _():
            wait(row_copies(b - EXP_YDEPTH, b % EXP_YDEPTH, False))
        return wset

    def compute(b, wset):
        x = xbuf[b % EXP_XDEPTH]
        hg = _dot(x, wg_bf[wset])
        hb = hg * jax.nn.sigmoid(hg) * _dot(x, wu_bf[wset])
        ybuf[b % EXP_YDEPTH] = _dot(hb.astype(BF16), wd_bf[wset]).astype(BF16)

    def body(p, wset):
        b0 = 2 * p
        w0 = enter_block(b0, wset)
        w1 = enter_block(b0 + 1, w0)
        compute(b0, w0)
        compute(b0 + 1, w1)
        start(row_copies(b0, b0 % EXP_YDEPTH, False))
        start(row_copies(b0 + 1, (b0 + 1) % EXP_YDEPTH, False))
        return w1

    lax.fori_loop(0, n_used // 2, body, jnp.int32(1))

    for back in range(EXP_YDEPTH, 0, -1):
        @pl.when(n_used >= back)
        def _(back=back):
            wait(row_copies(n_used - back, (n_used - back) % EXP_YDEPTH, False))


def _experts(plan, xs, w_gate, w_up, w_down):
    any_spec = pl.BlockSpec(memory_space=pl.ANY)
    return pl.pallas_call(
        _expert_kernel,
        grid_spec=pltpu.PrefetchScalarGridSpec(
            num_scalar_prefetch=3,
            grid=(1,),
            in_specs=[any_spec, any_spec, any_spec, any_spec],
            out_specs=any_spec,
            scratch_shapes=[
                pltpu.VMEM((EXP_XDEPTH, EXP_BM, D_MODEL), BF16),
                pltpu.VMEM((EXP_YDEPTH, EXP_BM, D_MODEL), BF16),
                pltpu.VMEM((2, D_MODEL, D_EXPERT), F32),
                pltpu.VMEM((2, D_MODEL, D_EXPERT), F32),
                pltpu.VMEM((2, D_EXPERT, D_MODEL), F32),
                pltpu.VMEM((2, D_MODEL, D_EXPERT), BF16),
                pltpu.VMEM((2, D_MODEL, D_EXPERT), BF16),
                pltpu.VMEM((2, D_EXPERT, D_MODEL), BF16),
                pltpu.SemaphoreType.DMA((EXP_XDEPTH,)),
                pltpu.SemaphoreType.DMA((EXP_YDEPTH,)),
                pltpu.SemaphoreType.DMA((2,)),
            ]),
        out_shape=jax.ShapeDtypeStruct(xs.shape, BF16),
        compiler_params=pltpu.CompilerParams(
            dimension_semantics=("arbitrary",), vmem_limit_bytes=VMEM_LIMIT),
        name="experts",
    )(plan['blk_expert'], plan['next_expert'], plan['n_used'], xs, w_gate, w_up, w_down)


def _combine_kernel(local_ref, runs_ref, x1_ref, hs_ref, p_ref, pos_ref, gsel_ref,
                    wsd_ref, gple_ref, wpg_ref, wp_ref, gfin_ref, ys_hbm, o_ref, ybuf, st_ref, e_buf, g_buf, sem,
                    *, n_win, final_norm):
    w = pl.program_id(0)
    slot = w % 2

    @pl.when(w == 0)
    def _():
        ybuf[...] = jnp.zeros(ybuf.shape, BF16)
        _start_runs(runs_ref, w, ybuf, slot, ys_hbm, sem, False)

    @pl.when(w + 1 < n_win)
    def _():
        _start_runs(runs_ref, w + 1, ybuf, 1 - slot, ys_hbm, sem, False)

    shared = _dot(hs_ref[...], wsd_ref[...])
    emb = _dot(p_ref[...].astype(BF16), wp_ref[...])

    assert CMB_BG <= 256 and SEL_ROWS // CMB_BG <= 256
    lane_b = lax.broadcasted_iota(jnp.int32, (WIN, CMB_BG), 1).astype(F32).astype(BF16)
    pos = pos_ref[...]
    gsel = gsel_ref[...]
    across = lambda col: jnp.broadcast_to(col.astype(F32), (WIN, CMB_BG)).astype(BF16)
    for k in range(TOP_K):
        pk = pos[:, k:k + 1]
        e_buf[k] = across(gsel[:, k:k + 1])
        g_buf[k] = jnp.where(lane_b == across(pk % CMB_BG), across(pk // CMB_BG), jnp.asarray(-1, BF16))

    def build_group(lg):
        for bg in range(lg * CMB_LG // CMB_BG, (lg + 1) * CMB_LG // CMB_BG):
            acc = jnp.zeros((WIN, CMB_BG), BF16)
            for k in range(TOP_K):
                acc = jnp.where(g_buf[k] == jnp.asarray(bg, BF16), e_buf[k], acc)
            st_ref[:, bg * CMB_BG:(bg + 1) * CMB_BG] = acc

    build_group(0)

    _wait_runs(local_ref, w, ybuf, slot, ys_hbm, sem, False)
    routed = None
    n_groups = SEL_ROWS // CMB_LG
    for lg in range(n_groups):
        if lg + 1 < n_groups:
            build_group(lg + 1)
        src = pl.multiple_of(slot * SEL_ROWS + lg * CMB_LG, CMB_LG)
        part = _dot(st_ref[:, lg * CMB_LG:(lg + 1) * CMB_LG], ybuf[pl.ds(src, CMB_LG), :])
        routed = part if routed is None else routed + part
    x2 = x1_ref[...] + routed + shared

    hp = _rms(x2, gple_ref[...]).astype(BF16)
    gate = jax.nn.sigmoid(_dot(hp, wpg_ref[...]))
    x3 = x2 + gate * emb
    o_ref[...] = _rms(x3, gfin_ref[...]) if final_norm else x3


def _combine(plan, ys, x1, hs, p, pos_t, gsel_t, wsd, g_ple, w_pg, w_p, g_fin, final_norm):
    t = x1.shape[0]
    n_win = t // WIN
    row = lambda width: pl.BlockSpec((WIN, width), lambda w, *_: (w, 0))
    const = lambda shape: pl.BlockSpec(shape, lambda w, *_: (0,) * len(shape))
    return pl.pallas_call(
        functools.partial(_combine_kernel, n_win=n_win, final_norm=final_norm),
        grid_spec=pltpu.PrefetchScalarGridSpec(
            num_scalar_prefetch=2,
            grid=(n_win,),
            in_specs=[
                row(D_MODEL), row(D_EXPERT), row(PLE_DIM),
                row(TOP_K), row(TOP_K),
                const((D_EXPERT, D_MODEL)),
                const((1, D_MODEL)), const((D_MODEL, D_MODEL)), const((PLE_DIM, D_MODEL)),
                const((1, D_MODEL)),
                pl.BlockSpec(memory_space=pl.ANY),
            ],
            out_specs=[row(D_MODEL), _staging_spec(0)],
            scratch_shapes=[
                pltpu.VMEM((WIN, SEL_ROWS), BF16),
                pltpu.VMEM((TOP_K, WIN, CMB_BG), BF16),
                pltpu.VMEM((TOP_K, WIN, CMB_BG), BF16),
                pltpu.SemaphoreType.DMA((2,)),
            ]),
        out_shape=[jax.ShapeDtypeStruct((t, D_MODEL), F32), _staging_shape(0)],
        compiler_params=pltpu.CompilerParams(
            dimension_semantics=("arbitrary",), vmem_limit_bytes=VMEM_LIMIT),
        name="combine",
    )(plan['local_off'], plan['runs'], x1, hs, p, pos_t, gsel_t,
      wsd, g_ple, w_pg, w_p, g_fin, ys)[0]


def kernel(x, p, g_mix, w_in, b_in, w_dw, b_dw, g_cln, b_cln, w_conv_out, b_conv_out, w_pool, s_pool,
           w_out, g_ffn, w_router, b_router, w_e_gate, w_e_up, w_e_down, w_s_gate, w_s_up, w_s_down,
           g_ple, w_ple_gate, w_ple, g_final):
    bsz, s, d = x.shape
    t = bsz * s
    depth = w_in.shape[0]
    xt = x.reshape(t, d)
    row = lambda v: v.reshape(1, -1)
    for i in range(depth):
        x1, h2 = _mixer(
            xt, s, row(g_mix[i]), w_in[i].astype(BF16), row(b_in[i]), w_dw[i], row(b_dw[i]),
            row(g_cln[i]), row(b_cln[i]), w_conv_out[i].astype(BF16), row(b_conv_out[i]),
            w_pool[i].astype(BF16), row(s_pool[i]), w_out[i].astype(BF16), row(g_ffn[i]))
        gsel, pos, cnt = _router(h2, w_router[i].T.astype(BF16), b_router[i].reshape(N_EXPERTS, 1))
        plan = _dispatch_plan(cnt, t)
        xs, hs = _dispatch(plan, h2, pos, w_s_gate[i].astype(BF16), w_s_up[i].astype(BF16))
        ys = _experts(plan, xs, w_e_gate[i], w_e_up[i], w_e_down[i])
        xt = _combine(
            plan, ys, x1, hs, p[i].reshape(t, PLE_DIM), pos.T, gsel.T, w_s_down[i].astype(BF16),
            row(g_ple[i]), w_ple_gate[i].astype(BF16), w_ple[i].astype(BF16), row(g_final),
            final_norm=(i == depth - 1))
    return xt.reshape(bsz, s, d)
```

```python
import functools

import jax
import jax.numpy as jnp
from jax import lax
from jax.experimental import pallas as pl
from jax.experimental.pallas import tpu as pltpu

D_MODEL = 1024
D_CONV = 1024
D_POOL = 1024
CONV_WIDTH = 31
POOL_WINDOWS = (2, 4, 8, 16)
POOL_GROUP = 256
PLE_DIM = 256
N_EXPERTS = 64
N_GROUPS = 8
GROUP_SIZE = N_EXPERTS // N_GROUPS
TOPK_GROUPS = 4
TOP_K = 8
D_EXPERT = 256
ROUTED_SCALE = 2.5
NORM_EPS = 1e-6

F32 = jnp.float32
BF16 = jnp.bfloat16

MIX_TM = 512
MIX_NV = MIX_TM // 8
CONV_MG = 8
ROW_CHUNK = 64
LANE = 128

ROUTER_TM = 1024
WIN = 256
SEL_ROWS = 2560
SEL_RG = 64
SEL_MM = 512
SEL_TAIL = 256
EXP_BM = 576
EXP_XDEPTH = 6
EXP_YDEPTH = 4
EXP_SPLIT = 4
RUN_FIELDS = 4
RUN_UNROLL = 4
CMB_LG = 512
CMB_BG = 128

V7X_VMEM_BYTES = 64 * 1024 * 1024
VMEM_LIMIT = V7X_VMEM_BYTES - 8 * 1024 * 1024

assert SEL_ROWS >= TOP_K * WIN + 7 * N_EXPERTS
assert (SEL_ROWS - 2 * SEL_TAIL) % SEL_MM == 0 and SEL_TAIL % SEL_RG == 0 and SEL_ROWS % CMB_LG == 0
assert N_EXPERTS % RUN_UNROLL == 0 and EXP_BM % (16 * EXP_SPLIT) == 0 and EXP_XDEPTH > 2 and EXP_YDEPTH >= 2


def _rms(x, g):
    ms = jnp.mean(x * x, axis=-1, keepdims=True)
    return x * lax.rsqrt(ms + NORM_EPS) * g


def _dot(a, b):
    return jnp.dot(a, b, preferred_element_type=F32)


def _mixer_kernel(x_ref, gmix_ref, win_ref, bin_ref, wdw_ref, bdw_ref, gcln_ref, bcln_ref,
                  wco_ref, bco_ref, wpool_ref, spool_ref, wout_ref, gffn_ref,
                  x1_ref, h2_ref, a_ext, a_prev, u_ext, u_prev, c_buf, q_buf, p_buf, *, tiles_per_seq):
    i = pl.program_id(0) % tiles_per_seq
    tm = MIX_TM
    nv = MIX_NV

    @pl.when(i == 0)
    def _():
        a_prev[...] = jnp.zeros(a_prev.shape, F32)
        u_prev[...] = jnp.zeros(u_prev.shape, F32)

    n_col = D_MODEL // LANE
    hn = _rms(x_ref[...], gmix_ref[...])
    for lc in range(n_col):
        for s in range(8):
            p_buf[lc, pl.ds(s, nv, stride=8), :] = hn[s * nv:(s + 1) * nv, lc * LANE:(lc + 1) * LANE]
    h = jnp.concatenate([p_buf[lc] for lc in range(n_col)], axis=-1).astype(BF16)

    def proj(lo, hi):
        return _dot(h, win_ref[:, lo:hi]) + bin_ref[:, lo:hi]

    glu = proj(0, D_CONV) * jax.nn.sigmoid(proj(D_CONV, 2 * D_CONV))
    for lc in range(D_CONV // LANE):
        a_ext[lc, tm:2 * tm, :] = glu[:, lc * LANE:(lc + 1) * LANE]
    u_ext[tm:2 * tm, :] = proj(2 * D_CONV, 2 * D_CONV + D_POOL)

    def delayed_groups(ext, prev, first_group):
        last_row = lax.broadcasted_iota(jnp.int32, (8, ext.shape[-1]), 0) == 7
        for g in range(first_group, nv):
            rows = slice(8 * g, 8 * g + 8)
            mixed = jnp.where(last_row, prev[rows, :], ext[tm + 8 * g:tm + 8 * g + 8, :])
            ext[rows, :] = pltpu.roll(mixed, 1, axis=0)
            prev[rows, :] = ext[tm + 8 * g:tm + 8 * g + 8, :]

    delayed_groups(u_ext, u_prev, nv - (max(POOL_WINDOWS) - 1))

    for r0 in range(0, tm, ROW_CHUNK):
        row = r0 + lax.broadcasted_iota(jnp.int32, (ROW_CHUNK, POOL_GROUP), 0)
        t1 = i * tm + (row % 8) * nv + row // 8 + 1
        for gi, w in enumerate(POOL_WINDOWS):
            ls = slice(gi * POOL_GROUP, (gi + 1) * POOL_GROUP)
            tok = u_ext[tm + r0:tm + r0 + ROW_CHUNK, ls]
            s = tok
            for j in range(1, w):
                s = s + u_ext[tm + r0 - 8 * j:tm + r0 - 8 * j + ROW_CHUNK, ls]
            cnt = jnp.minimum(t1, w).astype(F32)
            q_buf[r0:r0 + ROW_CHUNK, ls] = s / cnt - tok

    qs_out = []
    for gi in range(len(POOL_WINDOWS)):
        ls = slice(gi * POOL_GROUP, (gi + 1) * POOL_GROUP)
        qs_out.append(_dot(q_buf[:, ls].astype(BF16), wpool_ref[gi]) * spool_ref[:, ls])
    branch_b = jnp.concatenate(qs_out, axis=-1)

    def conv_column(lc, carry):
        a_col = a_ext.at[lc]
        delayed_groups(a_col, a_prev.at[lc], nv - (CONV_WIDTH - 1))
        w_col = wdw_ref.at[lc]
        for g0 in range(0, nv, CONV_MG):
            acc = None
            for k in range(CONV_WIDTH):
                src = nv + g0 + k - (CONV_WIDTH - 1)
                term = a_col[8 * src:8 * (src + CONV_MG), :] * w_col[k:k + 1, :]
                acc = term if acc is None else acc + term
            c_buf[lc, 8 * g0:8 * (g0 + CONV_MG), :] = acc + bdw_ref[lc]
        return carry
    lax.fori_loop(0, D_CONV // LANE, conv_column, 0)

    c2 = 2 * D_CONV + D_POOL
    gate_a = jax.nn.sigmoid(proj(c2, c2 + D_MODEL))
    gate_b = jax.nn.sigmoid(proj(c2 + D_MODEL, c2 + 2 * D_MODEL))

    c = jnp.concatenate([c_buf[lc] for lc in range(D_CONV // LANE)], axis=-1)
    mu = jnp.mean(c, axis=-1, keepdims=True)
    xc = c - mu
    var = jnp.mean(xc * xc, axis=-1, keepdims=True)
    y = xc * lax.rsqrt(var + NORM_EPS) * gcln_ref[...] + bcln_ref[...]
    y = y * jax.nn.sigmoid(y)
    branch_a = _dot(y.astype(BF16), wco_ref[...]) + bco_ref[...]

    merged = gate_a * branch_a + gate_b * branch_b
    y = _dot(merged.astype(BF16), wout_ref[...])
    for lc in range(n_col):
        p_buf[lc] = y[:, lc * LANE:(lc + 1) * LANE]
    for lc in range(n_col):
        cols = slice(lc * LANE, (lc + 1) * LANE)
        for s in range(8):
            for j in range(nv // 8):
                rows = slice(s * nv + 8 * j, s * nv + 8 * j + 8)
                x1_ref[rows, cols] = x_ref[rows, cols] + p_buf[lc, pl.ds(8 * (8 * j) + s, 8, stride=8), :]
    h2_ref[...] = _rms(x1_ref[...], gffn_ref[...]).astype(BF16)


def _const_spec(shape):
    n = len(shape)
    return pl.BlockSpec(shape, lambda i, _n=n: (0,) * _n, pipeline_mode=pl.Buffered(1))


def _mixer(x, seq_len, g_mix, w_in, b_in, w_dw, b_dw, g_cln, b_cln, w_co, b_co, w_pool, s_pool, w_out,
           g_ffn):
    t = x.shape[0]
    tm = MIX_TM
    assert seq_len % tm == 0 and MIX_NV >= CONV_WIDTH and MIX_NV >= max(POOL_WINDOWS)
    d_in = w_in.shape[1]
    row = pl.BlockSpec((tm, D_MODEL), lambda i: (i, 0))
    n_col = D_CONV // LANE
    w_dw = w_dw.reshape(CONV_WIDTH, n_col, LANE).transpose(1, 0, 2)
    b_dw = b_dw.reshape(n_col, 1, LANE)
    return pl.pallas_call(
        functools.partial(_mixer_kernel, tiles_per_seq=seq_len // tm),
        grid=(t // tm,),
        in_specs=[
            row,
            _const_spec((1, D_MODEL)),
            _const_spec((D_MODEL, d_in)),
            _const_spec((1, d_in)),
            _const_spec((n_col, CONV_WIDTH, LANE)),
            _const_spec((n_col, 1, LANE)),
            _const_spec((1, D_CONV)),
            _const_spec((1, D_CONV)),
            _const_spec((D_CONV, D_MODEL)),
            _const_spec((1, D_MODEL)),
            _const_spec((len(POOL_WINDOWS), POOL_GROUP, POOL_GROUP)),
            _const_spec((1, D_POOL)),
            _const_spec((D_MODEL, D_MODEL)),
            _const_spec((1, D_MODEL)),
        ],
        out_specs=[row, row],
        out_shape=[jax.ShapeDtypeStruct((t, D_MODEL), F32),
                   jax.ShapeDtypeStruct((t, D_MODEL), BF16)],
        scratch_shapes=[
            pltpu.VMEM((n_col, 2 * tm, LANE), F32),
            pltpu.VMEM((n_col, tm, LANE), F32),
            pltpu.VMEM((2 * tm, D_POOL), F32),
            pltpu.VMEM((tm, D_POOL), F32),
            pltpu.VMEM((n_col, tm, LANE), F32),
            pltpu.VMEM((tm, D_POOL), F32),
            pltpu.VMEM((D_MODEL // LANE, tm, LANE), F32),
        ],
        compiler_params=pltpu.CompilerParams(
            dimension_semantics=("arbitrary",), vmem_limit_bytes=VMEM_LIMIT),
        name="mixer",
    )(x, g_mix, w_in, b_in, w_dw, b_dw, g_cln, b_cln, w_co, b_co, w_pool, s_pool, w_out, g_ffn)


def _beats(v, other, other_is_later):
    v = jnp.broadcast_to(v, other.shape)
    return jnp.where(other_is_later, jnp.where(v >= other, 1, 0), jnp.where(v > other, 1, 0))


def _router_kernel(h2_ref, wrt_ref, br_ref, utri_ref, ltri_ref, gsel_ref, pos_ref, cnt_ref):
    tm = ROUTER_TM
    logits = lax.dot_general(wrt_ref[...], h2_ref[...], (((1,), (1,)), ((), ())),
                             preferred_element_type=F32)
    scores = jax.nn.sigmoid(logits)
    sel = scores + br_ref[...]
    shape3 = (N_GROUPS, GROUP_SIZE, tm)
    sel3 = sel.reshape(shape3)
    scores3 = scores.reshape(shape3)
    neg_inf = jnp.float32(-jnp.inf)

    member = lax.broadcasted_iota(jnp.int32, shape3, 1)
    m1 = jnp.max(sel3, axis=1, keepdims=True)
    first = jnp.min(jnp.where(sel3 == m1, member, GROUP_SIZE), axis=1, keepdims=True)
    m2 = jnp.max(jnp.where(member == first, neg_inf, sel3), axis=1, keepdims=True)
    gscore = jnp.broadcast_to(m1 + m2, shape3)

    gidx = lax.broadcasted_iota(jnp.int32, shape3, 0)
    grank = jnp.zeros(shape3, jnp.int32)
    for j in range(N_GROUPS):
        sj = gscore[j:j + 1]
        grank = grank + _beats(sj, gscore, gidx > j)
    masked = jnp.where(grank < TOPK_GROUPS, sel3, neg_inf)

    eidx = gidx * GROUP_SIZE + member
    work = masked
    erank = jnp.full(shape3, TOP_K, jnp.int32)
    for k in range(TOP_K):
        best = jnp.max(jnp.max(work, axis=0, keepdims=True), axis=1, keepdims=True)
        cand = jnp.where(work == best, eidx, N_EXPERTS)
        pick = jnp.min(jnp.min(cand, axis=0, keepdims=True), axis=1, keepdims=True)
        hit = eidx == pick
        work = jnp.where(hit, neg_inf, work)
        erank = jnp.where(hit, k, erank)
    chosen = erank < TOP_K
    top_s = jnp.where(chosen, scores3, 0.0)
    denom = jnp.sum(jnp.sum(top_s, axis=0, keepdims=True), axis=1, keepdims=True)
    gates3 = top_s / denom * ROUTED_SCALE
    chosen2 = jnp.where(chosen, 1.0, 0.0).reshape(N_EXPERTS, tm)
    gates3 = gates3.astype(BF16).astype(F32)
    for k in range(TOP_K):
        gk = jnp.sum(jnp.sum(jnp.where(erank == k, gates3, 0.0), axis=0, keepdims=True), axis=1, keepdims=True)
        gsel_ref[k:k + 1, :] = gk.reshape(1, tm)

    for w in range(tm // WIN):
        ls = slice(w * WIN, (w + 1) * WIN)
        mw = chosen2[:, ls]
        rank = _dot(mw.astype(BF16), utri_ref[...])
        n = jnp.sum(mw, axis=1, keepdims=True)
        run = jnp.floor((n + 7.0) * 0.125) * 8.0
        start = _dot(ltri_ref[...], jnp.broadcast_to(run, (N_EXPERTS, WIN)).astype(BF16))
        row3 = (rank + start).reshape(N_GROUPS, GROUP_SIZE, WIN)
        er = erank[:, :, ls]
        for k in range(TOP_K):
            pk = jnp.sum(jnp.sum(jnp.where(er == k, row3, 0.0), axis=0, keepdims=True), axis=1, keepdims=True)
            pos_ref[k:k + 1, ls] = pk.reshape(1, WIN).astype(jnp.int32)
        cnt_ref[w] = n


def _router(h2, w_rt, b_r):
    t = h2.shape[0]
    tm = ROUTER_TM
    utri = jnp.triu(jnp.ones((WIN, WIN), BF16), k=1)
    ltri = jnp.tril(jnp.ones((N_EXPERTS, N_EXPERTS), BF16), k=-1)
    return pl.pallas_call(
        _router_kernel,
        grid=(t // tm,),
        in_specs=[
            pl.BlockSpec((tm, D_MODEL), lambda i: (i, 0)),
            _const_spec((N_EXPERTS, D_MODEL)),
            _const_spec((N_EXPERTS, 1)),
            _const_spec((WIN, WIN)),
            _const_spec((N_EXPERTS, N_EXPERTS)),
        ],
        out_specs=[
            pl.BlockSpec((TOP_K, tm), lambda i: (0, i)),
            pl.BlockSpec((TOP_K, tm), lambda i: (0, i)),
            pl.BlockSpec((tm // WIN, N_EXPERTS, 1), lambda i: (i, 0, 0)),
        ],
        out_shape=[
            jax.ShapeDtypeStruct((TOP_K, t), F32),
            jax.ShapeDtypeStruct((TOP_K, t), jnp.int32),
            jax.ShapeDtypeStruct((t // WIN, N_EXPERTS, 1), F32),
        ],
        compiler_params=pltpu.CompilerParams(
            dimension_semantics=("arbitrary",), vmem_limit_bytes=VMEM_LIMIT),
        name="router",
    )(h2, w_rt, b_r, utri, ltri)


def _sorted_rows_bound(t):
    rows = t * TOP_K + (t // WIN) * N_EXPERTS * 7 + N_EXPERTS * (EXP_BM - 1)
    blocks = -(-rows // EXP_BM)
    return (blocks + blocks % 2) * EXP_BM


def _dispatch_plan(cnt, t):
    nw = t // WIN
    n = cnt.reshape(nw, N_EXPERTS).astype(jnp.int32)
    run = (n + 7) // 8 * 8
    local_end = jnp.cumsum(run, axis=1)
    local_off = jnp.concatenate([jnp.zeros((nw, 1), jnp.int32), local_end], axis=1)
    total = jnp.sum(run, axis=0)
    region = (total + EXP_BM - 1) // EXP_BM * EXP_BM
    eid = jnp.arange(N_EXPERTS, dtype=jnp.int32)
    last_owner = jnp.max(jnp.where(region > 0, eid, 0))
    odd = (jnp.sum(region) // EXP_BM) % 2
    region = region + jnp.where(eid == last_owner, odd * EXP_BM, 0)
    region_end = jnp.cumsum(region)
    base = region_end - region
    global_off = base[None, :] + jnp.cumsum(run, axis=0) - run
    n_blocks = _sorted_rows_bound(t) // EXP_BM
    n_used = region_end[-1] // EXP_BM
    blk = jnp.arange(n_blocks, dtype=jnp.int32)
    blk_expert = jnp.sum((region_end[None, :] <= blk[:, None] * EXP_BM).astype(jnp.int32), axis=1)
    blk_expert = jnp.minimum(blk_expert, N_EXPERTS - 1)
    later_nonempty = (eid[None, :] > eid[:, None]) & (region[None, :] > 0)
    next_expert = jnp.min(jnp.where(later_nonempty, eid[None, :], N_EXPERTS), axis=1).astype(jnp.int32)
    return dict(
        local_off=local_off.reshape(-1),
        runs=jnp.stack([local_off[:, :N_EXPERTS], run, global_off, jnp.zeros_like(run)], axis=-1).reshape(-1),
        fill_off=base + total, fill_cnt=region - total,
        blk_expert=blk_expert.astype(jnp.int32), next_expert=next_expert,
        n_used=n_used.reshape(1).astype(jnp.int32))


def _run_copy(runs_ref, win, e, vmem_buf, slot, hbm_buf, sem, to_hbm):
    p = (win * N_EXPERTS + e) * RUN_FIELDS
    lo = pl.multiple_of(runs_ref[p], 8)
    cnt = pl.multiple_of(runs_ref[p + 1], 8)
    go = pl.multiple_of(runs_ref[p + 2], 8)
    v = vmem_buf.at[pl.ds(pl.multiple_of(slot * SEL_ROWS + lo, 8), cnt)]
    h = hbm_buf.at[pl.ds(go, cnt)]
    cp = pltpu.make_async_copy(v, h, sem.at[slot]) if to_hbm else pltpu.make_async_copy(h, v, sem.at[slot])
    return cnt, cp


def _start_runs(runs_ref, win, vmem_buf, slot, hbm_buf, sem, to_hbm):
    def body(i, carry):
        copies = [_run_copy(runs_ref, win, i * RUN_UNROLL + j, vmem_buf, slot, hbm_buf, sem, to_hbm)
                  for j in range(RUN_UNROLL)]
        for cnt, cp in copies:
            @pl.when(cnt > 0)
            def _(cp=cp):
                cp.start()
        return carry
    lax.fori_loop(0, N_EXPERTS // RUN_UNROLL, body, 0)


def _wait_runs(local_ref, win, vmem_buf, slot, hbm_buf, sem, to_hbm):
    total = pl.multiple_of(local_ref[win * (N_EXPERTS + 1) + N_EXPERTS], 8)
    v = vmem_buf.at[pl.ds(pl.multiple_of(slot * SEL_ROWS, 8), total)]
    h = hbm_buf.at[pl.ds(0, total)]
    cp = pltpu.make_async_copy(v, h, sem.at[slot]) if to_hbm else pltpu.make_async_copy(h, v, sem.at[slot])

    @pl.when(total > 0)
    def _():
        cp.wait()


def _dispatch_kernel(local_ref, runs_ref, fill_off_ref, fill_cnt_ref, h2_ref, pos_ref, wsg_ref, wsu_ref,
                     xs_hbm, sbuf, hs_ref, s_ref, sem, zsem, *, n_win):
    w = pl.program_id(0)
    slot = w % 2
    pos = pos_ref[...]

    h2 = h2_ref[...]
    assert SEL_RG <= 256
    rid_b = lax.broadcasted_iota(jnp.int32, (SEL_RG, WIN), 0).astype(F32).astype(BF16)
    one_b = jnp.ones((SEL_RG, WIN), BF16)
    def compact(first, n_rows):
        for r0 in range(first, first + n_rows, SEL_RG):
            acc = jnp.zeros((SEL_RG, WIN), BF16)
            for k in range(TOP_K):
                off = (pos[k:k + 1, :] - r0).astype(F32)
                off = jnp.broadcast_to(off, (SEL_RG, WIN)).astype(BF16)
                acc = jnp.where(rid_b == off, one_b, acc)
            s_ref[r0:r0 + SEL_RG, :] = acc
        dst = pl.multiple_of(slot * SEL_ROWS + first, SEL_TAIL)
        sbuf[pl.ds(dst, n_rows), :] = _dot(s_ref[first:first + n_rows, :], h2).astype(BF16)

    @pl.when(w == 0)
    def _():
        for s in range(2):
            sbuf[(s + 1) * SEL_ROWS - SEL_TAIL:(s + 1) * SEL_ROWS, :] = jnp.zeros((SEL_TAIL, D_MODEL), BF16)

    for first in range(0, SEL_ROWS - 2 * SEL_TAIL, SEL_MM):
        compact(first, SEL_MM)
    hs = _dot(h2, wsg_ref[...])
    hs_ref[...] = (hs * jax.nn.sigmoid(hs) * _dot(h2, wsu_ref[...])).astype(BF16)
    compact(SEL_ROWS - 2 * SEL_TAIL, SEL_TAIL)

    @pl.when(local_ref[w * (N_EXPERTS + 1) + N_EXPERTS] > SEL_ROWS - SEL_TAIL)
    def _():
        compact(SEL_ROWS - SEL_TAIL, SEL_TAIL)

    _start_runs(runs_ref, w, sbuf, slot, xs_hbm, sem, True)

    @pl.when(w > 0)
    def _():
        _wait_runs(local_ref, w - 1, sbuf, 1 - slot, xs_hbm, sem, True)

    @pl.when(w == n_win - 1)
    def _():
        sbuf[2 * SEL_ROWS:, :] = jnp.zeros((2 * EXP_BM, D_MODEL), BF16)

        def fill(e, wait):
            cnt = pl.multiple_of(fill_cnt_ref[e], 8)
            off = pl.multiple_of(fill_off_ref[e], 8)
            cp = pltpu.make_async_copy(sbuf.at[pl.ds(2 * SEL_ROWS, cnt)], xs_hbm.at[pl.ds(off, cnt)], zsem)

            @pl.when(cnt > 0)
            def _():
                if wait:
                    cp.wait()
                else:
                    cp.start()

        def start_body(e, carry):
            fill(e, False)
            return carry

        def wait_body(e, carry):
            fill(e, True)
            return carry
        lax.fori_loop(0, N_EXPERTS, start_body, 0)
        _wait_runs(local_ref, w, sbuf, slot, xs_hbm, sem, True)
        lax.fori_loop(0, N_EXPERTS, wait_body, 0)


def _staging_shape(extra_rows):
    return jax.ShapeDtypeStruct((2 * SEL_ROWS + extra_rows, D_MODEL), BF16)


def _staging_spec(extra_rows):
    return pl.BlockSpec((2 * SEL_ROWS + extra_rows, D_MODEL), lambda w, *_: (0, 0))


def _dispatch(plan, h2, pos, wsg, wsu):
    t = h2.shape[0]
    n_win = t // WIN
    xs, _, hs = pl.pallas_call(
        functools.partial(_dispatch_kernel, n_win=n_win),
        grid_spec=pltpu.PrefetchScalarGridSpec(
            num_scalar_prefetch=4,
            grid=(n_win,),
            in_specs=[
                pl.BlockSpec((WIN, D_MODEL), lambda w, *_: (w, 0)),
                pl.BlockSpec((TOP_K, WIN), lambda w, *_: (0, w)),
                pl.BlockSpec((D_MODEL, D_EXPERT), lambda w, *_: (0, 0)),
                pl.BlockSpec((D_MODEL, D_EXPERT), lambda w, *_: (0, 0)),
            ],
            out_specs=[pl.BlockSpec(memory_space=pl.ANY), _staging_spec(2 * EXP_BM),
                       pl.BlockSpec((WIN, D_EXPERT), lambda w, *_: (w, 0))],
            scratch_shapes=[
                pltpu.VMEM((SEL_ROWS, WIN), BF16),
                pltpu.SemaphoreType.DMA((2,)),
                pltpu.SemaphoreType.DMA,
            ]),
        out_shape=[jax.ShapeDtypeStruct((_sorted_rows_bound(t), D_MODEL), BF16), _staging_shape(2 * EXP_BM),
                   jax.ShapeDtypeStruct((t, D_EXPERT), BF16)],
        compiler_params=pltpu.CompilerParams(
            dimension_semantics=("arbitrary",), vmem_limit_bytes=VMEM_LIMIT),
        name="dispatch",
    )(plan['local_off'], plan['runs'], plan['fill_off'], plan['fill_cnt'], h2, pos, wsg, wsu)
    return xs, hs


def _expert_kernel(blk_expert_ref, next_expert_ref, n_used_ref, xs_hbm, wg_hbm, wu_hbm, wd_hbm, ys_hbm,
                   xbuf, ybuf, wg_st, wu_st, wd_st, wg_bf, wu_bf, wd_bf, xsem, ysem, wsem):
    n_used = n_used_ref[0]
    part = EXP_BM // EXP_SPLIT

    def row_copies(b, slot, fetch):
        out = []
        for q in range(EXP_SPLIT):
            hbm_rows = pl.ds(pl.multiple_of(b * EXP_BM + q * part, part), part)
            if fetch:
                out.append(pltpu.make_async_copy(xs_hbm.at[hbm_rows], xbuf.at[slot, q * part:(q + 1) * part],
                                                 xsem.at[slot]))
            else:
                out.append(pltpu.make_async_copy(ybuf.at[slot, q * part:(q + 1) * part], ys_hbm.at[hbm_rows],
                                                 ysem.at[slot]))
        return out

    def weight_copies(e, slot):
        return [pltpu.make_async_copy(wg_hbm.at[e], wg_st.at[slot], wsem.at[slot]),
                pltpu.make_async_copy(wu_hbm.at[e], wu_st.at[slot], wsem.at[slot]),
                pltpu.make_async_copy(wd_hbm.at[e], wd_st.at[slot], wsem.at[slot])]

    def start(copies):
        for c in copies:
            c.start()

    def wait(copies):
        for c in copies:
            c.wait()

    for ahead in range(EXP_XDEPTH - 2):
        @pl.when(ahead < n_used)
        def _(ahead=ahead):
            start(row_copies(ahead, ahead, True))

    @pl.when(n_used > 0)
    def _():
        start(weight_copies(blk_expert_ref[0], 0))

    def enter_block(b, wset):
        e = blk_expert_ref[b]
        new_expert = jnp.logical_or(b == 0, e != blk_expert_ref[jnp.maximum(b - 1, 0)])
        wset = jnp.where(new_expert, 1 - wset, wset)

        @pl.when(new_expert)
        def _():
            wait(weight_copies(e, wset))
            wg_bf[wset] = wg_st[wset].astype(BF16)
            wu_bf[wset] = wu_st[wset].astype(BF16)
            wd_bf[wset] = wd_st[wset].astype(BF16)
            nxt = next_expert_ref[e]

            @pl.when(nxt < N_EXPERTS)
            def _():
                start(weight_copies(nxt, 1 - wset))

        ahead = b + EXP_XDEPTH - 2

        @pl.when(ahead < n_used)
        def _():
            start(row_copies(ahead, ahead % EXP_XDEPTH, True))

        wait(row_copies(b, b % EXP_XDEPTH, True))

        @pl.when(b >= EXP_YDEPTH)
        def _():
            wait(row_copies(b - EXP_YDEPTH, b % EXP_YDEPTH, False))
        return wset

    def compute(b, wset):
        x = xbuf[b % EXP_XDEPTH]
        hg = _dot(x, wg_bf[wset])
        hb = hg * jax.nn.sigmoid(hg) * _dot(x, wu_bf[wset])
        ybuf[b % EXP_YDEPTH] = _dot(hb.astype(BF16), wd_bf[wset]).astype(BF16)

    def body(p, wset):
        b0 = 2 * p
        w0 = enter_block(b0, wset)
        w1 = enter_block(b0 + 1, w0)
        compute(b0, w0)
        compute(b0 + 1, w1)
        start(row_copies(b0, b0 % EXP_YDEPTH, False))
        start(row_copies(b0 + 1, (b0 + 1) % EXP_YDEPTH, False))
        return w1

    lax.fori_loop(0, n_used // 2, body, jnp.int32(1))

    for back in range(EXP_YDEPTH, 0, -1):
        @pl.when(n_used >= back)
        def _(back=back):
            wait(row_copies(n_used - back, (n_used - back) % EXP_YDEPTH, False))


def _experts(plan, xs, w_gate, w_up, w_down):
    any_spec = pl.BlockSpec(memory_space=pl.ANY)
    return pl.pallas_call(
        _expert_kernel,
        grid_spec=pltpu.PrefetchScalarGridSpec(
            num_scalar_prefetch=3,
            grid=(1,),
            in_specs=[any_spec, any_spec, any_spec, any_spec],
            out_specs=any_spec,
            scratch_shapes=[
                pltpu.VMEM((EXP_XDEPTH, EXP_BM, D_MODEL), BF16),
                pltpu.VMEM((EXP_YDEPTH, EXP_BM, D_MODEL), BF16),
                pltpu.VMEM((2, D_MODEL, D_EXPERT), F32),
                pltpu.VMEM((2, D_MODEL, D_EXPERT), F32),
                pltpu.VMEM((2, D_EXPERT, D_MODEL), F32),
                pltpu.VMEM((2, D_MODEL, D_EXPERT), BF16),
                pltpu.VMEM((2, D_MODEL, D_EXPERT), BF16),
                pltpu.VMEM((2, D_EXPERT, D_MODEL), BF16),
                pltpu.SemaphoreType.DMA((EXP_XDEPTH,)),
                pltpu.SemaphoreType.DMA((EXP_YDEPTH,)),
                pltpu.SemaphoreType.DMA((2,)),
            ]),
        out_shape=jax.ShapeDtypeStruct(xs.shape, BF16),
        compiler_params=pltpu.CompilerParams(
            dimension_semantics=("arbitrary",), vmem_limit_bytes=VMEM_LIMIT),
        name="experts",
    )(plan['blk_expert'], plan['next_expert'], plan['n_used'], xs, w_gate, w_up, w_down)


def _combine_kernel(local_ref, runs_ref, x1_ref, hs_ref, p_ref, sel_ref,
                    wsd_ref, gple_ref, wpg_ref, wp_ref, gfin_ref, ys_hbm, o_ref, ybuf, st_ref, e_buf, g_buf, sem,
                    *, n_win, final_norm):
    w = pl.program_id(0)
    slot = w % 2

    @pl.when(w == 0)
    def _():
        ybuf[...] = jnp.zeros(ybuf.shape, BF16)
        _start_runs(runs_ref, w, ybuf, slot, ys_hbm, sem, False)

    @pl.when(w + 1 < n_win)
    def _():
        _start_runs(runs_ref, w + 1, ybuf, 1 - slot, ys_hbm, sem, False)

    shared = _dot(hs_ref[...], wsd_ref[...])
    emb = _dot(p_ref[...].astype(BF16), wp_ref[...])

    assert CMB_BG <= 256 and SEL_ROWS // CMB_BG <= 256
    lane_b = lax.broadcasted_iota(jnp.int32, (WIN, CMB_BG), 1).astype(F32).astype(BF16)
    by_token = sel_ref[...]
    across = lambda c: jnp.broadcast_to(by_token[:, c:c + 1], (WIN, CMB_BG)).astype(BF16)
    for k in range(TOP_K):
        e_buf[k] = across(2 * TOP_K + k)
        g_buf[k] = jnp.where(lane_b == across(k), across(TOP_K + k), jnp.asarray(-1, BF16))

    def build_group(lg):
        for bg in range(lg * CMB_LG // CMB_BG, (lg + 1) * CMB_LG // CMB_BG):
            acc = jnp.zeros((WIN, CMB_BG), BF16)
            for k in range(TOP_K):
                acc = jnp.where(g_buf[k] == jnp.asarray(bg, BF16), e_buf[k], acc)
            st_ref[:, bg * CMB_BG:(bg + 1) * CMB_BG] = acc

    build_group(0)

    _wait_runs(local_ref, w, ybuf, slot, ys_hbm, sem, False)
    routed = None
    n_groups = SEL_ROWS // CMB_LG
    for lg in range(n_groups):
        if lg + 1 < n_groups:
            build_group(lg + 1)
        src = pl.multiple_of(slot * SEL_ROWS + lg * CMB_LG, CMB_LG)
        part = _dot(st_ref[:, lg * CMB_LG:(lg + 1) * CMB_LG], ybuf[pl.ds(src, CMB_LG), :])
        routed = part if routed is None else routed + part
    x2 = x1_ref[...] + routed + shared

    hp = _rms(x2, gple_ref[...]).astype(BF16)
    gate = jax.nn.sigmoid(_dot(hp, wpg_ref[...]))
    x3 = x2 + gate * emb
    o_ref[...] = _rms(x3, gfin_ref[...]) if final_norm else x3


def _combine(plan, ys, x1, hs, p, pos, gsel, wsd, g_ple, w_pg, w_p, g_fin, final_norm):
    t = x1.shape[0]
    n_win = t // WIN
    by_token = jnp.concatenate([(pos % CMB_BG).astype(F32), (pos // CMB_BG).astype(F32), gsel], axis=0).T
    by_token = jnp.pad(by_token, ((0, 0), (0, LANE - 3 * TOP_K)))
    row = lambda width: pl.BlockSpec((WIN, width), lambda w, *_: (w, 0))
    const = lambda shape: pl.BlockSpec(shape, lambda w, *_: (0,) * len(shape))
    return pl.pallas_call(
        functools.partial(_combine_kernel, n_win=n_win, final_norm=final_norm),
        grid_spec=pltpu.PrefetchScalarGridSpec(
            num_scalar_prefetch=2,
            grid=(n_win,),
            in_specs=[
                row(D_MODEL), row(D_EXPERT), row(PLE_DIM),
                row(LANE),
                const((D_EXPERT, D_MODEL)),
                const((1, D_MODEL)), const((D_MODEL, D_MODEL)), const((PLE_DIM, D_MODEL)),
                const((1, D_MODEL)),
                pl.BlockSpec(memory_space=pl.ANY),
            ],
            out_specs=[row(D_MODEL), _staging_spec(0)],
            scratch_shapes=[
                pltpu.VMEM((WIN, SEL_ROWS), BF16),
                pltpu.VMEM((TOP_K, WIN, CMB_BG), BF16),
                pltpu.VMEM((TOP_K, WIN, CMB_BG), BF16),
                pltpu.SemaphoreType.DMA((2,)),
            ]),
        out_shape=[jax.ShapeDtypeStruct((t, D_MODEL), F32), _staging_shape(0)],
        compiler_params=pltpu.CompilerParams(
            dimension_semantics=("arbitrary",), vmem_limit_bytes=VMEM_LIMIT),
        name="combine",
    )(plan['local_off'], plan['runs'], x1, hs, p, by_token,
      wsd, g_ple, w_pg, w_p, g_fin, ys)[0]


def kernel(x, p, g_mix, w_in, b_in, w_dw, b_dw, g_cln, b_cln, w_conv_out, b_conv_out, w_pool, s_pool,
           w_out, g_ffn, w_router, b_router, w_e_gate, w_e_up, w_e_down, w_s_gate, w_s_up, w_s_down,
           g_ple, w_ple_gate, w_ple, g_final):
    bsz, s, d = x.shape
    t = bsz * s
    depth = w_in.shape[0]
    xt = x.reshape(t, d)
    row = lambda v: v.reshape(1, -1)
    for i in range(depth):
        x1, h2 = _mixer(
            xt, s, row(g_mix[i]), w_in[i].astype(BF16), row(b_in[i]), w_dw[i], row(b_dw[i]),
            row(g_cln[i]), row(b_cln[i]), w_conv_out[i].astype(BF16), row(b_conv_out[i]),
            w_pool[i].astype(BF16), row(s_pool[i]), w_out[i].astype(BF16), row(g_ffn[i]))
        gsel, pos, cnt = _router(h2, w_router[i].T.astype(BF16), b_router[i].reshape(N_EXPERTS, 1))
        plan = _dispatch_plan(cnt, t)
        xs, hs = _dispatch(plan, h2, pos, w_s_gate[i].astype(BF16), w_s_up[i].astype(BF16))
        ys = _experts(plan, xs, w_e_gate[i], w_e_up[i], w_e_down[i])
        xt = _combine(
            plan, ys, x1, hs, p[i].reshape(t, PLE_DIM), pos, gsel, w_s_down[i].astype(BF16),
            row(g_ple[i]), w_ple_gate[i].astype(BF16), w_ple[i].astype(BF16), row(g_final),
            final_norm=(i == depth - 1))
    return xt.reshape(bsz, s, d)
```

```python
import functools

import jax
import jax.numpy as jnp
from jax import lax
from jax.experimental import pallas as pl
from jax.experimental.pallas import tpu as pltpu

D_MODEL = 1024
D_CONV = 1024
D_POOL = 1024
CONV_WIDTH = 31
POOL_WINDOWS = (2, 4, 8, 16)
POOL_GROUP = 256
PLE_DIM = 256
N_EXPERTS = 64
N_GROUPS = 8
GROUP_SIZE = N_EXPERTS // N_GROUPS
TOPK_GROUPS = 4
TOP_K = 8
D_EXPERT = 256
ROUTED_SCALE = 2.5
NORM_EPS = 1e-6

F32 = jnp.float32
BF16 = jnp.bfloat16

MIX_TM = 512
MIX_NV = MIX_TM // 8
CONV_MG = 8
ROW_CHUNK = 64
LANE = 128

ROUTER_TM = 1024
WIN = 256
SEL_ROWS = 2560
SEL_RG = 128
SEL_MM = 512
SEL_TAIL = 256
EXP_BM = 576
EXP_XDEPTH = 6
EXP_YDEPTH = 4
EXP_SPLIT = 4
RUN_FIELDS = 4
RUN_UNROLL = 8
CMB_LG = 512

V7X_VMEM_BYTES = 64 * 1024 * 1024
VMEM_LIMIT = V7X_VMEM_BYTES - 8 * 1024 * 1024

assert SEL_ROWS >= TOP_K * WIN + 7 * N_EXPERTS
assert (SEL_ROWS - 2 * SEL_TAIL) % SEL_MM == 0 and SEL_TAIL % SEL_RG == 0 and SEL_ROWS % CMB_LG == 0
assert N_EXPERTS % RUN_UNROLL == 0 and EXP_BM % (16 * EXP_SPLIT) == 0 and EXP_XDEPTH > 2 and EXP_YDEPTH >= 2


def _rms(x, g):
    ms = jnp.mean(x * x, axis=-1, keepdims=True)
    return x * lax.rsqrt(ms + NORM_EPS) * g


def _dot(a, b):
    return jnp.dot(a, b, preferred_element_type=F32)


def _dot_t(a, b):
    return lax.dot_general(a, b, (((0,), (0,)), ((), ())), preferred_element_type=F32)


def _mixer_kernel(x_ref, gmix_ref, win_ref, bin_ref, wdw_ref, bdw_ref, gcln_ref, bcln_ref,
                  wco_ref, bco_ref, wpool_ref, spool_ref, wout_ref, gffn_ref,
                  x1_ref, h2_ref, a_ext, a_prev, u_ext, u_prev, c_buf, q_buf, p_buf, *, tiles_per_seq):
    i = pl.program_id(0) % tiles_per_seq
    tm = MIX_TM
    nv = MIX_NV

    @pl.when(i == 0)
    def _():
        a_prev[...] = jnp.zeros(a_prev.shape, F32)
        u_prev[...] = jnp.zeros(u_prev.shape, F32)

    n_col = D_MODEL // LANE
    hn = _rms(x_ref[...], gmix_ref[...])
    for lc in range(n_col):
        for s in range(8):
            p_buf[lc, pl.ds(s, nv, stride=8), :] = hn[s * nv:(s + 1) * nv, lc * LANE:(lc + 1) * LANE]
    h = jnp.concatenate([p_buf[lc] for lc in range(n_col)], axis=-1).astype(BF16)

    def proj(lo, hi):
        return _dot(h, win_ref[:, lo:hi]) + bin_ref[:, lo:hi]

    glu = proj(0, D_CONV) * jax.nn.sigmoid(proj(D_CONV, 2 * D_CONV))
    for lc in range(D_CONV // LANE):
        a_ext[lc, tm:2 * tm, :] = glu[:, lc * LANE:(lc + 1) * LANE]
    u_ext[tm:2 * tm, :] = proj(2 * D_CONV, 2 * D_CONV + D_POOL)

    def delayed_groups(ext, prev, first_group):
        last_row = lax.broadcasted_iota(jnp.int32, (8, ext.shape[-1]), 0) == 7
        for g in range(first_group, nv):
            rows = slice(8 * g, 8 * g + 8)
            mixed = jnp.where(last_row, prev[rows, :], ext[tm + 8 * g:tm + 8 * g + 8, :])
            ext[rows, :] = pltpu.roll(mixed, 1, axis=0)
            prev[rows, :] = ext[tm + 8 * g:tm + 8 * g + 8, :]

    delayed_groups(u_ext, u_prev, nv - (max(POOL_WINDOWS) - 1))

    for r0 in range(0, tm, ROW_CHUNK):
        row = r0 + lax.broadcasted_iota(jnp.int32, (ROW_CHUNK, POOL_GROUP), 0)
        t1 = i * tm + (row % 8) * nv + row // 8 + 1
        for gi, w in enumerate(POOL_WINDOWS):
            ls = slice(gi * POOL_GROUP, (gi + 1) * POOL_GROUP)
            tok = u_ext[tm + r0:tm + r0 + ROW_CHUNK, ls]
            s = tok
            for j in range(1, w):
                s = s + u_ext[tm + r0 - 8 * j:tm + r0 - 8 * j + ROW_CHUNK, ls]
            cnt = jnp.minimum(t1, w).astype(F32)
            q_buf[r0:r0 + ROW_CHUNK, ls] = s / cnt - tok

    qs_out = []
    for gi in range(len(POOL_WINDOWS)):
        ls = slice(gi * POOL_GROUP, (gi + 1) * POOL_GROUP)
        qs_out.append(_dot(q_buf[:, ls].astype(BF16), wpool_ref[gi]) * spool_ref[:, ls])
    branch_b = jnp.concatenate(qs_out, axis=-1)

    def conv_column(lc, carry):
        a_col = a_ext.at[lc]
        delayed_groups(a_col, a_prev.at[lc], nv - (CONV_WIDTH - 1))
        w_col = wdw_ref.at[lc]
        for g0 in range(0, nv, CONV_MG):
            acc = None
            for k in range(CONV_WIDTH):
                src = nv + g0 + k - (CONV_WIDTH - 1)
                term = a_col[8 * src:8 * (src + CONV_MG), :] * w_col[k:k + 1, :]
                acc = term if acc is None else acc + term
            c_buf[lc, 8 * g0:8 * (g0 + CONV_MG), :] = acc + bdw_ref[lc]
        return carry
    lax.fori_loop(0, D_CONV // LANE, conv_column, 0)

    c2 = 2 * D_CONV + D_POOL
    gate_a = jax.nn.sigmoid(proj(c2, c2 + D_MODEL))
    gate_b = jax.nn.sigmoid(proj(c2 + D_MODEL, c2 + 2 * D_MODEL))

    c = jnp.concatenate([c_buf[lc] for lc in range(D_CONV // LANE)], axis=-1)
    mu = jnp.mean(c, axis=-1, keepdims=True)
    xc = c - mu
    var = jnp.mean(xc * xc, axis=-1, keepdims=True)
    y = xc * lax.rsqrt(var + NORM_EPS) * gcln_ref[...] + bcln_ref[...]
    y = y * jax.nn.sigmoid(y)
    branch_a = _dot(y.astype(BF16), wco_ref[...]) + bco_ref[...]

    merged = gate_a * branch_a + gate_b * branch_b
    y = _dot(merged.astype(BF16), wout_ref[...])
    for lc in range(n_col):
        p_buf[lc] = y[:, lc * LANE:(lc + 1) * LANE]
    for lc in range(n_col):
        cols = slice(lc * LANE, (lc + 1) * LANE)
        for s in range(8):
            for j in range(nv // 8):
                rows = slice(s * nv + 8 * j, s * nv + 8 * j + 8)
                x1_ref[rows, cols] = x_ref[rows, cols] + p_buf[lc, pl.ds(8 * (8 * j) + s, 8, stride=8), :]
    h2_ref[...] = _rms(x1_ref[...], gffn_ref[...]).astype(BF16)


def _const_spec(shape):
    n = len(shape)
    return pl.BlockSpec(shape, lambda i, _n=n: (0,) * _n, pipeline_mode=pl.Buffered(1))


def _mixer(x, seq_len, g_mix, w_in, b_in, w_dw, b_dw, g_cln, b_cln, w_co, b_co, w_pool, s_pool, w_out,
           g_ffn):
    t = x.shape[0]
    tm = MIX_TM
    assert seq_len % tm == 0 and MIX_NV >= CONV_WIDTH and MIX_NV >= max(POOL_WINDOWS)
    d_in = w_in.shape[1]
    row = pl.BlockSpec((tm, D_MODEL), lambda i: (i, 0))
    n_col = D_CONV // LANE
    w_dw = w_dw.reshape(CONV_WIDTH, n_col, LANE).transpose(1, 0, 2)
    b_dw = b_dw.reshape(n_col, 1, LANE)
    return pl.pallas_call(
        functools.partial(_mixer_kernel, tiles_per_seq=seq_len // tm),
        grid=(t // tm,),
        in_specs=[
            row,
            _const_spec((1, D_MODEL)),
            _const_spec((D_MODEL, d_in)),
            _const_spec((1, d_in)),
            _const_spec((n_col, CONV_WIDTH, LANE)),
            _const_spec((n_col, 1, LANE)),
            _const_spec((1, D_CONV)),
            _const_spec((1, D_CONV)),
            _const_spec((D_CONV, D_MODEL)),
            _const_spec((1, D_MODEL)),
            _const_spec((len(POOL_WINDOWS), POOL_GROUP, POOL_GROUP)),
            _const_spec((1, D_POOL)),
            _const_spec((D_MODEL, D_MODEL)),
            _const_spec((1, D_MODEL)),
        ],
        out_specs=[row, row],
        out_shape=[jax.ShapeDtypeStruct((t, D_MODEL), F32),
                   jax.ShapeDtypeStruct((t, D_MODEL), BF16)],
        scratch_shapes=[
            pltpu.VMEM((n_col, 2 * tm, LANE), F32),
            pltpu.VMEM((n_col, tm, LANE), F32),
            pltpu.VMEM((2 * tm, D_POOL), F32),
            pltpu.VMEM((tm, D_POOL), F32),
            pltpu.VMEM((n_col, tm, LANE), F32),
            pltpu.VMEM((tm, D_POOL), F32),
            pltpu.VMEM((D_MODEL // LANE, tm, LANE), F32),
        ],
        compiler_params=pltpu.CompilerParams(
            dimension_semantics=("arbitrary",), vmem_limit_bytes=VMEM_LIMIT),
        name="mixer",
    )(x, g_mix, w_in, b_in, w_dw, b_dw, g_cln, b_cln, w_co, b_co, w_pool, s_pool, w_out, g_ffn)


def _beats(v, other, other_is_later):
    v = jnp.broadcast_to(v, other.shape)
    return jnp.where(other_is_later, jnp.where(v >= other, 1, 0), jnp.where(v > other, 1, 0))


def _router_kernel(h2_ref, wrt_ref, br_ref, utri_ref, ltri_ref, gate_ref, rank_ref, pos_ref, cnt_ref):
    tm = ROUTER_TM
    logits = lax.dot_general(wrt_ref[...], h2_ref[...], (((1,), (1,)), ((), ())),
                             preferred_element_type=F32)
    scores = jax.nn.sigmoid(logits)
    sel = scores + br_ref[...]
    shape3 = (N_GROUPS, GROUP_SIZE, tm)
    sel3 = sel.reshape(shape3)
    scores3 = scores.reshape(shape3)
    neg_inf = jnp.float32(-jnp.inf)

    member = lax.broadcasted_iota(jnp.int32, shape3, 1)
    m1 = jnp.max(sel3, axis=1, keepdims=True)
    first = jnp.min(jnp.where(sel3 == m1, member, GROUP_SIZE), axis=1, keepdims=True)
    m2 = jnp.max(jnp.where(member == first, neg_inf, sel3), axis=1, keepdims=True)
    gscore = jnp.broadcast_to(m1 + m2, shape3)

    gidx = lax.broadcasted_iota(jnp.int32, shape3, 0)
    grank = jnp.zeros(shape3, jnp.int32)
    for j in range(N_GROUPS):
        sj = gscore[j:j + 1]
        grank = grank + _beats(sj, gscore, gidx > j)
    masked = jnp.where(grank < TOPK_GROUPS, sel3, neg_inf)

    eidx = gidx * GROUP_SIZE + member
    work = masked
    erank = jnp.full(shape3, TOP_K, jnp.int32)
    for k in range(TOP_K):
        best = jnp.max(jnp.max(work, axis=0, keepdims=True), axis=1, keepdims=True)
        cand = jnp.where(work == best, eidx, N_EXPERTS)
        pick = jnp.min(jnp.min(cand, axis=0, keepdims=True), axis=1, keepdims=True)
        hit = eidx == pick
        work = jnp.where(hit, neg_inf, work)
        erank = jnp.where(hit, k, erank)
    chosen = erank < TOP_K
    top_s = jnp.where(chosen, scores3, 0.0)
    denom = jnp.sum(jnp.sum(top_s, axis=0, keepdims=True), axis=1, keepdims=True)
    gates3 = top_s / denom * ROUTED_SCALE
    chosen2 = jnp.where(chosen, 1.0, 0.0).reshape(N_EXPERTS, tm)
    gate_ref[...] = gates3.reshape(N_EXPERTS, tm).astype(BF16)

    for w in range(tm // WIN):
        ls = slice(w * WIN, (w + 1) * WIN)
        mw = chosen2[:, ls]
        rank = _dot(mw.astype(BF16), utri_ref[...])
        n = jnp.sum(mw, axis=1, keepdims=True)
        run = jnp.floor((n + 7.0) * 0.125) * 8.0
        start = _dot(ltri_ref[...], jnp.broadcast_to(run, (N_EXPERTS, WIN)).astype(BF16))
        rank_ref[:, ls] = jnp.where(mw > 0.5, rank, -1.0).astype(BF16)
        row3 = (rank + start).reshape(N_GROUPS, GROUP_SIZE, WIN)
        er = erank[:, :, ls]
        for k in range(TOP_K):
            pk = jnp.sum(jnp.sum(jnp.where(er == k, row3, 0.0), axis=0, keepdims=True), axis=1, keepdims=True)
            pos_ref[k:k + 1, ls] = pk.reshape(1, WIN).astype(jnp.int32)
        cnt_ref[w] = n


def _router(h2, w_rt, b_r):
    t = h2.shape[0]
    tm = ROUTER_TM
    utri = jnp.triu(jnp.ones((WIN, WIN), BF16), k=1)
    ltri = jnp.tril(jnp.ones((N_EXPERTS, N_EXPERTS), BF16), k=-1)
    return pl.pallas_call(
        _router_kernel,
        grid=(t // tm,),
        in_specs=[
            pl.BlockSpec((tm, D_MODEL), lambda i: (i, 0)),
            _const_spec((N_EXPERTS, D_MODEL)),
            _const_spec((N_EXPERTS, 1)),
            _const_spec((WIN, WIN)),
            _const_spec((N_EXPERTS, N_EXPERTS)),
        ],
        out_specs=[
            pl.BlockSpec((N_EXPERTS, tm), lambda i: (0, i)),
            pl.BlockSpec((N_EXPERTS, tm), lambda i: (0, i)),
            pl.BlockSpec((TOP_K, tm), lambda i: (0, i)),
            pl.BlockSpec((tm // WIN, N_EXPERTS, 1), lambda i: (i, 0, 0)),
        ],
        out_shape=[
            jax.ShapeDtypeStruct((N_EXPERTS, t), BF16),
            jax.ShapeDtypeStruct((N_EXPERTS, t), BF16),
            jax.ShapeDtypeStruct((TOP_K, t), jnp.int32),
            jax.ShapeDtypeStruct((t // WIN, N_EXPERTS, 1), F32),
        ],
        compiler_params=pltpu.CompilerParams(
            dimension_semantics=("arbitrary",), vmem_limit_bytes=VMEM_LIMIT),
        name="router",
    )(h2, w_rt, b_r, utri, ltri)


def _sorted_rows_bound(t):
    rows = t * TOP_K + (t // WIN) * N_EXPERTS * 7 + N_EXPERTS * (EXP_BM - 1)
    blocks = -(-rows // EXP_BM)
    return (blocks + blocks % 2) * EXP_BM


def _dispatch_plan(cnt, t):
    nw = t // WIN
    n = cnt.reshape(nw, N_EXPERTS).astype(jnp.int32)
    run = (n + 7) // 8 * 8
    local_end = jnp.cumsum(run, axis=1)
    local_off = jnp.concatenate([jnp.zeros((nw, 1), jnp.int32), local_end], axis=1)
    total = jnp.sum(run, axis=0)
    region = (total + EXP_BM - 1) // EXP_BM * EXP_BM
    eid = jnp.arange(N_EXPERTS, dtype=jnp.int32)
    last_owner = jnp.max(jnp.where(region > 0, eid, 0))
    odd = (jnp.sum(region) // EXP_BM) % 2
    region = region + jnp.where(eid == last_owner, odd * EXP_BM, 0)
    region_end = jnp.cumsum(region)
    base = region_end - region
    global_off = base[None, :] + jnp.cumsum(run, axis=0) - run
    n_blocks = _sorted_rows_bound(t) // EXP_BM
    n_used = region_end[-1] // EXP_BM
    blk = jnp.arange(n_blocks, dtype=jnp.int32)
    blk_expert = jnp.sum((region_end[None, :] <= blk[:, None] * EXP_BM).astype(jnp.int32), axis=1)
    blk_expert = jnp.minimum(blk_expert, N_EXPERTS - 1)
    later_nonempty = (eid[None, :] > eid[:, None]) & (region[None, :] > 0)
    next_expert = jnp.min(jnp.where(later_nonempty, eid[None, :], N_EXPERTS), axis=1).astype(jnp.int32)
    return dict(
        run_lo=local_off[:, :N_EXPERTS].reshape(nw, N_EXPERTS, 1),
        run_hi=local_off[:, 1:].reshape(nw, N_EXPERTS, 1),
        local_off=local_off.reshape(-1),
        runs=jnp.stack([local_off[:, :N_EXPERTS], run, global_off, jnp.zeros_like(run)], axis=-1).reshape(-1),
        fill_off=base + total, fill_cnt=region - total,
        blk_expert=blk_expert.astype(jnp.int32), next_expert=next_expert,
        n_used=n_used.reshape(1).astype(jnp.int32))


def _run_copy(runs_ref, win, e, vmem_buf, slot, hbm_buf, sem, to_hbm):
    p = (win * N_EXPERTS + e) * RUN_FIELDS
    lo = pl.multiple_of(runs_ref[p], 8)
    cnt = pl.multiple_of(runs_ref[p + 1], 8)
    go = pl.multiple_of(runs_ref[p + 2], 8)
    v = vmem_buf.at[pl.ds(pl.multiple_of(slot * SEL_ROWS + lo, 8), cnt)]
    h = hbm_buf.at[pl.ds(go, cnt)]
    cp = pltpu.make_async_copy(v, h, sem.at[slot]) if to_hbm else pltpu.make_async_copy(h, v, sem.at[slot])
    return cnt, cp


def _start_runs(runs_ref, win, vmem_buf, slot, hbm_buf, sem, to_hbm):
    def body(i, carry):
        copies = [_run_copy(runs_ref, win, i * RUN_UNROLL + j, vmem_buf, slot, hbm_buf, sem, to_hbm)
                  for j in range(RUN_UNROLL)]
        for cnt, cp in copies:
            @pl.when(cnt > 0)
            def _(cp=cp):
                cp.start()
        return carry
    lax.fori_loop(0, N_EXPERTS // RUN_UNROLL, body, 0)


def _wait_runs(local_ref, win, vmem_buf, slot, hbm_buf, sem, to_hbm):
    total = pl.multiple_of(local_ref[win * (N_EXPERTS + 1) + N_EXPERTS], 8)
    v = vmem_buf.at[pl.ds(pl.multiple_of(slot * SEL_ROWS, 8), total)]
    h = hbm_buf.at[pl.ds(0, total)]
    cp = pltpu.make_async_copy(v, h, sem.at[slot]) if to_hbm else pltpu.make_async_copy(h, v, sem.at[slot])

    @pl.when(total > 0)
    def _():
        cp.wait()


def _dispatch_kernel(local_ref, runs_ref, fill_off_ref, fill_cnt_ref, h2_ref, pos_ref, wsg_ref, wsu_ref,
                     xs_hbm, sbuf, hs_ref, s_ref, sem, zsem, *, n_win):
    w = pl.program_id(0)
    slot = w % 2
    pos = pos_ref[...]

    h2 = h2_ref[...]
    assert SEL_RG <= 256
    rid_b = lax.broadcasted_iota(jnp.int32, (SEL_RG, WIN), 0).astype(F32).astype(BF16)
    one_b = jnp.ones((SEL_RG, WIN), BF16)
    def compact(first, n_rows):
        for r0 in range(first, first + n_rows, SEL_RG):
            acc = jnp.zeros((SEL_RG, WIN), BF16)
            for k in range(TOP_K):
                off = (pos[k:k + 1, :] - r0).astype(F32)
                off = jnp.broadcast_to(off, (SEL_RG, WIN)).astype(BF16)
                acc = jnp.where(rid_b == off, one_b, acc)
            s_ref[r0:r0 + SEL_RG, :] = acc
        dst = pl.multiple_of(slot * SEL_ROWS + first, SEL_TAIL)
        sbuf[pl.ds(dst, n_rows), :] = _dot(s_ref[first:first + n_rows, :], h2).astype(BF16)

    @pl.when(w == 0)
    def _():
        for s in range(2):
            sbuf[(s + 1) * SEL_ROWS - SEL_TAIL:(s + 1) * SEL_ROWS, :] = jnp.zeros((SEL_TAIL, D_MODEL), BF16)

    for first in range(0, SEL_ROWS - 2 * SEL_TAIL, SEL_MM):
        compact(first, SEL_MM)
    hs = _dot(h2, wsg_ref[...])
    hs_ref[...] = (hs * jax.nn.sigmoid(hs) * _dot(h2, wsu_ref[...])).astype(BF16)
    compact(SEL_ROWS - 2 * SEL_TAIL, SEL_TAIL)

    @pl.when(local_ref[w * (N_EXPERTS + 1) + N_EXPERTS] > SEL_ROWS - SEL_TAIL)
    def _():
        compact(SEL_ROWS - SEL_TAIL, SEL_TAIL)

    _start_runs(runs_ref, w, sbuf, slot, xs_hbm, sem, True)

    @pl.when(w > 0)
    def _():
        _wait_runs(local_ref, w - 1, sbuf, 1 - slot, xs_hbm, sem, True)

    @pl.when(w == n_win - 1)
    def _():
        sbuf[2 * SEL_ROWS:, :] = jnp.zeros((2 * EXP_BM, D_MODEL), BF16)

        def fill(e, wait):
            cnt = pl.multiple_of(fill_cnt_ref[e], 8)
            off = pl.multiple_of(fill_off_ref[e], 8)
            cp = pltpu.make_async_copy(sbuf.at[pl.ds(2 * SEL_ROWS, cnt)], xs_hbm.at[pl.ds(off, cnt)], zsem)

            @pl.when(cnt > 0)
            def _():
                if wait:
                    cp.wait()
                else:
                    cp.start()

        def start_body(e, carry):
            fill(e, False)
            return carry

        def wait_body(e, carry):
            fill(e, True)
            return carry
        lax.fori_loop(0, N_EXPERTS, start_body, 0)
        _wait_runs(local_ref, w, sbuf, slot, xs_hbm, sem, True)
        lax.fori_loop(0, N_EXPERTS, wait_body, 0)


def _staging_shape(extra_rows):
    return jax.ShapeDtypeStruct((2 * SEL_ROWS + extra_rows, D_MODEL), BF16)


def _staging_spec(extra_rows):
    return pl.BlockSpec((2 * SEL_ROWS + extra_rows, D_MODEL), lambda w, *_: (0, 0))


def _dispatch(plan, h2, pos, wsg, wsu):
    t = h2.shape[0]
    n_win = t // WIN
    xs, _, hs = pl.pallas_call(
        functools.partial(_dispatch_kernel, n_win=n_win),
        grid_spec=pltpu.PrefetchScalarGridSpec(
            num_scalar_prefetch=4,
            grid=(n_win,),
            in_specs=[
                pl.BlockSpec((WIN, D_MODEL), lambda w, *_: (w, 0)),
                pl.BlockSpec((TOP_K, WIN), lambda w, *_: (0, w)),
                pl.BlockSpec((D_MODEL, D_EXPERT), lambda w, *_: (0, 0)),
                pl.BlockSpec((D_MODEL, D_EXPERT), lambda w, *_: (0, 0)),
            ],
            out_specs=[pl.BlockSpec(memory_space=pl.ANY), _staging_spec(2 * EXP_BM),
                       pl.BlockSpec((WIN, D_EXPERT), lambda w, *_: (w, 0))],
            scratch_shapes=[
                pltpu.VMEM((SEL_ROWS, WIN), BF16),
                pltpu.SemaphoreType.DMA((2,)),
                pltpu.SemaphoreType.DMA,
            ]),
        out_shape=[jax.ShapeDtypeStruct((_sorted_rows_bound(t), D_MODEL), BF16), _staging_shape(2 * EXP_BM),
                   jax.ShapeDtypeStruct((t, D_EXPERT), BF16)],
        compiler_params=pltpu.CompilerParams(
            dimension_semantics=("arbitrary",), vmem_limit_bytes=VMEM_LIMIT),
        name="dispatch",
    )(plan['local_off'], plan['runs'], plan['fill_off'], plan['fill_cnt'], h2, pos, wsg, wsu)
    return xs, hs


def _expert_kernel(blk_expert_ref, next_expert_ref, n_used_ref, xs_hbm, wg_hbm, wu_hbm, wd_hbm, ys_hbm,
                   xbuf, ybuf, wg_st, wu_st, wd_st, wg_bf, wu_bf, wd_bf, xsem, ysem, wsem):
    n_used = n_used_ref[0]
    part = EXP_BM // EXP_SPLIT

    def row_copies(b, slot, fetch):
        out = []
        for q in range(EXP_SPLIT):
            hbm_rows = pl.ds(pl.multiple_of(b * EXP_BM + q * part, part), part)
            if fetch:
                out.append(pltpu.make_async_copy(xs_hbm.at[hbm_rows], xbuf.at[slot, q * part:(q + 1) * part],
                                                 xsem.at[slot]))
            else:
                out.append(pltpu.make_async_copy(ybuf.at[slot, q * part:(q + 1) * part], ys_hbm.at[hbm_rows],
                                                 ysem.at[slot]))
        return out

    def weight_copies(e, slot):
        return [pltpu.make_async_copy(wg_hbm.at[e], wg_st.at[slot], wsem.at[slot]),
                pltpu.make_async_copy(wu_hbm.at[e], wu_st.at[slot], wsem.at[slot]),
                pltpu.make_async_copy(wd_hbm.at[e], wd_st.at[slot], wsem.at[slot])]

    def start(copies):
        for c in copies:
            c.start()

    def wait(copies):
        for c in copies:
            c.wait()

    for ahead in range(EXP_XDEPTH - 2):
        @pl.when(ahead < n_used)
        def _(ahead=ahead):
            start(row_copies(ahead, ahead, True))

    @pl.when(n_used > 0)
    def _():
        start(weight_copies(blk_expert_ref[0], 0))

    def enter_block(b, wset):
        e = blk_expert_ref[b]
        new_expert = jnp.logical_or(b == 0, e != blk_expert_ref[jnp.maximum(b - 1, 0)])
        wset = jnp.where(new_expert, 1 - wset, wset)

        @pl.when(new_expert)
        def _():
            wait(weight_copies(e, wset))
            wg_bf[wset] = wg_st[wset].astype(BF16)
            wu_bf[wset] = wu_st[wset].astype(BF16)
            wd_bf[wset] = wd_st[wset].astype(BF16)
            nxt = next_expert_ref[e]

            @pl.when(nxt < N_EXPERTS)
            def _():
                start(weight_copies(nxt, 1 - wset))

        ahead = b + EXP_XDEPTH - 2

        @pl.when(ahead < n_used)
        def _():
            start(row_copies(ahead, ahead % EXP_XDEPTH, True))

        wait(row_copies(b, b % EXP_XDEPTH, True))

        @pl.when(b >= EXP_YDEPTH)
        def _():
            wait(row_copies(b - EXP_YDEPTH, b % EXP_YDEPTH, False))
        return wset

    def compute(b, wset):
        x = xbuf[b % EXP_XDEPTH]
        hg = _dot(x, wg_bf[wset])
        hb = hg * jax.nn.sigmoid(hg) * _dot(x, wu_bf[wset])
        ybuf[b % EXP_YDEPTH] = _dot(hb.astype(BF16), wd_bf[wset]).astype(BF16)

    def body(p, wset):
        b0 = 2 * p
        w0 = enter_block(b0, wset)
        w1 = enter_block(b0 + 1, w0)
        compute(b0, w0)
        compute(b0 + 1, w1)
        start(row_copies(b0, b0 % EXP_YDEPTH, False))
        start(row_copies(b0 + 1, (b0 + 1) % EXP_YDEPTH, False))
        return w1

    lax.fori_loop(0, n_used // 2, body, jnp.int32(1))

    for back in range(EXP_YDEPTH, 0, -1):
        @pl.when(n_used >= back)
        def _(back=back):
            wait(row_copies(n_used - back, (n_used - back) % EXP_YDEPTH, False))


def _experts(plan, xs, w_gate, w_up, w_down):
    any_spec = pl.BlockSpec(memory_space=pl.ANY)
    return pl.pallas_call(
        _expert_kernel,
        grid_spec=pltpu.PrefetchScalarGridSpec(
            num_scalar_prefetch=3,
            grid=(1,),
            in_specs=[any_spec, any_spec, any_spec, any_spec],
            out_specs=any_spec,
            scratch_shapes=[
                pltpu.VMEM((EXP_XDEPTH, EXP_BM, D_MODEL), BF16),
                pltpu.VMEM((EXP_YDEPTH, EXP_BM, D_MODEL), BF16),
                pltpu.VMEM((2, D_MODEL, D_EXPERT), F32),
                pltpu.VMEM((2, D_MODEL, D_EXPERT), F32),
                pltpu.VMEM((2, D_EXPERT, D_MODEL), F32),
                pltpu.VMEM((2, D_MODEL, D_EXPERT), BF16),
                pltpu.VMEM((2, D_MODEL, D_EXPERT), BF16),
                pltpu.VMEM((2, D_EXPERT, D_MODEL), BF16),
                pltpu.SemaphoreType.DMA((EXP_XDEPTH,)),
                pltpu.SemaphoreType.DMA((EXP_YDEPTH,)),
                pltpu.SemaphoreType.DMA((2,)),
            ]),
        out_shape=jax.ShapeDtypeStruct(xs.shape, BF16),
        compiler_params=pltpu.CompilerParams(
            dimension_semantics=("arbitrary",), vmem_limit_bytes=VMEM_LIMIT),
        name="experts",
    )(plan['blk_expert'], plan['next_expert'], plan['n_used'], xs, w_gate, w_up, w_down)


def _combine_kernel(local_ref, runs_ref, x1_ref, hs_ref, p_ref, rank_ref, gate_ref, lo_ref, hi_ref,
                    wsd_ref, gple_ref, wpg_ref, wp_ref, gfin_ref, ys_hbm, o_ref, ybuf, st_ref, sem,
                    *, n_win, final_norm):
    w = pl.program_id(0)
    slot = w % 2

    @pl.when(w == 0)
    def _():
        ybuf[...] = jnp.zeros(ybuf.shape, BF16)
        _start_runs(runs_ref, w, ybuf, slot, ys_hbm, sem, False)

    @pl.when(w + 1 < n_win)
    def _():
        _start_runs(runs_ref, w + 1, ybuf, 1 - slot, ys_hbm, sem, False)

    lo = lo_ref[0]
    hi = hi_ref[0]
    lo_f = lo.astype(F32)
    rank_tbl = rank_ref[...]
    gate_tbl = gate_ref[...]

    def build_group(lg):
        cols = slice(lg * CMB_LG, (lg + 1) * CMB_LG)
        rid = lg * CMB_LG + lax.broadcasted_iota(jnp.int32, (N_EXPERTS, CMB_LG), 1)
        owner = jnp.where(rid >= lo, jnp.where(rid < hi, 1.0, 0.0), 0.0)
        run_row = rid[0:1, :].astype(F32) - jnp.sum(owner * lo_f, axis=0, keepdims=True)
        owner = owner.astype(BF16)
        hit = _dot_t(rank_tbl, owner) == run_row
        st_ref[:, cols] = jnp.where(hit, _dot_t(gate_tbl, owner), 0.0).astype(BF16)

    build_group(0)
    shared = _dot(hs_ref[...], wsd_ref[...])

    _wait_runs(local_ref, w, ybuf, slot, ys_hbm, sem, False)
    routed = None
    n_groups = SEL_ROWS // CMB_LG
    for lg in range(n_groups):
        if lg + 1 < n_groups:
            build_group(lg + 1)
        src = pl.multiple_of(slot * SEL_ROWS + lg * CMB_LG, CMB_LG)
        part = _dot(st_ref[:, lg * CMB_LG:(lg + 1) * CMB_LG], ybuf[pl.ds(src, CMB_LG), :])
        routed = part if routed is None else routed + part
    x2 = x1_ref[...] + routed + shared

    hp = _rms(x2, gple_ref[...]).astype(BF16)
    gate = jax.nn.sigmoid(_dot(hp, wpg_ref[...]))
    x3 = x2 + gate * _dot(p_ref[...].astype(BF16), wp_ref[...])
    o_ref[...] = _rms(x3, gfin_ref[...]) if final_norm else x3


def _combine(plan, ys, x1, hs, p, rank_tbl, gate_tbl, wsd, g_ple, w_pg, w_p, g_fin, final_norm):
    t = x1.shape[0]
    n_win = t // WIN
    row = lambda width: pl.BlockSpec((WIN, width), lambda w, *_: (w, 0))
    const = lambda shape: pl.BlockSpec(shape, lambda w, *_: (0,) * len(shape))
    return pl.pallas_call(
        functools.partial(_combine_kernel, n_win=n_win, final_norm=final_norm),
        grid_spec=pltpu.PrefetchScalarGridSpec(
            num_scalar_prefetch=2,
            grid=(n_win,),
            in_specs=[
                row(D_MODEL), row(D_EXPERT), row(PLE_DIM),
                pl.BlockSpec((N_EXPERTS, WIN), lambda w, *_: (0, w)),
                pl.BlockSpec((N_EXPERTS, WIN), lambda w, *_: (0, w)),
                pl.BlockSpec((1, N_EXPERTS, 1), lambda w, *_: (w, 0, 0)),
                pl.BlockSpec((1, N_EXPERTS, 1), lambda w, *_: (w, 0, 0)),
                const((D_EXPERT, D_MODEL)),
                const((1, D_MODEL)), const((D_MODEL, D_MODEL)), const((PLE_DIM, D_MODEL)),
                const((1, D_MODEL)),
                pl.BlockSpec(memory_space=pl.ANY),
            ],
            out_specs=[row(D_MODEL), _staging_spec(0)],
            scratch_shapes=[
                pltpu.VMEM((WIN, SEL_ROWS), BF16),
                pltpu.SemaphoreType.DMA((2,)),
            ]),
        out_shape=[jax.ShapeDtypeStruct((t, D_MODEL), F32), _staging_shape(0)],
        compiler_params=pltpu.CompilerParams(
            dimension_semantics=("arbitrary",), vmem_limit_bytes=VMEM_LIMIT),
        name="combine",
    )(plan['local_off'], plan['runs'], x1, hs, p, rank_tbl, gate_tbl, plan['run_lo'], plan['run_hi'],
      wsd, g_ple, w_pg, w_p, g_fin, ys)[0]


def kernel(x, p, g_mix, w_in, b_in, w_dw, b_dw, g_cln, b_cln, w_conv_out, b_conv_out, w_pool, s_pool,
           w_out, g_ffn, w_router, b_router, w_e_gate, w_e_up, w_e_down, w_s_gate, w_s_up, w_s_down,
           g_ple, w_ple_gate, w_ple, g_final):
    bsz, s, d = x.shape
    t = bsz * s
    depth = w_in.shape[0]
    xt = x.reshape(t, d)
    row = lambda v: v.reshape(1, -1)
    for i in range(depth):
        x1, h2 = _mixer(
            xt, s, row(g_mix[i]), w_in[i].astype(BF16), row(b_in[i]), w_dw[i], row(b_dw[i]),
            row(g_cln[i]), row(b_cln[i]), w_conv_out[i].astype(BF16), row(b_conv_out[i]),
            w_pool[i].astype(BF16), row(s_pool[i]), w_out[i].astype(BF16), row(g_ffn[i]))
        gate, rank, pos, cnt = _router(h2, w_router[i].T.astype(BF16), b_router[i].reshape(N_EXPERTS, 1))
        plan = _dispatch_plan(cnt, t)
        xs, hs = _dispatch(plan, h2, pos, w_s_gate[i].astype(BF16), w_s_up[i].astype(BF16))
        ys = _experts(plan, xs, w_e_gate[i], w_e_up[i], w_e_down[i])
        xt = _combine(
            plan, ys, x1, hs, p[i].reshape(t, PLE_DIM), rank, gate, w_s_down[i].astype(BF16),
            row(g_ple[i]), w_ple_gate[i].astype(BF16), w_ple[i].astype(BF16), row(g_final),
            final_norm=(i == depth - 1))
    return xt.reshape(bsz, s, d)
```

```python
import functools

import jax
import jax.numpy as jnp
from jax import lax
from jax.experimental import pallas as pl
from jax.experimental.pallas import tpu as pltpu

D_MODEL = 1024
D_CONV = 1024
D_POOL = 1024
CONV_WIDTH = 31
POOL_WINDOWS = (2, 4, 8, 16)
POOL_GROUP = 256
PLE_DIM = 256
N_EXPERTS = 64
N_GROUPS = 8
GROUP_SIZE = N_EXPERTS // N_GROUPS
TOPK_GROUPS = 4
TOP_K = 8
D_EXPERT = 256
ROUTED_SCALE = 2.5
NORM_EPS = 1e-6

F32 = jnp.float32
BF16 = jnp.bfloat16

MIX_TM = 512
MIX_NV = MIX_TM // 8
CONV_MG = 8
ROW_CHUNK = 64
LANE = 128

ROUTER_TM = 1024
WIN = 256
SEL_ROWS = 2560
SEL_RG = 128
SEL_MM = 512
SEL_TAIL = 256
EXP_BM = 576
EXP_XDEPTH = 6
EXP_YDEPTH = 4
EXP_SPLIT = 4
RUN_FIELDS = 4
RUN_UNROLL = 8
CMB_LG = 512

V7X_VMEM_BYTES = 64 * 1024 * 1024
VMEM_LIMIT = V7X_VMEM_BYTES - 8 * 1024 * 1024

assert SEL_ROWS >= TOP_K * WIN + 7 * N_EXPERTS
assert (SEL_ROWS - 2 * SEL_TAIL) % SEL_MM == 0 and SEL_TAIL % SEL_RG == 0 and SEL_ROWS % CMB_LG == 0
assert N_EXPERTS % RUN_UNROLL == 0 and EXP_BM % (16 * EXP_SPLIT) == 0 and EXP_XDEPTH > 2 and EXP_YDEPTH >= 2


def _rms(x, g):
    ms = jnp.mean(x * x, axis=-1, keepdims=True)
    return x * lax.rsqrt(ms + NORM_EPS) * g


def _dot(a, b):
    return jnp.dot(a, b, preferred_element_type=F32)


def _dot_t(a, b):
    return lax.dot_general(a, b, (((0,), (0,)), ((), ())), preferred_element_type=F32)


def _mixer_kernel(x_ref, gmix_ref, win_ref, bin_ref, wdw_ref, bdw_ref, gcln_ref, bcln_ref,
                  wco_ref, bco_ref, wpool_ref, spool_ref, wout_ref, gffn_ref,
                  x1_ref, h2_ref, a_ext, a_prev, u_ext, u_prev, c_buf, q_buf, p_buf, *, tiles_per_seq):
    i = pl.program_id(0) % tiles_per_seq
    tm = MIX_TM
    nv = MIX_NV

    @pl.when(i == 0)
    def _():
        a_prev[...] = jnp.zeros(a_prev.shape, F32)
        u_prev[...] = jnp.zeros(u_prev.shape, F32)

    n_col = D_MODEL // LANE
    hn = _rms(x_ref[...], gmix_ref[...])
    for lc in range(n_col):
        for s in range(8):
            p_buf[lc, pl.ds(s, nv, stride=8), :] = hn[s * nv:(s + 1) * nv, lc * LANE:(lc + 1) * LANE]
    h = jnp.concatenate([p_buf[lc] for lc in range(n_col)], axis=-1).astype(BF16)

    def proj(lo, hi):
        return _dot(h, win_ref[:, lo:hi]) + bin_ref[:, lo:hi]

    glu = proj(0, D_CONV) * jax.nn.sigmoid(proj(D_CONV, 2 * D_CONV))
    for lc in range(D_CONV // LANE):
        a_ext[lc, tm:2 * tm, :] = glu[:, lc * LANE:(lc + 1) * LANE]
    u_ext[tm:2 * tm, :] = proj(2 * D_CONV, 2 * D_CONV + D_POOL)

    def delayed_groups(ext, prev, first_group):
        last_row = lax.broadcasted_iota(jnp.int32, (8, ext.shape[-1]), 0) == 7
        for g in range(first_group, nv):
            rows = slice(8 * g, 8 * g + 8)
            mixed = jnp.where(last_row, prev[rows, :], ext[tm + 8 * g:tm + 8 * g + 8, :])
            ext[rows, :] = pltpu.roll(mixed, 1, axis=0)
            prev[rows, :] = ext[tm + 8 * g:tm + 8 * g + 8, :]

    delayed_groups(u_ext, u_prev, nv - (max(POOL_WINDOWS) - 1))

    for r0 in range(0, tm, ROW_CHUNK):
        row = r0 + lax.broadcasted_iota(jnp.int32, (ROW_CHUNK, POOL_GROUP), 0)
        t1 = i * tm + (row % 8) * nv + row // 8 + 1
        for gi, w in enumerate(POOL_WINDOWS):
            ls = slice(gi * POOL_GROUP, (gi + 1) * POOL_GROUP)
            tok = u_ext[tm + r0:tm + r0 + ROW_CHUNK, ls]
            s = tok
            for j in range(1, w):
                s = s + u_ext[tm + r0 - 8 * j:tm + r0 - 8 * j + ROW_CHUNK, ls]
            cnt = jnp.minimum(t1, w).astype(F32)
            q_buf[r0:r0 + ROW_CHUNK, ls] = s / cnt - tok

    qs_out = []
    for gi in range(len(POOL_WINDOWS)):
        ls = slice(gi * POOL_GROUP, (gi + 1) * POOL_GROUP)
        qs_out.append(_dot(q_buf[:, ls].astype(BF16), wpool_ref[gi]) * spool_ref[:, ls])
    branch_b = jnp.concatenate(qs_out, axis=-1)

    def conv_column(lc, carry):
        a_col = a_ext.at[lc]
        delayed_groups(a_col, a_prev.at[lc], nv - (CONV_WIDTH - 1))
        w_col = wdw_ref.at[lc]
        for g0 in range(0, nv, CONV_MG):
            acc = None
            for k in range(CONV_WIDTH):
                src = nv + g0 + k - (CONV_WIDTH - 1)
                term = a_col[8 * src:8 * (src + CONV_MG), :] * w_col[k:k + 1, :]
                acc = term if acc is None else acc + term
            c_buf[lc, 8 * g0:8 * (g0 + CONV_MG), :] = acc + bdw_ref[lc]
        return carry
    lax.fori_loop(0, D_CONV // LANE, conv_column, 0)

    c2 = 2 * D_CONV + D_POOL
    gate_a = jax.nn.sigmoid(proj(c2, c2 + D_MODEL))
    gate_b = jax.nn.sigmoid(proj(c2 + D_MODEL, c2 + 2 * D_MODEL))

    c = jnp.concatenate([c_buf[lc] for lc in range(D_CONV // LANE)], axis=-1)
    mu = jnp.mean(c, axis=-1, keepdims=True)
    xc = c - mu
    var = jnp.mean(xc * xc, axis=-1, keepdims=True)
    y = xc * lax.rsqrt(var + NORM_EPS) * gcln_ref[...] + bcln_ref[...]
    y = y * jax.nn.sigmoid(y)
    branch_a = _dot(y.astype(BF16), wco_ref[...]) + bco_ref[...]

    merged = gate_a * branch_a + gate_b * branch_b
    y = _dot(merged.astype(BF16), wout_ref[...])
    for lc in range(n_col):
        p_buf[lc] = y[:, lc * LANE:(lc + 1) * LANE]
    for lc in range(n_col):
        cols = slice(lc * LANE, (lc + 1) * LANE)
        for s in range(8):
            for j in range(nv // 8):
                rows = slice(s * nv + 8 * j, s * nv + 8 * j + 8)
                x1_ref[rows, cols] = x_ref[rows, cols] + p_buf[lc, pl.ds(8 * (8 * j) + s, 8, stride=8), :]
    h2_ref[...] = _rms(x1_ref[...], gffn_ref[...]).astype(BF16)


def _const_spec(shape):
    n = len(shape)
    return pl.BlockSpec(shape, lambda i, _n=n: (0,) * _n, pipeline_mode=pl.Buffered(1))


def _mixer(x, seq_len, g_mix, w_in, b_in, w_dw, b_dw, g_cln, b_cln, w_co, b_co, w_pool, s_pool, w_out,
           g_ffn):
    t = x.shape[0]
    tm = MIX_TM
    assert seq_len % tm == 0 and MIX_NV >= CONV_WIDTH and MIX_NV >= max(POOL_WINDOWS)
    d_in = w_in.shape[1]
    row = pl.BlockSpec((tm, D_MODEL), lambda i: (i, 0))
    n_col = D_CONV // LANE
    w_dw = w_dw.reshape(CONV_WIDTH, n_col, LANE).transpose(1, 0, 2)
    b_dw = b_dw.reshape(n_col, 1, LANE)
    return pl.pallas_call(
        functools.partial(_mixer_kernel, tiles_per_seq=seq_len // tm),
        grid=(t // tm,),
        in_specs=[
            row,
            _const_spec((1, D_MODEL)),
            _const_spec((D_MODEL, d_in)),
            _const_spec((1, d_in)),
            _const_spec((n_col, CONV_WIDTH, LANE)),
            _const_spec((n_col, 1, LANE)),
            _const_spec((1, D_CONV)),
            _const_spec((1, D_CONV)),
            _const_spec((D_CONV, D_MODEL)),
            _const_spec((1, D_MODEL)),
            _const_spec((len(POOL_WINDOWS), POOL_GROUP, POOL_GROUP)),
            _const_spec((1, D_POOL)),
            _const_spec((D_MODEL, D_MODEL)),
            _const_spec((1, D_MODEL)),
        ],
        out_specs=[row, row],
        out_shape=[jax.ShapeDtypeStruct((t, D_MODEL), F32),
                   jax.ShapeDtypeStruct((t, D_MODEL), BF16)],
        scratch_shapes=[
            pltpu.VMEM((n_col, 2 * tm, LANE), F32),
            pltpu.VMEM((n_col, tm, LANE), F32),
            pltpu.VMEM((2 * tm, D_POOL), F32),
            pltpu.VMEM((tm, D_POOL), F32),
            pltpu.VMEM((n_col, tm, LANE), F32),
            pltpu.VMEM((tm, D_POOL), F32),
            pltpu.VMEM((D_MODEL // LANE, tm, LANE), F32),
        ],
        compiler_params=pltpu.CompilerParams(
            dimension_semantics=("arbitrary",), vmem_limit_bytes=VMEM_LIMIT),
        name="mixer",
    )(x, g_mix, w_in, b_in, w_dw, b_dw, g_cln, b_cln, w_co, b_co, w_pool, s_pool, w_out, g_ffn)


def _beats(v, other, other_is_later):
    v = jnp.broadcast_to(v, other.shape)
    return jnp.where(other_is_later, jnp.where(v >= other, 1, 0), jnp.where(v > other, 1, 0))


def _router_kernel(h2_ref, wrt_ref, br_ref, utri_ref, ltri_ref, gate_ref, rank_ref, pos_ref, cnt_ref):
    tm = ROUTER_TM
    logits = lax.dot_general(wrt_ref[...], h2_ref[...], (((1,), (1,)), ((), ())),
                             preferred_element_type=F32)
    scores = jax.nn.sigmoid(logits)
    sel = scores + br_ref[...]
    shape3 = (N_GROUPS, GROUP_SIZE, tm)
    sel3 = sel.reshape(shape3)
    scores3 = scores.reshape(shape3)
    neg_inf = jnp.float32(-jnp.inf)

    member = lax.broadcasted_iota(jnp.int32, shape3, 1)
    m1 = jnp.max(sel3, axis=1, keepdims=True)
    first = jnp.min(jnp.where(sel3 == m1, member, GROUP_SIZE), axis=1, keepdims=True)
    m2 = jnp.max(jnp.where(member == first, neg_inf, sel3), axis=1, keepdims=True)
    gscore = jnp.broadcast_to(m1 + m2, shape3)

    gidx = lax.broadcasted_iota(jnp.int32, shape3, 0)
    grank = jnp.zeros(shape3, jnp.int32)
    for j in range(N_GROUPS):
        sj = gscore[j:j + 1]
        grank = grank + _beats(sj, gscore, gidx > j)
    masked = jnp.where(grank < TOPK_GROUPS, sel3, neg_inf)

    eidx = gidx * GROUP_SIZE + member
    work = masked
    erank = jnp.full(shape3, TOP_K, jnp.int32)
    for k in range(TOP_K):
        best = jnp.max(jnp.max(work, axis=0, keepdims=True), axis=1, keepdims=True)
        cand = jnp.where(work == best, eidx, N_EXPERTS)
        pick = jnp.min(jnp.min(cand, axis=0, keepdims=True), axis=1, keepdims=True)
        hit = eidx == pick
        work = jnp.where(hit, neg_inf, work)
        erank = jnp.where(hit, k, erank)
    chosen = erank < TOP_K
    top_s = jnp.where(chosen, scores3, 0.0)
    denom = jnp.sum(jnp.sum(top_s, axis=0, keepdims=True), axis=1, keepdims=True)
    gates3 = top_s / denom * ROUTED_SCALE
    chosen2 = jnp.where(chosen, 1.0, 0.0).reshape(N_EXPERTS, tm)
    gate_ref[...] = gates3.reshape(N_EXPERTS, tm).astype(BF16)

    for w in range(tm // WIN):
        ls = slice(w * WIN, (w + 1) * WIN)
        mw = chosen2[:, ls]
        rank = _dot(mw.astype(BF16), utri_ref[...])
        n = jnp.sum(mw, axis=1, keepdims=True)
        run = jnp.floor((n + 7.0) * 0.125) * 8.0
        start = _dot(ltri_ref[...], jnp.broadcast_to(run, (N_EXPERTS, WIN)).astype(BF16))
        rank_ref[:, ls] = jnp.where(mw > 0.5, rank, -1.0).astype(BF16)
        row3 = (rank + start).reshape(N_GROUPS, GROUP_SIZE, WIN)
        er = erank[:, :, ls]
        for k in range(TOP_K):
            pk = jnp.sum(jnp.sum(jnp.where(er == k, row3, 0.0), axis=0, keepdims=True), axis=1, keepdims=True)
            pos_ref[k:k + 1, ls] = pk.reshape(1, WIN).astype(jnp.int32)
        cnt_ref[w] = n


def _router(h2, w_rt, b_r):
    t = h2.shape[0]
    tm = ROUTER_TM
    utri = jnp.triu(jnp.ones((WIN, WIN), BF16), k=1)
    ltri = jnp.tril(jnp.ones((N_EXPERTS, N_EXPERTS), BF16), k=-1)
    return pl.pallas_call(
        _router_kernel,
        grid=(t // tm,),
        in_specs=[
            pl.BlockSpec((tm, D_MODEL), lambda i: (i, 0)),
            _const_spec((N_EXPERTS, D_MODEL)),
            _const_spec((N_EXPERTS, 1)),
            _const_spec((WIN, WIN)),
            _const_spec((N_EXPERTS, N_EXPERTS)),
        ],
        out_specs=[
            pl.BlockSpec((N_EXPERTS, tm), lambda i: (0, i)),
            pl.BlockSpec((N_EXPERTS, tm), lambda i: (0, i)),
            pl.BlockSpec((TOP_K, tm), lambda i: (0, i)),
            pl.BlockSpec((tm // WIN, N_EXPERTS, 1), lambda i: (i, 0, 0)),
        ],
        out_shape=[
            jax.ShapeDtypeStruct((N_EXPERTS, t), BF16),
            jax.ShapeDtypeStruct((N_EXPERTS, t), BF16),
            jax.ShapeDtypeStruct((TOP_K, t), jnp.int32),
            jax.ShapeDtypeStruct((t // WIN, N_EXPERTS, 1), F32),
        ],
        compiler_params=pltpu.CompilerParams(
            dimension_semantics=("arbitrary",), vmem_limit_bytes=VMEM_LIMIT),
        name="router",
    )(h2, w_rt, b_r, utri, ltri)


def _sorted_rows_bound(t):
    rows = t * TOP_K + (t // WIN) * N_EXPERTS * 7 + N_EXPERTS * (EXP_BM - 1)
    blocks = -(-rows // EXP_BM)
    return (blocks + blocks % 2) * EXP_BM


def _dispatch_plan(cnt, t):
    nw = t // WIN
    n = cnt.reshape(nw, N_EXPERTS).astype(jnp.int32)
    run = (n + 7) // 8 * 8
    local_end = jnp.cumsum(run, axis=1)
    local_off = jnp.concatenate([jnp.zeros((nw, 1), jnp.int32), local_end], axis=1)
    total = jnp.sum(run, axis=0)
    region = (total + EXP_BM - 1) // EXP_BM * EXP_BM
    eid = jnp.arange(N_EXPERTS, dtype=jnp.int32)
    last_owner = jnp.max(jnp.where(region > 0, eid, 0))
    odd = (jnp.sum(region) // EXP_BM) % 2
    region = region + jnp.where(eid == last_owner, odd * EXP_BM, 0)
    region_end = jnp.cumsum(region)
    base = region_end - region
    global_off = base[None, :] + jnp.cumsum(run, axis=0) - run
    n_blocks = _sorted_rows_bound(t) // EXP_BM
    n_used = region_end[-1] // EXP_BM
    blk = jnp.arange(n_blocks, dtype=jnp.int32)
    blk_expert = jnp.sum((region_end[None, :] <= blk[:, None] * EXP_BM).astype(jnp.int32), axis=1)
    blk_expert = jnp.minimum(blk_expert, N_EXPERTS - 1)
    later_nonempty = (eid[None, :] > eid[:, None]) & (region[None, :] > 0)
    next_expert = jnp.min(jnp.where(later_nonempty, eid[None, :], N_EXPERTS), axis=1).astype(jnp.int32)
    return dict(
        run_lo=local_off[:, :N_EXPERTS].reshape(nw, N_EXPERTS, 1),
        run_hi=local_off[:, 1:].reshape(nw, N_EXPERTS, 1),
        local_off=local_off.reshape(-1),
        runs=jnp.stack([local_off[:, :N_EXPERTS], run, global_off, jnp.zeros_like(run)], axis=-1).reshape(-1),
        fill_off=base + total, fill_cnt=region - total,
        blk_expert=blk_expert.astype(jnp.int32), next_expert=next_expert,
        n_used=n_used.reshape(1).astype(jnp.int32))


def _run_copy(runs_ref, win, e, vmem_buf, slot, hbm_buf, sem, to_hbm):
    p = (win * N_EXPERTS + e) * RUN_FIELDS
    lo = pl.multiple_of(runs_ref[p], 8)
    cnt = pl.multiple_of(runs_ref[p + 1], 8)
    go = pl.multiple_of(runs_ref[p + 2], 8)
    v = vmem_buf.at[pl.ds(pl.multiple_of(slot * SEL_ROWS + lo, 8), cnt)]
    h = hbm_buf.at[pl.ds(go, cnt)]
    cp = pltpu.make_async_copy(v, h, sem.at[slot]) if to_hbm else pltpu.make_async_copy(h, v, sem.at[slot])
    return cnt, cp


def _start_runs(runs_ref, win, vmem_buf, slot, hbm_buf, sem, to_hbm):
    def body(i, carry):
        copies = [_run_copy(runs_ref, win, i * RUN_UNROLL + j, vmem_buf, slot, hbm_buf, sem, to_hbm)
                  for j in range(RUN_UNROLL)]
        for j, (cnt, cp) in enumerate(copies):
            @pl.when(cnt > 0)
            def _(cp=cp, j=j):
                cp.start(priority=j % 2)
        return carry
    lax.fori_loop(0, N_EXPERTS // RUN_UNROLL, body, 0)


def _wait_runs(local_ref, win, vmem_buf, slot, hbm_buf, sem, to_hbm):
    total = pl.multiple_of(local_ref[win * (N_EXPERTS + 1) + N_EXPERTS], 8)
    v = vmem_buf.at[pl.ds(pl.multiple_of(slot * SEL_ROWS, 8), total)]
    h = hbm_buf.at[pl.ds(0, total)]
    cp = pltpu.make_async_copy(v, h, sem.at[slot]) if to_hbm else pltpu.make_async_copy(h, v, sem.at[slot])

    @pl.when(total > 0)
    def _():
        cp.wait()


def _dispatch_kernel(local_ref, runs_ref, fill_off_ref, fill_cnt_ref, h2_ref, pos_ref, wsg_ref, wsu_ref,
                     xs_hbm, sbuf, hs_ref, s_ref, sem, zsem, *, n_win):
    w = pl.program_id(0)
    slot = w % 2
    pos = pos_ref[...]

    h2 = h2_ref[...]
    assert SEL_RG <= 256
    rid_b = lax.broadcasted_iota(jnp.int32, (SEL_RG, WIN), 0).astype(F32).astype(BF16)
    one_b = jnp.ones((SEL_RG, WIN), BF16)
    def compact(first, n_rows):
        for r0 in range(first, first + n_rows, SEL_RG):
            acc = jnp.zeros((SEL_RG, WIN), BF16)
            for k in range(TOP_K):
                off = (pos[k:k + 1, :] - r0).astype(F32)
                off = jnp.broadcast_to(off, (SEL_RG, WIN)).astype(BF16)
                acc = jnp.where(rid_b == off, one_b, acc)
            s_ref[r0:r0 + SEL_RG, :] = acc
        dst = pl.multiple_of(slot * SEL_ROWS + first, SEL_TAIL)
        sbuf[pl.ds(dst, n_rows), :] = _dot(s_ref[first:first + n_rows, :], h2).astype(BF16)

    @pl.when(w == 0)
    def _():
        for s in range(2):
            sbuf[(s + 1) * SEL_ROWS - SEL_TAIL:(s + 1) * SEL_ROWS, :] = jnp.zeros((SEL_TAIL, D_MODEL), BF16)

    for first in range(0, SEL_ROWS - 2 * SEL_TAIL, SEL_MM):
        compact(first, SEL_MM)
    hs = _dot(h2, wsg_ref[...])
    hs_ref[...] = (hs * jax.nn.sigmoid(hs) * _dot(h2, wsu_ref[...])).astype(BF16)
    compact(SEL_ROWS - 2 * SEL_TAIL, SEL_TAIL)

    @pl.when(local_ref[w * (N_EXPERTS + 1) + N_EXPERTS] > SEL_ROWS - SEL_TAIL)
    def _():
        compact(SEL_ROWS - SEL_TAIL, SEL_TAIL)

    _start_runs(runs_ref, w, sbuf, slot, xs_hbm, sem, True)

    @pl.when(w > 0)
    def _():
        _wait_runs(local_ref, w - 1, sbuf, 1 - slot, xs_hbm, sem, True)

    @pl.when(w == n_win - 1)
    def _():
        sbuf[2 * SEL_ROWS:, :] = jnp.zeros((2 * EXP_BM, D_MODEL), BF16)

        def fill(e, wait):
            cnt = pl.multiple_of(fill_cnt_ref[e], 8)
            off = pl.multiple_of(fill_off_ref[e], 8)
            cp = pltpu.make_async_copy(sbuf.at[pl.ds(2 * SEL_ROWS, cnt)], xs_hbm.at[pl.ds(off, cnt)], zsem)

            @pl.when(cnt > 0)
            def _():
                if wait:
                    cp.wait()
                else:
                    cp.start()

        def start_body(e, carry):
            fill(e, False)
            return carry

        def wait_body(e, carry):
            fill(e, True)
            return carry
        lax.fori_loop(0, N_EXPERTS, start_body, 0)
        _wait_runs(local_ref, w, sbuf, slot, xs_hbm, sem, True)
        lax.fori_loop(0, N_EXPERTS, wait_body, 0)


def _staging_shape(extra_rows):
    return jax.ShapeDtypeStruct((2 * SEL_ROWS + extra_rows, D_MODEL), BF16)


def _staging_spec(extra_rows):
    return pl.BlockSpec((2 * SEL_ROWS + extra_rows, D_MODEL), lambda w, *_: (0, 0))


def _dispatch(plan, h2, pos, wsg, wsu):
    t = h2.shape[0]
    n_win = t // WIN
    xs, _, hs = pl.pallas_call(
        functools.partial(_dispatch_kernel, n_win=n_win),
        grid_spec=pltpu.PrefetchScalarGridSpec(
            num_scalar_prefetch=4,
            grid=(n_win,),
            in_specs=[
                pl.BlockSpec((WIN, D_MODEL), lambda w, *_: (w, 0)),
                pl.BlockSpec((TOP_K, WIN), lambda w, *_: (0, w)),
                pl.BlockSpec((D_MODEL, D_EXPERT), lambda w, *_: (0, 0)),
                pl.BlockSpec((D_MODEL, D_EXPERT), lambda w, *_: (0, 0)),
            ],
            out_specs=[pl.BlockSpec(memory_space=pl.ANY), _staging_spec(2 * EXP_BM),
                       pl.BlockSpec((WIN, D_EXPERT), lambda w, *_: (w, 0))],
            scratch_shapes=[
                pltpu.VMEM((SEL_ROWS, WIN), BF16),
                pltpu.SemaphoreType.DMA((2,)),
                pltpu.SemaphoreType.DMA,
            ]),
        out_shape=[jax.ShapeDtypeStruct((_sorted_rows_bound(t), D_MODEL), BF16), _staging_shape(2 * EXP_BM),
                   jax.ShapeDtypeStruct((t, D_EXPERT), BF16)],
        compiler_params=pltpu.CompilerParams(
            dimension_semantics=("arbitrary",), vmem_limit_bytes=VMEM_LIMIT),
        name="dispatch",
    )(plan['local_off'], plan['runs'], plan['fill_off'], plan['fill_cnt'], h2, pos, wsg, wsu)
    return xs, hs


def _expert_kernel(blk_expert_ref, next_expert_ref, n_used_ref, xs_hbm, wg_hbm, wu_hbm, wd_hbm, ys_hbm,
                   xbuf, ybuf, wg_st, wu_st, wd_st, wg_bf, wu_bf, wd_bf, xsem, ysem, wsem):
    n_used = n_used_ref[0]
    part = EXP_BM // EXP_SPLIT

    def row_copies(b, slot, fetch):
        out = []
        for q in range(EXP_SPLIT):
            hbm_rows = pl.ds(pl.multiple_of(b * EXP_BM + q * part, part), part)
            if fetch:
                out.append(pltpu.make_async_copy(xs_hbm.at[hbm_rows], xbuf.at[slot, q * part:(q + 1) * part],
                                                 xsem.at[slot]))
            else:
                out.append(pltpu.make_async_copy(ybuf.at[slot, q * part:(q + 1) * part], ys_hbm.at[hbm_rows],
                                                 ysem.at[slot]))
        return out

    def weight_copies(e, slot):
        return [pltpu.make_async_copy(wg_hbm.at[e], wg_st.at[slot], wsem.at[slot]),
                pltpu.make_async_copy(wu_hbm.at[e], wu_st.at[slot], wsem.at[slot]),
                pltpu.make_async_copy(wd_hbm.at[e], wd_st.at[slot], wsem.at[slot])]

    def start(copies):
        for c in copies:
            c.start()

    def wait(copies):
        for c in copies:
            c.wait()

    for ahead in range(EXP_XDEPTH - 2):
        @pl.when(ahead < n_used)
        def _(ahead=ahead):
            start(row_copies(ahead, ahead, True))

    @pl.when(n_used > 0)
    def _():
        start(weight_copies(blk_expert_ref[0], 0))

    def enter_block(b, wset):
        e = blk_expert_ref[b]
        new_expert = jnp.logical_or(b == 0, e != blk_expert_ref[jnp.maximum(b - 1, 0)])
        wset = jnp.where(new_expert, 1 - wset, wset)

        @pl.when(new_expert)
        def _():
            wait(weight_copies(e, wset))
            wg_bf[wset] = wg_st[wset].astype(BF16)
            wu_bf[wset] = wu_st[wset].astype(BF16)
            wd_bf[wset] = wd_st[wset].astype(BF16)
            nxt = next_expert_ref[e]

            @pl.when(nxt < N_EXPERTS)
            def _():
                start(weight_copies(nxt, 1 - wset))

        ahead = b + EXP_XDEPTH - 2

        @pl.when(ahead < n_used)
        def _():
            start(row_copies(ahead, ahead % EXP_XDEPTH, True))

        wait(row_copies(b, b % EXP_XDEPTH, True))

        @pl.when(b >= EXP_YDEPTH)
        def _():
            wait(row_copies(b - EXP_YDEPTH, b % EXP_YDEPTH, False))
        return wset

    def compute(b, wset):
        x = xbuf[b % EXP_XDEPTH]
        hg = _dot(x, wg_bf[wset])
        hb = hg * jax.nn.sigmoid(hg) * _dot(x, wu_bf[wset])
        ybuf[b % EXP_YDEPTH] = _dot(hb.astype(BF16), wd_bf[wset]).astype(BF16)

    def body(p, wset):
        b0 = 2 * p
        w0 = enter_block(b0, wset)
        w1 = enter_block(b0 + 1, w0)
        compute(b0, w0)
        compute(b0 + 1, w1)
        start(row_copies(b0, b0 % EXP_YDEPTH, False))
        start(row_copies(b0 + 1, (b0 + 1) % EXP_YDEPTH, False))
        return w1

    lax.fori_loop(0, n_used // 2, body, jnp.int32(1))

    for back in range(EXP_YDEPTH, 0, -1):
        @pl.when(n_used >= back)
        def _(back=back):
            wait(row_copies(n_used - back, (n_used - back) % EXP_YDEPTH, False))


def _experts(plan, xs, w_gate, w_up, w_down):
    any_spec = pl.BlockSpec(memory_space=pl.ANY)
    return pl.pallas_call(
        _expert_kernel,
        grid_spec=pltpu.PrefetchScalarGridSpec(
            num_scalar_prefetch=3,
            grid=(1,),
            in_specs=[any_spec, any_spec, any_spec, any_spec],
            out_specs=any_spec,
            scratch_shapes=[
                pltpu.VMEM((EXP_XDEPTH, EXP_BM, D_MODEL), BF16),
                pltpu.VMEM((EXP_YDEPTH, EXP_BM, D_MODEL), BF16),
                pltpu.VMEM((2, D_MODEL, D_EXPERT), F32),
                pltpu.VMEM((2, D_MODEL, D_EXPERT), F32),
                pltpu.VMEM((2, D_EXPERT, D_MODEL), F32),
                pltpu.VMEM((2, D_MODEL, D_EXPERT), BF16),
                pltpu.VMEM((2, D_MODEL, D_EXPERT), BF16),
                pltpu.VMEM((2, D_EXPERT, D_MODEL), BF16),
                pltpu.SemaphoreType.DMA((EXP_XDEPTH,)),
                pltpu.SemaphoreType.DMA((EXP_YDEPTH,)),
                pltpu.SemaphoreType.DMA((2,)),
            ]),
        out_shape=jax.ShapeDtypeStruct(xs.shape, BF16),
        compiler_params=pltpu.CompilerParams(
            dimension_semantics=("arbitrary",), vmem_limit_bytes=VMEM_LIMIT),
        name="experts",
    )(plan['blk_expert'], plan['next_expert'], plan['n_used'], xs, w_gate, w_up, w_down)


def _combine_kernel(local_ref, runs_ref, x1_ref, hs_ref, p_ref, rank_ref, gate_ref, lo_ref, hi_ref,
                    wsd_ref, gple_ref, wpg_ref, wp_ref, gfin_ref, ys_hbm, o_ref, ybuf, st_ref, sem,
                    *, n_win, final_norm):
    w = pl.program_id(0)
    slot = w % 2

    @pl.when(w == 0)
    def _():
        ybuf[...] = jnp.zeros(ybuf.shape, BF16)
        _start_runs(runs_ref, w, ybuf, slot, ys_hbm, sem, False)

    @pl.when(w + 1 < n_win)
    def _():
        _start_runs(runs_ref, w + 1, ybuf, 1 - slot, ys_hbm, sem, False)

    lo = lo_ref[0]
    hi = hi_ref[0]
    lo_f = lo.astype(F32)
    rank_tbl = rank_ref[...]
    gate_tbl = gate_ref[...]

    def build_group(lg):
        cols = slice(lg * CMB_LG, (lg + 1) * CMB_LG)
        rid = lg * CMB_LG + lax.broadcasted_iota(jnp.int32, (N_EXPERTS, CMB_LG), 1)
        owner = jnp.where(rid >= lo, jnp.where(rid < hi, 1.0, 0.0), 0.0)
        run_row = rid[0:1, :].astype(F32) - jnp.sum(owner * lo_f, axis=0, keepdims=True)
        owner = owner.astype(BF16)
        hit = _dot_t(rank_tbl, owner) == run_row
        st_ref[:, cols] = jnp.where(hit, _dot_t(gate_tbl, owner), 0.0).astype(BF16)

    build_group(0)
    shared = _dot(hs_ref[...], wsd_ref[...])

    _wait_runs(local_ref, w, ybuf, slot, ys_hbm, sem, False)
    routed = None
    n_groups = SEL_ROWS // CMB_LG
    for lg in range(n_groups):
        if lg + 1 < n_groups:
            build_group(lg + 1)
        src = pl.multiple_of(slot * SEL_ROWS + lg * CMB_LG, CMB_LG)
        part = _dot(st_ref[:, lg * CMB_LG:(lg + 1) * CMB_LG], ybuf[pl.ds(src, CMB_LG), :])
        routed = part if routed is None else routed + part
    x2 = x1_ref[...] + routed + shared

    hp = _rms(x2, gple_ref[...]).astype(BF16)
    gate = jax.nn.sigmoid(_dot(hp, wpg_ref[...]))
    x3 = x2 + gate * _dot(p_ref[...].astype(BF16), wp_ref[...])
    o_ref[...] = _rms(x3, gfin_ref[...]) if final_norm else x3


def _combine(plan, ys, x1, hs, p, rank_tbl, gate_tbl, wsd, g_ple, w_pg, w_p, g_fin, final_norm):
    t = x1.shape[0]
    n_win = t // WIN
    row = lambda width: pl.BlockSpec((WIN, width), lambda w, *_: (w, 0))
    const = lambda shape: pl.BlockSpec(shape, lambda w, *_: (0,) * len(shape))
    return pl.pallas_call(
        functools.partial(_combine_kernel, n_win=n_win, final_norm=final_norm),
        grid_spec=pltpu.PrefetchScalarGridSpec(
            num_scalar_prefetch=2,
            grid=(n_win,),
            in_specs=[
                row(D_MODEL), row(D_EXPERT), row(PLE_DIM),
                pl.BlockSpec((N_EXPERTS, WIN), lambda w, *_: (0, w)),
                pl.BlockSpec((N_EXPERTS, WIN), lambda w, *_: (0, w)),
                pl.BlockSpec((1, N_EXPERTS, 1), lambda w, *_: (w, 0, 0)),
                pl.BlockSpec((1, N_EXPERTS, 1), lambda w, *_: (w, 0, 0)),
                const((D_EXPERT, D_MODEL)),
                const((1, D_MODEL)), const((D_MODEL, D_MODEL)), const((PLE_DIM, D_MODEL)),
                const((1, D_MODEL)),
                pl.BlockSpec(memory_space=pl.ANY),
            ],
            out_specs=[row(D_MODEL), _staging_spec(0)],
            scratch_shapes=[
                pltpu.VMEM((WIN, SEL_ROWS), BF16),
                pltpu.SemaphoreType.DMA((2,)),
            ]),
        out_shape=[jax.ShapeDtypeStruct((t, D_MODEL), F32), _staging_shape(0)],
        compiler_params=pltpu.CompilerParams(
            dimension_semantics=("arbitrary",), vmem_limit_bytes=VMEM_LIMIT),
        name="combine",
    )(plan['local_off'], plan['runs'], x1, hs, p, rank_tbl, gate_tbl, plan['run_lo'], plan['run_hi'],
      wsd, g_ple, w_pg, w_p, g_fin, ys)[0]


def kernel(x, p, g_mix, w_in, b_in, w_dw, b_dw, g_cln, b_cln, w_conv_out, b_conv_out, w_pool, s_pool,
           w_out, g_ffn, w_router, b_router, w_e_gate, w_e_up, w_e_down, w_s_gate, w_s_up, w_s_down,
           g_ple, w_ple_gate, w_ple, g_final):
    bsz, s, d = x.shape
    t = bsz * s
    depth = w_in.shape[0]
    xt = x.reshape(t, d)
    row = lambda v: v.reshape(1, -1)
    for i in range(depth):
        x1, h2 = _mixer(
            xt, s, row(g_mix[i]), w_in[i].astype(BF16), row(b_in[i]), w_dw[i], row(b_dw[i]),
            row(g_cln[i]), row(b_cln[i]), w_conv_out[i].astype(BF16), row(b_conv_out[i]),
            w_pool[i].astype(BF16), row(s_pool[i]), w_out[i].astype(BF16), row(g_ffn[i]))
        gate, rank, pos, cnt = _router(h2, w_router[i].T.astype(BF16), b_router[i].reshape(N_EXPERTS, 1))
        plan = _dispatch_plan(cnt, t)
        xs, hs = _dispatch(plan, h2, pos, w_s_gate[i].astype(BF16), w_s_up[i].astype(BF16))
        ys = _experts(plan, xs, w_e_gate[i], w_e_up[i], w_e_down[i])
        xt = _combine(
            plan, ys, x1, hs, p[i].reshape(t, PLE_DIM), rank, gate, w_s_down[i].astype(BF16),
            row(g_ple[i]), w_ple_gate[i].astype(BF16), w_ple[i].astype(BF16), row(g_final),
            final_norm=(i == depth - 1))
    return xt.reshape(bsz, s, d)
```
